```python
import math
import jax
import jax.numpy as jnp
from jax import lax
import numpy as np

D_MODEL = 1024
BATCH = 16
SEQ = 4096
DEPTH = 4

PLE_DIM = 256
N_MIXERS = 2
HEAD_DIM = 64
N_HEADS = D_MODEL // HEAD_DIM
ATTN_WIDTH = N_HEADS * HEAD_DIM
WIN_DIL = ((128, 1), (512, 4), (2048, 16))
N_GROUPS = len(WIN_DIL)
ROPE_THETA = 10000.0
CONV_WIDTH = 31
CONV_CH = D_MODEL
RMS_EPS = 1e-6
LN_EPS = 1e-5
NEG_INF = -1e30
ATTN_IN_COLS = 3 * N_GROUPS * ATTN_WIDTH + ATTN_WIDTH
CONV_IN_COLS = 3 * CONV_CH
N_ATTN_LAYERS = (DEPTH + 1) // 2
N_CONV_LAYERS = DEPTH // 2

kernel_name = "hybrid_dilated_attn_conformer_trunk"


def _rmsnorm(x, g):
    xf = x.astype(jnp.float32)
    y = xf * lax.rsqrt(jnp.mean(xf * xf, axis=-1, keepdims=True) + RMS_EPS)
    return (y * g.astype(jnp.float32)).astype(x.dtype)


def _layernorm(x, g, b):
    xf = x.astype(jnp.float32)
    mu = jnp.mean(xf, axis=-1, keepdims=True)
    var = jnp.mean(jnp.square(xf - mu), axis=-1, keepdims=True)
    y = (xf - mu) * lax.rsqrt(var + LN_EPS)
    return (y * g.astype(jnp.float32) + b.astype(jnp.float32)).astype(x.dtype)


def _rope_tables(positions):
    inv_freq = 1.0 / (ROPE_THETA ** (jnp.arange(0, HEAD_DIM, 2, dtype=jnp.float32) / HEAD_DIM))
    ang = positions.astype(jnp.float32)[..., None] * inv_freq
    return jnp.cos(ang)[:, :, None, None, :], jnp.sin(ang)[:, :, None, None, :]


def _apply_rope(t, cos, sin):
    tf = t.astype(jnp.float32)
    t1, t2 = jnp.split(tf, 2, axis=-1)
    out = jnp.concatenate([t1 * cos - t2 * sin, t2 * cos + t1 * sin], axis=-1)
    return out.astype(t.dtype)


def _dilated_window_attention(q, k, v, dilation, n_back):
    B, S, H, E = q.shape
    L = S // dilation
    nb = -(-L // n_back)
    Lp = nb * n_back

    def to_blocks(t):
        t = t.reshape(B, L, dilation, H, E).transpose(0, 2, 1, 3, 4)
        t = jnp.pad(t, ((0, 0), (0, 0), (0, Lp - L), (0, 0), (0, 0)))
        return t.reshape(B, dilation, nb, n_back, H, E)

    def with_prev(t):
        prev = jnp.pad(t, ((0, 0), (0, 0), (1, 0), (0, 0), (0, 0), (0, 0)))[:, :, :-1]
        return jnp.concatenate([prev, t], axis=3)

    qb = to_blocks(q)
    kk = with_prev(to_blocks(k))
    vv = with_prev(to_blocks(v))

    s = jnp.einsum('brnqhe,brnkhe->brnhqk', qb, kk).astype(jnp.float32) * (E ** -0.5)
    qi = jnp.arange(n_back)[:, None]
    kj = jnp.arange(2 * n_back)[None, :]
    dist = n_back + qi - kj
    band = (dist >= 0) & (dist <= n_back)
    has_prev = (jnp.arange(nb) > 0)[:, None, None] | (kj >= n_back)[None]
    valid = band[None] & has_prev
    s = jnp.where(valid[None, None, :, None, :, :], s, NEG_INF)
    lse = jax.nn.logsumexp(s, axis=-1)
    prob = jnp.exp(s - lse[..., None])
    o = jnp.einsum('brnhqk,brnkhe->brnqhe', prob.astype(vv.dtype), vv).astype(jnp.float32)

    o = o.reshape(B, dilation, Lp, H, E)[:, :, :L].transpose(0, 2, 1, 3, 4).reshape(B, S, H, E)
    lse = lse.transpose(0, 1, 2, 4, 3).reshape(B, dilation, Lp, H)[:, :, :L]
    lse = lse.transpose(0, 2, 1, 3).reshape(B, S, H)
    return o, lse


def _attention_branch(h, w_in, w_out, cos, sin):
    B, S, _ = h.shape
    proj = h @ w_in.astype(h.dtype)
    qkv = proj[..., :3 * N_GROUPS * ATTN_WIDTH].reshape(B, S, 3, N_GROUPS, N_HEADS, HEAD_DIM)
    z = proj[..., 3 * N_GROUPS * ATTN_WIDTH:]
    q = _apply_rope(qkv[:, :, 0], cos, sin)
    k = _apply_rope(qkv[:, :, 1], cos, sin)
    v = qkv[:, :, 2]
    outs, lses = [], []
    for g, (window, dilation) in enumerate(WIN_DIL):
        o_g, lse_g = _dilated_window_attention(q[:, :, g], k[:, :, g], v[:, :, g],
                                               dilation, window // dilation)
        outs.append(o_g)
        lses.append(lse_g)
    wts = jax.nn.softmax(jnp.stack(lses, axis=0), axis=0)
    o = jnp.sum(wts[..., None] * jnp.stack(outs, axis=0), axis=0)
    o = o.reshape(B, S, ATTN_WIDTH).astype(h.dtype)
    return (o * jax.nn.silu(z)) @ w_out.astype(h.dtype)


def _conv_branch(h, w_in, dw_w, dw_b, ln_g, ln_b, w_out):
    proj = h @ w_in.astype(h.dtype)
    a, b, z = jnp.split(proj, 3, axis=-1)
    u = a * jax.nn.sigmoid(b)
    u = lax.conv_general_dilated(
        u, dw_w.astype(u.dtype)[:, None, :],
        window_strides=(1,), padding=((CONV_WIDTH - 1, 0),),
        dimension_numbers=('NWC', 'WIO', 'NWC'),
        feature_group_count=CONV_CH) + dw_b.astype(u.dtype)
    u = jax.nn.silu(_layernorm(u, ln_g, ln_b))
    return (u * jax.nn.silu(z)) @ w_out.astype(h.dtype)


def _fwd_setup_inputs(seed: int = 0) -> dict:
    key = jax.random.key(seed)
    ks = jax.random.split(key, 16)
    nrm = jax.random.normal
    f32 = jnp.float32
    x = nrm(ks[0], (BATCH, SEQ, D_MODEL), f32)
    p = nrm(ks[1], (DEPTH, BATCH, SEQ, PLE_DIM), f32)
    offset = jax.random.randint(ks[2], (BATCH, 1), 0, SEQ, dtype=jnp.int32)
    positions = jnp.arange(SEQ, dtype=jnp.int32)[None, :] + offset
    pre_norm_g = 1.0 + 0.05 * nrm(ks[3], (DEPTH, D_MODEL), f32)
    post_norm_g = 1.0 + 0.05 * nrm(ks[4], (DEPTH, D_MODEL), f32)
    attn_w_in = nrm(ks[5], (N_ATTN_LAYERS, D_MODEL, ATTN_IN_COLS), f32) * D_MODEL ** -0.5
    attn_w_out = nrm(ks[6], (N_ATTN_LAYERS, ATTN_WIDTH, D_MODEL), f32) * ATTN_WIDTH ** -0.5
    conv_w_in = nrm(ks[7], (N_CONV_LAYERS, D_MODEL, CONV_IN_COLS), f32) * D_MODEL ** -0.5
    conv_dw_w = nrm(ks[8], (N_CONV_LAYERS, CONV_WIDTH, CONV_CH), f32) * CONV_WIDTH ** -0.5
    conv_dw_b = 0.02 * nrm(ks[9], (N_CONV_LAYERS, CONV_CH), f32)
    conv_ln_g = 1.0 + 0.05 * nrm(ks[10], (N_CONV_LAYERS, CONV_CH), f32)
    conv_ln_b = 0.02 * nrm(ks[11], (N_CONV_LAYERS, CONV_CH), f32)
    conv_w_out = nrm(ks[12], (N_CONV_LAYERS, CONV_CH, D_MODEL), f32) * CONV_CH ** -0.5
    ple_w_proj = nrm(ks[13], (DEPTH, PLE_DIM, D_MODEL), f32) * PLE_DIM ** -0.5
    ple_w_gate = nrm(ks[14], (DEPTH, D_MODEL, D_MODEL), f32) * D_MODEL ** -0.5
    return {"x": x, "p": p, "positions": positions,
            "pre_norm_g": pre_norm_g, "post_norm_g": post_norm_g,
            "attn_w_in": attn_w_in, "attn_w_out": attn_w_out,
            "conv_w_in": conv_w_in, "conv_dw_w": conv_dw_w, "conv_dw_b": conv_dw_b,
            "conv_ln_g": conv_ln_g, "conv_ln_b": conv_ln_b, "conv_w_out": conv_w_out,
            "ple_w_proj": ple_w_proj, "ple_w_gate": ple_w_gate}


def _fwd_reference(x, p, positions, pre_norm_g, post_norm_g, attn_w_in, attn_w_out,
              conv_w_in, conv_dw_w, conv_dw_b, conv_ln_g, conv_ln_b, conv_w_out,
              ple_w_proj, ple_w_gate):
    cos, sin = _rope_tables(positions)
    for i in range(DEPTH):
        h = _rmsnorm(x, pre_norm_g[i])
        j = i // N_MIXERS
        if i % N_MIXERS == 0:
            y = _attention_branch(h, attn_w_in[j], attn_w_out[j], cos, sin)
        else:
            y = _conv_branch(h, conv_w_in[j], conv_dw_w[j], conv_dw_b[j],
                             conv_ln_g[j], conv_ln_b[j], conv_w_out[j])
        x = x + _rmsnorm(y, post_norm_g[i])
        x = x + (p[i].astype(x.dtype) @ ple_w_proj[i].astype(x.dtype)) * \
            jax.nn.sigmoid(x @ ple_w_gate[i].astype(x.dtype))
    return x


import jax as _jax
import jax.numpy as _jnp

TWIN_FORMAT = 'train_step'
FWD_PARAMS = ['x', 'p', 'positions', 'pre_norm_g', 'post_norm_g', 'attn_w_in', 'attn_w_out', 'conv_w_in', 'conv_dw_w', 'conv_dw_b', 'conv_ln_g', 'conv_ln_b', 'conv_w_out', 'ple_w_proj', 'ple_w_gate']
TWIN_WEIGHTS = ['pre_norm_g', 'post_norm_g', 'attn_w_in', 'attn_w_out', 'conv_w_in', 'conv_dw_w', 'conv_dw_b', 'conv_ln_g', 'conv_ln_b', 'conv_w_out', 'ple_w_proj', 'ple_w_gate']
TWIN_DIFF_INPUT = 'x'
TWIN_INPUTS = ['x', 'p', 'positions', 'pre_norm_g', 'post_norm_g', 'attn_w_in', 'attn_w_out', 'conv_w_in', 'conv_dw_w', 'conv_dw_b', 'conv_ln_g', 'conv_ln_b', 'conv_w_out', 'ple_w_proj', 'ple_w_gate', 'loss_target', 'm_pre_norm_g', 'm_post_norm_g', 'm_attn_w_in', 'm_attn_w_out', 'm_conv_w_in', 'm_conv_dw_w', 'm_conv_dw_b', 'm_conv_ln_g', 'm_conv_ln_b', 'm_conv_w_out', 'm_ple_w_proj', 'm_ple_w_gate', 'v_pre_norm_g', 'v_post_norm_g', 'v_attn_w_in', 'v_attn_w_out', 'v_conv_w_in', 'v_conv_dw_w', 'v_conv_dw_b', 'v_conv_ln_g', 'v_conv_ln_b', 'v_conv_w_out', 'v_ple_w_proj', 'v_ple_w_gate']
TWIN_OUTPUTS = ['loss', 'grad_x', 'grad_pre_norm_g', 'grad_post_norm_g', 'grad_attn_w_in', 'grad_attn_w_out', 'grad_conv_w_in', 'grad_conv_dw_w', 'grad_conv_dw_b', 'grad_conv_ln_g', 'grad_conv_ln_b', 'grad_conv_w_out', 'grad_ple_w_proj', 'grad_ple_w_gate', 'delta_pre_norm_g', 'delta_post_norm_g', 'delta_attn_w_in', 'delta_attn_w_out', 'delta_conv_w_in', 'delta_conv_dw_w', 'delta_conv_dw_b', 'delta_conv_ln_g', 'delta_conv_ln_b', 'delta_conv_w_out', 'delta_ple_w_proj', 'delta_ple_w_gate', 'new_m_pre_norm_g', 'new_m_post_norm_g', 'new_m_attn_w_in', 'new_m_attn_w_out', 'new_m_conv_w_in', 'new_m_conv_dw_w', 'new_m_conv_dw_b', 'new_m_conv_ln_g', 'new_m_conv_ln_b', 'new_m_conv_w_out', 'new_m_ple_w_proj', 'new_m_ple_w_gate', 'new_v_pre_norm_g', 'new_v_post_norm_g', 'new_v_attn_w_in', 'new_v_attn_w_out', 'new_v_conv_w_in', 'new_v_conv_dw_w', 'new_v_conv_dw_b', 'new_v_conv_ln_g', 'new_v_conv_ln_b', 'new_v_conv_w_out', 'new_v_ple_w_proj', 'new_v_ple_w_gate']
TWIN_LEAF_KINDS = {'loss': 'loss', 'grad_x': 'grad_x', 'grad_pre_norm_g': 'grad_w', 'grad_post_norm_g': 'grad_w', 'grad_attn_w_in': 'grad_w', 'grad_attn_w_out': 'grad_w', 'grad_conv_w_in': 'grad_w', 'grad_conv_dw_w': 'grad_w', 'grad_conv_dw_b': 'grad_w', 'grad_conv_ln_g': 'grad_w', 'grad_conv_ln_b': 'grad_w', 'grad_conv_w_out': 'grad_w', 'grad_ple_w_proj': 'grad_w', 'grad_ple_w_gate': 'grad_w', 'delta_pre_norm_g': 'delta_w', 'delta_post_norm_g': 'delta_w', 'delta_attn_w_in': 'delta_w', 'delta_attn_w_out': 'delta_w', 'delta_conv_w_in': 'delta_w', 'delta_conv_dw_w': 'delta_w', 'delta_conv_dw_b': 'delta_w', 'delta_conv_ln_g': 'delta_w', 'delta_conv_ln_b': 'delta_w', 'delta_conv_w_out': 'delta_w', 'delta_ple_w_proj': 'delta_w', 'delta_ple_w_gate': 'delta_w', 'new_m_pre_norm_g': 'new_m', 'new_m_post_norm_g': 'new_m', 'new_m_attn_w_in': 'new_m', 'new_m_attn_w_out': 'new_m', 'new_m_conv_w_in': 'new_m', 'new_m_conv_dw_w': 'new_m', 'new_m_conv_dw_b': 'new_m', 'new_m_conv_ln_g': 'new_m', 'new_m_conv_ln_b': 'new_m', 'new_m_conv_w_out': 'new_m', 'new_m_ple_w_proj': 'new_m', 'new_m_ple_w_gate': 'new_m', 'new_v_pre_norm_g': 'new_v', 'new_v_post_norm_g': 'new_v', 'new_v_attn_w_in': 'new_v', 'new_v_attn_w_out': 'new_v', 'new_v_conv_w_in': 'new_v', 'new_v_conv_dw_w': 'new_v', 'new_v_conv_dw_b': 'new_v', 'new_v_conv_ln_g': 'new_v', 'new_v_conv_ln_b': 'new_v', 'new_v_conv_w_out': 'new_v', 'new_v_ple_w_proj': 'new_v', 'new_v_ple_w_gate': 'new_v'}


def _forward(args):
    return _fwd_reference(*[args[k] for k in FWD_PARAMS])


def _output_shape():
    out = _jax.eval_shape(lambda: _forward(_fwd_setup_inputs(0)))
    return out.shape, out.dtype

N_MICROBATCH = 1
ADAM_LR = 0.001
ADAM_B1 = 0.9
ADAM_B2 = 0.999
ADAM_EPS = 1e-08
ADAM_WD = 0.01
ADAM_STEP = 10
PER_EXAMPLE_BATCH_AXIS = {'x': 0, 'p': 1, 'positions': 0, 'loss_target': 0}
SHARED_INPUTS = []
_WEIGHT_DTYPES = {'pre_norm_g': _jnp.float32, 'post_norm_g': _jnp.float32, 'attn_w_in': _jnp.float32, 'attn_w_out': _jnp.float32, 'conv_w_in': _jnp.float32, 'conv_dw_w': _jnp.float32, 'conv_dw_b': _jnp.float32, 'conv_ln_g': _jnp.float32, 'conv_ln_b': _jnp.float32, 'conv_w_out': _jnp.float32, 'ple_w_proj': _jnp.float32, 'ple_w_gate': _jnp.float32}
MOMENT_SCALE = {'pre_norm_g': 2.133312e+00, 'post_norm_g': 6.521271e+01, 'attn_w_in': 8.612498e-01, 'attn_w_out': 1.464946e+00, 'conv_w_in': 8.627329e-01, 'conv_dw_w': 1.018254e+00, 'conv_dw_b': 5.703223e+00, 'conv_ln_g': 2.372921e+00, 'conv_ln_b': 3.972638e+00, 'conv_w_out': 1.525369e+00, 'ple_w_proj': 8.940269e-01, 'ple_w_gate': 4.688392e-01}


def _to_microbatches(a, axis):
    t = _jnp.moveaxis(a, axis, 0)
    t = t.reshape((N_MICROBATCH, t.shape[0] // N_MICROBATCH) + t.shape[1:])
    return _jnp.moveaxis(t, 1, axis + 1)


def setup_inputs(seed: int = 0) -> dict:
    inp = _fwd_setup_inputs(seed)
    key = _jax.random.fold_in(_jax.random.key(seed), 7919)
    shape, _ = _output_shape()
    out = dict(inp)
    out["loss_target"] = _jax.random.normal(_jax.random.fold_in(key, 0), shape, _jnp.float32)
    for i, name in enumerate(TWIN_WEIGHTS):
        w = inp[name].astype(_jnp.float32)
        if MOMENT_SCALE is None:
            s = _jnp.sqrt(_jnp.mean(_jnp.square(w)) + 1e-30)
        else:
            s = MOMENT_SCALE[name]
        km, kv = _jax.random.split(_jax.random.fold_in(key, i + 1))
        out[name] = w
        out["m_" + name] = s * _jax.random.normal(km, w.shape, _jnp.float32)
        out["v_" + name] = (s * s) * _jax.random.uniform(kv, w.shape, _jnp.float32, 0.5, 1.5)
    if N_MICROBATCH > 1:
        for name, axis in PER_EXAMPLE_BATCH_AXIS.items():
            out[name] = _to_microbatches(out[name], axis)
    return {'x': out['x'], 'p': out['p'], 'positions': out['positions'], 'pre_norm_g': out['pre_norm_g'], 'post_norm_g': out['post_norm_g'], 'attn_w_in': out['attn_w_in'], 'attn_w_out': out['attn_w_out'], 'conv_w_in': out['conv_w_in'], 'conv_dw_w': out['conv_dw_w'], 'conv_dw_b': out['conv_dw_b'], 'conv_ln_g': out['conv_ln_g'], 'conv_ln_b': out['conv_ln_b'], 'conv_w_out': out['conv_w_out'], 'ple_w_proj': out['ple_w_proj'], 'ple_w_gate': out['ple_w_gate'], 'loss_target': out['loss_target'], 'm_pre_norm_g': out['m_pre_norm_g'], 'm_post_norm_g': out['m_post_norm_g'], 'm_attn_w_in': out['m_attn_w_in'], 'm_attn_w_out': out['m_attn_w_out'], 'm_conv_w_in': out['m_conv_w_in'], 'm_conv_dw_w': out['m_conv_dw_w'], 'm_conv_dw_b': out['m_conv_dw_b'], 'm_conv_ln_g': out['m_conv_ln_g'], 'm_conv_ln_b': out['m_conv_ln_b'], 'm_conv_w_out': out['m_conv_w_out'], 'm_ple_w_proj': out['m_ple_w_proj'], 'm_ple_w_gate': out['m_ple_w_gate'], 'v_pre_norm_g': out['v_pre_norm_g'], 'v_post_norm_g': out['v_post_norm_g'], 'v_attn_w_in': out['v_attn_w_in'], 'v_attn_w_out': out['v_attn_w_out'], 'v_conv_w_in': out['v_conv_w_in'], 'v_conv_dw_w': out['v_conv_dw_w'], 'v_conv_dw_b': out['v_conv_dw_b'], 'v_conv_ln_g': out['v_conv_ln_g'], 'v_conv_ln_b': out['v_conv_ln_b'], 'v_conv_w_out': out['v_conv_w_out'], 'v_ple_w_proj': out['v_ple_w_proj'], 'v_ple_w_gate': out['v_ple_w_gate']}


def _loss(weights, diff, rest, loss_target):
    with _jax.named_scope("forward"):
        args = {**rest, TWIN_DIFF_INPUT: diff, **{k: w.astype(_WEIGHT_DTYPES[k]) for k, w in weights.items()}}
        y = _forward(args)
    with _jax.named_scope("loss_head"):
        err = _jnp.square(y.astype(_jnp.float32) - loss_target)
        return 0.5 * _jnp.sum(_jnp.mean(err, axis=-1)) if err.ndim else 0.5 * err


def _adamw(w, g, m, v):
    m = ADAM_B1 * m + (1.0 - ADAM_B1) * g
    v = ADAM_B2 * v + (1.0 - ADAM_B2) * _jnp.square(g)
    m_hat = m / (1.0 - ADAM_B1 ** ADAM_STEP)
    v_hat = v / (1.0 - ADAM_B2 ** ADAM_STEP)
    delta = -ADAM_LR * (m_hat / (_jnp.sqrt(v_hat) + ADAM_EPS) + ADAM_WD * w)
    return delta, m, v


def reference(x, p, positions, pre_norm_g, post_norm_g, attn_w_in, attn_w_out, conv_w_in, conv_dw_w, conv_dw_b, conv_ln_g, conv_ln_b, conv_w_out, ple_w_proj, ple_w_gate, loss_target, m_pre_norm_g, m_post_norm_g, m_attn_w_in, m_attn_w_out, m_conv_w_in, m_conv_dw_w, m_conv_dw_b, m_conv_ln_g, m_conv_ln_b, m_conv_w_out, m_ple_w_proj, m_ple_w_gate, v_pre_norm_g, v_post_norm_g, v_attn_w_in, v_attn_w_out, v_conv_w_in, v_conv_dw_w, v_conv_dw_b, v_conv_ln_g, v_conv_ln_b, v_conv_w_out, v_ple_w_proj, v_ple_w_gate):
    given = dict(x=x, p=p, positions=positions, pre_norm_g=pre_norm_g, post_norm_g=post_norm_g, attn_w_in=attn_w_in, attn_w_out=attn_w_out, conv_w_in=conv_w_in, conv_dw_w=conv_dw_w, conv_dw_b=conv_dw_b, conv_ln_g=conv_ln_g, conv_ln_b=conv_ln_b, conv_w_out=conv_w_out, ple_w_proj=ple_w_proj, ple_w_gate=ple_w_gate, loss_target=loss_target, m_pre_norm_g=m_pre_norm_g, m_post_norm_g=m_post_norm_g, m_attn_w_in=m_attn_w_in, m_attn_w_out=m_attn_w_out, m_conv_w_in=m_conv_w_in, m_conv_dw_w=m_conv_dw_w, m_conv_dw_b=m_conv_dw_b, m_conv_ln_g=m_conv_ln_g, m_conv_ln_b=m_conv_ln_b, m_conv_w_out=m_conv_w_out, m_ple_w_proj=m_ple_w_proj, m_ple_w_gate=m_ple_w_gate, v_pre_norm_g=v_pre_norm_g, v_post_norm_g=v_post_norm_g, v_attn_w_in=v_attn_w_in, v_attn_w_out=v_attn_w_out, v_conv_w_in=v_conv_w_in, v_conv_dw_w=v_conv_dw_w, v_conv_dw_b=v_conv_dw_b, v_conv_ln_g=v_conv_ln_g, v_conv_ln_b=v_conv_ln_b, v_conv_w_out=v_conv_w_out, v_ple_w_proj=v_ple_w_proj, v_ple_w_gate=v_ple_w_gate)
    weights = {n: given[n] for n in TWIN_WEIGHTS}
    shared = {n: given[n] for n in SHARED_INPUTS}
    per_example = {n: given[n] for n in ['x', 'p', 'positions']}
    grad_fn = _jax.value_and_grad(_loss, argnums=(0, 1))

    def one_microbatch(ex, loss_target):
        ex = dict(ex)
        diff = ex.pop(TWIN_DIFF_INPUT)
        return grad_fn(weights, diff, {**shared, **ex}, loss_target)

    if N_MICROBATCH == 1:
        loss, (grad_w, grad_x) = one_microbatch(per_example, given["loss_target"])
    else:
        def body(carry, xs):
            loss_sum, grad_sum = carry
            l_k, (gw_k, gx_k) = one_microbatch(xs[0], xs[1])
            with _jax.named_scope("update"):
                return (loss_sum + l_k, _jax.tree.map(_jnp.add, grad_sum, gw_k)), gx_k

        init = (_jnp.zeros((), _jnp.float32), _jax.tree.map(_jnp.zeros_like, weights))
        (loss, grad_w), grad_x = _jax.lax.scan(body, init, (per_example, given["loss_target"]))
    with _jax.named_scope("update"):
        delta_w, new_m, new_v = {}, {}, {}
        for n in TWIN_WEIGHTS:
            delta_w[n], new_m[n], new_v[n] = _adamw(weights[n], grad_w[n], given["m_" + n], given["v_" + n])
    return (loss, grad_x, *[grad_w[n] for n in TWIN_WEIGHTS], *[delta_w[n] for n in TWIN_WEIGHTS],
            *[new_m[n] for n in TWIN_WEIGHTS], *[new_v[n] for n in TWIN_WEIGHTS])
```

```python
import functools

import jax
import jax.numpy as jnp
from jax import lax
from jax.experimental import pallas as pl
from jax.experimental.pallas import tpu as pltpu

F32 = jnp.float32
BF16 = jnp.bfloat16

D_MODEL = 1024
DEPTH = 4
PLE_DIM = 256
HEAD_DIM = 64
WIN_DIL = ((128, 1), (512, 4), (2048, 16))
N_GROUPS = 3
N_BACK = 128
ROPE_THETA = 10000.0
CONV_WIDTH = 31
CONV_HALO = 32
RMS_EPS = 1e-6
LN_EPS = 1e-5
NEG_INF = -1e30
ADAM_LR, ADAM_B1, ADAM_B2, ADAM_EPS, ADAM_WD, ADAM_STEP = 0.001, 0.9, 0.999, 1e-08, 0.01, 10

LANES = 128
N_CHIPS = 4
VMEM_LIMIT = 48 * 1024 * 1024
VMEM_LIMIT_ATTN = 56 * 1024 * 1024
FLAT_COLS = 1024
FLAT_ROWS = 9216

MESH = pl.DeviceIdType.MESH


def _params(sem=None, vmem=VMEM_LIMIT):
    return pltpu.CompilerParams(dimension_semantics=sem, vmem_limit_bytes=vmem)


def _sigmoid(v):
    return 1.0 / (1.0 + jnp.exp(-v))


def _mm(a, b, *, ta=False, tb=False, add=None, out_dtype=F32, tm=1024, tn=1024, tk=1024, name="mm"):
    if ta:
        K, M = a.shape
    else:
        M, K = a.shape
    if tb:
        N, K2 = b.shape
    else:
        K2, N = b.shape
    assert K == K2, (a.shape, b.shape)
    tm, tn, tk = min(tm, M), min(tn, N), min(tk, K)
    assert M % tm == 0 and N % tn == 0 and K % tk == 0
    nk = K // tk
    dims = (((0 if ta else 1,), (1 if tb else 0,)), ((), ()))

    def body(*refs):
        if add is None:
            a_ref, b_ref, o_ref, acc_ref = refs
        else:
            a_ref, b_ref, add_ref, o_ref, acc_ref = refs
        k = pl.program_id(2)

        @pl.when(k == 0)
        def _():
            acc_ref[...] = jnp.zeros_like(acc_ref)

        acc_ref[...] += lax.dot_general(a_ref[...].astype(BF16), b_ref[...].astype(BF16), dims,
                                        preferred_element_type=F32)

        @pl.when(k == nk - 1)
        def _():
            r = acc_ref[...]
            if add is not None:
                r = r + add_ref[...].astype(F32)
            o_ref[...] = r.astype(out_dtype)

    a_spec = pl.BlockSpec((tk, tm), lambda i, j, k: (k, i)) if ta else pl.BlockSpec((tm, tk), lambda i, j, k: (i, k))
    b_spec = pl.BlockSpec((tn, tk), lambda i, j, k: (j, k)) if tb else pl.BlockSpec((tk, tn), lambda i, j, k: (k, j))
    o_spec = pl.BlockSpec((tm, tn), lambda i, j, k: (i, j))
    in_specs, args = [a_spec, b_spec], [a, b]
    if add is not None:
        in_specs.append(o_spec)
        args.append(add)
    return pl.pallas_call(
        body, name=name, grid=(M // tm, N // tn, nk),
        in_specs=in_specs, out_specs=o_spec,
        out_shape=jax.ShapeDtypeStruct((M, N), out_dtype),
        scratch_shapes=[pltpu.VMEM((tm, tn), F32)],
        compiler_params=_params(("parallel", "parallel", "arbitrary")),
    )(*args)


ROW_TILE = 512


def _rows(w=D_MODEL, cb=0, tr=ROW_TILE):
    return pl.BlockSpec((tr, w), lambda i: (i, cb))


def _full(shape):
    return pl.BlockSpec(shape, lambda i: (0,) * len(shape))


def _row_call(body, name, T, in_specs, out_specs, out_shape, args, tr=ROW_TILE):
    return pl.pallas_call(body, name=name, grid=(T // tr,), in_specs=in_specs, out_specs=out_specs,
                          out_shape=out_shape, compiler_params=_params(("arbitrary",)))(*args)


def _sds(shape, dtype):
    return jax.ShapeDtypeStruct(shape, dtype)


def _rmsnorm_fwd(x, g):
    T = x.shape[0]

    def body(x_ref, g_ref, h_ref):
        xv = x_ref[...]
        r = lax.rsqrt(jnp.mean(xv * xv, axis=1, keepdims=True) + RMS_EPS)
        h_ref[...] = (xv * r * g_ref[...]).astype(BF16)

    return _row_call(body, "rmsnorm_fwd", T, [_rows(), _full((1, D_MODEL))], _rows(),
                     _sds((T, D_MODEL), BF16), (x, g))


def _post_fwd(x, y, g):
    T = x.shape[0]

    def body(x_ref, y_ref, g_ref, o_ref):
        yv = y_ref[...]
        r = lax.rsqrt(jnp.mean(yv * yv, axis=1, keepdims=True) + RMS_EPS)
        o_ref[...] = x_ref[...] + yv * r * g_ref[...]

    return _row_call(body, "post_fwd", T, [_rows(), _rows(), _full((1, D_MODEL))], _rows(),
                     _sds((T, D_MODEL), F32), (x, y, g))


def _rmsnorm_bwd(dout, xin, g, add, out_dtype, name):
    T = xin.shape[0]

    def body(*refs):
        if add is None:
            d_ref, x_ref, g_ref, dx_ref, dg_ref = refs
        else:
            d_ref, x_ref, g_ref, add_ref, dx_ref, dg_ref = refs

        @pl.when(pl.program_id(0) == 0)
        def _():
            dg_ref[...] = jnp.zeros_like(dg_ref)

        xv = x_ref[...]
        dv = d_ref[...].astype(F32)
        r = lax.rsqrt(jnp.mean(xv * xv, axis=1, keepdims=True) + RMS_EPS)
        xh = xv * r
        dg_ref[...] += jnp.sum(dv * xh, axis=0, keepdims=True)
        dn = dv * g_ref[...]
        dx = r * (dn - xh * jnp.mean(dn * xh, axis=1, keepdims=True))
        if add is not None:
            dx = dx + add_ref[...]
        dx_ref[...] = dx.astype(out_dtype)

    in_specs = [_rows(), _rows(), _full((1, D_MODEL))]
    args = [dout, xin, g]
    if add is not None:
        in_specs.append(_rows())
        args.append(add)
    return _row_call(body, name, T, in_specs, [_rows(), _full((1, D_MODEL))],
                     [_sds((T, D_MODEL), out_dtype), _sds((1, D_MODEL), F32)], args)


def _ple_fwd(x1, pe, gl):
    T = x1.shape[0]

    def body(x_ref, pe_ref, gl_ref, o_ref):
        o_ref[...] = x_ref[...] + pe_ref[...] * _sigmoid(gl_ref[...])

    return _row_call(body, "ple_fwd", T, [_rows()] * 3, _rows(), _sds((T, D_MODEL), F32), (x1, pe, gl))


def _ple_bwd(dx2, pe, gl):
    T = dx2.shape[0]

    def body(d_ref, pe_ref, gl_ref, dpe_ref, dgl_ref):
        dv = d_ref[...]
        sg = _sigmoid(gl_ref[...])
        dpe_ref[...] = (dv * sg).astype(BF16)
        dgl_ref[...] = (dv * pe_ref[...] * sg * (1.0 - sg)).astype(BF16)

    return _row_call(body, "ple_bwd", T, [_rows()] * 3, [_rows()] * 2,
                     [_sds((T, D_MODEL), BF16)] * 2, (dx2, pe, gl))


def _loss_fwd_bwd(y, target):
    T = y.shape[0]

    def body(y_ref, t_ref, s_ref, d_ref):
        @pl.when(pl.program_id(0) == 0)
        def _():
            s_ref[...] = jnp.zeros_like(s_ref)

        e = y_ref[...] - t_ref[...]
        s_ref[...] += jnp.sum(e * e).reshape(1, 1)
        d_ref[...] = e * (1.0 / D_MODEL)

    return _row_call(body, "loss", T, [_rows()] * 2, [_full((1, 1)), _rows()],
                     [_sds((1, 1), F32), _sds((T, D_MODEL), F32)], (y, target))


def _attn_combine(outs, lses, z):
    T = z.shape[0]

    def body(o0, o1, o2, l0, l1, l2, z_ref, a_ref, o_ref, lse_ref):
        a0, a1, a2 = l0[...], l1[...], l2[...]
        m = jnp.maximum(jnp.maximum(a0, a1), a2)
        e0, e1, e2 = jnp.exp(a0 - m), jnp.exp(a1 - m), jnp.exp(a2 - m)
        ssum = e0 + e1 + e2
        o = (e0 * o0[...] + e1 * o1[...] + e2 * o2[...]) / ssum
        zv = z_ref[...].astype(F32)
        o_ref[...] = o
        lse_ref[...] = m + jnp.log(ssum)
        a_ref[...] = (o * zv * _sigmoid(zv)).astype(BF16)

    return _row_call(body, "attn_combine", T, [_rows()] * 7, [_rows()] * 3,
                     [_sds((T, D_MODEL), BF16), _sds((T, D_MODEL), F32), _sds((T, D_MODEL), F32)],
                     (*outs, *lses, z))


def _gate_bwd(da, o, z):
    T = da.shape[0]

    def body(da_ref, o_ref, z_ref, do_ref, dz_ref):
        dv = da_ref[...]
        zv = z_ref[...].astype(F32)
        sg = _sigmoid(zv)
        do_ref[...] = (dv * zv * sg).astype(BF16)
        dz_ref[...] = (dv * o_ref[...] * sg * (1.0 + zv * (1.0 - sg))).astype(BF16)

    return _row_call(body, "gate_bwd", T, [_rows()] * 3, [_rows()] * 2,
                     [_sds((T, D_MODEL), BF16)] * 2, (da, o, z))


def _rope_tables(positions):
    inv_freq = 1.0 / (ROPE_THETA ** (jnp.arange(0, HEAD_DIM, 2, dtype=F32) / HEAD_DIM))
    ang = positions.astype(F32)[..., None] * inv_freq
    cos, sin = jnp.cos(ang), jnp.sin(ang)
    return jnp.tile(cos, (1, 1, 4)), jnp.concatenate([-sin, sin, -sin, sin], axis=-1)


def _rotate_half_partner(t):
    lane = lax.broadcasted_iota(jnp.int32, t.shape, 1)
    return jnp.where((lane % HEAD_DIM) < HEAD_DIM // 2,
                     pltpu.roll(t, LANES - HEAD_DIM // 2, 1), pltpu.roll(t, HEAD_DIM // 2, 1))


def _band_mask(n):
    qi = lax.broadcasted_iota(jnp.int32, (N_BACK, 2 * N_BACK), 0)
    kj = lax.broadcasted_iota(jnp.int32, (N_BACK, 2 * N_BACK), 1)
    has_prev = jnp.full((N_BACK, 2 * N_BACK), n, jnp.int32) > 0
    return (kj >= qi) & (kj <= qi + N_BACK) & ((kj >= N_BACK) | has_prev)


_NT = (((1,), (1,)), ((), ()))
_TN = (((0,), (0,)), ((), ()))


def _attn_specs(L, d, width):
    return pl.BlockSpec((1, L, width), lambda b, r, hp: (b, 0, r * (D_MODEL // LANES) + hp))


def _attn_fwd(qkv, cos, sin, group, Bl, S):
    d = WIN_DIL[group][1]
    L = S // d
    nb = L // N_BACK
    assert WIN_DIL[group][0] // d == N_BACK and L % N_BACK == 0

    def body(qkv_ref, cos_ref, sin_ref, o_ref, lse_ref, qr, kr, vp):
        zeros = jnp.zeros((N_BACK, LANES), BF16)
        kr[pl.ds(0, N_BACK), :] = zeros
        vp[pl.ds(0, N_BACK), :] = zeros

        def prep(i, carry):
            rows = pl.ds(pl.multiple_of(i * N_BACK, N_BACK), N_BACK)
            prow = pl.ds(pl.multiple_of(i * N_BACK + N_BACK, N_BACK), N_BACK)
            cs, sn = cos_ref[0, rows, :], sin_ref[0, rows, :]
            q = qkv_ref[0, rows, 0:LANES].astype(F32)
            k = qkv_ref[0, rows, LANES:2 * LANES].astype(F32)
            qr[rows, :] = ((q * cs + _rotate_half_partner(q) * sn) * (HEAD_DIM ** -0.5)).astype(BF16)
            kr[prow, :] = (k * cs + _rotate_half_partner(k) * sn).astype(BF16)
            vp[prow, :] = qkv_ref[0, rows, 2 * LANES:3 * LANES]
            return carry

        lax.fori_loop(0, nb, prep, 0)
        head0 = lax.broadcasted_iota(jnp.int32, (1, LANES), 1) < HEAD_DIM

        def block(n, carry):
            rows = pl.ds(pl.multiple_of(n * N_BACK, N_BACK), N_BACK)
            win = pl.ds(pl.multiple_of(n * N_BACK, N_BACK), 2 * N_BACK)
            qb, kw, vw = qr[rows, :], kr[win, :], vp[win, :]
            valid = _band_mask(n)
            outs, lses = [], []
            for hm in (head0, jnp.logical_not(head0)):
                qm = jnp.where(hm, qb, jnp.zeros_like(qb))
                s = lax.dot_general(qm, kw, _NT, preferred_element_type=F32)
                s = jnp.where(valid, s, NEG_INF)
                m = jnp.max(s, axis=1, keepdims=True)
                p = jnp.exp(s - m)
                l = jnp.sum(p, axis=1, keepdims=True)
                pv = jnp.dot(p.astype(BF16), vw, preferred_element_type=F32)
                outs.append(pv / l)
                lses.append(m + jnp.log(l))
            o_ref[0, rows, :] = jnp.where(head0, outs[0], outs[1])
            lse_ref[0, rows, :] = jnp.where(head0, lses[0], lses[1])
            return carry

        lax.fori_loop(0, nb, block, 0)

    out_spec = _attn_specs(L, d, LANES)
    tab_spec = pl.BlockSpec((1, L, LANES), lambda b, r, hp: (b, 0, r))
    o, lse = pl.pallas_call(
        body, name="attn_fwd_g%d" % group, grid=(Bl, d, D_MODEL // LANES),
        in_specs=[pl.BlockSpec((1, L, 3 * LANES), lambda b, r, hp: (b, 0, r * (D_MODEL // LANES) + hp)),
                  tab_spec, tab_spec],
        out_specs=[out_spec, out_spec],
        out_shape=[_sds((Bl, L, d * D_MODEL), F32)] * 2,
        scratch_shapes=[pltpu.VMEM((L, LANES), BF16), pltpu.VMEM((L + N_BACK, LANES), BF16),
                        pltpu.VMEM((L + N_BACK, LANES), BF16)],
        compiler_params=_params(("parallel", "parallel", "arbitrary"), VMEM_LIMIT_ATTN),
    )(qkv.reshape(Bl, L, d * 3 * D_MODEL), cos.reshape(Bl, L, d * LANES), sin.reshape(Bl, L, d * LANES))
    return o.reshape(Bl * S, D_MODEL), lse.reshape(Bl * S, D_MODEL)


def _attn_bwd(qkv, cos, sin, do, o, lse, group, Bl, S):
    d = WIN_DIL[group][1]
    L = S // d
    nb = L // N_BACK

    def body(qkv_ref, cos_ref, sin_ref, do_ref, o_ref, lse_ref, dqkv_ref, qr, kr, vp, dq_acc, dk_acc, dv_acc):
        zeros = jnp.zeros((N_BACK, LANES), BF16)
        kr[pl.ds(0, N_BACK), :] = zeros
        vp[pl.ds(0, N_BACK), :] = zeros
        dk_acc[...] = jnp.zeros_like(dk_acc)
        dv_acc[...] = jnp.zeros_like(dv_acc)

        def prep(i, carry):
            rows = pl.ds(pl.multiple_of(i * N_BACK, N_BACK), N_BACK)
            prow = pl.ds(pl.multiple_of(i * N_BACK + N_BACK, N_BACK), N_BACK)
            cs, sn = cos_ref[0, rows, :], sin_ref[0, rows, :]
            q = qkv_ref[0, rows, 0:LANES].astype(F32)
            k = qkv_ref[0, rows, LANES:2 * LANES].astype(F32)
            qr[rows, :] = ((q * cs + _rotate_half_partner(q) * sn) * (HEAD_DIM ** -0.5)).astype(BF16)
            kr[prow, :] = (k * cs + _rotate_half_partner(k) * sn).astype(BF16)
            vp[prow, :] = qkv_ref[0, rows, 2 * LANES:3 * LANES]
            return carry

        lax.fori_loop(0, nb, prep, 0)
        head0 = lax.broadcasted_iota(jnp.int32, (1, LANES), 1) < HEAD_DIM

        def block(n, carry):
            rows = pl.ds(pl.multiple_of(n * N_BACK, N_BACK), N_BACK)
            win = pl.ds(pl.multiple_of(n * N_BACK, N_BACK), 2 * N_BACK)
            qb, kw, vw = qr[rows, :], kr[win, :], vp[win, :]
            dob = do_ref[0, rows, :]
            lse_b = lse_ref[0, rows, :]
            dsum = dob.astype(F32) * o_ref[0, rows, :]
            valid = _band_mask(n)
            dqs = []
            dk = jnp.zeros((2 * N_BACK, LANES), F32)
            dv = jnp.zeros((2 * N_BACK, LANES), F32)
            for h, hm in enumerate((head0, jnp.logical_not(head0))):
                qm = jnp.where(hm, qb, jnp.zeros_like(qb))
                dom = jnp.where(hm, dob, jnp.zeros_like(dob))
                s = lax.dot_general(qm, kw, _NT, preferred_element_type=F32)
                p = jnp.where(valid, jnp.exp(s - lse_b[:, h * HEAD_DIM:h * HEAD_DIM + 1]), 0.0)
                dp = lax.dot_general(dom, vw, _NT, preferred_element_type=F32)
                delta = jnp.sum(jnp.where(hm, dsum, 0.0), axis=1, keepdims=True)
                ds = (p * (dp - delta)).astype(BF16)
                dqs.append(jnp.dot(ds, kw, preferred_element_type=F32))
                dk = dk + lax.dot_general(ds, qm, _TN, preferred_element_type=F32)
                dv = dv + lax.dot_general(p.astype(BF16), dom, _TN, preferred_element_type=F32)
            dq_acc[rows, :] = jnp.where(head0, dqs[0], dqs[1]) * (HEAD_DIM ** -0.5)
            dk_acc[win, :] += dk
            dv_acc[win, :] += dv
            return carry

        lax.fori_loop(0, nb, block, 0)

        def finish(i, carry):
            rows = pl.ds(pl.multiple_of(i * N_BACK, N_BACK), N_BACK)
            prow = pl.ds(pl.multiple_of(i * N_BACK + N_BACK, N_BACK), N_BACK)
            cs, sn = cos_ref[0, rows, :], sin_ref[0, rows, :]
            dq, dk = dq_acc[rows, :], dk_acc[prow, :]
            dqkv_ref[0, rows, 0:LANES] = (dq * cs + _rotate_half_partner(dq * sn)).astype(BF16)
            dqkv_ref[0, rows, LANES:2 * LANES] = (dk * cs + _rotate_half_partner(dk * sn)).astype(BF16)
            dqkv_ref[0, rows, 2 * LANES:3 * LANES] = dv_acc[prow, :].astype(BF16)
            return carry

        lax.fori_loop(0, nb, finish, 0)

    act_spec = _attn_specs(L, d, LANES)
    qkv_spec = pl.BlockSpec((1, L, 3 * LANES), lambda b, r, hp: (b, 0, r * (D_MODEL // LANES) + hp))
    tab_spec = pl.BlockSpec((1, L, LANES), lambda b, r, hp: (b, 0, r))
    view = lambda t: t.reshape(Bl, L, d * D_MODEL)
    dqkv = pl.pallas_call(
        body, name="attn_bwd_g%d" % group, grid=(Bl, d, D_MODEL // LANES),
        in_specs=[qkv_spec, tab_spec, tab_spec, act_spec, act_spec, act_spec],
        out_specs=qkv_spec,
        out_shape=_sds((Bl, L, d * 3 * D_MODEL), BF16),
        scratch_shapes=[pltpu.VMEM((L, LANES), BF16), pltpu.VMEM((L + N_BACK, LANES), BF16),
                        pltpu.VMEM((L + N_BACK, LANES), BF16), pltpu.VMEM((L, LANES), F32),
                        pltpu.VMEM((L + N_BACK, LANES), F32), pltpu.VMEM((L + N_BACK, LANES), F32)],
        compiler_params=_params(("parallel", "parallel", "arbitrary"), VMEM_LIMIT_ATTN),
    )(qkv.reshape(Bl, L, d * 3 * D_MODEL), cos.reshape(Bl, L, d * LANES), sin.reshape(Bl, L, d * LANES),
      view(do), view(o), view(lse))
    return dqkv.reshape(Bl * S, 3 * D_MODEL)


CONV_TILE = 256
CONV_CHUNK = 64


def _conv_fwd(proj, z, dw, dwb, ln_g, ln_b, Bl, S):
    tr = CONV_TILE
    nj = S // tr
    hb = tr // CONV_HALO

    def body(a_ref, b_ref, ah_ref, bh_ref, z_ref, dw_ref, dwb_ref, g_ref, bb_ref, u1_ref, out_ref, ext):
        j = pl.program_id(1)
        halo = ah_ref[0].astype(F32) * _sigmoid(bh_ref[0].astype(F32))
        ext[pl.ds(0, CONV_HALO), :] = jnp.where(j > 0, halo, 0.0)
        ext[pl.ds(CONV_HALO, tr), :] = a_ref[0].astype(F32) * _sigmoid(b_ref[0].astype(F32))

        def cols(c, carry):
            cs = pl.ds(pl.multiple_of(c * LANES, LANES), LANES)
            for rc in range(tr // CONV_CHUNK):
                acc = jnp.zeros((CONV_CHUNK, LANES), F32)
                for w in range(CONV_WIDTH):
                    off = rc * CONV_CHUNK + CONV_HALO - (CONV_WIDTH - 1) + w
                    acc = acc + dw_ref[pl.ds(w, 1), cs] * ext[pl.ds(off, CONV_CHUNK), cs]
                u1_ref[0, pl.ds(rc * CONV_CHUNK, CONV_CHUNK), cs] = acc + dwb_ref[:, cs]
            return carry

        lax.fori_loop(0, D_MODEL // LANES, cols, 0)
        u1 = u1_ref[0]
        mu = jnp.mean(u1, axis=1, keepdims=True)
        xc = u1 - mu
        rstd = lax.rsqrt(jnp.mean(xc * xc, axis=1, keepdims=True) + LN_EPS)
        u2 = xc * rstd * g_ref[...] + bb_ref[...]
        zv = z_ref[0].astype(F32)
        out_ref[0] = (u2 * _sigmoid(u2) * zv * _sigmoid(zv)).astype(BF16)

    tile = lambda cb: pl.BlockSpec((1, tr, D_MODEL), lambda b, j: (b, j, cb))
    halo = lambda cb: pl.BlockSpec((1, CONV_HALO, D_MODEL), lambda b, j: (b, jnp.maximum(j * hb - 1, 0), cb))
    par = lambda r: pl.BlockSpec((r, D_MODEL), lambda b, j: (0, 0))
    p3 = proj.reshape(Bl, S, 2 * D_MODEL)
    u1, out = pl.pallas_call(
        body, name="conv_fwd", grid=(Bl, nj),
        in_specs=[tile(0), tile(1), halo(0), halo(1), tile(0), par(32), par(1), par(1), par(1)],
        out_specs=[tile(0), tile(0)],
        out_shape=[_sds((Bl, S, D_MODEL), F32), _sds((Bl, S, D_MODEL), BF16)],
        scratch_shapes=[pltpu.VMEM((tr + CONV_HALO, D_MODEL), F32)],
        compiler_params=_params(("parallel", "arbitrary")),
    )(p3, p3, p3, p3, z.reshape(Bl, S, D_MODEL), dw, dwb, ln_g, ln_b)
    return u1.reshape(Bl * S, D_MODEL), out.reshape(Bl * S, D_MODEL)


def _conv_norm_bwd(da2, z, u1, ln_g, ln_b):
    T = da2.shape[0]

    def body(da_ref, z_ref, u_ref, g_ref, b_ref, du_ref, dz_ref, dg_ref, db_ref):
        @pl.when(pl.program_id(0) == 0)
        def _():
            dg_ref[...] = jnp.zeros_like(dg_ref)
            db_ref[...] = jnp.zeros_like(db_ref)

        u1 = u_ref[...]
        mu = jnp.mean(u1, axis=1, keepdims=True)
        xc = u1 - mu
        rstd = lax.rsqrt(jnp.mean(xc * xc, axis=1, keepdims=True) + LN_EPS)
        nrm = xc * rstd
        u2 = nrm * g_ref[...] + b_ref[...]
        s2 = _sigmoid(u2)
        zv = z_ref[...].astype(F32)
        sz = _sigmoid(zv)
        dv = da_ref[...]
        dz_ref[...] = (dv * u2 * s2 * sz * (1.0 + zv * (1.0 - sz))).astype(BF16)
        du2 = dv * zv * sz * s2 * (1.0 + u2 * (1.0 - s2))
        dg_ref[...] += jnp.sum(du2 * nrm, axis=0, keepdims=True)
        db_ref[...] += jnp.sum(du2, axis=0, keepdims=True)
        dn = du2 * g_ref[...]
        du_ref[...] = rstd * (dn - jnp.mean(dn, axis=1, keepdims=True)
                              - nrm * jnp.mean(dn * nrm, axis=1, keepdims=True))

    return _row_call(body, "conv_norm_bwd", T,
                     [_rows(), _rows(), _rows(), _full((1, D_MODEL)), _full((1, D_MODEL))],
                     [_rows(), _rows(), _full((1, D_MODEL)), _full((1, D_MODEL))],
                     [_sds((T, D_MODEL), F32), _sds((T, D_MODEL), BF16), _sds((1, D_MODEL), F32),
                      _sds((1, D_MODEL), F32)], (da2, z, u1, ln_g, ln_b))


def _conv_bwd(proj, du1, dw, Bl, S):
    tr = CONV_TILE
    nj = S // tr
    hb = tr // CONV_HALO

    def body(a_ref, b_ref, ah_ref, bh_ref, du_ref, duh_ref, dw_ref, dab_ref, ddw_ref, ddb_ref, uext, dext, du0):
        first = (pl.program_id(0) == 0) & (pl.program_id(1) == 0)
        j = pl.program_id(1)

        @pl.when(first)
        def _():
            ddw_ref[...] = jnp.zeros_like(ddw_ref)
            ddb_ref[...] = jnp.zeros_like(ddb_ref)

        halo = ah_ref[0].astype(F32) * _sigmoid(bh_ref[0].astype(F32))
        uext[pl.ds(0, CONV_HALO), :] = jnp.where(j > 0, halo, 0.0)
        av = a_ref[0].astype(F32)
        sb = _sigmoid(b_ref[0].astype(F32))
        uext[pl.ds(CONV_HALO, tr), :] = av * sb
        dext[pl.ds(0, tr), :] = du_ref[0]
        dext[pl.ds(tr, CONV_HALO), :] = jnp.where(j < nj - 1, duh_ref[0], 0.0)
        ddb_ref[...] += jnp.sum(du_ref[0], axis=0, keepdims=True)

        def cols(c, carry):
            cs = pl.ds(pl.multiple_of(c * LANES, LANES), LANES)
            for rc in range(tr // CONV_CHUNK):
                base = rc * CONV_CHUNK
                dchunk = dext[pl.ds(base, CONV_CHUNK), cs]
                acc = jnp.zeros((CONV_CHUNK, LANES), F32)
                for w in range(CONV_WIDTH):
                    acc = acc + dw_ref[pl.ds(w, 1), cs] * dext[pl.ds(base + CONV_WIDTH - 1 - w, CONV_CHUNK), cs]
                    off = base + CONV_HALO - (CONV_WIDTH - 1) + w
                    ddw_ref[pl.ds(w, 1), cs] += jnp.sum(dchunk * uext[pl.ds(off, CONV_CHUNK), cs],
                                                        axis=0, keepdims=True)
                du0[pl.ds(base, CONV_CHUNK), cs] = acc
            return carry

        lax.fori_loop(0, D_MODEL // LANES, cols, 0)
        g = du0[...]
        dab_ref[0, :, 0:D_MODEL] = (g * sb).astype(BF16)
        dab_ref[0, :, D_MODEL:2 * D_MODEL] = (g * av * sb * (1.0 - sb)).astype(BF16)

    tile = lambda cb: pl.BlockSpec((1, tr, D_MODEL), lambda b, j: (b, j, cb))
    halo = lambda cb: pl.BlockSpec((1, CONV_HALO, D_MODEL), lambda b, j: (b, jnp.maximum(j * hb - 1, 0), cb))
    nxt = pl.BlockSpec((1, CONV_HALO, D_MODEL), lambda b, j: (b, jnp.minimum((j + 1) * hb, S // CONV_HALO - 1), 0))
    par = lambda r: pl.BlockSpec((r, D_MODEL), lambda b, j: (0, 0))
    p3 = proj.reshape(Bl, S, 2 * D_MODEL)
    d3 = du1.reshape(Bl, S, D_MODEL)
    dab, ddw, ddb = pl.pallas_call(
        body, name="conv_bwd", grid=(Bl, nj),
        in_specs=[tile(0), tile(1), halo(0), halo(1), tile(0), nxt, par(32)],
        out_specs=[pl.BlockSpec((1, tr, 2 * D_MODEL), lambda b, j: (b, j, 0)), par(32), par(1)],
        out_shape=[_sds((Bl, S, 2 * D_MODEL), BF16), _sds((32, D_MODEL), F32), _sds((1, D_MODEL), F32)],
        scratch_shapes=[pltpu.VMEM((tr + CONV_HALO, D_MODEL), F32), pltpu.VMEM((tr + CONV_HALO, D_MODEL), F32),
                        pltpu.VMEM((tr, D_MODEL), F32)],
        compiler_params=_params(("arbitrary", "arbitrary")),
    )(p3, p3, p3, p3, d3, d3, dw)
    return dab.reshape(Bl * S, 2 * D_MODEL), ddw, ddb


_LAYOUT = (
    ("pre_norm_g", (4, 1024), None), ("post_norm_g", (4, 1024), None),
    ("attn_w_in", (2, 1024, 2560), 2), ("attn_w_out", (2, 256, 1024), 1),
    ("conv_w_in", (2, 1024, 768), 2), ("conv_dw_w", (2, 31, 256), 2),
    ("conv_dw_b", (2, 256), 1), ("conv_ln_g", (2, 256), 1), ("conv_ln_b", (2, 256), 1),
    ("conv_w_out", (2, 256, 1024), 1), ("ple_w_proj", (4, 256, 256), 2), ("ple_w_gate", (4, 256, 1024), 1),
)
_MATMUL_WEIGHTS = ("attn_w_in", "attn_w_out", "conv_w_in", "conv_w_out", "ple_w_proj", "ple_w_gate")
_SMALL_WEIGHTS = ("conv_dw_w", "conv_dw_b", "conv_ln_g", "conv_ln_b")
_SHAPE = {n: s for n, s, _ in _LAYOUT}
_AXIS = {n: a for n, _, a in _LAYOUT}


def _size(shape):
    n = 1
    for s in shape:
        n *= s
    return n


def _pack_rows(pieces, dtype):
    flat = jnp.concatenate([p.reshape(-1).astype(dtype) for p in pieces])
    pad = FLAT_ROWS * FLAT_COLS - flat.shape[0]
    assert pad >= 0
    return jnp.concatenate([flat, jnp.zeros((pad,), dtype)]).reshape(FLAT_ROWS, FLAT_COLS)


def _pack_f32(params):
    return _pack_rows([params[n] for n, _, _ in _LAYOUT], F32)


def _unpack_f32(flat):
    flat = flat.reshape(-1)
    out, off = {}, 0
    for n, shape, _ in _LAYOUT:
        out[n] = flat[off:off + _size(shape)].reshape(shape)
        off += _size(shape)
    return out


def _pack_gather_payload(w):
    pieces = [w[n].astype(BF16) for n in _MATMUL_WEIGHTS]
    pieces += [lax.bitcast_convert_type(w[n], BF16) for n in _SMALL_WEIGHTS]
    return _pack_rows(pieces, BF16)


def _unpack_gathered(g):
    g = g.reshape(N_CHIPS, -1)
    out, off = {}, 0
    for n in _MATMUL_WEIGHTS:
        size = _size(_SHAPE[n])
        parts = [g[s, off:off + size].reshape(_SHAPE[n]) for s in range(N_CHIPS)]
        out[n] = jnp.concatenate(parts, axis=_AXIS[n])
        off += size
    for n in _SMALL_WEIGHTS:
        size = 2 * _size(_SHAPE[n])
        parts = [lax.bitcast_convert_type(g[s, off:off + size].reshape(_SHAPE[n] + (2,)), F32)
                 for s in range(N_CHIPS)]
        out[n] = jnp.concatenate(parts, axis=_AXIS[n])
        off += size
    return out


def _pack_full_grads(grads):
    rows = []
    for s in range(N_CHIPS):
        pieces = []
        for n, shape, axis in _LAYOUT:
            gfull = grads[n]
            if axis is not None:
                gfull = lax.slice_in_dim(gfull, s * shape[axis], (s + 1) * shape[axis], axis=axis)
            pieces.append(gfull)
        rows.append(_pack_rows(pieces, F32))
    return jnp.stack(rows)


_ANY = pl.BlockSpec(memory_space=pl.ANY)


def _mesh_pos():
    return lax.axis_index("x"), lax.axis_index("y"), lax.axis_index("c")


def _other_chips(x, y):
    return [(1 - x, y), (x, 1 - y), (1 - x, 1 - y)]


def _allgather_weights(wl):
    R, C = wl.shape
    H = R // 2

    def body(w_ref, out_ref, send_sems, recv_sems, local_sem):
        x, y, c = _mesh_pos()
        me, sibling = (x, y, c), (x, y, 1 - c)
        chips = _other_chips(x, y)

        def half(px, py, pc):
            return out_ref.at[2 * px + py, pl.ds(pc * H, H), :]

        def copy(k, block, to, src=None):
            return pltpu.make_async_remote_copy(
                src_ref=half(*block) if src is None else src, dst_ref=half(*block),
                send_sem=send_sems.at[k], recv_sem=recv_sems.at[k], device_id=to, device_id_type=MESH)

        mine = pltpu.make_async_copy(w_ref, out_ref.at[2 * x + y], local_sem)
        mine.start()
        first = [copy(j, me, (*chip, c), src=w_ref.at[pl.ds(c * H, H), :]) for j, chip in enumerate(chips)]
        for cp in first:
            cp.start()
        passed = [copy(3 + j, (*chip, c), sibling) for j, chip in enumerate(chips)]
        for j, chip in enumerate(chips):
            copy(j, (*chip, c), me).wait_recv()
            passed[j].start()
        for j, chip in enumerate(chips):
            copy(3 + j, (*chip, 1 - c), me).wait_recv()
        for cp in first + passed:
            cp.wait_send()
        mine.wait()

    return pl.pallas_call(
        body, name="allgather_weights", in_specs=[_ANY], out_specs=_ANY,
        out_shape=_sds((N_CHIPS, R, C), wl.dtype),
        scratch_shapes=[pltpu.SemaphoreType.DMA((6,)), pltpu.SemaphoreType.DMA((6,)), pltpu.SemaphoreType.DMA],
    )(wl)


def _exchange_core_halves(g):
    n, _, H, C = g.shape

    def body(g_ref, own_ref, got_ref, send_sem, recv_sem, local_sem):
        x, y, c = _mesh_pos()
        keep = pltpu.make_async_copy(g_ref.at[pl.ds(0, n), c], own_ref, local_sem)
        keep.start()
        swap = pltpu.make_async_remote_copy(
            src_ref=g_ref.at[pl.ds(0, n), 1 - c], dst_ref=got_ref, send_sem=send_sem, recv_sem=recv_sem,
            device_id=(x, y, 1 - c), device_id_type=MESH)
        swap.start()
        swap.wait()
        keep.wait()

    return pl.pallas_call(
        body, name="exchange_core_halves", in_specs=[_ANY], out_specs=[_ANY, _ANY],
        out_shape=[_sds((n, H, C), g.dtype)] * 2,
        scratch_shapes=[pltpu.SemaphoreType.DMA, pltpu.SemaphoreType.DMA, pltpu.SemaphoreType.DMA],
    )(g)


def _scatter_to_chips(p):
    n, H, C = p.shape

    def body(p_ref, q_ref, send_sems, recv_sems, local_sem):
        x, y, c = _mesh_pos()
        mine = 2 * x + y
        chips = _other_chips(x, y)
        keep = pltpu.make_async_copy(p_ref.at[mine], q_ref.at[mine], local_sem)
        keep.start()
        sends = [pltpu.make_async_remote_copy(
            src_ref=p_ref.at[2 * cx + cy], dst_ref=q_ref.at[mine], send_sem=send_sems.at[j],
            recv_sem=recv_sems.at[j], device_id=(cx, cy, c), device_id_type=MESH)
            for j, (cx, cy) in enumerate(chips)]
        for cp in sends:
            cp.start()
        for j, (cx, cy) in enumerate(chips):
            pltpu.make_async_remote_copy(
                src_ref=p_ref.at[mine], dst_ref=q_ref.at[2 * cx + cy], send_sem=send_sems.at[j],
                recv_sem=recv_sems.at[j], device_id=(cx, cy, c), device_id_type=MESH).wait_recv()
        for cp in sends:
            cp.wait_send()
        keep.wait()

    return pl.pallas_call(
        body, name="scatter_to_chips", in_specs=[_ANY], out_specs=_ANY,
        out_shape=_sds((n, H, C), p.dtype),
        scratch_shapes=[pltpu.SemaphoreType.DMA((3,)), pltpu.SemaphoreType.DMA((3,)), pltpu.SemaphoreType.DMA],
    )(p)


def _allgather_core_halves(rh):
    H, C = rh.shape

    def body(r_ref, out_ref, send_sem, recv_sem, local_sem):
        x, y, c = _mesh_pos()
        keep = pltpu.make_async_copy(r_ref, out_ref.at[c], local_sem)
        keep.start()
        send = pltpu.make_async_remote_copy(
            src_ref=r_ref, dst_ref=out_ref.at[c], send_sem=send_sem, recv_sem=recv_sem,
            device_id=(x, y, 1 - c), device_id_type=MESH)
        send.start()
        send.wait_send()
        pltpu.make_async_remote_copy(
            src_ref=r_ref, dst_ref=out_ref.at[1 - c], send_sem=send_sem, recv_sem=recv_sem,
            device_id=(x, y, 1 - c), device_id_type=MESH).wait_recv()
        keep.wait()

    return pl.pallas_call(
        body, name="allgather_core_halves", in_specs=[_ANY], out_specs=_ANY,
        out_shape=_sds((2, H, C), rh.dtype),
        scratch_shapes=[pltpu.SemaphoreType.DMA, pltpu.SemaphoreType.DMA, pltpu.SemaphoreType.DMA],
    )(rh)


def _sum_pair(a, b):
    n, H, C = a.shape

    def body(a_ref, b_ref, o_ref):
        o_ref[...] = a_ref[...] + b_ref[...]

    spec = pl.BlockSpec((1, ROW_TILE, C), lambda s, i: (s, i, 0))
    return pl.pallas_call(body, name="sum_core_pair", grid=(n, H // ROW_TILE), in_specs=[spec, spec],
                          out_specs=spec, out_shape=_sds((n, H, C), a.dtype),
                          compiler_params=_params(("parallel", "parallel")))(a, b)


def _sum_chips(q):
    n, H, C = q.shape

    def body(q0, q1, q2, q3, o_ref):
        o_ref[...] = ((q0[0] + q1[0]) + q2[0]) + q3[0]

    specs = [pl.BlockSpec((1, ROW_TILE, C), functools.partial(lambda i, s: (s, i, 0), s=s)) for s in range(n)]
    return pl.pallas_call(body, name="sum_chips", grid=(H // ROW_TILE,), in_specs=specs,
                          out_specs=pl.BlockSpec((ROW_TILE, C), lambda i: (i, 0)),
                          out_shape=_sds((H, C), q.dtype), compiler_params=_params(("parallel",)))(q, q, q, q)


def _adamw(w, g, m, v):
    R, C = w.shape

    def body(w_ref, g_ref, m_ref, v_ref, d_ref, nm_ref, nv_ref):
        gv = g_ref[...]
        nm = ADAM_B1 * m_ref[...] + (1.0 - ADAM_B1) * gv
        nv = ADAM_B2 * v_ref[...] + (1.0 - ADAM_B2) * (gv * gv)
        m_hat = nm / (1.0 - ADAM_B1 ** ADAM_STEP)
        v_hat = nv / (1.0 - ADAM_B2 ** ADAM_STEP)
        d_ref[...] = -ADAM_LR * (m_hat / (jnp.sqrt(v_hat) + ADAM_EPS) + ADAM_WD * w_ref[...])
        nm_ref[...] = nm
        nv_ref[...] = nv

    spec = pl.BlockSpec((ROW_TILE, C), lambda i: (i, 0))
    return pl.pallas_call(body, name="adamw", grid=(R // ROW_TILE,), in_specs=[spec] * 4, out_specs=[spec] * 3,
                          out_shape=[_sds((R, C), F32)] * 3, compiler_params=_params(("parallel",)))(w, g, m, v)


def _reduce_scatter_grads(gfull):
    n, R, C = gfull.shape
    own, got = _exchange_core_halves(gfull.reshape(n, 2, R // 2, C))
    q = _scatter_to_chips(_sum_pair(own, got))
    return _allgather_core_halves(_sum_chips(q)).reshape(R, C)


def _split_attn_w_in(w):
    qkv = w[:, :3 * N_GROUPS * D_MODEL].reshape(D_MODEL, 3, N_GROUPS, D_MODEL // LANES, LANES)
    groups = [qkv[:, :, g].transpose(0, 2, 1, 3).reshape(D_MODEL, 3 * D_MODEL) for g in range(N_GROUPS)]
    return groups, w[:, 3 * N_GROUPS * D_MODEL:]


def _merge_attn_w_in(groups, z):
    parts = [g.reshape(D_MODEL, D_MODEL // LANES, 3, LANES).transpose(0, 2, 1, 3) for g in groups]
    qkv = jnp.stack(parts, axis=2).reshape(D_MODEL, 3 * N_GROUPS * D_MODEL)
    return jnp.concatenate([qkv, z], axis=1)


def _local_step(x, p, positions, loss_target, pre_g, post_g, w):
    Bl, S, _ = x.shape
    T = Bl * S
    cos, sin = _rope_tables(positions)
    xs = x.reshape(T, D_MODEL)
    saved = []
    for i in range(DEPTH):
        j = i // 2
        g_pre, g_post = pre_g[i:i + 1], post_g[i:i + 1]
        h = _rmsnorm_fwd(xs, g_pre)
        st = {"x": xs, "h": h}
        if i % 2 == 0:
            wg, wz = _split_attn_w_in(w["attn_w_in"][j])
            qkvs = [_mm(h, wg[g], out_dtype=BF16, name="attn_in_g") for g in range(N_GROUPS)]
            z = _mm(h, wz, out_dtype=BF16, name="attn_in_z")
            res = [_attn_fwd(qkvs[g], cos, sin, g, Bl, S) for g in range(N_GROUPS)]
            a, o, lse = _attn_combine([r[0] for r in res], [r[1] for r in res], z)
            y = _mm(a, w["attn_w_out"][j], name="attn_out")
            st.update(wg=wg, wz=wz, qkvs=qkvs, z=z, a=a, o=o, lse=lse)
        else:
            w_ab, w_z = w["conv_w_in"][j][:, :2 * D_MODEL], w["conv_w_in"][j][:, 2 * D_MODEL:]
            ab = _mm(h, w_ab, out_dtype=BF16, name="conv_in_ab")
            z = _mm(h, w_z, out_dtype=BF16, name="conv_in_z")
            dw = jnp.pad(w["conv_dw_w"][j], ((0, 1), (0, 0)))
            u1, a = _conv_fwd(ab, z, dw, w["conv_dw_b"][j:j + 1], w["conv_ln_g"][j:j + 1],
                              w["conv_ln_b"][j:j + 1], Bl, S)
            y = _mm(a, w["conv_w_out"][j], name="conv_out")
            st.update(w_ab=w_ab, w_z=w_z, ab=ab, z=z, dw=dw, u1=u1, a=a)
        x1 = _post_fwd(xs, y, g_post)
        pi = p[i].reshape(T, PLE_DIM)
        pe = _mm(pi, w["ple_w_proj"][i], name="ple_proj")
        gl = _mm(x1, w["ple_w_gate"][i], name="ple_gate")
        xs = _ple_fwd(x1, pe, gl)
        st.update(y=y, x1=x1, pi=pi, pe=pe, gl=gl)
        saved.append(st)

    sq, dx = _loss_fwd_bwd(xs, loss_target.reshape(T, D_MODEL))

    grads = {n: [None] * shape[0] for n, shape, _ in _LAYOUT}
    for i in reversed(range(DEPTH)):
        j = i // 2
        st = saved[i]
        g_pre, g_post = pre_g[i:i + 1], post_g[i:i + 1]
        dpe, dgl = _ple_bwd(dx, st["pe"], st["gl"])
        grads["ple_w_proj"][i] = _mm(st["pi"], dpe, ta=True, name="ple_proj_wgrad")
        grads["ple_w_gate"][i] = _mm(st["x1"], dgl, ta=True, tk=512, name="ple_gate_wgrad")
        dx1 = _mm(dgl, w["ple_w_gate"][i], tb=True, add=dx, name="ple_gate_dgrad")
        dy, dg_post = _rmsnorm_bwd(dx1, st["y"], g_post, None, BF16, "post_bwd")
        grads["post_norm_g"][i] = dg_post[0]
        if i % 2 == 0:
            grads["attn_w_out"][j] = _mm(st["a"], dy, ta=True, tk=512, name="attn_out_wgrad")
            da = _mm(dy, w["attn_w_out"][j], tb=True, name="attn_out_dgrad")
            do, dz = _gate_bwd(da, st["o"], st["z"])
            dh = _mm(dz, st["wz"], tb=True, name="attn_in_z_dgrad")
            dwz = _mm(st["h"], dz, ta=True, tk=512, name="attn_in_z_wgrad")
            dwg = []
            for g in range(N_GROUPS):
                dqkv = _attn_bwd(st["qkvs"][g], cos, sin, do, st["o"], st["lse"], g, Bl, S)
                dh = _mm(dqkv, st["wg"][g], tb=True, add=dh, name="attn_in_g_dgrad")
                dwg.append(_mm(st["h"], dqkv, ta=True, tk=512, name="attn_in_g_wgrad"))
            grads["attn_w_in"][j] = _merge_attn_w_in(dwg, dwz)
        else:
            grads["conv_w_out"][j] = _mm(st["a"], dy, ta=True, tk=512, name="conv_out_wgrad")
            da2 = _mm(dy, w["conv_w_out"][j], tb=True, name="conv_out_dgrad")
            du1, dz, dln_g, dln_b = _conv_norm_bwd(da2, st["z"], st["u1"], w["conv_ln_g"][j:j + 1],
                                                   w["conv_ln_b"][j:j + 1])
            dab, ddw, ddb = _conv_bwd(st["ab"], du1, st["dw"], Bl, S)
            dh = _mm(dz, st["w_z"], tb=True, name="conv_in_z_dgrad")
            dh = _mm(dab, st["w_ab"], tb=True, add=dh, name="conv_in_ab_dgrad")
            dw_ab = _mm(st["h"], dab, ta=True, tk=512, name="conv_in_ab_wgrad")
            dw_z = _mm(st["h"], dz, ta=True, tk=512, name="conv_in_z_wgrad")
            grads["conv_w_in"][j] = jnp.concatenate([dw_ab, dw_z], axis=1)
            grads["conv_dw_w"][j] = ddw[:CONV_WIDTH]
            grads["conv_dw_b"][j] = ddb[0]
            grads["conv_ln_g"][j] = dln_g[0]
            grads["conv_ln_b"][j] = dln_b[0]
        dx, dg_pre = _rmsnorm_bwd(dh, st["x"], g_pre, dx1, F32, "pre_bwd")
        grads["pre_norm_g"][i] = dg_pre[0]
    grads = {n: jnp.stack(v) for n, v in grads.items()}
    return sq, dx.reshape(Bl, S, D_MODEL), grads


_NAMES = tuple(n for n, _, _ in _LAYOUT)


def kernel(x, p, positions, pre_norm_g, post_norm_g, attn_w_in, attn_w_out, conv_w_in, conv_dw_w, conv_dw_b, conv_ln_g, conv_ln_b, conv_w_out, ple_w_proj, ple_w_gate, loss_target, m_pre_norm_g, m_post_norm_g, m_attn_w_in, m_attn_w_out, m_conv_w_in, m_conv_dw_w, m_conv_dw_b, m_conv_ln_g, m_conv_ln_b, m_conv_w_out, m_ple_w_proj, m_ple_w_gate, v_pre_norm_g, v_post_norm_g, v_attn_w_in, v_attn_w_out, v_conv_w_in, v_conv_dw_w, v_conv_dw_b, v_conv_ln_g, v_conv_ln_b, v_conv_w_out, v_ple_w_proj, v_ple_w_gate):
    w_loc = dict(zip(_NAMES, (pre_norm_g, post_norm_g, attn_w_in, attn_w_out, conv_w_in, conv_dw_w, conv_dw_b,
                              conv_ln_g, conv_ln_b, conv_w_out, ple_w_proj, ple_w_gate)))
    m_loc = dict(zip(_NAMES, (m_pre_norm_g, m_post_norm_g, m_attn_w_in, m_attn_w_out, m_conv_w_in, m_conv_dw_w,
                              m_conv_dw_b, m_conv_ln_g, m_conv_ln_b, m_conv_w_out, m_ple_w_proj, m_ple_w_gate)))
    v_loc = dict(zip(_NAMES, (v_pre_norm_g, v_post_norm_g, v_attn_w_in, v_attn_w_out, v_conv_w_in, v_conv_dw_w,
                              v_conv_dw_b, v_conv_ln_g, v_conv_ln_b, v_conv_w_out, v_ple_w_proj, v_ple_w_gate)))

    w_full = _unpack_gathered(_allgather_weights(_pack_gather_payload(w_loc)))
    sq, grad_x, grads = _local_step(x, p, positions, loss_target, pre_norm_g, post_norm_g, w_full)
    loss = lax.psum(sq[0, 0] * (0.5 / D_MODEL), ("x", "y", "c"))

    g_flat = _reduce_scatter_grads(_pack_full_grads(grads))
    delta, new_m, new_v = _adamw(_pack_f32(w_loc), g_flat, _pack_f32(m_loc), _pack_f32(v_loc))
    g_out, d_out, m_out, v_out = (_unpack_f32(t) for t in (g_flat, delta, new_m, new_v))
    return (loss, grad_x, *[g_out[n] for n in _NAMES], *[d_out[n] for n in _NAMES],
            *[m_out[n] for n in _NAMES], *[v_out[n] for n in _NAMES])
```

```python
import functools

import jax
import jax.numpy as jnp
from jax import lax
from jax.experimental import pallas as pl
from jax.experimental.pallas import tpu as pltpu

F32 = jnp.float32
BF16 = jnp.bfloat16

D_MODEL = 1024
DEPTH = 4
PLE_DIM = 256
HEAD_DIM = 64
WIN_DIL = ((128, 1), (512, 4), (2048, 16))
N_GROUPS = 3
N_BACK = 128
ROPE_THETA = 10000.0
CONV_WIDTH = 31
CONV_HALO = 32
RMS_EPS = 1e-6
LN_EPS = 1e-5
NEG_INF = -1e30
ADAM_LR, ADAM_B1, ADAM_B2, ADAM_EPS, ADAM_WD, ADAM_STEP = 0.001, 0.9, 0.999, 1e-08, 0.01, 10

LANES = 128
N_CHIPS = 4
VMEM_LIMIT = 48 * 1024 * 1024
VMEM_LIMIT_ATTN = 56 * 1024 * 1024
FLAT_COLS = 1024
FLAT_ROWS = 9216

MESH = pl.DeviceIdType.MESH


def _params(sem=None, vmem=VMEM_LIMIT):
    return pltpu.CompilerParams(dimension_semantics=sem, vmem_limit_bytes=vmem)


def _sigmoid(v):
    return 1.0 / (1.0 + jnp.exp(-v))


def _mm(a, b, *, ta=False, tb=False, add=None, out_dtype=F32, tm=1024, tn=1024, tk=1024, name="mm"):
    if ta:
        K, M = a.shape
    else:
        M, K = a.shape
    if tb:
        N, K2 = b.shape
    else:
        K2, N = b.shape
    assert K == K2, (a.shape, b.shape)
    tm, tn, tk = min(tm, M), min(tn, N), min(tk, K)
    assert M % tm == 0 and N % tn == 0 and K % tk == 0
    nk = K // tk
    dims = (((0 if ta else 1,), (1 if tb else 0,)), ((), ()))

    def body(*refs):
        if add is None:
            a_ref, b_ref, o_ref, acc_ref = refs
        else:
            a_ref, b_ref, add_ref, o_ref, acc_ref = refs
        k = pl.program_id(2)

        @pl.when(k == 0)
        def _():
            acc_ref[...] = jnp.zeros_like(acc_ref)

        acc_ref[...] += lax.dot_general(a_ref[...].astype(BF16), b_ref[...].astype(BF16), dims,
                                        preferred_element_type=F32)

        @pl.when(k == nk - 1)
        def _():
            r = acc_ref[...]
            if add is not None:
                r = r + add_ref[...].astype(F32)
            o_ref[...] = r.astype(out_dtype)

    a_spec = pl.BlockSpec((tk, tm), lambda i, j, k: (k, i)) if ta else pl.BlockSpec((tm, tk), lambda i, j, k: (i, k))
    b_spec = pl.BlockSpec((tn, tk), lambda i, j, k: (j, k)) if tb else pl.BlockSpec((tk, tn), lambda i, j, k: (k, j))
    o_spec = pl.BlockSpec((tm, tn), lambda i, j, k: (i, j))
    in_specs, args = [a_spec, b_spec], [a, b]
    if add is not None:
        in_specs.append(o_spec)
        args.append(add)
    return pl.pallas_call(
        body, name=name, grid=(M // tm, N // tn, nk),
        in_specs=in_specs, out_specs=o_spec,
        out_shape=jax.ShapeDtypeStruct((M, N), out_dtype),
        scratch_shapes=[pltpu.VMEM((tm, tn), F32)],
        compiler_params=_params(("parallel", "parallel", "arbitrary")),
    )(*args)


ROW_TILE = 512


def _rows(w=D_MODEL, cb=0, tr=ROW_TILE):
    return pl.BlockSpec((tr, w), lambda i: (i, cb))


def _full(shape):
    return pl.BlockSpec(shape, lambda i: (0,) * len(shape))


def _row_call(body, name, T, in_specs, out_specs, out_shape, args, tr=ROW_TILE):
    return pl.pallas_call(body, name=name, grid=(T // tr,), in_specs=in_specs, out_specs=out_specs,
                          out_shape=out_shape, compiler_params=_params(("arbitrary",)))(*args)


def _sds(shape, dtype):
    return jax.ShapeDtypeStruct(shape, dtype)


def _rmsnorm_fwd(x, g):
    T = x.shape[0]

    def body(x_ref, g_ref, h_ref):
        xv = x_ref[...]
        r = lax.rsqrt(jnp.mean(xv * xv, axis=1, keepdims=True) + RMS_EPS)
        h_ref[...] = (xv * r * g_ref[...]).astype(BF16)

    return _row_call(body, "rmsnorm_fwd", T, [_rows(), _full((1, D_MODEL))], _rows(),
                     _sds((T, D_MODEL), BF16), (x, g))


def _post_fwd(x, y, g):
    T = x.shape[0]

    def body(x_ref, y_ref, g_ref, o_ref):
        yv = y_ref[...]
        r = lax.rsqrt(jnp.mean(yv * yv, axis=1, keepdims=True) + RMS_EPS)
        o_ref[...] = x_ref[...] + yv * r * g_ref[...]

    return _row_call(body, "post_fwd", T, [_rows(), _rows(), _full((1, D_MODEL))], _rows(),
                     _sds((T, D_MODEL), F32), (x, y, g))


def _rmsnorm_bwd(dout, xin, g, add, out_dtype, name):
    T = xin.shape[0]

    def body(*refs):
        if add is None:
            d_ref, x_ref, g_ref, dx_ref, dg_ref = refs
        else:
            d_ref, x_ref, g_ref, add_ref, dx_ref, dg_ref = refs

        @pl.when(pl.program_id(0) == 0)
        def _():
            dg_ref[...] = jnp.zeros_like(dg_ref)

        xv = x_ref[...]
        dv = d_ref[...].astype(F32)
        r = lax.rsqrt(jnp.mean(xv * xv, axis=1, keepdims=True) + RMS_EPS)
        xh = xv * r
        dg_ref[...] += jnp.sum(dv * xh, axis=0, keepdims=True)
        dn = dv * g_ref[...]
        dx = r * (dn - xh * jnp.mean(dn * xh, axis=1, keepdims=True))
        if add is not None:
            dx = dx + add_ref[...]
        dx_ref[...] = dx.astype(out_dtype)

    in_specs = [_rows(), _rows(), _full((1, D_MODEL))]
    args = [dout, xin, g]
    if add is not None:
        in_specs.append(_rows())
        args.append(add)
    return _row_call(body, name, T, in_specs, [_rows(), _full((1, D_MODEL))],
                     [_sds((T, D_MODEL), out_dtype), _sds((1, D_MODEL), F32)], args)


def _ple_fwd(x1, pe, gl):
    T = x1.shape[0]

    def body(x_ref, pe_ref, gl_ref, o_ref):
        o_ref[...] = x_ref[...] + pe_ref[...] * _sigmoid(gl_ref[...])

    return _row_call(body, "ple_fwd", T, [_rows()] * 3, _rows(), _sds((T, D_MODEL), F32), (x1, pe, gl))


def _ple_bwd(dx2, pe, gl):
    T = dx2.shape[0]

    def body(d_ref, pe_ref, gl_ref, dpe_ref, dgl_ref):
        dv = d_ref[...]
        sg = _sigmoid(gl_ref[...])
        dpe_ref[...] = (dv * sg).astype(BF16)
        dgl_ref[...] = (dv * pe_ref[...] * sg * (1.0 - sg)).astype(BF16)

    return _row_call(body, "ple_bwd", T, [_rows()] * 3, [_rows()] * 2,
                     [_sds((T, D_MODEL), BF16)] * 2, (dx2, pe, gl))


def _loss_fwd_bwd(y, target):
    T = y.shape[0]

    def body(y_ref, t_ref, s_ref, d_ref):
        @pl.when(pl.program_id(0) == 0)
        def _():
            s_ref[...] = jnp.zeros_like(s_ref)

        e = y_ref[...] - t_ref[...]
        s_ref[...] += jnp.sum(e * e).reshape(1, 1)
        d_ref[...] = e * (1.0 / D_MODEL)

    return _row_call(body, "loss", T, [_rows()] * 2, [_full((1, 1)), _rows()],
                     [_sds((1, 1), F32), _sds((T, D_MODEL), F32)], (y, target))


def _attn_combine(outs, lses, z):
    T = z.shape[0]

    def body(o0, o1, o2, l0, l1, l2, z_ref, a_ref, o_ref, lse_ref):
        a0, a1, a2 = l0[...], l1[...], l2[...]
        m = jnp.maximum(jnp.maximum(a0, a1), a2)
        e0, e1, e2 = jnp.exp(a0 - m), jnp.exp(a1 - m), jnp.exp(a2 - m)
        ssum = e0 + e1 + e2
        o = (e0 * o0[...] + e1 * o1[...] + e2 * o2[...]) / ssum
        zv = z_ref[...].astype(F32)
        o_ref[...] = o
        lse_ref[...] = m + jnp.log(ssum)
        a_ref[...] = (o * zv * _sigmoid(zv)).astype(BF16)

    return _row_call(body, "attn_combine", T, [_rows()] * 7, [_rows()] * 3,
                     [_sds((T, D_MODEL), BF16), _sds((T, D_MODEL), F32), _sds((T, D_MODEL), F32)],
                     (*outs, *lses, z))


def _gate_bwd(da, o, z):
    T = da.shape[0]

    def body(da_ref, o_ref, z_ref, do_ref, dz_ref):
        dv = da_ref[...]
        zv = z_ref[...].astype(F32)
        sg = _sigmoid(zv)
        do_ref[...] = (dv * zv * sg).astype(BF16)
        dz_ref[...] = (dv * o_ref[...] * sg * (1.0 + zv * (1.0 - sg))).astype(BF16)

    return _row_call(body, "gate_bwd", T, [_rows()] * 3, [_rows()] * 2,
                     [_sds((T, D_MODEL), BF16)] * 2, (da, o, z))


def _rope_tables(positions):
    inv_freq = 1.0 / (ROPE_THETA ** (jnp.arange(0, HEAD_DIM, 2, dtype=F32) / HEAD_DIM))
    ang = positions.astype(F32)[..., None] * inv_freq
    cos, sin = jnp.cos(ang), jnp.sin(ang)
    return jnp.tile(cos, (1, 1, 4)), jnp.concatenate([-sin, sin, -sin, sin], axis=-1)


def _rotate_half_partner(t):
    lane = lax.broadcasted_iota(jnp.int32, t.shape, 1)
    return jnp.where((lane % HEAD_DIM) < HEAD_DIM // 2,
                     pltpu.roll(t, LANES - HEAD_DIM // 2, 1), pltpu.roll(t, HEAD_DIM // 2, 1))


def _mask_bias(first):
    qi = lax.broadcasted_iota(jnp.int32, (N_BACK, 2 * N_BACK), 0)
    kj = lax.broadcasted_iota(jnp.int32, (N_BACK, 2 * N_BACK), 1)
    ok = (kj >= qi) & (kj <= qi + N_BACK)
    if first:
        ok = ok & (kj >= N_BACK)
    return jnp.where(ok, 0.0, NEG_INF).astype(F32)


def _block_loop(nb, block):
    block(0, _mask_bias(True))
    if nb > 1:
        bias = _mask_bias(False)

        def step(n, carry):
            block(n, bias)
            return carry

        lax.fori_loop(1, nb, step, 0, unroll=2)


_NT = (((1,), (1,)), ((), ()))
_TN = (((0,), (0,)), ((), ()))


def _attn_specs(L, d, width):
    return pl.BlockSpec((1, L, width), lambda b, r, hp: (b, 0, r * (D_MODEL // LANES) + hp))


def _attn_fwd(qkv, cos, sin, group, Bl, S):
    d = WIN_DIL[group][1]
    L = S // d
    nb = L // N_BACK
    assert WIN_DIL[group][0] // d == N_BACK and L % N_BACK == 0

    def body(qkv_ref, cos_ref, sin_ref, o_ref, lse_ref, qr, kr, vp):
        zeros = jnp.zeros((N_BACK, LANES), BF16)
        kr[pl.ds(0, N_BACK), :] = zeros
        vp[pl.ds(0, N_BACK), :] = zeros

        def prep(i, carry):
            rows = pl.ds(pl.multiple_of(i * N_BACK, N_BACK), N_BACK)
            prow = pl.ds(pl.multiple_of(i * N_BACK + N_BACK, N_BACK), N_BACK)
            cs, sn = cos_ref[0, rows, :], sin_ref[0, rows, :]
            q = qkv_ref[0, rows, 0:LANES].astype(F32)
            k = qkv_ref[0, rows, LANES:2 * LANES].astype(F32)
            qr[rows, :] = ((q * cs + _rotate_half_partner(q) * sn) * (HEAD_DIM ** -0.5)).astype(BF16)
            kr[prow, :] = (k * cs + _rotate_half_partner(k) * sn).astype(BF16)
            vp[prow, :] = qkv_ref[0, rows, 2 * LANES:3 * LANES]
            return carry

        lax.fori_loop(0, nb, prep, 0)
        head0 = lax.broadcasted_iota(jnp.int32, (1, LANES), 1) < HEAD_DIM

        def block(n, bias):
            rows = pl.ds(pl.multiple_of(n * N_BACK, N_BACK), N_BACK)
            win = pl.ds(pl.multiple_of(n * N_BACK, N_BACK), 2 * N_BACK)
            qb, kw, vw = qr[rows, :], kr[win, :], vp[win, :]
            outs, lses = [], []
            for hm in (head0, jnp.logical_not(head0)):
                qm = jnp.where(hm, qb, jnp.zeros_like(qb))
                s = lax.dot_general(qm, kw, _NT, preferred_element_type=F32) + bias
                m = jnp.max(s, axis=1, keepdims=True)
                p = jnp.exp(s - m)
                l = jnp.sum(p, axis=1, keepdims=True)
                pv = jnp.dot(p.astype(BF16), vw, preferred_element_type=F32)
                outs.append(pv * (1.0 / l))
                lses.append(m + jnp.log(l))
            o_ref[0, rows, :] = jnp.where(head0, outs[0], outs[1])
            lse_ref[0, rows, :] = jnp.where(head0, lses[0], lses[1])

        _block_loop(nb, block)

    out_spec = _attn_specs(L, d, LANES)
    tab_spec = pl.BlockSpec((1, L, LANES), lambda b, r, hp: (b, 0, r))
    o, lse = pl.pallas_call(
        body, name="attn_fwd_g%d" % group, grid=(Bl, d, D_MODEL // LANES),
        in_specs=[pl.BlockSpec((1, L, 3 * LANES), lambda b, r, hp: (b, 0, r * (D_MODEL // LANES) + hp)),
                  tab_spec, tab_spec],
        out_specs=[out_spec, out_spec],
        out_shape=[_sds((Bl, L, d * D_MODEL), F32)] * 2,
        scratch_shapes=[pltpu.VMEM((L, LANES), BF16), pltpu.VMEM((L + N_BACK, LANES), BF16),
                        pltpu.VMEM((L + N_BACK, LANES), BF16)],
        compiler_params=_params(("parallel", "parallel", "arbitrary"), VMEM_LIMIT_ATTN),
    )(qkv.reshape(Bl, L, d * 3 * D_MODEL), cos.reshape(Bl, L, d * LANES), sin.reshape(Bl, L, d * LANES))
    return o.reshape(Bl * S, D_MODEL), lse.reshape(Bl * S, D_MODEL)


def _attn_bwd(qkv, cos, sin, do, o, lse, group, Bl, S):
    d = WIN_DIL[group][1]
    L = S // d
    nb = L // N_BACK

    def body(qkv_ref, cos_ref, sin_ref, do_ref, o_ref, lse_ref, dqkv_ref, qr, kr, vp, dq_acc, dk_acc, dv_acc):
        zeros = jnp.zeros((N_BACK, LANES), BF16)
        kr[pl.ds(0, N_BACK), :] = zeros
        vp[pl.ds(0, N_BACK), :] = zeros
        dk_acc[...] = jnp.zeros_like(dk_acc)
        dv_acc[...] = jnp.zeros_like(dv_acc)

        def prep(i, carry):
            rows = pl.ds(pl.multiple_of(i * N_BACK, N_BACK), N_BACK)
            prow = pl.ds(pl.multiple_of(i * N_BACK + N_BACK, N_BACK), N_BACK)
            cs, sn = cos_ref[0, rows, :], sin_ref[0, rows, :]
            q = qkv_ref[0, rows, 0:LANES].astype(F32)
            k = qkv_ref[0, rows, LANES:2 * LANES].astype(F32)
            qr[rows, :] = ((q * cs + _rotate_half_partner(q) * sn) * (HEAD_DIM ** -0.5)).astype(BF16)
            kr[prow, :] = (k * cs + _rotate_half_partner(k) * sn).astype(BF16)
            vp[prow, :] = qkv_ref[0, rows, 2 * LANES:3 * LANES]
            return carry

        lax.fori_loop(0, nb, prep, 0)
        head0 = lax.broadcasted_iota(jnp.int32, (1, LANES), 1) < HEAD_DIM

        def block(n, bias):
            rows = pl.ds(pl.multiple_of(n * N_BACK, N_BACK), N_BACK)
            win = pl.ds(pl.multiple_of(n * N_BACK, N_BACK), 2 * N_BACK)
            qb, kw, vw = qr[rows, :], kr[win, :], vp[win, :]
            dob = do_ref[0, rows, :]
            lse_b = lse_ref[0, rows, :]
            dsum = dob.astype(F32) * o_ref[0, rows, :]
            dqs = []
            dk = jnp.zeros((2 * N_BACK, LANES), F32)
            dv = jnp.zeros((2 * N_BACK, LANES), F32)
            for h, hm in enumerate((head0, jnp.logical_not(head0))):
                qm = jnp.where(hm, qb, jnp.zeros_like(qb))
                dom = jnp.where(hm, dob, jnp.zeros_like(dob))
                s = lax.dot_general(qm, kw, _NT, preferred_element_type=F32) + bias
                p = jnp.exp(s - lse_b[:, h * HEAD_DIM:h * HEAD_DIM + 1])
                dp = lax.dot_general(dom, vw, _NT, preferred_element_type=F32)
                delta = jnp.sum(jnp.where(hm, dsum, 0.0), axis=1, keepdims=True)
                ds = (p * (dp - delta)).astype(BF16)
                dqs.append(jnp.dot(ds, kw, preferred_element_type=F32))
                dk = dk + lax.dot_general(ds, qm, _TN, preferred_element_type=F32)
                dv = dv + lax.dot_general(p.astype(BF16), dom, _TN, preferred_element_type=F32)
            dq_acc[rows, :] = jnp.where(head0, dqs[0], dqs[1]) * (HEAD_DIM ** -0.5)
            dk_acc[win, :] += dk
            dv_acc[win, :] += dv

        _block_loop(nb, block)

        def finish(i, carry):
            rows = pl.ds(pl.multiple_of(i * N_BACK, N_BACK), N_BACK)
            prow = pl.ds(pl.multiple_of(i * N_BACK + N_BACK, N_BACK), N_BACK)
            cs, sn = cos_ref[0, rows, :], sin_ref[0, rows, :]
            dq, dk = dq_acc[rows, :], dk_acc[prow, :]
            dqkv_ref[0, rows, 0:LANES] = (dq * cs + _rotate_half_partner(dq * sn)).astype(BF16)
            dqkv_ref[0, rows, LANES:2 * LANES] = (dk * cs + _rotate_half_partner(dk * sn)).astype(BF16)
            dqkv_ref[0, rows, 2 * LANES:3 * LANES] = dv_acc[prow, :].astype(BF16)
            return carry

        lax.fori_loop(0, nb, finish, 0)

    act_spec = _attn_specs(L, d, LANES)
    qkv_spec = pl.BlockSpec((1, L, 3 * LANES), lambda b, r, hp: (b, 0, r * (D_MODEL // LANES) + hp))
    tab_spec = pl.BlockSpec((1, L, LANES), lambda b, r, hp: (b, 0, r))
    view = lambda t: t.reshape(Bl, L, d * D_MODEL)
    dqkv = pl.pallas_call(
        body, name="attn_bwd_g%d" % group, grid=(Bl, d, D_MODEL // LANES),
        in_specs=[qkv_spec, tab_spec, tab_spec, act_spec, act_spec, act_spec],
        out_specs=qkv_spec,
        out_shape=_sds((Bl, L, d * 3 * D_MODEL), BF16),
        scratch_shapes=[pltpu.VMEM((L, LANES), BF16), pltpu.VMEM((L + N_BACK, LANES), BF16),
                        pltpu.VMEM((L + N_BACK, LANES), BF16), pltpu.VMEM((L, LANES), F32),
                        pltpu.VMEM((L + N_BACK, LANES), F32), pltpu.VMEM((L + N_BACK, LANES), F32)],
        compiler_params=_params(("parallel", "parallel", "arbitrary"), VMEM_LIMIT_ATTN),
    )(qkv.reshape(Bl, L, d * 3 * D_MODEL), cos.reshape(Bl, L, d * LANES), sin.reshape(Bl, L, d * LANES),
      view(do), view(o), view(lse))
    return dqkv.reshape(Bl * S, 3 * D_MODEL)


CONV_TILE = 256
CONV_CHUNK = 64


def _conv_fwd(proj, z, dw, dwb, ln_g, ln_b, Bl, S):
    tr = CONV_TILE
    nj = S // tr
    hb = tr // CONV_HALO

    def body(a_ref, b_ref, ah_ref, bh_ref, z_ref, dw_ref, dwb_ref, g_ref, bb_ref, u1_ref, out_ref, ext):
        j = pl.program_id(1)
        halo = ah_ref[0].astype(F32) * _sigmoid(bh_ref[0].astype(F32))
        ext[pl.ds(0, CONV_HALO), :] = jnp.where(j > 0, halo, 0.0)
        ext[pl.ds(CONV_HALO, tr), :] = a_ref[0].astype(F32) * _sigmoid(b_ref[0].astype(F32))

        def cols(c, carry):
            cs = pl.ds(pl.multiple_of(c * LANES, LANES), LANES)
            for rc in range(tr // CONV_CHUNK):
                acc = jnp.zeros((CONV_CHUNK, LANES), F32)
                for w in range(CONV_WIDTH):
                    off = rc * CONV_CHUNK + CONV_HALO - (CONV_WIDTH - 1) + w
                    acc = acc + dw_ref[pl.ds(w, 1), cs] * ext[pl.ds(off, CONV_CHUNK), cs]
                u1_ref[0, pl.ds(rc * CONV_CHUNK, CONV_CHUNK), cs] = acc + dwb_ref[:, cs]
            return carry

        lax.fori_loop(0, D_MODEL // LANES, cols, 0)
        u1 = u1_ref[0]
        mu = jnp.mean(u1, axis=1, keepdims=True)
        xc = u1 - mu
        rstd = lax.rsqrt(jnp.mean(xc * xc, axis=1, keepdims=True) + LN_EPS)
        u2 = xc * rstd * g_ref[...] + bb_ref[...]
        zv = z_ref[0].astype(F32)
        out_ref[0] = (u2 * _sigmoid(u2) * zv * _sigmoid(zv)).astype(BF16)

    tile = lambda cb: pl.BlockSpec((1, tr, D_MODEL), lambda b, j: (b, j, cb))
    halo = lambda cb: pl.BlockSpec((1, CONV_HALO, D_MODEL), lambda b, j: (b, jnp.maximum(j * hb - 1, 0), cb))
    par = lambda r: pl.BlockSpec((r, D_MODEL), lambda b, j: (0, 0))
    p3 = proj.reshape(Bl, S, 2 * D_MODEL)
    u1, out = pl.pallas_call(
        body, name="conv_fwd", grid=(Bl, nj),
        in_specs=[tile(0), tile(1), halo(0), halo(1), tile(0), par(32), par(1), par(1), par(1)],
        out_specs=[tile(0), tile(0)],
        out_shape=[_sds((Bl, S, D_MODEL), F32), _sds((Bl, S, D_MODEL), BF16)],
        scratch_shapes=[pltpu.VMEM((tr + CONV_HALO, D_MODEL), F32)],
        compiler_params=_params(("parallel", "arbitrary")),
    )(p3, p3, p3, p3, z.reshape(Bl, S, D_MODEL), dw, dwb, ln_g, ln_b)
    return u1.reshape(Bl * S, D_MODEL), out.reshape(Bl * S, D_MODEL)


def _conv_norm_bwd(da2, z, u1, ln_g, ln_b):
    T = da2.shape[0]

    def body(da_ref, z_ref, u_ref, g_ref, b_ref, du_ref, dz_ref, dg_ref, db_ref):
        @pl.when(pl.program_id(0) == 0)
        def _():
            dg_ref[...] = jnp.zeros_like(dg_ref)
            db_ref[...] = jnp.zeros_like(db_ref)

        u1 = u_ref[...]
        mu = jnp.mean(u1, axis=1, keepdims=True)
        xc = u1 - mu
        rstd = lax.rsqrt(jnp.mean(xc * xc, axis=1, keepdims=True) + LN_EPS)
        nrm = xc * rstd
        u2 = nrm * g_ref[...] + b_ref[...]
        s2 = _sigmoid(u2)
        zv = z_ref[...].astype(F32)
        sz = _sigmoid(zv)
        dv = da_ref[...]
        dz_ref[...] = (dv * u2 * s2 * sz * (1.0 + zv * (1.0 - sz))).astype(BF16)
        du2 = dv * zv * sz * s2 * (1.0 + u2 * (1.0 - s2))
        dg_ref[...] += jnp.sum(du2 * nrm, axis=0, keepdims=True)
        db_ref[...] += jnp.sum(du2, axis=0, keepdims=True)
        dn = du2 * g_ref[...]
        du_ref[...] = rstd * (dn - jnp.mean(dn, axis=1, keepdims=True)
                              - nrm * jnp.mean(dn * nrm, axis=1, keepdims=True))

    return _row_call(body, "conv_norm_bwd", T,
                     [_rows(), _rows(), _rows(), _full((1, D_MODEL)), _full((1, D_MODEL))],
                     [_rows(), _rows(), _full((1, D_MODEL)), _full((1, D_MODEL))],
                     [_sds((T, D_MODEL), F32), _sds((T, D_MODEL), BF16), _sds((1, D_MODEL), F32),
                      _sds((1, D_MODEL), F32)], (da2, z, u1, ln_g, ln_b))


def _conv_bwd(proj, du1, dw, Bl, S):
    tr = CONV_TILE
    nj = S // tr
    hb = tr // CONV_HALO

    def body(a_ref, b_ref, ah_ref, bh_ref, du_ref, duh_ref, dw_ref, dab_ref, ddw_ref, ddb_ref, uext, dext, du0):
        first = (pl.program_id(0) == 0) & (pl.program_id(1) == 0)
        j = pl.program_id(1)

        @pl.when(first)
        def _():
            ddw_ref[...] = jnp.zeros_like(ddw_ref)
            ddb_ref[...] = jnp.zeros_like(ddb_ref)

        halo = ah_ref[0].astype(F32) * _sigmoid(bh_ref[0].astype(F32))
        uext[pl.ds(0, CONV_HALO), :] = jnp.where(j > 0, halo, 0.0)
        av = a_ref[0].astype(F32)
        sb = _sigmoid(b_ref[0].astype(F32))
        uext[pl.ds(CONV_HALO, tr), :] = av * sb
        dext[pl.ds(0, tr), :] = du_ref[0]
        dext[pl.ds(tr, CONV_HALO), :] = jnp.where(j < nj - 1, duh_ref[0], 0.0)
        ddb_ref[...] += jnp.sum(du_ref[0], axis=0, keepdims=True)

        def cols(c, carry):
            cs = pl.ds(pl.multiple_of(c * LANES, LANES), LANES)
            for rc in range(tr // CONV_CHUNK):
                base = rc * CONV_CHUNK
                dchunk = dext[pl.ds(base, CONV_CHUNK), cs]
                acc = jnp.zeros((CONV_CHUNK, LANES), F32)
                for w in range(CONV_WIDTH):
                    acc = acc + dw_ref[pl.ds(w, 1), cs] * dext[pl.ds(base + CONV_WIDTH - 1 - w, CONV_CHUNK), cs]
                    off = base + CONV_HALO - (CONV_WIDTH - 1) + w
                    ddw_ref[pl.ds(w, 1), cs] += jnp.sum(dchunk * uext[pl.ds(off, CONV_CHUNK), cs],
                                                        axis=0, keepdims=True)
                du0[pl.ds(base, CONV_CHUNK), cs] = acc
            return carry

        lax.fori_loop(0, D_MODEL // LANES, cols, 0)
        g = du0[...]
        dab_ref[0, :, 0:D_MODEL] = (g * sb).astype(BF16)
        dab_ref[0, :, D_MODEL:2 * D_MODEL] = (g * av * sb * (1.0 - sb)).astype(BF16)

    tile = lambda cb: pl.BlockSpec((1, tr, D_MODEL), lambda b, j: (b, j, cb))
    halo = lambda cb: pl.BlockSpec((1, CONV_HALO, D_MODEL), lambda b, j: (b, jnp.maximum(j * hb - 1, 0), cb))
    nxt = pl.BlockSpec((1, CONV_HALO, D_MODEL), lambda b, j: (b, jnp.minimum((j + 1) * hb, S // CONV_HALO - 1), 0))
    par = lambda r: pl.BlockSpec((r, D_MODEL), lambda b, j: (0, 0))
    p3 = proj.reshape(Bl, S, 2 * D_MODEL)
    d3 = du1.reshape(Bl, S, D_MODEL)
    dab, ddw, ddb = pl.pallas_call(
        body, name="conv_bwd", grid=(Bl, nj),
        in_specs=[tile(0), tile(1), halo(0), halo(1), tile(0), nxt, par(32)],
        out_specs=[pl.BlockSpec((1, tr, 2 * D_MODEL), lambda b, j: (b, j, 0)), par(32), par(1)],
        out_shape=[_sds((Bl, S, 2 * D_MODEL), BF16), _sds((32, D_MODEL), F32), _sds((1, D_MODEL), F32)],
        scratch_shapes=[pltpu.VMEM((tr + CONV_HALO, D_MODEL), F32), pltpu.VMEM((tr + CONV_HALO, D_MODEL), F32),
                        pltpu.VMEM((tr, D_MODEL), F32)],
        compiler_params=_params(("arbitrary", "arbitrary")),
    )(p3, p3, p3, p3, d3, d3, dw)
    return dab.reshape(Bl * S, 2 * D_MODEL), ddw, ddb


_LAYOUT = (
    ("pre_norm_g", (4, 1024), None), ("post_norm_g", (4, 1024), None),
    ("attn_w_in", (2, 1024, 2560), 2), ("attn_w_out", (2, 256, 1024), 1),
    ("conv_w_in", (2, 1024, 768), 2), ("conv_dw_w", (2, 31, 256), 2),
    ("conv_dw_b", (2, 256), 1), ("conv_ln_g", (2, 256), 1), ("conv_ln_b", (2, 256), 1),
    ("conv_w_out", (2, 256, 1024), 1), ("ple_w_proj", (4, 256, 256), 2), ("ple_w_gate", (4, 256, 1024), 1),
)
_MATMUL_WEIGHTS = ("attn_w_in", "attn_w_out", "conv_w_in", "conv_w_out", "ple_w_proj", "ple_w_gate")
_SMALL_WEIGHTS = ("conv_dw_w", "conv_dw_b", "conv_ln_g", "conv_ln_b")
_SHAPE = {n: s for n, s, _ in _LAYOUT}
_AXIS = {n: a for n, _, a in _LAYOUT}


def _size(shape):
    n = 1
    for s in shape:
        n *= s
    return n


def _pack_rows(pieces, dtype):
    flat = jnp.concatenate([p.reshape(-1).astype(dtype) for p in pieces])
    pad = FLAT_ROWS * FLAT_COLS - flat.shape[0]
    assert pad >= 0
    return jnp.concatenate([flat, jnp.zeros((pad,), dtype)]).reshape(FLAT_ROWS, FLAT_COLS)


def _pack_f32(params):
    return _pack_rows([params[n] for n, _, _ in _LAYOUT], F32)


def _unpack_f32(flat):
    flat = flat.reshape(-1)
    out, off = {}, 0
    for n, shape, _ in _LAYOUT:
        out[n] = flat[off:off + _size(shape)].reshape(shape)
        off += _size(shape)
    return out


def _pack_gather_payload(w):
    pieces = [w[n].astype(BF16) for n in _MATMUL_WEIGHTS]
    pieces += [lax.bitcast_convert_type(w[n], BF16) for n in _SMALL_WEIGHTS]
    return _pack_rows(pieces, BF16)


def _unpack_gathered(g):
    g = g.reshape(N_CHIPS, -1)
    out, off = {}, 0
    for n in _MATMUL_WEIGHTS:
        size = _size(_SHAPE[n])
        parts = [g[s, off:off + size].reshape(_SHAPE[n]) for s in range(N_CHIPS)]
        out[n] = jnp.concatenate(parts, axis=_AXIS[n])
        off += size
    for n in _SMALL_WEIGHTS:
        size = 2 * _size(_SHAPE[n])
        parts = [lax.bitcast_convert_type(g[s, off:off + size].reshape(_SHAPE[n] + (2,)), F32)
                 for s in range(N_CHIPS)]
        out[n] = jnp.concatenate(parts, axis=_AXIS[n])
        off += size
    return out


def _pack_full_grads(grads):
    rows = []
    for s in range(N_CHIPS):
        pieces = []
        for n, shape, axis in _LAYOUT:
            gfull = grads[n]
            if axis is not None:
                gfull = lax.slice_in_dim(gfull, s * shape[axis], (s + 1) * shape[axis], axis=axis)
            pieces.append(gfull)
        rows.append(_pack_rows(pieces, F32))
    return jnp.stack(rows)


_ANY = pl.BlockSpec(memory_space=pl.ANY)


def _mesh_pos():
    return lax.axis_index("x"), lax.axis_index("y"), lax.axis_index("c")


def _other_chips(x, y):
    return [(1 - x, y), (x, 1 - y), (1 - x, 1 - y)]


def _allgather_weights(wl):
    R, C = wl.shape
    H = R // 2

    def body(w_ref, out_ref, send_sems, recv_sems):
        x, y, c = _mesh_pos()
        me, sibling = (x, y, c), (x, y, 1 - c)
        chips = _other_chips(x, y)

        def half(px, py, pc):
            return out_ref.at[2 * px + py, pl.ds(pc * H, H), :]

        def copy(k, block, to, src=None):
            return pltpu.make_async_remote_copy(
                src_ref=half(*block) if src is None else src, dst_ref=half(*block),
                send_sem=send_sems.at[k], recv_sem=recv_sems.at[k], device_id=to, device_id_type=MESH)

        own = pltpu.make_async_remote_copy(
            src_ref=w_ref, dst_ref=out_ref.at[2 * x + y], send_sem=send_sems.at[6], recv_sem=recv_sems.at[6],
            device_id=sibling, device_id_type=MESH)
        own.start()
        first = [copy(j, me, (*chip, c), src=w_ref.at[pl.ds(c * H, H), :]) for j, chip in enumerate(chips)]
        for cp in first:
            cp.start()
        passed = [copy(3 + j, (*chip, c), sibling) for j, chip in enumerate(chips)]
        for j, chip in enumerate(chips):
            copy(j, (*chip, c), me).wait_recv()
            passed[j].start()
        for j, chip in enumerate(chips):
            copy(3 + j, (*chip, 1 - c), me).wait_recv()
        own.wait()
        for cp in first + passed:
            cp.wait_send()

    return pl.pallas_call(
        body, name="allgather_weights", in_specs=[_ANY], out_specs=_ANY,
        out_shape=_sds((N_CHIPS, R, C), wl.dtype),
        scratch_shapes=[pltpu.SemaphoreType.DMA((7,)), pltpu.SemaphoreType.DMA((7,))],
    )(wl)


def _exchange_core_halves(g):
    n, _, H, C = g.shape

    def body(g_ref, got_ref, send_sem, recv_sem):
        x, y, c = _mesh_pos()
        swap = pltpu.make_async_remote_copy(
            src_ref=g_ref.at[pl.ds(0, n), 1 - c], dst_ref=got_ref, send_sem=send_sem, recv_sem=recv_sem,
            device_id=(x, y, 1 - c), device_id_type=MESH)
        swap.start()
        swap.wait()

    return pl.pallas_call(
        body, name="exchange_core_halves", in_specs=[_ANY], out_specs=_ANY,
        out_shape=_sds((n, H, C), g.dtype),
        scratch_shapes=[pltpu.SemaphoreType.DMA, pltpu.SemaphoreType.DMA],
    )(g)


def _scatter_to_chips(p):
    n, H, C = p.shape

    def body(p_ref, q_ref, send_sems, recv_sems):
        x, y, c = _mesh_pos()
        chips = _other_chips(x, y)
        sends = [pltpu.make_async_remote_copy(
            src_ref=p_ref.at[2 * cx + cy], dst_ref=q_ref.at[j], send_sem=send_sems.at[j],
            recv_sem=recv_sems.at[j], device_id=(cx, cy, c), device_id_type=MESH)
            for j, (cx, cy) in enumerate(chips)]
        for cp in sends:
            cp.start()
        for cp in sends:
            cp.wait_recv()
        for cp in sends:
            cp.wait_send()

    return pl.pallas_call(
        body, name="scatter_to_chips", in_specs=[_ANY], out_specs=_ANY,
        out_shape=_sds((n - 1, H, C), p.dtype),
        scratch_shapes=[pltpu.SemaphoreType.DMA((3,)), pltpu.SemaphoreType.DMA((3,))],
    )(p)


def _share_core_halves(r2):
    _, H, C = r2.shape

    def body(r_ref, out_ref, send_sem, recv_sem):
        x, y, c = _mesh_pos()
        send = pltpu.make_async_remote_copy(
            src_ref=r_ref.at[c], dst_ref=out_ref.at[c], send_sem=send_sem, recv_sem=recv_sem,
            device_id=(x, y, 1 - c), device_id_type=MESH)
        send.start()
        send.wait_send()
        pltpu.make_async_remote_copy(
            src_ref=r_ref.at[c], dst_ref=out_ref.at[1 - c], send_sem=send_sem, recv_sem=recv_sem,
            device_id=(x, y, 1 - c), device_id_type=MESH).wait_recv()

    return pl.pallas_call(
        body, name="share_core_halves", in_specs=[_ANY], out_specs=_ANY,
        out_shape=_sds(r2.shape, r2.dtype), input_output_aliases={0: 0},
        scratch_shapes=[pltpu.SemaphoreType.DMA, pltpu.SemaphoreType.DMA],
    )(r2)


def _place():
    x, y, c = _mesh_pos()
    return jnp.stack([c, 2 * x + y]).astype(jnp.int32)


def _sum_pair(g, got, place):
    n, _, H, C = g.shape

    def body(place_ref, a_ref, b_ref, o_ref):
        o_ref[...] = a_ref[...] + b_ref[...]

    spec = pl.BlockSpec((1, ROW_TILE, C), lambda s, i, pr: (s, i, 0))
    return pl.pallas_call(
        body, name="sum_core_pair",
        grid_spec=pltpu.PrefetchScalarGridSpec(
            num_scalar_prefetch=1, grid=(n, H // ROW_TILE),
            in_specs=[pl.BlockSpec((1, None, ROW_TILE, C), lambda s, i, pr: (s, pr[0], i, 0)), spec],
            out_specs=spec),
        out_shape=_sds((n, H, C), g.dtype),
        compiler_params=_params(("parallel", "parallel")))(place, g, got)


def _sum_chips(p, q, place):
    n, H, C = p.shape

    def body(place_ref, own_ref, qx_ref, qy_ref, qxy_ref, o_ref):
        mine = place_ref[1]
        own, qx, qy, qxy = own_ref[0], qx_ref[0], qy_ref[0], qxy_ref[0]

        def term(s):
            rel = jnp.full(own.shape, mine ^ s, jnp.int32)
            return jnp.where(rel == 0, own, jnp.where(rel == 2, qx, jnp.where(rel == 1, qy, qxy)))

        o_ref[0] = ((term(0) + term(1)) + term(2)) + term(3)

    qspec = lambda j: pl.BlockSpec((1, ROW_TILE, C), lambda i, pr: (j, i, 0))
    return pl.pallas_call(
        body, name="sum_chips",
        grid_spec=pltpu.PrefetchScalarGridSpec(
            num_scalar_prefetch=1, grid=(H // ROW_TILE,),
            in_specs=[pl.BlockSpec((1, ROW_TILE, C), lambda i, pr: (pr[1], i, 0)), qspec(0), qspec(1), qspec(2)],
            out_specs=pl.BlockSpec((1, ROW_TILE, C), lambda i, pr: (pr[0], i, 0))),
        out_shape=_sds((2, H, C), p.dtype),
        compiler_params=_params(("parallel",)))(place, p, q, q, q)


def _adamw(w, g, m, v):
    R, C = w.shape

    def body(w_ref, g_ref, m_ref, v_ref, d_ref, nm_ref, nv_ref):
        gv = g_ref[...]
        nm = ADAM_B1 * m_ref[...] + (1.0 - ADAM_B1) * gv
        nv = ADAM_B2 * v_ref[...] + (1.0 - ADAM_B2) * (gv * gv)
        m_hat = nm / (1.0 - ADAM_B1 ** ADAM_STEP)
        v_hat = nv / (1.0 - ADAM_B2 ** ADAM_STEP)
        d_ref[...] = -ADAM_LR * (m_hat / (jnp.sqrt(v_hat) + ADAM_EPS) + ADAM_WD * w_ref[...])
        nm_ref[...] = nm
        nv_ref[...] = nv

    spec = pl.BlockSpec((ROW_TILE, C), lambda i: (i, 0))
    return pl.pallas_call(body, name="adamw", grid=(R // ROW_TILE,), in_specs=[spec] * 4, out_specs=[spec] * 3,
                          out_shape=[_sds((R, C), F32)] * 3, compiler_params=_params(("parallel",)))(w, g, m, v)


def _reduce_scatter_grads(gfull):
    n, R, C = gfull.shape
    place = _place()
    g4 = gfull.reshape(n, 2, R // 2, C)
    p = _sum_pair(g4, _exchange_core_halves(g4), place)
    return _share_core_halves(_sum_chips(p, _scatter_to_chips(p), place)).reshape(R, C)


def _split_attn_w_in(w):
    qkv = w[:, :3 * N_GROUPS * D_MODEL].reshape(D_MODEL, 3, N_GROUPS, D_MODEL // LANES, LANES)
    groups = [qkv[:, :, g].transpose(0, 2, 1, 3).reshape(D_MODEL, 3 * D_MODEL) for g in range(N_GROUPS)]
    return groups, w[:, 3 * N_GROUPS * D_MODEL:]


def _merge_attn_w_in(groups, z):
    parts = [g.reshape(D_MODEL, D_MODEL // LANES, 3, LANES).transpose(0, 2, 1, 3) for g in groups]
    qkv = jnp.stack(parts, axis=2).reshape(D_MODEL, 3 * N_GROUPS * D_MODEL)
    return jnp.concatenate([qkv, z], axis=1)


def _local_step(x, p, positions, loss_target, pre_g, post_g, w):
    Bl, S, _ = x.shape
    T = Bl * S
    cos, sin = _rope_tables(positions)
    xs = x.reshape(T, D_MODEL)
    saved = []
    for i in range(DEPTH):
        j = i // 2
        g_pre, g_post = pre_g[i:i + 1], post_g[i:i + 1]
        h = _rmsnorm_fwd(xs, g_pre)
        st = {"x": xs, "h": h}
        if i % 2 == 0:
            wg, wz = _split_attn_w_in(w["attn_w_in"][j])
            qkvs = [_mm(h, wg[g], out_dtype=BF16, name="attn_in_g") for g in range(N_GROUPS)]
            z = _mm(h, wz, out_dtype=BF16, name="attn_in_z")
            res = [_attn_fwd(qkvs[g], cos, sin, g, Bl, S) for g in range(N_GROUPS)]
            a, o, lse = _attn_combine([r[0] for r in res], [r[1] for r in res], z)
            y = _mm(a, w["attn_w_out"][j], name="attn_out")
            st.update(wg=wg, wz=wz, qkvs=qkvs, z=z, a=a, o=o, lse=lse)
        else:
            w_ab, w_z = w["conv_w_in"][j][:, :2 * D_MODEL], w["conv_w_in"][j][:, 2 * D_MODEL:]
            ab = _mm(h, w_ab, out_dtype=BF16, name="conv_in_ab")
            z = _mm(h, w_z, out_dtype=BF16, name="conv_in_z")
            dw = jnp.pad(w["conv_dw_w"][j], ((0, 1), (0, 0)))
            u1, a = _conv_fwd(ab, z, dw, w["conv_dw_b"][j:j + 1], w["conv_ln_g"][j:j + 1],
                              w["conv_ln_b"][j:j + 1], Bl, S)
            y = _mm(a, w["conv_w_out"][j], name="conv_out")
            st.update(w_ab=w_ab, w_z=w_z, ab=ab, z=z, dw=dw, u1=u1, a=a)
        x1 = _post_fwd(xs, y, g_post)
        pi = p[i].reshape(T, PLE_DIM)
        pe = _mm(pi, w["ple_w_proj"][i], name="ple_proj")
        gl = _mm(x1, w["ple_w_gate"][i], name="ple_gate")
        xs = _ple_fwd(x1, pe, gl)
        st.update(y=y, x1=x1, pi=pi, pe=pe, gl=gl)
        saved.append(st)

    sq, dx = _loss_fwd_bwd(xs, loss_target.reshape(T, D_MODEL))

    grads = {n: [None] * shape[0] for n, shape, _ in _LAYOUT}
    for i in reversed(range(DEPTH)):
        j = i // 2
        st = saved[i]
        g_pre, g_post = pre_g[i:i + 1], post_g[i:i + 1]
        dpe, dgl = _ple_bwd(dx, st["pe"], st["gl"])
        grads["ple_w_proj"][i] = _mm(st["pi"], dpe, ta=True, name="ple_proj_wgrad")
        grads["ple_w_gate"][i] = _mm(st["x1"], dgl, ta=True, tk=512, name="ple_gate_wgrad")
        dx1 = _mm(dgl, w["ple_w_gate"][i], tb=True, add=dx, name="ple_gate_dgrad")
        dy, dg_post = _rmsnorm_bwd(dx1, st["y"], g_post, None, BF16, "post_bwd")
        grads["post_norm_g"][i] = dg_post[0]
        if i % 2 == 0:
            grads["attn_w_out"][j] = _mm(st["a"], dy, ta=True, tk=512, name="attn_out_wgrad")
            da = _mm(dy, w["attn_w_out"][j], tb=True, name="attn_out_dgrad")
            do, dz = _gate_bwd(da, st["o"], st["z"])
            dh = _mm(dz, st["wz"], tb=True, name="attn_in_z_dgrad")
            dwz = _mm(st["h"], dz, ta=True, tk=512, name="attn_in_z_wgrad")
            dwg = []
            for g in range(N_GROUPS):
                dqkv = _attn_bwd(st["qkvs"][g], cos, sin, do, st["o"], st["lse"], g, Bl, S)
                dh = _mm(dqkv, st["wg"][g], tb=True, add=dh, name="attn_in_g_dgrad")
                dwg.append(_mm(st["h"], dqkv, ta=True, tk=512, name="attn_in_g_wgrad"))
            grads["attn_w_in"][j] = _merge_attn_w_in(dwg, dwz)
        else:
            grads["conv_w_out"][j] = _mm(st["a"], dy, ta=True, tk=512, name="conv_out_wgrad")
            da2 = _mm(dy, w["conv_w_out"][j], tb=True, name="conv_out_dgrad")
            du1, dz, dln_g, dln_b = _conv_norm_bwd(da2, st["z"], st["u1"], w["conv_ln_g"][j:j + 1],
                                                   w["conv_ln_b"][j:j + 1])
            dab, ddw, ddb = _conv_bwd(st["ab"], du1, st["dw"], Bl, S)
            dh = _mm(dz, st["w_z"], tb=True, name="conv_in_z_dgrad")
            dh = _mm(dab, st["w_ab"], tb=True, add=dh, name="conv_in_ab_dgrad")
            dw_ab = _mm(st["h"], dab, ta=True, tk=512, name="conv_in_ab_wgrad")
            dw_z = _mm(st["h"], dz, ta=True, tk=512, name="conv_in_z_wgrad")
            grads["conv_w_in"][j] = jnp.concatenate([dw_ab, dw_z], axis=1)
            grads["conv_dw_w"][j] = ddw[:CONV_WIDTH]
            grads["conv_dw_b"][j] = ddb[0]
            grads["conv_ln_g"][j] = dln_g[0]
            grads["conv_ln_b"][j] = dln_b[0]
        dx, dg_pre = _rmsnorm_bwd(dh, st["x"], g_pre, dx1, F32, "pre_bwd")
        grads["pre_norm_g"][i] = dg_pre[0]
    grads = {n: jnp.stack(v) for n, v in grads.items()}
    return sq, dx.reshape(Bl, S, D_MODEL), grads


_NAMES = tuple(n for n, _, _ in _LAYOUT)


def kernel(x, p, positions, pre_norm_g, post_norm_g, attn_w_in, attn_w_out, conv_w_in, conv_dw_w, conv_dw_b, conv_ln_g, conv_ln_b, conv_w_out, ple_w_proj, ple_w_gate, loss_target, m_pre_norm_g, m_post_norm_g, m_attn_w_in, m_attn_w_out, m_conv_w_in, m_conv_dw_w, m_conv_dw_b, m_conv_ln_g, m_conv_ln_b, m_conv_w_out, m_ple_w_proj, m_ple_w_gate, v_pre_norm_g, v_post_norm_g, v_attn_w_in, v_attn_w_out, v_conv_w_in, v_conv_dw_w, v_conv_dw_b, v_conv_ln_g, v_conv_ln_b, v_conv_w_out, v_ple_w_proj, v_ple_w_gate):
    w_loc = dict(zip(_NAMES, (pre_norm_g, post_norm_g, attn_w_in, attn_w_out, conv_w_in, conv_dw_w, conv_dw_b,
                              conv_ln_g, conv_ln_b, conv_w_out, ple_w_proj, ple_w_gate)))
    m_loc = dict(zip(_NAMES, (m_pre_norm_g, m_post_norm_g, m_attn_w_in, m_attn_w_out, m_conv_w_in, m_conv_dw_w,
                              m_conv_dw_b, m_conv_ln_g, m_conv_ln_b, m_conv_w_out, m_ple_w_proj, m_ple_w_gate)))
    v_loc = dict(zip(_NAMES, (v_pre_norm_g, v_post_norm_g, v_attn_w_in, v_attn_w_out, v_conv_w_in, v_conv_dw_w,
                              v_conv_dw_b, v_conv_ln_g, v_conv_ln_b, v_conv_w_out, v_ple_w_proj, v_ple_w_gate)))

    w_full = _unpack_gathered(_allgather_weights(_pack_gather_payload(w_loc)))
    sq, grad_x, grads = _local_step(x, p, positions, loss_target, pre_norm_g, post_norm_g, w_full)
    loss = lax.psum(sq[0, 0] * (0.5 / D_MODEL), ("x", "y", "c"))

    g_flat = _reduce_scatter_grads(_pack_full_grads(grads))
    delta, new_m, new_v = _adamw(_pack_f32(w_loc), g_flat, _pack_f32(m_loc), _pack_f32(v_loc))
    g_out, d_out, m_out, v_out = (_unpack_f32(t) for t in (g_flat, delta, new_m, new_v))
    return (loss, grad_x, *[g_out[n] for n in _NAMES], *[d_out[n] for n in _NAMES],
            *[m_out[n] for n in _NAMES], *[v_out[n] for n in _NAMES])
```

```python
import jax
import jax.numpy as jnp
from jax import lax
from jax.experimental import pallas as pl
from jax.experimental.pallas import tpu as pltpu

F32 = jnp.float32
BF16 = jnp.bfloat16

D_MODEL = 1024
DEPTH = 4
PLE_DIM = 256
HEAD_DIM = 64
WIN_DIL = ((128, 1), (512, 4), (2048, 16))
N_GROUPS = 3
N_BACK = 128
ROPE_THETA = 10000.0
CONV_WIDTH = 31
CONV_HALO = 32
RMS_EPS = 1e-6
LN_EPS = 1e-5
NEG_INF = -1e30
ADAM_LR, ADAM_B1, ADAM_B2, ADAM_EPS, ADAM_WD, ADAM_STEP = 0.001, 0.9, 0.999, 1e-08, 0.01, 10

LANES = 128
N_CHIPS = 4
VMEM_LIMIT = 48 * 1024 * 1024
VMEM_LIMIT_ATTN = 56 * 1024 * 1024
FLAT_COLS = 256
FLAT_ROWS = 36864
FLAT_TILE = 2048
FLAT_ROW_ALIGN = 16
PROJ_COLS = (3 * N_GROUPS + 1) * D_MODEL
HEAD_PAIRS = D_MODEL // LANES

MESH = pl.DeviceIdType.MESH


def _params(sem=None, vmem=VMEM_LIMIT):
    return pltpu.CompilerParams(dimension_semantics=sem, vmem_limit_bytes=vmem)


def _sigmoid(v):
    return 1.0 / (1.0 + jnp.exp(-v))


def _mm(a, b, *, ta=False, tb=False, add=None, out_dtype=F32, tm=1024, tn=1024, tk=1024, name="mm"):
    if ta:
        K, M = a.shape
    else:
        M, K = a.shape
    if tb:
        N, K2 = b.shape
    else:
        K2, N = b.shape
    assert K == K2, (a.shape, b.shape)
    tm, tn, tk = min(tm, M), min(tn, N), min(tk, K)
    assert M % tm == 0 and N % tn == 0 and K % tk == 0
    nk = K // tk
    dims = (((0 if ta else 1,), (1 if tb else 0,)), ((), ()))

    def body(*refs):
        if add is None:
            a_ref, b_ref, o_ref, acc_ref = refs
        else:
            a_ref, b_ref, add_ref, o_ref, acc_ref = refs
        k = pl.program_id(2)

        @pl.when(k == 0)
        def _():
            acc_ref[...] = jnp.zeros_like(acc_ref)

        acc_ref[...] += lax.dot_general(a_ref[...].astype(BF16), b_ref[...].astype(BF16), dims,
                                        preferred_element_type=F32)

        @pl.when(k == nk - 1)
        def _():
            r = acc_ref[...]
            if add is not None:
                r = r + add_ref[...].astype(F32)
            o_ref[...] = r.astype(out_dtype)

    a_spec = pl.BlockSpec((tk, tm), lambda i, j, k: (k, i)) if ta else pl.BlockSpec((tm, tk), lambda i, j, k: (i, k))
    b_spec = pl.BlockSpec((tn, tk), lambda i, j, k: (j, k)) if tb else pl.BlockSpec((tk, tn), lambda i, j, k: (k, j))
    o_spec = pl.BlockSpec((tm, tn), lambda i, j, k: (i, j))
    in_specs, args = [a_spec, b_spec], [a, b]
    if add is not None:
        in_specs.append(o_spec)
        args.append(add)
    return pl.pallas_call(
        body, name=name, grid=(M // tm, N // tn, nk),
        in_specs=in_specs, out_specs=o_spec,
        out_shape=jax.ShapeDtypeStruct((M, N), out_dtype),
        scratch_shapes=[pltpu.VMEM((tm, tn), F32)],
        compiler_params=_params(("parallel", "parallel", "arbitrary")),
    )(*args)


ROW_TILE = 512


def _rows(w=D_MODEL, cb=0, tr=ROW_TILE):
    return pl.BlockSpec((tr, w), lambda i: (i, cb))


def _full(shape):
    return pl.BlockSpec(shape, lambda i: (0,) * len(shape))


def _row_call(body, name, T, in_specs, out_specs, out_shape, args, tr=ROW_TILE):
    return pl.pallas_call(body, name=name, grid=(T // tr,), in_specs=in_specs, out_specs=out_specs,
                          out_shape=out_shape, compiler_params=_params(("arbitrary",)))(*args)


def _sds(shape, dtype):
    return jax.ShapeDtypeStruct(shape, dtype)


def _rmsnorm_fwd(x, g):
    T = x.shape[0]

    def body(x_ref, g_ref, h_ref):
        xv = x_ref[...]
        r = lax.rsqrt(jnp.mean(xv * xv, axis=1, keepdims=True) + RMS_EPS)
        h_ref[...] = (xv * r * g_ref[...]).astype(BF16)

    return _row_call(body, "rmsnorm_fwd", T, [_rows(), _full((1, D_MODEL))], _rows(),
                     _sds((T, D_MODEL), BF16), (x, g))


def _post_fwd(x, y, g):
    T = x.shape[0]

    def body(x_ref, y_ref, g_ref, o_ref):
        yv = y_ref[...]
        r = lax.rsqrt(jnp.mean(yv * yv, axis=1, keepdims=True) + RMS_EPS)
        o_ref[...] = x_ref[...] + yv * r * g_ref[...]

    return _row_call(body, "post_fwd", T, [_rows(), _rows(), _full((1, D_MODEL))], _rows(),
                     _sds((T, D_MODEL), F32), (x, y, g))


def _rmsnorm_bwd(dout, xin, g, add, out_dtype, name):
    T = xin.shape[0]

    def body(*refs):
        if add is None:
            d_ref, x_ref, g_ref, dx_ref, dg_ref = refs
        else:
            d_ref, x_ref, g_ref, add_ref, dx_ref, dg_ref = refs

        @pl.when(pl.program_id(0) == 0)
        def _():
            dg_ref[...] = jnp.zeros_like(dg_ref)

        xv = x_ref[...]
        dv = d_ref[...].astype(F32)
        r = lax.rsqrt(jnp.mean(xv * xv, axis=1, keepdims=True) + RMS_EPS)
        xh = xv * r
        dg_ref[...] += jnp.sum(dv * xh, axis=0, keepdims=True)
        dn = dv * g_ref[...]
        dx = r * (dn - xh * jnp.mean(dn * xh, axis=1, keepdims=True))
        if add is not None:
            dx = dx + add_ref[...]
        dx_ref[...] = dx.astype(out_dtype)

    in_specs = [_rows(), _rows(), _full((1, D_MODEL))]
    args = [dout, xin, g]
    if add is not None:
        in_specs.append(_rows())
        args.append(add)
    return _row_call(body, name, T, in_specs, [_rows(), _full((1, D_MODEL))],
                     [_sds((T, D_MODEL), out_dtype), _sds((1, D_MODEL), F32)], args)


def _ple_fwd(x1, pe, gl):
    T = x1.shape[0]

    def body(x_ref, pe_ref, gl_ref, o_ref):
        o_ref[...] = x_ref[...] + pe_ref[...] * _sigmoid(gl_ref[...])

    return _row_call(body, "ple_fwd", T, [_rows()] * 3, _rows(), _sds((T, D_MODEL), F32), (x1, pe, gl))


def _ple_bwd(dx2, pe, gl):
    T = dx2.shape[0]

    def body(d_ref, pe_ref, gl_ref, dpe_ref, dgl_ref):
        dv = d_ref[...]
        sg = _sigmoid(gl_ref[...])
        dpe_ref[...] = (dv * sg).astype(BF16)
        dgl_ref[...] = (dv * pe_ref[...] * sg * (1.0 - sg)).astype(BF16)

    return _row_call(body, "ple_bwd", T, [_rows()] * 3, [_rows()] * 2,
                     [_sds((T, D_MODEL), BF16)] * 2, (dx2, pe, gl))


def _loss_fwd_bwd(y, target):
    T = y.shape[0]

    def body(y_ref, t_ref, s_ref, d_ref):
        @pl.when(pl.program_id(0) == 0)
        def _():
            s_ref[...] = jnp.zeros_like(s_ref)

        e = y_ref[...] - t_ref[...]
        s_ref[...] += jnp.sum(e * e).reshape(1, 1)
        d_ref[...] = e * (1.0 / D_MODEL)

    return _row_call(body, "loss", T, [_rows()] * 2, [_full((1, 1)), _rows()],
                     [_sds((1, 1), F32), _sds((T, D_MODEL), F32)], (y, target))


def _attn_combine(outs, lses, proj):
    T = proj.shape[0]

    def body(o0, o1, o2, l0, l1, l2, z_ref, a_ref, o_ref, lse_ref):
        a0, a1, a2 = l0[...], l1[...], l2[...]
        m = jnp.maximum(jnp.maximum(a0, a1), a2)
        e0, e1, e2 = jnp.exp(a0 - m), jnp.exp(a1 - m), jnp.exp(a2 - m)
        ssum = e0 + e1 + e2
        o = (e0 * o0[...] + e1 * o1[...] + e2 * o2[...]) / ssum
        zv = z_ref[...].astype(F32)
        o_ref[...] = o
        lse_ref[...] = m + jnp.log(ssum)
        a_ref[...] = (o * zv * _sigmoid(zv)).astype(BF16)

    return _row_call(body, "attn_combine", T, [_rows()] * 6 + [_rows(cb=3 * N_GROUPS)], [_rows()] * 3,
                     [_sds((T, D_MODEL), BF16), _sds((T, D_MODEL), F32), _sds((T, D_MODEL), F32)],
                     (*outs, *lses, proj))


def _gate_bwd(da, o, proj):
    T = da.shape[0]

    def body(da_ref, o_ref, z_ref, do_ref, dz_ref):
        dv = da_ref[...]
        zv = z_ref[...]
        sg = _sigmoid(zv)
        do_ref[...] = dv * zv * sg
        dz_ref[...] = dv * o_ref[...] * sg * (1.0 + zv * (1.0 - sg))

    zcols = _rows(cb=3 * N_GROUPS)
    return _row_call(body, "gate_bwd", T, [_rows(), _rows(), zcols], [_rows(), zcols],
                     [_sds((T, D_MODEL), F32), _sds((T, PROJ_COLS), F32)], (da, o, proj))


def _rope_table(positions):
    inv_freq = 1.0 / (ROPE_THETA ** (jnp.arange(0, HEAD_DIM, 2, dtype=F32) / HEAD_DIM))
    ang = positions.astype(F32)[..., None] * inv_freq
    cos, sin = jnp.cos(ang), jnp.sin(ang)
    return jnp.concatenate([cos, cos, -sin, sin], axis=-1)


def _rotate_half_partner(t):
    lane = lax.broadcasted_iota(jnp.int32, t.shape, 1)
    return jnp.where((lane % HEAD_DIM) < HEAD_DIM // 2,
                     pltpu.roll(t, LANES - HEAD_DIM // 2, 1), pltpu.roll(t, HEAD_DIM // 2, 1))


def _mask_bias(first):
    qi = lax.broadcasted_iota(jnp.int32, (N_BACK, 2 * N_BACK), 0)
    kj = lax.broadcasted_iota(jnp.int32, (N_BACK, 2 * N_BACK), 1)
    ok = (kj >= qi) & (kj <= qi + N_BACK)
    if first:
        ok = ok & (kj >= N_BACK)
    return jnp.where(ok, 0.0, NEG_INF).astype(F32)


def _block_loop(nb, block):
    block(0, _mask_bias(True))
    if nb > 1:
        bias = _mask_bias(False)

        def step(n, carry):
            block(n, bias)
            return carry

        lax.fori_loop(1, nb, step, 0, unroll=2)


_NT = (((1,), (1,)), ((), ()))
_TN = (((0,), (0,)), ((), ()))


def _cos_sin(tab):
    lane = lax.broadcasted_iota(jnp.int32, tab.shape, 1)
    swapped = pltpu.roll(tab, HEAD_DIM, 1)
    low = lane < HEAD_DIM
    return jnp.where(low, tab, swapped), jnp.where(low, swapped, tab)


def _residue_rows(r, i, d):
    start = r + i * (N_BACK * d)
    if d == 1:
        return pl.ds(pl.multiple_of(start, N_BACK), N_BACK)
    return pl.ds(start, N_BACK, stride=d)


def _seq_rows(i):
    return pl.ds(pl.multiple_of(i * N_BACK, N_BACK), N_BACK)


def _rope_residue(q_ref, k_ref, v_ref, tab_ref, qr, kr, vp, r, d, nb):
    zeros = jnp.zeros((N_BACK, LANES), BF16)
    kr[pl.ds(0, N_BACK), :] = zeros
    vp[pl.ds(0, N_BACK), :] = zeros

    def body(i, carry):
        rows = _residue_rows(r, i, d)
        cs, sn = _cos_sin(tab_ref[rows, :])
        q, k = q_ref[rows, :], k_ref[rows, :]
        qr[_seq_rows(i), :] = ((q * cs + _rotate_half_partner(q) * sn) * (HEAD_DIM ** -0.5)).astype(BF16)
        kr[_seq_rows(i + 1), :] = (k * cs + _rotate_half_partner(k) * sn).astype(BF16)
        vp[_seq_rows(i + 1), :] = v_ref[rows, :].astype(BF16)
        return carry

    lax.fori_loop(0, nb, body, 0)


def _qkv_specs(S, group):
    col = lambda which: pl.BlockSpec((None, S, LANES),
                                     lambda b, hp: (b, 0, (which * N_GROUPS + group) * HEAD_PAIRS + hp))
    return [col(0), col(1), col(2), pl.BlockSpec((None, S, LANES), lambda b, hp: (b, 0, 0))]


def _attn_fwd(proj, tab, group, Bl, S):
    d = WIN_DIL[group][1]
    L = S // d
    nb = L // N_BACK
    assert WIN_DIL[group][0] // d == N_BACK and L % N_BACK == 0

    def body(q_ref, k_ref, v_ref, tab_ref, o_ref, lse_ref, qr, kr, vp):
        head0 = lax.broadcasted_iota(jnp.int32, (1, LANES), 1) < HEAD_DIM

        def residue(r, carry):
            _rope_residue(q_ref, k_ref, v_ref, tab_ref, qr, kr, vp, r, d, nb)

            def block(n, bias):
                win = pl.ds(pl.multiple_of(n * N_BACK, N_BACK), 2 * N_BACK)
                qb, kw, vw = qr[_seq_rows(n), :], kr[win, :], vp[win, :]
                outs, lses = [], []
                for hm in (head0, jnp.logical_not(head0)):
                    qm = jnp.where(hm, qb, jnp.zeros_like(qb))
                    s = lax.dot_general(qm, kw, _NT, preferred_element_type=F32) + bias
                    m = jnp.max(s, axis=1, keepdims=True)
                    p = jnp.exp(s - m)
                    l = jnp.sum(p, axis=1, keepdims=True)
                    pv = jnp.dot(p.astype(BF16), vw, preferred_element_type=F32)
                    outs.append(pv * (1.0 / l))
                    lses.append(m + jnp.log(l))
                rows = _residue_rows(r, n, d)
                o_ref[rows, :] = jnp.where(head0, outs[0], outs[1])
                lse_ref[rows, :] = jnp.where(head0, lses[0], lses[1]) + jnp.zeros((N_BACK, LANES), F32)

            _block_loop(nb, block)
            return carry

        lax.fori_loop(0, d, residue, 0)

    act = pl.BlockSpec((None, S, LANES), lambda b, hp: (b, 0, hp))
    p3 = proj.reshape(Bl, S, PROJ_COLS)
    o, lse = pl.pallas_call(
        body, name="attn_fwd_g%d" % group, grid=(Bl, HEAD_PAIRS),
        in_specs=_qkv_specs(S, group), out_specs=[act, act],
        out_shape=[_sds((Bl, S, D_MODEL), F32)] * 2,
        scratch_shapes=[pltpu.VMEM((L, LANES), BF16), pltpu.VMEM((L + N_BACK, LANES), BF16),
                        pltpu.VMEM((L + N_BACK, LANES), BF16)],
        compiler_params=_params(("parallel", "arbitrary"), VMEM_LIMIT_ATTN),
    )(p3, p3, p3, tab)
    return o.reshape(Bl * S, D_MODEL), lse.reshape(Bl * S, D_MODEL)


def _attn_bwd(proj, tab, do, o, lse, dproj, group, Bl, S):
    d = WIN_DIL[group][1]
    L = S // d
    nb = L // N_BACK

    def body(q_ref, k_ref, v_ref, tab_ref, do_ref, o_ref, lse_ref, dproj_in, dproj_ref,
             qr, kr, vp, dk_acc, dv_acc, dq_s, dk_s, dv_s, sems):
        del dproj_in
        head0 = lax.broadcasted_iota(jnp.int32, (1, LANES), 1) < HEAD_DIM

        def residue(r, carry):
            _rope_residue(q_ref, k_ref, v_ref, tab_ref, qr, kr, vp, r, d, nb)
            dk_acc[...] = jnp.zeros_like(dk_acc)
            dv_acc[...] = jnp.zeros_like(dv_acc)

            def block(n, bias):
                win = pl.ds(pl.multiple_of(n * N_BACK, N_BACK), 2 * N_BACK)
                rows = _residue_rows(r, n, d)
                qb, kw, vw = qr[_seq_rows(n), :], kr[win, :], vp[win, :]
                dof = do_ref[rows, :]
                dob = dof.astype(BF16)
                lse_b = lse_ref[rows, :]
                dsum = dof * o_ref[rows, :]
                dqs = []
                dk = jnp.zeros((2 * N_BACK, LANES), F32)
                dv = jnp.zeros((2 * N_BACK, LANES), F32)
                for h, hm in enumerate((head0, jnp.logical_not(head0))):
                    qm = jnp.where(hm, qb, jnp.zeros_like(qb))
                    dom = jnp.where(hm, dob, jnp.zeros_like(dob))
                    s = lax.dot_general(qm, kw, _NT, preferred_element_type=F32) + bias
                    p = jnp.exp(s - lse_b[:, h * HEAD_DIM:h * HEAD_DIM + 1])
                    dp = lax.dot_general(dom, vw, _NT, preferred_element_type=F32)
                    delta = jnp.sum(jnp.where(hm, dsum, 0.0), axis=1, keepdims=True)
                    ds = (p * (dp - delta)).astype(BF16)
                    dqs.append(jnp.dot(ds, kw, preferred_element_type=F32))
                    dk = dk + lax.dot_general(ds, qm, _TN, preferred_element_type=F32)
                    dv = dv + lax.dot_general(p.astype(BF16), dom, _TN, preferred_element_type=F32)
                dq = jnp.where(head0, dqs[0], dqs[1]) * (HEAD_DIM ** -0.5)
                cs, sn = _cos_sin(tab_ref[rows, :])
                dq_s[rows, :] = dq * cs + _rotate_half_partner(dq * sn)
                dk_acc[win, :] += dk
                dv_acc[win, :] += dv

            _block_loop(nb, block)

            def finish(i, carry2):
                rows = _residue_rows(r, i, d)
                cs, sn = _cos_sin(tab_ref[rows, :])
                dk = dk_acc[_seq_rows(i + 1), :]
                dk_s[rows, :] = dk * cs + _rotate_half_partner(dk * sn)
                dv_s[rows, :] = dv_acc[_seq_rows(i + 1), :]
                return carry2

            lax.fori_loop(0, nb, finish, 0)
            return carry

        lax.fori_loop(0, d, residue, 0)
        b, hp = pl.program_id(0), pl.program_id(1)
        copies = []
        for which, src in enumerate((dq_s, dk_s, dv_s)):
            col = ((which * N_GROUPS + group) * HEAD_PAIRS + hp) * LANES
            copies.append(pltpu.make_async_copy(
                src, dproj_ref.at[b, :, pl.ds(pl.multiple_of(col, LANES), LANES)], sems.at[which]))
        for cp in copies:
            cp.start()
        for cp in copies:
            cp.wait()

    act = pl.BlockSpec((None, S, LANES), lambda b, hp: (b, 0, hp))
    p3 = proj.reshape(Bl, S, PROJ_COLS)
    view = lambda t: t.reshape(Bl, S, D_MODEL)
    out = pl.pallas_call(
        body, name="attn_bwd_g%d" % group, grid=(Bl, HEAD_PAIRS),
        in_specs=_qkv_specs(S, group) + [act, act, act, _ANY], out_specs=_ANY,
        out_shape=_sds((Bl, S, PROJ_COLS), F32), input_output_aliases={7: 0},
        scratch_shapes=[pltpu.VMEM((L, LANES), BF16), pltpu.VMEM((L + N_BACK, LANES), BF16),
                        pltpu.VMEM((L + N_BACK, LANES), BF16),
                        pltpu.VMEM((L + N_BACK, LANES), F32), pltpu.VMEM((L + N_BACK, LANES), F32),
                        pltpu.VMEM((S, LANES), F32), pltpu.VMEM((S, LANES), F32), pltpu.VMEM((S, LANES), F32),
                        pltpu.SemaphoreType.DMA((3,))],
        compiler_params=_params(("arbitrary", "arbitrary"), VMEM_LIMIT_ATTN),
    )(p3, p3, p3, tab, view(do), view(o), view(lse), dproj.reshape(Bl, S, PROJ_COLS))
    return out.reshape(Bl * S, PROJ_COLS)


CONV_TILE = 256
CONV_CHUNK = 64


def _conv_fwd(proj, z, dw, dwb, ln_g, ln_b, Bl, S):
    tr = CONV_TILE
    nj = S // tr
    hb = tr // CONV_HALO

    def body(a_ref, b_ref, ah_ref, bh_ref, z_ref, dw_ref, dwb_ref, g_ref, bb_ref, u1_ref, out_ref, ext):
        j = pl.program_id(1)
        halo = ah_ref[0].astype(F32) * _sigmoid(bh_ref[0].astype(F32))
        ext[pl.ds(0, CONV_HALO), :] = jnp.where(j > 0, halo, 0.0)
        ext[pl.ds(CONV_HALO, tr), :] = a_ref[0].astype(F32) * _sigmoid(b_ref[0].astype(F32))

        def cols(c, carry):
            cs = pl.ds(pl.multiple_of(c * LANES, LANES), LANES)
            for rc in range(tr // CONV_CHUNK):
                acc = jnp.zeros((CONV_CHUNK, LANES), F32)
                for w in range(CONV_WIDTH):
                    off = rc * CONV_CHUNK + CONV_HALO - (CONV_WIDTH - 1) + w
                    acc = acc + dw_ref[pl.ds(w, 1), cs] * ext[pl.ds(off, CONV_CHUNK), cs]
                u1_ref[0, pl.ds(rc * CONV_CHUNK, CONV_CHUNK), cs] = acc + dwb_ref[:, cs]
            return carry

        lax.fori_loop(0, D_MODEL // LANES, cols, 0)
        u1 = u1_ref[0]
        mu = jnp.mean(u1, axis=1, keepdims=True)
        xc = u1 - mu
        rstd = lax.rsqrt(jnp.mean(xc * xc, axis=1, keepdims=True) + LN_EPS)
        u2 = xc * rstd * g_ref[...] + bb_ref[...]
        zv = z_ref[0].astype(F32)
        out_ref[0] = (u2 * _sigmoid(u2) * zv * _sigmoid(zv)).astype(BF16)

    tile = lambda cb: pl.BlockSpec((1, tr, D_MODEL), lambda b, j: (b, j, cb))
    halo = lambda cb: pl.BlockSpec((1, CONV_HALO, D_MODEL), lambda b, j: (b, jnp.maximum(j * hb - 1, 0), cb))
    par = lambda r: pl.BlockSpec((r, D_MODEL), lambda b, j: (0, 0))
    p3 = proj.reshape(Bl, S, 2 * D_MODEL)
    u1, out = pl.pallas_call(
        body, name="conv_fwd", grid=(Bl, nj),
        in_specs=[tile(0), tile(1), halo(0), halo(1), tile(0), par(32), par(1), par(1), par(1)],
        out_specs=[tile(0), tile(0)],
        out_shape=[_sds((Bl, S, D_MODEL), F32), _sds((Bl, S, D_MODEL), BF16)],
        scratch_shapes=[pltpu.VMEM((tr + CONV_HALO, D_MODEL), F32)],
        compiler_params=_params(("parallel", "arbitrary")),
    )(p3, p3, p3, p3, z.reshape(Bl, S, D_MODEL), dw, dwb, ln_g, ln_b)
    return u1.reshape(Bl * S, D_MODEL), out.reshape(Bl * S, D_MODEL)


def _conv_norm_bwd(da2, z, u1, ln_g, ln_b):
    T = da2.shape[0]

    def body(da_ref, z_ref, u_ref, g_ref, b_ref, du_ref, dz_ref, dg_ref, db_ref):
        @pl.when(pl.program_id(0) == 0)
        def _():
            dg_ref[...] = jnp.zeros_like(dg_ref)
            db_ref[...] = jnp.zeros_like(db_ref)

        u1 = u_ref[...]
        mu = jnp.mean(u1, axis=1, keepdims=True)
        xc = u1 - mu
        rstd = lax.rsqrt(jnp.mean(xc * xc, axis=1, keepdims=True) + LN_EPS)
        nrm = xc * rstd
        u2 = nrm * g_ref[...] + b_ref[...]
        s2 = _sigmoid(u2)
        zv = z_ref[...].astype(F32)
        sz = _sigmoid(zv)
        dv = da_ref[...]
        dz_ref[...] = (dv * u2 * s2 * sz * (1.0 + zv * (1.0 - sz))).astype(BF16)
        du2 = dv * zv * sz * s2 * (1.0 + u2 * (1.0 - s2))
        dg_ref[...] += jnp.sum(du2 * nrm, axis=0, keepdims=True)
        db_ref[...] += jnp.sum(du2, axis=0, keepdims=True)
        dn = du2 * g_ref[...]
        du_ref[...] = rstd * (dn - jnp.mean(dn, axis=1, keepdims=True)
                              - nrm * jnp.mean(dn * nrm, axis=1, keepdims=True))

    return _row_call(body, "conv_norm_bwd", T,
                     [_rows(), _rows(), _rows(), _full((1, D_MODEL)), _full((1, D_MODEL))],
                     [_rows(), _rows(), _full((1, D_MODEL)), _full((1, D_MODEL))],
                     [_sds((T, D_MODEL), F32), _sds((T, D_MODEL), BF16), _sds((1, D_MODEL), F32),
                      _sds((1, D_MODEL), F32)], (da2, z, u1, ln_g, ln_b))


def _conv_bwd(proj, du1, dw, Bl, S):
    tr = CONV_TILE
    nj = S // tr
    hb = tr // CONV_HALO

    def body(a_ref, b_ref, ah_ref, bh_ref, du_ref, duh_ref, dw_ref, dab_ref, ddw_ref, ddb_ref, uext, dext, du0):
        first = (pl.program_id(0) == 0) & (pl.program_id(1) == 0)
        j = pl.program_id(1)

        @pl.when(first)
        def _():
            ddw_ref[...] = jnp.zeros_like(ddw_ref)
            ddb_ref[...] = jnp.zeros_like(ddb_ref)

        halo = ah_ref[0].astype(F32) * _sigmoid(bh_ref[0].astype(F32))
        uext[pl.ds(0, CONV_HALO), :] = jnp.where(j > 0, halo, 0.0)
        av = a_ref[0].astype(F32)
        sb = _sigmoid(b_ref[0].astype(F32))
        uext[pl.ds(CONV_HALO, tr), :] = av * sb
        dext[pl.ds(0, tr), :] = du_ref[0]
        dext[pl.ds(tr, CONV_HALO), :] = jnp.where(j < nj - 1, duh_ref[0], 0.0)
        ddb_ref[...] += jnp.sum(du_ref[0], axis=0, keepdims=True)

        def cols(c, carry):
            cs = pl.ds(pl.multiple_of(c * LANES, LANES), LANES)
            for rc in range(tr // CONV_CHUNK):
                base = rc * CONV_CHUNK
                dchunk = dext[pl.ds(base, CONV_CHUNK), cs]
                acc = jnp.zeros((CONV_CHUNK, LANES), F32)
                for w in range(CONV_WIDTH):
                    acc = acc + dw_ref[pl.ds(w, 1), cs] * dext[pl.ds(base + CONV_WIDTH - 1 - w, CONV_CHUNK), cs]
                    off = base + CONV_HALO - (CONV_WIDTH - 1) + w
                    ddw_ref[pl.ds(w, 1), cs] += jnp.sum(dchunk * uext[pl.ds(off, CONV_CHUNK), cs],
                                                        axis=0, keepdims=True)
                du0[pl.ds(base, CONV_CHUNK), cs] = acc
            return carry

        lax.fori_loop(0, D_MODEL // LANES, cols, 0)
        g = du0[...]
        dab_ref[0, :, 0:D_MODEL] = (g * sb).astype(BF16)
        dab_ref[0, :, D_MODEL:2 * D_MODEL] = (g * av * sb * (1.0 - sb)).astype(BF16)

    tile = lambda cb: pl.BlockSpec((1, tr, D_MODEL), lambda b, j: (b, j, cb))
    halo = lambda cb: pl.BlockSpec((1, CONV_HALO, D_MODEL), lambda b, j: (b, jnp.maximum(j * hb - 1, 0), cb))
    nxt = pl.BlockSpec((1, CONV_HALO, D_MODEL), lambda b, j: (b, jnp.minimum((j + 1) * hb, S // CONV_HALO - 1), 0))
    par = lambda r: pl.BlockSpec((r, D_MODEL), lambda b, j: (0, 0))
    p3 = proj.reshape(Bl, S, 2 * D_MODEL)
    d3 = du1.reshape(Bl, S, D_MODEL)
    dab, ddw, ddb = pl.pallas_call(
        body, name="conv_bwd", grid=(Bl, nj),
        in_specs=[tile(0), tile(1), halo(0), halo(1), tile(0), nxt, par(32)],
        out_specs=[pl.BlockSpec((1, tr, 2 * D_MODEL), lambda b, j: (b, j, 0)), par(32), par(1)],
        out_shape=[_sds((Bl, S, 2 * D_MODEL), BF16), _sds((32, D_MODEL), F32), _sds((1, D_MODEL), F32)],
        scratch_shapes=[pltpu.VMEM((tr + CONV_HALO, D_MODEL), F32), pltpu.VMEM((tr + CONV_HALO, D_MODEL), F32),
                        pltpu.VMEM((tr, D_MODEL), F32)],
        compiler_params=_params(("arbitrary", "arbitrary")),
    )(p3, p3, p3, p3, d3, d3, dw)
    return dab.reshape(Bl * S, 2 * D_MODEL), ddw, ddb


_LAYOUT = (
    ("pre_norm_g", (4, 1024), None), ("post_norm_g", (4, 1024), None),
    ("attn_w_in", (2, 1024, 2560), 2), ("attn_w_out", (2, 256, 1024), 1),
    ("conv_w_in", (2, 1024, 768), 2), ("conv_dw_w", (2, 31, 256), 2),
    ("conv_dw_b", (2, 256), 1), ("conv_ln_g", (2, 256), 1), ("conv_ln_b", (2, 256), 1),
    ("conv_w_out", (2, 256, 1024), 1), ("ple_w_proj", (4, 256, 256), 2), ("ple_w_gate", (4, 256, 1024), 1),
)
_MATMUL_WEIGHTS = ("attn_w_in", "attn_w_out", "conv_w_in", "conv_w_out", "ple_w_proj", "ple_w_gate")
_SMALL_WEIGHTS = ("conv_dw_w", "conv_dw_b", "conv_ln_g", "conv_ln_b")
_SHAPE = {n: s for n, s, _ in _LAYOUT}
_AXIS = {n: a for n, _, a in _LAYOUT}


def _size(shape):
    n = 1
    for s in shape:
        n *= s
    return n


def _padded_rows(shape):
    rows = _size(shape) // shape[-1]
    return rows + (-rows) % FLAT_ROW_ALIGN


def _col_blocks(a):
    a2 = a.reshape(-1, a.shape[-1])
    a2 = jnp.pad(a2, ((0, _padded_rows(a.shape) - a2.shape[0]), (0, 0)))
    return jnp.concatenate([a2[:, c:c + FLAT_COLS] for c in range(0, a2.shape[1], FLAT_COLS)], axis=0)


def _from_col_blocks(flat, off, shape):
    rows, nblk = _padded_rows(shape), shape[-1] // FLAT_COLS
    a2 = jnp.concatenate([flat[off + b * rows:off + (b + 1) * rows] for b in range(nblk)], axis=1)
    return a2[:_size(shape) // shape[-1]].reshape(shape), off + nblk * rows


def _pack_rows(pieces):
    flat = jnp.concatenate(pieces, axis=0)
    assert flat.shape[0] <= FLAT_ROWS
    return jnp.pad(flat, ((0, FLAT_ROWS - flat.shape[0]), (0, 0)))


def _pack_f32(params):
    return _pack_rows([_col_blocks(params[n]) for n, _, _ in _LAYOUT])


def _unpack_f32(flat):
    out, off = {}, 0
    for n, shape, _ in _LAYOUT:
        out[n], off = _from_col_blocks(flat, off, shape)
    return out


def _bytes_shape(shape):
    return shape[:-1] + (4 * shape[-1],)


def _f32_to_bytes(a):
    u = lax.bitcast_convert_type(a, jnp.uint32)
    parts = jnp.stack([(u >> s) & 0xFF for s in (0, 8, 16, 24)], axis=-1)
    return parts.astype(F32).astype(BF16).reshape(_bytes_shape(a.shape))


def _bytes_to_f32(b, shape):
    u = b.reshape(shape + (4,)).astype(F32).astype(jnp.uint32)
    return lax.bitcast_convert_type(u[..., 0] | (u[..., 1] << 8) | (u[..., 2] << 16) | (u[..., 3] << 24), F32)


def _pack_gather_payload(w):
    pieces = [_col_blocks(w[n].astype(BF16)) for n in _MATMUL_WEIGHTS]
    pieces += [_col_blocks(_f32_to_bytes(w[n])) for n in _SMALL_WEIGHTS]
    return _pack_rows(pieces)


def _unpack_gathered(g):
    out, off = {}, 0
    for n in _MATMUL_WEIGHTS:
        parts = [_from_col_blocks(g[s], off, _SHAPE[n]) for s in range(N_CHIPS)]
        out[n] = jnp.concatenate([part for part, _ in parts], axis=_AXIS[n])
        off = parts[0][1]
    for n in _SMALL_WEIGHTS:
        parts = [_from_col_blocks(g[s], off, _bytes_shape(_SHAPE[n])) for s in range(N_CHIPS)]
        out[n] = jnp.concatenate([_bytes_to_f32(part, _SHAPE[n]) for part, _ in parts], axis=_AXIS[n])
        off = parts[0][1]
    return out


def _pack_full_grads(grads):
    rows = []
    for s in range(N_CHIPS):
        pieces = []
        for n, shape, axis in _LAYOUT:
            gfull = grads[n]
            if axis is not None:
                gfull = lax.slice_in_dim(gfull, s * shape[axis], (s + 1) * shape[axis], axis=axis)
            pieces.append(_col_blocks(gfull))
        rows.append(_pack_rows(pieces))
    return jnp.stack(rows)


_ANY = pl.BlockSpec(memory_space=pl.ANY)


def _mesh_pos():
    return lax.axis_index("x"), lax.axis_index("y"), lax.axis_index("c")


def _other_chips(x, y):
    return [(1 - x, y), (x, 1 - y), (1 - x, 1 - y)]


def _allgather_weights(wl):
    R, C = wl.shape
    H = R // 2

    def body(w_ref, out_ref, send_sems, recv_sems):
        x, y, c = _mesh_pos()
        me, sibling = (x, y, c), (x, y, 1 - c)
        chips = _other_chips(x, y)

        def half(px, py, pc):
            return out_ref.at[2 * px + py, pl.ds(pc * H, H), :]

        def copy(k, block, to, src=None):
            return pltpu.make_async_remote_copy(
                src_ref=half(*block) if src is None else src, dst_ref=half(*block),
                send_sem=send_sems.at[k], recv_sem=recv_sems.at[k], device_id=to, device_id_type=MESH)

        own = pltpu.make_async_remote_copy(
            src_ref=w_ref, dst_ref=out_ref.at[2 * x + y], send_sem=send_sems.at[6], recv_sem=recv_sems.at[6],
            device_id=sibling, device_id_type=MESH)
        own.start()
        first = [copy(j, me, (*chip, c), src=w_ref.at[pl.ds(c * H, H), :]) for j, chip in enumerate(chips)]
        for cp in first:
            cp.start()
        passed = [copy(3 + j, (*chip, c), sibling) for j, chip in enumerate(chips)]
        for j, chip in enumerate(chips):
            copy(j, (*chip, c), me).wait_recv()
            passed[j].start()
        for j, chip in enumerate(chips):
            copy(3 + j, (*chip, 1 - c), me).wait_recv()
        own.wait()
        for cp in first + passed:
            cp.wait_send()

    return pl.pallas_call(
        body, name="allgather_weights", in_specs=[_ANY], out_specs=_ANY,
        out_shape=_sds((N_CHIPS, R, C), wl.dtype),
        scratch_shapes=[pltpu.SemaphoreType.DMA((7,)), pltpu.SemaphoreType.DMA((7,))],
    )(wl)


def _exchange_core_halves(g):
    n, _, H, C = g.shape

    def body(g_ref, got_ref, send_sem, recv_sem):
        x, y, c = _mesh_pos()
        swap = pltpu.make_async_remote_copy(
            src_ref=g_ref.at[pl.ds(0, n), 1 - c], dst_ref=got_ref, send_sem=send_sem, recv_sem=recv_sem,
            device_id=(x, y, 1 - c), device_id_type=MESH)
        swap.start()
        swap.wait()

    return pl.pallas_call(
        body, name="exchange_core_halves", in_specs=[_ANY], out_specs=_ANY,
        out_shape=_sds((n, H, C), g.dtype),
        scratch_shapes=[pltpu.SemaphoreType.DMA, pltpu.SemaphoreType.DMA],
    )(g)


def _scatter_to_chips(p):
    n, H, C = p.shape

    def body(p_ref, q_ref, send_sems, recv_sems):
        x, y, c = _mesh_pos()
        chips = _other_chips(x, y)
        sends = [pltpu.make_async_remote_copy(
            src_ref=p_ref.at[2 * cx + cy], dst_ref=q_ref.at[j], send_sem=send_sems.at[j],
            recv_sem=recv_sems.at[j], device_id=(cx, cy, c), device_id_type=MESH)
            for j, (cx, cy) in enumerate(chips)]
        for cp in sends:
            cp.start()
        for cp in sends:
            cp.wait_recv()
        for cp in sends:
            cp.wait_send()

    return pl.pallas_call(
        body, name="scatter_to_chips", in_specs=[_ANY], out_specs=_ANY,
        out_shape=_sds((n - 1, H, C), p.dtype),
        scratch_shapes=[pltpu.SemaphoreType.DMA((3,)), pltpu.SemaphoreType.DMA((3,))],
    )(p)


def _share_core_halves(r2):
    _, H, C = r2.shape

    def body(r_ref, out_ref, send_sem, recv_sem):
        x, y, c = _mesh_pos()
        send = pltpu.make_async_remote_copy(
            src_ref=r_ref.at[c], dst_ref=out_ref.at[c], send_sem=send_sem, recv_sem=recv_sem,
            device_id=(x, y, 1 - c), device_id_type=MESH)
        send.start()
        send.wait_send()
        pltpu.make_async_remote_copy(
            src_ref=r_ref.at[c], dst_ref=out_ref.at[1 - c], send_sem=send_sem, recv_sem=recv_sem,
            device_id=(x, y, 1 - c), device_id_type=MESH).wait_recv()

    return pl.pallas_call(
        body, name="share_core_halves", in_specs=[_ANY], out_specs=_ANY,
        out_shape=_sds(r2.shape, r2.dtype), input_output_aliases={0: 0},
        scratch_shapes=[pltpu.SemaphoreType.DMA, pltpu.SemaphoreType.DMA],
    )(r2)


def _place():
    x, y, c = _mesh_pos()
    return jnp.stack([c, 2 * x + y]).astype(jnp.int32)


def _sum_pair(g, got, place):
    n, _, H, C = g.shape

    def body(place_ref, a_ref, b_ref, o_ref):
        o_ref[...] = a_ref[...] + b_ref[...]

    spec = pl.BlockSpec((1, FLAT_TILE, C), lambda s, i, pr: (s, i, 0))
    return pl.pallas_call(
        body, name="sum_core_pair",
        grid_spec=pltpu.PrefetchScalarGridSpec(
            num_scalar_prefetch=1, grid=(n, H // FLAT_TILE),
            in_specs=[pl.BlockSpec((1, None, FLAT_TILE, C), lambda s, i, pr: (s, pr[0], i, 0)), spec],
            out_specs=spec),
        out_shape=_sds((n, H, C), g.dtype),
        compiler_params=_params(("parallel", "parallel")))(place, g, got)


def _sum_chips(p, q, place):
    n, H, C = p.shape

    def body(place_ref, own_ref, qx_ref, qy_ref, qxy_ref, o_ref):
        mine = place_ref[1]
        own, qx, qy, qxy = own_ref[0], qx_ref[0], qy_ref[0], qxy_ref[0]

        def term(s):
            rel = jnp.full(own.shape, mine ^ s, jnp.int32)
            return jnp.where(rel == 0, own, jnp.where(rel == 2, qx, jnp.where(rel == 1, qy, qxy)))

        o_ref[0] = ((term(0) + term(1)) + term(2)) + term(3)

    qspec = lambda j: pl.BlockSpec((1, FLAT_TILE, C), lambda i, pr: (j, i, 0))
    return pl.pallas_call(
        body, name="sum_chips",
        grid_spec=pltpu.PrefetchScalarGridSpec(
            num_scalar_prefetch=1, grid=(H // FLAT_TILE,),
            in_specs=[pl.BlockSpec((1, FLAT_TILE, C), lambda i, pr: (pr[1], i, 0)), qspec(0), qspec(1), qspec(2)],
            out_specs=pl.BlockSpec((1, FLAT_TILE, C), lambda i, pr: (pr[0], i, 0))),
        out_shape=_sds((2, H, C), p.dtype),
        compiler_params=_params(("parallel",)))(place, p, q, q, q)


def _adamw(w, g, m, v):
    R, C = w.shape

    def body(w_ref, g_ref, m_ref, v_ref, d_ref, nm_ref, nv_ref):
        gv = g_ref[...]
        nm = ADAM_B1 * m_ref[...] + (1.0 - ADAM_B1) * gv
        nv = ADAM_B2 * v_ref[...] + (1.0 - ADAM_B2) * (gv * gv)
        m_hat = nm / (1.0 - ADAM_B1 ** ADAM_STEP)
        v_hat = nv / (1.0 - ADAM_B2 ** ADAM_STEP)
        d_ref[...] = -ADAM_LR * (m_hat / (jnp.sqrt(v_hat) + ADAM_EPS) + ADAM_WD * w_ref[...])
        nm_ref[...] = nm
        nv_ref[...] = nv

    spec = pl.BlockSpec((FLAT_TILE, C), lambda i: (i, 0))
    return pl.pallas_call(body, name="adamw", grid=(R // FLAT_TILE,), in_specs=[spec] * 4, out_specs=[spec] * 3,
                          out_shape=[_sds((R, C), F32)] * 3, compiler_params=_params(("parallel",)))(w, g, m, v)


def _reduce_scatter_grads(gfull):
    n, R, C = gfull.shape
    place = _place()
    g4 = gfull.reshape(n, 2, R // 2, C)
    p = _sum_pair(g4, _exchange_core_halves(g4), place)
    return _share_core_halves(_sum_chips(p, _scatter_to_chips(p), place)).reshape(R, C)


def _local_step(x, p, positions, loss_target, pre_g, post_g, w):
    Bl, S, _ = x.shape
    T = Bl * S
    tab = _rope_table(positions)
    xs = x.reshape(T, D_MODEL)
    saved = []
    for i in range(DEPTH):
        j = i // 2
        g_pre, g_post = pre_g[i:i + 1], post_g[i:i + 1]
        h = _rmsnorm_fwd(xs, g_pre)
        st = {"x": xs, "h": h}
        if i % 2 == 0:
            proj = _mm(h, w["attn_w_in"][j], name="attn_in")
            res = [_attn_fwd(proj, tab, g, Bl, S) for g in range(N_GROUPS)]
            a, o, lse = _attn_combine([r[0] for r in res], [r[1] for r in res], proj)
            y = _mm(a, w["attn_w_out"][j], name="attn_out")
            st.update(proj=proj, a=a, o=o, lse=lse)
        else:
            w_ab, w_z = w["conv_w_in"][j][:, :2 * D_MODEL], w["conv_w_in"][j][:, 2 * D_MODEL:]
            ab = _mm(h, w_ab, out_dtype=BF16, name="conv_in_ab")
            z = _mm(h, w_z, out_dtype=BF16, name="conv_in_z")
            dw = jnp.pad(w["conv_dw_w"][j], ((0, 1), (0, 0)))
            u1, a = _conv_fwd(ab, z, dw, w["conv_dw_b"][j:j + 1], w["conv_ln_g"][j:j + 1],
                              w["conv_ln_b"][j:j + 1], Bl, S)
            y = _mm(a, w["conv_w_out"][j], name="conv_out")
            st.update(w_ab=w_ab, w_z=w_z, ab=ab, z=z, dw=dw, u1=u1, a=a)
        x1 = _post_fwd(xs, y, g_post)
        pi = p[i].reshape(T, PLE_DIM)
        pe = _mm(pi, w["ple_w_proj"][i], name="ple_proj")
        gl = _mm(x1, w["ple_w_gate"][i], name="ple_gate")
        xs = _ple_fwd(x1, pe, gl)
        st.update(y=y, x1=x1, pi=pi, pe=pe, gl=gl)
        saved.append(st)

    sq, dx = _loss_fwd_bwd(xs, loss_target.reshape(T, D_MODEL))

    grads = {n: [None] * shape[0] for n, shape, _ in _LAYOUT}
    for i in reversed(range(DEPTH)):
        j = i // 2
        st = saved[i]
        g_pre, g_post = pre_g[i:i + 1], post_g[i:i + 1]
        dpe, dgl = _ple_bwd(dx, st["pe"], st["gl"])
        grads["ple_w_proj"][i] = _mm(st["pi"], dpe, ta=True, name="ple_proj_wgrad")
        grads["ple_w_gate"][i] = _mm(st["x1"], dgl, ta=True, tk=512, name="ple_gate_wgrad")
        dx1 = _mm(dgl, w["ple_w_gate"][i], tb=True, add=dx, name="ple_gate_dgrad")
        dy, dg_post = _rmsnorm_bwd(dx1, st["y"], g_post, None, BF16, "post_bwd")
        grads["post_norm_g"][i] = dg_post[0]
        if i % 2 == 0:
            grads["attn_w_out"][j] = _mm(st["a"], dy, ta=True, tk=512, name="attn_out_wgrad")
            da = _mm(dy, w["attn_w_out"][j], tb=True, name="attn_out_dgrad")
            do, dproj = _gate_bwd(da, st["o"], st["proj"])
            for g in range(N_GROUPS):
                dproj = _attn_bwd(st["proj"], tab, do, st["o"], st["lse"], dproj, g, Bl, S)
            dh = _mm(dproj, w["attn_w_in"][j], tb=True, name="attn_in_dgrad")
            grads["attn_w_in"][j] = _mm(st["h"], dproj, ta=True, tk=512, name="attn_in_wgrad")
        else:
            grads["conv_w_out"][j] = _mm(st["a"], dy, ta=True, tk=512, name="conv_out_wgrad")
            da2 = _mm(dy, w["conv_w_out"][j], tb=True, name="conv_out_dgrad")
            du1, dz, dln_g, dln_b = _conv_norm_bwd(da2, st["z"], st["u1"], w["conv_ln_g"][j:j + 1],
                                                   w["conv_ln_b"][j:j + 1])
            dab, ddw, ddb = _conv_bwd(st["ab"], du1, st["dw"], Bl, S)
            dh = _mm(dz, st["w_z"], tb=True, name="conv_in_z_dgrad")
            dh = _mm(dab, st["w_ab"], tb=True, add=dh, name="conv_in_ab_dgrad")
            dw_ab = _mm(st["h"], dab, ta=True, tk=512, name="conv_in_ab_wgrad")
            dw_z = _mm(st["h"], dz, ta=True, tk=512, name="conv_in_z_wgrad")
            grads["conv_w_in"][j] = jnp.concatenate([dw_ab, dw_z], axis=1)
            grads["conv_dw_w"][j] = ddw[:CONV_WIDTH]
            grads["conv_dw_b"][j] = ddb[0]
            grads["conv_ln_g"][j] = dln_g[0]
            grads["conv_ln_b"][j] = dln_b[0]
        dx, dg_pre = _rmsnorm_bwd(dh, st["x"], g_pre, dx1, F32, "pre_bwd")
        grads["pre_norm_g"][i] = dg_pre[0]
    grads = {n: jnp.stack(v) for n, v in grads.items()}
    return sq, dx.reshape(Bl, S, D_MODEL), grads


_NAMES = tuple(n for n, _, _ in _LAYOUT)


def kernel(x, p, positions, pre_norm_g, post_norm_g, attn_w_in, attn_w_out, conv_w_in, conv_dw_w, conv_dw_b, conv_ln_g, conv_ln_b, conv_w_out, ple_w_proj, ple_w_gate, loss_target, m_pre_norm_g, m_post_norm_g, m_attn_w_in, m_attn_w_out, m_conv_w_in, m_conv_dw_w, m_conv_dw_b, m_conv_ln_g, m_conv_ln_b, m_conv_w_out, m_ple_w_proj, m_ple_w_gate, v_pre_norm_g, v_post_norm_g, v_attn_w_in, v_attn_w_out, v_conv_w_in, v_conv_dw_w, v_conv_dw_b, v_conv_ln_g, v_conv_ln_b, v_conv_w_out, v_ple_w_proj, v_ple_w_gate):
    w_loc = dict(zip(_NAMES, (pre_norm_g, post_norm_g, attn_w_in, attn_w_out, conv_w_in, conv_dw_w, conv_dw_b,
                              conv_ln_g, conv_ln_b, conv_w_out, ple_w_proj, ple_w_gate)))
    m_loc = dict(zip(_NAMES, (m_pre_norm_g, m_post_norm_g, m_attn_w_in, m_attn_w_out, m_conv_w_in, m_conv_dw_w,
                              m_conv_dw_b, m_conv_ln_g, m_conv_ln_b, m_conv_w_out, m_ple_w_proj, m_ple_w_gate)))
    v_loc = dict(zip(_NAMES, (v_pre_norm_g, v_post_norm_g, v_attn_w_in, v_attn_w_out, v_conv_w_in, v_conv_dw_w,
                              v_conv_dw_b, v_conv_ln_g, v_conv_ln_b, v_conv_w_out, v_ple_w_proj, v_ple_w_gate)))

    w_full = _unpack_gathered(_allgather_weights(_pack_gather_payload(w_loc)))
    sq, grad_x, grads = _local_step(x, p, positions, loss_target, pre_norm_g, post_norm_g, w_full)
    loss = lax.psum(sq[0, 0] * (0.5 / D_MODEL), ("x", "y", "c"))

    g_flat = _reduce_scatter_grads(_pack_full_grads(grads))
    delta, new_m, new_v = _adamw(_pack_f32(w_loc), g_flat, _pack_f32(m_loc), _pack_f32(v_loc))
    g_out, d_out, m_out, v_out = (_unpack_f32(t) for t in (g_flat, delta, new_m, new_v))
    return (loss, grad_x, *[g_out[n] for n in _NAMES], *[d_out[n] for n in _NAMES],
            *[m_out[n] for n in _NAMES], *[v_out[n] for n in _NAMES])
```

```python
import jax
import jax.numpy as jnp
from jax import lax
from jax.experimental import pallas as pl
from jax.experimental.pallas import tpu as pltpu

F32 = jnp.float32
BF16 = jnp.bfloat16

D_MODEL = 1024
DEPTH = 4
PLE_DIM = 256
HEAD_DIM = 64
WIN_DIL = ((128, 1), (512, 4), (2048, 16))
N_GROUPS = 3
N_BACK = 128
ROPE_THETA = 10000.0
CONV_WIDTH = 31
CONV_HALO = 32
RMS_EPS = 1e-6
LN_EPS = 1e-5
NEG_INF = -1e30
ADAM_LR, ADAM_B1, ADAM_B2, ADAM_EPS, ADAM_WD, ADAM_STEP = 0.001, 0.9, 0.999, 1e-08, 0.01, 10

LANES = 128
N_CHIPS = 4
VMEM_LIMIT = 48 * 1024 * 1024
VMEM_LIMIT_ATTN = 56 * 1024 * 1024
FLAT_COLS = 256
FLAT_ROWS = 36864
FLAT_TILE = 2048
FLAT_ROW_ALIGN = 16
PROJ_COLS = (3 * N_GROUPS + 1) * D_MODEL
HEAD_PAIRS = D_MODEL // LANES

MESH = pl.DeviceIdType.MESH


def _params(sem=None, vmem=VMEM_LIMIT):
    return pltpu.CompilerParams(dimension_semantics=sem, vmem_limit_bytes=vmem)


def _sigmoid(v):
    return 1.0 / (1.0 + jnp.exp(-v))


def _mm(a, b, *, ta=False, tb=False, add=None, out_dtype=F32, tm=1024, tn=1024, tk=1024, name="mm"):
    if ta:
        K, M = a.shape
    else:
        M, K = a.shape
    if tb:
        N, K2 = b.shape
    else:
        K2, N = b.shape
    assert K == K2, (a.shape, b.shape)
    tm, tn, tk = min(tm, M), min(tn, N), min(tk, K)
    assert M % tm == 0 and N % tn == 0 and K % tk == 0
    nk = K // tk
    dims = (((0 if ta else 1,), (1 if tb else 0,)), ((), ()))

    def body(*refs):
        if add is None:
            a_ref, b_ref, o_ref = refs[:3]
        else:
            a_ref, b_ref, add_ref, o_ref = refs[:4]
        k = pl.program_id(2)
        part = lax.dot_general(a_ref[...].astype(BF16), b_ref[...].astype(BF16), dims, preferred_element_type=F32)

        def finish(r):
            if add is not None:
                r = r + add_ref[...].astype(F32)
            o_ref[...] = r.astype(out_dtype)

        if nk == 1:
            finish(part)
        else:
            acc_ref = refs[-1]

            @pl.when(k == 0)
            def _():
                acc_ref[...] = part

            @pl.when((k > 0) & (k < nk - 1))
            def _():
                acc_ref[...] += part

            @pl.when(k == nk - 1)
            def _():
                finish(acc_ref[...] + part)

    a_spec = pl.BlockSpec((tk, tm), lambda i, j, k: (k, i)) if ta else pl.BlockSpec((tm, tk), lambda i, j, k: (i, k))
    b_spec = pl.BlockSpec((tn, tk), lambda i, j, k: (j, k)) if tb else pl.BlockSpec((tk, tn), lambda i, j, k: (k, j))
    o_spec = pl.BlockSpec((tm, tn), lambda i, j, k: (i, j))
    in_specs, args = [a_spec, b_spec], [a, b]
    if add is not None:
        in_specs.append(o_spec)
        args.append(add)
    return pl.pallas_call(
        body, name=name, grid=(M // tm, N // tn, nk),
        in_specs=in_specs, out_specs=o_spec,
        out_shape=jax.ShapeDtypeStruct((M, N), out_dtype),
        scratch_shapes=[pltpu.VMEM((tm, tn), F32)] if nk > 1 else [],
        compiler_params=_params(("parallel", "parallel", "arbitrary")),
    )(*args)


ROW_TILE = 512


def _rows(w=D_MODEL, cb=0, tr=ROW_TILE):
    return pl.BlockSpec((tr, w), lambda i: (i, cb))


def _full(shape):
    return pl.BlockSpec(shape, lambda i: (0,) * len(shape))


def _row_call(body, name, T, in_specs, out_specs, out_shape, args, tr=ROW_TILE):
    return pl.pallas_call(body, name=name, grid=(T // tr,), in_specs=in_specs, out_specs=out_specs,
                          out_shape=out_shape, compiler_params=_params(("arbitrary",)))(*args)


def _sds(shape, dtype):
    return jax.ShapeDtypeStruct(shape, dtype)


def _rmsnorm_fwd(x, g):
    T = x.shape[0]

    def body(x_ref, g_ref, h_ref):
        xv = x_ref[...]
        r = lax.rsqrt(jnp.mean(xv * xv, axis=1, keepdims=True) + RMS_EPS)
        h_ref[...] = (xv * r * g_ref[...]).astype(BF16)

    return _row_call(body, "rmsnorm_fwd", T, [_rows(), _full((1, D_MODEL))], _rows(),
                     _sds((T, D_MODEL), BF16), (x, g))


def _post_fwd(x, y, g):
    T = x.shape[0]

    def body(x_ref, y_ref, g_ref, o_ref):
        yv = y_ref[...]
        r = lax.rsqrt(jnp.mean(yv * yv, axis=1, keepdims=True) + RMS_EPS)
        o_ref[...] = x_ref[...] + yv * r * g_ref[...]

    return _row_call(body, "post_fwd", T, [_rows(), _rows(), _full((1, D_MODEL))], _rows(),
                     _sds((T, D_MODEL), F32), (x, y, g))


def _rmsnorm_bwd(dout, xin, g, add, out_dtype, name):
    T = xin.shape[0]

    def body(*refs):
        if add is None:
            d_ref, x_ref, g_ref, dx_ref, dg_ref = refs
        else:
            d_ref, x_ref, g_ref, add_ref, dx_ref, dg_ref = refs

        @pl.when(pl.program_id(0) == 0)
        def _():
            dg_ref[...] = jnp.zeros_like(dg_ref)

        xv = x_ref[...]
        dv = d_ref[...].astype(F32)
        r = lax.rsqrt(jnp.mean(xv * xv, axis=1, keepdims=True) + RMS_EPS)
        xh = xv * r
        dg_ref[...] += jnp.sum(dv * xh, axis=0, keepdims=True)
        dn = dv * g_ref[...]
        dx = r * (dn - xh * jnp.mean(dn * xh, axis=1, keepdims=True))
        if add is not None:
            dx = dx + add_ref[...]
        dx_ref[...] = dx.astype(out_dtype)

    in_specs = [_rows(), _rows(), _full((1, D_MODEL))]
    args = [dout, xin, g]
    if add is not None:
        in_specs.append(_rows())
        args.append(add)
    return _row_call(body, name, T, in_specs, [_rows(), _full((1, D_MODEL))],
                     [_sds((T, D_MODEL), out_dtype), _sds((1, D_MODEL), F32)], args)


def _ple_fwd(x1, pe, gl):
    T = x1.shape[0]

    def body(x_ref, pe_ref, gl_ref, o_ref):
        o_ref[...] = x_ref[...] + pe_ref[...] * _sigmoid(gl_ref[...])

    return _row_call(body, "ple_fwd", T, [_rows()] * 3, _rows(), _sds((T, D_MODEL), F32), (x1, pe, gl))


def _ple_bwd(dx2, pe, gl):
    T = dx2.shape[0]

    def body(d_ref, pe_ref, gl_ref, dpe_ref, dgl_ref):
        dv = d_ref[...]
        sg = _sigmoid(gl_ref[...])
        dpe_ref[...] = (dv * sg).astype(BF16)
        dgl_ref[...] = (dv * pe_ref[...] * sg * (1.0 - sg)).astype(BF16)

    return _row_call(body, "ple_bwd", T, [_rows()] * 3, [_rows()] * 2,
                     [_sds((T, D_MODEL), BF16)] * 2, (dx2, pe, gl))


def _loss_fwd_bwd(y, target):
    T = y.shape[0]

    def body(y_ref, t_ref, s_ref, d_ref):
        @pl.when(pl.program_id(0) == 0)
        def _():
            s_ref[...] = jnp.zeros_like(s_ref)

        e = y_ref[...] - t_ref[...]
        s_ref[...] += jnp.sum(e * e).reshape(1, 1)
        d_ref[...] = e * (1.0 / D_MODEL)

    return _row_call(body, "loss", T, [_rows()] * 2, [_full((1, 1)), _rows()],
                     [_sds((1, 1), F32), _sds((T, D_MODEL), F32)], (y, target))


def _attn_combine(outs, lses, proj):
    T = proj.shape[0]

    def body(o0, o1, o2, l0, l1, l2, z_ref, a_ref, o_ref, lse_ref):
        a0, a1, a2 = l0[...], l1[...], l2[...]
        m = jnp.maximum(jnp.maximum(a0, a1), a2)
        e0, e1, e2 = jnp.exp(a0 - m), jnp.exp(a1 - m), jnp.exp(a2 - m)
        ssum = e0 + e1 + e2
        o = (e0 * o0[...] + e1 * o1[...] + e2 * o2[...]) / ssum
        zv = z_ref[...].astype(F32)
        o_ref[...] = o
        lse_ref[...] = m + jnp.log(ssum)
        a_ref[...] = (o * zv * _sigmoid(zv)).astype(BF16)

    return _row_call(body, "attn_combine", T, [_rows()] * 6 + [_rows(cb=3 * N_GROUPS)], [_rows()] * 3,
                     [_sds((T, D_MODEL), BF16), _sds((T, D_MODEL), F32), _sds((T, D_MODEL), F32)],
                     (*outs, *lses, proj))


def _gate_bwd(da, o, proj):
    T = da.shape[0]

    def body(da_ref, o_ref, z_ref, do_ref, dz_ref):
        dv = da_ref[...]
        zv = z_ref[...]
        sg = _sigmoid(zv)
        do_ref[...] = dv * zv * sg
        dz_ref[...] = dv * o_ref[...] * sg * (1.0 + zv * (1.0 - sg))

    zcols = _rows(cb=3 * N_GROUPS)
    return _row_call(body, "gate_bwd", T, [_rows(), _rows(), zcols], [_rows(), zcols],
                     [_sds((T, D_MODEL), F32), _sds((T, PROJ_COLS), F32)], (da, o, proj))


def _rope_table(positions):
    inv_freq = 1.0 / (ROPE_THETA ** (jnp.arange(0, HEAD_DIM, 2, dtype=F32) / HEAD_DIM))
    ang = positions.astype(F32)[..., None] * inv_freq
    cos, sin = jnp.cos(ang), jnp.sin(ang)
    return jnp.concatenate([cos, cos, -sin, sin], axis=-1)


def _rotate_half_partner(t):
    lane = lax.broadcasted_iota(jnp.int32, t.shape, 1)
    return jnp.where((lane % HEAD_DIM) < HEAD_DIM // 2,
                     pltpu.roll(t, LANES - HEAD_DIM // 2, 1), pltpu.roll(t, HEAD_DIM // 2, 1))


def _mask_bias(first):
    qi = lax.broadcasted_iota(jnp.int32, (N_BACK, 2 * N_BACK), 0)
    kj = lax.broadcasted_iota(jnp.int32, (N_BACK, 2 * N_BACK), 1)
    ok = (kj >= qi) & (kj <= qi + N_BACK)
    if first:
        ok = ok & (kj >= N_BACK)
    return jnp.where(ok, 0.0, NEG_INF).astype(F32)


def _stack_heads(t, head0):
    zero = jnp.zeros_like(t)
    return jnp.concatenate([jnp.where(head0, t, zero), jnp.where(head0, zero, t)], axis=0)


def _unstack_heads(t2, head0):
    return jnp.where(head0, t2[:N_BACK], t2[N_BACK:])


def _block_loop(nb, block):
    first = _mask_bias(True)
    block(0, jnp.concatenate([first, first], axis=0))
    if nb > 1:
        rest = _mask_bias(False)
        bias = jnp.concatenate([rest, rest], axis=0)

        def step(n, carry):
            block(n, bias)
            return carry

        lax.fori_loop(1, nb, step, 0, unroll=2)


_NT = (((1,), (1,)), ((), ()))
_TN = (((0,), (0,)), ((), ()))


def _cos_sin(tab):
    lane = lax.broadcasted_iota(jnp.int32, tab.shape, 1)
    swapped = pltpu.roll(tab, HEAD_DIM, 1)
    low = lane < HEAD_DIM
    return jnp.where(low, tab, swapped), jnp.where(low, swapped, tab)


def _residue_rows(r, i, d):
    start = r + i * (N_BACK * d)
    if d == 1:
        return pl.ds(pl.multiple_of(start, N_BACK), N_BACK)
    return pl.ds(start, N_BACK, stride=d)


def _seq_rows(i):
    return pl.ds(pl.multiple_of(i * N_BACK, N_BACK), N_BACK)


def _rope_residue(q_ref, k_ref, v_ref, tab_ref, qr, kr, vp, r, d, nb):
    zeros = jnp.zeros((N_BACK, LANES), BF16)
    kr[pl.ds(0, N_BACK), :] = zeros
    vp[pl.ds(0, N_BACK), :] = zeros

    def body(i, carry):
        rows = _residue_rows(r, i, d)
        cs, sn = _cos_sin(tab_ref[rows, :])
        q, k = q_ref[rows, :], k_ref[rows, :]
        qr[_seq_rows(i), :] = ((q * cs + _rotate_half_partner(q) * sn) * (HEAD_DIM ** -0.5)).astype(BF16)
        kr[_seq_rows(i + 1), :] = (k * cs + _rotate_half_partner(k) * sn).astype(BF16)
        vp[_seq_rows(i + 1), :] = v_ref[rows, :].astype(BF16)
        return carry

    lax.fori_loop(0, nb, body, 0)


def _qkv_specs(S, group):
    col = lambda which: pl.BlockSpec((None, S, LANES),
                                     lambda b, hp: (b, 0, (which * N_GROUPS + group) * HEAD_PAIRS + hp))
    return [col(0), col(1), col(2), pl.BlockSpec((None, S, LANES), lambda b, hp: (b, 0, 0))]


def _attn_fwd(proj, tab, group, Bl, S):
    d = WIN_DIL[group][1]
    L = S // d
    nb = L // N_BACK
    assert WIN_DIL[group][0] // d == N_BACK and L % N_BACK == 0

    def body(q_ref, k_ref, v_ref, tab_ref, o_ref, lse_ref, qr, kr, vp):
        head0 = lax.broadcasted_iota(jnp.int32, (1, LANES), 1) < HEAD_DIM

        def residue(r, carry):
            _rope_residue(q_ref, k_ref, v_ref, tab_ref, qr, kr, vp, r, d, nb)

            def block(n, bias):
                win = pl.ds(pl.multiple_of(n * N_BACK, N_BACK), 2 * N_BACK)
                q2, kw, vw = _stack_heads(qr[_seq_rows(n), :], head0), kr[win, :], vp[win, :]
                s = lax.dot_general(q2, kw, _NT, preferred_element_type=F32) + bias
                m = jnp.max(s, axis=1, keepdims=True)
                p = jnp.exp(s - m)
                l = jnp.sum(p, axis=1, keepdims=True)
                pv = jnp.dot(p.astype(BF16), vw, preferred_element_type=F32)
                rows = _residue_rows(r, n, d)
                o_ref[rows, :] = _unstack_heads(pv * (1.0 / l), head0)
                lse_ref[rows, :] = _unstack_heads((m + jnp.log(l)) + jnp.zeros((2 * N_BACK, LANES), F32), head0)

            _block_loop(nb, block)
            return carry

        lax.fori_loop(0, d, residue, 0)

    act = pl.BlockSpec((None, S, LANES), lambda b, hp: (b, 0, hp))
    p3 = proj.reshape(Bl, S, PROJ_COLS)
    o, lse = pl.pallas_call(
        body, name="attn_fwd_g%d" % group, grid=(Bl, HEAD_PAIRS),
        in_specs=_qkv_specs(S, group), out_specs=[act, act],
        out_shape=[_sds((Bl, S, D_MODEL), F32)] * 2,
        scratch_shapes=[pltpu.VMEM((L, LANES), BF16), pltpu.VMEM((L + N_BACK, LANES), BF16),
                        pltpu.VMEM((L + N_BACK, LANES), BF16)],
        compiler_params=_params(("parallel", "arbitrary"), VMEM_LIMIT_ATTN),
    )(p3, p3, p3, tab)
    return o.reshape(Bl * S, D_MODEL), lse.reshape(Bl * S, D_MODEL)


def _attn_bwd(proj, tab, do, o, lse, dproj, group, Bl, S):
    d = WIN_DIL[group][1]
    L = S // d
    nb = L // N_BACK

    def body(q_ref, k_ref, v_ref, tab_ref, do_ref, o_ref, lse_ref, dproj_in, dproj_ref,
             qr, kr, vp, dk_acc, dv_acc, dq_s, dk_s, dv_s, sems):
        del dproj_in
        head0 = lax.broadcasted_iota(jnp.int32, (1, LANES), 1) < HEAD_DIM

        def residue(r, carry):
            _rope_residue(q_ref, k_ref, v_ref, tab_ref, qr, kr, vp, r, d, nb)
            dk_acc[...] = jnp.zeros_like(dk_acc)
            dv_acc[...] = jnp.zeros_like(dv_acc)

            def block(n, bias):
                win = pl.ds(pl.multiple_of(n * N_BACK, N_BACK), 2 * N_BACK)
                rows = _residue_rows(r, n, d)
                q2, kw, vw = _stack_heads(qr[_seq_rows(n), :], head0), kr[win, :], vp[win, :]
                dof = do_ref[rows, :]
                do2 = _stack_heads(dof.astype(BF16), head0)
                lse_b = lse_ref[rows, :]
                lse2 = jnp.concatenate([lse_b[:, 0:1], lse_b[:, HEAD_DIM:HEAD_DIM + 1]], axis=0)
                dsum = _stack_heads(dof * o_ref[rows, :], head0)
                delta = jnp.sum(dsum, axis=1, keepdims=True)
                s = lax.dot_general(q2, kw, _NT, preferred_element_type=F32) + bias
                p = jnp.exp(s - lse2)
                dp = lax.dot_general(do2, vw, _NT, preferred_element_type=F32)
                ds = (p * (dp - delta)).astype(BF16)
                dq = _unstack_heads(jnp.dot(ds, kw, preferred_element_type=F32), head0) * (HEAD_DIM ** -0.5)
                cs, sn = _cos_sin(tab_ref[rows, :])
                dq_s[rows, :] = dq * cs + _rotate_half_partner(dq * sn)
                dk_acc[win, :] += lax.dot_general(ds, q2, _TN, preferred_element_type=F32)
                dv_acc[win, :] += lax.dot_general(p.astype(BF16), do2, _TN, preferred_element_type=F32)

            _block_loop(nb, block)

            def finish(i, carry2):
                rows = _residue_rows(r, i, d)
                cs, sn = _cos_sin(tab_ref[rows, :])
                dk = dk_acc[_seq_rows(i + 1), :]
                dk_s[rows, :] = dk * cs + _rotate_half_partner(dk * sn)
                dv_s[rows, :] = dv_acc[_seq_rows(i + 1), :]
                return carry2

            lax.fori_loop(0, nb, finish, 0)
            return carry

        lax.fori_loop(0, d, residue, 0)
        b, hp = pl.program_id(0), pl.program_id(1)
        copies = []
        for which, src in enumerate((dq_s, dk_s, dv_s)):
            col = ((which * N_GROUPS + group) * HEAD_PAIRS + hp) * LANES
            copies.append(pltpu.make_async_copy(
                src, dproj_ref.at[b, :, pl.ds(pl.multiple_of(col, LANES), LANES)], sems.at[which]))
        for cp in copies:
            cp.start()
        for cp in copies:
            cp.wait()

    act = pl.BlockSpec((None, S, LANES), lambda b, hp: (b, 0, hp))
    p3 = proj.reshape(Bl, S, PROJ_COLS)
    view = lambda t: t.reshape(Bl, S, D_MODEL)
    out = pl.pallas_call(
        body, name="attn_bwd_g%d" % group, grid=(Bl, HEAD_PAIRS),
        in_specs=_qkv_specs(S, group) + [act, act, act, _ANY], out_specs=_ANY,
        out_shape=_sds((Bl, S, PROJ_COLS), F32), input_output_aliases={7: 0},
        scratch_shapes=[pltpu.VMEM((L, LANES), BF16), pltpu.VMEM((L + N_BACK, LANES), BF16),
                        pltpu.VMEM((L + N_BACK, LANES), BF16),
                        pltpu.VMEM((L + N_BACK, LANES), F32), pltpu.VMEM((L + N_BACK, LANES), F32),
                        pltpu.VMEM((S, LANES), F32), pltpu.VMEM((S, LANES), F32), pltpu.VMEM((S, LANES), F32),
                        pltpu.SemaphoreType.DMA((3,))],
        compiler_params=_params(("arbitrary", "arbitrary"), VMEM_LIMIT_ATTN),
    )(p3, p3, p3, tab, view(do), view(o), view(lse), dproj.reshape(Bl, S, PROJ_COLS))
    return out.reshape(Bl * S, PROJ_COLS)


CONV_TILE = 256
CONV_CHUNK = 64


def _conv_fwd(proj, z, dw, dwb, ln_g, ln_b, Bl, S):
    tr = CONV_TILE
    nj = S // tr
    hb = tr // CONV_HALO

    def body(a_ref, b_ref, ah_ref, bh_ref, z_ref, dw_ref, dwb_ref, g_ref, bb_ref, u1_ref, out_ref, ext):
        j = pl.program_id(1)
        halo = ah_ref[0].astype(F32) * _sigmoid(bh_ref[0].astype(F32))
        ext[pl.ds(0, CONV_HALO), :] = jnp.where(j > 0, halo, 0.0)
        ext[pl.ds(CONV_HALO, tr), :] = a_ref[0].astype(F32) * _sigmoid(b_ref[0].astype(F32))

        def cols(c, carry):
            cs = pl.ds(pl.multiple_of(c * LANES, LANES), LANES)
            for rc in range(tr // CONV_CHUNK):
                acc = jnp.zeros((CONV_CHUNK, LANES), F32)
                for w in range(CONV_WIDTH):
                    off = rc * CONV_CHUNK + CONV_HALO - (CONV_WIDTH - 1) + w
                    acc = acc + dw_ref[pl.ds(w, 1), cs] * ext[pl.ds(off, CONV_CHUNK), cs]
                u1_ref[0, pl.ds(rc * CONV_CHUNK, CONV_CHUNK), cs] = acc + dwb_ref[:, cs]
            return carry

        lax.fori_loop(0, D_MODEL // LANES, cols, 0)
        u1 = u1_ref[0]
        mu = jnp.mean(u1, axis=1, keepdims=True)
        xc = u1 - mu
        rstd = lax.rsqrt(jnp.mean(xc * xc, axis=1, keepdims=True) + LN_EPS)
        u2 = xc * rstd * g_ref[...] + bb_ref[...]
        zv = z_ref[0].astype(F32)
        out_ref[0] = (u2 * _sigmoid(u2) * zv * _sigmoid(zv)).astype(BF16)

    tile = lambda cb: pl.BlockSpec((1, tr, D_MODEL), lambda b, j: (b, j, cb))
    halo = lambda cb: pl.BlockSpec((1, CONV_HALO, D_MODEL), lambda b, j: (b, jnp.maximum(j * hb - 1, 0), cb))
    par = lambda r: pl.BlockSpec((r, D_MODEL), lambda b, j: (0, 0))
    p3 = proj.reshape(Bl, S, 2 * D_MODEL)
    u1, out = pl.pallas_call(
        body, name="conv_fwd", grid=(Bl, nj),
        in_specs=[tile(0), tile(1), halo(0), halo(1), tile(0), par(32), par(1), par(1), par(1)],
        out_specs=[tile(0), tile(0)],
        out_shape=[_sds((Bl, S, D_MODEL), F32), _sds((Bl, S, D_MODEL), BF16)],
        scratch_shapes=[pltpu.VMEM((tr + CONV_HALO, D_MODEL), F32)],
        compiler_params=_params(("parallel", "arbitrary")),
    )(p3, p3, p3, p3, z.reshape(Bl, S, D_MODEL), dw, dwb, ln_g, ln_b)
    return u1.reshape(Bl * S, D_MODEL), out.reshape(Bl * S, D_MODEL)


def _conv_norm_bwd(da2, z, u1, ln_g, ln_b):
    T = da2.shape[0]

    def body(da_ref, z_ref, u_ref, g_ref, b_ref, du_ref, dz_ref, dg_ref, db_ref):
        @pl.when(pl.program_id(0) == 0)
        def _():
            dg_ref[...] = jnp.zeros_like(dg_ref)
            db_ref[...] = jnp.zeros_like(db_ref)

        u1 = u_ref[...]
        mu = jnp.mean(u1, axis=1, keepdims=True)
        xc = u1 - mu
        rstd = lax.rsqrt(jnp.mean(xc * xc, axis=1, keepdims=True) + LN_EPS)
        nrm = xc * rstd
        u2 = nrm * g_ref[...] + b_ref[...]
        s2 = _sigmoid(u2)
        zv = z_ref[...].astype(F32)
        sz = _sigmoid(zv)
        dv = da_ref[...]
        dz_ref[...] = (dv * u2 * s2 * sz * (1.0 + zv * (1.0 - sz))).astype(BF16)
        du2 = dv * zv * sz * s2 * (1.0 + u2 * (1.0 - s2))
        dg_ref[...] += jnp.sum(du2 * nrm, axis=0, keepdims=True)
        db_ref[...] += jnp.sum(du2, axis=0, keepdims=True)
        dn = du2 * g_ref[...]
        du_ref[...] = rstd * (dn - jnp.mean(dn, axis=1, keepdims=True)
                              - nrm * jnp.mean(dn * nrm, axis=1, keepdims=True))

    return _row_call(body, "conv_norm_bwd", T,
                     [_rows(), _rows(), _rows(), _full((1, D_MODEL)), _full((1, D_MODEL))],
                     [_rows(), _rows(), _full((1, D_MODEL)), _full((1, D_MODEL))],
                     [_sds((T, D_MODEL), F32), _sds((T, D_MODEL), BF16), _sds((1, D_MODEL), F32),
                      _sds((1, D_MODEL), F32)], (da2, z, u1, ln_g, ln_b))


def _conv_bwd(proj, du1, dw, Bl, S):
    tr = CONV_TILE
    nj = S // tr
    hb = tr // CONV_HALO

    def body(a_ref, b_ref, ah_ref, bh_ref, du_ref, duh_ref, dw_ref, dab_ref, ddw_ref, ddb_ref, uext, dext, du0):
        first = (pl.program_id(0) == 0) & (pl.program_id(1) == 0)
        j = pl.program_id(1)

        @pl.when(first)
        def _():
            ddw_ref[...] = jnp.zeros_like(ddw_ref)
            ddb_ref[...] = jnp.zeros_like(ddb_ref)

        halo = ah_ref[0].astype(F32) * _sigmoid(bh_ref[0].astype(F32))
        uext[pl.ds(0, CONV_HALO), :] = jnp.where(j > 0, halo, 0.0)
        av = a_ref[0].astype(F32)
        sb = _sigmoid(b_ref[0].astype(F32))
        uext[pl.ds(CONV_HALO, tr), :] = av * sb
        dext[pl.ds(0, tr), :] = du_ref[0]
        dext[pl.ds(tr, CONV_HALO), :] = jnp.where(j < nj - 1, duh_ref[0], 0.0)
        ddb_ref[...] += jnp.sum(du_ref[0], axis=0, keepdims=True)

        def cols(c, carry):
            cs = pl.ds(pl.multiple_of(c * LANES, LANES), LANES)
            for rc in range(tr // CONV_CHUNK):
                base = rc * CONV_CHUNK
                dchunk = dext[pl.ds(base, CONV_CHUNK), cs]
                acc = jnp.zeros((CONV_CHUNK, LANES), F32)
                for w in range(CONV_WIDTH):
                    acc = acc + dw_ref[pl.ds(w, 1), cs] * dext[pl.ds(base + CONV_WIDTH - 1 - w, CONV_CHUNK), cs]
                    off = base + CONV_HALO - (CONV_WIDTH - 1) + w
                    ddw_ref[pl.ds(w, 1), cs] += jnp.sum(dchunk * uext[pl.ds(off, CONV_CHUNK), cs],
                                                        axis=0, keepdims=True)
                du0[pl.ds(base, CONV_CHUNK), cs] = acc
            return carry

        lax.fori_loop(0, D_MODEL // LANES, cols, 0)
        g = du0[...]
        dab_ref[0, :, 0:D_MODEL] = (g * sb).astype(BF16)
        dab_ref[0, :, D_MODEL:2 * D_MODEL] = (g * av * sb * (1.0 - sb)).astype(BF16)

    tile = lambda cb: pl.BlockSpec((1, tr, D_MODEL), lambda b, j: (b, j, cb))
    halo = lambda cb: pl.BlockSpec((1, CONV_HALO, D_MODEL), lambda b, j: (b, jnp.maximum(j * hb - 1, 0), cb))
    nxt = pl.BlockSpec((1, CONV_HALO, D_MODEL), lambda b, j: (b, jnp.minimum((j + 1) * hb, S // CONV_HALO - 1), 0))
    par = lambda r: pl.BlockSpec((r, D_MODEL), lambda b, j: (0, 0))
    p3 = proj.reshape(Bl, S, 2 * D_MODEL)
    d3 = du1.reshape(Bl, S, D_MODEL)
    dab, ddw, ddb = pl.pallas_call(
        body, name="conv_bwd", grid=(Bl, nj),
        in_specs=[tile(0), tile(1), halo(0), halo(1), tile(0), nxt, par(32)],
        out_specs=[pl.BlockSpec((1, tr, 2 * D_MODEL), lambda b, j: (b, j, 0)), par(32), par(1)],
        out_shape=[_sds((Bl, S, 2 * D_MODEL), BF16), _sds((32, D_MODEL), F32), _sds((1, D_MODEL), F32)],
        scratch_shapes=[pltpu.VMEM((tr + CONV_HALO, D_MODEL), F32), pltpu.VMEM((tr + CONV_HALO, D_MODEL), F32),
                        pltpu.VMEM((tr, D_MODEL), F32)],
        compiler_params=_params(("arbitrary", "arbitrary")),
    )(p3, p3, p3, p3, d3, d3, dw)
    return dab.reshape(Bl * S, 2 * D_MODEL), ddw, ddb


_LAYOUT = (
    ("pre_norm_g", (4, 1024), None), ("post_norm_g", (4, 1024), None),
    ("attn_w_in", (2, 1024, 2560), 2), ("attn_w_out", (2, 256, 1024), 1),
    ("conv_w_in", (2, 1024, 768), 2), ("conv_dw_w", (2, 31, 256), 2),
    ("conv_dw_b", (2, 256), 1), ("conv_ln_g", (2, 256), 1), ("conv_ln_b", (2, 256), 1),
    ("conv_w_out", (2, 256, 1024), 1), ("ple_w_proj", (4, 256, 256), 2), ("ple_w_gate", (4, 256, 1024), 1),
)
_MATMUL_WEIGHTS = ("attn_w_in", "attn_w_out", "conv_w_in", "conv_w_out", "ple_w_proj", "ple_w_gate")
_SMALL_WEIGHTS = ("conv_dw_w", "conv_dw_b", "conv_ln_g", "conv_ln_b")
_SHAPE = {n: s for n, s, _ in _LAYOUT}
_AXIS = {n: a for n, _, a in _LAYOUT}


def _size(shape):
    n = 1
    for s in shape:
        n *= s
    return n


def _padded_rows(shape):
    rows = _size(shape) // shape[-1]
    return rows + (-rows) % FLAT_ROW_ALIGN


def _col_blocks(a):
    a2 = a.reshape(-1, a.shape[-1])
    a2 = jnp.pad(a2, ((0, _padded_rows(a.shape) - a2.shape[0]), (0, 0)))
    return jnp.concatenate([a2[:, c:c + FLAT_COLS] for c in range(0, a2.shape[1], FLAT_COLS)], axis=0)


def _from_col_blocks(flat, off, shape):
    rows, nblk = _padded_rows(shape), shape[-1] // FLAT_COLS
    a2 = jnp.concatenate([flat[off + b * rows:off + (b + 1) * rows] for b in range(nblk)], axis=1)
    return a2[:_size(shape) // shape[-1]].reshape(shape), off + nblk * rows


def _pack_rows(pieces):
    flat = jnp.concatenate(pieces, axis=0)
    assert flat.shape[0] <= FLAT_ROWS
    return jnp.pad(flat, ((0, FLAT_ROWS - flat.shape[0]), (0, 0)))


def _pack_f32(params):
    return _pack_rows([_col_blocks(params[n]) for n, _, _ in _LAYOUT])


def _unpack_f32(flat):
    out, off = {}, 0
    for n, shape, _ in _LAYOUT:
        out[n], off = _from_col_blocks(flat, off, shape)
    return out


def _bytes_shape(shape):
    return shape[:-1] + (4 * shape[-1],)


def _f32_to_bytes(a):
    u = lax.bitcast_convert_type(a, jnp.uint32)
    parts = jnp.stack([(u >> s) & 0xFF for s in (0, 8, 16, 24)], axis=-1)
    return parts.astype(F32).astype(BF16).reshape(_bytes_shape(a.shape))


def _bytes_to_f32(b, shape):
    u = b.reshape(shape + (4,)).astype(F32).astype(jnp.uint32)
    return lax.bitcast_convert_type(u[..., 0] | (u[..., 1] << 8) | (u[..., 2] << 16) | (u[..., 3] << 24), F32)


def _pack_gather_payload(w):
    pieces = [_col_blocks(w[n].astype(BF16)) for n in _MATMUL_WEIGHTS]
    pieces += [_col_blocks(_f32_to_bytes(w[n])) for n in _SMALL_WEIGHTS]
    return _pack_rows(pieces)


def _unpack_gathered(g):
    out, off = {}, 0
    for n in _MATMUL_WEIGHTS:
        parts = [_from_col_blocks(g[s], off, _SHAPE[n]) for s in range(N_CHIPS)]
        out[n] = jnp.concatenate([part for part, _ in parts], axis=_AXIS[n])
        off = parts[0][1]
    for n in _SMALL_WEIGHTS:
        parts = [_from_col_blocks(g[s], off, _bytes_shape(_SHAPE[n])) for s in range(N_CHIPS)]
        out[n] = jnp.concatenate([_bytes_to_f32(part, _SHAPE[n]) for part, _ in parts], axis=_AXIS[n])
        off = parts[0][1]
    return out


def _pack_full_grads(grads):
    rows = []
    for s in range(N_CHIPS):
        pieces = []
        for n, shape, axis in _LAYOUT:
            gfull = grads[n]
            if axis is not None:
                gfull = lax.slice_in_dim(gfull, s * shape[axis], (s + 1) * shape[axis], axis=axis)
            pieces.append(_col_blocks(gfull))
        rows.append(_pack_rows(pieces))
    return jnp.stack(rows)


_ANY = pl.BlockSpec(memory_space=pl.ANY)


def _mesh_pos():
    return lax.axis_index("x"), lax.axis_index("y"), lax.axis_index("c")


def _other_chips(x, y):
    return [(1 - x, y), (x, 1 - y), (1 - x, 1 - y)]


def _allgather_weights(wl):
    R, C = wl.shape
    H = R // 2

    def body(w_ref, out_ref, send_sems, recv_sems):
        x, y, c = _mesh_pos()
        me, sibling = (x, y, c), (x, y, 1 - c)
        chips = _other_chips(x, y)

        def half(px, py, pc):
            return out_ref.at[2 * px + py, pl.ds(pc * H, H), :]

        def copy(k, block, to, src=None):
            return pltpu.make_async_remote_copy(
                src_ref=half(*block) if src is None else src, dst_ref=half(*block),
                send_sem=send_sems.at[k], recv_sem=recv_sems.at[k], device_id=to, device_id_type=MESH)

        own = pltpu.make_async_remote_copy(
            src_ref=w_ref, dst_ref=out_ref.at[2 * x + y], send_sem=send_sems.at[6], recv_sem=recv_sems.at[6],
            device_id=sibling, device_id_type=MESH)
        own.start()
        first = [copy(j, me, (*chip, c), src=w_ref.at[pl.ds(c * H, H), :]) for j, chip in enumerate(chips)]
        for cp in first:
            cp.start()
        passed = [copy(3 + j, (*chip, c), sibling) for j, chip in enumerate(chips)]
        for j, chip in enumerate(chips):
            copy(j, (*chip, c), me).wait_recv()
            passed[j].start()
        for j, chip in enumerate(chips):
            copy(3 + j, (*chip, 1 - c), me).wait_recv()
        own.wait()
        for cp in first + passed:
            cp.wait_send()

    return pl.pallas_call(
        body, name="allgather_weights", in_specs=[_ANY], out_specs=_ANY,
        out_shape=_sds((N_CHIPS, R, C), wl.dtype),
        scratch_shapes=[pltpu.SemaphoreType.DMA((7,)), pltpu.SemaphoreType.DMA((7,))],
    )(wl)


def _exchange_core_halves(g):
    n, _, H, C = g.shape

    def body(g_ref, got_ref, send_sem, recv_sem):
        x, y, c = _mesh_pos()
        swap = pltpu.make_async_remote_copy(
            src_ref=g_ref.at[pl.ds(0, n), 1 - c], dst_ref=got_ref, send_sem=send_sem, recv_sem=recv_sem,
            device_id=(x, y, 1 - c), device_id_type=MESH)
        swap.start()
        swap.wait()

    return pl.pallas_call(
        body, name="exchange_core_halves", in_specs=[_ANY], out_specs=_ANY,
        out_shape=_sds((n, H, C), g.dtype),
        scratch_shapes=[pltpu.SemaphoreType.DMA, pltpu.SemaphoreType.DMA],
    )(g)


def _scatter_to_chips(p):
    n, H, C = p.shape

    def body(p_ref, q_ref, send_sems, recv_sems):
        x, y, c = _mesh_pos()
        chips = _other_chips(x, y)
        sends = [pltpu.make_async_remote_copy(
            src_ref=p_ref.at[2 * cx + cy], dst_ref=q_ref.at[j], send_sem=send_sems.at[j],
            recv_sem=recv_sems.at[j], device_id=(cx, cy, c), device_id_type=MESH)
            for j, (cx, cy) in enumerate(chips)]
        for cp in sends:
            cp.start()
        for cp in sends:
            cp.wait_recv()
        for cp in sends:
            cp.wait_send()

    return pl.pallas_call(
        body, name="scatter_to_chips", in_specs=[_ANY], out_specs=_ANY,
        out_shape=_sds((n - 1, H, C), p.dtype),
        scratch_shapes=[pltpu.SemaphoreType.DMA((3,)), pltpu.SemaphoreType.DMA((3,))],
    )(p)


def _share_core_halves(r2):
    _, H, C = r2.shape

    def body(r_ref, out_ref, send_sem, recv_sem):
        x, y, c = _mesh_pos()
        send = pltpu.make_async_remote_copy(
            src_ref=r_ref.at[c], dst_ref=out_ref.at[c], send_sem=send_sem, recv_sem=recv_sem,
            device_id=(x, y, 1 - c), device_id_type=MESH)
        send.start()
        send.wait_send()
        pltpu.make_async_remote_copy(
            src_ref=r_ref.at[c], dst_ref=out_ref.at[1 - c], send_sem=send_sem, recv_sem=recv_sem,
            device_id=(x, y, 1 - c), device_id_type=MESH).wait_recv()

    return pl.pallas_call(
        body, name="share_core_halves", in_specs=[_ANY], out_specs=_ANY,
        out_shape=_sds(r2.shape, r2.dtype), input_output_aliases={0: 0},
        scratch_shapes=[pltpu.SemaphoreType.DMA, pltpu.SemaphoreType.DMA],
    )(r2)


def _place():
    x, y, c = _mesh_pos()
    return jnp.stack([c, 2 * x + y]).astype(jnp.int32)


def _sum_pair(g, got, place):
    n, _, H, C = g.shape

    def body(place_ref, a_ref, b_ref, o_ref):
        o_ref[...] = a_ref[...] + b_ref[...]

    spec = pl.BlockSpec((1, FLAT_TILE, C), lambda s, i, pr: (s, i, 0))
    return pl.pallas_call(
        body, name="sum_core_pair",
        grid_spec=pltpu.PrefetchScalarGridSpec(
            num_scalar_prefetch=1, grid=(n, H // FLAT_TILE),
            in_specs=[pl.BlockSpec((1, None, FLAT_TILE, C), lambda s, i, pr: (s, pr[0], i, 0)), spec],
            out_specs=spec),
        out_shape=_sds((n, H, C), g.dtype),
        compiler_params=_params(("parallel", "parallel")))(place, g, got)


def _sum_chips(p, q, place):
    n, H, C = p.shape

    def body(place_ref, own_ref, qx_ref, qy_ref, qxy_ref, o_ref):
        mine = place_ref[1]
        own, qx, qy, qxy = own_ref[0], qx_ref[0], qy_ref[0], qxy_ref[0]

        def term(s):
            rel = jnp.full(own.shape, mine ^ s, jnp.int32)
            return jnp.where(rel == 0, own, jnp.where(rel == 2, qx, jnp.where(rel == 1, qy, qxy)))

        o_ref[0] = ((term(0) + term(1)) + term(2)) + term(3)

    qspec = lambda j: pl.BlockSpec((1, FLAT_TILE, C), lambda i, pr: (j, i, 0))
    return pl.pallas_call(
        body, name="sum_chips",
        grid_spec=pltpu.PrefetchScalarGridSpec(
            num_scalar_prefetch=1, grid=(H // FLAT_TILE,),
            in_specs=[pl.BlockSpec((1, FLAT_TILE, C), lambda i, pr: (pr[1], i, 0)), qspec(0), qspec(1), qspec(2)],
            out_specs=pl.BlockSpec((1, FLAT_TILE, C), lambda i, pr: (pr[0], i, 0))),
        out_shape=_sds((2, H, C), p.dtype),
        compiler_params=_params(("parallel",)))(place, p, q, q, q)


def _adamw(w, g, m, v):
    R, C = w.shape

    def body(w_ref, g_ref, m_ref, v_ref, d_ref, nm_ref, nv_ref):
        gv = g_ref[...]
        nm = ADAM_B1 * m_ref[...] + (1.0 - ADAM_B1) * gv
        nv = ADAM_B2 * v_ref[...] + (1.0 - ADAM_B2) * (gv * gv)
        m_hat = nm / (1.0 - ADAM_B1 ** ADAM_STEP)
        v_hat = nv / (1.0 - ADAM_B2 ** ADAM_STEP)
        d_ref[...] = -ADAM_LR * (m_hat / (jnp.sqrt(v_hat) + ADAM_EPS) + ADAM_WD * w_ref[...])
        nm_ref[...] = nm
        nv_ref[...] = nv

    spec = pl.BlockSpec((FLAT_TILE, C), lambda i: (i, 0))
    return pl.pallas_call(body, name="adamw", grid=(R // FLAT_TILE,), in_specs=[spec] * 4, out_specs=[spec] * 3,
                          out_shape=[_sds((R, C), F32)] * 3, compiler_params=_params(("parallel",)))(w, g, m, v)


def _reduce_scatter_grads(gfull):
    n, R, C = gfull.shape
    place = _place()
    g4 = gfull.reshape(n, 2, R // 2, C)
    p = _sum_pair(g4, _exchange_core_halves(g4), place)
    return _share_core_halves(_sum_chips(p, _scatter_to_chips(p), place)).reshape(R, C)


def _local_step(x, p, positions, loss_target, pre_g, post_g, w):
    Bl, S, _ = x.shape
    T = Bl * S
    tab = _rope_table(positions)
    xs = x.reshape(T, D_MODEL)
    saved = []
    for i in range(DEPTH):
        j = i // 2
        g_pre, g_post = pre_g[i:i + 1], post_g[i:i + 1]
        h = _rmsnorm_fwd(xs, g_pre)
        st = {"x": xs, "h": h}
        if i % 2 == 0:
            proj = _mm(h, w["attn_w_in"][j], name="attn_in")
            res = [_attn_fwd(proj, tab, g, Bl, S) for g in range(N_GROUPS)]
            a, o, lse = _attn_combine([r[0] for r in res], [r[1] for r in res], proj)
            y = _mm(a, w["attn_w_out"][j], name="attn_out")
            st.update(proj=proj, a=a, o=o, lse=lse)
        else:
            w_ab, w_z = w["conv_w_in"][j][:, :2 * D_MODEL], w["conv_w_in"][j][:, 2 * D_MODEL:]
            ab = _mm(h, w_ab, out_dtype=BF16, name="conv_in_ab")
            z = _mm(h, w_z, out_dtype=BF16, name="conv_in_z")
            dw = jnp.pad(w["conv_dw_w"][j], ((0, 1), (0, 0)))
            u1, a = _conv_fwd(ab, z, dw, w["conv_dw_b"][j:j + 1], w["conv_ln_g"][j:j + 1],
                              w["conv_ln_b"][j:j + 1], Bl, S)
            y = _mm(a, w["conv_w_out"][j], name="conv_out")
            st.update(w_ab=w_ab, w_z=w_z, ab=ab, z=z, dw=dw, u1=u1, a=a)
        x1 = _post_fwd(xs, y, g_post)
        pi = p[i].reshape(T, PLE_DIM)
        pe = _mm(pi, w["ple_w_proj"][i], name="ple_proj")
        gl = _mm(x1, w["ple_w_gate"][i], name="ple_gate")
        xs = _ple_fwd(x1, pe, gl)
        st.update(y=y, x1=x1, pi=pi, pe=pe, gl=gl)
        saved.append(st)

    sq, dx = _loss_fwd_bwd(xs, loss_target.reshape(T, D_MODEL))

    grads = {n: [None] * shape[0] for n, shape, _ in _LAYOUT}
    for i in reversed(range(DEPTH)):
        j = i // 2
        st = saved[i]
        g_pre, g_post = pre_g[i:i + 1], post_g[i:i + 1]
        dpe, dgl = _ple_bwd(dx, st["pe"], st["gl"])
        grads["ple_w_proj"][i] = _mm(st["pi"], dpe, ta=True, name="ple_proj_wgrad")
        grads["ple_w_gate"][i] = _mm(st["x1"], dgl, ta=True, name="ple_gate_wgrad")
        dx1 = _mm(dgl, w["ple_w_gate"][i], tb=True, add=dx, name="ple_gate_dgrad")
        dy, dg_post = _rmsnorm_bwd(dx1, st["y"], g_post, None, BF16, "post_bwd")
        grads["post_norm_g"][i] = dg_post[0]
        if i % 2 == 0:
            grads["attn_w_out"][j] = _mm(st["a"], dy, ta=True, name="attn_out_wgrad")
            da = _mm(dy, w["attn_w_out"][j], tb=True, name="attn_out_dgrad")
            do, dproj = _gate_bwd(da, st["o"], st["proj"])
            for g in range(N_GROUPS):
                dproj = _attn_bwd(st["proj"], tab, do, st["o"], st["lse"], dproj, g, Bl, S)
            dh = _mm(dproj, w["attn_w_in"][j], tb=True, tk=2048, name="attn_in_dgrad")
            grads["attn_w_in"][j] = _mm(st["h"], dproj, ta=True, name="attn_in_wgrad")
        else:
            grads["conv_w_out"][j] = _mm(st["a"], dy, ta=True, name="conv_out_wgrad")
            da2 = _mm(dy, w["conv_w_out"][j], tb=True, name="conv_out_dgrad")
            du1, dz, dln_g, dln_b = _conv_norm_bwd(da2, st["z"], st["u1"], w["conv_ln_g"][j:j + 1],
                                                   w["conv_ln_b"][j:j + 1])
            dab, ddw, ddb = _conv_bwd(st["ab"], du1, st["dw"], Bl, S)
            dh = _mm(dz, st["w_z"], tb=True, name="conv_in_z_dgrad")
            dh = _mm(dab, st["w_ab"], tb=True, add=dh, name="conv_in_ab_dgrad")
            dw_ab = _mm(st["h"], dab, ta=True, name="conv_in_ab_wgrad")
            dw_z = _mm(st["h"], dz, ta=True, name="conv_in_z_wgrad")
            grads["conv_w_in"][j] = jnp.concatenate([dw_ab, dw_z], axis=1)
            grads["conv_dw_w"][j] = ddw[:CONV_WIDTH]
            grads["conv_dw_b"][j] = ddb[0]
            grads["conv_ln_g"][j] = dln_g[0]
            grads["conv_ln_b"][j] = dln_b[0]
        dx, dg_pre = _rmsnorm_bwd(dh, st["x"], g_pre, dx1, F32, "pre_bwd")
        grads["pre_norm_g"][i] = dg_pre[0]
    grads = {n: jnp.stack(v) for n, v in grads.items()}
    return sq, dx.reshape(Bl, S, D_MODEL), grads


_NAMES = tuple(n for n, _, _ in _LAYOUT)


def kernel(x, p, positions, pre_norm_g, post_norm_g, attn_w_in, attn_w_out, conv_w_in, conv_dw_w, conv_dw_b, conv_ln_g, conv_ln_b, conv_w_out, ple_w_proj, ple_w_gate, loss_target, m_pre_norm_g, m_post_norm_g, m_attn_w_in, m_attn_w_out, m_conv_w_in, m_conv_dw_w, m_conv_dw_b, m_conv_ln_g, m_conv_ln_b, m_conv_w_out, m_ple_w_proj, m_ple_w_gate, v_pre_norm_g, v_post_norm_g, v_attn_w_in, v_attn_w_out, v_conv_w_in, v_conv_dw_w, v_conv_dw_b, v_conv_ln_g, v_conv_ln_b, v_conv_w_out, v_ple_w_proj, v_ple_w_gate):
    w_loc = dict(zip(_NAMES, (pre_norm_g, post_norm_g, attn_w_in, attn_w_out, conv_w_in, conv_dw_w, conv_dw_b,
                              conv_ln_g, conv_ln_b, conv_w_out, ple_w_proj, ple_w_gate)))
    m_loc = dict(zip(_NAMES, (m_pre_norm_g, m_post_norm_g, m_attn_w_in, m_attn_w_out, m_conv_w_in, m_conv_dw_w,
                              m_conv_dw_b, m_conv_ln_g, m_conv_ln_b, m_conv_w_out, m_ple_w_proj, m_ple_w_gate)))
    v_loc = dict(zip(_NAMES, (v_pre_norm_g, v_post_norm_g, v_attn_w_in, v_attn_w_out, v_conv_w_in, v_conv_dw_w,
                              v_conv_dw_b, v_conv_ln_g, v_conv_ln_b, v_conv_w_out, v_ple_w_proj, v_ple_w_gate)))

    w_full = _unpack_gathered(_allgather_weights(_pack_gather_payload(w_loc)))
    sq, grad_x, grads = _local_step(x, p, positions, loss_target, pre_norm_g, post_norm_g, w_full)
    loss = lax.psum(sq[0, 0] * (0.5 / D_MODEL), ("x", "y", "c"))

    g_flat = _reduce_scatter_grads(_pack_full_grads(grads))
    delta, new_m, new_v = _adamw(_pack_f32(w_loc), g_flat, _pack_f32(m_loc), _pack_f32(v_loc))
    g_out, d_out, m_out, v_out = (_unpack_f32(t) for t in (g_flat, delta, new_m, new_v))
    return (loss, grad_x, *[g_out[n] for n in _NAMES], *[d_out[n] for n in _NAMES],
            *[m_out[n] for n in _NAMES], *[v_out[n] for n in _NAMES])
```

```python
import jax
import jax.numpy as jnp
from jax import lax
from jax.experimental import pallas as pl
from jax.experimental.pallas import tpu as pltpu

F32 = jnp.float32
BF16 = jnp.bfloat16

D_MODEL = 1024
DEPTH = 4
PLE_DIM = 256
HEAD_DIM = 64
WIN_DIL = ((128, 1), (512, 4), (2048, 16))
N_GROUPS = 3
N_BACK = 128
ROPE_THETA = 10000.0
CONV_WIDTH = 31
CONV_HALO = 32
RMS_EPS = 1e-6
LN_EPS = 1e-5
NEG_INF = -1e30
ADAM_LR, ADAM_B1, ADAM_B2, ADAM_EPS, ADAM_WD, ADAM_STEP = 0.001, 0.9, 0.999, 1e-08, 0.01, 10

LANES = 128
N_CHIPS = 4
VMEM_LIMIT = 48 * 1024 * 1024
VMEM_LIMIT_ATTN = 56 * 1024 * 1024
FLAT_COLS = 256
FLAT_ROWS = 36864
FLAT_TILE = 2048
FLAT_ROW_ALIGN = 16
PROJ_COLS = (3 * N_GROUPS + 1) * D_MODEL
HEAD_PAIRS = D_MODEL // LANES

MESH = pl.DeviceIdType.MESH


def _params(sem=None, vmem=VMEM_LIMIT):
    return pltpu.CompilerParams(dimension_semantics=sem, vmem_limit_bytes=vmem)


def _sigmoid(v):
    return 1.0 / (1.0 + jnp.exp(-v))


def _mm(a, b, *, ta=False, tb=False, add=None, out_dtype=F32, tm=1024, tn=1024, tk=1024, name="mm"):
    if ta:
        K, M = a.shape
    else:
        M, K = a.shape
    if tb:
        N, K2 = b.shape
    else:
        K2, N = b.shape
    assert K == K2, (a.shape, b.shape)
    tm, tn, tk = min(tm, M), min(tn, N), min(tk, K)
    assert M % tm == 0 and N % tn == 0 and K % tk == 0
    nk = K // tk
    dims = (((0 if ta else 1,), (1 if tb else 0,)), ((), ()))

    def body(*refs):
        if add is None:
            a_ref, b_ref, o_ref = refs[:3]
        else:
            a_ref, b_ref, add_ref, o_ref = refs[:4]
        k = pl.program_id(2)
        part = lax.dot_general(a_ref[...].astype(BF16), b_ref[...].astype(BF16), dims, preferred_element_type=F32)

        def finish(r):
            if add is not None:
                r = r + add_ref[...].astype(F32)
            o_ref[...] = r.astype(out_dtype)

        if nk == 1:
            finish(part)
        else:
            acc_ref = refs[-1]

            @pl.when(k == 0)
            def _():
                acc_ref[...] = part

            @pl.when((k > 0) & (k < nk - 1))
            def _():
                acc_ref[...] += part

            @pl.when(k == nk - 1)
            def _():
                finish(acc_ref[...] + part)

    a_spec = pl.BlockSpec((tk, tm), lambda i, j, k: (k, i)) if ta else pl.BlockSpec((tm, tk), lambda i, j, k: (i, k))
    b_spec = pl.BlockSpec((tn, tk), lambda i, j, k: (j, k)) if tb else pl.BlockSpec((tk, tn), lambda i, j, k: (k, j))
    o_spec = pl.BlockSpec((tm, tn), lambda i, j, k: (i, j))
    in_specs, args = [a_spec, b_spec], [a, b]
    if add is not None:
        in_specs.append(o_spec)
        args.append(add)
    return pl.pallas_call(
        body, name=name, grid=(M // tm, N // tn, nk),
        in_specs=in_specs, out_specs=o_spec,
        out_shape=jax.ShapeDtypeStruct((M, N), out_dtype),
        scratch_shapes=[pltpu.VMEM((tm, tn), F32)] if nk > 1 else [],
        compiler_params=_params(("parallel", "parallel", "arbitrary")),
    )(*args)


ROW_TILE = 512


def _rows(w=D_MODEL, cb=0, tr=ROW_TILE):
    return pl.BlockSpec((tr, w), lambda i: (i, cb))


def _full(shape):
    return pl.BlockSpec(shape, lambda i: (0,) * len(shape))


def _row_call(body, name, T, in_specs, out_specs, out_shape, args, tr=ROW_TILE):
    return pl.pallas_call(body, name=name, grid=(T // tr,), in_specs=in_specs, out_specs=out_specs,
                          out_shape=out_shape, compiler_params=_params(("arbitrary",)))(*args)


def _sds(shape, dtype):
    return jax.ShapeDtypeStruct(shape, dtype)


def _rmsnorm_fwd(x, g):
    T = x.shape[0]

    def body(x_ref, g_ref, h_ref):
        xv = x_ref[...]
        r = lax.rsqrt(jnp.mean(xv * xv, axis=1, keepdims=True) + RMS_EPS)
        h_ref[...] = (xv * r * g_ref[...]).astype(BF16)

    return _row_call(body, "rmsnorm_fwd", T, [_rows(), _full((1, D_MODEL))], _rows(),
                     _sds((T, D_MODEL), BF16), (x, g))


def _post_fwd(x, y, g):
    T = x.shape[0]

    def body(x_ref, y_ref, g_ref, o_ref):
        yv = y_ref[...]
        r = lax.rsqrt(jnp.mean(yv * yv, axis=1, keepdims=True) + RMS_EPS)
        o_ref[...] = x_ref[...] + yv * r * g_ref[...]

    return _row_call(body, "post_fwd", T, [_rows(), _rows(), _full((1, D_MODEL))], _rows(),
                     _sds((T, D_MODEL), F32), (x, y, g))


def _rmsnorm_bwd(dout, xin, g, add, out_dtype, name):
    T = xin.shape[0]

    def body(*refs):
        if add is None:
            d_ref, x_ref, g_ref, dx_ref, dg_ref = refs
        else:
            d_ref, x_ref, g_ref, add_ref, dx_ref, dg_ref = refs

        @pl.when(pl.program_id(0) == 0)
        def _():
            dg_ref[...] = jnp.zeros_like(dg_ref)

        xv = x_ref[...]
        dv = d_ref[...].astype(F32)
        r = lax.rsqrt(jnp.mean(xv * xv, axis=1, keepdims=True) + RMS_EPS)
        xh = xv * r
        dg_ref[...] += jnp.sum(dv * xh, axis=0, keepdims=True)
        dn = dv * g_ref[...]
        dx = r * (dn - xh * jnp.mean(dn * xh, axis=1, keepdims=True))
        if add is not None:
            dx = dx + add_ref[...]
        dx_ref[...] = dx.astype(out_dtype)

    in_specs = [_rows(), _rows(), _full((1, D_MODEL))]
    args = [dout, xin, g]
    if add is not None:
        in_specs.append(_rows())
        args.append(add)
    return _row_call(body, name, T, in_specs, [_rows(), _full((1, D_MODEL))],
                     [_sds((T, D_MODEL), out_dtype), _sds((1, D_MODEL), F32)], args)


def _ple_fwd(x1, pe, gl):
    T = x1.shape[0]

    def body(x_ref, pe_ref, gl_ref, o_ref):
        o_ref[...] = x_ref[...] + pe_ref[...] * _sigmoid(gl_ref[...])

    return _row_call(body, "ple_fwd", T, [_rows()] * 3, _rows(), _sds((T, D_MODEL), F32), (x1, pe, gl))


def _ple_bwd(dx2, pe, gl):
    T = dx2.shape[0]

    def body(d_ref, pe_ref, gl_ref, dpe_ref, dgl_ref):
        dv = d_ref[...]
        sg = _sigmoid(gl_ref[...])
        dpe_ref[...] = (dv * sg).astype(BF16)
        dgl_ref[...] = (dv * pe_ref[...] * sg * (1.0 - sg)).astype(BF16)

    return _row_call(body, "ple_bwd", T, [_rows()] * 3, [_rows()] * 2,
                     [_sds((T, D_MODEL), BF16)] * 2, (dx2, pe, gl))


def _loss_fwd_bwd(y, target):
    T = y.shape[0]

    def body(y_ref, t_ref, s_ref, d_ref):
        @pl.when(pl.program_id(0) == 0)
        def _():
            s_ref[...] = jnp.zeros_like(s_ref)

        e = y_ref[...] - t_ref[...]
        s_ref[...] += jnp.sum(e * e).reshape(1, 1)
        d_ref[...] = e * (1.0 / D_MODEL)

    return _row_call(body, "loss", T, [_rows()] * 2, [_full((1, 1)), _rows()],
                     [_sds((1, 1), F32), _sds((T, D_MODEL), F32)], (y, target))


def _attn_combine(outs, lses, proj):
    T = proj.shape[0]

    def body(o0, o1, o2, l0, l1, l2, z_ref, a_ref, o_ref, lse_ref):
        a0, a1, a2 = l0[...], l1[...], l2[...]
        m = jnp.maximum(jnp.maximum(a0, a1), a2)
        e0, e1, e2 = jnp.exp(a0 - m), jnp.exp(a1 - m), jnp.exp(a2 - m)
        ssum = e0 + e1 + e2
        o = (e0 * o0[...] + e1 * o1[...] + e2 * o2[...]) / ssum
        zv = z_ref[...].astype(F32)
        o_ref[...] = o
        lse_ref[...] = m + jnp.log(ssum)
        a_ref[...] = (o * zv * _sigmoid(zv)).astype(BF16)

    return _row_call(body, "attn_combine", T, [_rows()] * 6 + [_rows(cb=3 * N_GROUPS)], [_rows()] * 3,
                     [_sds((T, D_MODEL), BF16), _sds((T, D_MODEL), F32), _sds((T, D_MODEL), F32)],
                     (*outs, *lses, proj))


def _gate_bwd(da, o, proj):
    T = da.shape[0]

    def body(da_ref, o_ref, z_ref, do_ref, dz_ref):
        dv = da_ref[...]
        zv = z_ref[...]
        sg = _sigmoid(zv)
        do_ref[...] = dv * zv * sg
        dz_ref[...] = dv * o_ref[...] * sg * (1.0 + zv * (1.0 - sg))

    zcols = _rows(cb=3 * N_GROUPS)
    return _row_call(body, "gate_bwd", T, [_rows(), _rows(), zcols], [_rows(), zcols],
                     [_sds((T, D_MODEL), F32), _sds((T, PROJ_COLS), F32)], (da, o, proj))


def _rope_tables(positions):
    inv_freq = 1.0 / (ROPE_THETA ** (jnp.arange(0, HEAD_DIM, 2, dtype=F32) / HEAD_DIM))
    ang = positions.astype(F32)[..., None] * inv_freq
    cos, sin = jnp.cos(ang), jnp.sin(ang)
    return jnp.tile(cos, (1, 1, 4)), jnp.concatenate([-sin, sin, -sin, sin], axis=-1)


def _rotate_half_partner(t):
    lane = lax.broadcasted_iota(jnp.int32, t.shape, 1)
    return jnp.where((lane % HEAD_DIM) < HEAD_DIM // 2,
                     pltpu.roll(t, LANES - HEAD_DIM // 2, 1), pltpu.roll(t, HEAD_DIM // 2, 1))


def _mask_bias(first):
    qi = lax.broadcasted_iota(jnp.int32, (N_BACK, 2 * N_BACK), 0)
    kj = lax.broadcasted_iota(jnp.int32, (N_BACK, 2 * N_BACK), 1)
    ok = (kj >= qi) & (kj <= qi + N_BACK)
    if first:
        ok = ok & (kj >= N_BACK)
    return jnp.where(ok, 0.0, NEG_INF).astype(F32)


def _stack_heads(t, head0):
    zero = jnp.zeros_like(t)
    return jnp.concatenate([jnp.where(head0, t, zero), jnp.where(head0, zero, t)], axis=0)


def _unstack_heads(t2, head0):
    return jnp.where(head0, t2[:N_BACK], t2[N_BACK:])


def _block_loop(nb, block):
    first = _mask_bias(True)
    block(0, jnp.concatenate([first, first], axis=0))
    if nb > 1:
        rest = _mask_bias(False)
        bias = jnp.concatenate([rest, rest], axis=0)

        def step(n, carry):
            block(n, bias)
            return carry

        lax.fori_loop(1, nb, step, 0, unroll=2)


_NT = (((1,), (1,)), ((), ()))
_TN = (((0,), (0,)), ((), ()))


def _residue_rows(r, i, d):
    start = r + i * (N_BACK * d)
    if d == 1:
        return pl.ds(pl.multiple_of(start, N_BACK), N_BACK)
    return pl.ds(start, N_BACK, stride=d)


def _seq_rows(i):
    return pl.ds(pl.multiple_of(i * N_BACK, N_BACK), N_BACK)


def _rows_at(base, i, size=N_BACK):
    return pl.ds(pl.multiple_of(base + i * N_BACK, N_BACK), size)


def _attn_fwd(proj, cos, sin, group, Bl, S):
    d = WIN_DIL[group][1]
    L = S // d
    nb = L // N_BACK
    P = L + N_BACK
    assert WIN_DIL[group][0] // d == N_BACK and L % N_BACK == 0

    def body(q_ref, k_ref, v_ref, cos_ref, sin_ref, o_ref, lse_ref, qr, kr, vp):
        head0 = lax.broadcasted_iota(jnp.int32, (1, LANES), 1) < HEAD_DIM
        zeros = jnp.zeros((N_BACK, LANES), BF16)

        def residue(r, carry):
            qbase, kbase = r * L, r * P
            kr[_rows_at(kbase, 0), :] = zeros
            vp[_rows_at(kbase, 0), :] = zeros

            def rope(i, carry2):
                rows = _residue_rows(r, i, d)
                cs, sn = cos_ref[rows, :], sin_ref[rows, :]
                q, k = q_ref[rows, :], k_ref[rows, :]
                qr[_rows_at(qbase, i), :] = ((q * cs + _rotate_half_partner(q) * sn)
                                            * (HEAD_DIM ** -0.5)).astype(BF16)
                kr[_rows_at(kbase, i + 1), :] = (k * cs + _rotate_half_partner(k) * sn).astype(BF16)
                vp[_rows_at(kbase, i + 1), :] = v_ref[rows, :].astype(BF16)
                return carry2

            lax.fori_loop(0, nb, rope, 0)

            def block(n, bias):
                win = _rows_at(kbase, n, 2 * N_BACK)
                q2, kw, vw = _stack_heads(qr[_rows_at(qbase, n), :], head0), kr[win, :], vp[win, :]
                s = lax.dot_general(q2, kw, _NT, preferred_element_type=F32) + bias
                m = jnp.max(s, axis=1, keepdims=True)
                p = jnp.exp(s - m)
                l = jnp.sum(p, axis=1, keepdims=True)
                pv = jnp.dot(p.astype(BF16), vw, preferred_element_type=F32)
                rows = _residue_rows(r, n, d)
                o_ref[rows, :] = _unstack_heads(pv * (1.0 / l), head0)
                lse_ref[rows, :] = _unstack_heads((m + jnp.log(l)) + jnp.zeros((2 * N_BACK, LANES), F32), head0)

            _block_loop(nb, block)
            return carry

        lax.fori_loop(0, d, residue, 0)

    act = pl.BlockSpec((None, S, LANES), lambda b, hp: (b, 0, hp))
    tab = pl.BlockSpec((None, S, LANES), lambda b, hp: (b, 0, 0))
    col = lambda which: pl.BlockSpec((None, S, LANES),
                                     lambda b, hp: (b, 0, (which * N_GROUPS + group) * HEAD_PAIRS + hp))
    seq = lambda rows: pl.BlockSpec((None, None, rows, LANES), lambda b, hp: (b, hp, 0, 0))
    p3 = proj.reshape(Bl, S, PROJ_COLS)
    o, lse, qr, kr, vp = pl.pallas_call(
        body, name="attn_fwd_g%d" % group, grid=(Bl, HEAD_PAIRS),
        in_specs=[col(0), col(1), col(2), tab, tab], out_specs=[act, act, seq(S), seq(d * P), seq(d * P)],
        out_shape=[_sds((Bl, S, D_MODEL), F32)] * 2 + [_sds((Bl, HEAD_PAIRS, S, LANES), BF16)]
        + [_sds((Bl, HEAD_PAIRS, d * P, LANES), BF16)] * 2,
        compiler_params=_params(("parallel", "arbitrary"), VMEM_LIMIT_ATTN),
    )(p3, p3, p3, cos, sin)
    return o.reshape(Bl * S, D_MODEL), lse.reshape(Bl * S, D_MODEL), (qr, kr, vp)


def _attn_bwd(saved, cos, sin, do, o, lse, dproj, group, Bl, S):
    d = WIN_DIL[group][1]
    L = S // d
    nb = L // N_BACK
    P = L + N_BACK
    steps = Bl * HEAD_PAIRS

    def body(qr, kr, vp, cos_ref, sin_ref, do_ref, o_ref, lse_ref, dproj_in, dproj_ref,
             dk_acc, dv_acc, stage, sems):
        del dproj_in
        head0 = lax.broadcasted_iota(jnp.int32, (1, LANES), 1) < HEAD_DIM
        b, hp = pl.program_id(0), pl.program_id(1)
        step = b * HEAD_PAIRS + hp
        slot = step % 2
        dq_s, dk_s, dv_s = stage.at[slot, 0], stage.at[slot, 1], stage.at[slot, 2]

        def copies(which_slot):
            out = []
            for which in range(3):
                col = ((which * N_GROUPS + group) * HEAD_PAIRS + hp) * LANES
                out.append(pltpu.make_async_copy(
                    stage.at[which_slot, which], dproj_ref.at[b, :, pl.ds(pl.multiple_of(col, LANES), LANES)],
                    sems.at[which_slot, which]))
            return out

        @pl.when(step >= 2)
        def _():
            for cp in copies(slot):
                cp.wait()

        def residue(r, carry):
            qbase, kbase = r * L, r * P
            dk_acc[...] = jnp.zeros_like(dk_acc)
            dv_acc[...] = jnp.zeros_like(dv_acc)

            def block(n, bias):
                win = pl.ds(pl.multiple_of(n * N_BACK, N_BACK), 2 * N_BACK)
                kwin = _rows_at(kbase, n, 2 * N_BACK)
                rows = _residue_rows(r, n, d)
                q2, kw, vw = _stack_heads(qr[_rows_at(qbase, n), :], head0), kr[kwin, :], vp[kwin, :]
                dof = do_ref[rows, :]
                do2 = _stack_heads(dof.astype(BF16), head0)
                lse_b = lse_ref[rows, :]
                lse2 = jnp.concatenate([lse_b[:, 0:1], lse_b[:, HEAD_DIM:HEAD_DIM + 1]], axis=0)
                dsum = _stack_heads(dof * o_ref[rows, :], head0)
                delta = jnp.sum(dsum, axis=1, keepdims=True)
                s = lax.dot_general(q2, kw, _NT, preferred_element_type=F32) + bias
                p = jnp.exp(s - lse2)
                dp = lax.dot_general(do2, vw, _NT, preferred_element_type=F32)
                ds = (p * (dp - delta)).astype(BF16)
                dq = _unstack_heads(jnp.dot(ds, kw, preferred_element_type=F32), head0) * (HEAD_DIM ** -0.5)
                cs, sn = cos_ref[rows, :], sin_ref[rows, :]
                dq_s[rows, :] = dq * cs + _rotate_half_partner(dq * sn)
                dk_acc[win, :] += lax.dot_general(ds, q2, _TN, preferred_element_type=F32)
                dv_acc[win, :] += lax.dot_general(p.astype(BF16), do2, _TN, preferred_element_type=F32)

            _block_loop(nb, block)

            def finish(i, carry2):
                rows = _residue_rows(r, i, d)
                cs, sn = cos_ref[rows, :], sin_ref[rows, :]
                dk = dk_acc[_seq_rows(i + 1), :]
                dk_s[rows, :] = dk * cs + _rotate_half_partner(dk * sn)
                dv_s[rows, :] = dv_acc[_seq_rows(i + 1), :]
                return carry2

            lax.fori_loop(0, nb, finish, 0)
            return carry

        lax.fori_loop(0, d, residue, 0)
        for cp in copies(slot):
            cp.start()

        @pl.when(step == steps - 1)
        def _():
            if steps > 1:
                for cp in copies(1 - slot):
                    cp.wait()
            for cp in copies(slot):
                cp.wait()

    act = pl.BlockSpec((None, S, LANES), lambda b, hp: (b, 0, hp))
    tab = pl.BlockSpec((None, S, LANES), lambda b, hp: (b, 0, 0))
    seq = lambda rows: pl.BlockSpec((None, None, rows, LANES), lambda b, hp: (b, hp, 0, 0))
    view = lambda t: t.reshape(Bl, S, D_MODEL)
    out = pl.pallas_call(
        body, name="attn_bwd_g%d" % group, grid=(Bl, HEAD_PAIRS),
        in_specs=[seq(S), seq(d * P), seq(d * P), tab, tab, act, act, act, _ANY], out_specs=_ANY,
        out_shape=_sds((Bl, S, PROJ_COLS), F32), input_output_aliases={8: 0},
        scratch_shapes=[pltpu.VMEM((P, LANES), F32), pltpu.VMEM((P, LANES), F32),
                        pltpu.VMEM((2, 3, S, LANES), F32), pltpu.SemaphoreType.DMA((2, 3))],
        compiler_params=_params(("arbitrary", "arbitrary"), VMEM_LIMIT_ATTN),
    )(*saved, cos, sin, view(do), view(o), view(lse), dproj.reshape(Bl, S, PROJ_COLS))
    return out.reshape(Bl * S, PROJ_COLS)


CONV_TILE = 256
CONV_CHUNK = 64


def _conv_fwd(proj, z, dw, dwb, ln_g, ln_b, Bl, S):
    tr = CONV_TILE
    nj = S // tr
    hb = tr // CONV_HALO

    def body(a_ref, b_ref, ah_ref, bh_ref, z_ref, dw_ref, dwb_ref, g_ref, bb_ref, u1_ref, out_ref, ext):
        j = pl.program_id(1)
        halo = ah_ref[0].astype(F32) * _sigmoid(bh_ref[0].astype(F32))
        ext[pl.ds(0, CONV_HALO), :] = jnp.where(j > 0, halo, 0.0)
        ext[pl.ds(CONV_HALO, tr), :] = a_ref[0].astype(F32) * _sigmoid(b_ref[0].astype(F32))

        def cols(c, carry):
            cs = pl.ds(pl.multiple_of(c * LANES, LANES), LANES)
            for rc in range(tr // CONV_CHUNK):
                acc = jnp.zeros((CONV_CHUNK, LANES), F32)
                for w in range(CONV_WIDTH):
                    off = rc * CONV_CHUNK + CONV_HALO - (CONV_WIDTH - 1) + w
                    acc = acc + dw_ref[pl.ds(w, 1), cs] * ext[pl.ds(off, CONV_CHUNK), cs]
                u1_ref[0, pl.ds(rc * CONV_CHUNK, CONV_CHUNK), cs] = acc + dwb_ref[:, cs]
            return carry

        lax.fori_loop(0, D_MODEL // LANES, cols, 0)
        u1 = u1_ref[0]
        mu = jnp.mean(u1, axis=1, keepdims=True)
        xc = u1 - mu
        rstd = lax.rsqrt(jnp.mean(xc * xc, axis=1, keepdims=True) + LN_EPS)
        u2 = xc * rstd * g_ref[...] + bb_ref[...]
        zv = z_ref[0].astype(F32)
        out_ref[0] = (u2 * _sigmoid(u2) * zv * _sigmoid(zv)).astype(BF16)

    tile = lambda cb: pl.BlockSpec((1, tr, D_MODEL), lambda b, j: (b, j, cb))
    halo = lambda cb: pl.BlockSpec((1, CONV_HALO, D_MODEL), lambda b, j: (b, jnp.maximum(j * hb - 1, 0), cb))
    par = lambda r: pl.BlockSpec((r, D_MODEL), lambda b, j: (0, 0))
    p3 = proj.reshape(Bl, S, 2 * D_MODEL)
    u1, out = pl.pallas_call(
        body, name="conv_fwd", grid=(Bl, nj),
        in_specs=[tile(0), tile(1), halo(0), halo(1), tile(0), par(32), par(1), par(1), par(1)],
        out_specs=[tile(0), tile(0)],
        out_shape=[_sds((Bl, S, D_MODEL), F32), _sds((Bl, S, D_MODEL), BF16)],
        scratch_shapes=[pltpu.VMEM((tr + CONV_HALO, D_MODEL), F32)],
        compiler_params=_params(("parallel", "arbitrary")),
    )(p3, p3, p3, p3, z.reshape(Bl, S, D_MODEL), dw, dwb, ln_g, ln_b)
    return u1.reshape(Bl * S, D_MODEL), out.reshape(Bl * S, D_MODEL)


def _conv_norm_bwd(da2, z, u1, ln_g, ln_b):
    T = da2.shape[0]

    def body(da_ref, z_ref, u_ref, g_ref, b_ref, du_ref, dz_ref, dg_ref, db_ref):
        @pl.when(pl.program_id(0) == 0)
        def _():
            dg_ref[...] = jnp.zeros_like(dg_ref)
            db_ref[...] = jnp.zeros_like(db_ref)

        u1 = u_ref[...]
        mu = jnp.mean(u1, axis=1, keepdims=True)
        xc = u1 - mu
        rstd = lax.rsqrt(jnp.mean(xc * xc, axis=1, keepdims=True) + LN_EPS)
        nrm = xc * rstd
        u2 = nrm * g_ref[...] + b_ref[...]
        s2 = _sigmoid(u2)
        zv = z_ref[...].astype(F32)
        sz = _sigmoid(zv)
        dv = da_ref[...]
        dz_ref[...] = (dv * u2 * s2 * sz * (1.0 + zv * (1.0 - sz))).astype(BF16)
        du2 = dv * zv * sz * s2 * (1.0 + u2 * (1.0 - s2))
        dg_ref[...] += jnp.sum(du2 * nrm, axis=0, keepdims=True)
        db_ref[...] += jnp.sum(du2, axis=0, keepdims=True)
        dn = du2 * g_ref[...]
        du_ref[...] = rstd * (dn - jnp.mean(dn, axis=1, keepdims=True)
                              - nrm * jnp.mean(dn * nrm, axis=1, keepdims=True))

    return _row_call(body, "conv_norm_bwd", T,
                     [_rows(), _rows(), _rows(), _full((1, D_MODEL)), _full((1, D_MODEL))],
                     [_rows(), _rows(), _full((1, D_MODEL)), _full((1, D_MODEL))],
                     [_sds((T, D_MODEL), F32), _sds((T, D_MODEL), BF16), _sds((1, D_MODEL), F32),
                      _sds((1, D_MODEL), F32)], (da2, z, u1, ln_g, ln_b))


def _conv_bwd(proj, du1, dw, Bl, S):
    tr = CONV_TILE
    nj = S // tr
    hb = tr // CONV_HALO

    def body(a_ref, b_ref, ah_ref, bh_ref, du_ref, duh_ref, dw_ref, dab_ref, ddw_ref, ddb_ref, uext, dext, du0):
        first = (pl.program_id(0) == 0) & (pl.program_id(1) == 0)
        j = pl.program_id(1)

        @pl.when(first)
        def _():
            ddw_ref[...] = jnp.zeros_like(ddw_ref)
            ddb_ref[...] = jnp.zeros_like(ddb_ref)

        halo = ah_ref[0].astype(F32) * _sigmoid(bh_ref[0].astype(F32))
        uext[pl.ds(0, CONV_HALO), :] = jnp.where(j > 0, halo, 0.0)
        av = a_ref[0].astype(F32)
        sb = _sigmoid(b_ref[0].astype(F32))
        uext[pl.ds(CONV_HALO, tr), :] = av * sb
        dext[pl.ds(0, tr), :] = du_ref[0]
        dext[pl.ds(tr, CONV_HALO), :] = jnp.where(j < nj - 1, duh_ref[0], 0.0)
        ddb_ref[...] += jnp.sum(du_ref[0], axis=0, keepdims=True)

        def cols(c, carry):
            cs = pl.ds(pl.multiple_of(c * LANES, LANES), LANES)
            for rc in range(tr // CONV_CHUNK):
                base = rc * CONV_CHUNK
                dchunk = dext[pl.ds(base, CONV_CHUNK), cs]
                acc = jnp.zeros((CONV_CHUNK, LANES), F32)
                for w in range(CONV_WIDTH):
                    acc = acc + dw_ref[pl.ds(w, 1), cs] * dext[pl.ds(base + CONV_WIDTH - 1 - w, CONV_CHUNK), cs]
                    off = base + CONV_HALO - (CONV_WIDTH - 1) + w
                    ddw_ref[pl.ds(w, 1), cs] += jnp.sum(dchunk * uext[pl.ds(off, CONV_CHUNK), cs],
                                                        axis=0, keepdims=True)
                du0[pl.ds(base, CONV_CHUNK), cs] = acc
            return carry

        lax.fori_loop(0, D_MODEL // LANES, cols, 0)
        g = du0[...]
        dab_ref[0, :, 0:D_MODEL] = (g * sb).astype(BF16)
        dab_ref[0, :, D_MODEL:2 * D_MODEL] = (g * av * sb * (1.0 - sb)).astype(BF16)

    tile = lambda cb: pl.BlockSpec((1, tr, D_MODEL), lambda b, j: (b, j, cb))
    halo = lambda cb: pl.BlockSpec((1, CONV_HALO, D_MODEL), lambda b, j: (b, jnp.maximum(j * hb - 1, 0), cb))
    nxt = pl.BlockSpec((1, CONV_HALO, D_MODEL), lambda b, j: (b, jnp.minimum((j + 1) * hb, S // CONV_HALO - 1), 0))
    par = lambda r: pl.BlockSpec((r, D_MODEL), lambda b, j: (0, 0))
    p3 = proj.reshape(Bl, S, 2 * D_MODEL)
    d3 = du1.reshape(Bl, S, D_MODEL)
    dab, ddw, ddb = pl.pallas_call(
        body, name="conv_bwd", grid=(Bl, nj),
        in_specs=[tile(0), tile(1), halo(0), halo(1), tile(0), nxt, par(32)],
        out_specs=[pl.BlockSpec((1, tr, 2 * D_MODEL), lambda b, j: (b, j, 0)), par(32), par(1)],
        out_shape=[_sds((Bl, S, 2 * D_MODEL), BF16), _sds((32, D_MODEL), F32), _sds((1, D_MODEL), F32)],
        scratch_shapes=[pltpu.VMEM((tr + CONV_HALO, D_MODEL), F32), pltpu.VMEM((tr + CONV_HALO, D_MODEL), F32),
                        pltpu.VMEM((tr, D_MODEL), F32)],
        compiler_params=_params(("arbitrary", "arbitrary")),
    )(p3, p3, p3, p3, d3, d3, dw)
    return dab.reshape(Bl * S, 2 * D_MODEL), ddw, ddb


_LAYOUT = (
    ("pre_norm_g", (4, 1024), None), ("post_norm_g", (4, 1024), None),
    ("attn_w_in", (2, 1024, 2560), 2), ("attn_w_out", (2, 256, 1024), 1),
    ("conv_w_in", (2, 1024, 768), 2), ("conv_dw_w", (2, 31, 256), 2),
    ("conv_dw_b", (2, 256), 1), ("conv_ln_g", (2, 256), 1), ("conv_ln_b", (2, 256), 1),
    ("conv_w_out", (2, 256, 1024), 1), ("ple_w_proj", (4, 256, 256), 2), ("ple_w_gate", (4, 256, 1024), 1),
)
_MATMUL_WEIGHTS = ("attn_w_in", "attn_w_out", "conv_w_in", "conv_w_out", "ple_w_proj", "ple_w_gate")
_SMALL_WEIGHTS = ("conv_dw_w", "conv_dw_b", "conv_ln_g", "conv_ln_b")
_SHAPE = {n: s for n, s, _ in _LAYOUT}
_AXIS = {n: a for n, _, a in _LAYOUT}


def _size(shape):
    n = 1
    for s in shape:
        n *= s
    return n


def _padded_rows(shape):
    rows = _size(shape) // shape[-1]
    return rows + (-rows) % FLAT_ROW_ALIGN


def _rows2d(a):
    a2 = a.reshape(-1, a.shape[-1])
    pad = _padded_rows(a.shape) - a2.shape[0]
    return jnp.pad(a2, ((0, pad), (0, 0))) if pad else a2


def _col_blocks(a):
    a2 = _rows2d(a)
    rows, nblk = a2.shape[0], a2.shape[1] // FLAT_COLS
    return a2.reshape(rows, nblk, FLAT_COLS).transpose(1, 0, 2).reshape(nblk * rows, FLAT_COLS)


def _from_col_blocks(flat, off, shape):
    rows, nblk = _padded_rows(shape), shape[-1] // FLAT_COLS
    a2 = flat[off:off + nblk * rows].reshape(nblk, rows, FLAT_COLS).transpose(1, 0, 2).reshape(rows, shape[-1])
    return a2[:_size(shape) // shape[-1]].reshape(shape), off + nblk * rows


def _shard_col_blocks(full, shape, axis):
    if axis is None:
        return jnp.broadcast_to(_col_blocks(full)[None], (N_CHIPS,) + _col_blocks(full).shape)
    nblk = shape[-1] // FLAT_COLS
    if axis == len(shape) - 1:
        a2 = _rows2d(full)
        rows = a2.shape[0]
        a5 = a2.reshape(rows, N_CHIPS, nblk, FLAT_COLS).transpose(1, 2, 0, 3)
        return a5.reshape(N_CHIPS, nblk * rows, FLAT_COLS)
    layers, r, _ = shape
    assert axis == 1 and (layers * r) % FLAT_ROW_ALIGN == 0
    a5 = full.reshape(layers, N_CHIPS, r, nblk, FLAT_COLS).transpose(1, 3, 0, 2, 4)
    return a5.reshape(N_CHIPS, nblk * layers * r, FLAT_COLS)


def _unshard_col_blocks(g, off, shape, axis):
    rows, nblk = _padded_rows(shape), shape[-1] // FLAT_COLS
    piece = g[:, off:off + nblk * rows]
    if axis == len(shape) - 1:
        a2 = piece.reshape(N_CHIPS, nblk, rows, FLAT_COLS).transpose(2, 0, 1, 3).reshape(rows, N_CHIPS * shape[-1])
        full = a2[:_size(shape) // shape[-1]].reshape(shape[:-1] + (N_CHIPS * shape[-1],))
    else:
        layers, r, m = shape
        a5 = piece.reshape(N_CHIPS, nblk, layers, r, FLAT_COLS).transpose(2, 0, 3, 1, 4)
        full = a5.reshape(layers, N_CHIPS * r, m)
    return full, off + nblk * rows


def _pack_rows(pieces):
    flat = jnp.concatenate(pieces, axis=0)
    assert flat.shape[0] <= FLAT_ROWS
    return jnp.pad(flat, ((0, FLAT_ROWS - flat.shape[0]), (0, 0)))


def _pack_f32(params):
    return _pack_rows([_col_blocks(params[n]) for n, _, _ in _LAYOUT])


def _unpack_f32(flat):
    out, off = {}, 0
    for n, shape, _ in _LAYOUT:
        out[n], off = _from_col_blocks(flat, off, shape)
    return out


def _bytes_shape(shape):
    return shape[:-1] + (4 * shape[-1],)


def _f32_to_bytes(a):
    u = lax.bitcast_convert_type(a, jnp.uint32)
    parts = jnp.stack([(u >> s) & 0xFF for s in (0, 8, 16, 24)], axis=-1)
    return parts.astype(F32).astype(BF16).reshape(_bytes_shape(a.shape))


def _bytes_to_f32(b, shape):
    u = b.reshape(shape + (4,)).astype(F32).astype(jnp.uint32)
    return lax.bitcast_convert_type(u[..., 0] | (u[..., 1] << 8) | (u[..., 2] << 16) | (u[..., 3] << 24), F32)


def _pack_gather_payload(w):
    pieces = [_col_blocks(w[n].astype(BF16)) for n in _MATMUL_WEIGHTS]
    pieces += [_col_blocks(_f32_to_bytes(w[n])) for n in _SMALL_WEIGHTS]
    return _pack_rows(pieces)


def _unpack_gathered(g):
    out, off = {}, 0
    for n in _MATMUL_WEIGHTS:
        out[n], off = _unshard_col_blocks(g, off, _SHAPE[n], _AXIS[n])
    for n in _SMALL_WEIGHTS:
        parts = [_from_col_blocks(g[s], off, _bytes_shape(_SHAPE[n])) for s in range(N_CHIPS)]
        out[n] = jnp.concatenate([_bytes_to_f32(part, _SHAPE[n]) for part, _ in parts], axis=_AXIS[n])
        off = parts[0][1]
    return out


def _pack_full_grads(grads):
    flat = jnp.concatenate([_shard_col_blocks(grads[n], shape, axis) for n, shape, axis in _LAYOUT], axis=1)
    assert flat.shape[1] <= FLAT_ROWS
    return jnp.pad(flat, ((0, 0), (0, FLAT_ROWS - flat.shape[1]), (0, 0)))


_ANY = pl.BlockSpec(memory_space=pl.ANY)


def _mesh_pos():
    return lax.axis_index("x"), lax.axis_index("y"), lax.axis_index("c")


def _other_chips(x, y):
    return [(1 - x, y), (x, 1 - y), (1 - x, 1 - y)]


def _allgather_weights(wl):
    R, C = wl.shape
    H = R // 2

    def body(w_ref, out_ref, send_sems, recv_sems):
        x, y, c = _mesh_pos()
        me, sibling = (x, y, c), (x, y, 1 - c)
        chips = _other_chips(x, y)

        def half(px, py, pc):
            return out_ref.at[2 * px + py, pl.ds(pc * H, H), :]

        def copy(k, block, to, src=None):
            return pltpu.make_async_remote_copy(
                src_ref=half(*block) if src is None else src, dst_ref=half(*block),
                send_sem=send_sems.at[k], recv_sem=recv_sems.at[k], device_id=to, device_id_type=MESH)

        own = pltpu.make_async_remote_copy(
            src_ref=w_ref, dst_ref=out_ref.at[2 * x + y], send_sem=send_sems.at[6], recv_sem=recv_sems.at[6],
            device_id=sibling, device_id_type=MESH)
        own.start()
        first = [copy(j, me, (*chip, c), src=w_ref.at[pl.ds(c * H, H), :]) for j, chip in enumerate(chips)]
        for cp in first:
            cp.start()
        passed = [copy(3 + j, (*chip, c), sibling) for j, chip in enumerate(chips)]
        for j, chip in enumerate(chips):
            copy(j, (*chip, c), me).wait_recv()
            passed[j].start()
        for j, chip in enumerate(chips):
            copy(3 + j, (*chip, 1 - c), me).wait_recv()
        own.wait()
        for cp in first + passed:
            cp.wait_send()

    return pl.pallas_call(
        body, name="allgather_weights", in_specs=[_ANY], out_specs=_ANY,
        out_shape=_sds((N_CHIPS, R, C), wl.dtype),
        scratch_shapes=[pltpu.SemaphoreType.DMA((7,)), pltpu.SemaphoreType.DMA((7,))],
    )(wl)


def _exchange_core_halves(g):
    n, _, H, C = g.shape

    def body(g_ref, got_ref, send_sem, recv_sem):
        x, y, c = _mesh_pos()
        swap = pltpu.make_async_remote_copy(
            src_ref=g_ref.at[pl.ds(0, n), 1 - c], dst_ref=got_ref, send_sem=send_sem, recv_sem=recv_sem,
            device_id=(x, y, 1 - c), device_id_type=MESH)
        swap.start()
        swap.wait()

    return pl.pallas_call(
        body, name="exchange_core_halves", in_specs=[_ANY], out_specs=_ANY,
        out_shape=_sds((n, H, C), g.dtype),
        scratch_shapes=[pltpu.SemaphoreType.DMA, pltpu.SemaphoreType.DMA],
    )(g)


def _scatter_to_chips(p):
    n, H, C = p.shape

    def body(p_ref, q_ref, send_sems, recv_sems):
        x, y, c = _mesh_pos()
        chips = _other_chips(x, y)
        sends = [pltpu.make_async_remote_copy(
            src_ref=p_ref.at[2 * cx + cy], dst_ref=q_ref.at[j], send_sem=send_sems.at[j],
            recv_sem=recv_sems.at[j], device_id=(cx, cy, c), device_id_type=MESH)
            for j, (cx, cy) in enumerate(chips)]
        for cp in sends:
            cp.start()
        for cp in sends:
            cp.wait_recv()
        for cp in sends:
            cp.wait_send()

    return pl.pallas_call(
        body, name="scatter_to_chips", in_specs=[_ANY], out_specs=_ANY,
        out_shape=_sds((n - 1, H, C), p.dtype),
        scratch_shapes=[pltpu.SemaphoreType.DMA((3,)), pltpu.SemaphoreType.DMA((3,))],
    )(p)


def _share_core_halves(r2):
    _, H, C = r2.shape

    def body(r_ref, out_ref, send_sem, recv_sem):
        x, y, c = _mesh_pos()
        send = pltpu.make_async_remote_copy(
            src_ref=r_ref.at[c], dst_ref=out_ref.at[c], send_sem=send_sem, recv_sem=recv_sem,
            device_id=(x, y, 1 - c), device_id_type=MESH)
        send.start()
        send.wait_send()
        pltpu.make_async_remote_copy(
            src_ref=r_ref.at[c], dst_ref=out_ref.at[1 - c], send_sem=send_sem, recv_sem=recv_sem,
            device_id=(x, y, 1 - c), device_id_type=MESH).wait_recv()

    return pl.pallas_call(
        body, name="share_core_halves", in_specs=[_ANY], out_specs=_ANY,
        out_shape=_sds(r2.shape, r2.dtype), input_output_aliases={0: 0},
        scratch_shapes=[pltpu.SemaphoreType.DMA, pltpu.SemaphoreType.DMA],
    )(r2)


def _place():
    x, y, c = _mesh_pos()
    return jnp.stack([c, 2 * x + y]).astype(jnp.int32)


def _sum_pair(g, got, place):
    n, _, H, C = g.shape

    def body(place_ref, a_ref, b_ref, o_ref):
        o_ref[...] = (a_ref[...] + b_ref[...]).astype(BF16)

    spec = pl.BlockSpec((1, FLAT_TILE, C), lambda s, i, pr: (s, i, 0))
    return pl.pallas_call(
        body, name="sum_core_pair",
        grid_spec=pltpu.PrefetchScalarGridSpec(
            num_scalar_prefetch=1, grid=(n, H // FLAT_TILE),
            in_specs=[pl.BlockSpec((1, None, FLAT_TILE, C), lambda s, i, pr: (s, pr[0], i, 0)), spec],
            out_specs=spec),
        out_shape=_sds((n, H, C), BF16),
        compiler_params=_params(("parallel", "parallel")))(place, g, got)


def _sum_chips(p, q, place):
    n, H, C = p.shape

    def body(place_ref, own_ref, qx_ref, qy_ref, qxy_ref, o_ref):
        mine = place_ref[1]
        own, qx, qy, qxy = (t[0].astype(F32) for t in (own_ref, qx_ref, qy_ref, qxy_ref))

        def term(s):
            rel = jnp.full(own.shape, mine ^ s, jnp.int32)
            return jnp.where(rel == 0, own, jnp.where(rel == 2, qx, jnp.where(rel == 1, qy, qxy)))

        o_ref[0] = ((term(0) + term(1)) + term(2)) + term(3)

    qspec = lambda j: pl.BlockSpec((1, FLAT_TILE, C), lambda i, pr: (j, i, 0))
    return pl.pallas_call(
        body, name="sum_chips",
        grid_spec=pltpu.PrefetchScalarGridSpec(
            num_scalar_prefetch=1, grid=(H // FLAT_TILE,),
            in_specs=[pl.BlockSpec((1, FLAT_TILE, C), lambda i, pr: (pr[1], i, 0)), qspec(0), qspec(1), qspec(2)],
            out_specs=pl.BlockSpec((1, FLAT_TILE, C), lambda i, pr: (pr[0], i, 0))),
        out_shape=_sds((2, H, C), F32),
        compiler_params=_params(("parallel",)))(place, p, q, q, q)


def _adamw(w, g, m, v):
    R, C = w.shape

    def body(w_ref, g_ref, m_ref, v_ref, d_ref, nm_ref, nv_ref):
        gv = g_ref[...]
        nm = ADAM_B1 * m_ref[...] + (1.0 - ADAM_B1) * gv
        nv = ADAM_B2 * v_ref[...] + (1.0 - ADAM_B2) * (gv * gv)
        m_hat = nm / (1.0 - ADAM_B1 ** ADAM_STEP)
        v_hat = nv / (1.0 - ADAM_B2 ** ADAM_STEP)
        d_ref[...] = -ADAM_LR * (m_hat / (jnp.sqrt(v_hat) + ADAM_EPS) + ADAM_WD * w_ref[...])
        nm_ref[...] = nm
        nv_ref[...] = nv

    spec = pl.BlockSpec((FLAT_TILE, C), lambda i: (i, 0))
    return pl.pallas_call(body, name="adamw", grid=(R // FLAT_TILE,), in_specs=[spec] * 4, out_specs=[spec] * 3,
                          out_shape=[_sds((R, C), F32)] * 3, compiler_params=_params(("parallel",)))(w, g, m, v)


def _reduce_scatter_grads(gfull):
    n, R, C = gfull.shape
    place = _place()
    g4 = gfull.reshape(n, 2, R // 2, C)
    p = _sum_pair(g4, _exchange_core_halves(g4), place)
    return _share_core_halves(_sum_chips(p, _scatter_to_chips(p), place)).reshape(R, C)


def _local_step(x, p, positions, loss_target, pre_g, post_g, w):
    Bl, S, _ = x.shape
    T = Bl * S
    cos, sin = _rope_tables(positions)
    xs = x.reshape(T, D_MODEL)
    saved = []
    for i in range(DEPTH):
        j = i // 2
        g_pre, g_post = pre_g[i:i + 1], post_g[i:i + 1]
        h = _rmsnorm_fwd(xs, g_pre)
        st = {"x": xs, "h": h}
        if i % 2 == 0:
            proj = _mm(h, w["attn_w_in"][j], name="attn_in")
            res = [_attn_fwd(proj, cos, sin, g, Bl, S) for g in range(N_GROUPS)]
            a, o, lse = _attn_combine([r[0] for r in res], [r[1] for r in res], proj)
            y = _mm(a, w["attn_w_out"][j], name="attn_out")
            st.update(proj=proj, a=a, o=o, lse=lse, qkv=[r[2] for r in res])
        else:
            w_ab, w_z = w["conv_w_in"][j][:, :2 * D_MODEL], w["conv_w_in"][j][:, 2 * D_MODEL:]
            ab = _mm(h, w_ab, out_dtype=BF16, name="conv_in_ab")
            z = _mm(h, w_z, out_dtype=BF16, name="conv_in_z")
            dw = jnp.pad(w["conv_dw_w"][j], ((0, 1), (0, 0)))
            u1, a = _conv_fwd(ab, z, dw, w["conv_dw_b"][j:j + 1], w["conv_ln_g"][j:j + 1],
                              w["conv_ln_b"][j:j + 1], Bl, S)
            y = _mm(a, w["conv_w_out"][j], name="conv_out")
            st.update(w_ab=w_ab, w_z=w_z, ab=ab, z=z, dw=dw, u1=u1, a=a)
        x1 = _post_fwd(xs, y, g_post)
        pi = p[i].reshape(T, PLE_DIM)
        pe = _mm(pi, w["ple_w_proj"][i], name="ple_proj")
        gl = _mm(x1, w["ple_w_gate"][i], name="ple_gate")
        xs = _ple_fwd(x1, pe, gl)
        st.update(y=y, x1=x1, pi=pi, pe=pe, gl=gl)
        saved.append(st)

    sq, dx = _loss_fwd_bwd(xs, loss_target.reshape(T, D_MODEL))

    grads = {n: [None] * shape[0] for n, shape, _ in _LAYOUT}
    for i in reversed(range(DEPTH)):
        j = i // 2
        st = saved[i]
        g_pre, g_post = pre_g[i:i + 1], post_g[i:i + 1]
        dpe, dgl = _ple_bwd(dx, st["pe"], st["gl"])
        grads["ple_w_proj"][i] = _mm(st["pi"], dpe, ta=True, name="ple_proj_wgrad")
        grads["ple_w_gate"][i] = _mm(st["x1"], dgl, ta=True, name="ple_gate_wgrad")
        dx1 = _mm(dgl, w["ple_w_gate"][i], tb=True, add=dx, name="ple_gate_dgrad")
        dy, dg_post = _rmsnorm_bwd(dx1, st["y"], g_post, None, BF16, "post_bwd")
        grads["post_norm_g"][i] = dg_post[0]
        if i % 2 == 0:
            grads["attn_w_out"][j] = _mm(st["a"], dy, ta=True, name="attn_out_wgrad")
            da = _mm(dy, w["attn_w_out"][j], tb=True, name="attn_out_dgrad")
            do, dproj = _gate_bwd(da, st["o"], st["proj"])
            for g in range(N_GROUPS):
                dproj = _attn_bwd(st["qkv"][g], cos, sin, do, st["o"], st["lse"], dproj, g, Bl, S)
            dh = _mm(dproj, w["attn_w_in"][j], tb=True, tk=2048, name="attn_in_dgrad")
            grads["attn_w_in"][j] = _mm(st["h"], dproj, ta=True, name="attn_in_wgrad")
        else:
            grads["conv_w_out"][j] = _mm(st["a"], dy, ta=True, name="conv_out_wgrad")
            da2 = _mm(dy, w["conv_w_out"][j], tb=True, name="conv_out_dgrad")
            du1, dz, dln_g, dln_b = _conv_norm_bwd(da2, st["z"], st["u1"], w["conv_ln_g"][j:j + 1],
                                                   w["conv_ln_b"][j:j + 1])
            dab, ddw, ddb = _conv_bwd(st["ab"], du1, st["dw"], Bl, S)
            dh = _mm(dz, st["w_z"], tb=True, name="conv_in_z_dgrad")
            dh = _mm(dab, st["w_ab"], tb=True, add=dh, name="conv_in_ab_dgrad")
            dw_ab = _mm(st["h"], dab, ta=True, name="conv_in_ab_wgrad")
            dw_z = _mm(st["h"], dz, ta=True, name="conv_in_z_wgrad")
            grads["conv_w_in"][j] = jnp.concatenate([dw_ab, dw_z], axis=1)
            grads["conv_dw_w"][j] = ddw[:CONV_WIDTH]
            grads["conv_dw_b"][j] = ddb[0]
            grads["conv_ln_g"][j] = dln_g[0]
            grads["conv_ln_b"][j] = dln_b[0]
        dx, dg_pre = _rmsnorm_bwd(dh, st["x"], g_pre, dx1, F32, "pre_bwd")
        grads["pre_norm_g"][i] = dg_pre[0]
    grads = {n: jnp.stack(v) for n, v in grads.items()}
    return sq, dx.reshape(Bl, S, D_MODEL), grads


_NAMES = tuple(n for n, _, _ in _LAYOUT)


def kernel(x, p, positions, pre_norm_g, post_norm_g, attn_w_in, attn_w_out, conv_w_in, conv_dw_w, conv_dw_b, conv_ln_g, conv_ln_b, conv_w_out, ple_w_proj, ple_w_gate, loss_target, m_pre_norm_g, m_post_norm_g, m_attn_w_in, m_attn_w_out, m_conv_w_in, m_conv_dw_w, m_conv_dw_b, m_conv_ln_g, m_conv_ln_b, m_conv_w_out, m_ple_w_proj, m_ple_w_gate, v_pre_norm_g, v_post_norm_g, v_attn_w_in, v_attn_w_out, v_conv_w_in, v_conv_dw_w, v_conv_dw_b, v_conv_ln_g, v_conv_ln_b, v_conv_w_out, v_ple_w_proj, v_ple_w_gate):
    w_loc = dict(zip(_NAMES, (pre_norm_g, post_norm_g, attn_w_in, attn_w_out, conv_w_in, conv_dw_w, conv_dw_b,
                              conv_ln_g, conv_ln_b, conv_w_out, ple_w_proj, ple_w_gate)))
    m_loc = dict(zip(_NAMES, (m_pre_norm_g, m_post_norm_g, m_attn_w_in, m_attn_w_out, m_conv_w_in, m_conv_dw_w,
                              m_conv_dw_b, m_conv_ln_g, m_conv_ln_b, m_conv_w_out, m_ple_w_proj, m_ple_w_gate)))
    v_loc = dict(zip(_NAMES, (v_pre_norm_g, v_post_norm_g, v_attn_w_in, v_attn_w_out, v_conv_w_in, v_conv_dw_w,
                              v_conv_dw_b, v_conv_ln_g, v_conv_ln_b, v_conv_w_out, v_ple_w_proj, v_ple_w_gate)))

    w_full = _unpack_gathered(_allgather_weights(_pack_gather_payload(w_loc)))
    sq, grad_x, grads = _local_step(x, p, positions, loss_target, pre_norm_g, post_norm_g, w_full)
    loss = lax.psum(sq[0, 0] * (0.5 / D_MODEL), ("x", "y", "c"))

    g_flat = _reduce_scatter_grads(_pack_full_grads(grads))
    delta, new_m, new_v = _adamw(_pack_f32(w_loc), g_flat, _pack_f32(m_loc), _pack_f32(v_loc))
    g_out, d_out, m_out, v_out = (_unpack_f32(t) for t in (g_flat, delta, new_m, new_v))
    return (loss, grad_x, *[g_out[n] for n in _NAMES], *[d_out[n] for n in _NAMES],
            *[m_out[n] for n in _NAMES], *[v_out[n] for n in _NAMES])
```

```python
import jax
import jax.numpy as jnp
from jax import lax
from jax.experimental import pallas as pl
from jax.experimental.pallas import tpu as pltpu

F32 = jnp.float32
BF16 = jnp.bfloat16

D_MODEL = 1024
DEPTH = 4
PLE_DIM = 256
HEAD_DIM = 64
WIN_DIL = ((128, 1), (512, 4), (2048, 16))
N_GROUPS = 3
N_BACK = 128
ROPE_THETA = 10000.0
CONV_WIDTH = 31
CONV_HALO = 32
RMS_EPS = 1e-6
LN_EPS = 1e-5
NEG_INF = -1e30
ADAM_LR, ADAM_B1, ADAM_B2, ADAM_EPS, ADAM_WD, ADAM_STEP = 0.001, 0.9, 0.999, 1e-08, 0.01, 10

LANES = 128
N_CHIPS = 4
VMEM_LIMIT = 48 * 1024 * 1024
VMEM_LIMIT_ATTN = 56 * 1024 * 1024
FLAT_COLS = 256
FLAT_ROWS = 36864
FLAT_TILE = 2048
FLAT_ROW_ALIGN = 16
PROJ_COLS = (3 * N_GROUPS + 1) * D_MODEL
HEAD_PAIRS = D_MODEL // LANES

MESH = pl.DeviceIdType.MESH


def _params(sem=None, vmem=VMEM_LIMIT):
    return pltpu.CompilerParams(dimension_semantics=sem, vmem_limit_bytes=vmem)


def _sigmoid(v):
    return 1.0 / (1.0 + jnp.exp(-v))


def _mm(a, b, *, ta=False, tb=False, add=None, out_dtype=F32, tm=1024, tn=1024, tk=1024, name="mm"):
    if ta:
        K, M = a.shape
    else:
        M, K = a.shape
    if tb:
        N, K2 = b.shape
    else:
        K2, N = b.shape
    assert K == K2, (a.shape, b.shape)
    tm, tn, tk = min(tm, M), min(tn, N), min(tk, K)
    assert M % tm == 0 and N % tn == 0 and K % tk == 0
    nk = K // tk
    dims = (((0 if ta else 1,), (1 if tb else 0,)), ((), ()))

    def body(*refs):
        if add is None:
            a_ref, b_ref, o_ref = refs[:3]
        else:
            a_ref, b_ref, add_ref, o_ref = refs[:4]
        k = pl.program_id(2)
        part = lax.dot_general(a_ref[...].astype(BF16), b_ref[...].astype(BF16), dims, preferred_element_type=F32)

        def finish(r):
            if add is not None:
                r = r + add_ref[...].astype(F32)
            o_ref[...] = r.astype(out_dtype)

        if nk == 1:
            finish(part)
        else:
            acc_ref = refs[-1]

            @pl.when(k == 0)
            def _():
                acc_ref[...] = part

            @pl.when((k > 0) & (k < nk - 1))
            def _():
                acc_ref[...] += part

            @pl.when(k == nk - 1)
            def _():
                finish(acc_ref[...] + part)

    a_spec = pl.BlockSpec((tk, tm), lambda i, j, k: (k, i)) if ta else pl.BlockSpec((tm, tk), lambda i, j, k: (i, k))
    b_spec = pl.BlockSpec((tn, tk), lambda i, j, k: (j, k)) if tb else pl.BlockSpec((tk, tn), lambda i, j, k: (k, j))
    o_spec = pl.BlockSpec((tm, tn), lambda i, j, k: (i, j))
    in_specs, args = [a_spec, b_spec], [a, b]
    if add is not None:
        in_specs.append(o_spec)
        args.append(add)
    return pl.pallas_call(
        body, name=name, grid=(M // tm, N // tn, nk),
        in_specs=in_specs, out_specs=o_spec,
        out_shape=jax.ShapeDtypeStruct((M, N), out_dtype),
        scratch_shapes=[pltpu.VMEM((tm, tn), F32)] if nk > 1 else [],
        compiler_params=_params(("parallel", "parallel", "arbitrary")),
    )(*args)


ROW_TILE = 512


def _rows(w=D_MODEL, cb=0, tr=ROW_TILE):
    return pl.BlockSpec((tr, w), lambda i: (i, cb))


def _full(shape):
    return pl.BlockSpec(shape, lambda i: (0,) * len(shape))


def _row_call(body, name, T, in_specs, out_specs, out_shape, args, tr=ROW_TILE):
    return pl.pallas_call(body, name=name, grid=(T // tr,), in_specs=in_specs, out_specs=out_specs,
                          out_shape=out_shape, compiler_params=_params(("arbitrary",)))(*args)


def _sds(shape, dtype):
    return jax.ShapeDtypeStruct(shape, dtype)


def _rmsnorm_fwd(x, g):
    T = x.shape[0]

    def body(x_ref, g_ref, h_ref):
        xv = x_ref[...]
        r = lax.rsqrt(jnp.mean(xv * xv, axis=1, keepdims=True) + RMS_EPS)
        h_ref[...] = (xv * r * g_ref[...]).astype(BF16)

    return _row_call(body, "rmsnorm_fwd", T, [_rows(), _full((1, D_MODEL))], _rows(),
                     _sds((T, D_MODEL), BF16), (x, g))


def _post_fwd(x, y, g):
    T = x.shape[0]

    def body(x_ref, y_ref, g_ref, o_ref):
        yv = y_ref[...]
        r = lax.rsqrt(jnp.mean(yv * yv, axis=1, keepdims=True) + RMS_EPS)
        o_ref[...] = x_ref[...] + yv * r * g_ref[...]

    return _row_call(body, "post_fwd", T, [_rows(), _rows(), _full((1, D_MODEL))], _rows(),
                     _sds((T, D_MODEL), F32), (x, y, g))


def _rmsnorm_bwd(dout, xin, g, add, out_dtype, name):
    T = xin.shape[0]

    def body(*refs):
        if add is None:
            d_ref, x_ref, g_ref, dx_ref, dg_ref = refs
        else:
            d_ref, x_ref, g_ref, add_ref, dx_ref, dg_ref = refs

        @pl.when(pl.program_id(0) == 0)
        def _():
            dg_ref[...] = jnp.zeros_like(dg_ref)

        xv = x_ref[...]
        dv = d_ref[...].astype(F32)
        r = lax.rsqrt(jnp.mean(xv * xv, axis=1, keepdims=True) + RMS_EPS)
        xh = xv * r
        dg_ref[...] += jnp.sum(dv * xh, axis=0, keepdims=True)
        dn = dv * g_ref[...]
        dx = r * (dn - xh * jnp.mean(dn * xh, axis=1, keepdims=True))
        if add is not None:
            dx = dx + add_ref[...]
        dx_ref[...] = dx.astype(out_dtype)

    in_specs = [_rows(), _rows(), _full((1, D_MODEL))]
    args = [dout, xin, g]
    if add is not None:
        in_specs.append(_rows())
        args.append(add)
    return _row_call(body, name, T, in_specs, [_rows(), _full((1, D_MODEL))],
                     [_sds((T, D_MODEL), out_dtype), _sds((1, D_MODEL), F32)], args)


def _ple_fwd(x1, pe, gl):
    T = x1.shape[0]

    def body(x_ref, pe_ref, gl_ref, o_ref):
        o_ref[...] = x_ref[...] + pe_ref[...] * _sigmoid(gl_ref[...])

    return _row_call(body, "ple_fwd", T, [_rows()] * 3, _rows(), _sds((T, D_MODEL), F32), (x1, pe, gl))


def _ple_bwd(dx2, pe, gl):
    T = dx2.shape[0]

    def body(d_ref, pe_ref, gl_ref, dpe_ref, dgl_ref):
        dv = d_ref[...]
        sg = _sigmoid(gl_ref[...])
        dpe_ref[...] = (dv * sg).astype(BF16)
        dgl_ref[...] = (dv * pe_ref[...] * sg * (1.0 - sg)).astype(BF16)

    return _row_call(body, "ple_bwd", T, [_rows()] * 3, [_rows()] * 2,
                     [_sds((T, D_MODEL), BF16)] * 2, (dx2, pe, gl))


def _loss_fwd_bwd(y, target):
    T = y.shape[0]

    def body(y_ref, t_ref, s_ref, d_ref):
        @pl.when(pl.program_id(0) == 0)
        def _():
            s_ref[...] = jnp.zeros_like(s_ref)

        e = y_ref[...] - t_ref[...]
        s_ref[...] += jnp.sum(e * e).reshape(1, 1)
        d_ref[...] = e * (1.0 / D_MODEL)

    return _row_call(body, "loss", T, [_rows()] * 2, [_full((1, 1)), _rows()],
                     [_sds((1, 1), F32), _sds((T, D_MODEL), F32)], (y, target))


def _attn_combine(outs, lses, proj):
    T = proj.shape[0]

    def body(o0, o1, o2, l0, l1, l2, z_ref, a_ref, o_ref, lse_ref):
        a0, a1, a2 = l0[...], l1[...], l2[...]
        m = jnp.maximum(jnp.maximum(a0, a1), a2)
        e0, e1, e2 = jnp.exp(a0 - m), jnp.exp(a1 - m), jnp.exp(a2 - m)
        ssum = e0 + e1 + e2
        o = (e0 * o0[...] + e1 * o1[...] + e2 * o2[...]) / ssum
        zv = z_ref[...].astype(F32)
        o_ref[...] = o
        lse_ref[...] = m + jnp.log(ssum)
        a_ref[...] = (o * zv * _sigmoid(zv)).astype(BF16)

    return _row_call(body, "attn_combine", T, [_rows()] * 6 + [_rows(cb=3 * N_GROUPS)], [_rows()] * 3,
                     [_sds((T, D_MODEL), BF16), _sds((T, D_MODEL), F32), _sds((T, D_MODEL), F32)],
                     (*outs, *lses, proj))


def _gate_bwd(da, o, proj):
    T = da.shape[0]

    def body(da_ref, o_ref, z_ref, do_ref, dz_ref):
        dv = da_ref[...]
        zv = z_ref[...]
        sg = _sigmoid(zv)
        do_ref[...] = dv * zv * sg
        dz_ref[...] = dv * o_ref[...] * sg * (1.0 + zv * (1.0 - sg))

    zcols = _rows(cb=3 * N_GROUPS)
    return _row_call(body, "gate_bwd", T, [_rows(), _rows(), zcols], [_rows(), zcols],
                     [_sds((T, D_MODEL), F32), _sds((T, PROJ_COLS), F32)], (da, o, proj))


def _rope_tables(positions):
    inv_freq = 1.0 / (ROPE_THETA ** (jnp.arange(0, HEAD_DIM, 2, dtype=F32) / HEAD_DIM))
    ang = positions.astype(F32)[..., None] * inv_freq
    cos, sin = jnp.cos(ang), jnp.sin(ang)
    return jnp.tile(cos, (1, 1, 4)), jnp.concatenate([-sin, sin, -sin, sin], axis=-1)


def _rotate_half_partner(t):
    lane = lax.broadcasted_iota(jnp.int32, t.shape, 1)
    return jnp.where((lane % HEAD_DIM) < HEAD_DIM // 2,
                     pltpu.roll(t, LANES - HEAD_DIM // 2, 1), pltpu.roll(t, HEAD_DIM // 2, 1))


def _mask_bias(first):
    qi = lax.broadcasted_iota(jnp.int32, (N_BACK, 2 * N_BACK), 0)
    kj = lax.broadcasted_iota(jnp.int32, (N_BACK, 2 * N_BACK), 1)
    ok = (kj >= qi) & (kj <= qi + N_BACK)
    if first:
        ok = ok & (kj >= N_BACK)
    return jnp.where(ok, 0.0, NEG_INF).astype(F32)


def _stack_heads(t, head0):
    zero = jnp.zeros_like(t)
    return jnp.concatenate([jnp.where(head0, t, zero), jnp.where(head0, zero, t)], axis=0)


def _unstack_heads(t2, head0):
    return jnp.where(head0, t2[:N_BACK], t2[N_BACK:])


def _block_loop(nb, block):
    first = _mask_bias(True)
    block(0, jnp.concatenate([first, first], axis=0))
    if nb > 1:
        rest = _mask_bias(False)
        bias = jnp.concatenate([rest, rest], axis=0)

        def step(n, carry):
            block(n, bias)
            return carry

        lax.fori_loop(1, nb, step, 0, unroll=2)


_NT = (((1,), (1,)), ((), ()))
_TN = (((0,), (0,)), ((), ()))


def _residue_rows(r, i, d):
    start = r + i * (N_BACK * d)
    if d == 1:
        return pl.ds(pl.multiple_of(start, N_BACK), N_BACK)
    return pl.ds(start, N_BACK, stride=d)


def _seq_rows(i):
    return pl.ds(pl.multiple_of(i * N_BACK, N_BACK), N_BACK)


def _rows_at(base, i, size=N_BACK):
    return pl.ds(pl.multiple_of(base + i * N_BACK, N_BACK), size)


def _attn_fwd(proj, cos, sin, group, Bl, S):
    d = WIN_DIL[group][1]
    L = S // d
    nb = L // N_BACK
    P = L + N_BACK
    assert WIN_DIL[group][0] // d == N_BACK and L % N_BACK == 0

    def body(q_ref, k_ref, v_ref, cos_ref, sin_ref, o_ref, lse_ref, qr, kr, vp):
        head0 = lax.broadcasted_iota(jnp.int32, (1, LANES), 1) < HEAD_DIM
        zeros = jnp.zeros((N_BACK, LANES), BF16)

        def residue(r, carry):
            qbase, kbase = r * L, r * P
            kr[_rows_at(kbase, 0), :] = zeros
            vp[_rows_at(kbase, 0), :] = zeros

            def rope(i, carry2):
                rows = _residue_rows(r, i, d)
                cs, sn = cos_ref[rows, :], sin_ref[rows, :]
                q, k = q_ref[rows, :], k_ref[rows, :]
                qr[_rows_at(qbase, i), :] = ((q * cs + _rotate_half_partner(q) * sn)
                                            * (HEAD_DIM ** -0.5)).astype(BF16)
                kr[_rows_at(kbase, i + 1), :] = (k * cs + _rotate_half_partner(k) * sn).astype(BF16)
                vp[_rows_at(kbase, i + 1), :] = v_ref[rows, :].astype(BF16)
                return carry2

            lax.fori_loop(0, nb, rope, 0)

            def block(n, bias):
                win = _rows_at(kbase, n, 2 * N_BACK)
                q2, kw, vw = _stack_heads(qr[_rows_at(qbase, n), :], head0), kr[win, :], vp[win, :]
                s = lax.dot_general(q2, kw, _NT, preferred_element_type=F32) + bias
                m = jnp.max(s, axis=1, keepdims=True)
                p = jnp.exp(s - m)
                l = jnp.sum(p, axis=1, keepdims=True)
                pv = jnp.dot(p.astype(BF16), vw, preferred_element_type=F32)
                rows = _residue_rows(r, n, d)
                o_ref[rows, :] = _unstack_heads(pv * (1.0 / l), head0)
                lse_ref[rows, :] = _unstack_heads((m + jnp.log(l)) + jnp.zeros((2 * N_BACK, LANES), F32), head0)

            _block_loop(nb, block)
            return carry

        lax.fori_loop(0, d, residue, 0)

    act = pl.BlockSpec((None, S, LANES), lambda b, hp: (b, 0, hp))
    tab = pl.BlockSpec((None, S, LANES), lambda b, hp: (b, 0, 0))
    col = lambda which: pl.BlockSpec((None, S, LANES),
                                     lambda b, hp: (b, 0, (which * N_GROUPS + group) * HEAD_PAIRS + hp))
    seq = lambda rows: pl.BlockSpec((None, None, rows, LANES), lambda b, hp: (b, hp, 0, 0))
    p3 = proj.reshape(Bl, S, PROJ_COLS)
    o, lse, qr, kr, vp = pl.pallas_call(
        body, name="attn_fwd_g%d" % group, grid=(Bl, HEAD_PAIRS),
        in_specs=[col(0), col(1), col(2), tab, tab], out_specs=[act, act, seq(S), seq(d * P), seq(d * P)],
        out_shape=[_sds((Bl, S, D_MODEL), F32)] * 2 + [_sds((Bl, HEAD_PAIRS, S, LANES), BF16)]
        + [_sds((Bl, HEAD_PAIRS, d * P, LANES), BF16)] * 2,
        compiler_params=_params(("parallel", "arbitrary"), VMEM_LIMIT_ATTN),
    )(p3, p3, p3, cos, sin)
    return o.reshape(Bl * S, D_MODEL), lse.reshape(Bl * S, D_MODEL), (qr, kr, vp)


def _attn_bwd(saved, cos, sin, do, o, lse, dproj, group, Bl, S):
    d = WIN_DIL[group][1]
    L = S // d
    nb = L // N_BACK
    P = L + N_BACK
    steps = Bl * HEAD_PAIRS

    def body(qr, kr, vp, cos_ref, sin_ref, do_ref, o_ref, lse_ref, dproj_in, dproj_ref,
             dk_acc, dv_acc, stage, sems):
        del dproj_in
        head0 = lax.broadcasted_iota(jnp.int32, (1, LANES), 1) < HEAD_DIM
        b, hp = pl.program_id(0), pl.program_id(1)
        step = b * HEAD_PAIRS + hp
        slot = step % 2
        dq_s, dk_s, dv_s = stage.at[slot, 0], stage.at[slot, 1], stage.at[slot, 2]

        def copies(which_slot):
            out = []
            for which in range(3):
                col = ((which * N_GROUPS + group) * HEAD_PAIRS + hp) * LANES
                out.append(pltpu.make_async_copy(
                    stage.at[which_slot, which], dproj_ref.at[b, :, pl.ds(pl.multiple_of(col, LANES), LANES)],
                    sems.at[which_slot, which]))
            return out

        @pl.when(step >= 2)
        def _():
            for cp in copies(slot):
                cp.wait()

        def residue(r, carry):
            qbase, kbase = r * L, r * P
            dk_acc[...] = jnp.zeros_like(dk_acc)
            dv_acc[...] = jnp.zeros_like(dv_acc)

            def block(n, bias):
                win = pl.ds(pl.multiple_of(n * N_BACK, N_BACK), 2 * N_BACK)
                kwin = _rows_at(kbase, n, 2 * N_BACK)
                rows = _residue_rows(r, n, d)
                q2, kw, vw = _stack_heads(qr[_rows_at(qbase, n), :], head0), kr[kwin, :], vp[kwin, :]
                dof = do_ref[rows, :]
                do2 = _stack_heads(dof.astype(BF16), head0)
                lse_b = lse_ref[rows, :]
                lse2 = jnp.concatenate([lse_b[:, 0:1], lse_b[:, HEAD_DIM:HEAD_DIM + 1]], axis=0)
                dsum = _stack_heads(dof * o_ref[rows, :], head0)
                delta = jnp.sum(dsum, axis=1, keepdims=True)
                s = lax.dot_general(q2, kw, _NT, preferred_element_type=F32) + bias
                p = jnp.exp(s - lse2)
                dp = lax.dot_general(do2, vw, _NT, preferred_element_type=F32)
                ds = (p * (dp - delta)).astype(BF16)
                dq = _unstack_heads(jnp.dot(ds, kw, preferred_element_type=F32), head0) * (HEAD_DIM ** -0.5)
                cs, sn = cos_ref[rows, :], sin_ref[rows, :]
                dq_s[rows, :] = dq * cs + _rotate_half_partner(dq * sn)
                dk_acc[win, :] += lax.dot_general(ds, q2, _TN, preferred_element_type=F32)
                dv_acc[win, :] += lax.dot_general(p.astype(BF16), do2, _TN, preferred_element_type=F32)

            _block_loop(nb, block)

            def finish(i, carry2):
                rows = _residue_rows(r, i, d)
                cs, sn = cos_ref[rows, :], sin_ref[rows, :]
                dk = dk_acc[_seq_rows(i + 1), :]
                dk_s[rows, :] = dk * cs + _rotate_half_partner(dk * sn)
                dv_s[rows, :] = dv_acc[_seq_rows(i + 1), :]
                return carry2

            lax.fori_loop(0, nb, finish, 0)
            return carry

        lax.fori_loop(0, d, residue, 0)
        for cp in copies(slot):
            cp.start()

        @pl.when(step == steps - 1)
        def _():
            if steps > 1:
                for cp in copies(1 - slot):
                    cp.wait()
            for cp in copies(slot):
                cp.wait()

    act = pl.BlockSpec((None, S, LANES), lambda b, hp: (b, 0, hp))
    tab = pl.BlockSpec((None, S, LANES), lambda b, hp: (b, 0, 0))
    seq = lambda rows: pl.BlockSpec((None, None, rows, LANES), lambda b, hp: (b, hp, 0, 0))
    view = lambda t: t.reshape(Bl, S, D_MODEL)
    out = pl.pallas_call(
        body, name="attn_bwd_g%d" % group, grid=(Bl, HEAD_PAIRS),
        in_specs=[seq(S), seq(d * P), seq(d * P), tab, tab, act, act, act, _ANY], out_specs=_ANY,
        out_shape=_sds((Bl, S, PROJ_COLS), F32), input_output_aliases={8: 0},
        scratch_shapes=[pltpu.VMEM((P, LANES), F32), pltpu.VMEM((P, LANES), F32),
                        pltpu.VMEM((2, 3, S, LANES), F32), pltpu.SemaphoreType.DMA((2, 3))],
        compiler_params=_params(("arbitrary", "arbitrary"), VMEM_LIMIT_ATTN),
    )(*saved, cos, sin, view(do), view(o), view(lse), dproj.reshape(Bl, S, PROJ_COLS))
    return out.reshape(Bl * S, PROJ_COLS)


CONV_TILE = 256
CONV_CHUNK = 64
SUBLANES = 8
CONV_SHIFT_ROWS = CONV_TILE + CONV_HALO - SUBLANES


def _fill_shifted(shifted, ext, cs):
    for k in range(1, SUBLANES):
        shifted[k - 1] = ext[pl.ds(k, CONV_SHIFT_ROWS), cs]


def _shifted_rows(shifted, ext, cs, off):
    k = off % SUBLANES
    if k == 0:
        return ext[pl.ds(off, CONV_CHUNK), cs]
    return shifted[k - 1, pl.ds(off - k, CONV_CHUNK), :]


def _conv_fwd(proj, z, dw, dwb, ln_g, ln_b, Bl, S):
    tr = CONV_TILE
    nj = S // tr
    hb = tr // CONV_HALO

    def body(a_ref, b_ref, ah_ref, bh_ref, z_ref, dw_ref, dwb_ref, g_ref, bb_ref, u1_ref, out_ref, ext, shifted):
        j = pl.program_id(1)
        halo = ah_ref[0].astype(F32) * _sigmoid(bh_ref[0].astype(F32))
        ext[pl.ds(0, CONV_HALO), :] = jnp.where(j > 0, halo, 0.0)
        ext[pl.ds(CONV_HALO, tr), :] = a_ref[0].astype(F32) * _sigmoid(b_ref[0].astype(F32))

        def cols(c, carry):
            cs = pl.ds(pl.multiple_of(c * LANES, LANES), LANES)
            _fill_shifted(shifted, ext, cs)
            for rc in range(tr // CONV_CHUNK):
                acc = jnp.zeros((CONV_CHUNK, LANES), F32)
                for w in range(CONV_WIDTH):
                    off = rc * CONV_CHUNK + CONV_HALO - (CONV_WIDTH - 1) + w
                    acc = acc + dw_ref[pl.ds(w, 1), cs] * _shifted_rows(shifted, ext, cs, off)
                u1_ref[0, pl.ds(rc * CONV_CHUNK, CONV_CHUNK), cs] = acc + dwb_ref[:, cs]
            return carry

        lax.fori_loop(0, D_MODEL // LANES, cols, 0)
        u1 = u1_ref[0]
        mu = jnp.mean(u1, axis=1, keepdims=True)
        xc = u1 - mu
        rstd = lax.rsqrt(jnp.mean(xc * xc, axis=1, keepdims=True) + LN_EPS)
        u2 = xc * rstd * g_ref[...] + bb_ref[...]
        zv = z_ref[0].astype(F32)
        out_ref[0] = (u2 * _sigmoid(u2) * zv * _sigmoid(zv)).astype(BF16)

    tile = lambda cb: pl.BlockSpec((1, tr, D_MODEL), lambda b, j: (b, j, cb))
    halo = lambda cb: pl.BlockSpec((1, CONV_HALO, D_MODEL), lambda b, j: (b, jnp.maximum(j * hb - 1, 0), cb))
    par = lambda r: pl.BlockSpec((r, D_MODEL), lambda b, j: (0, 0))
    p3 = proj.reshape(Bl, S, 2 * D_MODEL)
    u1, out = pl.pallas_call(
        body, name="conv_fwd", grid=(Bl, nj),
        in_specs=[tile(0), tile(1), halo(0), halo(1), tile(0), par(32), par(1), par(1), par(1)],
        out_specs=[tile(0), tile(0)],
        out_shape=[_sds((Bl, S, D_MODEL), F32), _sds((Bl, S, D_MODEL), BF16)],
        scratch_shapes=[pltpu.VMEM((tr + CONV_HALO, D_MODEL), F32),
                        pltpu.VMEM((SUBLANES - 1, CONV_SHIFT_ROWS, LANES), F32)],
        compiler_params=_params(("parallel", "arbitrary")),
    )(p3, p3, p3, p3, z.reshape(Bl, S, D_MODEL), dw, dwb, ln_g, ln_b)
    return u1.reshape(Bl * S, D_MODEL), out.reshape(Bl * S, D_MODEL)


def _conv_norm_bwd(da2, z, u1, ln_g, ln_b):
    T = da2.shape[0]

    def body(da_ref, z_ref, u_ref, g_ref, b_ref, du_ref, dz_ref, dg_ref, db_ref):
        @pl.when(pl.program_id(0) == 0)
        def _():
            dg_ref[...] = jnp.zeros_like(dg_ref)
            db_ref[...] = jnp.zeros_like(db_ref)

        u1 = u_ref[...]
        mu = jnp.mean(u1, axis=1, keepdims=True)
        xc = u1 - mu
        rstd = lax.rsqrt(jnp.mean(xc * xc, axis=1, keepdims=True) + LN_EPS)
        nrm = xc * rstd
        u2 = nrm * g_ref[...] + b_ref[...]
        s2 = _sigmoid(u2)
        zv = z_ref[...].astype(F32)
        sz = _sigmoid(zv)
        dv = da_ref[...]
        dz_ref[...] = (dv * u2 * s2 * sz * (1.0 + zv * (1.0 - sz))).astype(BF16)
        du2 = dv * zv * sz * s2 * (1.0 + u2 * (1.0 - s2))
        dg_ref[...] += jnp.sum(du2 * nrm, axis=0, keepdims=True)
        db_ref[...] += jnp.sum(du2, axis=0, keepdims=True)
        dn = du2 * g_ref[...]
        du_ref[...] = rstd * (dn - jnp.mean(dn, axis=1, keepdims=True)
                              - nrm * jnp.mean(dn * nrm, axis=1, keepdims=True))

    return _row_call(body, "conv_norm_bwd", T,
                     [_rows(), _rows(), _rows(), _full((1, D_MODEL)), _full((1, D_MODEL))],
                     [_rows(), _rows(), _full((1, D_MODEL)), _full((1, D_MODEL))],
                     [_sds((T, D_MODEL), F32), _sds((T, D_MODEL), BF16), _sds((1, D_MODEL), F32),
                      _sds((1, D_MODEL), F32)], (da2, z, u1, ln_g, ln_b))


def _conv_bwd(proj, du1, dw, Bl, S):
    tr = CONV_TILE
    nj = S // tr
    hb = tr // CONV_HALO

    def body(a_ref, b_ref, ah_ref, bh_ref, du_ref, duh_ref, dw_ref, dab_ref, ddw_ref, ddb_ref, uext, dext, du0,
             ushift, dshift, ddw8):
        first = (pl.program_id(0) == 0) & (pl.program_id(1) == 0)
        last = (pl.program_id(0) == Bl - 1) & (pl.program_id(1) == nj - 1)
        j = pl.program_id(1)

        @pl.when(first)
        def _():
            ddw8[...] = jnp.zeros_like(ddw8)
            ddb_ref[...] = jnp.zeros_like(ddb_ref)

        halo = ah_ref[0].astype(F32) * _sigmoid(bh_ref[0].astype(F32))
        uext[pl.ds(0, CONV_HALO), :] = jnp.where(j > 0, halo, 0.0)
        av = a_ref[0].astype(F32)
        sb = _sigmoid(b_ref[0].astype(F32))
        uext[pl.ds(CONV_HALO, tr), :] = av * sb
        dext[pl.ds(0, tr), :] = du_ref[0]
        dext[pl.ds(tr, CONV_HALO), :] = jnp.where(j < nj - 1, duh_ref[0], 0.0)
        ddb_ref[...] += jnp.sum(du_ref[0], axis=0, keepdims=True)

        def cols(c, carry):
            cs = pl.ds(pl.multiple_of(c * LANES, LANES), LANES)
            _fill_shifted(dshift, dext, cs)
            _fill_shifted(ushift, uext, cs)
            for rc in range(tr // CONV_CHUNK):
                base = rc * CONV_CHUNK
                acc = jnp.zeros((CONV_CHUNK, LANES), F32)
                for w in range(CONV_WIDTH):
                    acc = acc + dw_ref[pl.ds(w, 1), cs] * _shifted_rows(dshift, dext, cs, base + CONV_WIDTH - 1 - w)
                du0[pl.ds(base, CONV_CHUNK), cs] = acc
            for w in range(CONV_WIDTH):
                part = jnp.zeros((SUBLANES, LANES), F32)
                for rc in range(tr // CONV_CHUNK):
                    base = rc * CONV_CHUNK
                    prod = dext[pl.ds(base, CONV_CHUNK), cs] * _shifted_rows(
                        ushift, uext, cs, base + CONV_HALO - (CONV_WIDTH - 1) + w)
                    for i in range(CONV_CHUNK // SUBLANES):
                        part = part + prod[i * SUBLANES:(i + 1) * SUBLANES]
                ddw8[pl.ds(w * SUBLANES, SUBLANES), cs] += part
            return carry

        lax.fori_loop(0, D_MODEL // LANES, cols, 0)
        g = du0[...]
        dab_ref[0, :, 0:D_MODEL] = (g * sb).astype(BF16)
        dab_ref[0, :, D_MODEL:2 * D_MODEL] = (g * av * sb * (1.0 - sb)).astype(BF16)

        @pl.when(last)
        def _():
            for w in range(CONV_WIDTH + 1):
                ddw_ref[pl.ds(w, 1), :] = jnp.sum(ddw8[pl.ds(w * SUBLANES, SUBLANES), :], axis=0, keepdims=True)

    tile = lambda cb: pl.BlockSpec((1, tr, D_MODEL), lambda b, j: (b, j, cb))
    halo = lambda cb: pl.BlockSpec((1, CONV_HALO, D_MODEL), lambda b, j: (b, jnp.maximum(j * hb - 1, 0), cb))
    nxt = pl.BlockSpec((1, CONV_HALO, D_MODEL), lambda b, j: (b, jnp.minimum((j + 1) * hb, S // CONV_HALO - 1), 0))
    par = lambda r: pl.BlockSpec((r, D_MODEL), lambda b, j: (0, 0))
    p3 = proj.reshape(Bl, S, 2 * D_MODEL)
    d3 = du1.reshape(Bl, S, D_MODEL)
    dab, ddw, ddb = pl.pallas_call(
        body, name="conv_bwd", grid=(Bl, nj),
        in_specs=[tile(0), tile(1), halo(0), halo(1), tile(0), nxt, par(32)],
        out_specs=[pl.BlockSpec((1, tr, 2 * D_MODEL), lambda b, j: (b, j, 0)), par(32), par(1)],
        out_shape=[_sds((Bl, S, 2 * D_MODEL), BF16), _sds((32, D_MODEL), F32), _sds((1, D_MODEL), F32)],
        scratch_shapes=[pltpu.VMEM((tr + CONV_HALO, D_MODEL), F32), pltpu.VMEM((tr + CONV_HALO, D_MODEL), F32),
                        pltpu.VMEM((tr, D_MODEL), F32),
                        pltpu.VMEM((SUBLANES - 1, CONV_SHIFT_ROWS, LANES), F32),
                        pltpu.VMEM((SUBLANES - 1, CONV_SHIFT_ROWS, LANES), F32),
                        pltpu.VMEM(((CONV_WIDTH + 1) * SUBLANES, D_MODEL), F32)],
        compiler_params=_params(("arbitrary", "arbitrary")),
    )(p3, p3, p3, p3, d3, d3, dw)
    return dab.reshape(Bl * S, 2 * D_MODEL), ddw, ddb


_LAYOUT = (
    ("pre_norm_g", (4, 1024), None), ("post_norm_g", (4, 1024), None),
    ("attn_w_in", (2, 1024, 2560), 2), ("attn_w_out", (2, 256, 1024), 1),
    ("conv_w_in", (2, 1024, 768), 2), ("conv_dw_w", (2, 31, 256), 2),
    ("conv_dw_b", (2, 256), 1), ("conv_ln_g", (2, 256), 1), ("conv_ln_b", (2, 256), 1),
    ("conv_w_out", (2, 256, 1024), 1), ("ple_w_proj", (4, 256, 256), 2), ("ple_w_gate", (4, 256, 1024), 1),
)
_MATMUL_WEIGHTS = ("attn_w_in", "attn_w_out", "conv_w_in", "conv_w_out", "ple_w_proj", "ple_w_gate")
_SMALL_WEIGHTS = ("conv_dw_w", "conv_dw_b", "conv_ln_g", "conv_ln_b")
_SHAPE = {n: s for n, s, _ in _LAYOUT}
_AXIS = {n: a for n, _, a in _LAYOUT}


def _size(shape):
    n = 1
    for s in shape:
        n *= s
    return n


def _padded_rows(shape):
    rows = _size(shape) // shape[-1]
    return rows + (-rows) % FLAT_ROW_ALIGN


def _rows2d(a):
    a2 = a.reshape(-1, a.shape[-1])
    pad = _padded_rows(a.shape) - a2.shape[0]
    return jnp.pad(a2, ((0, pad), (0, 0))) if pad else a2


def _col_blocks(a):
    a2 = _rows2d(a)
    return jnp.concatenate([a2[:, c:c + FLAT_COLS] for c in range(0, a2.shape[1], FLAT_COLS)], axis=0)


def _from_col_blocks(flat, off, shape):
    rows, nblk = _padded_rows(shape), shape[-1] // FLAT_COLS
    a2 = jnp.concatenate([flat[off + b * rows:off + (b + 1) * rows] for b in range(nblk)], axis=1)
    return a2[:_size(shape) // shape[-1]].reshape(shape), off + nblk * rows


def _shard_col_blocks(full, shape, axis):
    if axis is None:
        blocks = _col_blocks(full)
        return jnp.broadcast_to(blocks[None], (N_CHIPS,) + blocks.shape)
    m = shape[-1]
    if axis == len(shape) - 1:
        a2 = _rows2d(full)
        pieces = [a2[:, c:c + FLAT_COLS] for c in range(0, N_CHIPS * m, FLAT_COLS)]
    else:
        layers, r, _ = shape
        assert axis == 1 and (layers * r) % FLAT_ROW_ALIGN == 0
        pieces = [full[:, s * r:(s + 1) * r, c:c + FLAT_COLS].reshape(layers * r, FLAT_COLS)
                  for s in range(N_CHIPS) for c in range(0, m, FLAT_COLS)]
    return jnp.concatenate(pieces, axis=0).reshape(N_CHIPS, -1, FLAT_COLS)


def _unshard_col_blocks(g, off, shape, axis):
    rows, nblk = _padded_rows(shape), shape[-1] // FLAT_COLS
    block = lambda s, b: g[s, off + b * rows:off + (b + 1) * rows]
    if axis == len(shape) - 1:
        a2 = jnp.concatenate([block(s, b) for s in range(N_CHIPS) for b in range(nblk)], axis=1)
        full = a2[:_size(shape) // shape[-1]].reshape(shape[:-1] + (N_CHIPS * shape[-1],))
    else:
        layers, r, m = shape
        shards = [jnp.concatenate([block(s, b) for b in range(nblk)], axis=1).reshape(layers, r, m)
                  for s in range(N_CHIPS)]
        full = jnp.concatenate(shards, axis=1)
    return full, off + nblk * rows


def _pack_rows(pieces):
    flat = jnp.concatenate(pieces, axis=0)
    assert flat.shape[0] <= FLAT_ROWS
    return jnp.pad(flat, ((0, FLAT_ROWS - flat.shape[0]), (0, 0)))


def _pack_f32(params):
    return _pack_rows([_col_blocks(params[n]) for n, _, _ in _LAYOUT])


def _unpack_f32(flat):
    out, off = {}, 0
    for n, shape, _ in _LAYOUT:
        out[n], off = _from_col_blocks(flat, off, shape)
    return out


def _bytes_shape(shape):
    return shape[:-1] + (4 * shape[-1],)


def _f32_to_bytes(a):
    u = lax.bitcast_convert_type(a, jnp.uint32)
    parts = jnp.stack([(u >> s) & 0xFF for s in (0, 8, 16, 24)], axis=-1)
    return parts.astype(F32).astype(BF16).reshape(_bytes_shape(a.shape))


def _bytes_to_f32(b, shape):
    u = b.reshape(shape + (4,)).astype(F32).astype(jnp.uint32)
    return lax.bitcast_convert_type(u[..., 0] | (u[..., 1] << 8) | (u[..., 2] << 16) | (u[..., 3] << 24), F32)


def _pack_gather_payload(w):
    pieces = [_col_blocks(w[n].astype(BF16)) for n in _MATMUL_WEIGHTS]
    pieces += [_col_blocks(_f32_to_bytes(w[n])) for n in _SMALL_WEIGHTS]
    return _pack_rows(pieces)


def _unpack_gathered(g):
    out, off = {}, 0
    for n in _MATMUL_WEIGHTS:
        out[n], off = _unshard_col_blocks(g, off, _SHAPE[n], _AXIS[n])
    for n in _SMALL_WEIGHTS:
        parts = [_from_col_blocks(g[s], off, _bytes_shape(_SHAPE[n])) for s in range(N_CHIPS)]
        out[n] = jnp.concatenate([_bytes_to_f32(part, _SHAPE[n]) for part, _ in parts], axis=_AXIS[n])
        off = parts[0][1]
    return out


def _pack_full_grads(grads):
    flat = jnp.concatenate([_shard_col_blocks(grads[n], shape, axis) for n, shape, axis in _LAYOUT], axis=1)
    assert flat.shape[1] <= FLAT_ROWS
    return jnp.pad(flat, ((0, 0), (0, FLAT_ROWS - flat.shape[1]), (0, 0)))


_ANY = pl.BlockSpec(memory_space=pl.ANY)


def _mesh_pos():
    return lax.axis_index("x"), lax.axis_index("y"), lax.axis_index("c")


def _other_chips(x, y):
    return [(1 - x, y), (x, 1 - y), (1 - x, 1 - y)]


def _allgather_weights(wl):
    R, C = wl.shape
    H = R // 2

    def body(w_ref, out_ref, send_sems, recv_sems):
        x, y, c = _mesh_pos()
        me, sibling = (x, y, c), (x, y, 1 - c)
        chips = _other_chips(x, y)

        def half(px, py, pc):
            return out_ref.at[2 * px + py, pl.ds(pc * H, H), :]

        def copy(k, block, to, src=None):
            return pltpu.make_async_remote_copy(
                src_ref=half(*block) if src is None else src, dst_ref=half(*block),
                send_sem=send_sems.at[k], recv_sem=recv_sems.at[k], device_id=to, device_id_type=MESH)

        own = pltpu.make_async_remote_copy(
            src_ref=w_ref, dst_ref=out_ref.at[2 * x + y], send_sem=send_sems.at[6], recv_sem=recv_sems.at[6],
            device_id=sibling, device_id_type=MESH)
        own.start()
        first = [copy(j, me, (*chip, c), src=w_ref.at[pl.ds(c * H, H), :]) for j, chip in enumerate(chips)]
        for cp in first:
            cp.start()
        passed = [copy(3 + j, (*chip, c), sibling) for j, chip in enumerate(chips)]
        for j, chip in enumerate(chips):
            copy(j, (*chip, c), me).wait_recv()
            passed[j].start()
        for j, chip in enumerate(chips):
            copy(3 + j, (*chip, 1 - c), me).wait_recv()
        own.wait()
        for cp in first + passed:
            cp.wait_send()

    return pl.pallas_call(
        body, name="allgather_weights", in_specs=[_ANY], out_specs=_ANY,
        out_shape=_sds((N_CHIPS, R, C), wl.dtype),
        scratch_shapes=[pltpu.SemaphoreType.DMA((7,)), pltpu.SemaphoreType.DMA((7,))],
    )(wl)


def _exchange_core_halves(g):
    n, _, H, C = g.shape

    def body(g_ref, got_ref, send_sem, recv_sem):
        x, y, c = _mesh_pos()
        swap = pltpu.make_async_remote_copy(
            src_ref=g_ref.at[pl.ds(0, n), 1 - c], dst_ref=got_ref, send_sem=send_sem, recv_sem=recv_sem,
            device_id=(x, y, 1 - c), device_id_type=MESH)
        swap.start()
        swap.wait()

    return pl.pallas_call(
        body, name="exchange_core_halves", in_specs=[_ANY], out_specs=_ANY,
        out_shape=_sds((n, H, C), g.dtype),
        scratch_shapes=[pltpu.SemaphoreType.DMA, pltpu.SemaphoreType.DMA],
    )(g)


def _scatter_to_chips(p):
    n, H, C = p.shape

    def body(p_ref, q_ref, send_sems, recv_sems):
        x, y, c = _mesh_pos()
        chips = _other_chips(x, y)
        sends = [pltpu.make_async_remote_copy(
            src_ref=p_ref.at[2 * cx + cy], dst_ref=q_ref.at[j], send_sem=send_sems.at[j],
            recv_sem=recv_sems.at[j], device_id=(cx, cy, c), device_id_type=MESH)
            for j, (cx, cy) in enumerate(chips)]
        for cp in sends:
            cp.start()
        for cp in sends:
            cp.wait_recv()
        for cp in sends:
            cp.wait_send()

    return pl.pallas_call(
        body, name="scatter_to_chips", in_specs=[_ANY], out_specs=_ANY,
        out_shape=_sds((n - 1, H, C), p.dtype),
        scratch_shapes=[pltpu.SemaphoreType.DMA((3,)), pltpu.SemaphoreType.DMA((3,))],
    )(p)


def _share_core_halves(r2):
    _, H, C = r2.shape

    def body(r_ref, out_ref, send_sem, recv_sem):
        x, y, c = _mesh_pos()
        send = pltpu.make_async_remote_copy(
            src_ref=r_ref.at[c], dst_ref=out_ref.at[c], send_sem=send_sem, recv_sem=recv_sem,
            device_id=(x, y, 1 - c), device_id_type=MESH)
        send.start()
        send.wait_send()
        pltpu.make_async_remote_copy(
            src_ref=r_ref.at[c], dst_ref=out_ref.at[1 - c], send_sem=send_sem, recv_sem=recv_sem,
            device_id=(x, y, 1 - c), device_id_type=MESH).wait_recv()

    return pl.pallas_call(
        body, name="share_core_halves", in_specs=[_ANY], out_specs=_ANY,
        out_shape=_sds(r2.shape, r2.dtype), input_output_aliases={0: 0},
        scratch_shapes=[pltpu.SemaphoreType.DMA, pltpu.SemaphoreType.DMA],
    )(r2)


def _place():
    x, y, c = _mesh_pos()
    return jnp.stack([c, 2 * x + y]).astype(jnp.int32)


def _sum_pair(g, got, place):
    n, _, H, C = g.shape

    def body(place_ref, a_ref, b_ref, o_ref):
        o_ref[...] = (a_ref[...] + b_ref[...]).astype(BF16)

    spec = pl.BlockSpec((1, FLAT_TILE, C), lambda s, i, pr: (s, i, 0))
    return pl.pallas_call(
        body, name="sum_core_pair",
        grid_spec=pltpu.PrefetchScalarGridSpec(
            num_scalar_prefetch=1, grid=(n, H // FLAT_TILE),
            in_specs=[pl.BlockSpec((1, None, FLAT_TILE, C), lambda s, i, pr: (s, pr[0], i, 0)), spec],
            out_specs=spec),
        out_shape=_sds((n, H, C), BF16),
        compiler_params=_params(("parallel", "parallel")))(place, g, got)


def _sum_chips(p, q, place):
    n, H, C = p.shape

    def body(place_ref, own_ref, qx_ref, qy_ref, qxy_ref, o_ref):
        mine = place_ref[1]
        own, qx, qy, qxy = (t[0].astype(F32) for t in (own_ref, qx_ref, qy_ref, qxy_ref))

        def term(s):
            rel = jnp.full(own.shape, mine ^ s, jnp.int32)
            return jnp.where(rel == 0, own, jnp.where(rel == 2, qx, jnp.where(rel == 1, qy, qxy)))

        o_ref[0] = ((term(0) + term(1)) + term(2)) + term(3)

    qspec = lambda j: pl.BlockSpec((1, FLAT_TILE, C), lambda i, pr: (j, i, 0))
    return pl.pallas_call(
        body, name="sum_chips",
        grid_spec=pltpu.PrefetchScalarGridSpec(
            num_scalar_prefetch=1, grid=(H // FLAT_TILE,),
            in_specs=[pl.BlockSpec((1, FLAT_TILE, C), lambda i, pr: (pr[1], i, 0)), qspec(0), qspec(1), qspec(2)],
            out_specs=pl.BlockSpec((1, FLAT_TILE, C), lambda i, pr: (pr[0], i, 0))),
        out_shape=_sds((2, H, C), F32),
        compiler_params=_params(("parallel",)))(place, p, q, q, q)


def _adamw(w, g, m, v):
    R, C = w.shape

    def body(w_ref, g_ref, m_ref, v_ref, d_ref, nm_ref, nv_ref):
        gv = g_ref[...]
        nm = ADAM_B1 * m_ref[...] + (1.0 - ADAM_B1) * gv
        nv = ADAM_B2 * v_ref[...] + (1.0 - ADAM_B2) * (gv * gv)
        m_hat = nm / (1.0 - ADAM_B1 ** ADAM_STEP)
        v_hat = nv / (1.0 - ADAM_B2 ** ADAM_STEP)
        d_ref[...] = -ADAM_LR * (m_hat / (jnp.sqrt(v_hat) + ADAM_EPS) + ADAM_WD * w_ref[...])
        nm_ref[...] = nm
        nv_ref[...] = nv

    spec = pl.BlockSpec((FLAT_TILE, C), lambda i: (i, 0))
    return pl.pallas_call(body, name="adamw", grid=(R // FLAT_TILE,), in_specs=[spec] * 4, out_specs=[spec] * 3,
                          out_shape=[_sds((R, C), F32)] * 3, compiler_params=_params(("parallel",)))(w, g, m, v)


def _reduce_scatter_grads(gfull):
    n, R, C = gfull.shape
    place = _place()
    g4 = gfull.reshape(n, 2, R // 2, C)
    p = _sum_pair(g4, _exchange_core_halves(g4), place)
    return _share_core_halves(_sum_chips(p, _scatter_to_chips(p), place)).reshape(R, C)


def _local_step(x, p, positions, loss_target, pre_g, post_g, w):
    Bl, S, _ = x.shape
    T = Bl * S
    cos, sin = _rope_tables(positions)
    xs = x.reshape(T, D_MODEL)
    saved = []
    for i in range(DEPTH):
        j = i // 2
        g_pre, g_post = pre_g[i:i + 1], post_g[i:i + 1]
        h = _rmsnorm_fwd(xs, g_pre)
        st = {"x": xs, "h": h}
        if i % 2 == 0:
            proj = _mm(h, w["attn_w_in"][j], name="attn_in")
            res = [_attn_fwd(proj, cos, sin, g, Bl, S) for g in range(N_GROUPS)]
            a, o, lse = _attn_combine([r[0] for r in res], [r[1] for r in res], proj)
            y = _mm(a, w["attn_w_out"][j], name="attn_out")
            st.update(proj=proj, a=a, o=o, lse=lse, qkv=[r[2] for r in res])
        else:
            w_ab, w_z = w["conv_w_in"][j][:, :2 * D_MODEL], w["conv_w_in"][j][:, 2 * D_MODEL:]
            ab = _mm(h, w_ab, out_dtype=BF16, name="conv_in_ab")
            z = _mm(h, w_z, out_dtype=BF16, name="conv_in_z")
            dw = jnp.pad(w["conv_dw_w"][j], ((0, 1), (0, 0)))
            u1, a = _conv_fwd(ab, z, dw, w["conv_dw_b"][j:j + 1], w["conv_ln_g"][j:j + 1],
                              w["conv_ln_b"][j:j + 1], Bl, S)
            y = _mm(a, w["conv_w_out"][j], name="conv_out")
            st.update(w_ab=w_ab, w_z=w_z, ab=ab, z=z, dw=dw, u1=u1, a=a)
        x1 = _post_fwd(xs, y, g_post)
        pi = p[i].reshape(T, PLE_DIM)
        pe = _mm(pi, w["ple_w_proj"][i], name="ple_proj")
        gl = _mm(x1, w["ple_w_gate"][i], name="ple_gate")
        xs = _ple_fwd(x1, pe, gl)
        st.update(y=y, x1=x1, pi=pi, pe=pe, gl=gl)
        saved.append(st)

    sq, dx = _loss_fwd_bwd(xs, loss_target.reshape(T, D_MODEL))

    grads = {n: [None] * shape[0] for n, shape, _ in _LAYOUT}
    for i in reversed(range(DEPTH)):
        j = i // 2
        st = saved[i]
        g_pre, g_post = pre_g[i:i + 1], post_g[i:i + 1]
        dpe, dgl = _ple_bwd(dx, st["pe"], st["gl"])
        grads["ple_w_proj"][i] = _mm(st["pi"], dpe, ta=True, name="ple_proj_wgrad")
        grads["ple_w_gate"][i] = _mm(st["x1"], dgl, ta=True, name="ple_gate_wgrad")
        dx1 = _mm(dgl, w["ple_w_gate"][i], tb=True, add=dx, name="ple_gate_dgrad")
        dy, dg_post = _rmsnorm_bwd(dx1, st["y"], g_post, None, BF16, "post_bwd")
        grads["post_norm_g"][i] = dg_post[0]
        if i % 2 == 0:
            grads["attn_w_out"][j] = _mm(st["a"], dy, ta=True, name="attn_out_wgrad")
            da = _mm(dy, w["attn_w_out"][j], tb=True, name="attn_out_dgrad")
            do, dproj = _gate_bwd(da, st["o"], st["proj"])
            for g in range(N_GROUPS):
                dproj = _attn_bwd(st["qkv"][g], cos, sin, do, st["o"], st["lse"], dproj, g, Bl, S)
            dh = _mm(dproj, w["attn_w_in"][j], tb=True, tk=2048, name="attn_in_dgrad")
            grads["attn_w_in"][j] = _mm(st["h"], dproj, ta=True, name="attn_in_wgrad")
        else:
            grads["conv_w_out"][j] = _mm(st["a"], dy, ta=True, name="conv_out_wgrad")
            da2 = _mm(dy, w["conv_w_out"][j], tb=True, name="conv_out_dgrad")
            du1, dz, dln_g, dln_b = _conv_norm_bwd(da2, st["z"], st["u1"], w["conv_ln_g"][j:j + 1],
                                                   w["conv_ln_b"][j:j + 1])
            dab, ddw, ddb = _conv_bwd(st["ab"], du1, st["dw"], Bl, S)
            dh = _mm(dz, st["w_z"], tb=True, name="conv_in_z_dgrad")
            dh = _mm(dab, st["w_ab"], tb=True, add=dh, name="conv_in_ab_dgrad")
            dw_ab = _mm(st["h"], dab, ta=True, name="conv_in_ab_wgrad")
            dw_z = _mm(st["h"], dz, ta=True, name="conv_in_z_wgrad")
            grads["conv_w_in"][j] = jnp.concatenate([dw_ab, dw_z], axis=1)
            grads["conv_dw_w"][j] = ddw[:CONV_WIDTH]
            grads["conv_dw_b"][j] = ddb[0]
            grads["conv_ln_g"][j] = dln_g[0]
            grads["conv_ln_b"][j] = dln_b[0]
        dx, dg_pre = _rmsnorm_bwd(dh, st["x"], g_pre, dx1, F32, "pre_bwd")
        grads["pre_norm_g"][i] = dg_pre[0]
    grads = {n: jnp.stack(v) for n, v in grads.items()}
    return sq, dx.reshape(Bl, S, D_MODEL), grads


_NAMES = tuple(n for n, _, _ in _LAYOUT)


def kernel(x, p, positions, pre_norm_g, post_norm_g, attn_w_in, attn_w_out, conv_w_in, conv_dw_w, conv_dw_b, conv_ln_g, conv_ln_b, conv_w_out, ple_w_proj, ple_w_gate, loss_target, m_pre_norm_g, m_post_norm_g, m_attn_w_in, m_attn_w_out, m_conv_w_in, m_conv_dw_w, m_conv_dw_b, m_conv_ln_g, m_conv_ln_b, m_conv_w_out, m_ple_w_proj, m_ple_w_gate, v_pre_norm_g, v_post_norm_g, v_attn_w_in, v_attn_w_out, v_conv_w_in, v_conv_dw_w, v_conv_dw_b, v_conv_ln_g, v_conv_ln_b, v_conv_w_out, v_ple_w_proj, v_ple_w_gate):
    w_loc = dict(zip(_NAMES, (pre_norm_g, post_norm_g, attn_w_in, attn_w_out, conv_w_in, conv_dw_w, conv_dw_b,
                              conv_ln_g, conv_ln_b, conv_w_out, ple_w_proj, ple_w_gate)))
    m_loc = dict(zip(_NAMES, (m_pre_norm_g, m_post_norm_g, m_attn_w_in, m_attn_w_out, m_conv_w_in, m_conv_dw_w,
                              m_conv_dw_b, m_conv_ln_g, m_conv_ln_b, m_conv_w_out, m_ple_w_proj, m_ple_w_gate)))
    v_loc = dict(zip(_NAMES, (v_pre_norm_g, v_post_norm_g, v_attn_w_in, v_attn_w_out, v_conv_w_in, v_conv_dw_w,
                              v_conv_dw_b, v_conv_ln_g, v_conv_ln_b, v_conv_w_out, v_ple_w_proj, v_ple_w_gate)))

    w_full = _unpack_gathered(_allgather_weights(_pack_gather_payload(w_loc)))
    sq, grad_x, grads = _local_step(x, p, positions, loss_target, pre_norm_g, post_norm_g, w_full)
    loss = lax.psum(sq[0, 0] * (0.5 / D_MODEL), ("x", "y", "c"))

    g_flat = _reduce_scatter_grads(_pack_full_grads(grads))
    delta, new_m, new_v = _adamw(_pack_f32(w_loc), g_flat, _pack_f32(m_loc), _pack_f32(v_loc))
    g_out, d_out, m_out, v_out = (_unpack_f32(t) for t in (g_flat, delta, new_m, new_v))
    return (loss, grad_x, *[g_out[n] for n in _NAMES], *[d_out[n] for n in _NAMES],
            *[m_out[n] for n in _NAMES], *[v_out[n] for n in _NAMES])
```

```python
import jax
import jax.numpy as jnp
from jax import lax
from jax.experimental import pallas as pl
from jax.experimental.pallas import tpu as pltpu

F32 = jnp.float32
BF16 = jnp.bfloat16

D_MODEL = 1024
DEPTH = 4
PLE_DIM = 256
HEAD_DIM = 64
WIN_DIL = ((128, 1), (512, 4), (2048, 16))
N_GROUPS = 3
N_BACK = 128
BLOCK_UNROLL = 4
ROPE_THETA = 10000.0
CONV_WIDTH = 31
CONV_HALO = 32
RMS_EPS = 1e-6
LN_EPS = 1e-5
NEG_INF = -1e30
ADAM_LR, ADAM_B1, ADAM_B2, ADAM_EPS, ADAM_WD, ADAM_STEP = 0.001, 0.9, 0.999, 1e-08, 0.01, 10

LANES = 128
N_CHIPS = 4
VMEM_LIMIT = 48 * 1024 * 1024
VMEM_LIMIT_ATTN = 56 * 1024 * 1024
FLAT_COLS = 256
FLAT_ROWS = 36864
FLAT_TILE = 2048
FLAT_ROW_ALIGN = 16
PROJ_COLS = (3 * N_GROUPS + 1) * D_MODEL
HEAD_PAIRS = D_MODEL // LANES

MESH = pl.DeviceIdType.MESH


def _params(sem=None, vmem=VMEM_LIMIT):
    return pltpu.CompilerParams(dimension_semantics=sem, vmem_limit_bytes=vmem)


def _sigmoid(v):
    return 1.0 / (1.0 + jnp.exp(-v))


def _mm(a, b, *, ta=False, tb=False, add=None, out_dtype=F32, tm=1024, tn=1024, tk=1024, name="mm"):
    if ta:
        K, M = a.shape
    else:
        M, K = a.shape
    if tb:
        N, K2 = b.shape
    else:
        K2, N = b.shape
    assert K == K2, (a.shape, b.shape)
    tm, tn, tk = min(tm, M), min(tn, N), min(tk, K)
    assert M % tm == 0 and N % tn == 0 and K % tk == 0
    nk = K // tk
    dims = (((0 if ta else 1,), (1 if tb else 0,)), ((), ()))

    def body(*refs):
        if add is None:
            a_ref, b_ref, o_ref = refs[:3]
        else:
            a_ref, b_ref, add_ref, o_ref = refs[:4]
        k = pl.program_id(2)
        part = lax.dot_general(a_ref[...].astype(BF16), b_ref[...].astype(BF16), dims, preferred_element_type=F32)

        def finish(r):
            if add is not None:
                r = r + add_ref[...].astype(F32)
            o_ref[...] = r.astype(out_dtype)

        if nk == 1:
            finish(part)
        else:
            acc_ref = refs[-1]

            @pl.when(k == 0)
            def _():
                acc_ref[...] = part

            @pl.when((k > 0) & (k < nk - 1))
            def _():
                acc_ref[...] += part

            @pl.when(k == nk - 1)
            def _():
                finish(acc_ref[...] + part)

    a_spec = pl.BlockSpec((tk, tm), lambda i, j, k: (k, i)) if ta else pl.BlockSpec((tm, tk), lambda i, j, k: (i, k))
    b_spec = pl.BlockSpec((tn, tk), lambda i, j, k: (j, k)) if tb else pl.BlockSpec((tk, tn), lambda i, j, k: (k, j))
    o_spec = pl.BlockSpec((tm, tn), lambda i, j, k: (i, j))
    in_specs, args = [a_spec, b_spec], [a, b]
    if add is not None:
        in_specs.append(o_spec)
        args.append(add)
    return pl.pallas_call(
        body, name=name, grid=(M // tm, N // tn, nk),
        in_specs=in_specs, out_specs=o_spec,
        out_shape=jax.ShapeDtypeStruct((M, N), out_dtype),
        scratch_shapes=[pltpu.VMEM((tm, tn), F32)] if nk > 1 else [],
        compiler_params=_params(("parallel", "parallel", "arbitrary")),
    )(*args)


ROW_TILE = 512


def _rows(w=D_MODEL, cb=0, tr=ROW_TILE):
    return pl.BlockSpec((tr, w), lambda i: (i, cb))


def _full(shape):
    return pl.BlockSpec(shape, lambda i: (0,) * len(shape))


def _row_call(body, name, T, in_specs, out_specs, out_shape, args, tr=ROW_TILE):
    return pl.pallas_call(body, name=name, grid=(T // tr,), in_specs=in_specs, out_specs=out_specs,
                          out_shape=out_shape, compiler_params=_params(("arbitrary",)))(*args)


def _sds(shape, dtype):
    return jax.ShapeDtypeStruct(shape, dtype)


def _rmsnorm_fwd(x, g):
    T = x.shape[0]

    def body(x_ref, g_ref, h_ref):
        xv = x_ref[...]
        r = lax.rsqrt(jnp.mean(xv * xv, axis=1, keepdims=True) + RMS_EPS)
        h_ref[...] = (xv * r * g_ref[...]).astype(BF16)

    return _row_call(body, "rmsnorm_fwd", T, [_rows(), _full((1, D_MODEL))], _rows(),
                     _sds((T, D_MODEL), BF16), (x, g))


def _post_fwd(x, y, g):
    T = x.shape[0]

    def body(x_ref, y_ref, g_ref, o_ref):
        yv = y_ref[...]
        r = lax.rsqrt(jnp.mean(yv * yv, axis=1, keepdims=True) + RMS_EPS)
        o_ref[...] = x_ref[...] + yv * r * g_ref[...]

    return _row_call(body, "post_fwd", T, [_rows(), _rows(), _full((1, D_MODEL))], _rows(),
                     _sds((T, D_MODEL), F32), (x, y, g))


def _rmsnorm_bwd(dout, xin, g, add, out_dtype, name):
    T = xin.shape[0]

    def body(*refs):
        if add is None:
            d_ref, x_ref, g_ref, dx_ref, dg_ref = refs
        else:
            d_ref, x_ref, g_ref, add_ref, dx_ref, dg_ref = refs

        @pl.when(pl.program_id(0) == 0)
        def _():
            dg_ref[...] = jnp.zeros_like(dg_ref)

        xv = x_ref[...]
        dv = d_ref[...].astype(F32)
        r = lax.rsqrt(jnp.mean(xv * xv, axis=1, keepdims=True) + RMS_EPS)
        xh = xv * r
        dg_ref[...] += jnp.sum(dv * xh, axis=0, keepdims=True)
        dn = dv * g_ref[...]
        dx = r * (dn - xh * jnp.mean(dn * xh, axis=1, keepdims=True))
        if add is not None:
            dx = dx + add_ref[...]
        dx_ref[...] = dx.astype(out_dtype)

    in_specs = [_rows(), _rows(), _full((1, D_MODEL))]
    args = [dout, xin, g]
    if add is not None:
        in_specs.append(_rows())
        args.append(add)
    return _row_call(body, name, T, in_specs, [_rows(), _full((1, D_MODEL))],
                     [_sds((T, D_MODEL), out_dtype), _sds((1, D_MODEL), F32)], args)


def _ple_fwd(x1, pe, gl):
    T = x1.shape[0]

    def body(x_ref, pe_ref, gl_ref, o_ref):
        o_ref[...] = x_ref[...] + pe_ref[...] * _sigmoid(gl_ref[...])

    return _row_call(body, "ple_fwd", T, [_rows()] * 3, _rows(), _sds((T, D_MODEL), F32), (x1, pe, gl))


def _ple_bwd(dx2, pe, gl):
    T = dx2.shape[0]

    def body(d_ref, pe_ref, gl_ref, dpe_ref, dgl_ref):
        dv = d_ref[...]
        sg = _sigmoid(gl_ref[...])
        dpe_ref[...] = (dv * sg).astype(BF16)
        dgl_ref[...] = (dv * pe_ref[...] * sg * (1.0 - sg)).astype(BF16)

    return _row_call(body, "ple_bwd", T, [_rows()] * 3, [_rows()] * 2,
                     [_sds((T, D_MODEL), BF16)] * 2, (dx2, pe, gl))


def _loss_fwd_bwd(y, target):
    T = y.shape[0]

    def body(y_ref, t_ref, s_ref, d_ref):
        @pl.when(pl.program_id(0) == 0)
        def _():
            s_ref[...] = jnp.zeros_like(s_ref)

        e = y_ref[...] - t_ref[...]
        s_ref[...] += jnp.sum(e * e).reshape(1, 1)
        d_ref[...] = e * (1.0 / D_MODEL)

    return _row_call(body, "loss", T, [_rows()] * 2, [_full((1, 1)), _rows()],
                     [_sds((1, 1), F32), _sds((T, D_MODEL), F32)], (y, target))


def _attn_combine(outs, lses, proj):
    T = proj.shape[0]

    def body(o0, o1, o2, l0, l1, l2, z_ref, a_ref, o_ref, lse_ref):
        a0, a1, a2 = l0[...], l1[...], l2[...]
        m = jnp.maximum(jnp.maximum(a0, a1), a2)
        e0, e1, e2 = jnp.exp(a0 - m), jnp.exp(a1 - m), jnp.exp(a2 - m)
        ssum = e0 + e1 + e2
        o = (e0 * o0[...] + e1 * o1[...] + e2 * o2[...]) / ssum
        zv = z_ref[...].astype(F32)
        o_ref[...] = o
        lse_ref[...] = m + jnp.log(ssum)
        a_ref[...] = (o * zv * _sigmoid(zv)).astype(BF16)

    return _row_call(body, "attn_combine", T, [_rows()] * 6 + [_rows(cb=3 * N_GROUPS)], [_rows()] * 3,
                     [_sds((T, D_MODEL), BF16), _sds((T, D_MODEL), F32), _sds((T, D_MODEL), F32)],
                     (*outs, *lses, proj))


def _gate_bwd(da, o, proj):
    T = da.shape[0]

    def body(da_ref, o_ref, z_ref, do_ref, dz_ref):
        dv = da_ref[...]
        zv = z_ref[...]
        sg = _sigmoid(zv)
        do_ref[...] = dv * zv * sg
        dz_ref[...] = dv * o_ref[...] * sg * (1.0 + zv * (1.0 - sg))

    zcols = _rows(cb=3 * N_GROUPS)
    return _row_call(body, "gate_bwd", T, [_rows(), _rows(), zcols], [_rows(), zcols],
                     [_sds((T, D_MODEL), F32), _sds((T, PROJ_COLS), F32)], (da, o, proj))


def _rope_tables(positions):
    inv_freq = 1.0 / (ROPE_THETA ** (jnp.arange(0, HEAD_DIM, 2, dtype=F32) / HEAD_DIM))
    ang = positions.astype(F32)[..., None] * inv_freq
    cos, sin = jnp.cos(ang), jnp.sin(ang)
    return jnp.tile(cos, (1, 1, 4)), jnp.concatenate([-sin, sin, -sin, sin], axis=-1)


def _rotate_half_partner(t):
    lane = lax.broadcasted_iota(jnp.int32, t.shape, 1)
    return jnp.where((lane % HEAD_DIM) < HEAD_DIM // 2,
                     pltpu.roll(t, LANES - HEAD_DIM // 2, 1), pltpu.roll(t, HEAD_DIM // 2, 1))


def _mask_bias(first):
    qi = lax.broadcasted_iota(jnp.int32, (N_BACK, 2 * N_BACK), 0)
    kj = lax.broadcasted_iota(jnp.int32, (N_BACK, 2 * N_BACK), 1)
    ok = (kj >= qi) & (kj <= qi + N_BACK)
    if first:
        ok = ok & (kj >= N_BACK)
    return jnp.where(ok, 0.0, NEG_INF).astype(F32)


def _stack_heads(t, head0):
    zero = jnp.zeros_like(t)
    return jnp.concatenate([jnp.where(head0, t, zero), jnp.where(head0, zero, t)], axis=0)


def _unstack_heads(t2, head0):
    return jnp.where(head0, t2[:N_BACK], t2[N_BACK:])


def _block_loop(nb, block):
    first, rest = _mask_bias(True), _mask_bias(False)
    first, rest = jnp.concatenate([first, first], axis=0), jnp.concatenate([rest, rest], axis=0)

    def step(n, carry):
        block(n, jnp.where(n == 0, first, rest))
        return carry

    lax.fori_loop(0, nb, step, 0, unroll=min(BLOCK_UNROLL, nb))


_NT = (((1,), (1,)), ((), ()))
_TN = (((0,), (0,)), ((), ()))


def _residue_rows(r, i, d):
    start = r + i * (N_BACK * d)
    if d == 1:
        return pl.ds(pl.multiple_of(start, N_BACK), N_BACK)
    return pl.ds(start, N_BACK, stride=d)


def _seq_rows(i):
    return pl.ds(pl.multiple_of(i * N_BACK, N_BACK), N_BACK)


def _rows_at(base, i, size=N_BACK):
    return pl.ds(pl.multiple_of(base + i * N_BACK, N_BACK), size)


def _attn_fwd(proj, cos, sin, group, Bl, S):
    d = WIN_DIL[group][1]
    L = S // d
    nb = L // N_BACK
    P = L + N_BACK
    assert WIN_DIL[group][0] // d == N_BACK and L % N_BACK == 0

    def body(q_ref, k_ref, v_ref, cos_ref, sin_ref, o_ref, lse_ref, qr, kr, vp):
        head0 = lax.broadcasted_iota(jnp.int32, (1, LANES), 1) < HEAD_DIM
        zeros = jnp.zeros((N_BACK, LANES), BF16)

        def residue(r, carry):
            qbase, kbase = r * L, r * P
            kr[_rows_at(kbase, 0), :] = zeros
            vp[_rows_at(kbase, 0), :] = zeros

            def rope(i, carry2):
                rows = _residue_rows(r, i, d)
                cs, sn = cos_ref[rows, :], sin_ref[rows, :]
                q, k = q_ref[rows, :], k_ref[rows, :]
                qr[_rows_at(qbase, i), :] = ((q * cs + _rotate_half_partner(q) * sn)
                                            * (HEAD_DIM ** -0.5)).astype(BF16)
                kr[_rows_at(kbase, i + 1), :] = (k * cs + _rotate_half_partner(k) * sn).astype(BF16)
                vp[_rows_at(kbase, i + 1), :] = v_ref[rows, :].astype(BF16)
                return carry2

            lax.fori_loop(0, nb, rope, 0)

            def block(n, bias):
                win = _rows_at(kbase, n, 2 * N_BACK)
                q2, kw, vw = _stack_heads(qr[_rows_at(qbase, n), :], head0), kr[win, :], vp[win, :]
                s = lax.dot_general(q2, kw, _NT, preferred_element_type=F32) + bias
                m = jnp.max(s, axis=1, keepdims=True)
                p = jnp.exp(s - m)
                l = jnp.sum(p, axis=1, keepdims=True)
                pv = jnp.dot(p.astype(BF16), vw, preferred_element_type=F32)
                rows = _residue_rows(r, n, d)
                o_ref[rows, :] = _unstack_heads(pv * (1.0 / l), head0)
                lse_ref[rows, :] = _unstack_heads((m + jnp.log(l)) + jnp.zeros((2 * N_BACK, LANES), F32), head0)

            _block_loop(nb, block)
            return carry

        lax.fori_loop(0, d, residue, 0)

    act = pl.BlockSpec((None, S, LANES), lambda b, hp: (b, 0, hp))
    tab = pl.BlockSpec((None, S, LANES), lambda b, hp: (b, 0, 0))
    col = lambda which: pl.BlockSpec((None, S, LANES),
                                     lambda b, hp: (b, 0, (which * N_GROUPS + group) * HEAD_PAIRS + hp))
    seq = lambda rows: pl.BlockSpec((None, None, rows, LANES), lambda b, hp: (b, hp, 0, 0))
    p3 = proj.reshape(Bl, S, PROJ_COLS)
    o, lse, qr, kr, vp = pl.pallas_call(
        body, name="attn_fwd_g%d" % group, grid=(Bl, HEAD_PAIRS),
        in_specs=[col(0), col(1), col(2), tab, tab], out_specs=[act, act, seq(S), seq(d * P), seq(d * P)],
        out_shape=[_sds((Bl, S, D_MODEL), F32)] * 2 + [_sds((Bl, HEAD_PAIRS, S, LANES), BF16)]
        + [_sds((Bl, HEAD_PAIRS, d * P, LANES), BF16)] * 2,
        compiler_params=_params(("parallel", "arbitrary"), VMEM_LIMIT_ATTN),
    )(p3, p3, p3, cos, sin)
    return o.reshape(Bl * S, D_MODEL), lse.reshape(Bl * S, D_MODEL), (qr, kr, vp)


def _attn_bwd(saved, cos, sin, do, o, lse, dproj, group, Bl, S):
    d = WIN_DIL[group][1]
    L = S // d
    nb = L // N_BACK
    P = L + N_BACK
    steps = Bl * HEAD_PAIRS

    def body(qr, kr, vp, cos_ref, sin_ref, do_ref, o_ref, lse_ref, dproj_in, dproj_ref,
             dk_acc, dv_acc, stage, sems):
        del dproj_in
        head0 = lax.broadcasted_iota(jnp.int32, (1, LANES), 1) < HEAD_DIM
        b, hp = pl.program_id(0), pl.program_id(1)
        step = b * HEAD_PAIRS + hp
        slot = step % 2
        dq_s, dk_s, dv_s = stage.at[slot, 0], stage.at[slot, 1], stage.at[slot, 2]

        def copies(which_slot):
            out = []
            for which in range(3):
                col = ((which * N_GROUPS + group) * HEAD_PAIRS + hp) * LANES
                out.append(pltpu.make_async_copy(
                    stage.at[which_slot, which], dproj_ref.at[b, :, pl.ds(pl.multiple_of(col, LANES), LANES)],
                    sems.at[which_slot, which]))
            return out

        @pl.when(step >= 2)
        def _():
            for cp in copies(slot):
                cp.wait()

        def residue(r, carry):
            qbase, kbase = r * L, r * P
            dk_acc[...] = jnp.zeros_like(dk_acc)
            dv_acc[...] = jnp.zeros_like(dv_acc)

            def block(n, bias):
                win = pl.ds(pl.multiple_of(n * N_BACK, N_BACK), 2 * N_BACK)
                kwin = _rows_at(kbase, n, 2 * N_BACK)
                rows = _residue_rows(r, n, d)
                q2, kw, vw = _stack_heads(qr[_rows_at(qbase, n), :], head0), kr[kwin, :], vp[kwin, :]
                dof = do_ref[rows, :]
                do2 = _stack_heads(dof.astype(BF16), head0)
                lse_b = lse_ref[rows, :]
                lse2 = jnp.concatenate([lse_b[:, 0:1], lse_b[:, HEAD_DIM:HEAD_DIM + 1]], axis=0)
                dsum = _stack_heads(dof * o_ref[rows, :], head0)
                delta = jnp.sum(dsum, axis=1, keepdims=True)
                s = lax.dot_general(q2, kw, _NT, preferred_element_type=F32) + bias
                p = jnp.exp(s - lse2)
                dp = lax.dot_general(do2, vw, _NT, preferred_element_type=F32)
                ds = (p * (dp - delta)).astype(BF16)
                dq = _unstack_heads(jnp.dot(ds, kw, preferred_element_type=F32), head0) * (HEAD_DIM ** -0.5)
                cs, sn = cos_ref[rows, :], sin_ref[rows, :]
                dq_s[rows, :] = dq * cs + _rotate_half_partner(dq * sn)
                dk_acc[win, :] += lax.dot_general(ds, q2, _TN, preferred_element_type=F32)
                dv_acc[win, :] += lax.dot_general(p.astype(BF16), do2, _TN, preferred_element_type=F32)

            _block_loop(nb, block)

            def finish(i, carry2):
                rows = _residue_rows(r, i, d)
                cs, sn = cos_ref[rows, :], sin_ref[rows, :]
                dk = dk_acc[_seq_rows(i + 1), :]
                dk_s[rows, :] = dk * cs + _rotate_half_partner(dk * sn)
                dv_s[rows, :] = dv_acc[_seq_rows(i + 1), :]
                return carry2

            lax.fori_loop(0, nb, finish, 0)
            return carry

        lax.fori_loop(0, d, residue, 0)
        for cp in copies(slot):
            cp.start()

        @pl.when(step == steps - 1)
        def _():
            if steps > 1:
                for cp in copies(1 - slot):
                    cp.wait()
            for cp in copies(slot):
                cp.wait()

    act = pl.BlockSpec((None, S, LANES), lambda b, hp: (b, 0, hp))
    tab = pl.BlockSpec((None, S, LANES), lambda b, hp: (b, 0, 0))
    seq = lambda rows: pl.BlockSpec((None, None, rows, LANES), lambda b, hp: (b, hp, 0, 0))
    view = lambda t: t.reshape(Bl, S, D_MODEL)
    out = pl.pallas_call(
        body, name="attn_bwd_g%d" % group, grid=(Bl, HEAD_PAIRS),
        in_specs=[seq(S), seq(d * P), seq(d * P), tab, tab, act, act, act, _ANY], out_specs=_ANY,
        out_shape=_sds((Bl, S, PROJ_COLS), F32), input_output_aliases={8: 0},
        scratch_shapes=[pltpu.VMEM((P, LANES), F32), pltpu.VMEM((P, LANES), F32),
                        pltpu.VMEM((2, 3, S, LANES), F32), pltpu.SemaphoreType.DMA((2, 3))],
        compiler_params=_params(("arbitrary", "arbitrary"), VMEM_LIMIT_ATTN),
    )(*saved, cos, sin, view(do), view(o), view(lse), dproj.reshape(Bl, S, PROJ_COLS))
    return out.reshape(Bl * S, PROJ_COLS)


CONV_TILE = 256
CONV_CHUNK = 64
SUBLANES = 8
CONV_SHIFT_ROWS = CONV_TILE + CONV_HALO - SUBLANES


def _fill_shifted(shifted, ext, cs):
    for k in range(1, SUBLANES):
        shifted[k - 1] = ext[pl.ds(k, CONV_SHIFT_ROWS), cs]


def _shifted_rows(shifted, ext, cs, off):
    k = off % SUBLANES
    if k == 0:
        return ext[pl.ds(off, CONV_CHUNK), cs]
    return shifted[k - 1, pl.ds(off - k, CONV_CHUNK), :]


def _conv_fwd(proj, z, dw, dwb, ln_g, ln_b, Bl, S):
    tr = CONV_TILE
    nj = S // tr
    hb = tr // CONV_HALO

    def body(a_ref, b_ref, ah_ref, bh_ref, z_ref, dw_ref, dwb_ref, g_ref, bb_ref, u1_ref, out_ref, ext, shifted):
        j = pl.program_id(1)
        halo = ah_ref[0].astype(F32) * _sigmoid(bh_ref[0].astype(F32))
        ext[pl.ds(0, CONV_HALO), :] = jnp.where(j > 0, halo, 0.0)
        ext[pl.ds(CONV_HALO, tr), :] = a_ref[0].astype(F32) * _sigmoid(b_ref[0].astype(F32))

        def cols(c, carry):
            cs = pl.ds(pl.multiple_of(c * LANES, LANES), LANES)
            _fill_shifted(shifted, ext, cs)
            for rc in range(tr // CONV_CHUNK):
                acc = jnp.zeros((CONV_CHUNK, LANES), F32)
                for w in range(CONV_WIDTH):
                    off = rc * CONV_CHUNK + CONV_HALO - (CONV_WIDTH - 1) + w
                    acc = acc + dw_ref[pl.ds(w, 1), cs] * _shifted_rows(shifted, ext, cs, off)
                u1_ref[0, pl.ds(rc * CONV_CHUNK, CONV_CHUNK), cs] = acc + dwb_ref[:, cs]
            return carry

        lax.fori_loop(0, D_MODEL // LANES, cols, 0)
        u1 = u1_ref[0]
        mu = jnp.mean(u1, axis=1, keepdims=True)
        xc = u1 - mu
        rstd = lax.rsqrt(jnp.mean(xc * xc, axis=1, keepdims=True) + LN_EPS)
        u2 = xc * rstd * g_ref[...] + bb_ref[...]
        zv = z_ref[0].astype(F32)
        out_ref[0] = (u2 * _sigmoid(u2) * zv * _sigmoid(zv)).astype(BF16)

    tile = lambda cb: pl.BlockSpec((1, tr, D_MODEL), lambda b, j: (b, j, cb))
    halo = lambda cb: pl.BlockSpec((1, CONV_HALO, D_MODEL), lambda b, j: (b, jnp.maximum(j * hb - 1, 0), cb))
    par = lambda r: pl.BlockSpec((r, D_MODEL), lambda b, j: (0, 0))
    p3 = proj.reshape(Bl, S, 2 * D_MODEL)
    u1, out = pl.pallas_call(
        body, name="conv_fwd", grid=(Bl, nj),
        in_specs=[tile(0), tile(1), halo(0), halo(1), tile(0), par(32), par(1), par(1), par(1)],
        out_specs=[tile(0), tile(0)],
        out_shape=[_sds((Bl, S, D_MODEL), F32), _sds((Bl, S, D_MODEL), BF16)],
        scratch_shapes=[pltpu.VMEM((tr + CONV_HALO, D_MODEL), F32),
                        pltpu.VMEM((SUBLANES - 1, CONV_SHIFT_ROWS, LANES), F32)],
        compiler_params=_params(("parallel", "arbitrary")),
    )(p3, p3, p3, p3, z.reshape(Bl, S, D_MODEL), dw, dwb, ln_g, ln_b)
    return u1.reshape(Bl * S, D_MODEL), out.reshape(Bl * S, D_MODEL)


def _conv_norm_bwd(da2, z, u1, ln_g, ln_b):
    T = da2.shape[0]

    def body(da_ref, z_ref, u_ref, g_ref, b_ref, du_ref, dz_ref, dg_ref, db_ref):
        @pl.when(pl.program_id(0) == 0)
        def _():
            dg_ref[...] = jnp.zeros_like(dg_ref)
            db_ref[...] = jnp.zeros_like(db_ref)

        u1 = u_ref[...]
        mu = jnp.mean(u1, axis=1, keepdims=True)
        xc = u1 - mu
        rstd = lax.rsqrt(jnp.mean(xc * xc, axis=1, keepdims=True) + LN_EPS)
        nrm = xc * rstd
        u2 = nrm * g_ref[...] + b_ref[...]
        s2 = _sigmoid(u2)
        zv = z_ref[...].astype(F32)
        sz = _sigmoid(zv)
        dv = da_ref[...]
        dz_ref[...] = (dv * u2 * s2 * sz * (1.0 + zv * (1.0 - sz))).astype(BF16)
        du2 = dv * zv * sz * s2 * (1.0 + u2 * (1.0 - s2))
        dg_ref[...] += jnp.sum(du2 * nrm, axis=0, keepdims=True)
        db_ref[...] += jnp.sum(du2, axis=0, keepdims=True)
        dn = du2 * g_ref[...]
        du_ref[...] = rstd * (dn - jnp.mean(dn, axis=1, keepdims=True)
                              - nrm * jnp.mean(dn * nrm, axis=1, keepdims=True))

    return _row_call(body, "conv_norm_bwd", T,
                     [_rows(), _rows(), _rows(), _full((1, D_MODEL)), _full((1, D_MODEL))],
                     [_rows(), _rows(), _full((1, D_MODEL)), _full((1, D_MODEL))],
                     [_sds((T, D_MODEL), F32), _sds((T, D_MODEL), BF16), _sds((1, D_MODEL), F32),
                      _sds((1, D_MODEL), F32)], (da2, z, u1, ln_g, ln_b))


def _conv_bwd(proj, du1, dw, Bl, S):
    tr = CONV_TILE
    nj = S // tr
    hb = tr // CONV_HALO

    def body(a_ref, b_ref, ah_ref, bh_ref, du_ref, duh_ref, dw_ref, dab_ref, ddw_ref, ddb_ref, uext, dext, du0,
             ushift, dshift, ddw8):
        first = (pl.program_id(0) == 0) & (pl.program_id(1) == 0)
        last = (pl.program_id(0) == Bl - 1) & (pl.program_id(1) == nj - 1)
        j = pl.program_id(1)

        @pl.when(first)
        def _():
            ddw8[...] = jnp.zeros_like(ddw8)
            ddb_ref[...] = jnp.zeros_like(ddb_ref)

        halo = ah_ref[0].astype(F32) * _sigmoid(bh_ref[0].astype(F32))
        uext[pl.ds(0, CONV_HALO), :] = jnp.where(j > 0, halo, 0.0)
        av = a_ref[0].astype(F32)
        sb = _sigmoid(b_ref[0].astype(F32))
        uext[pl.ds(CONV_HALO, tr), :] = av * sb
        dext[pl.ds(0, tr), :] = du_ref[0]
        dext[pl.ds(tr, CONV_HALO), :] = jnp.where(j < nj - 1, duh_ref[0], 0.0)
        ddb_ref[...] += jnp.sum(du_ref[0], axis=0, keepdims=True)

        def cols(c, carry):
            cs = pl.ds(pl.multiple_of(c * LANES, LANES), LANES)
            _fill_shifted(dshift, dext, cs)
            _fill_shifted(ushift, uext, cs)
            for rc in range(tr // CONV_CHUNK):
                base = rc * CONV_CHUNK
                acc = jnp.zeros((CONV_CHUNK, LANES), F32)
                for w in range(CONV_WIDTH):
                    acc = acc + dw_ref[pl.ds(w, 1), cs] * _shifted_rows(dshift, dext, cs, base + CONV_WIDTH - 1 - w)
                du0[pl.ds(base, CONV_CHUNK), cs] = acc
            for w in range(CONV_WIDTH):
                part = jnp.zeros((SUBLANES, LANES), F32)
                for rc in range(tr // CONV_CHUNK):
                    base = rc * CONV_CHUNK
                    prod = dext[pl.ds(base, CONV_CHUNK), cs] * _shifted_rows(
                        ushift, uext, cs, base + CONV_HALO - (CONV_WIDTH - 1) + w)
                    for i in range(CONV_CHUNK // SUBLANES):
                        part = part + prod[i * SUBLANES:(i + 1) * SUBLANES]
                ddw8[pl.ds(w * SUBLANES, SUBLANES), cs] += part
            return carry

        lax.fori_loop(0, D_MODEL // LANES, cols, 0)
        g = du0[...]
        dab_ref[0, :, 0:D_MODEL] = (g * sb).astype(BF16)
        dab_ref[0, :, D_MODEL:2 * D_MODEL] = (g * av * sb * (1.0 - sb)).astype(BF16)

        @pl.when(last)
        def _():
            for w in range(CONV_WIDTH + 1):
                ddw_ref[pl.ds(w, 1), :] = jnp.sum(ddw8[pl.ds(w * SUBLANES, SUBLANES), :], axis=0, keepdims=True)

    tile = lambda cb: pl.BlockSpec((1, tr, D_MODEL), lambda b, j: (b, j, cb))
    halo = lambda cb: pl.BlockSpec((1, CONV_HALO, D_MODEL), lambda b, j: (b, jnp.maximum(j * hb - 1, 0), cb))
    nxt = pl.BlockSpec((1, CONV_HALO, D_MODEL), lambda b, j: (b, jnp.minimum((j + 1) * hb, S // CONV_HALO - 1), 0))
    par = lambda r: pl.BlockSpec((r, D_MODEL), lambda b, j: (0, 0))
    p3 = proj.reshape(Bl, S, 2 * D_MODEL)
    d3 = du1.reshape(Bl, S, D_MODEL)
    dab, ddw, ddb = pl.pallas_call(
        body, name="conv_bwd", grid=(Bl, nj),
        in_specs=[tile(0), tile(1), halo(0), halo(1), tile(0), nxt, par(32)],
        out_specs=[pl.BlockSpec((1, tr, 2 * D_MODEL), lambda b, j: (b, j, 0)), par(32), par(1)],
        out_shape=[_sds((Bl, S, 2 * D_MODEL), BF16), _sds((32, D_MODEL), F32), _sds((1, D_MODEL), F32)],
        scratch_shapes=[pltpu.VMEM((tr + CONV_HALO, D_MODEL), F32), pltpu.VMEM((tr + CONV_HALO, D_MODEL), F32),
                        pltpu.VMEM((tr, D_MODEL), F32),
                        pltpu.VMEM((SUBLANES - 1, CONV_SHIFT_ROWS, LANES), F32),
                        pltpu.VMEM((SUBLANES - 1, CONV_SHIFT_ROWS, LANES), F32),
                        pltpu.VMEM(((CONV_WIDTH + 1) * SUBLANES, D_MODEL), F32)],
        compiler_params=_params(("arbitrary", "arbitrary")),
    )(p3, p3, p3, p3, d3, d3, dw)
    return dab.reshape(Bl * S, 2 * D_MODEL), ddw, ddb


_LAYOUT = (
    ("pre_norm_g", (4, 1024), None), ("post_norm_g", (4, 1024), None),
    ("attn_w_in", (2, 1024, 2560), 2), ("attn_w_out", (2, 256, 1024), 1),
    ("conv_w_in", (2, 1024, 768), 2), ("conv_dw_w", (2, 31, 256), 2),
    ("conv_dw_b", (2, 256), 1), ("conv_ln_g", (2, 256), 1), ("conv_ln_b", (2, 256), 1),
    ("conv_w_out", (2, 256, 1024), 1), ("ple_w_proj", (4, 256, 256), 2), ("ple_w_gate", (4, 256, 1024), 1),
)
_MATMUL_WEIGHTS = ("attn_w_in", "attn_w_out", "conv_w_in", "conv_w_out", "ple_w_proj", "ple_w_gate")
_SMALL_WEIGHTS = ("conv_dw_w", "conv_dw_b", "conv_ln_g", "conv_ln_b")
_SHAPE = {n: s for n, s, _ in _LAYOUT}
_AXIS = {n: a for n, _, a in _LAYOUT}


def _size(shape):
    n = 1
    for s in shape:
        n *= s
    return n


def _padded_rows(shape):
    rows = _size(shape) // shape[-1]
    return rows + (-rows) % FLAT_ROW_ALIGN


def _rows2d(a):
    a2 = a.reshape(-1, a.shape[-1])
    pad = _padded_rows(a.shape) - a2.shape[0]
    return jnp.pad(a2, ((0, pad), (0, 0))) if pad else a2


def _col_blocks(a):
    a2 = _rows2d(a)
    return jnp.concatenate([a2[:, c:c + FLAT_COLS] for c in range(0, a2.shape[1], FLAT_COLS)], axis=0)


def _from_col_blocks(flat, off, shape):
    rows, nblk = _padded_rows(shape), shape[-1] // FLAT_COLS
    a2 = jnp.concatenate([flat[off + b * rows:off + (b + 1) * rows] for b in range(nblk)], axis=1)
    return a2[:_size(shape) // shape[-1]].reshape(shape), off + nblk * rows


def _shard_col_blocks(full, shape, axis):
    if axis is None:
        blocks = _col_blocks(full)
        return jnp.broadcast_to(blocks[None], (N_CHIPS,) + blocks.shape)
    m = shape[-1]
    if axis == len(shape) - 1:
        a2 = _rows2d(full)
        pieces = [a2[:, c:c + FLAT_COLS] for c in range(0, N_CHIPS * m, FLAT_COLS)]
    else:
        layers, r, _ = shape
        assert axis == 1 and (layers * r) % FLAT_ROW_ALIGN == 0
        pieces = [full[:, s * r:(s + 1) * r, c:c + FLAT_COLS].reshape(layers * r, FLAT_COLS)
                  for s in range(N_CHIPS) for c in range(0, m, FLAT_COLS)]
    return jnp.concatenate(pieces, axis=0).reshape(N_CHIPS, -1, FLAT_COLS)


def _unshard_col_blocks(g, off, shape, axis):
    rows, nblk = _padded_rows(shape), shape[-1] // FLAT_COLS
    block = lambda s, b: g[s, off + b * rows:off + (b + 1) * rows]
    if axis == len(shape) - 1:
        a2 = jnp.concatenate([block(s, b) for s in range(N_CHIPS) for b in range(nblk)], axis=1)
        full = a2[:_size(shape) // shape[-1]].reshape(shape[:-1] + (N_CHIPS * shape[-1],))
    else:
        layers, r, m = shape
        shards = [jnp.concatenate([block(s, b) for b in range(nblk)], axis=1).reshape(layers, r, m)
                  for s in range(N_CHIPS)]
        full = jnp.concatenate(shards, axis=1)
    return full, off + nblk * rows


def _pack_rows(pieces):
    flat = jnp.concatenate(pieces, axis=0)
    assert flat.shape[0] <= FLAT_ROWS
    return jnp.pad(flat, ((0, FLAT_ROWS - flat.shape[0]), (0, 0)))


def _unpack_f32(flat):
    out, off = {}, 0
    for n, shape, _ in _LAYOUT:
        out[n], off = _from_col_blocks(flat, off, shape)
    return out


def _bytes_shape(shape):
    return shape[:-1] + (4 * shape[-1],)


def _f32_to_bytes(a):
    u = lax.bitcast_convert_type(a, jnp.uint32)
    parts = jnp.stack([(u >> s) & 0xFF for s in (0, 8, 16, 24)], axis=-1)
    return parts.astype(F32).astype(BF16).reshape(_bytes_shape(a.shape))


def _bytes_to_f32(b, shape):
    u = b.reshape(shape + (4,)).astype(F32).astype(jnp.uint32)
    return lax.bitcast_convert_type(u[..., 0] | (u[..., 1] << 8) | (u[..., 2] << 16) | (u[..., 3] << 24), F32)


def _pack_gather_payload(w):
    pieces = [_col_blocks(w[n].astype(BF16)) for n in _MATMUL_WEIGHTS]
    pieces += [_col_blocks(_f32_to_bytes(w[n])) for n in _SMALL_WEIGHTS]
    return _pack_rows(pieces)


def _unpack_gathered(g):
    out, off = {}, 0
    for n in _MATMUL_WEIGHTS:
        out[n], off = _unshard_col_blocks(g, off, _SHAPE[n], _AXIS[n])
    for n in _SMALL_WEIGHTS:
        parts = [_from_col_blocks(g[s], off, _bytes_shape(_SHAPE[n])) for s in range(N_CHIPS)]
        out[n] = jnp.concatenate([_bytes_to_f32(part, _SHAPE[n]) for part, _ in parts], axis=_AXIS[n])
        off = parts[0][1]
    return out


def _pack_full_grads(grads):
    flat = jnp.concatenate([_shard_col_blocks(grads[n], shape, axis) for n, shape, axis in _LAYOUT], axis=1)
    assert flat.shape[1] <= FLAT_ROWS
    return jnp.pad(flat, ((0, 0), (0, FLAT_ROWS - flat.shape[1]), (0, 0)))


_ANY = pl.BlockSpec(memory_space=pl.ANY)


def _mesh_pos():
    return lax.axis_index("x"), lax.axis_index("y"), lax.axis_index("c")


def _other_chips(x, y):
    return [(1 - x, y), (x, 1 - y), (1 - x, 1 - y)]


def _allgather_weights(wl):
    R, C = wl.shape
    H = R // 2

    def body(w_ref, out_ref, send_sems, recv_sems):
        x, y, c = _mesh_pos()
        me, sibling = (x, y, c), (x, y, 1 - c)
        chips = _other_chips(x, y)

        def half(px, py, pc):
            return out_ref.at[2 * px + py, pl.ds(pc * H, H), :]

        def copy(k, block, to, src=None):
            return pltpu.make_async_remote_copy(
                src_ref=half(*block) if src is None else src, dst_ref=half(*block),
                send_sem=send_sems.at[k], recv_sem=recv_sems.at[k], device_id=to, device_id_type=MESH)

        own = pltpu.make_async_remote_copy(
            src_ref=w_ref, dst_ref=out_ref.at[2 * x + y], send_sem=send_sems.at[6], recv_sem=recv_sems.at[6],
            device_id=sibling, device_id_type=MESH)
        own.start()
        first = [copy(j, me, (*chip, c), src=w_ref.at[pl.ds(c * H, H), :]) for j, chip in enumerate(chips)]
        for cp in first:
            cp.start()
        passed = [copy(3 + j, (*chip, c), sibling) for j, chip in enumerate(chips)]
        for j, chip in enumerate(chips):
            copy(j, (*chip, c), me).wait_recv()
            passed[j].start()
        for j, chip in enumerate(chips):
            copy(3 + j, (*chip, 1 - c), me).wait_recv()
        own.wait()
        for cp in first + passed:
            cp.wait_send()

    return pl.pallas_call(
        body, name="allgather_weights", in_specs=[_ANY], out_specs=_ANY,
        out_shape=_sds((N_CHIPS, R, C), wl.dtype),
        scratch_shapes=[pltpu.SemaphoreType.DMA((7,)), pltpu.SemaphoreType.DMA((7,))],
    )(wl)


def _exchange_core_halves(g):
    n, _, H, C = g.shape

    def body(g_ref, got_ref, send_sem, recv_sem):
        x, y, c = _mesh_pos()
        swap = pltpu.make_async_remote_copy(
            src_ref=g_ref.at[pl.ds(0, n), 1 - c], dst_ref=got_ref, send_sem=send_sem, recv_sem=recv_sem,
            device_id=(x, y, 1 - c), device_id_type=MESH)
        swap.start()
        swap.wait()

    return pl.pallas_call(
        body, name="exchange_core_halves", in_specs=[_ANY], out_specs=_ANY,
        out_shape=_sds((n, H, C), g.dtype),
        scratch_shapes=[pltpu.SemaphoreType.DMA, pltpu.SemaphoreType.DMA],
    )(g)


def _scatter_to_chips(p):
    n, H, C = p.shape

    def body(p_ref, q_ref, send_sems, recv_sems):
        x, y, c = _mesh_pos()
        chips = _other_chips(x, y)
        sends = [pltpu.make_async_remote_copy(
            src_ref=p_ref.at[2 * cx + cy], dst_ref=q_ref.at[j], send_sem=send_sems.at[j],
            recv_sem=recv_sems.at[j], device_id=(cx, cy, c), device_id_type=MESH)
            for j, (cx, cy) in enumerate(chips)]
        for cp in sends:
            cp.start()
        for cp in sends:
            cp.wait_recv()
        for cp in sends:
            cp.wait_send()

    return pl.pallas_call(
        body, name="scatter_to_chips", in_specs=[_ANY], out_specs=_ANY,
        out_shape=_sds((n - 1, H, C), p.dtype),
        scratch_shapes=[pltpu.SemaphoreType.DMA((3,)), pltpu.SemaphoreType.DMA((3,))],
    )(p)


def _share_core_halves(r2):
    _, H, C = r2.shape

    def body(r_ref, out_ref, send_sem, recv_sem):
        x, y, c = _mesh_pos()
        send = pltpu.make_async_remote_copy(
            src_ref=r_ref.at[c], dst_ref=out_ref.at[c], send_sem=send_sem, recv_sem=recv_sem,
            device_id=(x, y, 1 - c), device_id_type=MESH)
        send.start()
        send.wait_send()
        pltpu.make_async_remote_copy(
            src_ref=r_ref.at[c], dst_ref=out_ref.at[1 - c], send_sem=send_sem, recv_sem=recv_sem,
            device_id=(x, y, 1 - c), device_id_type=MESH).wait_recv()

    return pl.pallas_call(
        body, name="share_core_halves", in_specs=[_ANY], out_specs=_ANY,
        out_shape=_sds(r2.shape, r2.dtype), input_output_aliases={0: 0},
        scratch_shapes=[pltpu.SemaphoreType.DMA, pltpu.SemaphoreType.DMA],
    )(r2)


def _place():
    x, y, c = _mesh_pos()
    return jnp.stack([c, 2 * x + y]).astype(jnp.int32)


def _sum_pair(g, got, place):
    n, _, H, C = g.shape

    def body(place_ref, a_ref, b_ref, o_ref):
        o_ref[...] = (a_ref[...] + b_ref[...]).astype(BF16)

    spec = pl.BlockSpec((1, FLAT_TILE, C), lambda s, i, pr: (s, i, 0))
    return pl.pallas_call(
        body, name="sum_core_pair",
        grid_spec=pltpu.PrefetchScalarGridSpec(
            num_scalar_prefetch=1, grid=(n, H // FLAT_TILE),
            in_specs=[pl.BlockSpec((1, None, FLAT_TILE, C), lambda s, i, pr: (s, pr[0], i, 0)), spec],
            out_specs=spec),
        out_shape=_sds((n, H, C), BF16),
        compiler_params=_params(("parallel", "parallel")))(place, g, got)


def _sum_chips(p, q, place):
    n, H, C = p.shape

    def body(place_ref, own_ref, qx_ref, qy_ref, qxy_ref, o_ref):
        mine = place_ref[1]
        own, qx, qy, qxy = (t[0].astype(F32) for t in (own_ref, qx_ref, qy_ref, qxy_ref))

        def term(s):
            rel = jnp.full(own.shape, mine ^ s, jnp.int32)
            return jnp.where(rel == 0, own, jnp.where(rel == 2, qx, jnp.where(rel == 1, qy, qxy)))

        o_ref[0] = ((term(0) + term(1)) + term(2)) + term(3)

    qspec = lambda j: pl.BlockSpec((1, FLAT_TILE, C), lambda i, pr: (j, i, 0))
    return pl.pallas_call(
        body, name="sum_chips",
        grid_spec=pltpu.PrefetchScalarGridSpec(
            num_scalar_prefetch=1, grid=(H // FLAT_TILE,),
            in_specs=[pl.BlockSpec((1, FLAT_TILE, C), lambda i, pr: (pr[1], i, 0)), qspec(0), qspec(1), qspec(2)],
            out_specs=pl.BlockSpec((1, FLAT_TILE, C), lambda i, pr: (pr[0], i, 0))),
        out_shape=_sds((2, H, C), F32),
        compiler_params=_params(("parallel",)))(place, p, q, q, q)


ADAMW_BLOCK = 1 << 18


def _adamw(w, g, m, v):
    shape = w.shape
    C = shape[-1]
    R = _size(shape) // C
    tr = R
    while tr * C > ADAMW_BLOCK and tr % 16 == 0:
        tr //= 2
    w, g, m, v = (t.reshape(R, C) for t in (w, g, m, v))

    def body(w_ref, g_ref, m_ref, v_ref, d_ref, nm_ref, nv_ref):
        gv = g_ref[...]
        nm = ADAM_B1 * m_ref[...] + (1.0 - ADAM_B1) * gv
        nv = ADAM_B2 * v_ref[...] + (1.0 - ADAM_B2) * (gv * gv)
        m_hat = nm / (1.0 - ADAM_B1 ** ADAM_STEP)
        v_hat = nv / (1.0 - ADAM_B2 ** ADAM_STEP)
        d_ref[...] = -ADAM_LR * (m_hat / (jnp.sqrt(v_hat) + ADAM_EPS) + ADAM_WD * w_ref[...])
        nm_ref[...] = nm
        nv_ref[...] = nv

    spec = pl.BlockSpec((tr, C), lambda i: (i, 0))
    outs = pl.pallas_call(body, name="adamw", grid=(R // tr,), in_specs=[spec] * 4, out_specs=[spec] * 3,
                          out_shape=[_sds((R, C), F32)] * 3, compiler_params=_params(("parallel",)))(w, g, m, v)
    return tuple(t.reshape(shape) for t in outs)


def _reduce_scatter_grads(gfull):
    n, R, C = gfull.shape
    place = _place()
    g4 = gfull.reshape(n, 2, R // 2, C)
    p = _sum_pair(g4, _exchange_core_halves(g4), place)
    return _share_core_halves(_sum_chips(p, _scatter_to_chips(p), place)).reshape(R, C)


def _local_step(x, p, positions, loss_target, pre_g, post_g, w):
    Bl, S, _ = x.shape
    T = Bl * S
    cos, sin = _rope_tables(positions)
    xs = x.reshape(T, D_MODEL)
    saved = []
    for i in range(DEPTH):
        j = i // 2
        g_pre, g_post = pre_g[i:i + 1], post_g[i:i + 1]
        h = _rmsnorm_fwd(xs, g_pre)
        st = {"x": xs, "h": h}
        if i % 2 == 0:
            proj = _mm(h, w["attn_w_in"][j], name="attn_in")
            res = [_attn_fwd(proj, cos, sin, g, Bl, S) for g in range(N_GROUPS)]
            a, o, lse = _attn_combine([r[0] for r in res], [r[1] for r in res], proj)
            y = _mm(a, w["attn_w_out"][j], name="attn_out")
            st.update(proj=proj, a=a, o=o, lse=lse, qkv=[r[2] for r in res])
        else:
            w_ab, w_z = w["conv_w_in"][j][:, :2 * D_MODEL], w["conv_w_in"][j][:, 2 * D_MODEL:]
            ab = _mm(h, w_ab, out_dtype=BF16, name="conv_in_ab")
            z = _mm(h, w_z, out_dtype=BF16, name="conv_in_z")
            dw = jnp.pad(w["conv_dw_w"][j], ((0, 1), (0, 0)))
            u1, a = _conv_fwd(ab, z, dw, w["conv_dw_b"][j:j + 1], w["conv_ln_g"][j:j + 1],
                              w["conv_ln_b"][j:j + 1], Bl, S)
            y = _mm(a, w["conv_w_out"][j], name="conv_out")
            st.update(w_ab=w_ab, w_z=w_z, ab=ab, z=z, dw=dw, u1=u1, a=a)
        x1 = _post_fwd(xs, y, g_post)
        pi = p[i].reshape(T, PLE_DIM)
        pe = _mm(pi, w["ple_w_proj"][i], name="ple_proj")
        gl = _mm(x1, w["ple_w_gate"][i], name="ple_gate")
        xs = _ple_fwd(x1, pe, gl)
        st.update(y=y, x1=x1, pi=pi, pe=pe, gl=gl)
        saved.append(st)

    sq, dx = _loss_fwd_bwd(xs, loss_target.reshape(T, D_MODEL))

    grads = {n: [None] * shape[0] for n, shape, _ in _LAYOUT}
    for i in reversed(range(DEPTH)):
        j = i // 2
        st = saved[i]
        g_pre, g_post = pre_g[i:i + 1], post_g[i:i + 1]
        dpe, dgl = _ple_bwd(dx, st["pe"], st["gl"])
        grads["ple_w_proj"][i] = _mm(st["pi"], dpe, ta=True, name="ple_proj_wgrad")
        grads["ple_w_gate"][i] = _mm(st["x1"], dgl, ta=True, name="ple_gate_wgrad")
        dx1 = _mm(dgl, w["ple_w_gate"][i], tb=True, add=dx, name="ple_gate_dgrad")
        dy, dg_post = _rmsnorm_bwd(dx1, st["y"], g_post, None, BF16, "post_bwd")
        grads["post_norm_g"][i] = dg_post[0]
        if i % 2 == 0:
            grads["attn_w_out"][j] = _mm(st["a"], dy, ta=True, name="attn_out_wgrad")
            da = _mm(dy, w["attn_w_out"][j], tb=True, name="attn_out_dgrad")
            do, dproj = _gate_bwd(da, st["o"], st["proj"])
            for g in range(N_GROUPS):
                dproj = _attn_bwd(st["qkv"][g], cos, sin, do, st["o"], st["lse"], dproj, g, Bl, S)
            dh = _mm(dproj, w["attn_w_in"][j], tb=True, tk=2048, name="attn_in_dgrad")
            grads["attn_w_in"][j] = _mm(st["h"], dproj, ta=True, name="attn_in_wgrad")
        else:
            grads["conv_w_out"][j] = _mm(st["a"], dy, ta=True, name="conv_out_wgrad")
            da2 = _mm(dy, w["conv_w_out"][j], tb=True, name="conv_out_dgrad")
            du1, dz, dln_g, dln_b = _conv_norm_bwd(da2, st["z"], st["u1"], w["conv_ln_g"][j:j + 1],
                                                   w["conv_ln_b"][j:j + 1])
            dab, ddw, ddb = _conv_bwd(st["ab"], du1, st["dw"], Bl, S)
            dh = _mm(dz, st["w_z"], tb=True, name="conv_in_z_dgrad")
            dh = _mm(dab, st["w_ab"], tb=True, add=dh, name="conv_in_ab_dgrad")
            dw_ab = _mm(st["h"], dab, ta=True, name="conv_in_ab_wgrad")
            dw_z = _mm(st["h"], dz, ta=True, name="conv_in_z_wgrad")
            grads["conv_w_in"][j] = jnp.concatenate([dw_ab, dw_z], axis=1)
            grads["conv_dw_w"][j] = ddw[:CONV_WIDTH]
            grads["conv_dw_b"][j] = ddb[0]
            grads["conv_ln_g"][j] = dln_g[0]
            grads["conv_ln_b"][j] = dln_b[0]
        dx, dg_pre = _rmsnorm_bwd(dh, st["x"], g_pre, dx1, F32, "pre_bwd")
        grads["pre_norm_g"][i] = dg_pre[0]
    grads = {n: jnp.stack(v) for n, v in grads.items()}
    return sq, dx.reshape(Bl, S, D_MODEL), grads


_NAMES = tuple(n for n, _, _ in _LAYOUT)


def kernel(x, p, positions, pre_norm_g, post_norm_g, attn_w_in, attn_w_out, conv_w_in, conv_dw_w, conv_dw_b, conv_ln_g, conv_ln_b, conv_w_out, ple_w_proj, ple_w_gate, loss_target, m_pre_norm_g, m_post_norm_g, m_attn_w_in, m_attn_w_out, m_conv_w_in, m_conv_dw_w, m_conv_dw_b, m_conv_ln_g, m_conv_ln_b, m_conv_w_out, m_ple_w_proj, m_ple_w_gate, v_pre_norm_g, v_post_norm_g, v_attn_w_in, v_attn_w_out, v_conv_w_in, v_conv_dw_w, v_conv_dw_b, v_conv_ln_g, v_conv_ln_b, v_conv_w_out, v_ple_w_proj, v_ple_w_gate):
    w_loc = dict(zip(_NAMES, (pre_norm_g, post_norm_g, attn_w_in, attn_w_out, conv_w_in, conv_dw_w, conv_dw_b,
                              conv_ln_g, conv_ln_b, conv_w_out, ple_w_proj, ple_w_gate)))
    m_loc = dict(zip(_NAMES, (m_pre_norm_g, m_post_norm_g, m_attn_w_in, m_attn_w_out, m_conv_w_in, m_conv_dw_w,
                              m_conv_dw_b, m_conv_ln_g, m_conv_ln_b, m_conv_w_out, m_ple_w_proj, m_ple_w_gate)))
    v_loc = dict(zip(_NAMES, (v_pre_norm_g, v_post_norm_g, v_attn_w_in, v_attn_w_out, v_conv_w_in, v_conv_dw_w,
                              v_conv_dw_b, v_conv_ln_g, v_conv_ln_b, v_conv_w_out, v_ple_w_proj, v_ple_w_gate)))

    w_full = _unpack_gathered(_allgather_weights(_pack_gather_payload(w_loc)))
    sq, grad_x, grads = _local_step(x, p, positions, loss_target, pre_norm_g, post_norm_g, w_full)
    loss = lax.psum(sq[0, 0] * (0.5 / D_MODEL), ("x", "y", "c"))

    g_flat = _reduce_scatter_grads(_pack_full_grads(grads))
    g_out = _unpack_f32(g_flat)
    updates = {n: _adamw(w_loc[n], g_out[n], m_loc[n], v_loc[n]) for n in _NAMES}
    d_out, m_out, v_out = ({n: updates[n][k] for n in _NAMES} for k in range(3))
    return (loss, grad_x, *[g_out[n] for n in _NAMES], *[d_out[n] for n in _NAMES],
            *[m_out[n] for n in _NAMES], *[v_out[n] for n in _NAMES])
```

```python
import jax
import jax.numpy as jnp
from jax import lax
from jax.experimental import pallas as pl
from jax.experimental.pallas import tpu as pltpu

F32 = jnp.float32
BF16 = jnp.bfloat16

D_MODEL = 1024
DEPTH = 4
PLE_DIM = 256
HEAD_DIM = 64
WIN_DIL = ((128, 1), (512, 4), (2048, 16))
N_GROUPS = 3
N_BACK = 128
BLOCK_UNROLL = 4
ROPE_THETA = 10000.0
CONV_WIDTH = 31
CONV_HALO = 32
RMS_EPS = 1e-6
LN_EPS = 1e-5
NEG_INF = -1e30
ADAM_LR, ADAM_B1, ADAM_B2, ADAM_EPS, ADAM_WD, ADAM_STEP = 0.001, 0.9, 0.999, 1e-08, 0.01, 10

LANES = 128
N_CHIPS = 4
VMEM_LIMIT = 48 * 1024 * 1024
VMEM_LIMIT_ATTN = 56 * 1024 * 1024
FLAT_COLS = 256
FLAT_ROWS = 36864
FLAT_TILE = 2048
FLAT_ROW_ALIGN = 16
PROJ_COLS = (3 * N_GROUPS + 1) * D_MODEL
HEAD_PAIRS = D_MODEL // LANES

MESH = pl.DeviceIdType.MESH


def _params(sem=None, vmem=VMEM_LIMIT):
    return pltpu.CompilerParams(dimension_semantics=sem, vmem_limit_bytes=vmem)


def _sigmoid(v):
    return 1.0 / (1.0 + jnp.exp(-v))


def _mm(a, b, *, ta=False, tb=False, add=None, out_dtype=F32, tm=1024, tn=1024, tk=1024, name="mm"):
    if ta:
        K, M = a.shape
    else:
        M, K = a.shape
    if tb:
        N, K2 = b.shape
    else:
        K2, N = b.shape
    assert K == K2, (a.shape, b.shape)
    tm, tn, tk = min(tm, M), min(tn, N), min(tk, K)
    assert M % tm == 0 and N % tn == 0 and K % tk == 0
    nk = K // tk
    dims = (((0 if ta else 1,), (1 if tb else 0,)), ((), ()))

    def body(*refs):
        if add is None:
            a_ref, b_ref, o_ref = refs[:3]
        else:
            a_ref, b_ref, add_ref, o_ref = refs[:4]
        k = pl.program_id(2)
        part = lax.dot_general(a_ref[...].astype(BF16), b_ref[...].astype(BF16), dims, preferred_element_type=F32)

        def finish(r):
            if add is not None:
                r = r + add_ref[...].astype(F32)
            o_ref[...] = r.astype(out_dtype)

        if nk == 1:
            finish(part)
        else:
            acc_ref = refs[-1]

            @pl.when(k == 0)
            def _():
                acc_ref[...] = part

            @pl.when((k > 0) & (k < nk - 1))
            def _():
                acc_ref[...] += part

            @pl.when(k == nk - 1)
            def _():
                finish(acc_ref[...] + part)

    a_spec = pl.BlockSpec((tk, tm), lambda i, j, k: (k, i)) if ta else pl.BlockSpec((tm, tk), lambda i, j, k: (i, k))
    b_spec = pl.BlockSpec((tn, tk), lambda i, j, k: (j, k)) if tb else pl.BlockSpec((tk, tn), lambda i, j, k: (k, j))
    o_spec = pl.BlockSpec((tm, tn), lambda i, j, k: (i, j))
    in_specs, args = [a_spec, b_spec], [a, b]
    if add is not None:
        in_specs.append(o_spec)
        args.append(add)
    return pl.pallas_call(
        body, name=name, grid=(M // tm, N // tn, nk),
        in_specs=in_specs, out_specs=o_spec,
        out_shape=jax.ShapeDtypeStruct((M, N), out_dtype),
        scratch_shapes=[pltpu.VMEM((tm, tn), F32)] if nk > 1 else [],
        compiler_params=_params(("parallel", "parallel", "arbitrary")),
    )(*args)


ROW_TILE = 512


def _rows(w=D_MODEL, cb=0, tr=ROW_TILE):
    return pl.BlockSpec((tr, w), lambda i: (i, cb))


def _full(shape):
    return pl.BlockSpec(shape, lambda i: (0,) * len(shape))


def _row_call(body, name, T, in_specs, out_specs, out_shape, args, tr=ROW_TILE):
    return pl.pallas_call(body, name=name, grid=(T // tr,), in_specs=in_specs, out_specs=out_specs,
                          out_shape=out_shape, compiler_params=_params(("arbitrary",)))(*args)


def _sds(shape, dtype):
    return jax.ShapeDtypeStruct(shape, dtype)


def _rmsnorm_fwd(x, g):
    T = x.shape[0]

    def body(x_ref, g_ref, h_ref):
        xv = x_ref[...]
        r = lax.rsqrt(jnp.mean(xv * xv, axis=1, keepdims=True) + RMS_EPS)
        h_ref[...] = (xv * r * g_ref[...]).astype(BF16)

    return _row_call(body, "rmsnorm_fwd", T, [_rows(), _full((1, D_MODEL))], _rows(),
                     _sds((T, D_MODEL), BF16), (x, g))


def _post_fwd(x, y, g):
    T = x.shape[0]

    def body(x_ref, y_ref, g_ref, o_ref):
        yv = y_ref[...]
        r = lax.rsqrt(jnp.mean(yv * yv, axis=1, keepdims=True) + RMS_EPS)
        o_ref[...] = x_ref[...] + yv * r * g_ref[...]

    return _row_call(body, "post_fwd", T, [_rows(), _rows(), _full((1, D_MODEL))], _rows(),
                     _sds((T, D_MODEL), F32), (x, y, g))


def _rmsnorm_bwd(dout, xin, g, add, out_dtype, name):
    T = xin.shape[0]

    def body(*refs):
        if add is None:
            d_ref, x_ref, g_ref, dx_ref, dg_ref = refs
        else:
            d_ref, x_ref, g_ref, add_ref, dx_ref, dg_ref = refs

        @pl.when(pl.program_id(0) == 0)
        def _():
            dg_ref[...] = jnp.zeros_like(dg_ref)

        xv = x_ref[...]
        dv = d_ref[...].astype(F32)
        r = lax.rsqrt(jnp.mean(xv * xv, axis=1, keepdims=True) + RMS_EPS)
        xh = xv * r
        dg_ref[...] += jnp.sum(dv * xh, axis=0, keepdims=True)
        dn = dv * g_ref[...]
        dx = r * (dn - xh * jnp.mean(dn * xh, axis=1, keepdims=True))
        if add is not None:
            dx = dx + add_ref[...]
        dx_ref[...] = dx.astype(out_dtype)

    in_specs = [_rows(), _rows(), _full((1, D_MODEL))]
    args = [dout, xin, g]
    if add is not None:
        in_specs.append(_rows())
        args.append(add)
    return _row_call(body, name, T, in_specs, [_rows(), _full((1, D_MODEL))],
                     [_sds((T, D_MODEL), out_dtype), _sds((1, D_MODEL), F32)], args)


def _ple_fwd(x1, pe, gl):
    T = x1.shape[0]

    def body(x_ref, pe_ref, gl_ref, o_ref):
        o_ref[...] = x_ref[...] + pe_ref[...] * _sigmoid(gl_ref[...])

    return _row_call(body, "ple_fwd", T, [_rows()] * 3, _rows(), _sds((T, D_MODEL), F32), (x1, pe, gl))


def _ple_bwd(dx2, pe, gl):
    T = dx2.shape[0]

    def body(d_ref, pe_ref, gl_ref, dpe_ref, dgl_ref):
        dv = d_ref[...]
        sg = _sigmoid(gl_ref[...])
        dpe_ref[...] = (dv * sg).astype(BF16)
        dgl_ref[...] = (dv * pe_ref[...] * sg * (1.0 - sg)).astype(BF16)

    return _row_call(body, "ple_bwd", T, [_rows()] * 3, [_rows()] * 2,
                     [_sds((T, D_MODEL), BF16)] * 2, (dx2, pe, gl))


def _loss_fwd_bwd(y, target):
    T = y.shape[0]

    def body(y_ref, t_ref, s_ref, d_ref):
        @pl.when(pl.program_id(0) == 0)
        def _():
            s_ref[...] = jnp.zeros_like(s_ref)

        e = y_ref[...] - t_ref[...]
        s_ref[...] += jnp.sum(e * e).reshape(1, 1)
        d_ref[...] = e * (1.0 / D_MODEL)

    return _row_call(body, "loss", T, [_rows()] * 2, [_full((1, 1)), _rows()],
                     [_sds((1, 1), F32), _sds((T, D_MODEL), F32)], (y, target))


def _attn_combine(outs, lses, proj):
    T = proj.shape[0]

    def body(o0, o1, o2, l0, l1, l2, z_ref, a_ref, o_ref, lse_ref):
        a0, a1, a2 = l0[...], l1[...], l2[...]
        m = jnp.maximum(jnp.maximum(a0, a1), a2)
        e0, e1, e2 = jnp.exp(a0 - m), jnp.exp(a1 - m), jnp.exp(a2 - m)
        ssum = e0 + e1 + e2
        o = (e0 * o0[...] + e1 * o1[...] + e2 * o2[...]) / ssum
        zv = z_ref[...].astype(F32)
        o_ref[...] = o
        lse_ref[...] = m + jnp.log(ssum)
        a_ref[...] = (o * zv * _sigmoid(zv)).astype(BF16)

    return _row_call(body, "attn_combine", T, [_rows()] * 6 + [_rows(cb=3 * N_GROUPS)], [_rows()] * 3,
                     [_sds((T, D_MODEL), BF16), _sds((T, D_MODEL), F32), _sds((T, D_MODEL), F32)],
                     (*outs, *lses, proj))


def _gate_bwd(da, o, proj):
    T = da.shape[0]

    def body(da_ref, o_ref, z_ref, do_ref, dz_ref):
        dv = da_ref[...]
        zv = z_ref[...]
        sg = _sigmoid(zv)
        do_ref[...] = dv * zv * sg
        dz_ref[...] = dv * o_ref[...] * sg * (1.0 + zv * (1.0 - sg))

    zcols = _rows(cb=3 * N_GROUPS)
    return _row_call(body, "gate_bwd", T, [_rows(), _rows(), zcols], [_rows(), zcols],
                     [_sds((T, D_MODEL), F32), _sds((T, PROJ_COLS), F32)], (da, o, proj))


def _rope_tables(positions):
    inv_freq = 1.0 / (ROPE_THETA ** (jnp.arange(0, HEAD_DIM, 2, dtype=F32) / HEAD_DIM))
    ang = positions.astype(F32)[..., None] * inv_freq
    cos, sin = jnp.cos(ang), jnp.sin(ang)
    return jnp.tile(cos, (1, 1, 4)), jnp.concatenate([-sin, sin, -sin, sin], axis=-1)


def _rotate_half_partner(t):
    lane = lax.broadcasted_iota(jnp.int32, t.shape, 1)
    return jnp.where((lane % HEAD_DIM) < HEAD_DIM // 2,
                     pltpu.roll(t, LANES - HEAD_DIM // 2, 1), pltpu.roll(t, HEAD_DIM // 2, 1))


def _mask_bias(first):
    qi = lax.broadcasted_iota(jnp.int32, (N_BACK, 2 * N_BACK), 0)
    kj = lax.broadcasted_iota(jnp.int32, (N_BACK, 2 * N_BACK), 1)
    ok = (kj >= qi) & (kj <= qi + N_BACK)
    if first:
        ok = ok & (kj >= N_BACK)
    return jnp.where(ok, 0.0, NEG_INF).astype(F32)


def _stack_heads(t, head0):
    zero = jnp.zeros_like(t)
    return jnp.concatenate([jnp.where(head0, t, zero), jnp.where(head0, zero, t)], axis=0)


def _unstack_heads(t2, head0):
    return jnp.where(head0, t2[:N_BACK], t2[N_BACK:])


def _block_loop(nb, block):
    first, rest = _mask_bias(True), _mask_bias(False)
    first, rest = jnp.concatenate([first, first], axis=0), jnp.concatenate([rest, rest], axis=0)
    if nb <= BLOCK_UNROLL:
        for n in range(nb):
            block(n, first if n == 0 else rest)
        return

    def step(n, carry):
        block(n, jnp.where(n == 0, first, rest))
        return carry

    lax.fori_loop(0, nb, step, 0, unroll=BLOCK_UNROLL)


def _for(count, body, unroll_fully):
    if unroll_fully:
        for i in range(count):
            body(i)
    else:
        lax.fori_loop(0, count, lambda i, carry: (body(i), carry)[1], 0)


def _residues_together(nb):
    return max(1, BLOCK_UNROLL // nb)


def _residue_loop(d, nb, residue):
    together = _residues_together(nb)
    assert d % together == 0

    def group(i, carry):
        for u in range(together):
            residue(i * together + u, u)
        return carry

    lax.fori_loop(0, d // together, group, 0)


_NT = (((1,), (1,)), ((), ()))
_TN = (((0,), (0,)), ((), ()))


def _residue_rows(r, i, d):
    start = r + i * (N_BACK * d)
    if d == 1:
        return pl.ds(pl.multiple_of(start, N_BACK), N_BACK)
    return pl.ds(start, N_BACK, stride=d)


def _seq_rows(i):
    return pl.ds(pl.multiple_of(i * N_BACK, N_BACK), N_BACK)


def _rows_at(base, i, size=N_BACK):
    return pl.ds(pl.multiple_of(base + i * N_BACK, N_BACK), size)


def _attn_fwd(proj, cos, sin, group, Bl, S):
    d = WIN_DIL[group][1]
    L = S // d
    nb = L // N_BACK
    P = L + N_BACK
    assert WIN_DIL[group][0] // d == N_BACK and L % N_BACK == 0

    def body(q_ref, k_ref, v_ref, cos_ref, sin_ref, o_ref, lse_ref, qr, kr, vp):
        head0 = lax.broadcasted_iota(jnp.int32, (1, LANES), 1) < HEAD_DIM
        zeros = jnp.zeros((N_BACK, LANES), BF16)

        def residue(r, u):
            del u
            qbase, kbase = r * L, r * P
            kr[_rows_at(kbase, 0), :] = zeros
            vp[_rows_at(kbase, 0), :] = zeros

            def rope(i):
                rows = _residue_rows(r, i, d)
                cs, sn = cos_ref[rows, :], sin_ref[rows, :]
                q, k = q_ref[rows, :], k_ref[rows, :]
                qr[_rows_at(qbase, i), :] = ((q * cs + _rotate_half_partner(q) * sn)
                                            * (HEAD_DIM ** -0.5)).astype(BF16)
                kr[_rows_at(kbase, i + 1), :] = (k * cs + _rotate_half_partner(k) * sn).astype(BF16)
                vp[_rows_at(kbase, i + 1), :] = v_ref[rows, :].astype(BF16)

            _for(nb, rope, nb <= BLOCK_UNROLL)

            def block(n, bias):
                win = _rows_at(kbase, n, 2 * N_BACK)
                q2, kw, vw = _stack_heads(qr[_rows_at(qbase, n), :], head0), kr[win, :], vp[win, :]
                s = lax.dot_general(q2, kw, _NT, preferred_element_type=F32) + bias
                m = jnp.max(s, axis=1, keepdims=True)
                p = jnp.exp(s - m)
                l = jnp.sum(p, axis=1, keepdims=True)
                pv = jnp.dot(p.astype(BF16), vw, preferred_element_type=F32)
                rows = _residue_rows(r, n, d)
                o_ref[rows, :] = _unstack_heads(pv * (1.0 / l), head0)
                lse_ref[rows, :] = _unstack_heads((m + jnp.log(l)) + jnp.zeros((2 * N_BACK, LANES), F32), head0)

            _block_loop(nb, block)

        _residue_loop(d, nb, residue)

    act = pl.BlockSpec((None, S, LANES), lambda b, hp: (b, 0, hp))
    tab = pl.BlockSpec((None, S, LANES), lambda b, hp: (b, 0, 0))
    col = lambda which: pl.BlockSpec((None, S, LANES),
                                     lambda b, hp: (b, 0, (which * N_GROUPS + group) * HEAD_PAIRS + hp))
    seq = lambda rows: pl.BlockSpec((None, None, rows, LANES), lambda b, hp: (b, hp, 0, 0))
    p3 = proj.reshape(Bl, S, PROJ_COLS)
    o, lse, qr, kr, vp = pl.pallas_call(
        body, name="attn_fwd_g%d" % group, grid=(Bl, HEAD_PAIRS),
        in_specs=[col(0), col(1), col(2), tab, tab], out_specs=[act, act, seq(S), seq(d * P), seq(d * P)],
        out_shape=[_sds((Bl, S, D_MODEL), F32)] * 2 + [_sds((Bl, HEAD_PAIRS, S, LANES), BF16)]
        + [_sds((Bl, HEAD_PAIRS, d * P, LANES), BF16)] * 2,
        compiler_params=_params(("parallel", "arbitrary"), VMEM_LIMIT_ATTN),
    )(p3, p3, p3, cos, sin)
    return o.reshape(Bl * S, D_MODEL), lse.reshape(Bl * S, D_MODEL), (qr, kr, vp)


def _attn_bwd(saved, cos, sin, do, o, lse, dproj, group, Bl, S):
    d = WIN_DIL[group][1]
    L = S // d
    nb = L // N_BACK
    P = L + N_BACK
    steps = Bl * HEAD_PAIRS

    def body(qr, kr, vp, cos_ref, sin_ref, do_ref, o_ref, lse_ref, dproj_in, dproj_ref,
             dk_accs, dv_accs, stage, sems):
        del dproj_in
        head0 = lax.broadcasted_iota(jnp.int32, (1, LANES), 1) < HEAD_DIM
        b, hp = pl.program_id(0), pl.program_id(1)
        step = b * HEAD_PAIRS + hp
        slot = step % 2
        dq_s, dk_s, dv_s = stage.at[slot, 0], stage.at[slot, 1], stage.at[slot, 2]

        def copies(which_slot):
            out = []
            for which in range(3):
                col = ((which * N_GROUPS + group) * HEAD_PAIRS + hp) * LANES
                out.append(pltpu.make_async_copy(
                    stage.at[which_slot, which], dproj_ref.at[b, :, pl.ds(pl.multiple_of(col, LANES), LANES)],
                    sems.at[which_slot, which]))
            return out

        @pl.when(step >= 2)
        def _():
            for cp in copies(slot):
                cp.wait()

        def residue(r, u):
            qbase, kbase = r * L, r * P
            dk_acc, dv_acc = dk_accs.at[u], dv_accs.at[u]
            dk_acc[...] = jnp.zeros_like(dk_acc)
            dv_acc[...] = jnp.zeros_like(dv_acc)

            def block(n, bias):
                win = pl.ds(pl.multiple_of(n * N_BACK, N_BACK), 2 * N_BACK)
                kwin = _rows_at(kbase, n, 2 * N_BACK)
                rows = _residue_rows(r, n, d)
                q2, kw, vw = _stack_heads(qr[_rows_at(qbase, n), :], head0), kr[kwin, :], vp[kwin, :]
                dof = do_ref[rows, :]
                do2 = _stack_heads(dof.astype(BF16), head0)
                lse_b = lse_ref[rows, :]
                lse2 = jnp.concatenate([lse_b[:, 0:1], lse_b[:, HEAD_DIM:HEAD_DIM + 1]], axis=0)
                dsum = _stack_heads(dof * o_ref[rows, :], head0)
                delta = jnp.sum(dsum, axis=1, keepdims=True)
                s = lax.dot_general(q2, kw, _NT, preferred_element_type=F32) + bias
                p = jnp.exp(s - lse2)
                dp = lax.dot_general(do2, vw, _NT, preferred_element_type=F32)
                ds = (p * (dp - delta)).astype(BF16)
                dq = _unstack_heads(jnp.dot(ds, kw, preferred_element_type=F32), head0) * (HEAD_DIM ** -0.5)
                cs, sn = cos_ref[rows, :], sin_ref[rows, :]
                dq_s[rows, :] = dq * cs + _rotate_half_partner(dq * sn)
                dk_acc[win, :] += lax.dot_general(ds, q2, _TN, preferred_element_type=F32)
                dv_acc[win, :] += lax.dot_general(p.astype(BF16), do2, _TN, preferred_element_type=F32)

            _block_loop(nb, block)

            def finish(i):
                rows = _residue_rows(r, i, d)
                cs, sn = cos_ref[rows, :], sin_ref[rows, :]
                dk = dk_acc[_seq_rows(i + 1), :]
                dk_s[rows, :] = dk * cs + _rotate_half_partner(dk * sn)
                dv_s[rows, :] = dv_acc[_seq_rows(i + 1), :]

            _for(nb, finish, nb <= BLOCK_UNROLL)

        _residue_loop(d, nb, residue)
        for cp in copies(slot):
            cp.start()

        @pl.when(step == steps - 1)
        def _():
            if steps > 1:
                for cp in copies(1 - slot):
                    cp.wait()
            for cp in copies(slot):
                cp.wait()

    act = pl.BlockSpec((None, S, LANES), lambda b, hp: (b, 0, hp))
    tab = pl.BlockSpec((None, S, LANES), lambda b, hp: (b, 0, 0))
    seq = lambda rows: pl.BlockSpec((None, None, rows, LANES), lambda b, hp: (b, hp, 0, 0))
    view = lambda t: t.reshape(Bl, S, D_MODEL)
    out = pl.pallas_call(
        body, name="attn_bwd_g%d" % group, grid=(Bl, HEAD_PAIRS),
        in_specs=[seq(S), seq(d * P), seq(d * P), tab, tab, act, act, act, _ANY], out_specs=_ANY,
        out_shape=_sds((Bl, S, PROJ_COLS), F32), input_output_aliases={8: 0},
        scratch_shapes=[pltpu.VMEM((_residues_together(nb), P, LANES), F32),
                        pltpu.VMEM((_residues_together(nb), P, LANES), F32),
                        pltpu.VMEM((2, 3, S, LANES), F32), pltpu.SemaphoreType.DMA((2, 3))],
        compiler_params=_params(("arbitrary", "arbitrary"), VMEM_LIMIT_ATTN),
    )(*saved, cos, sin, view(do), view(o), view(lse), dproj.reshape(Bl, S, PROJ_COLS))
    return out.reshape(Bl * S, PROJ_COLS)


CONV_TILE = 256
CONV_CHUNK = 64
SUBLANES = 8
CONV_SHIFT_ROWS = CONV_TILE + CONV_HALO - SUBLANES


def _fill_shifted(shifted, ext, cs):
    for k in range(1, SUBLANES):
        shifted[k - 1] = ext[pl.ds(k, CONV_SHIFT_ROWS), cs]


def _shifted_rows(shifted, ext, cs, off):
    k = off % SUBLANES
    if k == 0:
        return ext[pl.ds(off, CONV_CHUNK), cs]
    return shifted[k - 1, pl.ds(off - k, CONV_CHUNK), :]


def _conv_fwd(proj, z, dw, dwb, ln_g, ln_b, Bl, S):
    tr = CONV_TILE
    nj = S // tr
    hb = tr // CONV_HALO

    def body(a_ref, b_ref, ah_ref, bh_ref, z_ref, dw_ref, dwb_ref, g_ref, bb_ref, u1_ref, out_ref, ext, shifted):
        j = pl.program_id(1)
        halo = ah_ref[0].astype(F32) * _sigmoid(bh_ref[0].astype(F32))
        ext[pl.ds(0, CONV_HALO), :] = jnp.where(j > 0, halo, 0.0)
        ext[pl.ds(CONV_HALO, tr), :] = a_ref[0].astype(F32) * _sigmoid(b_ref[0].astype(F32))

        def cols(c, carry):
            cs = pl.ds(pl.multiple_of(c * LANES, LANES), LANES)
            _fill_shifted(shifted, ext, cs)
            for rc in range(tr // CONV_CHUNK):
                acc = jnp.zeros((CONV_CHUNK, LANES), F32)
                for w in range(CONV_WIDTH):
                    off = rc * CONV_CHUNK + CONV_HALO - (CONV_WIDTH - 1) + w
                    acc = acc + dw_ref[pl.ds(w, 1), cs] * _shifted_rows(shifted, ext, cs, off)
                u1_ref[0, pl.ds(rc * CONV_CHUNK, CONV_CHUNK), cs] = acc + dwb_ref[:, cs]
            return carry

        lax.fori_loop(0, D_MODEL // LANES, cols, 0)
        u1 = u1_ref[0]
        mu = jnp.mean(u1, axis=1, keepdims=True)
        xc = u1 - mu
        rstd = lax.rsqrt(jnp.mean(xc * xc, axis=1, keepdims=True) + LN_EPS)
        u2 = xc * rstd * g_ref[...] + bb_ref[...]
        zv = z_ref[0].astype(F32)
        out_ref[0] = (u2 * _sigmoid(u2) * zv * _sigmoid(zv)).astype(BF16)

    tile = lambda cb: pl.BlockSpec((1, tr, D_MODEL), lambda b, j: (b, j, cb))
    halo = lambda cb: pl.BlockSpec((1, CONV_HALO, D_MODEL), lambda b, j: (b, jnp.maximum(j * hb - 1, 0), cb))
    par = lambda r: pl.BlockSpec((r, D_MODEL), lambda b, j: (0, 0))
    p3 = proj.reshape(Bl, S, 2 * D_MODEL)
    u1, out = pl.pallas_call(
        body, name="conv_fwd", grid=(Bl, nj),
        in_specs=[tile(0), tile(1), halo(0), halo(1), tile(0), par(32), par(1), par(1), par(1)],
        out_specs=[tile(0), tile(0)],
        out_shape=[_sds((Bl, S, D_MODEL), F32), _sds((Bl, S, D_MODEL), BF16)],
        scratch_shapes=[pltpu.VMEM((tr + CONV_HALO, D_MODEL), F32),
                        pltpu.VMEM((SUBLANES - 1, CONV_SHIFT_ROWS, LANES), F32)],
        compiler_params=_params(("parallel", "arbitrary")),
    )(p3, p3, p3, p3, z.reshape(Bl, S, D_MODEL), dw, dwb, ln_g, ln_b)
    return u1.reshape(Bl * S, D_MODEL), out.reshape(Bl * S, D_MODEL)


def _conv_norm_bwd(da2, z, u1, ln_g, ln_b):
    T = da2.shape[0]

    def body(da_ref, z_ref, u_ref, g_ref, b_ref, du_ref, dz_ref, dg_ref, db_ref):
        @pl.when(pl.program_id(0) == 0)
        def _():
            dg_ref[...] = jnp.zeros_like(dg_ref)
            db_ref[...] = jnp.zeros_like(db_ref)

        u1 = u_ref[...]
        mu = jnp.mean(u1, axis=1, keepdims=True)
        xc = u1 - mu
        rstd = lax.rsqrt(jnp.mean(xc * xc, axis=1, keepdims=True) + LN_EPS)
        nrm = xc * rstd
        u2 = nrm * g_ref[...] + b_ref[...]
        s2 = _sigmoid(u2)
        zv = z_ref[...].astype(F32)
        sz = _sigmoid(zv)
        dv = da_ref[...]
        dz_ref[...] = (dv * u2 * s2 * sz * (1.0 + zv * (1.0 - sz))).astype(BF16)
        du2 = dv * zv * sz * s2 * (1.0 + u2 * (1.0 - s2))
        dg_ref[...] += jnp.sum(du2 * nrm, axis=0, keepdims=True)
        db_ref[...] += jnp.sum(du2, axis=0, keepdims=True)
        dn = du2 * g_ref[...]
        du_ref[...] = rstd * (dn - jnp.mean(dn, axis=1, keepdims=True)
                              - nrm * jnp.mean(dn * nrm, axis=1, keepdims=True))

    return _row_call(body, "conv_norm_bwd", T,
                     [_rows(), _rows(), _rows(), _full((1, D_MODEL)), _full((1, D_MODEL))],
                     [_rows(), _rows(), _full((1, D_MODEL)), _full((1, D_MODEL))],
                     [_sds((T, D_MODEL), F32), _sds((T, D_MODEL), BF16), _sds((1, D_MODEL), F32),
                      _sds((1, D_MODEL), F32)], (da2, z, u1, ln_g, ln_b))


def _conv_bwd(proj, du1, dw, Bl, S):
    tr = CONV_TILE
    nj = S // tr
    hb = tr // CONV_HALO

    def body(a_ref, b_ref, ah_ref, bh_ref, du_ref, duh_ref, dw_ref, dab_ref, ddw_ref, ddb_ref, uext, dext, du0,
             ushift, dshift, ddw8):
        first = (pl.program_id(0) == 0) & (pl.program_id(1) == 0)
        last = (pl.program_id(0) == Bl - 1) & (pl.program_id(1) == nj - 1)
        j = pl.program_id(1)

        @pl.when(first)
        def _():
            ddw8[...] = jnp.zeros_like(ddw8)
            ddb_ref[...] = jnp.zeros_like(ddb_ref)

        halo = ah_ref[0].astype(F32) * _sigmoid(bh_ref[0].astype(F32))
        uext[pl.ds(0, CONV_HALO), :] = jnp.where(j > 0, halo, 0.0)
        av = a_ref[0].astype(F32)
        sb = _sigmoid(b_ref[0].astype(F32))
        uext[pl.ds(CONV_HALO, tr), :] = av * sb
        dext[pl.ds(0, tr), :] = du_ref[0]
        dext[pl.ds(tr, CONV_HALO), :] = jnp.where(j < nj - 1, duh_ref[0], 0.0)
        ddb_ref[...] += jnp.sum(du_ref[0], axis=0, keepdims=True)

        def cols(c, carry):
            cs = pl.ds(pl.multiple_of(c * LANES, LANES), LANES)
            _fill_shifted(dshift, dext, cs)
            _fill_shifted(ushift, uext, cs)
            for rc in range(tr // CONV_CHUNK):
                base = rc * CONV_CHUNK
                acc = jnp.zeros((CONV_CHUNK, LANES), F32)
                for w in range(CONV_WIDTH):
                    acc = acc + dw_ref[pl.ds(w, 1), cs] * _shifted_rows(dshift, dext, cs, base + CONV_WIDTH - 1 - w)
                du0[pl.ds(base, CONV_CHUNK), cs] = acc
            for w in range(CONV_WIDTH):
                part = jnp.zeros((SUBLANES, LANES), F32)
                for rc in range(tr // CONV_CHUNK):
                    base = rc * CONV_CHUNK
                    prod = dext[pl.ds(base, CONV_CHUNK), cs] * _shifted_rows(
                        ushift, uext, cs, base + CONV_HALO - (CONV_WIDTH - 1) + w)
                    for i in range(CONV_CHUNK // SUBLANES):
                        part = part + prod[i * SUBLANES:(i + 1) * SUBLANES]
                ddw8[pl.ds(w * SUBLANES, SUBLANES), cs] += part
            return carry

        lax.fori_loop(0, D_MODEL // LANES, cols, 0)
        g = du0[...]
        dab_ref[0, :, 0:D_MODEL] = (g * sb).astype(BF16)
        dab_ref[0, :, D_MODEL:2 * D_MODEL] = (g * av * sb * (1.0 - sb)).astype(BF16)

        @pl.when(last)
        def _():
            for w in range(CONV_WIDTH + 1):
                ddw_ref[pl.ds(w, 1), :] = jnp.sum(ddw8[pl.ds(w * SUBLANES, SUBLANES), :], axis=0, keepdims=True)

    tile = lambda cb: pl.BlockSpec((1, tr, D_MODEL), lambda b, j: (b, j, cb))
    halo = lambda cb: pl.BlockSpec((1, CONV_HALO, D_MODEL), lambda b, j: (b, jnp.maximum(j * hb - 1, 0), cb))
    nxt = pl.BlockSpec((1, CONV_HALO, D_MODEL), lambda b, j: (b, jnp.minimum((j + 1) * hb, S // CONV_HALO - 1), 0))
    par = lambda r: pl.BlockSpec((r, D_MODEL), lambda b, j: (0, 0))
    p3 = proj.reshape(Bl, S, 2 * D_MODEL)
    d3 = du1.reshape(Bl, S, D_MODEL)
    dab, ddw, ddb = pl.pallas_call(
        body, name="conv_bwd", grid=(Bl, nj),
        in_specs=[tile(0), tile(1), halo(0), halo(1), tile(0), nxt, par(32)],
        out_specs=[pl.BlockSpec((1, tr, 2 * D_MODEL), lambda b, j: (b, j, 0)), par(32), par(1)],
        out_shape=[_sds((Bl, S, 2 * D_MODEL), BF16), _sds((32, D_MODEL), F32), _sds((1, D_MODEL), F32)],
        scratch_shapes=[pltpu.VMEM((tr + CONV_HALO, D_MODEL), F32), pltpu.VMEM((tr + CONV_HALO, D_MODEL), F32),
                        pltpu.VMEM((tr, D_MODEL), F32),
                        pltpu.VMEM((SUBLANES - 1, CONV_SHIFT_ROWS, LANES), F32),
                        pltpu.VMEM((SUBLANES - 1, CONV_SHIFT_ROWS, LANES), F32),
                        pltpu.VMEM(((CONV_WIDTH + 1) * SUBLANES, D_MODEL), F32)],
        compiler_params=_params(("arbitrary", "arbitrary")),
    )(p3, p3, p3, p3, d3, d3, dw)
    return dab.reshape(Bl * S, 2 * D_MODEL), ddw, ddb


_LAYOUT = (
    ("pre_norm_g", (4, 1024), None), ("post_norm_g", (4, 1024), None),
    ("attn_w_in", (2, 1024, 2560), 2), ("attn_w_out", (2, 256, 1024), 1),
    ("conv_w_in", (2, 1024, 768), 2), ("conv_dw_w", (2, 31, 256), 2),
    ("conv_dw_b", (2, 256), 1), ("conv_ln_g", (2, 256), 1), ("conv_ln_b", (2, 256), 1),
    ("conv_w_out", (2, 256, 1024), 1), ("ple_w_proj", (4, 256, 256), 2), ("ple_w_gate", (4, 256, 1024), 1),
)
_MATMUL_WEIGHTS = ("attn_w_in", "attn_w_out", "conv_w_in", "conv_w_out", "ple_w_proj", "ple_w_gate")
_AXIS = {n: a for n, _, a in _LAYOUT}


def _size(shape):
    n = 1
    for s in shape:
        n *= s
    return n


def _padded_rows(shape):
    rows = _size(shape) // shape[-1]
    return rows + (-rows) % FLAT_ROW_ALIGN


def _rows2d(a):
    a2 = a.reshape(-1, a.shape[-1])
    pad = _padded_rows(a.shape) - a2.shape[0]
    return jnp.pad(a2, ((0, pad), (0, 0))) if pad else a2


def _col_blocks(a):
    a2 = _rows2d(a)
    return jnp.concatenate([a2[:, c:c + FLAT_COLS] for c in range(0, a2.shape[1], FLAT_COLS)], axis=0)


def _from_col_blocks(flat, off, shape):
    rows, nblk = _padded_rows(shape), shape[-1] // FLAT_COLS
    a2 = jnp.concatenate([flat[off + b * rows:off + (b + 1) * rows] for b in range(nblk)], axis=1)
    return a2[:_size(shape) // shape[-1]].reshape(shape), off + nblk * rows


def _shard_col_blocks(full, shape, axis):
    if axis is None:
        blocks = _col_blocks(full)
        return jnp.broadcast_to(blocks[None], (N_CHIPS,) + blocks.shape)
    m = shape[-1]
    if axis == len(shape) - 1:
        a2 = _rows2d(full)
        pieces = [a2[:, c:c + FLAT_COLS] for c in range(0, N_CHIPS * m, FLAT_COLS)]
    else:
        layers, r, _ = shape
        assert axis == 1 and (layers * r) % FLAT_ROW_ALIGN == 0
        pieces = [full[:, s * r:(s + 1) * r, c:c + FLAT_COLS].reshape(layers * r, FLAT_COLS)
                  for s in range(N_CHIPS) for c in range(0, m, FLAT_COLS)]
    return jnp.concatenate(pieces, axis=0).reshape(N_CHIPS, -1, FLAT_COLS)


def _unpack_f32(flat):
    out, off = {}, 0
    for n, shape, _ in _LAYOUT:
        out[n], off = _from_col_blocks(flat, off, shape)
    return out


SMALL_ROWS = 40


def _stack_small(w):
    rows = [w["conv_dw_w"]] + [w[n][:, None, :] for n in ("conv_dw_b", "conv_ln_g", "conv_ln_b")]
    stacked = jnp.concatenate(rows, axis=1)
    return jnp.pad(stacked, ((0, 0), (0, SMALL_ROWS - stacked.shape[1]), (0, 0)))


def _unstack_small(small):
    return {"conv_dw_w": small[:, :CONV_WIDTH], "conv_dw_b": small[:, CONV_WIDTH],
            "conv_ln_g": small[:, CONV_WIDTH + 1], "conv_ln_b": small[:, CONV_WIDTH + 2]}


def _pack_full_grads(grads):
    flat = jnp.concatenate([_shard_col_blocks(grads[n], shape, axis) for n, shape, axis in _LAYOUT], axis=1)
    assert flat.shape[1] <= FLAT_ROWS
    return jnp.pad(flat, ((0, 0), (0, FLAT_ROWS - flat.shape[1]), (0, 0)))


_ANY = pl.BlockSpec(memory_space=pl.ANY)


def _mesh_pos():
    return lax.axis_index("x"), lax.axis_index("y"), lax.axis_index("c")


def _other_chips(x, y):
    return [(1 - x, y), (x, 1 - y), (1 - x, 1 - y)]


COPIES_PER_ARRAY = 7


def _allgather_weights(shards, axes, small):
    n = len(shards)
    full_shape = lambda a, axis: tuple(d * (N_CHIPS if i == axis else 1) for i, d in enumerate(a.shape))

    def body(*refs):
        ins, small_in = refs[:n], refs[n]
        outs, small_out = refs[n + 1:2 * n + 1], refs[2 * n + 1]
        send_sems, recv_sems = refs[2 * n + 2:]
        x, y, c = _mesh_pos()
        mine, me, sibling = 2 * x + y, (x, y, c), (x, y, 1 - c)
        chips = _other_chips(x, y)

        def region(a, chip, half):
            _, rows, cols = shards[a].shape
            h = rows // 2
            if axes[a] == 2:
                return outs[a].at[:, slice(None) if half is None else pl.ds(half * h, h), pl.ds(chip * cols, cols)]
            if half is None:
                return outs[a].at[:, pl.ds(chip * rows, rows), :]
            return outs[a].at[:, pl.ds(chip * rows + half * h, h), :]

        def copy(k, src, dst, to):
            return pltpu.make_async_remote_copy(src_ref=src, dst_ref=dst, send_sem=send_sems.at[k],
                                                recv_sem=recv_sems.at[k], device_id=to, device_id_type=MESH)

        def arrival(k, dst):
            return copy(k, dst, dst, me)

        sends = []
        for a in range(n):
            h = shards[a].shape[1] // 2
            base = a * COPIES_PER_ARRAY
            sends.append(copy(base + 6, ins[a], region(a, mine, None), sibling))
            for j, (cx, cy) in enumerate(chips):
                sends.append(copy(base + j, ins[a].at[:, pl.ds(c * h, h), :], region(a, mine, c), (cx, cy, c)))
        small_cols = small.shape[2]
        small_region = lambda chip: small_out.at[:, :, pl.ds(chip * small_cols, small_cols)]
        base = n * COPIES_PER_ARRAY
        sends.append(copy(base + 3, small_in, small_region(mine), sibling))
        for j, (cx, cy) in enumerate(chips):
            sends.append(copy(base + j, small_in, small_region(mine), (cx, cy, c)))
        for cp in sends:
            cp.start()
        for j, (cx, cy) in enumerate(chips):
            for a in range(n):
                k = a * COPIES_PER_ARRAY + j
                arrival(k, region(a, 2 * cx + cy, c)).wait_recv()
                passed = copy(k + 3, region(a, 2 * cx + cy, c), region(a, 2 * cx + cy, c), sibling)
                passed.start()
                sends.append(passed)
        for j, (cx, cy) in enumerate(chips):
            for a in range(n):
                arrival(a * COPIES_PER_ARRAY + 3 + j, region(a, 2 * cx + cy, 1 - c)).wait_recv()
            arrival(base + j, small_region(2 * cx + cy)).wait_recv()
        for a in range(n):
            arrival(a * COPIES_PER_ARRAY + 6, region(a, mine, None)).wait_recv()
        arrival(base + 3, small_region(mine)).wait_recv()
        for cp in sends:
            cp.wait_send()

    n_sems = n * COPIES_PER_ARRAY + 4
    out = pl.pallas_call(
        body, name="allgather_weights", in_specs=[_ANY] * (n + 1), out_specs=[_ANY] * (n + 1),
        out_shape=[_sds(full_shape(a, axis), a.dtype) for a, axis in zip(shards, axes)]
        + [_sds(full_shape(small, 2), small.dtype)],
        scratch_shapes=[pltpu.SemaphoreType.DMA((n_sems,)), pltpu.SemaphoreType.DMA((n_sems,))],
    )(*shards, small)
    return out[:n], out[n]


def _exchange_core_halves(g):
    n, _, H, C = g.shape

    def body(g_ref, got_ref, send_sem, recv_sem):
        x, y, c = _mesh_pos()
        swap = pltpu.make_async_remote_copy(
            src_ref=g_ref.at[pl.ds(0, n), 1 - c], dst_ref=got_ref, send_sem=send_sem, recv_sem=recv_sem,
            device_id=(x, y, 1 - c), device_id_type=MESH)
        swap.start()
        swap.wait()

    return pl.pallas_call(
        body, name="exchange_core_halves", in_specs=[_ANY], out_specs=_ANY,
        out_shape=_sds((n, H, C), g.dtype),
        scratch_shapes=[pltpu.SemaphoreType.DMA, pltpu.SemaphoreType.DMA],
    )(g)


def _scatter_to_chips(p):
    n, H, C = p.shape

    def body(p_ref, q_ref, send_sems, recv_sems):
        x, y, c = _mesh_pos()
        chips = _other_chips(x, y)
        sends = [pltpu.make_async_remote_copy(
            src_ref=p_ref.at[2 * cx + cy], dst_ref=q_ref.at[j], send_sem=send_sems.at[j],
            recv_sem=recv_sems.at[j], device_id=(cx, cy, c), device_id_type=MESH)
            for j, (cx, cy) in enumerate(chips)]
        for cp in sends:
            cp.start()
        for cp in sends:
            cp.wait_recv()
        for cp in sends:
            cp.wait_send()

    return pl.pallas_call(
        body, name="scatter_to_chips", in_specs=[_ANY], out_specs=_ANY,
        out_shape=_sds((n - 1, H, C), p.dtype),
        scratch_shapes=[pltpu.SemaphoreType.DMA((3,)), pltpu.SemaphoreType.DMA((3,))],
    )(p)


def _share_core_halves(r2):
    _, H, C = r2.shape

    def body(r_ref, out_ref, send_sem, recv_sem):
        x, y, c = _mesh_pos()
        send = pltpu.make_async_remote_copy(
            src_ref=r_ref.at[c], dst_ref=out_ref.at[c], send_sem=send_sem, recv_sem=recv_sem,
            device_id=(x, y, 1 - c), device_id_type=MESH)
        send.start()
        send.wait_send()
        pltpu.make_async_remote_copy(
            src_ref=r_ref.at[c], dst_ref=out_ref.at[1 - c], send_sem=send_sem, recv_sem=recv_sem,
            device_id=(x, y, 1 - c), device_id_type=MESH).wait_recv()

    return pl.pallas_call(
        body, name="share_core_halves", in_specs=[_ANY], out_specs=_ANY,
        out_shape=_sds(r2.shape, r2.dtype), input_output_aliases={0: 0},
        scratch_shapes=[pltpu.SemaphoreType.DMA, pltpu.SemaphoreType.DMA],
    )(r2)


def _place():
    x, y, c = _mesh_pos()
    return jnp.stack([c, 2 * x + y]).astype(jnp.int32)


def _sum_pair(g, got, place):
    n, _, H, C = g.shape

    def body(place_ref, a_ref, b_ref, o_ref):
        o_ref[...] = (a_ref[...] + b_ref[...]).astype(BF16)

    spec = pl.BlockSpec((1, FLAT_TILE, C), lambda s, i, pr: (s, i, 0))
    return pl.pallas_call(
        body, name="sum_core_pair",
        grid_spec=pltpu.PrefetchScalarGridSpec(
            num_scalar_prefetch=1, grid=(n, H // FLAT_TILE),
            in_specs=[pl.BlockSpec((1, None, FLAT_TILE, C), lambda s, i, pr: (s, pr[0], i, 0)), spec],
            out_specs=spec),
        out_shape=_sds((n, H, C), BF16),
        compiler_params=_params(("parallel", "parallel")))(place, g, got)


def _sum_chips(p, q, place):
    n, H, C = p.shape

    def body(place_ref, own_ref, qx_ref, qy_ref, qxy_ref, o_ref):
        mine = place_ref[1]
        own, qx, qy, qxy = (t[0].astype(F32) for t in (own_ref, qx_ref, qy_ref, qxy_ref))

        def term(s):
            rel = jnp.full(own.shape, mine ^ s, jnp.int32)
            return jnp.where(rel == 0, own, jnp.where(rel == 2, qx, jnp.where(rel == 1, qy, qxy)))

        o_ref[0] = ((term(0) + term(1)) + term(2)) + term(3)

    qspec = lambda j: pl.BlockSpec((1, FLAT_TILE, C), lambda i, pr: (j, i, 0))
    return pl.pallas_call(
        body, name="sum_chips",
        grid_spec=pltpu.PrefetchScalarGridSpec(
            num_scalar_prefetch=1, grid=(H // FLAT_TILE,),
            in_specs=[pl.BlockSpec((1, FLAT_TILE, C), lambda i, pr: (pr[1], i, 0)), qspec(0), qspec(1), qspec(2)],
            out_specs=pl.BlockSpec((1, FLAT_TILE, C), lambda i, pr: (pr[0], i, 0))),
        out_shape=_sds((2, H, C), F32),
        compiler_params=_params(("parallel",)))(place, p, q, q, q)


ADAMW_BLOCK = 1 << 18


def _adamw(w, g, m, v):
    shape = w.shape
    C = shape[-1]
    R = _size(shape) // C
    tr = R
    while tr * C > ADAMW_BLOCK and tr % 16 == 0:
        tr //= 2
    w, g, m, v = (t.reshape(R, C) for t in (w, g, m, v))

    def body(w_ref, g_ref, m_ref, v_ref, d_ref, nm_ref, nv_ref):
        gv = g_ref[...]
        nm = ADAM_B1 * m_ref[...] + (1.0 - ADAM_B1) * gv
        nv = ADAM_B2 * v_ref[...] + (1.0 - ADAM_B2) * (gv * gv)
        m_hat = nm / (1.0 - ADAM_B1 ** ADAM_STEP)
        v_hat = nv / (1.0 - ADAM_B2 ** ADAM_STEP)
        d_ref[...] = -ADAM_LR * (m_hat / (jnp.sqrt(v_hat) + ADAM_EPS) + ADAM_WD * w_ref[...])
        nm_ref[...] = nm
        nv_ref[...] = nv

    spec = pl.BlockSpec((tr, C), lambda i: (i, 0))
    outs = pl.pallas_call(body, name="adamw", grid=(R // tr,), in_specs=[spec] * 4, out_specs=[spec] * 3,
                          out_shape=[_sds((R, C), F32)] * 3, compiler_params=_params(("parallel",)))(w, g, m, v)
    return tuple(t.reshape(shape) for t in outs)


def _reduce_scatter_grads(gfull):
    n, R, C = gfull.shape
    place = _place()
    g4 = gfull.reshape(n, 2, R // 2, C)
    p = _sum_pair(g4, _exchange_core_halves(g4), place)
    return _share_core_halves(_sum_chips(p, _scatter_to_chips(p), place)).reshape(R, C)


def _local_step(x, p, positions, loss_target, pre_g, post_g, w):
    Bl, S, _ = x.shape
    T = Bl * S
    cos, sin = _rope_tables(positions)
    xs = x.reshape(T, D_MODEL)
    saved = []
    for i in range(DEPTH):
        j = i // 2
        g_pre, g_post = pre_g[i:i + 1], post_g[i:i + 1]
        h = _rmsnorm_fwd(xs, g_pre)
        st = {"x": xs, "h": h}
        if i % 2 == 0:
            proj = _mm(h, w["attn_w_in"][j], name="attn_in")
            res = [_attn_fwd(proj, cos, sin, g, Bl, S) for g in range(N_GROUPS)]
            a, o, lse = _attn_combine([r[0] for r in res], [r[1] for r in res], proj)
            y = _mm(a, w["attn_w_out"][j], name="attn_out")
            st.update(proj=proj, a=a, o=o, lse=lse, qkv=[r[2] for r in res])
        else:
            w_ab, w_z = w["conv_w_in"][j][:, :2 * D_MODEL], w["conv_w_in"][j][:, 2 * D_MODEL:]
            ab = _mm(h, w_ab, out_dtype=BF16, name="conv_in_ab")
            z = _mm(h, w_z, out_dtype=BF16, name="conv_in_z")
            dw = jnp.pad(w["conv_dw_w"][j], ((0, 1), (0, 0)))
            u1, a = _conv_fwd(ab, z, dw, w["conv_dw_b"][j:j + 1], w["conv_ln_g"][j:j + 1],
                              w["conv_ln_b"][j:j + 1], Bl, S)
            y = _mm(a, w["conv_w_out"][j], name="conv_out")
            st.update(w_ab=w_ab, w_z=w_z, ab=ab, z=z, dw=dw, u1=u1, a=a)
        x1 = _post_fwd(xs, y, g_post)
        pi = p[i].reshape(T, PLE_DIM)
        pe = _mm(pi, w["ple_w_proj"][i], name="ple_proj")
        gl = _mm(x1, w["ple_w_gate"][i], name="ple_gate")
        xs = _ple_fwd(x1, pe, gl)
        st.update(y=y, x1=x1, pi=pi, pe=pe, gl=gl)
        saved.append(st)

    sq, dx = _loss_fwd_bwd(xs, loss_target.reshape(T, D_MODEL))

    grads = {n: [None] * shape[0] for n, shape, _ in _LAYOUT}
    for i in reversed(range(DEPTH)):
        j = i // 2
        st = saved[i]
        g_pre, g_post = pre_g[i:i + 1], post_g[i:i + 1]
        dpe, dgl = _ple_bwd(dx, st["pe"], st["gl"])
        grads["ple_w_proj"][i] = _mm(st["pi"], dpe, ta=True, name="ple_proj_wgrad")
        grads["ple_w_gate"][i] = _mm(st["x1"], dgl, ta=True, name="ple_gate_wgrad")
        dx1 = _mm(dgl, w["ple_w_gate"][i], tb=True, add=dx, name="ple_gate_dgrad")
        dy, dg_post = _rmsnorm_bwd(dx1, st["y"], g_post, None, BF16, "post_bwd")
        grads["post_norm_g"][i] = dg_post[0]
        if i % 2 == 0:
            grads["attn_w_out"][j] = _mm(st["a"], dy, ta=True, name="attn_out_wgrad")
            da = _mm(dy, w["attn_w_out"][j], tb=True, name="attn_out_dgrad")
            do, dproj = _gate_bwd(da, st["o"], st["proj"])
            for g in range(N_GROUPS):
                dproj = _attn_bwd(st["qkv"][g], cos, sin, do, st["o"], st["lse"], dproj, g, Bl, S)
            dh = _mm(dproj, w["attn_w_in"][j], tb=True, tk=2048, name="attn_in_dgrad")
            grads["attn_w_in"][j] = _mm(st["h"], dproj, ta=True, name="attn_in_wgrad")
        else:
            grads["conv_w_out"][j] = _mm(st["a"], dy, ta=True, name="conv_out_wgrad")
            da2 = _mm(dy, w["conv_w_out"][j], tb=True, name="conv_out_dgrad")
            du1, dz, dln_g, dln_b = _conv_norm_bwd(da2, st["z"], st["u1"], w["conv_ln_g"][j:j + 1],
                                                   w["conv_ln_b"][j:j + 1])
            dab, ddw, ddb = _conv_bwd(st["ab"], du1, st["dw"], Bl, S)
            dh = _mm(dz, st["w_z"], tb=True, name="conv_in_z_dgrad")
            dh = _mm(dab, st["w_ab"], tb=True, add=dh, name="conv_in_ab_dgrad")
            dw_ab = _mm(st["h"], dab, ta=True, name="conv_in_ab_wgrad")
            dw_z = _mm(st["h"], dz, ta=True, name="conv_in_z_wgrad")
            grads["conv_w_in"][j] = jnp.concatenate([dw_ab, dw_z], axis=1)
            grads["conv_dw_w"][j] = ddw[:CONV_WIDTH]
            grads["conv_dw_b"][j] = ddb[0]
            grads["conv_ln_g"][j] = dln_g[0]
            grads["conv_ln_b"][j] = dln_b[0]
        dx, dg_pre = _rmsnorm_bwd(dh, st["x"], g_pre, dx1, F32, "pre_bwd")
        grads["pre_norm_g"][i] = dg_pre[0]
    grads = {n: jnp.stack(v) for n, v in grads.items()}
    return sq, dx.reshape(Bl, S, D_MODEL), grads


_NAMES = tuple(n for n, _, _ in _LAYOUT)


def kernel(x, p, positions, pre_norm_g, post_norm_g, attn_w_in, attn_w_out, conv_w_in, conv_dw_w, conv_dw_b, conv_ln_g, conv_ln_b, conv_w_out, ple_w_proj, ple_w_gate, loss_target, m_pre_norm_g, m_post_norm_g, m_attn_w_in, m_attn_w_out, m_conv_w_in, m_conv_dw_w, m_conv_dw_b, m_conv_ln_g, m_conv_ln_b, m_conv_w_out, m_ple_w_proj, m_ple_w_gate, v_pre_norm_g, v_post_norm_g, v_attn_w_in, v_attn_w_out, v_conv_w_in, v_conv_dw_w, v_conv_dw_b, v_conv_ln_g, v_conv_ln_b, v_conv_w_out, v_ple_w_proj, v_ple_w_gate):
    w_loc = dict(zip(_NAMES, (pre_norm_g, post_norm_g, attn_w_in, attn_w_out, conv_w_in, conv_dw_w, conv_dw_b,
                              conv_ln_g, conv_ln_b, conv_w_out, ple_w_proj, ple_w_gate)))
    m_loc = dict(zip(_NAMES, (m_pre_norm_g, m_post_norm_g, m_attn_w_in, m_attn_w_out, m_conv_w_in, m_conv_dw_w,
                              m_conv_dw_b, m_conv_ln_g, m_conv_ln_b, m_conv_w_out, m_ple_w_proj, m_ple_w_gate)))
    v_loc = dict(zip(_NAMES, (v_pre_norm_g, v_post_norm_g, v_attn_w_in, v_attn_w_out, v_conv_w_in, v_conv_dw_w,
                              v_conv_dw_b, v_conv_ln_g, v_conv_ln_b, v_conv_w_out, v_ple_w_proj, v_ple_w_gate)))

    gathered, small = _allgather_weights([w_loc[n].astype(BF16) for n in _MATMUL_WEIGHTS],
                                         [_AXIS[n] for n in _MATMUL_WEIGHTS], _stack_small(w_loc))
    w_full = dict(zip(_MATMUL_WEIGHTS, gathered), **_unstack_small(small))
    sq, grad_x, grads = _local_step(x, p, positions, loss_target, pre_norm_g, post_norm_g, w_full)
    loss = lax.psum(sq[0, 0] * (0.5 / D_MODEL), ("x", "y", "c"))

    g_flat = _reduce_scatter_grads(_pack_full_grads(grads))
    g_out = _unpack_f32(g_flat)
    updates = {n: _adamw(w_loc[n], g_out[n], m_loc[n], v_loc[n]) for n in _NAMES}
    d_out, m_out, v_out = ({n: updates[n][k] for n in _NAMES} for k in range(3))
    return (loss, grad_x, *[g_out[n] for n in _NAMES], *[d_out[n] for n in _NAMES],
            *[m_out[n] for n in _NAMES], *[v_out[n] for n in _NAMES])
```

```python
import jax
import jax.numpy as jnp
from jax import lax
from jax.experimental import pallas as pl
from jax.experimental.pallas import tpu as pltpu

F32 = jnp.float32
BF16 = jnp.bfloat16

D_MODEL = 1024
DEPTH = 4
PLE_DIM = 256
HEAD_DIM = 64
WIN_DIL = ((128, 1), (512, 4), (2048, 16))
N_GROUPS = 3
N_BACK = 128
BLOCK_UNROLL = 4
ROPE_THETA = 10000.0
CONV_WIDTH = 31
CONV_HALO = 32
RMS_EPS = 1e-6
LN_EPS = 1e-5
NEG_INF = -1e30
ADAM_LR, ADAM_B1, ADAM_B2, ADAM_EPS, ADAM_WD, ADAM_STEP = 0.001, 0.9, 0.999, 1e-08, 0.01, 10

LANES = 128
N_CHIPS = 4
VMEM_LIMIT = 48 * 1024 * 1024
VMEM_LIMIT_ATTN = 56 * 1024 * 1024
FLAT_COLS = 256
FLAT_ROWS = 36864
FLAT_TILE = 2048
FLAT_ROW_ALIGN = 16
PROJ_COLS = (3 * N_GROUPS + 1) * D_MODEL
HEAD_PAIRS = D_MODEL // LANES

MESH = pl.DeviceIdType.MESH


def _params(sem=None, vmem=VMEM_LIMIT):
    return pltpu.CompilerParams(dimension_semantics=sem, vmem_limit_bytes=vmem)


def _sigmoid(v):
    return 1.0 / (1.0 + jnp.exp(-v))


def _mm(a, b, *, ta=False, tb=False, add=None, out_dtype=F32, tm=1024, tn=1024, tk=1024, name="mm"):
    if ta:
        K, M = a.shape
    else:
        M, K = a.shape
    if tb:
        N, K2 = b.shape
    else:
        K2, N = b.shape
    assert K == K2, (a.shape, b.shape)
    tm, tn, tk = min(tm, M), min(tn, N), min(tk, K)
    assert M % tm == 0 and N % tn == 0 and K % tk == 0
    nk = K // tk
    dims = (((0 if ta else 1,), (1 if tb else 0,)), ((), ()))

    def body(*refs):
        if add is None:
            a_ref, b_ref, o_ref = refs[:3]
        else:
            a_ref, b_ref, add_ref, o_ref = refs[:4]
        k = pl.program_id(2)
        part = lax.dot_general(a_ref[...].astype(BF16), b_ref[...].astype(BF16), dims, preferred_element_type=F32)

        def finish(r):
            if add is not None:
                r = r + add_ref[...].astype(F32)
            o_ref[...] = r.astype(out_dtype)

        if nk == 1:
            finish(part)
        else:
            acc_ref = refs[-1]

            @pl.when(k == 0)
            def _():
                acc_ref[...] = part

            @pl.when((k > 0) & (k < nk - 1))
            def _():
                acc_ref[...] += part

            @pl.when(k == nk - 1)
            def _():
                finish(acc_ref[...] + part)

    a_spec = pl.BlockSpec((tk, tm), lambda i, j, k: (k, i)) if ta else pl.BlockSpec((tm, tk), lambda i, j, k: (i, k))
    b_spec = pl.BlockSpec((tn, tk), lambda i, j, k: (j, k)) if tb else pl.BlockSpec((tk, tn), lambda i, j, k: (k, j))
    o_spec = pl.BlockSpec((tm, tn), lambda i, j, k: (i, j))
    in_specs, args = [a_spec, b_spec], [a, b]
    if add is not None:
        in_specs.append(o_spec)
        args.append(add)
    return pl.pallas_call(
        body, name=name, grid=(M // tm, N // tn, nk),
        in_specs=in_specs, out_specs=o_spec,
        out_shape=jax.ShapeDtypeStruct((M, N), out_dtype),
        scratch_shapes=[pltpu.VMEM((tm, tn), F32)] if nk > 1 else [],
        compiler_params=_params(("parallel", "parallel", "arbitrary")),
    )(*args)


def _mm_rows(a, b, extras, epilogue, out_dtypes, name, tm=512):
    M, K = a.shape
    N = b.shape[1]
    n_ex = len(extras)

    def body(*refs):
        a_ref, b_ref = refs[:2]
        av = a_ref[...]
        acc = jnp.dot(av.astype(BF16), b_ref[...].astype(BF16), preferred_element_type=F32)
        results = epilogue(acc, av, *[e[...] for e in refs[2:2 + n_ex]])
        for o_ref, r in zip(refs[2 + n_ex:], results):
            o_ref[...] = r.astype(o_ref.dtype)

    tile = pl.BlockSpec((tm, N), lambda i: (i, 0))
    in_specs = [pl.BlockSpec((tm, K), lambda i: (i, 0)), pl.BlockSpec((K, N), lambda i: (0, 0))]
    in_specs += [tile if e.shape[0] == M else pl.BlockSpec((1, N), lambda i: (0, 0)) for e in extras]
    return pl.pallas_call(
        body, name=name, grid=(M // tm,), in_specs=in_specs, out_specs=[tile] * len(out_dtypes),
        out_shape=[jax.ShapeDtypeStruct((M, N), dt) for dt in out_dtypes],
        compiler_params=_params(("parallel",)),
    )(a, b, *extras)


ROW_TILE = 512


def _rows(w=D_MODEL, cb=0, tr=ROW_TILE):
    return pl.BlockSpec((tr, w), lambda i: (i, cb))


def _full(shape):
    return pl.BlockSpec(shape, lambda i: (0,) * len(shape))


def _row_call(body, name, T, in_specs, out_specs, out_shape, args, tr=ROW_TILE):
    return pl.pallas_call(body, name=name, grid=(T // tr,), in_specs=in_specs, out_specs=out_specs,
                          out_shape=out_shape, compiler_params=_params(("arbitrary",)))(*args)


def _sds(shape, dtype):
    return jax.ShapeDtypeStruct(shape, dtype)


def _rmsnorm_fwd(x, g):
    T = x.shape[0]

    def body(x_ref, g_ref, h_ref):
        xv = x_ref[...]
        r = lax.rsqrt(jnp.mean(xv * xv, axis=1, keepdims=True) + RMS_EPS)
        h_ref[...] = (xv * r * g_ref[...]).astype(BF16)

    return _row_call(body, "rmsnorm_fwd", T, [_rows(), _full((1, D_MODEL))], _rows(),
                     _sds((T, D_MODEL), BF16), (x, g))


def _post_epilogue(y, a_tile, x, g):
    del a_tile
    return y, x + y * lax.rsqrt(jnp.mean(y * y, axis=1, keepdims=True) + RMS_EPS) * g


def _ple_epilogue(gl, x1, pe):
    return gl, x1 + pe * _sigmoid(gl)


def _rmsnorm_bwd(dout, xin, g, add, out_dtype, name):
    T = xin.shape[0]

    def body(*refs):
        if add is None:
            d_ref, x_ref, g_ref, dx_ref, dg_ref = refs
        else:
            d_ref, x_ref, g_ref, add_ref, dx_ref, dg_ref = refs

        @pl.when(pl.program_id(0) == 0)
        def _():
            dg_ref[...] = jnp.zeros_like(dg_ref)

        xv = x_ref[...]
        dv = d_ref[...].astype(F32)
        r = lax.rsqrt(jnp.mean(xv * xv, axis=1, keepdims=True) + RMS_EPS)
        xh = xv * r
        dg_ref[...] += jnp.sum(dv * xh, axis=0, keepdims=True)
        dn = dv * g_ref[...]
        dx = r * (dn - xh * jnp.mean(dn * xh, axis=1, keepdims=True))
        if add is not None:
            dx = dx + add_ref[...]
        dx_ref[...] = dx.astype(out_dtype)

    in_specs = [_rows(), _rows(), _full((1, D_MODEL))]
    args = [dout, xin, g]
    if add is not None:
        in_specs.append(_rows())
        args.append(add)
    return _row_call(body, name, T, in_specs, [_rows(), _full((1, D_MODEL))],
                     [_sds((T, D_MODEL), out_dtype), _sds((1, D_MODEL), F32)], args)


def _ple_bwd(dx2, pe, gl):
    T = dx2.shape[0]

    def body(d_ref, pe_ref, gl_ref, dpe_ref, dgl_ref):
        dv = d_ref[...]
        sg = _sigmoid(gl_ref[...])
        dpe_ref[...] = (dv * sg).astype(BF16)
        dgl_ref[...] = (dv * pe_ref[...] * sg * (1.0 - sg)).astype(BF16)

    return _row_call(body, "ple_bwd", T, [_rows()] * 3, [_rows()] * 2,
                     [_sds((T, D_MODEL), BF16)] * 2, (dx2, pe, gl))


def _loss_fwd_bwd(y, target):
    T = y.shape[0]

    def body(y_ref, t_ref, s_ref, d_ref):
        @pl.when(pl.program_id(0) == 0)
        def _():
            s_ref[...] = jnp.zeros_like(s_ref)

        e = y_ref[...] - t_ref[...]
        s_ref[...] += jnp.sum(e * e).reshape(1, 1)
        d_ref[...] = e * (1.0 / D_MODEL)

    return _row_call(body, "loss", T, [_rows()] * 2, [_full((1, 1)), _rows()],
                     [_sds((1, 1), F32), _sds((T, D_MODEL), F32)], (y, target))


def _attn_combine(outs, lses, proj):
    T = proj.shape[0]

    def body(o0, o1, o2, l0, l1, l2, z_ref, a_ref, o_ref, lse_ref):
        a0, a1, a2 = l0[...], l1[...], l2[...]
        m = jnp.maximum(jnp.maximum(a0, a1), a2)
        e0, e1, e2 = jnp.exp(a0 - m), jnp.exp(a1 - m), jnp.exp(a2 - m)
        ssum = e0 + e1 + e2
        o = (e0 * o0[...] + e1 * o1[...] + e2 * o2[...]) / ssum
        zv = z_ref[...].astype(F32)
        o_ref[...] = o
        lse_ref[...] = m + jnp.log(ssum)
        a_ref[...] = (o * zv * _sigmoid(zv)).astype(BF16)

    return _row_call(body, "attn_combine", T, [_rows()] * 6 + [_rows(cb=3 * N_GROUPS)], [_rows()] * 3,
                     [_sds((T, D_MODEL), BF16), _sds((T, D_MODEL), F32), _sds((T, D_MODEL), F32)],
                     (*outs, *lses, proj))


def _gate_bwd(da, o, proj):
    T = da.shape[0]

    def body(da_ref, o_ref, z_ref, do_ref, dz_ref):
        dv = da_ref[...]
        zv = z_ref[...]
        sg = _sigmoid(zv)
        do_ref[...] = dv * zv * sg
        dz_ref[...] = dv * o_ref[...] * sg * (1.0 + zv * (1.0 - sg))

    zcols = _rows(cb=3 * N_GROUPS)
    return _row_call(body, "gate_bwd", T, [_rows(), _rows(), zcols], [_rows(), zcols],
                     [_sds((T, D_MODEL), F32), _sds((T, PROJ_COLS), F32)], (da, o, proj))


def _rope_tables(positions):
    inv_freq = 1.0 / (ROPE_THETA ** (jnp.arange(0, HEAD_DIM, 2, dtype=F32) / HEAD_DIM))
    ang = positions.astype(F32)[..., None] * inv_freq
    cos, sin = jnp.cos(ang), jnp.sin(ang)
    return jnp.tile(cos, (1, 1, 4)), jnp.concatenate([-sin, sin, -sin, sin], axis=-1)


def _rotate_half_partner(t):
    lane = lax.broadcasted_iota(jnp.int32, t.shape, 1)
    return jnp.where((lane % HEAD_DIM) < HEAD_DIM // 2,
                     pltpu.roll(t, LANES - HEAD_DIM // 2, 1), pltpu.roll(t, HEAD_DIM // 2, 1))


def _mask_bias(first):
    qi = lax.broadcasted_iota(jnp.int32, (N_BACK, 2 * N_BACK), 0)
    kj = lax.broadcasted_iota(jnp.int32, (N_BACK, 2 * N_BACK), 1)
    ok = (kj >= qi) & (kj <= qi + N_BACK)
    if first:
        ok = ok & (kj >= N_BACK)
    return jnp.where(ok, 0.0, NEG_INF).astype(F32)


def _stack_heads(t, head0):
    zero = jnp.zeros_like(t)
    return jnp.concatenate([jnp.where(head0, t, zero), jnp.where(head0, zero, t)], axis=0)


def _unstack_heads(t2, head0):
    return jnp.where(head0, t2[:N_BACK], t2[N_BACK:])


def _block_loop(nb, block):
    first, rest = _mask_bias(True), _mask_bias(False)
    first, rest = jnp.concatenate([first, first], axis=0), jnp.concatenate([rest, rest], axis=0)
    if nb <= BLOCK_UNROLL:
        for n in range(nb):
            block(n, first if n == 0 else rest)
        return

    def step(n, carry):
        block(n, jnp.where(n == 0, first, rest))
        return carry

    lax.fori_loop(0, nb, step, 0, unroll=BLOCK_UNROLL)


def _for(count, body, unroll_fully):
    if unroll_fully:
        for i in range(count):
            body(i)
    else:
        lax.fori_loop(0, count, lambda i, carry: (body(i), carry)[1], 0)


def _residues_together(nb):
    return max(1, BLOCK_UNROLL // nb)


def _residue_loop(d, nb, residue):
    together = _residues_together(nb)
    assert d % together == 0

    def group(i, carry):
        for u in range(together):
            residue(i * together + u, u)
        return carry

    lax.fori_loop(0, d // together, group, 0)


_NT = (((1,), (1,)), ((), ()))
_TN = (((0,), (0,)), ((), ()))


def _residue_rows(r, i, d):
    start = r + i * (N_BACK * d)
    if d == 1:
        return pl.ds(pl.multiple_of(start, N_BACK), N_BACK)
    return pl.ds(start, N_BACK, stride=d)


def _seq_rows(i):
    return pl.ds(pl.multiple_of(i * N_BACK, N_BACK), N_BACK)


def _rows_at(base, i, size=N_BACK):
    return pl.ds(pl.multiple_of(base + i * N_BACK, N_BACK), size)


def _attn_fwd(proj, cos, sin, group, Bl, S):
    d = WIN_DIL[group][1]
    L = S // d
    nb = L // N_BACK
    P = L + N_BACK
    assert WIN_DIL[group][0] // d == N_BACK and L % N_BACK == 0

    def body(q_ref, k_ref, v_ref, cos_ref, sin_ref, o_ref, lse_ref, qr, kr, vp):
        head0 = lax.broadcasted_iota(jnp.int32, (1, LANES), 1) < HEAD_DIM
        zeros = jnp.zeros((N_BACK, LANES), BF16)

        def residue(r, u):
            del u
            qbase, kbase = r * L, r * P
            kr[_rows_at(kbase, 0), :] = zeros
            vp[_rows_at(kbase, 0), :] = zeros

            def rope(i):
                rows = _residue_rows(r, i, d)
                cs, sn = cos_ref[rows, :], sin_ref[rows, :]
                q, k = q_ref[rows, :], k_ref[rows, :]
                qr[_rows_at(qbase, i), :] = ((q * cs + _rotate_half_partner(q) * sn)
                                            * (HEAD_DIM ** -0.5)).astype(BF16)
                kr[_rows_at(kbase, i + 1), :] = (k * cs + _rotate_half_partner(k) * sn).astype(BF16)
                vp[_rows_at(kbase, i + 1), :] = v_ref[rows, :].astype(BF16)

            _for(nb, rope, nb <= BLOCK_UNROLL)

            def block(n, bias):
                win = _rows_at(kbase, n, 2 * N_BACK)
                q2, kw, vw = _stack_heads(qr[_rows_at(qbase, n), :], head0), kr[win, :], vp[win, :]
                s = lax.dot_general(q2, kw, _NT, preferred_element_type=F32) + bias
                m = jnp.max(s, axis=1, keepdims=True)
                p = jnp.exp(s - m)
                l = jnp.sum(p, axis=1, keepdims=True)
                pv = jnp.dot(p.astype(BF16), vw, preferred_element_type=F32)
                rows = _residue_rows(r, n, d)
                o_ref[rows, :] = _unstack_heads(pv * (1.0 / l), head0)
                lse_ref[rows, :] = _unstack_heads((m + jnp.log(l)) + jnp.zeros((2 * N_BACK, LANES), F32), head0)

            _block_loop(nb, block)

        _residue_loop(d, nb, residue)

    act = pl.BlockSpec((None, S, LANES), lambda b, hp: (b, 0, hp))
    tab = pl.BlockSpec((None, S, LANES), lambda b, hp: (b, 0, 0))
    col = lambda which: pl.BlockSpec((None, S, LANES),
                                     lambda b, hp: (b, 0, (which * N_GROUPS + group) * HEAD_PAIRS + hp))
    seq = lambda rows: pl.BlockSpec((None, None, rows, LANES), lambda b, hp: (b, hp, 0, 0))
    p3 = proj.reshape(Bl, S, PROJ_COLS)
    o, lse, qr, kr, vp = pl.pallas_call(
        body, name="attn_fwd_g%d" % group, grid=(Bl, HEAD_PAIRS),
        in_specs=[col(0), col(1), col(2), tab, tab], out_specs=[act, act, seq(S), seq(d * P), seq(d * P)],
        out_shape=[_sds((Bl, S, D_MODEL), F32)] * 2 + [_sds((Bl, HEAD_PAIRS, S, LANES), BF16)]
        + [_sds((Bl, HEAD_PAIRS, d * P, LANES), BF16)] * 2,
        compiler_params=_params(("parallel", "arbitrary"), VMEM_LIMIT_ATTN),
    )(p3, p3, p3, cos, sin)
    return o.reshape(Bl * S, D_MODEL), lse.reshape(Bl * S, D_MODEL), (qr, kr, vp)


def _attn_bwd(saved, cos, sin, do, o, lse, dproj, group, Bl, S):
    d = WIN_DIL[group][1]
    L = S // d
    nb = L // N_BACK
    P = L + N_BACK
    steps = Bl * HEAD_PAIRS

    def body(qr, kr, vp, cos_ref, sin_ref, do_ref, o_ref, lse_ref, dproj_in, dproj_ref,
             dk_accs, dv_accs, stage, sems):
        del dproj_in
        head0 = lax.broadcasted_iota(jnp.int32, (1, LANES), 1) < HEAD_DIM
        b, hp = pl.program_id(0), pl.program_id(1)
        step = b * HEAD_PAIRS + hp
        slot = step % 2
        dq_s, dk_s, dv_s = stage.at[slot, 0], stage.at[slot, 1], stage.at[slot, 2]

        def copies(which_slot):
            out = []
            for which in range(3):
                col = ((which * N_GROUPS + group) * HEAD_PAIRS + hp) * LANES
                out.append(pltpu.make_async_copy(
                    stage.at[which_slot, which], dproj_ref.at[b, :, pl.ds(pl.multiple_of(col, LANES), LANES)],
                    sems.at[which_slot, which]))
            return out

        @pl.when(step >= 2)
        def _():
            for cp in copies(slot):
                cp.wait()

        def residue(r, u):
            qbase, kbase = r * L, r * P
            dk_acc, dv_acc = dk_accs.at[u], dv_accs.at[u]
            dk_acc[...] = jnp.zeros_like(dk_acc)
            dv_acc[...] = jnp.zeros_like(dv_acc)

            def block(n, bias):
                win = pl.ds(pl.multiple_of(n * N_BACK, N_BACK), 2 * N_BACK)
                kwin = _rows_at(kbase, n, 2 * N_BACK)
                rows = _residue_rows(r, n, d)
                q2, kw, vw = _stack_heads(qr[_rows_at(qbase, n), :], head0), kr[kwin, :], vp[kwin, :]
                dof = do_ref[rows, :]
                do2 = _stack_heads(dof.astype(BF16), head0)
                lse_b = lse_ref[rows, :]
                lse2 = jnp.concatenate([lse_b[:, 0:1], lse_b[:, HEAD_DIM:HEAD_DIM + 1]], axis=0)
                dsum = _stack_heads(dof * o_ref[rows, :], head0)
                delta = jnp.sum(dsum, axis=1, keepdims=True)
                s = lax.dot_general(q2, kw, _NT, preferred_element_type=F32) + bias
                p = jnp.exp(s - lse2)
                dp = lax.dot_general(do2, vw, _NT, preferred_element_type=F32)
                ds = (p * (dp - delta)).astype(BF16)
                dq = _unstack_heads(jnp.dot(ds, kw, preferred_element_type=F32), head0) * (HEAD_DIM ** -0.5)
                cs, sn = cos_ref[rows, :], sin_ref[rows, :]
                dq_s[rows, :] = dq * cs + _rotate_half_partner(dq * sn)
                dk_acc[win, :] += lax.dot_general(ds, q2, _TN, preferred_element_type=F32)
                dv_acc[win, :] += lax.dot_general(p.astype(BF16), do2, _TN, preferred_element_type=F32)

            _block_loop(nb, block)

            def finish(i):
                rows = _residue_rows(r, i, d)
                cs, sn = cos_ref[rows, :], sin_ref[rows, :]
                dk = dk_acc[_seq_rows(i + 1), :]
                dk_s[rows, :] = dk * cs + _rotate_half_partner(dk * sn)
                dv_s[rows, :] = dv_acc[_seq_rows(i + 1), :]

            _for(nb, finish, nb <= BLOCK_UNROLL)

        _residue_loop(d, nb, residue)
        for cp in copies(slot):
            cp.start()

        @pl.when(step == steps - 1)
        def _():
            if steps > 1:
                for cp in copies(1 - slot):
                    cp.wait()
            for cp in copies(slot):
                cp.wait()

    act = pl.BlockSpec((None, S, LANES), lambda b, hp: (b, 0, hp))
    tab = pl.BlockSpec((None, S, LANES), lambda b, hp: (b, 0, 0))
    seq = lambda rows: pl.BlockSpec((None, None, rows, LANES), lambda b, hp: (b, hp, 0, 0))
    view = lambda t: t.reshape(Bl, S, D_MODEL)
    out = pl.pallas_call(
        body, name="attn_bwd_g%d" % group, grid=(Bl, HEAD_PAIRS),
        in_specs=[seq(S), seq(d * P), seq(d * P), tab, tab, act, act, act, _ANY], out_specs=_ANY,
        out_shape=_sds((Bl, S, PROJ_COLS), F32), input_output_aliases={8: 0},
        scratch_shapes=[pltpu.VMEM((_residues_together(nb), P, LANES), F32),
                        pltpu.VMEM((_residues_together(nb), P, LANES), F32),
                        pltpu.VMEM((2, 3, S, LANES), F32), pltpu.SemaphoreType.DMA((2, 3))],
        compiler_params=_params(("arbitrary", "arbitrary"), VMEM_LIMIT_ATTN),
    )(*saved, cos, sin, view(do), view(o), view(lse), dproj.reshape(Bl, S, PROJ_COLS))
    return out.reshape(Bl * S, PROJ_COLS)


CONV_TILE = 256
CONV_CHUNK = 64
SUBLANES = 8
CONV_SHIFT_ROWS = CONV_TILE + CONV_HALO - SUBLANES


def _fill_shifted(shifted, ext, cs):
    for k in range(1, SUBLANES):
        shifted[k - 1] = ext[pl.ds(k, CONV_SHIFT_ROWS), cs]


def _shifted_rows(shifted, ext, cs, off):
    k = off % SUBLANES
    if k == 0:
        return ext[pl.ds(off, CONV_CHUNK), cs]
    return shifted[k - 1, pl.ds(off - k, CONV_CHUNK), :]


def _conv_fwd(proj, z, dw, dwb, ln_g, ln_b, Bl, S):
    tr = CONV_TILE
    nj = S // tr
    hb = tr // CONV_HALO

    def body(a_ref, b_ref, ah_ref, bh_ref, z_ref, dw_ref, dwb_ref, g_ref, bb_ref, u1_ref, out_ref, ext, shifted):
        j = pl.program_id(1)
        halo = ah_ref[0].astype(F32) * _sigmoid(bh_ref[0].astype(F32))
        ext[pl.ds(0, CONV_HALO), :] = jnp.where(j > 0, halo, 0.0)
        ext[pl.ds(CONV_HALO, tr), :] = a_ref[0].astype(F32) * _sigmoid(b_ref[0].astype(F32))

        def cols(c, carry):
            cs = pl.ds(pl.multiple_of(c * LANES, LANES), LANES)
            _fill_shifted(shifted, ext, cs)
            for rc in range(tr // CONV_CHUNK):
                acc = jnp.zeros((CONV_CHUNK, LANES), F32)
                for w in range(CONV_WIDTH):
                    off = rc * CONV_CHUNK + CONV_HALO - (CONV_WIDTH - 1) + w
                    acc = acc + dw_ref[pl.ds(w, 1), cs] * _shifted_rows(shifted, ext, cs, off)
                u1_ref[0, pl.ds(rc * CONV_CHUNK, CONV_CHUNK), cs] = acc + dwb_ref[:, cs]
            return carry

        lax.fori_loop(0, D_MODEL // LANES, cols, 0)
        u1 = u1_ref[0]
        mu = jnp.mean(u1, axis=1, keepdims=True)
        xc = u1 - mu
        rstd = lax.rsqrt(jnp.mean(xc * xc, axis=1, keepdims=True) + LN_EPS)
        u2 = xc * rstd * g_ref[...] + bb_ref[...]
        zv = z_ref[0].astype(F32)
        out_ref[0] = (u2 * _sigmoid(u2) * zv * _sigmoid(zv)).astype(BF16)

    tile = lambda cb: pl.BlockSpec((1, tr, D_MODEL), lambda b, j: (b, j, cb))
    halo = lambda cb: pl.BlockSpec((1, CONV_HALO, D_MODEL), lambda b, j: (b, jnp.maximum(j * hb - 1, 0), cb))
    par = lambda r: pl.BlockSpec((r, D_MODEL), lambda b, j: (0, 0))
    p3 = proj.reshape(Bl, S, 2 * D_MODEL)
    u1, out = pl.pallas_call(
        body, name="conv_fwd", grid=(Bl, nj),
        in_specs=[tile(0), tile(1), halo(0), halo(1), tile(0), par(32), par(1), par(1), par(1)],
        out_specs=[tile(0), tile(0)],
        out_shape=[_sds((Bl, S, D_MODEL), F32), _sds((Bl, S, D_MODEL), BF16)],
        scratch_shapes=[pltpu.VMEM((tr + CONV_HALO, D_MODEL), F32),
                        pltpu.VMEM((SUBLANES - 1, CONV_SHIFT_ROWS, LANES), F32)],
        compiler_params=_params(("parallel", "arbitrary")),
    )(p3, p3, p3, p3, z.reshape(Bl, S, D_MODEL), dw, dwb, ln_g, ln_b)
    return u1.reshape(Bl * S, D_MODEL), out.reshape(Bl * S, D_MODEL)


def _conv_norm_bwd(da2, z, u1, ln_g, ln_b):
    T = da2.shape[0]

    def body(da_ref, z_ref, u_ref, g_ref, b_ref, du_ref, dz_ref, dg_ref, db_ref):
        @pl.when(pl.program_id(0) == 0)
        def _():
            dg_ref[...] = jnp.zeros_like(dg_ref)
            db_ref[...] = jnp.zeros_like(db_ref)

        u1 = u_ref[...]
        mu = jnp.mean(u1, axis=1, keepdims=True)
        xc = u1 - mu
        rstd = lax.rsqrt(jnp.mean(xc * xc, axis=1, keepdims=True) + LN_EPS)
        nrm = xc * rstd
        u2 = nrm * g_ref[...] + b_ref[...]
        s2 = _sigmoid(u2)
        zv = z_ref[...].astype(F32)
        sz = _sigmoid(zv)
        dv = da_ref[...]
        dz_ref[...] = (dv * u2 * s2 * sz * (1.0 + zv * (1.0 - sz))).astype(BF16)
        du2 = dv * zv * sz * s2 * (1.0 + u2 * (1.0 - s2))
        dg_ref[...] += jnp.sum(du2 * nrm, axis=0, keepdims=True)
        db_ref[...] += jnp.sum(du2, axis=0, keepdims=True)
        dn = du2 * g_ref[...]
        du_ref[...] = rstd * (dn - jnp.mean(dn, axis=1, keepdims=True)
                              - nrm * jnp.mean(dn * nrm, axis=1, keepdims=True))

    return _row_call(body, "conv_norm_bwd", T,
                     [_rows(), _rows(), _rows(), _full((1, D_MODEL)), _full((1, D_MODEL))],
                     [_rows(), _rows(), _full((1, D_MODEL)), _full((1, D_MODEL))],
                     [_sds((T, D_MODEL), F32), _sds((T, D_MODEL), BF16), _sds((1, D_MODEL), F32),
                      _sds((1, D_MODEL), F32)], (da2, z, u1, ln_g, ln_b))


def _conv_bwd(proj, du1, dw, Bl, S):
    tr = CONV_TILE
    nj = S // tr
    hb = tr // CONV_HALO

    def body(a_ref, b_ref, ah_ref, bh_ref, du_ref, duh_ref, dw_ref, dab_ref, ddw_ref, ddb_ref, uext, dext, du0,
             ushift, dshift, ddw8):
        first = (pl.program_id(0) == 0) & (pl.program_id(1) == 0)
        last = (pl.program_id(0) == Bl - 1) & (pl.program_id(1) == nj - 1)
        j = pl.program_id(1)

        @pl.when(first)
        def _():
            ddw8[...] = jnp.zeros_like(ddw8)
            ddb_ref[...] = jnp.zeros_like(ddb_ref)

        halo = ah_ref[0].astype(F32) * _sigmoid(bh_ref[0].astype(F32))
        uext[pl.ds(0, CONV_HALO), :] = jnp.where(j > 0, halo, 0.0)
        av = a_ref[0].astype(F32)
        sb = _sigmoid(b_ref[0].astype(F32))
        uext[pl.ds(CONV_HALO, tr), :] = av * sb
        dext[pl.ds(0, tr), :] = du_ref[0]
        dext[pl.ds(tr, CONV_HALO), :] = jnp.where(j < nj - 1, duh_ref[0], 0.0)
        ddb_ref[...] += jnp.sum(du_ref[0], axis=0, keepdims=True)

        def cols(c, carry):
            cs = pl.ds(pl.multiple_of(c * LANES, LANES), LANES)
            _fill_shifted(dshift, dext, cs)
            _fill_shifted(ushift, uext, cs)
            for rc in range(tr // CONV_CHUNK):
                base = rc * CONV_CHUNK
                acc = jnp.zeros((CONV_CHUNK, LANES), F32)
                for w in range(CONV_WIDTH):
                    acc = acc + dw_ref[pl.ds(w, 1), cs] * _shifted_rows(dshift, dext, cs, base + CONV_WIDTH - 1 - w)
                du0[pl.ds(base, CONV_CHUNK), cs] = acc
            for w in range(CONV_WIDTH):
                part = jnp.zeros((SUBLANES, LANES), F32)
                for rc in range(tr // CONV_CHUNK):
                    base = rc * CONV_CHUNK
                    prod = dext[pl.ds(base, CONV_CHUNK), cs] * _shifted_rows(
                        ushift, uext, cs, base + CONV_HALO - (CONV_WIDTH - 1) + w)
                    for i in range(CONV_CHUNK // SUBLANES):
                        part = part + prod[i * SUBLANES:(i + 1) * SUBLANES]
                ddw8[pl.ds(w * SUBLANES, SUBLANES), cs] += part
            return carry

        lax.fori_loop(0, D_MODEL // LANES, cols, 0)
        g = du0[...]
        dab_ref[0, :, 0:D_MODEL] = (g * sb).astype(BF16)
        dab_ref[0, :, D_MODEL:2 * D_MODEL] = (g * av * sb * (1.0 - sb)).astype(BF16)

        @pl.when(last)
        def _():
            for w in range(CONV_WIDTH + 1):
                ddw_ref[pl.ds(w, 1), :] = jnp.sum(ddw8[pl.ds(w * SUBLANES, SUBLANES), :], axis=0, keepdims=True)

    tile = lambda cb: pl.BlockSpec((1, tr, D_MODEL), lambda b, j: (b, j, cb))
    halo = lambda cb: pl.BlockSpec((1, CONV_HALO, D_MODEL), lambda b, j: (b, jnp.maximum(j * hb - 1, 0), cb))
    nxt = pl.BlockSpec((1, CONV_HALO, D_MODEL), lambda b, j: (b, jnp.minimum((j + 1) * hb, S // CONV_HALO - 1), 0))
    par = lambda r: pl.BlockSpec((r, D_MODEL), lambda b, j: (0, 0))
    p3 = proj.reshape(Bl, S, 2 * D_MODEL)
    d3 = du1.reshape(Bl, S, D_MODEL)
    dab, ddw, ddb = pl.pallas_call(
        body, name="conv_bwd", grid=(Bl, nj),
        in_specs=[tile(0), tile(1), halo(0), halo(1), tile(0), nxt, par(32)],
        out_specs=[pl.BlockSpec((1, tr, 2 * D_MODEL), lambda b, j: (b, j, 0)), par(32), par(1)],
        out_shape=[_sds((Bl, S, 2 * D_MODEL), BF16), _sds((32, D_MODEL), F32), _sds((1, D_MODEL), F32)],
        scratch_shapes=[pltpu.VMEM((tr + CONV_HALO, D_MODEL), F32), pltpu.VMEM((tr + CONV_HALO, D_MODEL), F32),
                        pltpu.VMEM((tr, D_MODEL), F32),
                        pltpu.VMEM((SUBLANES - 1, CONV_SHIFT_ROWS, LANES), F32),
                        pltpu.VMEM((SUBLANES - 1, CONV_SHIFT_ROWS, LANES), F32),
                        pltpu.VMEM(((CONV_WIDTH + 1) * SUBLANES, D_MODEL), F32)],
        compiler_params=_params(("arbitrary", "arbitrary")),
    )(p3, p3, p3, p3, d3, d3, dw)
    return dab.reshape(Bl * S, 2 * D_MODEL), ddw, ddb


_LAYOUT = (
    ("pre_norm_g", (4, 1024), None), ("post_norm_g", (4, 1024), None),
    ("attn_w_in", (2, 1024, 2560), 2), ("attn_w_out", (2, 256, 1024), 1),
    ("conv_w_in", (2, 1024, 768), 2), ("conv_dw_w", (2, 31, 256), 2),
    ("conv_dw_b", (2, 256), 1), ("conv_ln_g", (2, 256), 1), ("conv_ln_b", (2, 256), 1),
    ("conv_w_out", (2, 256, 1024), 1), ("ple_w_proj", (4, 256, 256), 2), ("ple_w_gate", (4, 256, 1024), 1),
)
_MATMUL_WEIGHTS = ("attn_w_in", "attn_w_out", "conv_w_in", "conv_w_out", "ple_w_proj", "ple_w_gate")
_AXIS = {n: a for n, _, a in _LAYOUT}


def _size(shape):
    n = 1
    for s in shape:
        n *= s
    return n


def _padded_rows(shape):
    rows = _size(shape) // shape[-1]
    return rows + (-rows) % FLAT_ROW_ALIGN


def _rows2d(a):
    a2 = a.reshape(-1, a.shape[-1])
    pad = _padded_rows(a.shape) - a2.shape[0]
    return jnp.pad(a2, ((0, pad), (0, 0))) if pad else a2


def _col_blocks(a):
    a2 = _rows2d(a)
    return jnp.concatenate([a2[:, c:c + FLAT_COLS] for c in range(0, a2.shape[1], FLAT_COLS)], axis=0)


def _from_col_blocks(flat, off, shape):
    rows, nblk = _padded_rows(shape), shape[-1] // FLAT_COLS
    a2 = jnp.concatenate([flat[off + b * rows:off + (b + 1) * rows] for b in range(nblk)], axis=1)
    return a2[:_size(shape) // shape[-1]].reshape(shape), off + nblk * rows


def _shard_col_blocks(full, shape, axis):
    if axis is None:
        blocks = _col_blocks(full)
        return jnp.broadcast_to(blocks[None], (N_CHIPS,) + blocks.shape)
    m = shape[-1]
    if axis == len(shape) - 1:
        a2 = _rows2d(full)
        pieces = [a2[:, c:c + FLAT_COLS] for c in range(0, N_CHIPS * m, FLAT_COLS)]
    else:
        layers, r, _ = shape
        assert axis == 1 and (layers * r) % FLAT_ROW_ALIGN == 0
        pieces = [full[:, s * r:(s + 1) * r, c:c + FLAT_COLS].reshape(layers * r, FLAT_COLS)
                  for s in range(N_CHIPS) for c in range(0, m, FLAT_COLS)]
    return jnp.concatenate(pieces, axis=0).reshape(N_CHIPS, -1, FLAT_COLS)


_FLAT_BIG = ("attn_w_in", "conv_w_in", "ple_w_gate", "attn_w_out", "conv_w_out", "ple_w_proj")
_FLAT_SMALL = ("pre_norm_g", "post_norm_g", "conv_dw_w", "conv_dw_b", "conv_ln_g", "conv_ln_b")
_SHARD_SHAPE = {n: s for n, s, _ in _LAYOUT}
PACK_TILE = 1024


def _flat_offsets():
    out, off = {}, 0
    for n in _FLAT_BIG + _FLAT_SMALL:
        out[n] = off
        off += _padded_rows(_SHARD_SHAPE[n]) * (_SHARD_SHAPE[n][-1] // FLAT_COLS)
    return out, off


def _unpack_f32(flat):
    offsets, _ = _flat_offsets()
    return {n: _from_col_blocks(flat, offsets[n], _SHARD_SHAPE[n])[0] for n in _FLAT_BIG + _FLAT_SMALL}


def _pack_param(full, shape, axis, off, flat):
    layers, r, m = shape
    nblk = m // FLAT_COLS
    if axis == 2:
        rows = layers * r
        tr = min(PACK_TILE, rows)
        assert rows % tr == 0 and off % tr == 0
        src = full.reshape(rows, N_CHIPS * m)
        grid = (N_CHIPS * nblk, rows // tr)
        in_spec = pl.BlockSpec((tr, FLAT_COLS), lambda j, i: (i, j))
        out_spec = pl.BlockSpec((None, tr, FLAT_COLS), lambda j, i: (j // nblk, (off + (j % nblk) * rows) // tr + i, 0))
    else:
        assert axis == 1 and off % r == 0
        src = full.reshape(layers * N_CHIPS * r, m)
        grid = (layers, N_CHIPS, nblk)
        in_spec = pl.BlockSpec((r, FLAT_COLS), lambda l, s, b: (l * N_CHIPS + s, b))
        out_spec = pl.BlockSpec((None, r, FLAT_COLS), lambda l, s, b: (s, (off + b * layers * r) // r + l, 0))

    def copy_body(src_ref, *rest):
        rest[-1][...] = src_ref[...]

    args, in_specs, aliases = [src], [in_spec], {}
    if flat is not None:
        args.append(flat)
        in_specs.append(_ANY)
        aliases = {1: 0}
    return pl.pallas_call(
        copy_body, name="pack_grad", grid=grid, in_specs=in_specs, out_specs=out_spec,
        out_shape=_sds((N_CHIPS, FLAT_ROWS, FLAT_COLS), F32), input_output_aliases=aliases,
        compiler_params=_params(("arbitrary",) * len(grid)))(*args)


SMALL_ROWS = 40


def _stack_small(w):
    rows = [w["conv_dw_w"]] + [w[n][:, None, :] for n in ("conv_dw_b", "conv_ln_g", "conv_ln_b")]
    stacked = jnp.concatenate(rows, axis=1)
    return jnp.pad(stacked, ((0, 0), (0, SMALL_ROWS - stacked.shape[1]), (0, 0)))


def _unstack_small(small):
    return {"conv_dw_w": small[:, :CONV_WIDTH], "conv_dw_b": small[:, CONV_WIDTH],
            "conv_ln_g": small[:, CONV_WIDTH + 1], "conv_ln_b": small[:, CONV_WIDTH + 2]}


def _pack_full_grads(grads):
    offsets, end = _flat_offsets()
    assert end <= FLAT_ROWS
    flat = None
    for n in _FLAT_BIG:
        flat = _pack_param(grads[n], _SHARD_SHAPE[n], _AXIS[n], offsets[n], flat)
    small = jnp.concatenate([_shard_col_blocks(grads[n], _SHARD_SHAPE[n], _AXIS[n]) for n in _FLAT_SMALL], axis=1)
    start = offsets[_FLAT_SMALL[0]]
    small = jnp.pad(small, ((0, 0), (0, FLAT_ROWS - start - small.shape[1]), (0, 0)))
    return lax.dynamic_update_slice(flat, small, (0, start, 0))


_ANY = pl.BlockSpec(memory_space=pl.ANY)


def _mesh_pos():
    return lax.axis_index("x"), lax.axis_index("y"), lax.axis_index("c")


def _other_chips(x, y):
    return [(1 - x, y), (x, 1 - y), (1 - x, 1 - y)]


COPIES_PER_ARRAY = 7


def _allgather_weights(shards, axes, small):
    n = len(shards)
    full_shape = lambda a, axis: tuple(d * (N_CHIPS if i == axis else 1) for i, d in enumerate(a.shape))

    def body(*refs):
        ins, small_in = refs[:n], refs[n]
        outs, small_out = refs[n + 1:2 * n + 1], refs[2 * n + 1]
        send_sems, recv_sems = refs[2 * n + 2:]
        x, y, c = _mesh_pos()
        mine, me, sibling = 2 * x + y, (x, y, c), (x, y, 1 - c)
        chips = _other_chips(x, y)

        def region(a, chip, half):
            _, rows, cols = shards[a].shape
            h = rows // 2
            if axes[a] == 2:
                return outs[a].at[:, slice(None) if half is None else pl.ds(half * h, h), pl.ds(chip * cols, cols)]
            if half is None:
                return outs[a].at[:, pl.ds(chip * rows, rows), :]
            return outs[a].at[:, pl.ds(chip * rows + half * h, h), :]

        def copy(k, src, dst, to):
            return pltpu.make_async_remote_copy(src_ref=src, dst_ref=dst, send_sem=send_sems.at[k],
                                                recv_sem=recv_sems.at[k], device_id=to, device_id_type=MESH)

        def arrival(k, dst):
            return copy(k, dst, dst, me)

        sends = []
        for a in range(n):
            h = shards[a].shape[1] // 2
            base = a * COPIES_PER_ARRAY
            sends.append(copy(base + 6, ins[a], region(a, mine, None), sibling))
            for j, (cx, cy) in enumerate(chips):
                sends.append(copy(base + j, ins[a].at[:, pl.ds(c * h, h), :], region(a, mine, c), (cx, cy, c)))
        small_cols = small.shape[2]
        small_region = lambda chip: small_out.at[:, :, pl.ds(chip * small_cols, small_cols)]
        base = n * COPIES_PER_ARRAY
        sends.append(copy(base + 3, small_in, small_region(mine), sibling))
        for j, (cx, cy) in enumerate(chips):
            sends.append(copy(base + j, small_in, small_region(mine), (cx, cy, c)))
        for cp in sends:
            cp.start()
        for j, (cx, cy) in enumerate(chips):
            for a in range(n):
                k = a * COPIES_PER_ARRAY + j
                arrival(k, region(a, 2 * cx + cy, c)).wait_recv()
                passed = copy(k + 3, region(a, 2 * cx + cy, c), region(a, 2 * cx + cy, c), sibling)
                passed.start()
                sends.append(passed)
        for j, (cx, cy) in enumerate(chips):
            for a in range(n):
                arrival(a * COPIES_PER_ARRAY + 3 + j, region(a, 2 * cx + cy, 1 - c)).wait_recv()
            arrival(base + j, small_region(2 * cx + cy)).wait_recv()
        for a in range(n):
            arrival(a * COPIES_PER_ARRAY + 6, region(a, mine, None)).wait_recv()
        arrival(base + 3, small_region(mine)).wait_recv()
        for cp in sends:
            cp.wait_send()

    n_sems = n * COPIES_PER_ARRAY + 4
    out = pl.pallas_call(
        body, name="allgather_weights", in_specs=[_ANY] * (n + 1), out_specs=[_ANY] * (n + 1),
        out_shape=[_sds(full_shape(a, axis), a.dtype) for a, axis in zip(shards, axes)]
        + [_sds(full_shape(small, 2), small.dtype)],
        scratch_shapes=[pltpu.SemaphoreType.DMA((n_sems,)), pltpu.SemaphoreType.DMA((n_sems,))],
    )(*shards, small)
    return out[:n], out[n]


def _exchange_core_halves(g):
    n, _, H, C = g.shape

    def body(g_ref, got_ref, send_sem, recv_sem):
        x, y, c = _mesh_pos()
        swap = pltpu.make_async_remote_copy(
            src_ref=g_ref.at[pl.ds(0, n), 1 - c], dst_ref=got_ref, send_sem=send_sem, recv_sem=recv_sem,
            device_id=(x, y, 1 - c), device_id_type=MESH)
        swap.start()
        swap.wait()

    return pl.pallas_call(
        body, name="exchange_core_halves", in_specs=[_ANY], out_specs=_ANY,
        out_shape=_sds((n, H, C), g.dtype),
        scratch_shapes=[pltpu.SemaphoreType.DMA, pltpu.SemaphoreType.DMA],
    )(g)


def _scatter_to_chips(p):
    n, H, C = p.shape

    def body(p_ref, q_ref, send_sems, recv_sems):
        x, y, c = _mesh_pos()
        chips = _other_chips(x, y)
        sends = [pltpu.make_async_remote_copy(
            src_ref=p_ref.at[2 * cx + cy], dst_ref=q_ref.at[j], send_sem=send_sems.at[j],
            recv_sem=recv_sems.at[j], device_id=(cx, cy, c), device_id_type=MESH)
            for j, (cx, cy) in enumerate(chips)]
        for cp in sends:
            cp.start()
        for cp in sends:
            cp.wait_recv()
        for cp in sends:
            cp.wait_send()

    return pl.pallas_call(
        body, name="scatter_to_chips", in_specs=[_ANY], out_specs=_ANY,
        out_shape=_sds((n - 1, H, C), p.dtype),
        scratch_shapes=[pltpu.SemaphoreType.DMA((3,)), pltpu.SemaphoreType.DMA((3,))],
    )(p)


def _share_core_halves(r2):
    _, H, C = r2.shape

    def body(r_ref, out_ref, send_sem, recv_sem):
        x, y, c = _mesh_pos()
        send = pltpu.make_async_remote_copy(
            src_ref=r_ref.at[c], dst_ref=out_ref.at[c], send_sem=send_sem, recv_sem=recv_sem,
            device_id=(x, y, 1 - c), device_id_type=MESH)
        send.start()
        send.wait_send()
        pltpu.make_async_remote_copy(
            src_ref=r_ref.at[c], dst_ref=out_ref.at[1 - c], send_sem=send_sem, recv_sem=recv_sem,
            device_id=(x, y, 1 - c), device_id_type=MESH).wait_recv()

    return pl.pallas_call(
        body, name="share_core_halves", in_specs=[_ANY], out_specs=_ANY,
        out_shape=_sds(r2.shape, r2.dtype), input_output_aliases={0: 0},
        scratch_shapes=[pltpu.SemaphoreType.DMA, pltpu.SemaphoreType.DMA],
    )(r2)


def _place():
    x, y, c = _mesh_pos()
    return jnp.stack([c, 2 * x + y]).astype(jnp.int32)


def _sum_pair(g, got, place):
    n, _, H, C = g.shape

    def body(place_ref, a_ref, b_ref, o_ref):
        o_ref[...] = (a_ref[...] + b_ref[...]).astype(BF16)

    spec = pl.BlockSpec((1, FLAT_TILE, C), lambda s, i, pr: (s, i, 0))
    return pl.pallas_call(
        body, name="sum_core_pair",
        grid_spec=pltpu.PrefetchScalarGridSpec(
            num_scalar_prefetch=1, grid=(n, H // FLAT_TILE),
            in_specs=[pl.BlockSpec((1, None, FLAT_TILE, C), lambda s, i, pr: (s, pr[0], i, 0)), spec],
            out_specs=spec),
        out_shape=_sds((n, H, C), BF16),
        compiler_params=_params(("parallel", "parallel")))(place, g, got)


def _sum_chips(p, q, place):
    n, H, C = p.shape

    def body(place_ref, own_ref, qx_ref, qy_ref, qxy_ref, o_ref):
        mine = place_ref[1]
        own, qx, qy, qxy = (t[0].astype(F32) for t in (own_ref, qx_ref, qy_ref, qxy_ref))

        def term(s):
            rel = jnp.full(own.shape, mine ^ s, jnp.int32)
            return jnp.where(rel == 0, own, jnp.where(rel == 2, qx, jnp.where(rel == 1, qy, qxy)))

        o_ref[0] = ((term(0) + term(1)) + term(2)) + term(3)

    qspec = lambda j: pl.BlockSpec((1, FLAT_TILE, C), lambda i, pr: (j, i, 0))
    return pl.pallas_call(
        body, name="sum_chips",
        grid_spec=pltpu.PrefetchScalarGridSpec(
            num_scalar_prefetch=1, grid=(H // FLAT_TILE,),
            in_specs=[pl.BlockSpec((1, FLAT_TILE, C), lambda i, pr: (pr[1], i, 0)), qspec(0), qspec(1), qspec(2)],
            out_specs=pl.BlockSpec((1, FLAT_TILE, C), lambda i, pr: (pr[0], i, 0))),
        out_shape=_sds((2, H, C), F32),
        compiler_params=_params(("parallel",)))(place, p, q, q, q)


ADAMW_BLOCK = 1 << 18


def _adamw(w, g, m, v):
    shape = w.shape
    C = shape[-1]
    R = _size(shape) // C
    tr = R
    while tr * C > ADAMW_BLOCK and tr % 16 == 0:
        tr //= 2
    w, g, m, v = (t.reshape(R, C) for t in (w, g, m, v))

    def body(w_ref, g_ref, m_ref, v_ref, d_ref, nm_ref, nv_ref):
        gv = g_ref[...]
        nm = ADAM_B1 * m_ref[...] + (1.0 - ADAM_B1) * gv
        nv = ADAM_B2 * v_ref[...] + (1.0 - ADAM_B2) * (gv * gv)
        m_hat = nm / (1.0 - ADAM_B1 ** ADAM_STEP)
        v_hat = nv / (1.0 - ADAM_B2 ** ADAM_STEP)
        d_ref[...] = -ADAM_LR * (m_hat / (jnp.sqrt(v_hat) + ADAM_EPS) + ADAM_WD * w_ref[...])
        nm_ref[...] = nm
        nv_ref[...] = nv

    spec = pl.BlockSpec((tr, C), lambda i: (i, 0))
    outs = pl.pallas_call(body, name="adamw", grid=(R // tr,), in_specs=[spec] * 4, out_specs=[spec] * 3,
                          out_shape=[_sds((R, C), F32)] * 3, compiler_params=_params(("parallel",)))(w, g, m, v)
    return tuple(t.reshape(shape) for t in outs)


def _reduce_scatter_grads(gfull):
    n, R, C = gfull.shape
    place = _place()
    g4 = gfull.reshape(n, 2, R // 2, C)
    p = _sum_pair(g4, _exchange_core_halves(g4), place)
    return _share_core_halves(_sum_chips(p, _scatter_to_chips(p), place)).reshape(R, C)


def _local_step(x, p, positions, loss_target, pre_g, post_g, w):
    Bl, S, _ = x.shape
    T = Bl * S
    cos, sin = _rope_tables(positions)
    xs = x.reshape(T, D_MODEL)
    saved = []
    for i in range(DEPTH):
        j = i // 2
        g_pre, g_post = pre_g[i:i + 1], post_g[i:i + 1]
        h = _rmsnorm_fwd(xs, g_pre)
        st = {"x": xs, "h": h}
        if i % 2 == 0:
            proj = _mm(h, w["attn_w_in"][j], name="attn_in")
            res = [_attn_fwd(proj, cos, sin, g, Bl, S) for g in range(N_GROUPS)]
            a, o, lse = _attn_combine([r[0] for r in res], [r[1] for r in res], proj)
            w_out = w["attn_w_out"][j]
            st.update(proj=proj, a=a, o=o, lse=lse, qkv=[r[2] for r in res])
        else:
            w_ab, w_z = w["conv_w_in"][j][:, :2 * D_MODEL], w["conv_w_in"][j][:, 2 * D_MODEL:]
            ab = _mm(h, w_ab, out_dtype=BF16, name="conv_in_ab")
            z = _mm(h, w_z, out_dtype=BF16, name="conv_in_z")
            dw = jnp.pad(w["conv_dw_w"][j], ((0, 1), (0, 0)))
            u1, a = _conv_fwd(ab, z, dw, w["conv_dw_b"][j:j + 1], w["conv_ln_g"][j:j + 1],
                              w["conv_ln_b"][j:j + 1], Bl, S)
            w_out = w["conv_w_out"][j]
            st.update(w_ab=w_ab, w_z=w_z, ab=ab, z=z, dw=dw, u1=u1, a=a)
        y, x1 = _mm_rows(a, w_out, [xs, g_post], _post_epilogue, (F32, F32), "branch_out_post")
        pi = p[i].reshape(T, PLE_DIM)
        pe = _mm(pi, w["ple_w_proj"][i], name="ple_proj")
        gl, xs = _mm_rows(x1, w["ple_w_gate"][i], [pe], _ple_epilogue, (F32, F32), "ple_gate_fwd")
        st.update(y=y, x1=x1, pi=pi, pe=pe, gl=gl)
        saved.append(st)

    sq, dx = _loss_fwd_bwd(xs, loss_target.reshape(T, D_MODEL))

    grads = {n: [None] * shape[0] for n, shape, _ in _LAYOUT}
    for i in reversed(range(DEPTH)):
        j = i // 2
        st = saved[i]
        g_pre, g_post = pre_g[i:i + 1], post_g[i:i + 1]
        dpe, dgl = _ple_bwd(dx, st["pe"], st["gl"])
        grads["ple_w_proj"][i] = _mm(st["pi"], dpe, ta=True, name="ple_proj_wgrad")
        grads["ple_w_gate"][i] = _mm(st["x1"], dgl, ta=True, name="ple_gate_wgrad")
        dx1 = _mm(dgl, w["ple_w_gate"][i], tb=True, add=dx, name="ple_gate_dgrad")
        dy, dg_post = _rmsnorm_bwd(dx1, st["y"], g_post, None, BF16, "post_bwd")
        grads["post_norm_g"][i] = dg_post[0]
        if i % 2 == 0:
            grads["attn_w_out"][j] = _mm(st["a"], dy, ta=True, name="attn_out_wgrad")
            da = _mm(dy, w["attn_w_out"][j], tb=True, name="attn_out_dgrad")
            do, dproj = _gate_bwd(da, st["o"], st["proj"])
            for g in range(N_GROUPS):
                dproj = _attn_bwd(st["qkv"][g], cos, sin, do, st["o"], st["lse"], dproj, g, Bl, S)
            dh = _mm(dproj, w["attn_w_in"][j], tb=True, tk=2048, name="attn_in_dgrad")
            grads["attn_w_in"][j] = _mm(st["h"], dproj, ta=True, name="attn_in_wgrad")
        else:
            grads["conv_w_out"][j] = _mm(st["a"], dy, ta=True, name="conv_out_wgrad")
            da2 = _mm(dy, w["conv_w_out"][j], tb=True, name="conv_out_dgrad")
            du1, dz, dln_g, dln_b = _conv_norm_bwd(da2, st["z"], st["u1"], w["conv_ln_g"][j:j + 1],
                                                   w["conv_ln_b"][j:j + 1])
            dab, ddw, ddb = _conv_bwd(st["ab"], du1, st["dw"], Bl, S)
            dh = _mm(dz, st["w_z"], tb=True, name="conv_in_z_dgrad")
            dh = _mm(dab, st["w_ab"], tb=True, add=dh, name="conv_in_ab_dgrad")
            dw_ab = _mm(st["h"], dab, ta=True, name="conv_in_ab_wgrad")
            dw_z = _mm(st["h"], dz, ta=True, name="conv_in_z_wgrad")
            grads["conv_w_in"][j] = jnp.concatenate([dw_ab, dw_z], axis=1)
            grads["conv_dw_w"][j] = ddw[:CONV_WIDTH]
            grads["conv_dw_b"][j] = ddb[0]
            grads["conv_ln_g"][j] = dln_g[0]
            grads["conv_ln_b"][j] = dln_b[0]
        dx, dg_pre = _rmsnorm_bwd(dh, st["x"], g_pre, dx1, F32, "pre_bwd")
        grads["pre_norm_g"][i] = dg_pre[0]
    grads = {n: jnp.stack(v) for n, v in grads.items()}
    return sq, dx.reshape(Bl, S, D_MODEL), grads


_NAMES = tuple(n for n, _, _ in _LAYOUT)


def kernel(x, p, positions, pre_norm_g, post_norm_g, attn_w_in, attn_w_out, conv_w_in, conv_dw_w, conv_dw_b, conv_ln_g, conv_ln_b, conv_w_out, ple_w_proj, ple_w_gate, loss_target, m_pre_norm_g, m_post_norm_g, m_attn_w_in, m_attn_w_out, m_conv_w_in, m_conv_dw_w, m_conv_dw_b, m_conv_ln_g, m_conv_ln_b, m_conv_w_out, m_ple_w_proj, m_ple_w_gate, v_pre_norm_g, v_post_norm_g, v_attn_w_in, v_attn_w_out, v_conv_w_in, v_conv_dw_w, v_conv_dw_b, v_conv_ln_g, v_conv_ln_b, v_conv_w_out, v_ple_w_proj, v_ple_w_gate):
    w_loc = dict(zip(_NAMES, (pre_norm_g, post_norm_g, attn_w_in, attn_w_out, conv_w_in, conv_dw_w, conv_dw_b,
                              conv_ln_g, conv_ln_b, conv_w_out, ple_w_proj, ple_w_gate)))
    m_loc = dict(zip(_NAMES, (m_pre_norm_g, m_post_norm_g, m_attn_w_in, m_attn_w_out, m_conv_w_in, m_conv_dw_w,
                              m_conv_dw_b, m_conv_ln_g, m_conv_ln_b, m_conv_w_out, m_ple_w_proj, m_ple_w_gate)))
    v_loc = dict(zip(_NAMES, (v_pre_norm_g, v_post_norm_g, v_attn_w_in, v_attn_w_out, v_conv_w_in, v_conv_dw_w,
                              v_conv_dw_b, v_conv_ln_g, v_conv_ln_b, v_conv_w_out, v_ple_w_proj, v_ple_w_gate)))

    gathered, small = _allgather_weights([w_loc[n].astype(BF16) for n in _MATMUL_WEIGHTS],
                                         [_AXIS[n] for n in _MATMUL_WEIGHTS], _stack_small(w_loc))
    w_full = dict(zip(_MATMUL_WEIGHTS, gathered), **_unstack_small(small))
    sq, grad_x, grads = _local_step(x, p, positions, loss_target, pre_norm_g, post_norm_g, w_full)
    loss = lax.psum(sq[0, 0] * (0.5 / D_MODEL), ("x", "y", "c"))

    g_flat = _reduce_scatter_grads(_pack_full_grads(grads))
    g_out = _unpack_f32(g_flat)
    updates = {n: _adamw(w_loc[n], g_out[n], m_loc[n], v_loc[n]) for n in _NAMES}
    d_out, m_out, v_out = ({n: updates[n][k] for n in _NAMES} for k in range(3))
    return (loss, grad_x, *[g_out[n] for n in _NAMES], *[d_out[n] for n in _NAMES],
            *[m_out[n] for n in _NAMES], *[v_out[n] for n in _NAMES])
```

```python
import jax
import jax.numpy as jnp
from jax import lax
from jax.experimental import pallas as pl
from jax.experimental.pallas import tpu as pltpu

F32 = jnp.float32
BF16 = jnp.bfloat16

D_MODEL = 1024
DEPTH = 4
PLE_DIM = 256
HEAD_DIM = 64
WIN_DIL = ((128, 1), (512, 4), (2048, 16))
N_GROUPS = 3
N_BACK = 128
BLOCK_UNROLL = 4
ROPE_THETA = 10000.0
CONV_WIDTH = 31
CONV_HALO = 32
RMS_EPS = 1e-6
LN_EPS = 1e-5
NEG_INF = -1e30
ADAM_LR, ADAM_B1, ADAM_B2, ADAM_EPS, ADAM_WD, ADAM_STEP = 0.001, 0.9, 0.999, 1e-08, 0.01, 10

LANES = 128
N_CHIPS = 4
VMEM_LIMIT = 48 * 1024 * 1024
VMEM_LIMIT_ATTN = 56 * 1024 * 1024
FLAT_COLS = 256
FLAT_ROWS = 36864
FLAT_TILE = 2048
FLAT_ROW_ALIGN = 16
PROJ_COLS = (3 * N_GROUPS + 1) * D_MODEL
HEAD_PAIRS = D_MODEL // LANES

MESH = pl.DeviceIdType.MESH


def _params(sem=None, vmem=VMEM_LIMIT):
    return pltpu.CompilerParams(dimension_semantics=sem, vmem_limit_bytes=vmem)


def _sigmoid(v):
    return 1.0 / (1.0 + jnp.exp(-v))


def _mm(a, b, *, ta=False, tb=False, out_dtype=F32, tm=1024, tn=1024, tk=1024, name="mm"):
    if ta:
        K, M = a.shape
    else:
        M, K = a.shape
    if tb:
        N, K2 = b.shape
    else:
        K2, N = b.shape
    assert K == K2, (a.shape, b.shape)
    tm, tn, tk = min(tm, M), min(tn, N), min(tk, K)
    assert M % tm == 0 and N % tn == 0 and K % tk == 0
    nk = K // tk
    dims = (((0 if ta else 1,), (1 if tb else 0,)), ((), ()))

    def body(a_ref, b_ref, o_ref, *scratch):
        k = pl.program_id(2)
        part = lax.dot_general(a_ref[...].astype(BF16), b_ref[...].astype(BF16), dims, preferred_element_type=F32)
        if nk == 1:
            o_ref[...] = part.astype(out_dtype)
        else:
            acc_ref, = scratch

            @pl.when(k == 0)
            def _():
                acc_ref[...] = part

            @pl.when((k > 0) & (k < nk - 1))
            def _():
                acc_ref[...] += part

            @pl.when(k == nk - 1)
            def _():
                o_ref[...] = (acc_ref[...] + part).astype(out_dtype)

    a_spec = pl.BlockSpec((tk, tm), lambda i, j, k: (k, i)) if ta else pl.BlockSpec((tm, tk), lambda i, j, k: (i, k))
    b_spec = pl.BlockSpec((tn, tk), lambda i, j, k: (j, k)) if tb else pl.BlockSpec((tk, tn), lambda i, j, k: (k, j))
    return pl.pallas_call(
        body, name=name, grid=(M // tm, N // tn, nk),
        in_specs=[a_spec, b_spec], out_specs=pl.BlockSpec((tm, tn), lambda i, j, k: (i, j)),
        out_shape=jax.ShapeDtypeStruct((M, N), out_dtype),
        scratch_shapes=[pltpu.VMEM((tm, tn), F32)] if nk > 1 else [],
        compiler_params=_params(("parallel", "parallel", "arbitrary")),
    )(a, b)


def _mm_rows(a, b, extras, epilogue, out_dtypes, name, tm=512):
    M, K = a.shape
    N = b.shape[1]
    n_ex = len(extras)

    def body(*refs):
        a_ref, b_ref = refs[:2]
        av = a_ref[...]
        acc = jnp.dot(av.astype(BF16), b_ref[...].astype(BF16), preferred_element_type=F32)
        results = epilogue(acc, av, *[e[...] for e in refs[2:2 + n_ex]])
        for o_ref, r in zip(refs[2 + n_ex:], results):
            o_ref[...] = r.astype(o_ref.dtype)

    tile = pl.BlockSpec((tm, N), lambda i: (i, 0))
    in_specs = [pl.BlockSpec((tm, K), lambda i: (i, 0)), pl.BlockSpec((K, N), lambda i: (0, 0))]
    in_specs += [tile if e.shape[0] == M else pl.BlockSpec((1, N), lambda i: (0, 0)) for e in extras]
    return pl.pallas_call(
        body, name=name, grid=(M // tm,), in_specs=in_specs, out_specs=[tile] * len(out_dtypes),
        out_shape=[jax.ShapeDtypeStruct((M, N), dt) for dt in out_dtypes],
        compiler_params=_params(("parallel",)),
    )(a, b, *extras)


ROW_TILE = 512


def _rows(w=D_MODEL, cb=0, tr=ROW_TILE):
    return pl.BlockSpec((tr, w), lambda i: (i, cb))


def _full(shape):
    return pl.BlockSpec(shape, lambda i: (0,) * len(shape))


def _row_call(body, name, T, in_specs, out_specs, out_shape, args, tr=ROW_TILE):
    return pl.pallas_call(body, name=name, grid=(T // tr,), in_specs=in_specs, out_specs=out_specs,
                          out_shape=out_shape, compiler_params=_params(("arbitrary",)))(*args)


def _sds(shape, dtype):
    return jax.ShapeDtypeStruct(shape, dtype)


def _rmsnorm_fwd(x, g):
    T = x.shape[0]

    def body(x_ref, g_ref, h_ref):
        xv = x_ref[...]
        r = lax.rsqrt(jnp.mean(xv * xv, axis=1, keepdims=True) + RMS_EPS)
        h_ref[...] = (xv * r * g_ref[...]).astype(BF16)

    return _row_call(body, "rmsnorm_fwd", T, [_rows(), _full((1, D_MODEL))], _rows(),
                     _sds((T, D_MODEL), BF16), (x, g))


def _post_epilogue(y, a_tile, x, g):
    del a_tile
    return y, x + y * lax.rsqrt(jnp.mean(y * y, axis=1, keepdims=True) + RMS_EPS) * g


def _ple_epilogue(gl, x1, pe):
    return gl, x1 + pe * _sigmoid(gl)


def _dgrad_pre_bwd(a, b, x, g, dx1, add, tk, name, tm=512):
    M, K = a.shape
    N = b.shape[0]
    tk = min(tk, K)
    assert M % tm == 0 and K % tk == 0 and N == D_MODEL
    nk = K // tk

    def body(*refs):
        a_ref, b_ref, x_ref, g_ref, dx1_ref = refs[:5]
        add_ref = refs[5] if add is not None else None
        dx_ref, dg_ref = refs[-3:-1] if nk > 1 else refs[-2:]
        i, k = pl.program_id(0), pl.program_id(1)

        @pl.when((i == 0) & (k == 0))
        def _():
            dg_ref[...] = jnp.zeros_like(dg_ref)

        part = lax.dot_general(a_ref[...].astype(BF16), b_ref[...].astype(BF16), _NT, preferred_element_type=F32)

        def finish(dh):
            if add is not None:
                dh = dh + add_ref[...]
            xv = x_ref[...]
            r = lax.rsqrt(jnp.mean(xv * xv, axis=1, keepdims=True) + RMS_EPS)
            xh = xv * r
            dg_ref[...] += jnp.sum(dh * xh, axis=0, keepdims=True)
            dn = dh * g_ref[...]
            dx_ref[...] = dx1_ref[...] + r * (dn - xh * jnp.mean(dn * xh, axis=1, keepdims=True))

        if nk == 1:
            finish(part)
        else:
            acc_ref = refs[-1]

            @pl.when(k == 0)
            def _():
                acc_ref[...] = part

            @pl.when((k > 0) & (k < nk - 1))
            def _():
                acc_ref[...] += part

            @pl.when(k == nk - 1)
            def _():
                finish(acc_ref[...] + part)

    tile = pl.BlockSpec((tm, N), lambda i, k: (i, 0))
    row = pl.BlockSpec((1, N), lambda i, k: (0, 0))
    in_specs = [pl.BlockSpec((tm, tk), lambda i, k: (i, k)), pl.BlockSpec((N, tk), lambda i, k: (0, k)), tile, row, tile]
    args = [a, b, x, g, dx1]
    if add is not None:
        in_specs.append(tile)
        args.append(add)
    return pl.pallas_call(
        body, name=name, grid=(M // tm, nk), in_specs=in_specs, out_specs=[tile, row],
        out_shape=[_sds((M, N), F32), _sds((1, N), F32)],
        scratch_shapes=[pltpu.VMEM((tm, N), F32)] if nk > 1 else [],
        compiler_params=_params(("arbitrary", "arbitrary")),
    )(*args)


def _ple_post_bwd(dx2, pe, gl, w_gate, y, g_post):
    T = dx2.shape[0]

    def body(d_ref, pe_ref, gl_ref, w_ref, y_ref, g_ref, dpe_ref, dgl_ref, dx1_ref, dy_ref, dg_ref):
        @pl.when(pl.program_id(0) == 0)
        def _():
            dg_ref[...] = jnp.zeros_like(dg_ref)

        dv = d_ref[...]
        sg = _sigmoid(gl_ref[...])
        dpe_ref[...] = (dv * sg).astype(BF16)
        dgl = (dv * pe_ref[...] * sg * (1.0 - sg)).astype(BF16)
        dgl_ref[...] = dgl
        dx1 = dv + lax.dot_general(dgl, w_ref[...], _NT, preferred_element_type=F32)
        dx1_ref[...] = dx1
        yv = y_ref[...]
        r = lax.rsqrt(jnp.mean(yv * yv, axis=1, keepdims=True) + RMS_EPS)
        yh = yv * r
        dg_ref[...] += jnp.sum(dx1 * yh, axis=0, keepdims=True)
        dn = dx1 * g_ref[...]
        dy_ref[...] = (r * (dn - yh * jnp.mean(dn * yh, axis=1, keepdims=True))).astype(BF16)

    row = _full((1, D_MODEL))
    return _row_call(body, "ple_post_bwd", T, [_rows()] * 3 + [_full((D_MODEL, D_MODEL)), _rows(), row],
                     [_rows()] * 4 + [row],
                     [_sds((T, D_MODEL), BF16)] * 2 + [_sds((T, D_MODEL), F32), _sds((T, D_MODEL), BF16),
                                                      _sds((1, D_MODEL), F32)],
                     (dx2, pe, gl, w_gate, y, g_post))


def _loss_fwd_bwd(y, target):
    T = y.shape[0]

    def body(y_ref, t_ref, s_ref, d_ref):
        @pl.when(pl.program_id(0) == 0)
        def _():
            s_ref[...] = jnp.zeros_like(s_ref)

        e = y_ref[...] - t_ref[...]
        s_ref[...] += jnp.sum(e * e).reshape(1, 1)
        d_ref[...] = e * (1.0 / D_MODEL)

    return _row_call(body, "loss", T, [_rows()] * 2, [_full((1, 1)), _rows()],
                     [_sds((1, 1), F32), _sds((T, D_MODEL), F32)], (y, target))


def _attn_combine(outs, lses, proj):
    T = proj.shape[0]

    def body(o0, o1, o2, l0, l1, l2, z_ref, a_ref, o_ref, lse_ref):
        a0, a1, a2 = l0[...], l1[...], l2[...]
        m = jnp.maximum(jnp.maximum(a0, a1), a2)
        e0, e1, e2 = jnp.exp(a0 - m), jnp.exp(a1 - m), jnp.exp(a2 - m)
        ssum = e0 + e1 + e2
        o = (e0 * o0[...] + e1 * o1[...] + e2 * o2[...]) / ssum
        zv = z_ref[...].astype(F32)
        o_ref[...] = o
        lse_ref[...] = m + jnp.log(ssum)
        a_ref[...] = (o * zv * _sigmoid(zv)).astype(BF16)

    return _row_call(body, "attn_combine", T, [_rows()] * 6 + [_rows(cb=3 * N_GROUPS)], [_rows()] * 3,
                     [_sds((T, D_MODEL), BF16), _sds((T, D_MODEL), F32), _sds((T, D_MODEL), F32)],
                     (*outs, *lses, proj))


def _gate_bwd(da, o, proj):
    T = da.shape[0]

    def body(da_ref, o_ref, z_ref, do_ref, dz_ref):
        dv = da_ref[...]
        zv = z_ref[...]
        sg = _sigmoid(zv)
        do_ref[...] = dv * zv * sg
        dz_ref[...] = dv * o_ref[...] * sg * (1.0 + zv * (1.0 - sg))

    zcols = _rows(cb=3 * N_GROUPS)
    return _row_call(body, "gate_bwd", T, [_rows(), _rows(), zcols], [_rows(), zcols],
                     [_sds((T, D_MODEL), F32), _sds((T, PROJ_COLS), F32)], (da, o, proj))


def _rope_tables(positions):
    inv_freq = 1.0 / (ROPE_THETA ** (jnp.arange(0, HEAD_DIM, 2, dtype=F32) / HEAD_DIM))
    ang = positions.astype(F32)[..., None] * inv_freq
    cos, sin = jnp.cos(ang), jnp.sin(ang)
    return jnp.tile(cos, (1, 1, 4)), jnp.concatenate([-sin, sin, -sin, sin], axis=-1)


def _rotate_half_partner(t):
    lane = lax.broadcasted_iota(jnp.int32, t.shape, 1)
    return jnp.where((lane % HEAD_DIM) < HEAD_DIM // 2,
                     pltpu.roll(t, LANES - HEAD_DIM // 2, 1), pltpu.roll(t, HEAD_DIM // 2, 1))


def _mask_bias(first):
    qi = lax.broadcasted_iota(jnp.int32, (N_BACK, 2 * N_BACK), 0)
    kj = lax.broadcasted_iota(jnp.int32, (N_BACK, 2 * N_BACK), 1)
    ok = (kj >= qi) & (kj <= qi + N_BACK)
    if first:
        ok = ok & (kj >= N_BACK)
    return jnp.where(ok, 0.0, NEG_INF).astype(F32)


def _stack_heads(t, head0):
    zero = jnp.zeros_like(t)
    return jnp.concatenate([jnp.where(head0, t, zero), jnp.where(head0, zero, t)], axis=0)


def _unstack_heads(t2, head0):
    return jnp.where(head0, t2[:N_BACK], t2[N_BACK:])


def _block_loop(nb, block):
    first, rest = _mask_bias(True), _mask_bias(False)
    first, rest = jnp.concatenate([first, first], axis=0), jnp.concatenate([rest, rest], axis=0)
    if nb <= BLOCK_UNROLL:
        for n in range(nb):
            block(n, first if n == 0 else rest)
        return

    def step(n, carry):
        block(n, jnp.where(n == 0, first, rest))
        return carry

    lax.fori_loop(0, nb, step, 0, unroll=BLOCK_UNROLL)


def _for(count, body, unroll_fully):
    if unroll_fully:
        for i in range(count):
            body(i)
    else:
        lax.fori_loop(0, count, lambda i, carry: (body(i), carry)[1], 0)


def _residues_together(nb):
    return max(1, BLOCK_UNROLL // nb)


def _residue_loop(d, nb, residue):
    together = _residues_together(nb)
    assert d % together == 0

    def group(i, carry):
        for u in range(together):
            residue(i * together + u, u)
        return carry

    lax.fori_loop(0, d // together, group, 0)


_NT = (((1,), (1,)), ((), ()))
_TN = (((0,), (0,)), ((), ()))


def _residue_rows(r, i, d):
    start = r + i * (N_BACK * d)
    if d == 1:
        return pl.ds(pl.multiple_of(start, N_BACK), N_BACK)
    return pl.ds(start, N_BACK, stride=d)


def _seq_rows(i):
    return pl.ds(pl.multiple_of(i * N_BACK, N_BACK), N_BACK)


def _rows_at(base, i, size=N_BACK):
    return pl.ds(pl.multiple_of(base + i * N_BACK, N_BACK), size)


def _attn_fwd(proj, cos, sin, group, Bl, S):
    d = WIN_DIL[group][1]
    L = S // d
    nb = L // N_BACK
    P = L + N_BACK
    assert WIN_DIL[group][0] // d == N_BACK and L % N_BACK == 0

    def body(q_ref, k_ref, v_ref, cos_ref, sin_ref, o_ref, lse_ref, qr, kr, vp):
        head0 = lax.broadcasted_iota(jnp.int32, (1, LANES), 1) < HEAD_DIM
        zeros = jnp.zeros((N_BACK, LANES), BF16)

        def residue(r, u):
            del u
            qbase, kbase = r * L, r * P
            kr[_rows_at(kbase, 0), :] = zeros
            vp[_rows_at(kbase, 0), :] = zeros

            def rope(i):
                rows = _residue_rows(r, i, d)
                cs, sn = cos_ref[rows, :], sin_ref[rows, :]
                q, k = q_ref[rows, :], k_ref[rows, :]
                qr[_rows_at(qbase, i), :] = ((q * cs + _rotate_half_partner(q) * sn)
                                            * (HEAD_DIM ** -0.5)).astype(BF16)
                kr[_rows_at(kbase, i + 1), :] = (k * cs + _rotate_half_partner(k) * sn).astype(BF16)
                vp[_rows_at(kbase, i + 1), :] = v_ref[rows, :].astype(BF16)

            _for(nb, rope, nb <= BLOCK_UNROLL)

            def block(n, bias):
                win = _rows_at(kbase, n, 2 * N_BACK)
                q2, kw, vw = _stack_heads(qr[_rows_at(qbase, n), :], head0), kr[win, :], vp[win, :]
                s = lax.dot_general(q2, kw, _NT, preferred_element_type=F32) + bias
                m = jnp.max(s, axis=1, keepdims=True)
                p = jnp.exp(s - m)
                l = jnp.sum(p, axis=1, keepdims=True)
                pv = jnp.dot(p.astype(BF16), vw, preferred_element_type=F32)
                rows = _residue_rows(r, n, d)
                o_ref[rows, :] = _unstack_heads(pv * (1.0 / l), head0)
                lse_ref[rows, :] = _unstack_heads((m + jnp.log(l)) + jnp.zeros((2 * N_BACK, LANES), F32), head0)

            _block_loop(nb, block)

        _residue_loop(d, nb, residue)

    act = pl.BlockSpec((None, S, LANES), lambda b, hp: (b, 0, hp))
    tab = pl.BlockSpec((None, S, LANES), lambda b, hp: (b, 0, 0))
    col = lambda which: pl.BlockSpec((None, S, LANES),
                                     lambda b, hp: (b, 0, (which * N_GROUPS + group) * HEAD_PAIRS + hp))
    seq = lambda rows: pl.BlockSpec((None, None, rows, LANES), lambda b, hp: (b, hp, 0, 0))
    p3 = proj.reshape(Bl, S, PROJ_COLS)
    o, lse, qr, kr, vp = pl.pallas_call(
        body, name="attn_fwd_g%d" % group, grid=(Bl, HEAD_PAIRS),
        in_specs=[col(0), col(1), col(2), tab, tab], out_specs=[act, act, seq(S), seq(d * P), seq(d * P)],
        out_shape=[_sds((Bl, S, D_MODEL), F32)] * 2 + [_sds((Bl, HEAD_PAIRS, S, LANES), BF16)]
        + [_sds((Bl, HEAD_PAIRS, d * P, LANES), BF16)] * 2,
        compiler_params=_params(("parallel", "arbitrary"), VMEM_LIMIT_ATTN),
    )(p3, p3, p3, cos, sin)
    return o.reshape(Bl * S, D_MODEL), lse.reshape(Bl * S, D_MODEL), (qr, kr, vp)


def _attn_bwd(saved, cos, sin, do, o, lse, dproj, group, Bl, S):
    d = WIN_DIL[group][1]
    L = S // d
    nb = L // N_BACK
    P = L + N_BACK
    steps = Bl * HEAD_PAIRS

    def body(qr, kr, vp, cos_ref, sin_ref, do_ref, o_ref, lse_ref, dproj_in, dproj_ref,
             dk_accs, dv_accs, stage, sems):
        del dproj_in
        head0 = lax.broadcasted_iota(jnp.int32, (1, LANES), 1) < HEAD_DIM
        b, hp = pl.program_id(0), pl.program_id(1)
        step = b * HEAD_PAIRS + hp
        slot = step % 2
        dq_s, dk_s, dv_s = stage.at[slot, 0], stage.at[slot, 1], stage.at[slot, 2]

        def copies(which_slot):
            out = []
            for which in range(3):
                col = ((which * N_GROUPS + group) * HEAD_PAIRS + hp) * LANES
                out.append(pltpu.make_async_copy(
                    stage.at[which_slot, which], dproj_ref.at[b, :, pl.ds(pl.multiple_of(col, LANES), LANES)],
                    sems.at[which_slot, which]))
            return out

        @pl.when(step >= 2)
        def _():
            for cp in copies(slot):
                cp.wait()

        def residue(r, u):
            qbase, kbase = r * L, r * P
            dk_acc, dv_acc = dk_accs.at[u], dv_accs.at[u]
            dk_acc[...] = jnp.zeros_like(dk_acc)
            dv_acc[...] = jnp.zeros_like(dv_acc)

            def block(n, bias):
                win = pl.ds(pl.multiple_of(n * N_BACK, N_BACK), 2 * N_BACK)
                kwin = _rows_at(kbase, n, 2 * N_BACK)
                rows = _residue_rows(r, n, d)
                q2, kw, vw = _stack_heads(qr[_rows_at(qbase, n), :], head0), kr[kwin, :], vp[kwin, :]
                dof = do_ref[rows, :]
                do2 = _stack_heads(dof.astype(BF16), head0)
                lse_b = lse_ref[rows, :]
                lse2 = jnp.concatenate([lse_b[:, 0:1], lse_b[:, HEAD_DIM:HEAD_DIM + 1]], axis=0)
                dsum = _stack_heads(dof * o_ref[rows, :], head0)
                delta = jnp.sum(dsum, axis=1, keepdims=True)
                s = lax.dot_general(q2, kw, _NT, preferred_element_type=F32) + bias
                p = jnp.exp(s - lse2)
                dp = lax.dot_general(do2, vw, _NT, preferred_element_type=F32)
                ds = (p * (dp - delta)).astype(BF16)
                dq = _unstack_heads(jnp.dot(ds, kw, preferred_element_type=F32), head0) * (HEAD_DIM ** -0.5)
                cs, sn = cos_ref[rows, :], sin_ref[rows, :]
                dq_s[rows, :] = dq * cs + _rotate_half_partner(dq * sn)
                dk_acc[win, :] += lax.dot_general(ds, q2, _TN, preferred_element_type=F32)
                dv_acc[win, :] += lax.dot_general(p.astype(BF16), do2, _TN, preferred_element_type=F32)

            _block_loop(nb, block)

            def finish(i):
                rows = _residue_rows(r, i, d)
                cs, sn = cos_ref[rows, :], sin_ref[rows, :]
                dk = dk_acc[_seq_rows(i + 1), :]
                dk_s[rows, :] = dk * cs + _rotate_half_partner(dk * sn)
                dv_s[rows, :] = dv_acc[_seq_rows(i + 1), :]

            _for(nb, finish, nb <= BLOCK_UNROLL)

        _residue_loop(d, nb, residue)
        for cp in copies(slot):
            cp.start()

        @pl.when(step == steps - 1)
        def _():
            if steps > 1:
                for cp in copies(1 - slot):
                    cp.wait()
            for cp in copies(slot):
                cp.wait()

    act = pl.BlockSpec((None, S, LANES), lambda b, hp: (b, 0, hp))
    tab = pl.BlockSpec((None, S, LANES), lambda b, hp: (b, 0, 0))
    seq = lambda rows: pl.BlockSpec((None, None, rows, LANES), lambda b, hp: (b, hp, 0, 0))
    view = lambda t: t.reshape(Bl, S, D_MODEL)
    out = pl.pallas_call(
        body, name="attn_bwd_g%d" % group, grid=(Bl, HEAD_PAIRS),
        in_specs=[seq(S), seq(d * P), seq(d * P), tab, tab, act, act, act, _ANY], out_specs=_ANY,
        out_shape=_sds((Bl, S, PROJ_COLS), F32), input_output_aliases={8: 0},
        scratch_shapes=[pltpu.VMEM((_residues_together(nb), P, LANES), F32),
                        pltpu.VMEM((_residues_together(nb), P, LANES), F32),
                        pltpu.VMEM((2, 3, S, LANES), F32), pltpu.SemaphoreType.DMA((2, 3))],
        compiler_params=_params(("arbitrary", "arbitrary"), VMEM_LIMIT_ATTN),
    )(*saved, cos, sin, view(do), view(o), view(lse), dproj.reshape(Bl, S, PROJ_COLS))
    return out.reshape(Bl * S, PROJ_COLS)


CONV_TILE = 256
CONV_CHUNK = 64
SUBLANES = 8
CONV_SHIFT_ROWS = CONV_TILE + CONV_HALO - SUBLANES


def _fill_shifted(shifted, ext, cs):
    for k in range(1, SUBLANES):
        shifted[k - 1] = ext[pl.ds(k, CONV_SHIFT_ROWS), cs]


def _shifted_rows(shifted, ext, cs, off):
    k = off % SUBLANES
    if k == 0:
        return ext[pl.ds(off, CONV_CHUNK), cs]
    return shifted[k - 1, pl.ds(off - k, CONV_CHUNK), :]


def _conv_fwd(proj, z, dw, dwb, ln_g, ln_b, Bl, S):
    tr = CONV_TILE
    nj = S // tr
    hb = tr // CONV_HALO

    def body(a_ref, b_ref, ah_ref, bh_ref, z_ref, dw_ref, dwb_ref, g_ref, bb_ref, u1_ref, out_ref, ext, shifted):
        j = pl.program_id(1)
        halo = ah_ref[0].astype(F32) * _sigmoid(bh_ref[0].astype(F32))
        ext[pl.ds(0, CONV_HALO), :] = jnp.where(j > 0, halo, 0.0)
        ext[pl.ds(CONV_HALO, tr), :] = a_ref[0].astype(F32) * _sigmoid(b_ref[0].astype(F32))

        def cols(c, carry):
            cs = pl.ds(pl.multiple_of(c * LANES, LANES), LANES)
            _fill_shifted(shifted, ext, cs)
            for rc in range(tr // CONV_CHUNK):
                acc = jnp.zeros((CONV_CHUNK, LANES), F32)
                for w in range(CONV_WIDTH):
                    off = rc * CONV_CHUNK + CONV_HALO - (CONV_WIDTH - 1) + w
                    acc = acc + dw_ref[pl.ds(w, 1), cs] * _shifted_rows(shifted, ext, cs, off)
                u1_ref[0, pl.ds(rc * CONV_CHUNK, CONV_CHUNK), cs] = acc + dwb_ref[:, cs]
            return carry

        lax.fori_loop(0, D_MODEL // LANES, cols, 0)
        u1 = u1_ref[0]
        mu = jnp.mean(u1, axis=1, keepdims=True)
        xc = u1 - mu
        rstd = lax.rsqrt(jnp.mean(xc * xc, axis=1, keepdims=True) + LN_EPS)
        u2 = xc * rstd * g_ref[...] + bb_ref[...]
        zv = z_ref[0].astype(F32)
        out_ref[0] = (u2 * _sigmoid(u2) * zv * _sigmoid(zv)).astype(BF16)

    tile = lambda cb: pl.BlockSpec((1, tr, D_MODEL), lambda b, j: (b, j, cb))
    halo = lambda cb: pl.BlockSpec((1, CONV_HALO, D_MODEL), lambda b, j: (b, jnp.maximum(j * hb - 1, 0), cb))
    par = lambda r: pl.BlockSpec((r, D_MODEL), lambda b, j: (0, 0))
    p3 = proj.reshape(Bl, S, 2 * D_MODEL)
    u1, out = pl.pallas_call(
        body, name="conv_fwd", grid=(Bl, nj),
        in_specs=[tile(0), tile(1), halo(0), halo(1), tile(0), par(32), par(1), par(1), par(1)],
        out_specs=[tile(0), tile(0)],
        out_shape=[_sds((Bl, S, D_MODEL), F32), _sds((Bl, S, D_MODEL), BF16)],
        scratch_shapes=[pltpu.VMEM((tr + CONV_HALO, D_MODEL), F32),
                        pltpu.VMEM((SUBLANES - 1, CONV_SHIFT_ROWS, LANES), F32)],
        compiler_params=_params(("parallel", "arbitrary")),
    )(p3, p3, p3, p3, z.reshape(Bl, S, D_MODEL), dw, dwb, ln_g, ln_b)
    return u1.reshape(Bl * S, D_MODEL), out.reshape(Bl * S, D_MODEL)


def _conv_norm_bwd(da2, z, u1, ln_g, ln_b):
    T = da2.shape[0]

    def body(da_ref, z_ref, u_ref, g_ref, b_ref, du_ref, dz_ref, dg_ref, db_ref):
        @pl.when(pl.program_id(0) == 0)
        def _():
            dg_ref[...] = jnp.zeros_like(dg_ref)
            db_ref[...] = jnp.zeros_like(db_ref)

        u1 = u_ref[...]
        mu = jnp.mean(u1, axis=1, keepdims=True)
        xc = u1 - mu
        rstd = lax.rsqrt(jnp.mean(xc * xc, axis=1, keepdims=True) + LN_EPS)
        nrm = xc * rstd
        u2 = nrm * g_ref[...] + b_ref[...]
        s2 = _sigmoid(u2)
        zv = z_ref[...].astype(F32)
        sz = _sigmoid(zv)
        dv = da_ref[...]
        dz_ref[...] = (dv * u2 * s2 * sz * (1.0 + zv * (1.0 - sz))).astype(BF16)
        du2 = dv * zv * sz * s2 * (1.0 + u2 * (1.0 - s2))
        dg_ref[...] += jnp.sum(du2 * nrm, axis=0, keepdims=True)
        db_ref[...] += jnp.sum(du2, axis=0, keepdims=True)
        dn = du2 * g_ref[...]
        du_ref[...] = rstd * (dn - jnp.mean(dn, axis=1, keepdims=True)
                              - nrm * jnp.mean(dn * nrm, axis=1, keepdims=True))

    return _row_call(body, "conv_norm_bwd", T,
                     [_rows(), _rows(), _rows(), _full((1, D_MODEL)), _full((1, D_MODEL))],
                     [_rows(), _rows(), _full((1, D_MODEL)), _full((1, D_MODEL))],
                     [_sds((T, D_MODEL), F32), _sds((T, D_MODEL), BF16), _sds((1, D_MODEL), F32),
                      _sds((1, D_MODEL), F32)], (da2, z, u1, ln_g, ln_b))


def _conv_bwd(proj, du1, dw, Bl, S):
    tr = CONV_TILE
    nj = S // tr
    hb = tr // CONV_HALO

    def body(a_ref, b_ref, ah_ref, bh_ref, du_ref, duh_ref, dw_ref, dab_ref, ddw_ref, ddb_ref, uext, dext, du0,
             ushift, dshift, ddw8):
        first = (pl.program_id(0) == 0) & (pl.program_id(1) == 0)
        last = (pl.program_id(0) == Bl - 1) & (pl.program_id(1) == nj - 1)
        j = pl.program_id(1)

        @pl.when(first)
        def _():
            ddw8[...] = jnp.zeros_like(ddw8)
            ddb_ref[...] = jnp.zeros_like(ddb_ref)

        halo = ah_ref[0].astype(F32) * _sigmoid(bh_ref[0].astype(F32))
        uext[pl.ds(0, CONV_HALO), :] = jnp.where(j > 0, halo, 0.0)
        av = a_ref[0].astype(F32)
        sb = _sigmoid(b_ref[0].astype(F32))
        uext[pl.ds(CONV_HALO, tr), :] = av * sb
        dext[pl.ds(0, tr), :] = du_ref[0]
        dext[pl.ds(tr, CONV_HALO), :] = jnp.where(j < nj - 1, duh_ref[0], 0.0)
        ddb_ref[...] += jnp.sum(du_ref[0], axis=0, keepdims=True)

        def cols(c, carry):
            cs = pl.ds(pl.multiple_of(c * LANES, LANES), LANES)
            _fill_shifted(dshift, dext, cs)
            _fill_shifted(ushift, uext, cs)
            for rc in range(tr // CONV_CHUNK):
                base = rc * CONV_CHUNK
                acc = jnp.zeros((CONV_CHUNK, LANES), F32)
                for w in range(CONV_WIDTH):
                    acc = acc + dw_ref[pl.ds(w, 1), cs] * _shifted_rows(dshift, dext, cs, base + CONV_WIDTH - 1 - w)
                du0[pl.ds(base, CONV_CHUNK), cs] = acc
            for w in range(CONV_WIDTH):
                part = jnp.zeros((SUBLANES, LANES), F32)
                for rc in range(tr // CONV_CHUNK):
                    base = rc * CONV_CHUNK
                    prod = dext[pl.ds(base, CONV_CHUNK), cs] * _shifted_rows(
                        ushift, uext, cs, base + CONV_HALO - (CONV_WIDTH - 1) + w)
                    for i in range(CONV_CHUNK // SUBLANES):
                        part = part + prod[i * SUBLANES:(i + 1) * SUBLANES]
                ddw8[pl.ds(w * SUBLANES, SUBLANES), cs] += part
            return carry

        lax.fori_loop(0, D_MODEL // LANES, cols, 0)
        g = du0[...]
        dab_ref[0, :, 0:D_MODEL] = (g * sb).astype(BF16)
        dab_ref[0, :, D_MODEL:2 * D_MODEL] = (g * av * sb * (1.0 - sb)).astype(BF16)

        @pl.when(last)
        def _():
            for w in range(CONV_WIDTH + 1):
                ddw_ref[pl.ds(w, 1), :] = jnp.sum(ddw8[pl.ds(w * SUBLANES, SUBLANES), :], axis=0, keepdims=True)

    tile = lambda cb: pl.BlockSpec((1, tr, D_MODEL), lambda b, j: (b, j, cb))
    halo = lambda cb: pl.BlockSpec((1, CONV_HALO, D_MODEL), lambda b, j: (b, jnp.maximum(j * hb - 1, 0), cb))
    nxt = pl.BlockSpec((1, CONV_HALO, D_MODEL), lambda b, j: (b, jnp.minimum((j + 1) * hb, S // CONV_HALO - 1), 0))
    par = lambda r: pl.BlockSpec((r, D_MODEL), lambda b, j: (0, 0))
    p3 = proj.reshape(Bl, S, 2 * D_MODEL)
    d3 = du1.reshape(Bl, S, D_MODEL)
    dab, ddw, ddb = pl.pallas_call(
        body, name="conv_bwd", grid=(Bl, nj),
        in_specs=[tile(0), tile(1), halo(0), halo(1), tile(0), nxt, par(32)],
        out_specs=[pl.BlockSpec((1, tr, 2 * D_MODEL), lambda b, j: (b, j, 0)), par(32), par(1)],
        out_shape=[_sds((Bl, S, 2 * D_MODEL), BF16), _sds((32, D_MODEL), F32), _sds((1, D_MODEL), F32)],
        scratch_shapes=[pltpu.VMEM((tr + CONV_HALO, D_MODEL), F32), pltpu.VMEM((tr + CONV_HALO, D_MODEL), F32),
                        pltpu.VMEM((tr, D_MODEL), F32),
                        pltpu.VMEM((SUBLANES - 1, CONV_SHIFT_ROWS, LANES), F32),
                        pltpu.VMEM((SUBLANES - 1, CONV_SHIFT_ROWS, LANES), F32),
                        pltpu.VMEM(((CONV_WIDTH + 1) * SUBLANES, D_MODEL), F32)],
        compiler_params=_params(("arbitrary", "arbitrary")),
    )(p3, p3, p3, p3, d3, d3, dw)
    return dab.reshape(Bl * S, 2 * D_MODEL), ddw, ddb


_LAYOUT = (
    ("pre_norm_g", (4, 1024), None), ("post_norm_g", (4, 1024), None),
    ("attn_w_in", (2, 1024, 2560), 2), ("attn_w_out", (2, 256, 1024), 1),
    ("conv_w_in", (2, 1024, 768), 2), ("conv_dw_w", (2, 31, 256), 2),
    ("conv_dw_b", (2, 256), 1), ("conv_ln_g", (2, 256), 1), ("conv_ln_b", (2, 256), 1),
    ("conv_w_out", (2, 256, 1024), 1), ("ple_w_proj", (4, 256, 256), 2), ("ple_w_gate", (4, 256, 1024), 1),
)
_MATMUL_WEIGHTS = ("attn_w_in", "attn_w_out", "conv_w_in", "conv_w_out", "ple_w_proj", "ple_w_gate")
_AXIS = {n: a for n, _, a in _LAYOUT}


def _size(shape):
    n = 1
    for s in shape:
        n *= s
    return n


def _padded_rows(shape):
    rows = _size(shape) // shape[-1]
    return rows + (-rows) % FLAT_ROW_ALIGN


def _rows2d(a):
    a2 = a.reshape(-1, a.shape[-1])
    pad = _padded_rows(a.shape) - a2.shape[0]
    return jnp.pad(a2, ((0, pad), (0, 0))) if pad else a2


def _col_blocks(a):
    a2 = _rows2d(a)
    return jnp.concatenate([a2[:, c:c + FLAT_COLS] for c in range(0, a2.shape[1], FLAT_COLS)], axis=0)


def _from_col_blocks(flat, off, shape):
    rows, nblk = _padded_rows(shape), shape[-1] // FLAT_COLS
    a2 = jnp.concatenate([flat[off + b * rows:off + (b + 1) * rows] for b in range(nblk)], axis=1)
    return a2[:_size(shape) // shape[-1]].reshape(shape), off + nblk * rows


def _shard_col_blocks(full, shape, axis):
    if axis is None:
        blocks = _col_blocks(full)
        return jnp.broadcast_to(blocks[None], (N_CHIPS,) + blocks.shape)
    m = shape[-1]
    if axis == len(shape) - 1:
        a2 = _rows2d(full)
        pieces = [a2[:, c:c + FLAT_COLS] for c in range(0, N_CHIPS * m, FLAT_COLS)]
    else:
        layers, r, _ = shape
        assert axis == 1 and (layers * r) % FLAT_ROW_ALIGN == 0
        pieces = [full[:, s * r:(s + 1) * r, c:c + FLAT_COLS].reshape(layers * r, FLAT_COLS)
                  for s in range(N_CHIPS) for c in range(0, m, FLAT_COLS)]
    return jnp.concatenate(pieces, axis=0).reshape(N_CHIPS, -1, FLAT_COLS)


_FLAT_BIG = ("attn_w_in", "conv_w_in", "ple_w_gate", "attn_w_out", "conv_w_out", "ple_w_proj")
_FLAT_SMALL = ("pre_norm_g", "post_norm_g", "conv_dw_w", "conv_dw_b", "conv_ln_g", "conv_ln_b")
_SHARD_SHAPE = {n: s for n, s, _ in _LAYOUT}
PACK_TILE = 1024


def _flat_offsets():
    out, off = {}, 0
    for n in _FLAT_BIG + _FLAT_SMALL:
        out[n] = off
        off += _padded_rows(_SHARD_SHAPE[n]) * (_SHARD_SHAPE[n][-1] // FLAT_COLS)
    return out, off


def _unpack_f32(flat):
    offsets, _ = _flat_offsets()
    return {n: _from_col_blocks(flat, offsets[n], _SHARD_SHAPE[n])[0] for n in _FLAT_BIG + _FLAT_SMALL}


def _pack_param(full, shape, axis, off, flat):
    layers, r, m = shape
    nblk = m // FLAT_COLS
    if axis == 2:
        rows = layers * r
        tr = min(PACK_TILE, rows)
        assert rows % tr == 0 and off % tr == 0
        src = full.reshape(rows, N_CHIPS * m)
        grid = (N_CHIPS * nblk, rows // tr)
        in_spec = pl.BlockSpec((tr, FLAT_COLS), lambda j, i: (i, j))
        out_spec = pl.BlockSpec((None, tr, FLAT_COLS), lambda j, i: (j // nblk, (off + (j % nblk) * rows) // tr + i, 0))
    else:
        assert axis == 1 and off % r == 0
        src = full.reshape(layers * N_CHIPS * r, m)
        grid = (layers, N_CHIPS, nblk)
        in_spec = pl.BlockSpec((r, FLAT_COLS), lambda l, s, b: (l * N_CHIPS + s, b))
        out_spec = pl.BlockSpec((None, r, FLAT_COLS), lambda l, s, b: (s, (off + b * layers * r) // r + l, 0))

    def copy_body(src_ref, *rest):
        rest[-1][...] = src_ref[...]

    args, in_specs, aliases = [src], [in_spec], {}
    if flat is not None:
        args.append(flat)
        in_specs.append(_ANY)
        aliases = {1: 0}
    return pl.pallas_call(
        copy_body, name="pack_grad", grid=grid, in_specs=in_specs, out_specs=out_spec,
        out_shape=_sds((N_CHIPS, FLAT_ROWS, FLAT_COLS), F32), input_output_aliases=aliases,
        compiler_params=_params(("arbitrary",) * len(grid)))(*args)


SMALL_ROWS = 40


def _stack_small(w):
    rows = [w["conv_dw_w"]] + [w[n][:, None, :] for n in ("conv_dw_b", "conv_ln_g", "conv_ln_b")]
    stacked = jnp.concatenate(rows, axis=1)
    return jnp.pad(stacked, ((0, 0), (0, SMALL_ROWS - stacked.shape[1]), (0, 0)))


def _unstack_small(small):
    return {"conv_dw_w": small[:, :CONV_WIDTH], "conv_dw_b": small[:, CONV_WIDTH],
            "conv_ln_g": small[:, CONV_WIDTH + 1], "conv_ln_b": small[:, CONV_WIDTH + 2]}


def _pack_full_grads(grads):
    offsets, end = _flat_offsets()
    assert end <= FLAT_ROWS
    flat = None
    for n in _FLAT_BIG:
        flat = _pack_param(grads[n], _SHARD_SHAPE[n], _AXIS[n], offsets[n], flat)
    small = jnp.concatenate([_shard_col_blocks(grads[n], _SHARD_SHAPE[n], _AXIS[n]) for n in _FLAT_SMALL], axis=1)
    start = offsets[_FLAT_SMALL[0]]
    small = jnp.pad(small, ((0, 0), (0, FLAT_ROWS - start - small.shape[1]), (0, 0)))
    return lax.dynamic_update_slice(flat, small, (0, start, 0))


_ANY = pl.BlockSpec(memory_space=pl.ANY)


def _mesh_pos():
    return lax.axis_index("x"), lax.axis_index("y"), lax.axis_index("c")


def _other_chips(x, y):
    return [(1 - x, y), (x, 1 - y), (1 - x, 1 - y)]


COPIES_PER_ARRAY = 7


def _allgather_weights(shards, axes, small):
    n = len(shards)
    full_shape = lambda a, axis: tuple(d * (N_CHIPS if i == axis else 1) for i, d in enumerate(a.shape))

    def body(*refs):
        ins, small_in = refs[:n], refs[n]
        outs, small_out = refs[n + 1:2 * n + 1], refs[2 * n + 1]
        send_sems, recv_sems = refs[2 * n + 2:]
        x, y, c = _mesh_pos()
        mine, me, sibling = 2 * x + y, (x, y, c), (x, y, 1 - c)
        chips = _other_chips(x, y)

        def region(a, chip, half):
            _, rows, cols = shards[a].shape
            h = rows // 2
            if axes[a] == 2:
                return outs[a].at[:, slice(None) if half is None else pl.ds(half * h, h), pl.ds(chip * cols, cols)]
            if half is None:
                return outs[a].at[:, pl.ds(chip * rows, rows), :]
            return outs[a].at[:, pl.ds(chip * rows + half * h, h), :]

        def copy(k, src, dst, to):
            return pltpu.make_async_remote_copy(src_ref=src, dst_ref=dst, send_sem=send_sems.at[k],
                                                recv_sem=recv_sems.at[k], device_id=to, device_id_type=MESH)

        def arrival(k, dst):
            return copy(k, dst, dst, me)

        sends = []
        for a in range(n):
            h = shards[a].shape[1] // 2
            base = a * COPIES_PER_ARRAY
            sends.append(copy(base + 6, ins[a], region(a, mine, None), sibling))
            for j, (cx, cy) in enumerate(chips):
                sends.append(copy(base + j, ins[a].at[:, pl.ds(c * h, h), :], region(a, mine, c), (cx, cy, c)))
        small_cols = small.shape[2]
        small_region = lambda chip: small_out.at[:, :, pl.ds(chip * small_cols, small_cols)]
        base = n * COPIES_PER_ARRAY
        sends.append(copy(base + 3, small_in, small_region(mine), sibling))
        for j, (cx, cy) in enumerate(chips):
            sends.append(copy(base + j, small_in, small_region(mine), (cx, cy, c)))
        for cp in sends:
            cp.start()
        for j, (cx, cy) in enumerate(chips):
            for a in range(n):
                k = a * COPIES_PER_ARRAY + j
                arrival(k, region(a, 2 * cx + cy, c)).wait_recv()
                passed = copy(k + 3, region(a, 2 * cx + cy, c), region(a, 2 * cx + cy, c), sibling)
                passed.start()
                sends.append(passed)
        for j, (cx, cy) in enumerate(chips):
            for a in range(n):
                arrival(a * COPIES_PER_ARRAY + 3 + j, region(a, 2 * cx + cy, 1 - c)).wait_recv()
            arrival(base + j, small_region(2 * cx + cy)).wait_recv()
        for a in range(n):
            arrival(a * COPIES_PER_ARRAY + 6, region(a, mine, None)).wait_recv()
        arrival(base + 3, small_region(mine)).wait_recv()
        for cp in sends:
            cp.wait_send()

    n_sems = n * COPIES_PER_ARRAY + 4
    out = pl.pallas_call(
        body, name="allgather_weights", in_specs=[_ANY] * (n + 1), out_specs=[_ANY] * (n + 1),
        out_shape=[_sds(full_shape(a, axis), a.dtype) for a, axis in zip(shards, axes)]
        + [_sds(full_shape(small, 2), small.dtype)],
        scratch_shapes=[pltpu.SemaphoreType.DMA((n_sems,)), pltpu.SemaphoreType.DMA((n_sems,))],
    )(*shards, small)
    return out[:n], out[n]


def _exchange_core_halves(g):
    n, _, H, C = g.shape

    def body(g_ref, got_ref, send_sem, recv_sem):
        x, y, c = _mesh_pos()
        swap = pltpu.make_async_remote_copy(
            src_ref=g_ref.at[pl.ds(0, n), 1 - c], dst_ref=got_ref, send_sem=send_sem, recv_sem=recv_sem,
            device_id=(x, y, 1 - c), device_id_type=MESH)
        swap.start()
        swap.wait()

    return pl.pallas_call(
        body, name="exchange_core_halves", in_specs=[_ANY], out_specs=_ANY,
        out_shape=_sds((n, H, C), g.dtype),
        scratch_shapes=[pltpu.SemaphoreType.DMA, pltpu.SemaphoreType.DMA],
    )(g)


def _scatter_to_chips(p):
    n, H, C = p.shape

    def body(p_ref, q_ref, send_sems, recv_sems):
        x, y, c = _mesh_pos()
        chips = _other_chips(x, y)
        sends = [pltpu.make_async_remote_copy(
            src_ref=p_ref.at[2 * cx + cy], dst_ref=q_ref.at[j], send_sem=send_sems.at[j],
            recv_sem=recv_sems.at[j], device_id=(cx, cy, c), device_id_type=MESH)
            for j, (cx, cy) in enumerate(chips)]
        for cp in sends:
            cp.start()
        for cp in sends:
            cp.wait_recv()
        for cp in sends:
            cp.wait_send()

    return pl.pallas_call(
        body, name="scatter_to_chips", in_specs=[_ANY], out_specs=_ANY,
        out_shape=_sds((n - 1, H, C), p.dtype),
        scratch_shapes=[pltpu.SemaphoreType.DMA((3,)), pltpu.SemaphoreType.DMA((3,))],
    )(p)


def _share_core_halves(r2):
    _, H, C = r2.shape

    def body(r_ref, out_ref, send_sem, recv_sem):
        x, y, c = _mesh_pos()
        send = pltpu.make_async_remote_copy(
            src_ref=r_ref.at[c], dst_ref=out_ref.at[c], send_sem=send_sem, recv_sem=recv_sem,
            device_id=(x, y, 1 - c), device_id_type=MESH)
        send.start()
        send.wait_send()
        pltpu.make_async_remote_copy(
            src_ref=r_ref.at[c], dst_ref=out_ref.at[1 - c], send_sem=send_sem, recv_sem=recv_sem,
            device_id=(x, y, 1 - c), device_id_type=MESH).wait_recv()

    return pl.pallas_call(
        body, name="share_core_halves", in_specs=[_ANY], out_specs=_ANY,
        out_shape=_sds(r2.shape, r2.dtype), input_output_aliases={0: 0},
        scratch_shapes=[pltpu.SemaphoreType.DMA, pltpu.SemaphoreType.DMA],
    )(r2)


def _place():
    x, y, c = _mesh_pos()
    return jnp.stack([c, 2 * x + y]).astype(jnp.int32)


def _sum_pair(g, got, place):
    n, _, H, C = g.shape

    def body(place_ref, a_ref, b_ref, o_ref):
        o_ref[...] = (a_ref[...] + b_ref[...]).astype(BF16)

    spec = pl.BlockSpec((1, FLAT_TILE, C), lambda s, i, pr: (s, i, 0))
    return pl.pallas_call(
        body, name="sum_core_pair",
        grid_spec=pltpu.PrefetchScalarGridSpec(
            num_scalar_prefetch=1, grid=(n, H // FLAT_TILE),
            in_specs=[pl.BlockSpec((1, None, FLAT_TILE, C), lambda s, i, pr: (s, pr[0], i, 0)), spec],
            out_specs=spec),
        out_shape=_sds((n, H, C), BF16),
        compiler_params=_params(("parallel", "parallel")))(place, g, got)


def _sum_chips(p, q, place):
    n, H, C = p.shape

    def body(place_ref, own_ref, qx_ref, qy_ref, qxy_ref, o_ref):
        mine = place_ref[1]
        own, qx, qy, qxy = (t[0].astype(F32) for t in (own_ref, qx_ref, qy_ref, qxy_ref))

        def term(s):
            rel = jnp.full(own.shape, mine ^ s, jnp.int32)
            return jnp.where(rel == 0, own, jnp.where(rel == 2, qx, jnp.where(rel == 1, qy, qxy)))

        o_ref[0] = ((term(0) + term(1)) + term(2)) + term(3)

    qspec = lambda j: pl.BlockSpec((1, FLAT_TILE, C), lambda i, pr: (j, i, 0))
    return pl.pallas_call(
        body, name="sum_chips",
        grid_spec=pltpu.PrefetchScalarGridSpec(
            num_scalar_prefetch=1, grid=(H // FLAT_TILE,),
            in_specs=[pl.BlockSpec((1, FLAT_TILE, C), lambda i, pr: (pr[1], i, 0)), qspec(0), qspec(1), qspec(2)],
            out_specs=pl.BlockSpec((1, FLAT_TILE, C), lambda i, pr: (pr[0], i, 0))),
        out_shape=_sds((2, H, C), F32),
        compiler_params=_params(("parallel",)))(place, p, q, q, q)


ADAMW_BLOCK = 1 << 18


def _adamw(w, g, m, v):
    shape = w.shape
    C = shape[-1]
    R = _size(shape) // C
    tr = R
    while tr * C > ADAMW_BLOCK and tr % 16 == 0:
        tr //= 2
    w, g, m, v = (t.reshape(R, C) for t in (w, g, m, v))

    def body(w_ref, g_ref, m_ref, v_ref, d_ref, nm_ref, nv_ref):
        gv = g_ref[...]
        nm = ADAM_B1 * m_ref[...] + (1.0 - ADAM_B1) * gv
        nv = ADAM_B2 * v_ref[...] + (1.0 - ADAM_B2) * (gv * gv)
        m_hat = nm / (1.0 - ADAM_B1 ** ADAM_STEP)
        v_hat = nv / (1.0 - ADAM_B2 ** ADAM_STEP)
        d_ref[...] = -ADAM_LR * (m_hat / (jnp.sqrt(v_hat) + ADAM_EPS) + ADAM_WD * w_ref[...])
        nm_ref[...] = nm
        nv_ref[...] = nv

    spec = pl.BlockSpec((tr, C), lambda i: (i, 0))
    outs = pl.pallas_call(body, name="adamw", grid=(R // tr,), in_specs=[spec] * 4, out_specs=[spec] * 3,
                          out_shape=[_sds((R, C), F32)] * 3, compiler_params=_params(("parallel",)))(w, g, m, v)
    return tuple(t.reshape(shape) for t in outs)


def _reduce_scatter_grads(gfull):
    n, R, C = gfull.shape
    place = _place()
    g4 = gfull.reshape(n, 2, R // 2, C)
    p = _sum_pair(g4, _exchange_core_halves(g4), place)
    return _share_core_halves(_sum_chips(p, _scatter_to_chips(p), place)).reshape(R, C)


def _local_step(x, p, positions, loss_target, pre_g, post_g, w):
    Bl, S, _ = x.shape
    T = Bl * S
    cos, sin = _rope_tables(positions)
    xs = x.reshape(T, D_MODEL)
    saved = []
    for i in range(DEPTH):
        j = i // 2
        g_pre, g_post = pre_g[i:i + 1], post_g[i:i + 1]
        h = _rmsnorm_fwd(xs, g_pre)
        st = {"x": xs, "h": h}
        if i % 2 == 0:
            proj = _mm(h, w["attn_w_in"][j], name="attn_in")
            res = [_attn_fwd(proj, cos, sin, g, Bl, S) for g in range(N_GROUPS)]
            a, o, lse = _attn_combine([r[0] for r in res], [r[1] for r in res], proj)
            w_out = w["attn_w_out"][j]
            st.update(proj=proj, a=a, o=o, lse=lse, qkv=[r[2] for r in res])
        else:
            w_ab, w_z = w["conv_w_in"][j][:, :2 * D_MODEL], w["conv_w_in"][j][:, 2 * D_MODEL:]
            ab = _mm(h, w_ab, out_dtype=BF16, name="conv_in_ab")
            z = _mm(h, w_z, out_dtype=BF16, name="conv_in_z")
            dw = jnp.pad(w["conv_dw_w"][j], ((0, 1), (0, 0)))
            u1, a = _conv_fwd(ab, z, dw, w["conv_dw_b"][j:j + 1], w["conv_ln_g"][j:j + 1],
                              w["conv_ln_b"][j:j + 1], Bl, S)
            w_out = w["conv_w_out"][j]
            st.update(w_ab=w_ab, w_z=w_z, ab=ab, z=z, dw=dw, u1=u1, a=a)
        y, x1 = _mm_rows(a, w_out, [xs, g_post], _post_epilogue, (F32, F32), "branch_out_post")
        pi = p[i].reshape(T, PLE_DIM)
        pe = _mm(pi, w["ple_w_proj"][i], name="ple_proj")
        gl, xs = _mm_rows(x1, w["ple_w_gate"][i], [pe], _ple_epilogue, (F32, F32), "ple_gate_fwd")
        st.update(y=y, x1=x1, pi=pi, pe=pe, gl=gl)
        saved.append(st)

    sq, dx = _loss_fwd_bwd(xs, loss_target.reshape(T, D_MODEL))

    grads = {n: [None] * shape[0] for n, shape, _ in _LAYOUT}
    for i in reversed(range(DEPTH)):
        j = i // 2
        st = saved[i]
        g_pre, g_post = pre_g[i:i + 1], post_g[i:i + 1]
        dpe, dgl, dx1, dy, dg_post = _ple_post_bwd(dx, st["pe"], st["gl"], w["ple_w_gate"][i], st["y"], g_post)
        grads["ple_w_proj"][i] = _mm(st["pi"], dpe, ta=True, name="ple_proj_wgrad")
        grads["ple_w_gate"][i] = _mm(st["x1"], dgl, ta=True, name="ple_gate_wgrad")
        grads["post_norm_g"][i] = dg_post[0]
        if i % 2 == 0:
            grads["attn_w_out"][j] = _mm(st["a"], dy, ta=True, name="attn_out_wgrad")
            da = _mm(dy, w["attn_w_out"][j], tb=True, name="attn_out_dgrad")
            do, dproj = _gate_bwd(da, st["o"], st["proj"])
            for g in range(N_GROUPS):
                dproj = _attn_bwd(st["qkv"][g], cos, sin, do, st["o"], st["lse"], dproj, g, Bl, S)
            dx, dg_pre = _dgrad_pre_bwd(dproj, w["attn_w_in"][j], st["x"], g_pre, dx1, None, 2048, "attn_in_dgrad_pre")
            grads["attn_w_in"][j] = _mm(st["h"], dproj, ta=True, name="attn_in_wgrad")
        else:
            grads["conv_w_out"][j] = _mm(st["a"], dy, ta=True, name="conv_out_wgrad")
            da2 = _mm(dy, w["conv_w_out"][j], tb=True, name="conv_out_dgrad")
            du1, dz, dln_g, dln_b = _conv_norm_bwd(da2, st["z"], st["u1"], w["conv_ln_g"][j:j + 1],
                                                   w["conv_ln_b"][j:j + 1])
            dab, ddw, ddb = _conv_bwd(st["ab"], du1, st["dw"], Bl, S)
            dh = _mm(dz, st["w_z"], tb=True, name="conv_in_z_dgrad")
            dx, dg_pre = _dgrad_pre_bwd(dab, st["w_ab"], st["x"], g_pre, dx1, dh, 2048, "conv_in_dgrad_pre")
            dw_ab = _mm(st["h"], dab, ta=True, name="conv_in_ab_wgrad")
            dw_z = _mm(st["h"], dz, ta=True, name="conv_in_z_wgrad")
            grads["conv_w_in"][j] = jnp.concatenate([dw_ab, dw_z], axis=1)
            grads["conv_dw_w"][j] = ddw[:CONV_WIDTH]
            grads["conv_dw_b"][j] = ddb[0]
            grads["conv_ln_g"][j] = dln_g[0]
            grads["conv_ln_b"][j] = dln_b[0]
        grads["pre_norm_g"][i] = dg_pre[0]
    grads = {n: jnp.stack(v) for n, v in grads.items()}
    return sq, dx.reshape(Bl, S, D_MODEL), grads


_NAMES = tuple(n for n, _, _ in _LAYOUT)


def kernel(x, p, positions, pre_norm_g, post_norm_g, attn_w_in, attn_w_out, conv_w_in, conv_dw_w, conv_dw_b, conv_ln_g, conv_ln_b, conv_w_out, ple_w_proj, ple_w_gate, loss_target, m_pre_norm_g, m_post_norm_g, m_attn_w_in, m_attn_w_out, m_conv_w_in, m_conv_dw_w, m_conv_dw_b, m_conv_ln_g, m_conv_ln_b, m_conv_w_out, m_ple_w_proj, m_ple_w_gate, v_pre_norm_g, v_post_norm_g, v_attn_w_in, v_attn_w_out, v_conv_w_in, v_conv_dw_w, v_conv_dw_b, v_conv_ln_g, v_conv_ln_b, v_conv_w_out, v_ple_w_proj, v_ple_w_gate):
    w_loc = dict(zip(_NAMES, (pre_norm_g, post_norm_g, attn_w_in, attn_w_out, conv_w_in, conv_dw_w, conv_dw_b,
                              conv_ln_g, conv_ln_b, conv_w_out, ple_w_proj, ple_w_gate)))
    m_loc = dict(zip(_NAMES, (m_pre_norm_g, m_post_norm_g, m_attn_w_in, m_attn_w_out, m_conv_w_in, m_conv_dw_w,
                              m_conv_dw_b, m_conv_ln_g, m_conv_ln_b, m_conv_w_out, m_ple_w_proj, m_ple_w_gate)))
    v_loc = dict(zip(_NAMES, (v_pre_norm_g, v_post_norm_g, v_attn_w_in, v_attn_w_out, v_conv_w_in, v_conv_dw_w,
                              v_conv_dw_b, v_conv_ln_g, v_conv_ln_b, v_conv_w_out, v_ple_w_proj, v_ple_w_gate)))

    gathered, small = _allgather_weights([w_loc[n].astype(BF16) for n in _MATMUL_WEIGHTS],
                                         [_AXIS[n] for n in _MATMUL_WEIGHTS], _stack_small(w_loc))
    w_full = dict(zip(_MATMUL_WEIGHTS, gathered), **_unstack_small(small))
    sq, grad_x, grads = _local_step(x, p, positions, loss_target, pre_norm_g, post_norm_g, w_full)
    loss = lax.psum(sq[0, 0] * (0.5 / D_MODEL), ("x", "y", "c"))

    g_flat = _reduce_scatter_grads(_pack_full_grads(grads))
    g_out = _unpack_f32(g_flat)
    updates = {n: _adamw(w_loc[n], g_out[n], m_loc[n], v_loc[n]) for n in _NAMES}
    d_out, m_out, v_out = ({n: updates[n][k] for n in _NAMES} for k in range(3))
    return (loss, grad_x, *[g_out[n] for n in _NAMES], *[d_out[n] for n in _NAMES],
            *[m_out[n] for n in _NAMES], *[v_out[n] for n in _NAMES])
```

```python
import jax
import jax.numpy as jnp
from jax import lax
from jax.experimental import pallas as pl
from jax.experimental.pallas import tpu as pltpu

F32 = jnp.float32
BF16 = jnp.bfloat16

D_MODEL = 1024
DEPTH = 4
PLE_DIM = 256
HEAD_DIM = 64
WIN_DIL = ((128, 1), (512, 4), (2048, 16))
N_GROUPS = 3
N_BACK = 128
BLOCK_UNROLL = 8
ROPE_THETA = 10000.0
CONV_WIDTH = 31
CONV_HALO = 32
RMS_EPS = 1e-6
LN_EPS = 1e-5
NEG_INF = -1e30
ADAM_LR, ADAM_B1, ADAM_B2, ADAM_EPS, ADAM_WD, ADAM_STEP = 0.001, 0.9, 0.999, 1e-08, 0.01, 10

LANES = 128
N_CHIPS = 4
VMEM_LIMIT = 48 * 1024 * 1024
VMEM_LIMIT_ATTN = 56 * 1024 * 1024
FLAT_COLS = 256
FLAT_ROWS = 36864
FLAT_TILE = 2048
FLAT_ROW_ALIGN = 16
PROJ_COLS = (3 * N_GROUPS + 1) * D_MODEL
HEAD_PAIRS = D_MODEL // LANES

MESH = pl.DeviceIdType.MESH


def _params(sem=None, vmem=VMEM_LIMIT):
    return pltpu.CompilerParams(dimension_semantics=sem, vmem_limit_bytes=vmem)


def _sigmoid(v):
    return 1.0 / (1.0 + jnp.exp(-v))


def _mm(a, b, *, ta=False, tb=False, out_dtype=F32, tm=1024, tn=1024, tk=1024, name="mm"):
    if ta:
        K, M = a.shape
    else:
        M, K = a.shape
    if tb:
        N, K2 = b.shape
    else:
        K2, N = b.shape
    assert K == K2, (a.shape, b.shape)
    tm, tn, tk = min(tm, M), min(tn, N), min(tk, K)
    assert M % tm == 0 and N % tn == 0 and K % tk == 0
    nk = K // tk
    dims = (((0 if ta else 1,), (1 if tb else 0,)), ((), ()))

    def body(a_ref, b_ref, o_ref, *scratch):
        k = pl.program_id(2)
        part = lax.dot_general(a_ref[...].astype(BF16), b_ref[...].astype(BF16), dims, preferred_element_type=F32)
        if nk == 1:
            o_ref[...] = part.astype(out_dtype)
        else:
            acc_ref, = scratch

            @pl.when(k == 0)
            def _():
                acc_ref[...] = part

            @pl.when((k > 0) & (k < nk - 1))
            def _():
                acc_ref[...] += part

            @pl.when(k == nk - 1)
            def _():
                o_ref[...] = (acc_ref[...] + part).astype(out_dtype)

    a_spec = pl.BlockSpec((tk, tm), lambda i, j, k: (k, i)) if ta else pl.BlockSpec((tm, tk), lambda i, j, k: (i, k))
    b_spec = pl.BlockSpec((tn, tk), lambda i, j, k: (j, k)) if tb else pl.BlockSpec((tk, tn), lambda i, j, k: (k, j))
    return pl.pallas_call(
        body, name=name, grid=(M // tm, N // tn, nk),
        in_specs=[a_spec, b_spec], out_specs=pl.BlockSpec((tm, tn), lambda i, j, k: (i, j)),
        out_shape=jax.ShapeDtypeStruct((M, N), out_dtype),
        scratch_shapes=[pltpu.VMEM((tm, tn), F32)] if nk > 1 else [],
        compiler_params=_params(("parallel", "parallel", "arbitrary")),
    )(a, b)


def _mm_rows(a, b, extras, epilogue, out_dtypes, name, tm=512):
    M, K = a.shape
    N = b.shape[1]
    n_ex = len(extras)

    def body(*refs):
        a_ref, b_ref = refs[:2]
        av = a_ref[...]
        acc = jnp.dot(av.astype(BF16), b_ref[...].astype(BF16), preferred_element_type=F32)
        results = epilogue(acc, av, *[e[...] for e in refs[2:2 + n_ex]])
        for o_ref, r in zip(refs[2 + n_ex:], results):
            o_ref[...] = r.astype(o_ref.dtype)

    tile = pl.BlockSpec((tm, N), lambda i: (i, 0))
    in_specs = [pl.BlockSpec((tm, K), lambda i: (i, 0)), pl.BlockSpec((K, N), lambda i: (0, 0))]
    in_specs += [tile if e.shape[0] == M else pl.BlockSpec((1, N), lambda i: (0, 0)) for e in extras]
    return pl.pallas_call(
        body, name=name, grid=(M // tm,), in_specs=in_specs, out_specs=[tile] * len(out_dtypes),
        out_shape=[jax.ShapeDtypeStruct((M, N), dt) for dt in out_dtypes],
        compiler_params=_params(("parallel",)),
    )(a, b, *extras)


ROW_TILE = 512


def _rows(w=D_MODEL, cb=0, tr=ROW_TILE):
    return pl.BlockSpec((tr, w), lambda i: (i, cb))


def _full(shape):
    return pl.BlockSpec(shape, lambda i: (0,) * len(shape))


def _row_call(body, name, T, in_specs, out_specs, out_shape, args, tr=ROW_TILE):
    return pl.pallas_call(body, name=name, grid=(T // tr,), in_specs=in_specs, out_specs=out_specs,
                          out_shape=out_shape, compiler_params=_params(("arbitrary",)))(*args)


def _sds(shape, dtype):
    return jax.ShapeDtypeStruct(shape, dtype)


def _rmsnorm_fwd(x, g):
    T = x.shape[0]

    def body(x_ref, g_ref, h_ref):
        xv = x_ref[...]
        r = lax.rsqrt(jnp.mean(xv * xv, axis=1, keepdims=True) + RMS_EPS)
        h_ref[...] = (xv * r * g_ref[...]).astype(BF16)

    return _row_call(body, "rmsnorm_fwd", T, [_rows(), _full((1, D_MODEL))], _rows(),
                     _sds((T, D_MODEL), BF16), (x, g))


def _post_epilogue(y, a_tile, x, g):
    del a_tile
    return y, x + y * lax.rsqrt(jnp.mean(y * y, axis=1, keepdims=True) + RMS_EPS) * g


def _ple_epilogue(gl, x1, pe):
    return gl, x1 + pe * _sigmoid(gl)


def _dgrad_pre_bwd(a, b, x, g, dx1, add, tk, name, tm=512):
    M, K = a.shape
    N = b.shape[0]
    tk = min(tk, K)
    assert M % tm == 0 and K % tk == 0 and N == D_MODEL
    nk = K // tk

    def body(*refs):
        a_ref, b_ref, x_ref, g_ref, dx1_ref = refs[:5]
        add_ref = refs[5] if add is not None else None
        dx_ref, dg_ref = refs[-3:-1] if nk > 1 else refs[-2:]
        i, k = pl.program_id(0), pl.program_id(1)

        @pl.when((i == 0) & (k == 0))
        def _():
            dg_ref[...] = jnp.zeros_like(dg_ref)

        part = lax.dot_general(a_ref[...].astype(BF16), b_ref[...].astype(BF16), _NT, preferred_element_type=F32)

        def finish(dh):
            if add is not None:
                dh = dh + add_ref[...]
            xv = x_ref[...]
            r = lax.rsqrt(jnp.mean(xv * xv, axis=1, keepdims=True) + RMS_EPS)
            xh = xv * r
            dg_ref[...] += jnp.sum(dh * xh, axis=0, keepdims=True)
            dn = dh * g_ref[...]
            dx_ref[...] = dx1_ref[...] + r * (dn - xh * jnp.mean(dn * xh, axis=1, keepdims=True))

        if nk == 1:
            finish(part)
        else:
            acc_ref = refs[-1]

            @pl.when(k == 0)
            def _():
                acc_ref[...] = part

            @pl.when((k > 0) & (k < nk - 1))
            def _():
                acc_ref[...] += part

            @pl.when(k == nk - 1)
            def _():
                finish(acc_ref[...] + part)

    tile = pl.BlockSpec((tm, N), lambda i, k: (i, 0))
    row = pl.BlockSpec((1, N), lambda i, k: (0, 0))
    in_specs = [pl.BlockSpec((tm, tk), lambda i, k: (i, k)), pl.BlockSpec((N, tk), lambda i, k: (0, k)), tile, row, tile]
    args = [a, b, x, g, dx1]
    if add is not None:
        in_specs.append(tile)
        args.append(add)
    return pl.pallas_call(
        body, name=name, grid=(M // tm, nk), in_specs=in_specs, out_specs=[tile, row],
        out_shape=[_sds((M, N), F32), _sds((1, N), F32)],
        scratch_shapes=[pltpu.VMEM((tm, N), F32)] if nk > 1 else [],
        compiler_params=_params(("arbitrary", "arbitrary")),
    )(*args)


def _ple_post_bwd(dx2, pe, gl, w_gate, y, g_post):
    T = dx2.shape[0]

    def body(d_ref, pe_ref, gl_ref, w_ref, y_ref, g_ref, dpe_ref, dgl_ref, dx1_ref, dy_ref, dg_ref):
        @pl.when(pl.program_id(0) == 0)
        def _():
            dg_ref[...] = jnp.zeros_like(dg_ref)

        dv = d_ref[...]
        sg = _sigmoid(gl_ref[...])
        dpe_ref[...] = (dv * sg).astype(BF16)
        dgl = (dv * pe_ref[...] * sg * (1.0 - sg)).astype(BF16)
        dgl_ref[...] = dgl
        dx1 = dv + lax.dot_general(dgl, w_ref[...], _NT, preferred_element_type=F32)
        dx1_ref[...] = dx1
        yv = y_ref[...]
        r = lax.rsqrt(jnp.mean(yv * yv, axis=1, keepdims=True) + RMS_EPS)
        yh = yv * r
        dg_ref[...] += jnp.sum(dx1 * yh, axis=0, keepdims=True)
        dn = dx1 * g_ref[...]
        dy_ref[...] = (r * (dn - yh * jnp.mean(dn * yh, axis=1, keepdims=True))).astype(BF16)

    row = _full((1, D_MODEL))
    return _row_call(body, "ple_post_bwd", T, [_rows()] * 3 + [_full((D_MODEL, D_MODEL)), _rows(), row],
                     [_rows()] * 4 + [row],
                     [_sds((T, D_MODEL), BF16)] * 2 + [_sds((T, D_MODEL), F32), _sds((T, D_MODEL), BF16),
                                                      _sds((1, D_MODEL), F32)],
                     (dx2, pe, gl, w_gate, y, g_post))


def _loss_fwd_bwd(y, target):
    T = y.shape[0]

    def body(y_ref, t_ref, s_ref, d_ref):
        @pl.when(pl.program_id(0) == 0)
        def _():
            s_ref[...] = jnp.zeros_like(s_ref)

        e = y_ref[...] - t_ref[...]
        s_ref[...] += jnp.sum(e * e).reshape(1, 1)
        d_ref[...] = e * (1.0 / D_MODEL)

    return _row_call(body, "loss", T, [_rows()] * 2, [_full((1, 1)), _rows()],
                     [_sds((1, 1), F32), _sds((T, D_MODEL), F32)], (y, target))


def _attn_combine(outs, lses, proj):
    T = proj.shape[0]

    def body(o0, o1, o2, l0, l1, l2, z_ref, a_ref, o_ref, lse_ref):
        a0, a1, a2 = l0[...], l1[...], l2[...]
        m = jnp.maximum(jnp.maximum(a0, a1), a2)
        e0, e1, e2 = jnp.exp(a0 - m), jnp.exp(a1 - m), jnp.exp(a2 - m)
        ssum = e0 + e1 + e2
        o = (e0 * o0[...] + e1 * o1[...] + e2 * o2[...]) / ssum
        zv = z_ref[...].astype(F32)
        o_ref[...] = o
        lse_ref[...] = m + jnp.log(ssum)
        a_ref[...] = (o * zv * _sigmoid(zv)).astype(BF16)

    return _row_call(body, "attn_combine", T, [_rows()] * 6 + [_rows(cb=3 * N_GROUPS)], [_rows()] * 3,
                     [_sds((T, D_MODEL), BF16), _sds((T, D_MODEL), F32), _sds((T, D_MODEL), F32)],
                     (*outs, *lses, proj))


def _gate_bwd(da, o, proj):
    T = da.shape[0]

    def body(da_ref, o_ref, z_ref, do_ref, dz_ref):
        dv = da_ref[...]
        zv = z_ref[...]
        sg = _sigmoid(zv)
        do_ref[...] = dv * zv * sg
        dz_ref[...] = dv * o_ref[...] * sg * (1.0 + zv * (1.0 - sg))

    zcols = _rows(cb=3 * N_GROUPS)
    return _row_call(body, "gate_bwd", T, [_rows(), _rows(), zcols], [_rows(), zcols],
                     [_sds((T, D_MODEL), F32), _sds((T, PROJ_COLS), F32)], (da, o, proj))


def _rope_tables(positions):
    inv_freq = 1.0 / (ROPE_THETA ** (jnp.arange(0, HEAD_DIM, 2, dtype=F32) / HEAD_DIM))
    ang = positions.astype(F32)[..., None] * inv_freq
    cos, sin = jnp.cos(ang), jnp.sin(ang)
    return jnp.tile(cos, (1, 1, 4)), jnp.concatenate([-sin, sin, -sin, sin], axis=-1)


def _rotate_half_partner(t):
    lane = lax.broadcasted_iota(jnp.int32, t.shape, 1)
    return jnp.where((lane % HEAD_DIM) < HEAD_DIM // 2,
                     pltpu.roll(t, LANES - HEAD_DIM // 2, 1), pltpu.roll(t, HEAD_DIM // 2, 1))


def _mask_bias(first):
    qi = lax.broadcasted_iota(jnp.int32, (N_BACK, 2 * N_BACK), 0)
    kj = lax.broadcasted_iota(jnp.int32, (N_BACK, 2 * N_BACK), 1)
    ok = (kj >= qi) & (kj <= qi + N_BACK)
    if first:
        ok = ok & (kj >= N_BACK)
    return jnp.where(ok, 0.0, NEG_INF).astype(F32)


def _stack_heads(t, head0):
    zero = jnp.zeros_like(t)
    return jnp.concatenate([jnp.where(head0, t, zero), jnp.where(head0, zero, t)], axis=0)


def _unstack_heads(t2, head0):
    return jnp.where(head0, t2[:N_BACK], t2[N_BACK:])


def _block_loop(nb, block):
    first, rest = _mask_bias(True), _mask_bias(False)
    first, rest = jnp.concatenate([first, first], axis=0), jnp.concatenate([rest, rest], axis=0)
    if nb <= BLOCK_UNROLL:
        for n in range(nb):
            block(n, first if n == 0 else rest)
        return

    def step(n, carry):
        block(n, jnp.where(n == 0, first, rest))
        return carry

    lax.fori_loop(0, nb, step, 0, unroll=BLOCK_UNROLL)


def _for(count, body, unroll_fully):
    if unroll_fully:
        for i in range(count):
            body(i)
    else:
        lax.fori_loop(0, count, lambda i, carry: (body(i), carry)[1], 0)


def _residues_together(nb):
    return max(1, BLOCK_UNROLL // nb)


def _residue_loop(d, nb, residue):
    together = _residues_together(nb)
    assert d % together == 0

    def group(i, carry):
        for u in range(together):
            residue(i * together + u, u)
        return carry

    lax.fori_loop(0, d // together, group, 0)


_NT = (((1,), (1,)), ((), ()))
_TN = (((0,), (0,)), ((), ()))


def _residue_rows(r, i, d):
    start = r + i * (N_BACK * d)
    if d == 1:
        return pl.ds(pl.multiple_of(start, N_BACK), N_BACK)
    return pl.ds(start, N_BACK, stride=d)


def _seq_rows(i):
    return pl.ds(pl.multiple_of(i * N_BACK, N_BACK), N_BACK)


def _rows_at(base, i, size=N_BACK):
    return pl.ds(pl.multiple_of(base + i * N_BACK, N_BACK), size)


def _attn_fwd(proj, cos, sin, group, Bl, S):
    d = WIN_DIL[group][1]
    L = S // d
    nb = L // N_BACK
    P = L + N_BACK
    assert WIN_DIL[group][0] // d == N_BACK and L % N_BACK == 0

    def body(q_ref, k_ref, v_ref, cos_ref, sin_ref, o_ref, lse_ref, qr, kr, vp):
        head0 = lax.broadcasted_iota(jnp.int32, (1, LANES), 1) < HEAD_DIM
        zeros = jnp.zeros((N_BACK, LANES), BF16)

        def residue(r, u):
            del u
            qbase, kbase = r * L, r * P
            kr[_rows_at(kbase, 0), :] = zeros
            vp[_rows_at(kbase, 0), :] = zeros

            def rope(i):
                rows = _residue_rows(r, i, d)
                cs, sn = cos_ref[rows, :], sin_ref[rows, :]
                q, k = q_ref[rows, :], k_ref[rows, :]
                qr[_rows_at(qbase, i), :] = ((q * cs + _rotate_half_partner(q) * sn)
                                            * (HEAD_DIM ** -0.5)).astype(BF16)
                kr[_rows_at(kbase, i + 1), :] = (k * cs + _rotate_half_partner(k) * sn).astype(BF16)
                vp[_rows_at(kbase, i + 1), :] = v_ref[rows, :].astype(BF16)

            _for(nb, rope, nb <= BLOCK_UNROLL)

            def block(n, bias):
                win = _rows_at(kbase, n, 2 * N_BACK)
                q2, kw, vw = _stack_heads(qr[_rows_at(qbase, n), :], head0), kr[win, :], vp[win, :]
                s = lax.dot_general(q2, kw, _NT, preferred_element_type=F32) + bias
                m = jnp.max(s, axis=1, keepdims=True)
                p = jnp.exp(s - m)
                l = jnp.sum(p, axis=1, keepdims=True)
                pv = jnp.dot(p.astype(BF16), vw, preferred_element_type=F32)
                rows = _residue_rows(r, n, d)
                o_ref[rows, :] = _unstack_heads(pv * (1.0 / l), head0)
                lse_ref[rows, :] = _unstack_heads((m + jnp.log(l)) + jnp.zeros((2 * N_BACK, LANES), F32), head0)

            _block_loop(nb, block)

        _residue_loop(d, nb, residue)

    act = pl.BlockSpec((None, S, LANES), lambda b, hp: (b, 0, hp))
    tab = pl.BlockSpec((None, S, LANES), lambda b, hp: (b, 0, 0))
    col = lambda which: pl.BlockSpec((None, S, LANES),
                                     lambda b, hp: (b, 0, (which * N_GROUPS + group) * HEAD_PAIRS + hp))
    seq = lambda rows: pl.BlockSpec((None, None, rows, LANES), lambda b, hp: (b, hp, 0, 0))
    p3 = proj.reshape(Bl, S, PROJ_COLS)
    o, lse, qr, kr, vp = pl.pallas_call(
        body, name="attn_fwd_g%d" % group, grid=(Bl, HEAD_PAIRS),
        in_specs=[col(0), col(1), col(2), tab, tab], out_specs=[act, act, seq(S), seq(d * P), seq(d * P)],
        out_shape=[_sds((Bl, S, D_MODEL), F32)] * 2 + [_sds((Bl, HEAD_PAIRS, S, LANES), BF16)]
        + [_sds((Bl, HEAD_PAIRS, d * P, LANES), BF16)] * 2,
        compiler_params=_params(("parallel", "arbitrary"), VMEM_LIMIT_ATTN),
    )(p3, p3, p3, cos, sin)
    return o.reshape(Bl * S, D_MODEL), lse.reshape(Bl * S, D_MODEL), (qr, kr, vp)


def _attn_bwd(saved, cos, sin, do, o, lse, dproj, group, Bl, S):
    d = WIN_DIL[group][1]
    L = S // d
    nb = L // N_BACK
    P = L + N_BACK
    steps = Bl * HEAD_PAIRS

    def body(qr, kr, vp, cos_ref, sin_ref, do_ref, o_ref, lse_ref, dproj_in, dproj_ref,
             dk_accs, dv_accs, stage, sems):
        del dproj_in
        head0 = lax.broadcasted_iota(jnp.int32, (1, LANES), 1) < HEAD_DIM
        b, hp = pl.program_id(0), pl.program_id(1)
        step = b * HEAD_PAIRS + hp
        slot = step % 2
        dq_s, dk_s, dv_s = stage.at[slot, 0], stage.at[slot, 1], stage.at[slot, 2]

        def copies(which_slot):
            out = []
            for which in range(3):
                col = ((which * N_GROUPS + group) * HEAD_PAIRS + hp) * LANES
                out.append(pltpu.make_async_copy(
                    stage.at[which_slot, which], dproj_ref.at[b, :, pl.ds(pl.multiple_of(col, LANES), LANES)],
                    sems.at[which_slot, which]))
            return out

        @pl.when(step >= 2)
        def _():
            for cp in copies(slot):
                cp.wait()

        def residue(r, u):
            qbase, kbase = r * L, r * P
            dk_acc, dv_acc = dk_accs.at[u], dv_accs.at[u]
            dk_acc[...] = jnp.zeros_like(dk_acc)
            dv_acc[...] = jnp.zeros_like(dv_acc)

            def block(n, bias):
                win = pl.ds(pl.multiple_of(n * N_BACK, N_BACK), 2 * N_BACK)
                kwin = _rows_at(kbase, n, 2 * N_BACK)
                rows = _residue_rows(r, n, d)
                q2, kw, vw = _stack_heads(qr[_rows_at(qbase, n), :], head0), kr[kwin, :], vp[kwin, :]
                dof = do_ref[rows, :]
                do2 = _stack_heads(dof.astype(BF16), head0)
                lse_b = lse_ref[rows, :]
                lse2 = jnp.concatenate([lse_b[:, 0:1], lse_b[:, HEAD_DIM:HEAD_DIM + 1]], axis=0)
                dsum = _stack_heads(dof * o_ref[rows, :], head0)
                delta = jnp.sum(dsum, axis=1, keepdims=True)
                s = lax.dot_general(q2, kw, _NT, preferred_element_type=F32) + bias
                p = jnp.exp(s - lse2)
                dp = lax.dot_general(do2, vw, _NT, preferred_element_type=F32)
                ds = (p * (dp - delta)).astype(BF16)
                dq = _unstack_heads(jnp.dot(ds, kw, preferred_element_type=F32), head0) * (HEAD_DIM ** -0.5)
                cs, sn = cos_ref[rows, :], sin_ref[rows, :]
                dq_s[rows, :] = dq * cs + _rotate_half_partner(dq * sn)
                dk_acc[win, :] += lax.dot_general(ds, q2, _TN, preferred_element_type=F32)
                dv_acc[win, :] += lax.dot_general(p.astype(BF16), do2, _TN, preferred_element_type=F32)

            _block_loop(nb, block)

            def finish(i):
                rows = _residue_rows(r, i, d)
                cs, sn = cos_ref[rows, :], sin_ref[rows, :]
                dk = dk_acc[_seq_rows(i + 1), :]
                dk_s[rows, :] = dk * cs + _rotate_half_partner(dk * sn)
                dv_s[rows, :] = dv_acc[_seq_rows(i + 1), :]

            _for(nb, finish, nb <= BLOCK_UNROLL)

        _residue_loop(d, nb, residue)
        for cp in copies(slot):
            cp.start()

        @pl.when(step == steps - 1)
        def _():
            if steps > 1:
                for cp in copies(1 - slot):
                    cp.wait()
            for cp in copies(slot):
                cp.wait()

    act = pl.BlockSpec((None, S, LANES), lambda b, hp: (b, 0, hp))
    tab = pl.BlockSpec((None, S, LANES), lambda b, hp: (b, 0, 0))
    seq = lambda rows: pl.BlockSpec((None, None, rows, LANES), lambda b, hp: (b, hp, 0, 0))
    view = lambda t: t.reshape(Bl, S, D_MODEL)
    out = pl.pallas_call(
        body, name="attn_bwd_g%d" % group, grid=(Bl, HEAD_PAIRS),
        in_specs=[seq(S), seq(d * P), seq(d * P), tab, tab, act, act, act, _ANY], out_specs=_ANY,
        out_shape=_sds((Bl, S, PROJ_COLS), F32), input_output_aliases={8: 0},
        scratch_shapes=[pltpu.VMEM((_residues_together(nb), P, LANES), F32),
                        pltpu.VMEM((_residues_together(nb), P, LANES), F32),
                        pltpu.VMEM((2, 3, S, LANES), F32), pltpu.SemaphoreType.DMA((2, 3))],
        compiler_params=_params(("arbitrary", "arbitrary"), VMEM_LIMIT_ATTN),
    )(*saved, cos, sin, view(do), view(o), view(lse), dproj.reshape(Bl, S, PROJ_COLS))
    return out.reshape(Bl * S, PROJ_COLS)


CONV_TILE = 256
CONV_CHUNK = 64
SUBLANES = 8
CONV_SHIFT_ROWS = CONV_TILE + CONV_HALO - SUBLANES


def _fill_shifted(shifted, ext, cs):
    for k in range(1, SUBLANES):
        shifted[k - 1] = ext[pl.ds(k, CONV_SHIFT_ROWS), cs]


def _shifted_rows(shifted, ext, cs, off):
    k = off % SUBLANES
    if k == 0:
        return ext[pl.ds(off, CONV_CHUNK), cs]
    return shifted[k - 1, pl.ds(off - k, CONV_CHUNK), :]


def _conv_fwd(proj, z, dw, dwb, ln_g, ln_b, Bl, S):
    tr = CONV_TILE
    nj = S // tr
    hb = tr // CONV_HALO

    def body(a_ref, b_ref, ah_ref, bh_ref, z_ref, dw_ref, dwb_ref, g_ref, bb_ref, u1_ref, out_ref, ext, shifted):
        j = pl.program_id(1)
        halo = ah_ref[0].astype(F32) * _sigmoid(bh_ref[0].astype(F32))
        ext[pl.ds(0, CONV_HALO), :] = jnp.where(j > 0, halo, 0.0)
        ext[pl.ds(CONV_HALO, tr), :] = a_ref[0].astype(F32) * _sigmoid(b_ref[0].astype(F32))

        def cols(c, carry):
            cs = pl.ds(pl.multiple_of(c * LANES, LANES), LANES)
            _fill_shifted(shifted, ext, cs)
            for rc in range(tr // CONV_CHUNK):
                acc = jnp.zeros((CONV_CHUNK, LANES), F32)
                for w in range(CONV_WIDTH):
                    off = rc * CONV_CHUNK + CONV_HALO - (CONV_WIDTH - 1) + w
                    acc = acc + dw_ref[pl.ds(w, 1), cs] * _shifted_rows(shifted, ext, cs, off)
                u1_ref[0, pl.ds(rc * CONV_CHUNK, CONV_CHUNK), cs] = acc + dwb_ref[:, cs]
            return carry

        lax.fori_loop(0, D_MODEL // LANES, cols, 0)
        u1 = u1_ref[0]
        mu = jnp.mean(u1, axis=1, keepdims=True)
        xc = u1 - mu
        rstd = lax.rsqrt(jnp.mean(xc * xc, axis=1, keepdims=True) + LN_EPS)
        u2 = xc * rstd * g_ref[...] + bb_ref[...]
        zv = z_ref[0].astype(F32)
        out_ref[0] = (u2 * _sigmoid(u2) * zv * _sigmoid(zv)).astype(BF16)

    tile = lambda cb: pl.BlockSpec((1, tr, D_MODEL), lambda b, j: (b, j, cb))
    halo = lambda cb: pl.BlockSpec((1, CONV_HALO, D_MODEL), lambda b, j: (b, jnp.maximum(j * hb - 1, 0), cb))
    par = lambda r: pl.BlockSpec((r, D_MODEL), lambda b, j: (0, 0))
    p3 = proj.reshape(Bl, S, 2 * D_MODEL)
    u1, out = pl.pallas_call(
        body, name="conv_fwd", grid=(Bl, nj),
        in_specs=[tile(0), tile(1), halo(0), halo(1), tile(0), par(32), par(1), par(1), par(1)],
        out_specs=[tile(0), tile(0)],
        out_shape=[_sds((Bl, S, D_MODEL), F32), _sds((Bl, S, D_MODEL), BF16)],
        scratch_shapes=[pltpu.VMEM((tr + CONV_HALO, D_MODEL), F32),
                        pltpu.VMEM((SUBLANES - 1, CONV_SHIFT_ROWS, LANES), F32)],
        compiler_params=_params(("parallel", "arbitrary")),
    )(p3, p3, p3, p3, z.reshape(Bl, S, D_MODEL), dw, dwb, ln_g, ln_b)
    return u1.reshape(Bl * S, D_MODEL), out.reshape(Bl * S, D_MODEL)


def _conv_norm_bwd(da2, z, u1, ln_g, ln_b):
    T = da2.shape[0]

    def body(da_ref, z_ref, u_ref, g_ref, b_ref, du_ref, dz_ref, dg_ref, db_ref):
        @pl.when(pl.program_id(0) == 0)
        def _():
            dg_ref[...] = jnp.zeros_like(dg_ref)
            db_ref[...] = jnp.zeros_like(db_ref)

        u1 = u_ref[...]
        mu = jnp.mean(u1, axis=1, keepdims=True)
        xc = u1 - mu
        rstd = lax.rsqrt(jnp.mean(xc * xc, axis=1, keepdims=True) + LN_EPS)
        nrm = xc * rstd
        u2 = nrm * g_ref[...] + b_ref[...]
        s2 = _sigmoid(u2)
        zv = z_ref[...].astype(F32)
        sz = _sigmoid(zv)
        dv = da_ref[...]
        dz_ref[...] = (dv * u2 * s2 * sz * (1.0 + zv * (1.0 - sz))).astype(BF16)
        du2 = dv * zv * sz * s2 * (1.0 + u2 * (1.0 - s2))
        dg_ref[...] += jnp.sum(du2 * nrm, axis=0, keepdims=True)
        db_ref[...] += jnp.sum(du2, axis=0, keepdims=True)
        dn = du2 * g_ref[...]
        du_ref[...] = rstd * (dn - jnp.mean(dn, axis=1, keepdims=True)
                              - nrm * jnp.mean(dn * nrm, axis=1, keepdims=True))

    return _row_call(body, "conv_norm_bwd", T,
                     [_rows(), _rows(), _rows(), _full((1, D_MODEL)), _full((1, D_MODEL))],
                     [_rows(), _rows(), _full((1, D_MODEL)), _full((1, D_MODEL))],
                     [_sds((T, D_MODEL), F32), _sds((T, D_MODEL), BF16), _sds((1, D_MODEL), F32),
                      _sds((1, D_MODEL), F32)], (da2, z, u1, ln_g, ln_b))


def _conv_bwd(proj, du1, dw, Bl, S):
    tr = CONV_TILE
    nj = S // tr
    hb = tr // CONV_HALO

    def body(a_ref, b_ref, ah_ref, bh_ref, du_ref, duh_ref, dw_ref, dab_ref, ddw_ref, ddb_ref, uext, dext, du0,
             ushift, dshift, ddw8):
        first = (pl.program_id(0) == 0) & (pl.program_id(1) == 0)
        last = (pl.program_id(0) == Bl - 1) & (pl.program_id(1) == nj - 1)
        j = pl.program_id(1)

        @pl.when(first)
        def _():
            ddw8[...] = jnp.zeros_like(ddw8)
            ddb_ref[...] = jnp.zeros_like(ddb_ref)

        halo = ah_ref[0].astype(F32) * _sigmoid(bh_ref[0].astype(F32))
        uext[pl.ds(0, CONV_HALO), :] = jnp.where(j > 0, halo, 0.0)
        av = a_ref[0].astype(F32)
        sb = _sigmoid(b_ref[0].astype(F32))
        uext[pl.ds(CONV_HALO, tr), :] = av * sb
        dext[pl.ds(0, tr), :] = du_ref[0]
        dext[pl.ds(tr, CONV_HALO), :] = jnp.where(j < nj - 1, duh_ref[0], 0.0)
        ddb_ref[...] += jnp.sum(du_ref[0], axis=0, keepdims=True)

        def cols(c, carry):
            cs = pl.ds(pl.multiple_of(c * LANES, LANES), LANES)
            _fill_shifted(dshift, dext, cs)
            _fill_shifted(ushift, uext, cs)
            for rc in range(tr // CONV_CHUNK):
                base = rc * CONV_CHUNK
                acc = jnp.zeros((CONV_CHUNK, LANES), F32)
                for w in range(CONV_WIDTH):
                    acc = acc + dw_ref[pl.ds(w, 1), cs] * _shifted_rows(dshift, dext, cs, base + CONV_WIDTH - 1 - w)
                du0[pl.ds(base, CONV_CHUNK), cs] = acc
            for w in range(CONV_WIDTH):
                part = jnp.zeros((SUBLANES, LANES), F32)
                for rc in range(tr // CONV_CHUNK):
                    base = rc * CONV_CHUNK
                    prod = dext[pl.ds(base, CONV_CHUNK), cs] * _shifted_rows(
                        ushift, uext, cs, base + CONV_HALO - (CONV_WIDTH - 1) + w)
                    for i in range(CONV_CHUNK // SUBLANES):
                        part = part + prod[i * SUBLANES:(i + 1) * SUBLANES]
                ddw8[pl.ds(w * SUBLANES, SUBLANES), cs] += part
            return carry

        lax.fori_loop(0, D_MODEL // LANES, cols, 0)
        g = du0[...]
        dab_ref[0, :, 0:D_MODEL] = (g * sb).astype(BF16)
        dab_ref[0, :, D_MODEL:2 * D_MODEL] = (g * av * sb * (1.0 - sb)).astype(BF16)

        @pl.when(last)
        def _():
            for w in range(CONV_WIDTH + 1):
                ddw_ref[pl.ds(w, 1), :] = jnp.sum(ddw8[pl.ds(w * SUBLANES, SUBLANES), :], axis=0, keepdims=True)

    tile = lambda cb: pl.BlockSpec((1, tr, D_MODEL), lambda b, j: (b, j, cb))
    halo = lambda cb: pl.BlockSpec((1, CONV_HALO, D_MODEL), lambda b, j: (b, jnp.maximum(j * hb - 1, 0), cb))
    nxt = pl.BlockSpec((1, CONV_HALO, D_MODEL), lambda b, j: (b, jnp.minimum((j + 1) * hb, S // CONV_HALO - 1), 0))
    par = lambda r: pl.BlockSpec((r, D_MODEL), lambda b, j: (0, 0))
    p3 = proj.reshape(Bl, S, 2 * D_MODEL)
    d3 = du1.reshape(Bl, S, D_MODEL)
    dab, ddw, ddb = pl.pallas_call(
        body, name="conv_bwd", grid=(Bl, nj),
        in_specs=[tile(0), tile(1), halo(0), halo(1), tile(0), nxt, par(32)],
        out_specs=[pl.BlockSpec((1, tr, 2 * D_MODEL), lambda b, j: (b, j, 0)), par(32), par(1)],
        out_shape=[_sds((Bl, S, 2 * D_MODEL), BF16), _sds((32, D_MODEL), F32), _sds((1, D_MODEL), F32)],
        scratch_shapes=[pltpu.VMEM((tr + CONV_HALO, D_MODEL), F32), pltpu.VMEM((tr + CONV_HALO, D_MODEL), F32),
                        pltpu.VMEM((tr, D_MODEL), F32),
                        pltpu.VMEM((SUBLANES - 1, CONV_SHIFT_ROWS, LANES), F32),
                        pltpu.VMEM((SUBLANES - 1, CONV_SHIFT_ROWS, LANES), F32),
                        pltpu.VMEM(((CONV_WIDTH + 1) * SUBLANES, D_MODEL), F32)],
        compiler_params=_params(("arbitrary", "arbitrary")),
    )(p3, p3, p3, p3, d3, d3, dw)
    return dab.reshape(Bl * S, 2 * D_MODEL), ddw, ddb


_LAYOUT = (
    ("pre_norm_g", (4, 1024), None), ("post_norm_g", (4, 1024), None),
    ("attn_w_in", (2, 1024, 2560), 2), ("attn_w_out", (2, 256, 1024), 1),
    ("conv_w_in", (2, 1024, 768), 2), ("conv_dw_w", (2, 31, 256), 2),
    ("conv_dw_b", (2, 256), 1), ("conv_ln_g", (2, 256), 1), ("conv_ln_b", (2, 256), 1),
    ("conv_w_out", (2, 256, 1024), 1), ("ple_w_proj", (4, 256, 256), 2), ("ple_w_gate", (4, 256, 1024), 1),
)
_MATMUL_WEIGHTS = ("attn_w_in", "attn_w_out", "conv_w_in", "conv_w_out", "ple_w_proj", "ple_w_gate")
_AXIS = {n: a for n, _, a in _LAYOUT}


def _size(shape):
    n = 1
    for s in shape:
        n *= s
    return n


def _padded_rows(shape):
    rows = _size(shape) // shape[-1]
    return rows + (-rows) % FLAT_ROW_ALIGN


def _rows2d(a):
    a2 = a.reshape(-1, a.shape[-1])
    pad = _padded_rows(a.shape) - a2.shape[0]
    return jnp.pad(a2, ((0, pad), (0, 0))) if pad else a2


def _col_blocks(a):
    a2 = _rows2d(a)
    return jnp.concatenate([a2[:, c:c + FLAT_COLS] for c in range(0, a2.shape[1], FLAT_COLS)], axis=0)


def _from_col_blocks(flat, off, shape):
    rows, nblk = _padded_rows(shape), shape[-1] // FLAT_COLS
    a2 = jnp.concatenate([flat[off + b * rows:off + (b + 1) * rows] for b in range(nblk)], axis=1)
    return a2[:_size(shape) // shape[-1]].reshape(shape), off + nblk * rows


def _shard_col_blocks(full, shape, axis):
    if axis is None:
        blocks = _col_blocks(full)
        return jnp.broadcast_to(blocks[None], (N_CHIPS,) + blocks.shape)
    m = shape[-1]
    if axis == len(shape) - 1:
        a2 = _rows2d(full)
        pieces = [a2[:, c:c + FLAT_COLS] for c in range(0, N_CHIPS * m, FLAT_COLS)]
    else:
        layers, r, _ = shape
        assert axis == 1 and (layers * r) % FLAT_ROW_ALIGN == 0
        pieces = [full[:, s * r:(s + 1) * r, c:c + FLAT_COLS].reshape(layers * r, FLAT_COLS)
                  for s in range(N_CHIPS) for c in range(0, m, FLAT_COLS)]
    return jnp.concatenate(pieces, axis=0).reshape(N_CHIPS, -1, FLAT_COLS)


_FLAT_BIG = ("attn_w_in", "conv_w_in", "ple_w_gate", "attn_w_out", "conv_w_out", "ple_w_proj")
_FLAT_SMALL = ("pre_norm_g", "post_norm_g", "conv_dw_w", "conv_dw_b", "conv_ln_g", "conv_ln_b")
_SHARD_SHAPE = {n: s for n, s, _ in _LAYOUT}
PACK_TILE = 1024


def _flat_offsets():
    out, off = {}, 0
    for n in _FLAT_BIG + _FLAT_SMALL:
        out[n] = off
        off += _padded_rows(_SHARD_SHAPE[n]) * (_SHARD_SHAPE[n][-1] // FLAT_COLS)
    return out, off


def _unpack_f32(flat):
    offsets, _ = _flat_offsets()
    return {n: _from_col_blocks(flat, offsets[n], _SHARD_SHAPE[n])[0] for n in _FLAT_BIG + _FLAT_SMALL}


def _pack_param(full, shape, axis, off, flat):
    layers, r, m = shape
    nblk = m // FLAT_COLS
    if axis == 2:
        rows = layers * r
        tr = min(PACK_TILE, rows)
        assert rows % tr == 0 and off % tr == 0
        src = full.reshape(rows, N_CHIPS * m)
        grid = (N_CHIPS * nblk, rows // tr)
        in_spec = pl.BlockSpec((tr, FLAT_COLS), lambda j, i: (i, j))
        out_spec = pl.BlockSpec((None, tr, FLAT_COLS), lambda j, i: (j // nblk, (off + (j % nblk) * rows) // tr + i, 0))
    else:
        assert axis == 1 and off % r == 0
        src = full.reshape(layers * N_CHIPS * r, m)
        grid = (layers, N_CHIPS, nblk)
        in_spec = pl.BlockSpec((r, FLAT_COLS), lambda l, s, b: (l * N_CHIPS + s, b))
        out_spec = pl.BlockSpec((None, r, FLAT_COLS), lambda l, s, b: (s, (off + b * layers * r) // r + l, 0))

    def copy_body(src_ref, *rest):
        rest[-1][...] = src_ref[...]

    args, in_specs, aliases = [src], [in_spec], {}
    if flat is not None:
        args.append(flat)
        in_specs.append(_ANY)
        aliases = {1: 0}
    return pl.pallas_call(
        copy_body, name="pack_grad", grid=grid, in_specs=in_specs, out_specs=out_spec,
        out_shape=_sds((N_CHIPS, FLAT_ROWS, FLAT_COLS), F32), input_output_aliases=aliases,
        compiler_params=_params(("arbitrary",) * len(grid)))(*args)


SMALL_ROWS = 40


def _stack_small(w):
    rows = [w["conv_dw_w"]] + [w[n][:, None, :] for n in ("conv_dw_b", "conv_ln_g", "conv_ln_b")]
    stacked = jnp.concatenate(rows, axis=1)
    return jnp.pad(stacked, ((0, 0), (0, SMALL_ROWS - stacked.shape[1]), (0, 0)))


def _unstack_small(small):
    return {"conv_dw_w": small[:, :CONV_WIDTH], "conv_dw_b": small[:, CONV_WIDTH],
            "conv_ln_g": small[:, CONV_WIDTH + 1], "conv_ln_b": small[:, CONV_WIDTH + 2]}


def _pack_full_grads(grads):
    offsets, end = _flat_offsets()
    assert end <= FLAT_ROWS
    flat = None
    for n in _FLAT_BIG:
        flat = _pack_param(grads[n], _SHARD_SHAPE[n], _AXIS[n], offsets[n], flat)
    small = jnp.concatenate([_shard_col_blocks(grads[n], _SHARD_SHAPE[n], _AXIS[n]) for n in _FLAT_SMALL], axis=1)
    start = offsets[_FLAT_SMALL[0]]
    small = jnp.pad(small, ((0, 0), (0, FLAT_ROWS - start - small.shape[1]), (0, 0)))
    return lax.dynamic_update_slice(flat, small, (0, start, 0))


_ANY = pl.BlockSpec(memory_space=pl.ANY)


def _mesh_pos():
    return lax.axis_index("x"), lax.axis_index("y"), lax.axis_index("c")


def _other_chips(x, y):
    return [(1 - x, y), (x, 1 - y), (1 - x, 1 - y)]


COPIES_PER_ARRAY = 7


def _allgather_weights(shards, axes, small):
    n = len(shards)
    full_shape = lambda a, axis: tuple(d * (N_CHIPS if i == axis else 1) for i, d in enumerate(a.shape))

    def body(*refs):
        ins, small_in = refs[:n], refs[n]
        outs, small_out = refs[n + 1:2 * n + 1], refs[2 * n + 1]
        send_sems, recv_sems = refs[2 * n + 2:]
        x, y, c = _mesh_pos()
        mine, me, sibling = 2 * x + y, (x, y, c), (x, y, 1 - c)
        chips = _other_chips(x, y)

        def region(a, chip, half):
            _, rows, cols = shards[a].shape
            h = rows // 2
            if axes[a] == 2:
                return outs[a].at[:, slice(None) if half is None else pl.ds(half * h, h), pl.ds(chip * cols, cols)]
            if half is None:
                return outs[a].at[:, pl.ds(chip * rows, rows), :]
            return outs[a].at[:, pl.ds(chip * rows + half * h, h), :]

        def copy(k, src, dst, to):
            return pltpu.make_async_remote_copy(src_ref=src, dst_ref=dst, send_sem=send_sems.at[k],
                                                recv_sem=recv_sems.at[k], device_id=to, device_id_type=MESH)

        def arrival(k, dst):
            return copy(k, dst, dst, me)

        sends = []
        for a in range(n):
            h = shards[a].shape[1] // 2
            base = a * COPIES_PER_ARRAY
            sends.append(copy(base + 6, ins[a], region(a, mine, None), sibling))
            for j, (cx, cy) in enumerate(chips):
                sends.append(copy(base + j, ins[a].at[:, pl.ds(c * h, h), :], region(a, mine, c), (cx, cy, c)))
        small_cols = small.shape[2]
        small_region = lambda chip: small_out.at[:, :, pl.ds(chip * small_cols, small_cols)]
        base = n * COPIES_PER_ARRAY
        sends.append(copy(base + 3, small_in, small_region(mine), sibling))
        for j, (cx, cy) in enumerate(chips):
            sends.append(copy(base + j, small_in, small_region(mine), (cx, cy, c)))
        for cp in sends:
            cp.start()
        for j, (cx, cy) in enumerate(chips):
            for a in range(n):
                k = a * COPIES_PER_ARRAY + j
                arrival(k, region(a, 2 * cx + cy, c)).wait_recv()
                passed = copy(k + 3, region(a, 2 * cx + cy, c), region(a, 2 * cx + cy, c), sibling)
                passed.start()
                sends.append(passed)
        for j, (cx, cy) in enumerate(chips):
            for a in range(n):
                arrival(a * COPIES_PER_ARRAY + 3 + j, region(a, 2 * cx + cy, 1 - c)).wait_recv()
            arrival(base + j, small_region(2 * cx + cy)).wait_recv()
        for a in range(n):
            arrival(a * COPIES_PER_ARRAY + 6, region(a, mine, None)).wait_recv()
        arrival(base + 3, small_region(mine)).wait_recv()
        for cp in sends:
            cp.wait_send()

    n_sems = n * COPIES_PER_ARRAY + 4
    out = pl.pallas_call(
        body, name="allgather_weights", in_specs=[_ANY] * (n + 1), out_specs=[_ANY] * (n + 1),
        out_shape=[_sds(full_shape(a, axis), a.dtype) for a, axis in zip(shards, axes)]
        + [_sds(full_shape(small, 2), small.dtype)],
        scratch_shapes=[pltpu.SemaphoreType.DMA((n_sems,)), pltpu.SemaphoreType.DMA((n_sems,))],
    )(*shards, small)
    return out[:n], out[n]


def _exchange_core_halves(g):
    n, _, H, C = g.shape

    def body(g_ref, got_ref, send_sem, recv_sem):
        x, y, c = _mesh_pos()
        swap = pltpu.make_async_remote_copy(
            src_ref=g_ref.at[pl.ds(0, n), 1 - c], dst_ref=got_ref, send_sem=send_sem, recv_sem=recv_sem,
            device_id=(x, y, 1 - c), device_id_type=MESH)
        swap.start()
        swap.wait()

    return pl.pallas_call(
        body, name="exchange_core_halves", in_specs=[_ANY], out_specs=_ANY,
        out_shape=_sds((n, H, C), g.dtype),
        scratch_shapes=[pltpu.SemaphoreType.DMA, pltpu.SemaphoreType.DMA],
    )(g)


def _scatter_to_chips(p):
    n, H, C = p.shape

    def body(p_ref, q_ref, send_sems, recv_sems):
        x, y, c = _mesh_pos()
        chips = _other_chips(x, y)
        sends = [pltpu.make_async_remote_copy(
            src_ref=p_ref.at[2 * cx + cy], dst_ref=q_ref.at[j], send_sem=send_sems.at[j],
            recv_sem=recv_sems.at[j], device_id=(cx, cy, c), device_id_type=MESH)
            for j, (cx, cy) in enumerate(chips)]
        for cp in sends:
            cp.start()
        for cp in sends:
            cp.wait_recv()
        for cp in sends:
            cp.wait_send()

    return pl.pallas_call(
        body, name="scatter_to_chips", in_specs=[_ANY], out_specs=_ANY,
        out_shape=_sds((n - 1, H, C), p.dtype),
        scratch_shapes=[pltpu.SemaphoreType.DMA((3,)), pltpu.SemaphoreType.DMA((3,))],
    )(p)


def _share_core_halves(r2):
    _, H, C = r2.shape

    def body(r_ref, out_ref, send_sem, recv_sem):
        x, y, c = _mesh_pos()
        send = pltpu.make_async_remote_copy(
            src_ref=r_ref.at[c], dst_ref=out_ref.at[c], send_sem=send_sem, recv_sem=recv_sem,
            device_id=(x, y, 1 - c), device_id_type=MESH)
        send.start()
        send.wait_send()
        pltpu.make_async_remote_copy(
            src_ref=r_ref.at[c], dst_ref=out_ref.at[1 - c], send_sem=send_sem, recv_sem=recv_sem,
            device_id=(x, y, 1 - c), device_id_type=MESH).wait_recv()

    return pl.pallas_call(
        body, name="share_core_halves", in_specs=[_ANY], out_specs=_ANY,
        out_shape=_sds(r2.shape, r2.dtype), input_output_aliases={0: 0},
        scratch_shapes=[pltpu.SemaphoreType.DMA, pltpu.SemaphoreType.DMA],
    )(r2)


def _place():
    x, y, c = _mesh_pos()
    return jnp.stack([c, 2 * x + y]).astype(jnp.int32)


def _sum_pair(g, got, place):
    n, _, H, C = g.shape

    def body(place_ref, a_ref, b_ref, o_ref):
        o_ref[...] = (a_ref[...] + b_ref[...]).astype(BF16)

    spec = pl.BlockSpec((1, FLAT_TILE, C), lambda s, i, pr: (s, i, 0))
    return pl.pallas_call(
        body, name="sum_core_pair",
        grid_spec=pltpu.PrefetchScalarGridSpec(
            num_scalar_prefetch=1, grid=(n, H // FLAT_TILE),
            in_specs=[pl.BlockSpec((1, None, FLAT_TILE, C), lambda s, i, pr: (s, pr[0], i, 0)), spec],
            out_specs=spec),
        out_shape=_sds((n, H, C), BF16),
        compiler_params=_params(("parallel", "parallel")))(place, g, got)


def _sum_chips(p, q, place):
    n, H, C = p.shape

    def body(place_ref, own_ref, qx_ref, qy_ref, qxy_ref, o_ref):
        mine = place_ref[1]
        own, qx, qy, qxy = (t[0].astype(F32) for t in (own_ref, qx_ref, qy_ref, qxy_ref))

        def term(s):
            rel = jnp.full(own.shape, mine ^ s, jnp.int32)
            return jnp.where(rel == 0, own, jnp.where(rel == 2, qx, jnp.where(rel == 1, qy, qxy)))

        o_ref[0] = ((term(0) + term(1)) + term(2)) + term(3)

    qspec = lambda j: pl.BlockSpec((1, FLAT_TILE, C), lambda i, pr: (j, i, 0))
    return pl.pallas_call(
        body, name="sum_chips",
        grid_spec=pltpu.PrefetchScalarGridSpec(
            num_scalar_prefetch=1, grid=(H // FLAT_TILE,),
            in_specs=[pl.BlockSpec((1, FLAT_TILE, C), lambda i, pr: (pr[1], i, 0)), qspec(0), qspec(1), qspec(2)],
            out_specs=pl.BlockSpec((1, FLAT_TILE, C), lambda i, pr: (pr[0], i, 0))),
        out_shape=_sds((2, H, C), F32),
        compiler_params=_params(("parallel",)))(place, p, q, q, q)


ADAMW_BLOCK = 1 << 18


def _adamw(w, g, m, v):
    shape = w.shape
    C = shape[-1]
    R = _size(shape) // C
    tr = R
    while tr * C > ADAMW_BLOCK and tr % 16 == 0:
        tr //= 2
    w, g, m, v = (t.reshape(R, C) for t in (w, g, m, v))

    def body(w_ref, g_ref, m_ref, v_ref, d_ref, nm_ref, nv_ref):
        gv = g_ref[...]
        nm = ADAM_B1 * m_ref[...] + (1.0 - ADAM_B1) * gv
        nv = ADAM_B2 * v_ref[...] + (1.0 - ADAM_B2) * (gv * gv)
        m_hat = nm / (1.0 - ADAM_B1 ** ADAM_STEP)
        v_hat = nv / (1.0 - ADAM_B2 ** ADAM_STEP)
        d_ref[...] = -ADAM_LR * (m_hat / (jnp.sqrt(v_hat) + ADAM_EPS) + ADAM_WD * w_ref[...])
        nm_ref[...] = nm
        nv_ref[...] = nv

    spec = pl.BlockSpec((tr, C), lambda i: (i, 0))
    outs = pl.pallas_call(body, name="adamw", grid=(R // tr,), in_specs=[spec] * 4, out_specs=[spec] * 3,
                          out_shape=[_sds((R, C), F32)] * 3, compiler_params=_params(("parallel",)))(w, g, m, v)
    return tuple(t.reshape(shape) for t in outs)


def _reduce_scatter_grads(gfull):
    n, R, C = gfull.shape
    place = _place()
    g4 = gfull.reshape(n, 2, R // 2, C)
    p = _sum_pair(g4, _exchange_core_halves(g4), place)
    return _share_core_halves(_sum_chips(p, _scatter_to_chips(p), place)).reshape(R, C)


def _local_step(x, p, positions, loss_target, pre_g, post_g, w):
    Bl, S, _ = x.shape
    T = Bl * S
    cos, sin = _rope_tables(positions)
    xs = x.reshape(T, D_MODEL)
    saved = []
    for i in range(DEPTH):
        j = i // 2
        g_pre, g_post = pre_g[i:i + 1], post_g[i:i + 1]
        h = _rmsnorm_fwd(xs, g_pre)
        st = {"x": xs, "h": h}
        if i % 2 == 0:
            proj = _mm(h, w["attn_w_in"][j], name="attn_in")
            res = [_attn_fwd(proj, cos, sin, g, Bl, S) for g in range(N_GROUPS)]
            a, o, lse = _attn_combine([r[0] for r in res], [r[1] for r in res], proj)
            w_out = w["attn_w_out"][j]
            st.update(proj=proj, a=a, o=o, lse=lse, qkv=[r[2] for r in res])
        else:
            w_ab, w_z = w["conv_w_in"][j][:, :2 * D_MODEL], w["conv_w_in"][j][:, 2 * D_MODEL:]
            ab = _mm(h, w_ab, out_dtype=BF16, name="conv_in_ab")
            z = _mm(h, w_z, out_dtype=BF16, name="conv_in_z")
            dw = jnp.pad(w["conv_dw_w"][j], ((0, 1), (0, 0)))
            u1, a = _conv_fwd(ab, z, dw, w["conv_dw_b"][j:j + 1], w["conv_ln_g"][j:j + 1],
                              w["conv_ln_b"][j:j + 1], Bl, S)
            w_out = w["conv_w_out"][j]
            st.update(w_ab=w_ab, w_z=w_z, ab=ab, z=z, dw=dw, u1=u1, a=a)
        y, x1 = _mm_rows(a, w_out, [xs, g_post], _post_epilogue, (F32, F32), "branch_out_post")
        pi = p[i].reshape(T, PLE_DIM)
        pe = _mm(pi, w["ple_w_proj"][i], name="ple_proj")
        gl, xs = _mm_rows(x1, w["ple_w_gate"][i], [pe], _ple_epilogue, (F32, F32), "ple_gate_fwd")
        st.update(y=y, x1=x1, pi=pi, pe=pe, gl=gl)
        saved.append(st)

    sq, dx = _loss_fwd_bwd(xs, loss_target.reshape(T, D_MODEL))

    grads = {n: [None] * shape[0] for n, shape, _ in _LAYOUT}
    for i in reversed(range(DEPTH)):
        j = i // 2
        st = saved[i]
        g_pre, g_post = pre_g[i:i + 1], post_g[i:i + 1]
        dpe, dgl, dx1, dy, dg_post = _ple_post_bwd(dx, st["pe"], st["gl"], w["ple_w_gate"][i], st["y"], g_post)
        grads["ple_w_proj"][i] = _mm(st["pi"], dpe, ta=True, name="ple_proj_wgrad")
        grads["ple_w_gate"][i] = _mm(st["x1"], dgl, ta=True, name="ple_gate_wgrad")
        grads["post_norm_g"][i] = dg_post[0]
        if i % 2 == 0:
            grads["attn_w_out"][j] = _mm(st["a"], dy, ta=True, name="attn_out_wgrad")
            da = _mm(dy, w["attn_w_out"][j], tb=True, name="attn_out_dgrad")
            do, dproj = _gate_bwd(da, st["o"], st["proj"])
            for g in range(N_GROUPS):
                dproj = _attn_bwd(st["qkv"][g], cos, sin, do, st["o"], st["lse"], dproj, g, Bl, S)
            dx, dg_pre = _dgrad_pre_bwd(dproj, w["attn_w_in"][j], st["x"], g_pre, dx1, None, 2048, "attn_in_dgrad_pre")
            grads["attn_w_in"][j] = _mm(st["h"], dproj, ta=True, name="attn_in_wgrad")
        else:
            grads["conv_w_out"][j] = _mm(st["a"], dy, ta=True, name="conv_out_wgrad")
            da2 = _mm(dy, w["conv_w_out"][j], tb=True, name="conv_out_dgrad")
            du1, dz, dln_g, dln_b = _conv_norm_bwd(da2, st["z"], st["u1"], w["conv_ln_g"][j:j + 1],
                                                   w["conv_ln_b"][j:j + 1])
            dab, ddw, ddb = _conv_bwd(st["ab"], du1, st["dw"], Bl, S)
            dh = _mm(dz, st["w_z"], tb=True, name="conv_in_z_dgrad")
            dx, dg_pre = _dgrad_pre_bwd(dab, st["w_ab"], st["x"], g_pre, dx1, dh, 2048, "conv_in_dgrad_pre")
            dw_ab = _mm(st["h"], dab, ta=True, name="conv_in_ab_wgrad")
            dw_z = _mm(st["h"], dz, ta=True, name="conv_in_z_wgrad")
            grads["conv_w_in"][j] = jnp.concatenate([dw_ab, dw_z], axis=1)
            grads["conv_dw_w"][j] = ddw[:CONV_WIDTH]
            grads["conv_dw_b"][j] = ddb[0]
            grads["conv_ln_g"][j] = dln_g[0]
            grads["conv_ln_b"][j] = dln_b[0]
        grads["pre_norm_g"][i] = dg_pre[0]
    grads = {n: jnp.stack(v) for n, v in grads.items()}
    return sq, dx.reshape(Bl, S, D_MODEL), grads


_NAMES = tuple(n for n, _, _ in _LAYOUT)


def kernel(x, p, positions, pre_norm_g, post_norm_g, attn_w_in, attn_w_out, conv_w_in, conv_dw_w, conv_dw_b, conv_ln_g, conv_ln_b, conv_w_out, ple_w_proj, ple_w_gate, loss_target, m_pre_norm_g, m_post_norm_g, m_attn_w_in, m_attn_w_out, m_conv_w_in, m_conv_dw_w, m_conv_dw_b, m_conv_ln_g, m_conv_ln_b, m_conv_w_out, m_ple_w_proj, m_ple_w_gate, v_pre_norm_g, v_post_norm_g, v_attn_w_in, v_attn_w_out, v_conv_w_in, v_conv_dw_w, v_conv_dw_b, v_conv_ln_g, v_conv_ln_b, v_conv_w_out, v_ple_w_proj, v_ple_w_gate):
    w_loc = dict(zip(_NAMES, (pre_norm_g, post_norm_g, attn_w_in, attn_w_out, conv_w_in, conv_dw_w, conv_dw_b,
                              conv_ln_g, conv_ln_b, conv_w_out, ple_w_proj, ple_w_gate)))
    m_loc = dict(zip(_NAMES, (m_pre_norm_g, m_post_norm_g, m_attn_w_in, m_attn_w_out, m_conv_w_in, m_conv_dw_w,
                              m_conv_dw_b, m_conv_ln_g, m_conv_ln_b, m_conv_w_out, m_ple_w_proj, m_ple_w_gate)))
    v_loc = dict(zip(_NAMES, (v_pre_norm_g, v_post_norm_g, v_attn_w_in, v_attn_w_out, v_conv_w_in, v_conv_dw_w,
                              v_conv_dw_b, v_conv_ln_g, v_conv_ln_b, v_conv_w_out, v_ple_w_proj, v_ple_w_gate)))

    gathered, small = _allgather_weights([w_loc[n].astype(BF16) for n in _MATMUL_WEIGHTS],
                                         [_AXIS[n] for n in _MATMUL_WEIGHTS], _stack_small(w_loc))
    w_full = dict(zip(_MATMUL_WEIGHTS, gathered), **_unstack_small(small))
    sq, grad_x, grads = _local_step(x, p, positions, loss_target, pre_norm_g, post_norm_g, w_full)
    loss = lax.psum(sq[0, 0] * (0.5 / D_MODEL), ("x", "y", "c"))

    g_flat = _reduce_scatter_grads(_pack_full_grads(grads))
    g_out = _unpack_f32(g_flat)
    updates = {n: _adamw(w_loc[n], g_out[n], m_loc[n], v_loc[n]) for n in _NAMES}
    d_out, m_out, v_out = ({n: updates[n][k] for n in _NAMES} for k in range(3))
    return (loss, grad_x, *[g_out[n] for n in _NAMES], *[d_out[n] for n in _NAMES],
            *[m_out[n] for n in _NAMES], *[v_out[n] for n in _NAMES])
```

```python
import jax
import jax.numpy as jnp
from jax import lax
from jax.experimental import pallas as pl
from jax.experimental.pallas import tpu as pltpu

F32 = jnp.float32
BF16 = jnp.bfloat16

D_MODEL = 1024
DEPTH = 4
PLE_DIM = 256
HEAD_DIM = 64
WIN_DIL = ((128, 1), (512, 4), (2048, 16))
N_GROUPS = 3
N_BACK = 128
BLOCK_UNROLL = 8
RESIDUES_TOGETHER = 2
ROPE_THETA = 10000.0
CONV_WIDTH = 31
CONV_HALO = 32
RMS_EPS = 1e-6
LN_EPS = 1e-5
NEG_INF = -1e30
ADAM_LR, ADAM_B1, ADAM_B2, ADAM_EPS, ADAM_WD, ADAM_STEP = 0.001, 0.9, 0.999, 1e-08, 0.01, 10

LANES = 128
N_CHIPS = 4
VMEM_LIMIT = 48 * 1024 * 1024
VMEM_LIMIT_ATTN = 56 * 1024 * 1024
FLAT_COLS = 256
FLAT_ROWS = 36864
FLAT_TILE = 2048
FLAT_ROW_ALIGN = 16
PROJ_COLS = (3 * N_GROUPS + 1) * D_MODEL
HEAD_PAIRS = D_MODEL // LANES

MESH = pl.DeviceIdType.MESH


def _params(sem=None, vmem=VMEM_LIMIT):
    return pltpu.CompilerParams(dimension_semantics=sem, vmem_limit_bytes=vmem)


def _sigmoid(v):
    return 1.0 / (1.0 + jnp.exp(-v))


def _mm(a, b, *, ta=False, tb=False, out_dtype=F32, tm=1024, tn=1024, tk=1024, name="mm"):
    if ta:
        K, M = a.shape
    else:
        M, K = a.shape
    if tb:
        N, K2 = b.shape
    else:
        K2, N = b.shape
    assert K == K2, (a.shape, b.shape)
    tm, tn, tk = min(tm, M), min(tn, N), min(tk, K)
    assert M % tm == 0 and N % tn == 0 and K % tk == 0
    nk = K // tk
    dims = (((0 if ta else 1,), (1 if tb else 0,)), ((), ()))

    def body(a_ref, b_ref, o_ref, *scratch):
        k = pl.program_id(2)
        part = lax.dot_general(a_ref[...].astype(BF16), b_ref[...].astype(BF16), dims, preferred_element_type=F32)
        if nk == 1:
            o_ref[...] = part.astype(out_dtype)
        else:
            acc_ref, = scratch

            @pl.when(k == 0)
            def _():
                acc_ref[...] = part

            @pl.when((k > 0) & (k < nk - 1))
            def _():
                acc_ref[...] += part

            @pl.when(k == nk - 1)
            def _():
                o_ref[...] = (acc_ref[...] + part).astype(out_dtype)

    a_spec = pl.BlockSpec((tk, tm), lambda i, j, k: (k, i)) if ta else pl.BlockSpec((tm, tk), lambda i, j, k: (i, k))
    b_spec = pl.BlockSpec((tn, tk), lambda i, j, k: (j, k)) if tb else pl.BlockSpec((tk, tn), lambda i, j, k: (k, j))
    return pl.pallas_call(
        body, name=name, grid=(M // tm, N // tn, nk),
        in_specs=[a_spec, b_spec], out_specs=pl.BlockSpec((tm, tn), lambda i, j, k: (i, j)),
        out_shape=jax.ShapeDtypeStruct((M, N), out_dtype),
        scratch_shapes=[pltpu.VMEM((tm, tn), F32)] if nk > 1 else [],
        compiler_params=_params(("parallel", "parallel", "arbitrary")),
    )(a, b)


def _mm_rows(a, b, extras, epilogue, out_dtypes, name, tm=512):
    M, K = a.shape
    N = b.shape[1]
    n_ex = len(extras)

    def body(*refs):
        a_ref, b_ref = refs[:2]
        av = a_ref[...]
        acc = jnp.dot(av.astype(BF16), b_ref[...].astype(BF16), preferred_element_type=F32)
        results = epilogue(acc, av, *[e[...] for e in refs[2:2 + n_ex]])
        for o_ref, r in zip(refs[2 + n_ex:], results):
            o_ref[...] = r.astype(o_ref.dtype)

    tile = pl.BlockSpec((tm, N), lambda i: (i, 0))
    in_specs = [pl.BlockSpec((tm, K), lambda i: (i, 0)), pl.BlockSpec((K, N), lambda i: (0, 0))]
    in_specs += [tile if e.shape[0] == M else pl.BlockSpec((1, N), lambda i: (0, 0)) for e in extras]
    return pl.pallas_call(
        body, name=name, grid=(M // tm,), in_specs=in_specs, out_specs=[tile] * len(out_dtypes),
        out_shape=[jax.ShapeDtypeStruct((M, N), dt) for dt in out_dtypes],
        compiler_params=_params(("parallel",)),
    )(a, b, *extras)


ROW_TILE = 512


def _rows(w=D_MODEL, cb=0, tr=ROW_TILE):
    return pl.BlockSpec((tr, w), lambda i: (i, cb))


def _full(shape):
    return pl.BlockSpec(shape, lambda i: (0,) * len(shape))


def _row_call(body, name, T, in_specs, out_specs, out_shape, args, tr=ROW_TILE):
    return pl.pallas_call(body, name=name, grid=(T // tr,), in_specs=in_specs, out_specs=out_specs,
                          out_shape=out_shape, compiler_params=_params(("arbitrary",)))(*args)


def _sds(shape, dtype):
    return jax.ShapeDtypeStruct(shape, dtype)


def _rmsnorm_fwd(x, g):
    T = x.shape[0]

    def body(x_ref, g_ref, h_ref):
        xv = x_ref[...]
        r = lax.rsqrt(jnp.mean(xv * xv, axis=1, keepdims=True) + RMS_EPS)
        h_ref[...] = (xv * r * g_ref[...]).astype(BF16)

    return _row_call(body, "rmsnorm_fwd", T, [_rows(), _full((1, D_MODEL))], _rows(),
                     _sds((T, D_MODEL), BF16), (x, g))


def _post_epilogue(y, a_tile, x, g):
    del a_tile
    return y, x + y * lax.rsqrt(jnp.mean(y * y, axis=1, keepdims=True) + RMS_EPS) * g


def _ple_epilogue(gl, x1, pe):
    return gl, x1 + pe * _sigmoid(gl)


def _dgrad_pre_bwd(a, b, x, g, dx1, add, tk, name, tm=512):
    M, K = a.shape
    N = b.shape[0]
    tk = min(tk, K)
    assert M % tm == 0 and K % tk == 0 and N == D_MODEL
    nk = K // tk

    def body(*refs):
        a_ref, b_ref, x_ref, g_ref, dx1_ref = refs[:5]
        add_ref = refs[5] if add is not None else None
        dx_ref, dg_ref = refs[-3:-1] if nk > 1 else refs[-2:]
        i, k = pl.program_id(0), pl.program_id(1)

        @pl.when((i == 0) & (k == 0))
        def _():
            dg_ref[...] = jnp.zeros_like(dg_ref)

        part = lax.dot_general(a_ref[...].astype(BF16), b_ref[...].astype(BF16), _NT, preferred_element_type=F32)

        def finish(dh):
            if add is not None:
                dh = dh + add_ref[...]
            xv = x_ref[...]
            r = lax.rsqrt(jnp.mean(xv * xv, axis=1, keepdims=True) + RMS_EPS)
            xh = xv * r
            dg_ref[...] += jnp.sum(dh * xh, axis=0, keepdims=True)
            dn = dh * g_ref[...]
            dx_ref[...] = dx1_ref[...] + r * (dn - xh * jnp.mean(dn * xh, axis=1, keepdims=True))

        if nk == 1:
            finish(part)
        else:
            acc_ref = refs[-1]

            @pl.when(k == 0)
            def _():
                acc_ref[...] = part

            @pl.when((k > 0) & (k < nk - 1))
            def _():
                acc_ref[...] += part

            @pl.when(k == nk - 1)
            def _():
                finish(acc_ref[...] + part)

    tile = pl.BlockSpec((tm, N), lambda i, k: (i, 0))
    row = pl.BlockSpec((1, N), lambda i, k: (0, 0))
    in_specs = [pl.BlockSpec((tm, tk), lambda i, k: (i, k)), pl.BlockSpec((N, tk), lambda i, k: (0, k)), tile, row, tile]
    args = [a, b, x, g, dx1]
    if add is not None:
        in_specs.append(tile)
        args.append(add)
    return pl.pallas_call(
        body, name=name, grid=(M // tm, nk), in_specs=in_specs, out_specs=[tile, row],
        out_shape=[_sds((M, N), F32), _sds((1, N), F32)],
        scratch_shapes=[pltpu.VMEM((tm, N), F32)] if nk > 1 else [],
        compiler_params=_params(("arbitrary", "arbitrary")),
    )(*args)


def _ple_post_bwd(dx2, pe, gl, w_gate, y, g_post):
    T = dx2.shape[0]

    def body(d_ref, pe_ref, gl_ref, w_ref, y_ref, g_ref, dpe_ref, dgl_ref, dx1_ref, dy_ref, dg_ref):
        @pl.when(pl.program_id(0) == 0)
        def _():
            dg_ref[...] = jnp.zeros_like(dg_ref)

        dv = d_ref[...]
        sg = _sigmoid(gl_ref[...])
        dpe_ref[...] = (dv * sg).astype(BF16)
        dgl = (dv * pe_ref[...] * sg * (1.0 - sg)).astype(BF16)
        dgl_ref[...] = dgl
        dx1 = dv + lax.dot_general(dgl, w_ref[...], _NT, preferred_element_type=F32)
        dx1_ref[...] = dx1
        yv = y_ref[...]
        r = lax.rsqrt(jnp.mean(yv * yv, axis=1, keepdims=True) + RMS_EPS)
        yh = yv * r
        dg_ref[...] += jnp.sum(dx1 * yh, axis=0, keepdims=True)
        dn = dx1 * g_ref[...]
        dy_ref[...] = (r * (dn - yh * jnp.mean(dn * yh, axis=1, keepdims=True))).astype(BF16)

    row = _full((1, D_MODEL))
    return _row_call(body, "ple_post_bwd", T, [_rows()] * 3 + [_full((D_MODEL, D_MODEL)), _rows(), row],
                     [_rows()] * 4 + [row],
                     [_sds((T, D_MODEL), BF16)] * 2 + [_sds((T, D_MODEL), F32), _sds((T, D_MODEL), BF16),
                                                      _sds((1, D_MODEL), F32)],
                     (dx2, pe, gl, w_gate, y, g_post))


def _loss_fwd_bwd(y, target):
    T = y.shape[0]

    def body(y_ref, t_ref, s_ref, d_ref):
        @pl.when(pl.program_id(0) == 0)
        def _():
            s_ref[...] = jnp.zeros_like(s_ref)

        e = y_ref[...] - t_ref[...]
        s_ref[...] += jnp.sum(e * e).reshape(1, 1)
        d_ref[...] = e * (1.0 / D_MODEL)

    return _row_call(body, "loss", T, [_rows()] * 2, [_full((1, 1)), _rows()],
                     [_sds((1, 1), F32), _sds((T, D_MODEL), F32)], (y, target))


def _attn_combine(outs, lses, proj):
    T = proj.shape[0]

    def body(o0, o1, o2, l0, l1, l2, z_ref, a_ref, o_ref, lse_ref):
        a0, a1, a2 = l0[...], l1[...], l2[...]
        m = jnp.maximum(jnp.maximum(a0, a1), a2)
        e0, e1, e2 = jnp.exp(a0 - m), jnp.exp(a1 - m), jnp.exp(a2 - m)
        ssum = e0 + e1 + e2
        o = (e0 * o0[...] + e1 * o1[...] + e2 * o2[...]) / ssum
        zv = z_ref[...].astype(F32)
        o_ref[...] = o
        lse_ref[...] = m + jnp.log(ssum)
        a_ref[...] = (o * zv * _sigmoid(zv)).astype(BF16)

    return _row_call(body, "attn_combine", T, [_rows()] * 6 + [_rows(cb=3 * N_GROUPS)], [_rows()] * 3,
                     [_sds((T, D_MODEL), BF16), _sds((T, D_MODEL), F32), _sds((T, D_MODEL), F32)],
                     (*outs, *lses, proj))


def _gate_bwd(da, o, proj):
    T = da.shape[0]

    def body(da_ref, o_ref, z_ref, do_ref, dz_ref):
        dv = da_ref[...]
        zv = z_ref[...]
        sg = _sigmoid(zv)
        do_ref[...] = dv * zv * sg
        dz_ref[...] = dv * o_ref[...] * sg * (1.0 + zv * (1.0 - sg))

    zcols = _rows(cb=3 * N_GROUPS)
    return _row_call(body, "gate_bwd", T, [_rows(), _rows(), zcols], [_rows(), zcols],
                     [_sds((T, D_MODEL), F32), _sds((T, PROJ_COLS), F32)], (da, o, proj))


def _rope_tables(positions):
    inv_freq = 1.0 / (ROPE_THETA ** (jnp.arange(0, HEAD_DIM, 2, dtype=F32) / HEAD_DIM))
    ang = positions.astype(F32)[..., None] * inv_freq
    cos, sin = jnp.cos(ang), jnp.sin(ang)
    return jnp.tile(cos, (1, 1, 4)), jnp.concatenate([-sin, sin, -sin, sin], axis=-1)


def _rotate_half_partner(t):
    lane = lax.broadcasted_iota(jnp.int32, t.shape, 1)
    return jnp.where((lane % HEAD_DIM) < HEAD_DIM // 2,
                     pltpu.roll(t, LANES - HEAD_DIM // 2, 1), pltpu.roll(t, HEAD_DIM // 2, 1))


def _mask_bias(first):
    qi = lax.broadcasted_iota(jnp.int32, (N_BACK, 2 * N_BACK), 0)
    kj = lax.broadcasted_iota(jnp.int32, (N_BACK, 2 * N_BACK), 1)
    ok = (kj >= qi) & (kj <= qi + N_BACK)
    if first:
        ok = ok & (kj >= N_BACK)
    return jnp.where(ok, 0.0, NEG_INF).astype(F32)


def _stack_heads(t, head0):
    zero = jnp.zeros_like(t)
    return jnp.concatenate([jnp.where(head0, t, zero), jnp.where(head0, zero, t)], axis=0)


def _unstack_heads(t2, head0):
    return jnp.where(head0, t2[:N_BACK], t2[N_BACK:])


def _block_loop(nb, block):
    first, rest = _mask_bias(True), _mask_bias(False)
    first, rest = jnp.concatenate([first, first], axis=0), jnp.concatenate([rest, rest], axis=0)
    if nb <= BLOCK_UNROLL:
        for n in range(nb):
            block(n, first if n == 0 else rest)
        return

    def step(n, carry):
        block(n, jnp.where(n == 0, first, rest))
        return carry

    lax.fori_loop(0, nb, step, 0, unroll=BLOCK_UNROLL)


def _for(count, body, unroll_fully):
    if unroll_fully:
        for i in range(count):
            body(i)
    else:
        lax.fori_loop(0, count, lambda i, carry: (body(i), carry)[1], 0)


def _residues_together(nb):
    return min(RESIDUES_TOGETHER, max(1, BLOCK_UNROLL // nb))


def _residue_loop(d, nb, residue):
    together = _residues_together(nb)
    assert d % together == 0

    def group(i, carry):
        for u in range(together):
            residue(i * together + u, u)
        return carry

    lax.fori_loop(0, d // together, group, 0)


_NT = (((1,), (1,)), ((), ()))
_TN = (((0,), (0,)), ((), ()))


def _residue_rows(r, i, d):
    start = r + i * (N_BACK * d)
    if d == 1:
        return pl.ds(pl.multiple_of(start, N_BACK), N_BACK)
    return pl.ds(start, N_BACK, stride=d)


def _seq_rows(i):
    return pl.ds(pl.multiple_of(i * N_BACK, N_BACK), N_BACK)


def _rows_at(base, i, size=N_BACK):
    return pl.ds(pl.multiple_of(base + i * N_BACK, N_BACK), size)


def _attn_fwd(proj, cos, sin, group, Bl, S):
    d = WIN_DIL[group][1]
    L = S // d
    nb = L // N_BACK
    P = L + N_BACK
    assert WIN_DIL[group][0] // d == N_BACK and L % N_BACK == 0

    def body(q_ref, k_ref, v_ref, cos_ref, sin_ref, o_ref, lse_ref, qr, kr, vp):
        head0 = lax.broadcasted_iota(jnp.int32, (1, LANES), 1) < HEAD_DIM
        zeros = jnp.zeros((N_BACK, LANES), BF16)

        def residue(r, u):
            del u
            qbase, kbase = r * L, r * P
            kr[_rows_at(kbase, 0), :] = zeros
            vp[_rows_at(kbase, 0), :] = zeros

            def rope(i):
                rows = _residue_rows(r, i, d)
                cs, sn = cos_ref[rows, :], sin_ref[rows, :]
                q, k = q_ref[rows, :], k_ref[rows, :]
                qr[_rows_at(qbase, i), :] = ((q * cs + _rotate_half_partner(q) * sn)
                                            * (HEAD_DIM ** -0.5)).astype(BF16)
                kr[_rows_at(kbase, i + 1), :] = (k * cs + _rotate_half_partner(k) * sn).astype(BF16)
                vp[_rows_at(kbase, i + 1), :] = v_ref[rows, :].astype(BF16)

            _for(nb, rope, nb <= BLOCK_UNROLL)

            def block(n, bias):
                win = _rows_at(kbase, n, 2 * N_BACK)
                q2, kw, vw = _stack_heads(qr[_rows_at(qbase, n), :], head0), kr[win, :], vp[win, :]
                s = lax.dot_general(q2, kw, _NT, preferred_element_type=F32) + bias
                m = jnp.max(s, axis=1, keepdims=True)
                p = jnp.exp(s - m)
                l = jnp.sum(p, axis=1, keepdims=True)
                pv = jnp.dot(p.astype(BF16), vw, preferred_element_type=F32)
                rows = _residue_rows(r, n, d)
                o_ref[rows, :] = _unstack_heads(pv * (1.0 / l), head0)
                lse_ref[rows, :] = _unstack_heads((m + jnp.log(l)) + jnp.zeros((2 * N_BACK, LANES), F32), head0)

            _block_loop(nb, block)

        _residue_loop(d, nb, residue)

    act = pl.BlockSpec((None, S, LANES), lambda b, hp: (b, 0, hp))
    tab = pl.BlockSpec((None, S, LANES), lambda b, hp: (b, 0, 0))
    col = lambda which: pl.BlockSpec((None, S, LANES),
                                     lambda b, hp: (b, 0, (which * N_GROUPS + group) * HEAD_PAIRS + hp))
    seq = lambda rows: pl.BlockSpec((None, None, rows, LANES), lambda b, hp: (b, hp, 0, 0))
    p3 = proj.reshape(Bl, S, PROJ_COLS)
    o, lse, qr, kr, vp = pl.pallas_call(
        body, name="attn_fwd_g%d" % group, grid=(Bl, HEAD_PAIRS),
        in_specs=[col(0), col(1), col(2), tab, tab], out_specs=[act, act, seq(S), seq(d * P), seq(d * P)],
        out_shape=[_sds((Bl, S, D_MODEL), F32)] * 2 + [_sds((Bl, HEAD_PAIRS, S, LANES), BF16)]
        + [_sds((Bl, HEAD_PAIRS, d * P, LANES), BF16)] * 2,
        compiler_params=_params(("parallel", "arbitrary"), VMEM_LIMIT_ATTN),
    )(p3, p3, p3, cos, sin)
    return o.reshape(Bl * S, D_MODEL), lse.reshape(Bl * S, D_MODEL), (qr, kr, vp)


def _attn_bwd(saved, cos, sin, do, o, lse, dproj, group, Bl, S):
    d = WIN_DIL[group][1]
    L = S // d
    nb = L // N_BACK
    P = L + N_BACK
    steps = Bl * HEAD_PAIRS

    def body(qr, kr, vp, cos_ref, sin_ref, do_ref, o_ref, lse_ref, dproj_in, dproj_ref,
             dk_accs, dv_accs, stage, sems):
        del dproj_in
        head0 = lax.broadcasted_iota(jnp.int32, (1, LANES), 1) < HEAD_DIM
        b, hp = pl.program_id(0), pl.program_id(1)
        step = b * HEAD_PAIRS + hp
        slot = step % 2
        dq_s, dk_s, dv_s = stage.at[slot, 0], stage.at[slot, 1], stage.at[slot, 2]

        def copies(which_slot):
            out = []
            for which in range(3):
                col = ((which * N_GROUPS + group) * HEAD_PAIRS + hp) * LANES
                out.append(pltpu.make_async_copy(
                    stage.at[which_slot, which], dproj_ref.at[b, :, pl.ds(pl.multiple_of(col, LANES), LANES)],
                    sems.at[which_slot, which]))
            return out

        @pl.when(step >= 2)
        def _():
            for cp in copies(slot):
                cp.wait()

        def residue(r, u):
            qbase, kbase = r * L, r * P
            dk_acc, dv_acc = dk_accs.at[u], dv_accs.at[u]
            dk_acc[...] = jnp.zeros_like(dk_acc)
            dv_acc[...] = jnp.zeros_like(dv_acc)

            def block(n, bias):
                win = pl.ds(pl.multiple_of(n * N_BACK, N_BACK), 2 * N_BACK)
                kwin = _rows_at(kbase, n, 2 * N_BACK)
                rows = _residue_rows(r, n, d)
                q2, kw, vw = _stack_heads(qr[_rows_at(qbase, n), :], head0), kr[kwin, :], vp[kwin, :]
                dof = do_ref[rows, :]
                do2 = _stack_heads(dof.astype(BF16), head0)
                lse_b = lse_ref[rows, :]
                lse2 = jnp.concatenate([lse_b[:, 0:1], lse_b[:, HEAD_DIM:HEAD_DIM + 1]], axis=0)
                dsum = _stack_heads(dof * o_ref[rows, :], head0)
                delta = jnp.sum(dsum, axis=1, keepdims=True)
                s = lax.dot_general(q2, kw, _NT, preferred_element_type=F32) + bias
                p = jnp.exp(s - lse2)
                dp = lax.dot_general(do2, vw, _NT, preferred_element_type=F32)
                ds = (p * (dp - delta)).astype(BF16)
                dq = _unstack_heads(jnp.dot(ds, kw, preferred_element_type=F32), head0) * (HEAD_DIM ** -0.5)
                cs, sn = cos_ref[rows, :], sin_ref[rows, :]
                dq_s[rows, :] = dq * cs + _rotate_half_partner(dq * sn)
                dk_acc[win, :] += lax.dot_general(ds, q2, _TN, preferred_element_type=F32)
                dv_acc[win, :] += lax.dot_general(p.astype(BF16), do2, _TN, preferred_element_type=F32)

            _block_loop(nb, block)

            def finish(i):
                rows = _residue_rows(r, i, d)
                cs, sn = cos_ref[rows, :], sin_ref[rows, :]
                dk = dk_acc[_seq_rows(i + 1), :]
                dk_s[rows, :] = dk * cs + _rotate_half_partner(dk * sn)
                dv_s[rows, :] = dv_acc[_seq_rows(i + 1), :]

            _for(nb, finish, nb <= BLOCK_UNROLL)

        _residue_loop(d, nb, residue)
        for cp in copies(slot):
            cp.start()

        @pl.when(step == steps - 1)
        def _():
            if steps > 1:
                for cp in copies(1 - slot):
                    cp.wait()
            for cp in copies(slot):
                cp.wait()

    act = pl.BlockSpec((None, S, LANES), lambda b, hp: (b, 0, hp))
    tab = pl.BlockSpec((None, S, LANES), lambda b, hp: (b, 0, 0))
    seq = lambda rows: pl.BlockSpec((None, None, rows, LANES), lambda b, hp: (b, hp, 0, 0))
    view = lambda t: t.reshape(Bl, S, D_MODEL)
    out = pl.pallas_call(
        body, name="attn_bwd_g%d" % group, grid=(Bl, HEAD_PAIRS),
        in_specs=[seq(S), seq(d * P), seq(d * P), tab, tab, act, act, act, _ANY], out_specs=_ANY,
        out_shape=_sds((Bl, S, PROJ_COLS), F32), input_output_aliases={8: 0},
        scratch_shapes=[pltpu.VMEM((_residues_together(nb), P, LANES), F32),
                        pltpu.VMEM((_residues_together(nb), P, LANES), F32),
                        pltpu.VMEM((2, 3, S, LANES), F32), pltpu.SemaphoreType.DMA((2, 3))],
        compiler_params=_params(("arbitrary", "arbitrary"), VMEM_LIMIT_ATTN),
    )(*saved, cos, sin, view(do), view(o), view(lse), dproj.reshape(Bl, S, PROJ_COLS))
    return out.reshape(Bl * S, PROJ_COLS)


CONV_TILE = 256
CONV_CHUNK = 64
SUBLANES = 8
CONV_SHIFT_ROWS = CONV_TILE + CONV_HALO - SUBLANES


def _fill_shifted(shifted, ext, cs):
    for k in range(1, SUBLANES):
        shifted[k - 1] = ext[pl.ds(k, CONV_SHIFT_ROWS), cs]


def _shifted_rows(shifted, ext, cs, off):
    k = off % SUBLANES
    if k == 0:
        return ext[pl.ds(off, CONV_CHUNK), cs]
    return shifted[k - 1, pl.ds(off - k, CONV_CHUNK), :]


def _conv_fwd(proj, z, dw, dwb, ln_g, ln_b, Bl, S):
    tr = CONV_TILE
    nj = S // tr
    hb = tr // CONV_HALO

    def body(a_ref, b_ref, ah_ref, bh_ref, z_ref, dw_ref, dwb_ref, g_ref, bb_ref, u1_ref, out_ref, ext, shifted):
        j = pl.program_id(1)
        halo = ah_ref[0].astype(F32) * _sigmoid(bh_ref[0].astype(F32))
        ext[pl.ds(0, CONV_HALO), :] = jnp.where(j > 0, halo, 0.0)
        ext[pl.ds(CONV_HALO, tr), :] = a_ref[0].astype(F32) * _sigmoid(b_ref[0].astype(F32))

        def cols(c, carry):
            cs = pl.ds(pl.multiple_of(c * LANES, LANES), LANES)
            _fill_shifted(shifted, ext, cs)
            for rc in range(tr // CONV_CHUNK):
                acc = jnp.zeros((CONV_CHUNK, LANES), F32)
                for w in range(CONV_WIDTH):
                    off = rc * CONV_CHUNK + CONV_HALO - (CONV_WIDTH - 1) + w
                    acc = acc + dw_ref[pl.ds(w, 1), cs] * _shifted_rows(shifted, ext, cs, off)
                u1_ref[0, pl.ds(rc * CONV_CHUNK, CONV_CHUNK), cs] = acc + dwb_ref[:, cs]
            return carry

        lax.fori_loop(0, D_MODEL // LANES, cols, 0)
        u1 = u1_ref[0]
        mu = jnp.mean(u1, axis=1, keepdims=True)
        xc = u1 - mu
        rstd = lax.rsqrt(jnp.mean(xc * xc, axis=1, keepdims=True) + LN_EPS)
        u2 = xc * rstd * g_ref[...] + bb_ref[...]
        zv = z_ref[0].astype(F32)
        out_ref[0] = (u2 * _sigmoid(u2) * zv * _sigmoid(zv)).astype(BF16)

    tile = lambda cb: pl.BlockSpec((1, tr, D_MODEL), lambda b, j: (b, j, cb))
    halo = lambda cb: pl.BlockSpec((1, CONV_HALO, D_MODEL), lambda b, j: (b, jnp.maximum(j * hb - 1, 0), cb))
    par = lambda r: pl.BlockSpec((r, D_MODEL), lambda b, j: (0, 0))
    p3 = proj.reshape(Bl, S, 2 * D_MODEL)
    u1, out = pl.pallas_call(
        body, name="conv_fwd", grid=(Bl, nj),
        in_specs=[tile(0), tile(1), halo(0), halo(1), tile(0), par(32), par(1), par(1), par(1)],
        out_specs=[tile(0), tile(0)],
        out_shape=[_sds((Bl, S, D_MODEL), F32), _sds((Bl, S, D_MODEL), BF16)],
        scratch_shapes=[pltpu.VMEM((tr + CONV_HALO, D_MODEL), F32),
                        pltpu.VMEM((SUBLANES - 1, CONV_SHIFT_ROWS, LANES), F32)],
        compiler_params=_params(("parallel", "arbitrary")),
    )(p3, p3, p3, p3, z.reshape(Bl, S, D_MODEL), dw, dwb, ln_g, ln_b)
    return u1.reshape(Bl * S, D_MODEL), out.reshape(Bl * S, D_MODEL)


def _conv_norm_bwd(da2, z, u1, ln_g, ln_b):
    T = da2.shape[0]

    def body(da_ref, z_ref, u_ref, g_ref, b_ref, du_ref, dz_ref, dg_ref, db_ref):
        @pl.when(pl.program_id(0) == 0)
        def _():
            dg_ref[...] = jnp.zeros_like(dg_ref)
            db_ref[...] = jnp.zeros_like(db_ref)

        u1 = u_ref[...]
        mu = jnp.mean(u1, axis=1, keepdims=True)
        xc = u1 - mu
        rstd = lax.rsqrt(jnp.mean(xc * xc, axis=1, keepdims=True) + LN_EPS)
        nrm = xc * rstd
        u2 = nrm * g_ref[...] + b_ref[...]
        s2 = _sigmoid(u2)
        zv = z_ref[...].astype(F32)
        sz = _sigmoid(zv)
        dv = da_ref[...]
        dz_ref[...] = (dv * u2 * s2 * sz * (1.0 + zv * (1.0 - sz))).astype(BF16)
        du2 = dv * zv * sz * s2 * (1.0 + u2 * (1.0 - s2))
        dg_ref[...] += jnp.sum(du2 * nrm, axis=0, keepdims=True)
        db_ref[...] += jnp.sum(du2, axis=0, keepdims=True)
        dn = du2 * g_ref[...]
        du_ref[...] = rstd * (dn - jnp.mean(dn, axis=1, keepdims=True)
                              - nrm * jnp.mean(dn * nrm, axis=1, keepdims=True))

    return _row_call(body, "conv_norm_bwd", T,
                     [_rows(), _rows(), _rows(), _full((1, D_MODEL)), _full((1, D_MODEL))],
                     [_rows(), _rows(), _full((1, D_MODEL)), _full((1, D_MODEL))],
                     [_sds((T, D_MODEL), F32), _sds((T, D_MODEL), BF16), _sds((1, D_MODEL), F32),
                      _sds((1, D_MODEL), F32)], (da2, z, u1, ln_g, ln_b))


def _conv_bwd(proj, du1, dw, Bl, S):
    tr = CONV_TILE
    nj = S // tr
    hb = tr // CONV_HALO

    def body(a_ref, b_ref, ah_ref, bh_ref, du_ref, duh_ref, dw_ref, dab_ref, ddw_ref, ddb_ref, uext, dext, du0,
             ushift, dshift, ddw8):
        first = (pl.program_id(0) == 0) & (pl.program_id(1) == 0)
        last = (pl.program_id(0) == Bl - 1) & (pl.program_id(1) == nj - 1)
        j = pl.program_id(1)

        @pl.when(first)
        def _():
            ddw8[...] = jnp.zeros_like(ddw8)
            ddb_ref[...] = jnp.zeros_like(ddb_ref)

        halo = ah_ref[0].astype(F32) * _sigmoid(bh_ref[0].astype(F32))
        uext[pl.ds(0, CONV_HALO), :] = jnp.where(j > 0, halo, 0.0)
        av = a_ref[0].astype(F32)
        sb = _sigmoid(b_ref[0].astype(F32))
        uext[pl.ds(CONV_HALO, tr), :] = av * sb
        dext[pl.ds(0, tr), :] = du_ref[0]
        dext[pl.ds(tr, CONV_HALO), :] = jnp.where(j < nj - 1, duh_ref[0], 0.0)
        ddb_ref[...] += jnp.sum(du_ref[0], axis=0, keepdims=True)

        def cols(c, carry):
            cs = pl.ds(pl.multiple_of(c * LANES, LANES), LANES)
            _fill_shifted(dshift, dext, cs)
            _fill_shifted(ushift, uext, cs)
            for rc in range(tr // CONV_CHUNK):
                base = rc * CONV_CHUNK
                acc = jnp.zeros((CONV_CHUNK, LANES), F32)
                for w in range(CONV_WIDTH):
                    acc = acc + dw_ref[pl.ds(w, 1), cs] * _shifted_rows(dshift, dext, cs, base + CONV_WIDTH - 1 - w)
                du0[pl.ds(base, CONV_CHUNK), cs] = acc
            for w in range(CONV_WIDTH):
                part = jnp.zeros((SUBLANES, LANES), F32)
                for rc in range(tr // CONV_CHUNK):
                    base = rc * CONV_CHUNK
                    prod = dext[pl.ds(base, CONV_CHUNK), cs] * _shifted_rows(
                        ushift, uext, cs, base + CONV_HALO - (CONV_WIDTH - 1) + w)
                    for i in range(CONV_CHUNK // SUBLANES):
                        part = part + prod[i * SUBLANES:(i + 1) * SUBLANES]
                ddw8[pl.ds(w * SUBLANES, SUBLANES), cs] += part
            return carry

        lax.fori_loop(0, D_MODEL // LANES, cols, 0)
        g = du0[...]
        dab_ref[0, :, 0:D_MODEL] = (g * sb).astype(BF16)
        dab_ref[0, :, D_MODEL:2 * D_MODEL] = (g * av * sb * (1.0 - sb)).astype(BF16)

        @pl.when(last)
        def _():
            for w in range(CONV_WIDTH + 1):
                ddw_ref[pl.ds(w, 1), :] = jnp.sum(ddw8[pl.ds(w * SUBLANES, SUBLANES), :], axis=0, keepdims=True)

    tile = lambda cb: pl.BlockSpec((1, tr, D_MODEL), lambda b, j: (b, j, cb))
    halo = lambda cb: pl.BlockSpec((1, CONV_HALO, D_MODEL), lambda b, j: (b, jnp.maximum(j * hb - 1, 0), cb))
    nxt = pl.BlockSpec((1, CONV_HALO, D_MODEL), lambda b, j: (b, jnp.minimum((j + 1) * hb, S // CONV_HALO - 1), 0))
    par = lambda r: pl.BlockSpec((r, D_MODEL), lambda b, j: (0, 0))
    p3 = proj.reshape(Bl, S, 2 * D_MODEL)
    d3 = du1.reshape(Bl, S, D_MODEL)
    dab, ddw, ddb = pl.pallas_call(
        body, name="conv_bwd", grid=(Bl, nj),
        in_specs=[tile(0), tile(1), halo(0), halo(1), tile(0), nxt, par(32)],
        out_specs=[pl.BlockSpec((1, tr, 2 * D_MODEL), lambda b, j: (b, j, 0)), par(32), par(1)],
        out_shape=[_sds((Bl, S, 2 * D_MODEL), BF16), _sds((32, D_MODEL), F32), _sds((1, D_MODEL), F32)],
        scratch_shapes=[pltpu.VMEM((tr + CONV_HALO, D_MODEL), F32), pltpu.VMEM((tr + CONV_HALO, D_MODEL), F32),
                        pltpu.VMEM((tr, D_MODEL), F32),
                        pltpu.VMEM((SUBLANES - 1, CONV_SHIFT_ROWS, LANES), F32),
                        pltpu.VMEM((SUBLANES - 1, CONV_SHIFT_ROWS, LANES), F32),
                        pltpu.VMEM(((CONV_WIDTH + 1) * SUBLANES, D_MODEL), F32)],
        compiler_params=_params(("arbitrary", "arbitrary")),
    )(p3, p3, p3, p3, d3, d3, dw)
    return dab.reshape(Bl * S, 2 * D_MODEL), ddw, ddb


_LAYOUT = (
    ("pre_norm_g", (4, 1024), None), ("post_norm_g", (4, 1024), None),
    ("attn_w_in", (2, 1024, 2560), 2), ("attn_w_out", (2, 256, 1024), 1),
    ("conv_w_in", (2, 1024, 768), 2), ("conv_dw_w", (2, 31, 256), 2),
    ("conv_dw_b", (2, 256), 1), ("conv_ln_g", (2, 256), 1), ("conv_ln_b", (2, 256), 1),
    ("conv_w_out", (2, 256, 1024), 1), ("ple_w_proj", (4, 256, 256), 2), ("ple_w_gate", (4, 256, 1024), 1),
)
_MATMUL_WEIGHTS = ("attn_w_in", "attn_w_out", "conv_w_in", "conv_w_out", "ple_w_proj", "ple_w_gate")
_FIRST_LAYER = ("attn_w_in", "attn_w_out", "ple_w_proj", "ple_w_gate")
_AXIS = {n: a for n, _, a in _LAYOUT}


def _size(shape):
    n = 1
    for s in shape:
        n *= s
    return n


def _padded_rows(shape):
    rows = _size(shape) // shape[-1]
    return rows + (-rows) % FLAT_ROW_ALIGN


def _rows2d(a):
    a2 = a.reshape(-1, a.shape[-1])
    pad = _padded_rows(a.shape) - a2.shape[0]
    return jnp.pad(a2, ((0, pad), (0, 0))) if pad else a2


def _col_blocks(a):
    a2 = _rows2d(a)
    return jnp.concatenate([a2[:, c:c + FLAT_COLS] for c in range(0, a2.shape[1], FLAT_COLS)], axis=0)


def _from_col_blocks(flat, off, shape):
    rows, nblk = _padded_rows(shape), shape[-1] // FLAT_COLS
    a2 = jnp.concatenate([flat[off + b * rows:off + (b + 1) * rows] for b in range(nblk)], axis=1)
    return a2[:_size(shape) // shape[-1]].reshape(shape), off + nblk * rows


def _shard_col_blocks(full, shape, axis):
    if axis is None:
        blocks = _col_blocks(full)
        return jnp.broadcast_to(blocks[None], (N_CHIPS,) + blocks.shape)
    m = shape[-1]
    if axis == len(shape) - 1:
        a2 = _rows2d(full)
        pieces = [a2[:, c:c + FLAT_COLS] for c in range(0, N_CHIPS * m, FLAT_COLS)]
    else:
        layers, r, _ = shape
        assert axis == 1 and (layers * r) % FLAT_ROW_ALIGN == 0
        pieces = [full[:, s * r:(s + 1) * r, c:c + FLAT_COLS].reshape(layers * r, FLAT_COLS)
                  for s in range(N_CHIPS) for c in range(0, m, FLAT_COLS)]
    return jnp.concatenate(pieces, axis=0).reshape(N_CHIPS, -1, FLAT_COLS)


_FLAT_BIG = ("attn_w_in", "conv_w_in", "ple_w_gate", "attn_w_out", "conv_w_out", "ple_w_proj")
_FLAT_SMALL = ("pre_norm_g", "post_norm_g", "conv_dw_w", "conv_dw_b", "conv_ln_g", "conv_ln_b")
_SHARD_SHAPE = {n: s for n, s, _ in _LAYOUT}
PACK_TILE = 1024


def _flat_offsets():
    out, off = {}, 0
    for n in _FLAT_BIG + _FLAT_SMALL:
        out[n] = off
        off += _padded_rows(_SHARD_SHAPE[n]) * (_SHARD_SHAPE[n][-1] // FLAT_COLS)
    return out, off


def _unpack_f32(flat):
    offsets, _ = _flat_offsets()
    return {n: _from_col_blocks(flat, offsets[n], _SHARD_SHAPE[n])[0] for n in _FLAT_BIG + _FLAT_SMALL}


def _pack_param(full, shape, axis, off, flat):
    layers, r, m = shape
    nblk = m // FLAT_COLS
    if axis == 2:
        rows = layers * r
        tr = min(PACK_TILE, rows)
        assert rows % tr == 0 and off % tr == 0
        src = full.reshape(rows, N_CHIPS * m)
        grid = (N_CHIPS * nblk, rows // tr)
        in_spec = pl.BlockSpec((tr, FLAT_COLS), lambda j, i: (i, j))
        out_spec = pl.BlockSpec((None, tr, FLAT_COLS), lambda j, i: (j // nblk, (off + (j % nblk) * rows) // tr + i, 0))
    else:
        assert axis == 1 and off % r == 0
        src = full.reshape(layers * N_CHIPS * r, m)
        grid = (layers, N_CHIPS, nblk)
        in_spec = pl.BlockSpec((r, FLAT_COLS), lambda l, s, b: (l * N_CHIPS + s, b))
        out_spec = pl.BlockSpec((None, r, FLAT_COLS), lambda l, s, b: (s, (off + b * layers * r) // r + l, 0))

    def copy_body(src_ref, *rest):
        rest[-1][...] = src_ref[...]

    args, in_specs, aliases = [src], [in_spec], {}
    if flat is not None:
        args.append(flat)
        in_specs.append(_ANY)
        aliases = {1: 0}
    return pl.pallas_call(
        copy_body, name="pack_grad", grid=grid, in_specs=in_specs, out_specs=out_spec,
        out_shape=_sds((N_CHIPS, FLAT_ROWS, FLAT_COLS), F32), input_output_aliases=aliases,
        compiler_params=_params(("arbitrary",) * len(grid)))(*args)


SMALL_ROWS = 40


def _stack_small(w):
    rows = [w["conv_dw_w"]] + [w[n][:, None, :] for n in ("conv_dw_b", "conv_ln_g", "conv_ln_b")]
    stacked = jnp.concatenate(rows, axis=1)
    return jnp.pad(stacked, ((0, 0), (0, SMALL_ROWS - stacked.shape[1]), (0, 0)))


def _unstack_small(small):
    return {"conv_dw_w": small[:, :CONV_WIDTH], "conv_dw_b": small[:, CONV_WIDTH],
            "conv_ln_g": small[:, CONV_WIDTH + 1], "conv_ln_b": small[:, CONV_WIDTH + 2]}


def _pack_full_grads(grads):
    offsets, end = _flat_offsets()
    assert end <= FLAT_ROWS
    flat = None
    for n in _FLAT_BIG:
        flat = _pack_param(grads[n], _SHARD_SHAPE[n], _AXIS[n], offsets[n], flat)
    small = jnp.concatenate([_shard_col_blocks(grads[n], _SHARD_SHAPE[n], _AXIS[n]) for n in _FLAT_SMALL], axis=1)
    start = offsets[_FLAT_SMALL[0]]
    small = jnp.pad(small, ((0, 0), (0, FLAT_ROWS - start - small.shape[1]), (0, 0)))
    return lax.dynamic_update_slice(flat, small, (0, start, 0))


_ANY = pl.BlockSpec(memory_space=pl.ANY)


def _mesh_pos():
    return lax.axis_index("x"), lax.axis_index("y"), lax.axis_index("c")


def _other_chips(x, y):
    return [(1 - x, y), (x, 1 - y), (1 - x, 1 - y)]


COPIES_PER_ARRAY = 7


def _allgather_weights(shards, axes, small):
    n = len(shards)
    full_shape = lambda a, axis: tuple(d * (N_CHIPS if i == axis else 1) for i, d in enumerate(a.shape))

    def body(*refs):
        ins, small_in = refs[:n], refs[n]
        outs, small_out = refs[n + 1:2 * n + 1], refs[2 * n + 1]
        send_sems, recv_sems = refs[2 * n + 2:]
        x, y, c = _mesh_pos()
        mine, me, sibling = 2 * x + y, (x, y, c), (x, y, 1 - c)
        chips = _other_chips(x, y)

        def region(a, chip, half):
            _, rows, cols = shards[a].shape
            h = rows // 2
            if axes[a] == 2:
                return outs[a].at[:, slice(None) if half is None else pl.ds(half * h, h), pl.ds(chip * cols, cols)]
            if half is None:
                return outs[a].at[:, pl.ds(chip * rows, rows), :]
            return outs[a].at[:, pl.ds(chip * rows + half * h, h), :]

        def copy(k, src, dst, to):
            return pltpu.make_async_remote_copy(src_ref=src, dst_ref=dst, send_sem=send_sems.at[k],
                                                recv_sem=recv_sems.at[k], device_id=to, device_id_type=MESH)

        def arrival(k, dst):
            return copy(k, dst, dst, me)

        sends = []
        for a in range(n):
            h = shards[a].shape[1] // 2
            base = a * COPIES_PER_ARRAY
            sends.append(copy(base + 6, ins[a], region(a, mine, None), sibling))
            for j, (cx, cy) in enumerate(chips):
                sends.append(copy(base + j, ins[a].at[:, pl.ds(c * h, h), :], region(a, mine, c), (cx, cy, c)))
        small_cols = small.shape[2]
        small_region = lambda chip: small_out.at[:, :, pl.ds(chip * small_cols, small_cols)]
        base = n * COPIES_PER_ARRAY
        sends.append(copy(base + 3, small_in, small_region(mine), sibling))
        for j, (cx, cy) in enumerate(chips):
            sends.append(copy(base + j, small_in, small_region(mine), (cx, cy, c)))
        for cp in sends:
            cp.start()
        for j, (cx, cy) in enumerate(chips):
            for a in range(n):
                k = a * COPIES_PER_ARRAY + j
                arrival(k, region(a, 2 * cx + cy, c)).wait_recv()
                passed = copy(k + 3, region(a, 2 * cx + cy, c), region(a, 2 * cx + cy, c), sibling)
                passed.start()
                sends.append(passed)
        for j, (cx, cy) in enumerate(chips):
            for a in range(n):
                arrival(a * COPIES_PER_ARRAY + 3 + j, region(a, 2 * cx + cy, 1 - c)).wait_recv()
            arrival(base + j, small_region(2 * cx + cy)).wait_recv()
        for a in range(n):
            arrival(a * COPIES_PER_ARRAY + 6, region(a, mine, None)).wait_recv()
        arrival(base + 3, small_region(mine)).wait_recv()
        for cp in sends:
            cp.wait_send()

    n_sems = n * COPIES_PER_ARRAY + 4
    out = pl.pallas_call(
        body, name="allgather_weights", in_specs=[_ANY] * (n + 1), out_specs=[_ANY] * (n + 1),
        out_shape=[_sds(full_shape(a, axis), a.dtype) for a, axis in zip(shards, axes)]
        + [_sds(full_shape(small, 2), small.dtype)],
        scratch_shapes=[pltpu.SemaphoreType.DMA((n_sems,)), pltpu.SemaphoreType.DMA((n_sems,))],
    )(*shards, small)
    return out[:n], out[n]


_HBM = pl.BlockSpec(memory_space=pltpu.HBM)
_SEM = pl.BlockSpec(memory_space=pltpu.SEMAPHORE)


def _full_shape(shard, axis):
    return tuple(d * (N_CHIPS if i == axis else 1) for i, d in enumerate(shard.shape))


def _direct_gather_copies(srcs, lands, shapes, axes, send_sems, recv_sems):
    x, y, c = _mesh_pos()
    mine, me, sibling = 2 * x + y, (x, y, c), (x, y, 1 - c)

    def region(a, chip, half):
        _, rows, cols = shapes[a]
        h = rows // 2
        if axes[a] == 2:
            return lands[a].at[:, slice(None) if half is None else pl.ds(half * h, h), pl.ds(chip * cols, cols)]
        if half is None:
            return lands[a].at[:, pl.ds(chip * rows, rows), :]
        return lands[a].at[:, pl.ds(chip * rows + half * h, h), :]

    def copy(k, src, dst, to):
        return pltpu.make_async_remote_copy(src_ref=src, dst_ref=dst, send_sem=send_sems.at[k],
                                            recv_sem=recv_sems.at[k], device_id=to, device_id_type=MESH)

    outgoing, incoming = [], []
    for a in range(len(srcs)):
        h = shapes[a][1] // 2
        base = a * COPIES_PER_ARRAY
        for j, (cx, cy) in enumerate(_other_chips(x, y)):
            for t, ct in enumerate((c, 1 - c)):
                k = base + 2 * j + t
                outgoing.append(copy(k, srcs[a].at[:, pl.ds(c * h, h), :], region(a, mine, c), (cx, cy, ct)))
                landed = region(a, 2 * cx + cy, ct)
                incoming.append(copy(k, landed, landed, me))
        outgoing.append(copy(base + 6, srcs[a], region(a, mine, None), sibling))
        incoming.append(copy(base + 6, region(a, mine, None), region(a, mine, None), me))
    return outgoing, incoming


def _allgather_start(shards, axes):
    n = len(shards)
    shapes = [s.shape for s in shards]

    def body(*refs):
        srcs, lands = refs[:n], refs[n:2 * n]
        send_sems, recv_sems = refs[2 * n], refs[2 * n + 1]
        token = refs[-1]
        outgoing, _ = _direct_gather_copies(srcs, lands, shapes, axes, send_sems, recv_sems)
        for cp in outgoing:
            cp.start()
        token[...] = jnp.zeros_like(token)

    n_sems = n * COPIES_PER_ARRAY
    zones = [pltpu.with_memory_space_constraint(lax.empty(_full_shape(s, ax), s.dtype), pltpu.HBM)
             for s, ax in zip(shards, axes)]
    out = pl.pallas_call(
        body, name="allgather_rest_start",
        out_shape=(pltpu.SemaphoreType.DMA((n_sems,)), pltpu.SemaphoreType.DMA((n_sems,)),
                   *[pltpu.HBM(s.shape, s.dtype) for s in shards], *[pltpu.HBM(z.shape, z.dtype) for z in zones],
                   jax.ShapeDtypeStruct((8, LANES), F32)),
        in_specs=[_HBM] * (2 * n),
        out_specs=(_SEM, _SEM, *[_HBM] * (2 * n), pl.BlockSpec(memory_space=pltpu.VMEM)),
        input_output_aliases={i: 2 + i for i in range(2 * n)},
        compiler_params=pltpu.CompilerParams(has_side_effects=pltpu.SideEffectType.DATAFLOW_SIDE_EFFECTING),
    )(*[pltpu.with_memory_space_constraint(s, pltpu.HBM) for s in shards], *zones)
    return out[0], out[1], out[2:2 + n], out[2 + n:2 + 2 * n], out[-1]


def _allgather_wait(send_sems, recv_sems, shards, zones, axes, after):
    n = len(shards)
    shapes = [s.shape for s in shards]

    def body(*refs):
        srcs, lands = refs[:n], refs[n:2 * n]
        outgoing, incoming = _direct_gather_copies(srcs, lands, shapes, axes, refs[2 * n], refs[2 * n + 1])
        for cp in outgoing:
            cp.wait_send()
        for cp in incoming:
            cp.wait_recv()

    out = pl.pallas_call(
        body, name="allgather_rest_wait",
        out_shape=(*[pltpu.HBM(s.shape, s.dtype) for s in shards], *[pltpu.HBM(z.shape, z.dtype) for z in zones]),
        in_specs=[_HBM] * (2 * n) + [_SEM, _SEM, _ANY], out_specs=[_HBM] * (2 * n),
        input_output_aliases={i: i for i in range(2 * n)},
        compiler_params=pltpu.CompilerParams(has_side_effects=pltpu.SideEffectType.DATAFLOW_SIDE_EFFECTING),
    )(*shards, *zones, send_sems, recv_sems, after)
    return out[n:]


def _exchange_core_halves(g):
    n, _, H, C = g.shape

    def body(g_ref, got_ref, send_sem, recv_sem):
        x, y, c = _mesh_pos()
        swap = pltpu.make_async_remote_copy(
            src_ref=g_ref.at[pl.ds(0, n), 1 - c], dst_ref=got_ref, send_sem=send_sem, recv_sem=recv_sem,
            device_id=(x, y, 1 - c), device_id_type=MESH)
        swap.start()
        swap.wait()

    return pl.pallas_call(
        body, name="exchange_core_halves", in_specs=[_ANY], out_specs=_ANY,
        out_shape=_sds((n, H, C), g.dtype),
        scratch_shapes=[pltpu.SemaphoreType.DMA, pltpu.SemaphoreType.DMA],
    )(g)


def _scatter_to_chips(p):
    n, H, C = p.shape

    def body(p_ref, q_ref, send_sems, recv_sems):
        x, y, c = _mesh_pos()
        chips = _other_chips(x, y)
        sends = [pltpu.make_async_remote_copy(
            src_ref=p_ref.at[2 * cx + cy], dst_ref=q_ref.at[j], send_sem=send_sems.at[j],
            recv_sem=recv_sems.at[j], device_id=(cx, cy, c), device_id_type=MESH)
            for j, (cx, cy) in enumerate(chips)]
        for cp in sends:
            cp.start()
        for cp in sends:
            cp.wait_recv()
        for cp in sends:
            cp.wait_send()

    return pl.pallas_call(
        body, name="scatter_to_chips", in_specs=[_ANY], out_specs=_ANY,
        out_shape=_sds((n - 1, H, C), p.dtype),
        scratch_shapes=[pltpu.SemaphoreType.DMA((3,)), pltpu.SemaphoreType.DMA((3,))],
    )(p)


def _share_core_halves(r2):
    _, H, C = r2.shape

    def body(r_ref, out_ref, send_sem, recv_sem):
        x, y, c = _mesh_pos()
        send = pltpu.make_async_remote_copy(
            src_ref=r_ref.at[c], dst_ref=out_ref.at[c], send_sem=send_sem, recv_sem=recv_sem,
            device_id=(x, y, 1 - c), device_id_type=MESH)
        send.start()
        send.wait_send()
        pltpu.make_async_remote_copy(
            src_ref=r_ref.at[c], dst_ref=out_ref.at[1 - c], send_sem=send_sem, recv_sem=recv_sem,
            device_id=(x, y, 1 - c), device_id_type=MESH).wait_recv()

    return pl.pallas_call(
        body, name="share_core_halves", in_specs=[_ANY], out_specs=_ANY,
        out_shape=_sds(r2.shape, r2.dtype), input_output_aliases={0: 0},
        scratch_shapes=[pltpu.SemaphoreType.DMA, pltpu.SemaphoreType.DMA],
    )(r2)


def _place():
    x, y, c = _mesh_pos()
    return jnp.stack([c, 2 * x + y]).astype(jnp.int32)


def _sum_pair(g, got, place):
    n, _, H, C = g.shape

    def body(place_ref, a_ref, b_ref, o_ref):
        o_ref[...] = (a_ref[...] + b_ref[...]).astype(BF16)

    spec = pl.BlockSpec((1, FLAT_TILE, C), lambda s, i, pr: (s, i, 0))
    return pl.pallas_call(
        body, name="sum_core_pair",
        grid_spec=pltpu.PrefetchScalarGridSpec(
            num_scalar_prefetch=1, grid=(n, H // FLAT_TILE),
            in_specs=[pl.BlockSpec((1, None, FLAT_TILE, C), lambda s, i, pr: (s, pr[0], i, 0)), spec],
            out_specs=spec),
        out_shape=_sds((n, H, C), BF16),
        compiler_params=_params(("parallel", "parallel")))(place, g, got)


def _sum_chips(p, q, place):
    n, H, C = p.shape

    def body(place_ref, own_ref, qx_ref, qy_ref, qxy_ref, o_ref):
        mine = place_ref[1]
        own, qx, qy, qxy = (t[0].astype(F32) for t in (own_ref, qx_ref, qy_ref, qxy_ref))

        def term(s):
            rel = jnp.full(own.shape, mine ^ s, jnp.int32)
            return jnp.where(rel == 0, own, jnp.where(rel == 2, qx, jnp.where(rel == 1, qy, qxy)))

        o_ref[0] = ((term(0) + term(1)) + term(2)) + term(3)

    qspec = lambda j: pl.BlockSpec((1, FLAT_TILE, C), lambda i, pr: (j, i, 0))
    return pl.pallas_call(
        body, name="sum_chips",
        grid_spec=pltpu.PrefetchScalarGridSpec(
            num_scalar_prefetch=1, grid=(H // FLAT_TILE,),
            in_specs=[pl.BlockSpec((1, FLAT_TILE, C), lambda i, pr: (pr[1], i, 0)), qspec(0), qspec(1), qspec(2)],
            out_specs=pl.BlockSpec((1, FLAT_TILE, C), lambda i, pr: (pr[0], i, 0))),
        out_shape=_sds((2, H, C), F32),
        compiler_params=_params(("parallel",)))(place, p, q, q, q)


ADAMW_BLOCK = 1 << 18


def _adamw(w, g, m, v):
    shape = w.shape
    C = shape[-1]
    R = _size(shape) // C
    tr = R
    while tr * C > ADAMW_BLOCK and tr % 16 == 0:
        tr //= 2
    w, g, m, v = (t.reshape(R, C) for t in (w, g, m, v))

    def body(w_ref, g_ref, m_ref, v_ref, d_ref, nm_ref, nv_ref):
        gv = g_ref[...]
        nm = ADAM_B1 * m_ref[...] + (1.0 - ADAM_B1) * gv
        nv = ADAM_B2 * v_ref[...] + (1.0 - ADAM_B2) * (gv * gv)
        m_hat = nm / (1.0 - ADAM_B1 ** ADAM_STEP)
        v_hat = nv / (1.0 - ADAM_B2 ** ADAM_STEP)
        d_ref[...] = -ADAM_LR * (m_hat / (jnp.sqrt(v_hat) + ADAM_EPS) + ADAM_WD * w_ref[...])
        nm_ref[...] = nm
        nv_ref[...] = nv

    spec = pl.BlockSpec((tr, C), lambda i: (i, 0))
    outs = pl.pallas_call(body, name="adamw", grid=(R // tr,), in_specs=[spec] * 4, out_specs=[spec] * 3,
                          out_shape=[_sds((R, C), F32)] * 3, compiler_params=_params(("parallel",)))(w, g, m, v)
    return tuple(t.reshape(shape) for t in outs)


def _reduce_scatter_grads(gfull):
    n, R, C = gfull.shape
    place = _place()
    g4 = gfull.reshape(n, 2, R // 2, C)
    p = _sum_pair(g4, _exchange_core_halves(g4), place)
    return _share_core_halves(_sum_chips(p, _scatter_to_chips(p), place)).reshape(R, C)


def _local_step(x, p, positions, loss_target, pre_g, post_g, w, later_weights):
    Bl, S, _ = x.shape
    T = Bl * S
    cos, sin = _rope_tables(positions)
    xs = x.reshape(T, D_MODEL)
    saved = []
    for i in range(DEPTH):
        if i == 1:
            later_weights(xs)
        j = i // 2
        g_pre, g_post = pre_g[i:i + 1], post_g[i:i + 1]
        h = _rmsnorm_fwd(xs, g_pre)
        st = {"x": xs, "h": h}
        if i % 2 == 0:
            proj = _mm(h, w["attn_w_in"][j], name="attn_in")
            res = [_attn_fwd(proj, cos, sin, g, Bl, S) for g in range(N_GROUPS)]
            a, o, lse = _attn_combine([r[0] for r in res], [r[1] for r in res], proj)
            w_out = w["attn_w_out"][j]
            st.update(proj=proj, a=a, o=o, lse=lse, qkv=[r[2] for r in res])
        else:
            w_ab, w_z = w["conv_w_in"][j][:, :2 * D_MODEL], w["conv_w_in"][j][:, 2 * D_MODEL:]
            ab = _mm(h, w_ab, out_dtype=BF16, name="conv_in_ab")
            z = _mm(h, w_z, out_dtype=BF16, name="conv_in_z")
            dw = jnp.pad(w["conv_dw_w"][j], ((0, 1), (0, 0)))
            u1, a = _conv_fwd(ab, z, dw, w["conv_dw_b"][j:j + 1], w["conv_ln_g"][j:j + 1],
                              w["conv_ln_b"][j:j + 1], Bl, S)
            w_out = w["conv_w_out"][j]
            st.update(w_ab=w_ab, w_z=w_z, ab=ab, z=z, dw=dw, u1=u1, a=a)
        y, x1 = _mm_rows(a, w_out, [xs, g_post], _post_epilogue, (F32, F32), "branch_out_post")
        pi = p[i].reshape(T, PLE_DIM)
        pe = _mm(pi, w["ple_w_proj"][i], name="ple_proj")
        gl, xs = _mm_rows(x1, w["ple_w_gate"][i], [pe], _ple_epilogue, (F32, F32), "ple_gate_fwd")
        st.update(y=y, x1=x1, pi=pi, pe=pe, gl=gl)
        saved.append(st)

    sq, dx = _loss_fwd_bwd(xs, loss_target.reshape(T, D_MODEL))

    grads = {n: [None] * shape[0] for n, shape, _ in _LAYOUT}
    for i in reversed(range(DEPTH)):
        j = i // 2
        st = saved[i]
        g_pre, g_post = pre_g[i:i + 1], post_g[i:i + 1]
        dpe, dgl, dx1, dy, dg_post = _ple_post_bwd(dx, st["pe"], st["gl"], w["ple_w_gate"][i], st["y"], g_post)
        grads["ple_w_proj"][i] = _mm(st["pi"], dpe, ta=True, name="ple_proj_wgrad")
        grads["ple_w_gate"][i] = _mm(st["x1"], dgl, ta=True, name="ple_gate_wgrad")
        grads["post_norm_g"][i] = dg_post[0]
        if i % 2 == 0:
            grads["attn_w_out"][j] = _mm(st["a"], dy, ta=True, name="attn_out_wgrad")
            da = _mm(dy, w["attn_w_out"][j], tb=True, name="attn_out_dgrad")
            do, dproj = _gate_bwd(da, st["o"], st["proj"])
            for g in range(N_GROUPS):
                dproj = _attn_bwd(st["qkv"][g], cos, sin, do, st["o"], st["lse"], dproj, g, Bl, S)
            dx, dg_pre = _dgrad_pre_bwd(dproj, w["attn_w_in"][j], st["x"], g_pre, dx1, None, 2048, "attn_in_dgrad_pre")
            grads["attn_w_in"][j] = _mm(st["h"], dproj, ta=True, name="attn_in_wgrad")
        else:
            grads["conv_w_out"][j] = _mm(st["a"], dy, ta=True, name="conv_out_wgrad")
            da2 = _mm(dy, w["conv_w_out"][j], tb=True, name="conv_out_dgrad")
            du1, dz, dln_g, dln_b = _conv_norm_bwd(da2, st["z"], st["u1"], w["conv_ln_g"][j:j + 1],
                                                   w["conv_ln_b"][j:j + 1])
            dab, ddw, ddb = _conv_bwd(st["ab"], du1, st["dw"], Bl, S)
            dh = _mm(dz, st["w_z"], tb=True, name="conv_in_z_dgrad")
            dx, dg_pre = _dgrad_pre_bwd(dab, st["w_ab"], st["x"], g_pre, dx1, dh, 2048, "conv_in_dgrad_pre")
            dw_ab = _mm(st["h"], dab, ta=True, name="conv_in_ab_wgrad")
            dw_z = _mm(st["h"], dz, ta=True, name="conv_in_z_wgrad")
            grads["conv_w_in"][j] = jnp.concatenate([dw_ab, dw_z], axis=1)
            grads["conv_dw_w"][j] = ddw[:CONV_WIDTH]
            grads["conv_dw_b"][j] = ddb[0]
            grads["conv_ln_g"][j] = dln_g[0]
            grads["conv_ln_b"][j] = dln_b[0]
        grads["pre_norm_g"][i] = dg_pre[0]
    grads = {n: jnp.stack(v) for n, v in grads.items()}
    return sq, dx.reshape(Bl, S, D_MODEL), grads


_NAMES = tuple(n for n, _, _ in _LAYOUT)


def kernel(x, p, positions, pre_norm_g, post_norm_g, attn_w_in, attn_w_out, conv_w_in, conv_dw_w, conv_dw_b, conv_ln_g, conv_ln_b, conv_w_out, ple_w_proj, ple_w_gate, loss_target, m_pre_norm_g, m_post_norm_g, m_attn_w_in, m_attn_w_out, m_conv_w_in, m_conv_dw_w, m_conv_dw_b, m_conv_ln_g, m_conv_ln_b, m_conv_w_out, m_ple_w_proj, m_ple_w_gate, v_pre_norm_g, v_post_norm_g, v_attn_w_in, v_attn_w_out, v_conv_w_in, v_conv_dw_w, v_conv_dw_b, v_conv_ln_g, v_conv_ln_b, v_conv_w_out, v_ple_w_proj, v_ple_w_gate):
    w_loc = dict(zip(_NAMES, (pre_norm_g, post_norm_g, attn_w_in, attn_w_out, conv_w_in, conv_dw_w, conv_dw_b,
                              conv_ln_g, conv_ln_b, conv_w_out, ple_w_proj, ple_w_gate)))
    m_loc = dict(zip(_NAMES, (m_pre_norm_g, m_post_norm_g, m_attn_w_in, m_attn_w_out, m_conv_w_in, m_conv_dw_w,
                              m_conv_dw_b, m_conv_ln_g, m_conv_ln_b, m_conv_w_out, m_ple_w_proj, m_ple_w_gate)))
    v_loc = dict(zip(_NAMES, (v_pre_norm_g, v_post_norm_g, v_attn_w_in, v_attn_w_out, v_conv_w_in, v_conv_dw_w,
                              v_conv_dw_b, v_conv_ln_g, v_conv_ln_b, v_conv_w_out, v_ple_w_proj, v_ple_w_gate)))

    bf = {n: w_loc[n].astype(BF16) for n in _MATMUL_WEIGHTS}
    axes = [_AXIS[n] for n in _MATMUL_WEIGHTS]
    first, small = _allgather_weights([bf[n][:1] for n in _FIRST_LAYER], [_AXIS[n] for n in _FIRST_LAYER],
                                      _stack_small(w_loc))
    send_sems, recv_sems, shards, zones, token = _allgather_start(
        [bf[n][1:] if n in _FIRST_LAYER else bf[n] for n in _MATMUL_WEIGHTS], axes)
    w_full = dict({n: [full[0]] for n, full in zip(_FIRST_LAYER, first)}, **_unstack_small(small))

    def later_weights(after):
        for n, full in zip(_MATMUL_WEIGHTS, _allgather_wait(send_sems, recv_sems, shards, zones, axes, after)):
            w_full[n] = w_full[n] + [full[l] for l in range(full.shape[0])] if n in _FIRST_LAYER else full

    sq, grad_x, grads = _local_step(x, p, positions, loss_target, pre_norm_g + token[0, 0], post_norm_g,
                                    w_full, later_weights)
    loss = lax.psum(sq[0, 0] * (0.5 / D_MODEL), ("x", "y", "c"))

    g_flat = _reduce_scatter_grads(_pack_full_grads(grads))
    g_out = _unpack_f32(g_flat)
    updates = {n: _adamw(w_loc[n], g_out[n], m_loc[n], v_loc[n]) for n in _NAMES}
    d_out, m_out, v_out = ({n: updates[n][k] for n in _NAMES} for k in range(3))
    return (loss, grad_x, *[g_out[n] for n in _NAMES], *[d_out[n] for n in _NAMES],
            *[m_out[n] for n in _NAMES], *[v_out[n] for n in _NAMES])
```

```python
import math

import jax
import jax.numpy as jnp
from jax import lax
from jax.experimental import pallas as pl
from jax.experimental.pallas import tpu as pltpu

F32 = jnp.float32
BF16 = jnp.bfloat16

D_MODEL = 1024
DEPTH = 4
PLE_DIM = 256
HEAD_DIM = 64
WIN_DIL = ((128, 1), (512, 4), (2048, 16))
N_GROUPS = 3
N_BACK = 128
BLOCK_UNROLL = 8
RESIDUES_TOGETHER = 2
ROPE_THETA = 10000.0
CONV_WIDTH = 31
CONV_HALO = 32
RMS_EPS = 1e-6
LN_EPS = 1e-5
NEG_INF = -1e30
ADAM_LR, ADAM_B1, ADAM_B2, ADAM_EPS, ADAM_WD, ADAM_STEP = 0.001, 0.9, 0.999, 1e-08, 0.01, 10

LANES = 128
N_CHIPS = 4
VMEM_LIMIT = 48 * 1024 * 1024
VMEM_LIMIT_ATTN = 56 * 1024 * 1024
FLAT_COLS = 256
FLAT_TILE = 2048
FLAT_ROW_ALIGN = 16
PROJ_COLS = (3 * N_GROUPS + 1) * D_MODEL
HEAD_PAIRS = D_MODEL // LANES

MESH = pl.DeviceIdType.MESH


def _params(sem=None, vmem=VMEM_LIMIT):
    return pltpu.CompilerParams(dimension_semantics=sem, vmem_limit_bytes=vmem)


def _sigmoid(v):
    return 1.0 / (1.0 + jnp.exp(-v))


def _mm(a, b, *, ta=False, tb=False, out_dtype=F32, tm=1024, tn=1024, tk=1024, name="mm"):
    if ta:
        K, M = a.shape
    else:
        M, K = a.shape
    if tb:
        N, K2 = b.shape
    else:
        K2, N = b.shape
    assert K == K2, (a.shape, b.shape)
    tm, tn, tk = min(tm, M), min(tn, N), min(tk, K)
    assert M % tm == 0 and N % tn == 0 and K % tk == 0
    nk = K // tk
    dims = (((0 if ta else 1,), (1 if tb else 0,)), ((), ()))

    def body(a_ref, b_ref, o_ref, *scratch):
        k = pl.program_id(2)
        part = lax.dot_general(a_ref[...].astype(BF16), b_ref[...].astype(BF16), dims, preferred_element_type=F32)
        if nk == 1:
            o_ref[...] = part.astype(out_dtype)
        else:
            acc_ref, = scratch

            @pl.when(k == 0)
            def _():
                acc_ref[...] = part

            @pl.when((k > 0) & (k < nk - 1))
            def _():
                acc_ref[...] += part

            @pl.when(k == nk - 1)
            def _():
                o_ref[...] = (acc_ref[...] + part).astype(out_dtype)

    a_spec = pl.BlockSpec((tk, tm), lambda i, j, k: (k, i)) if ta else pl.BlockSpec((tm, tk), lambda i, j, k: (i, k))
    b_spec = pl.BlockSpec((tn, tk), lambda i, j, k: (j, k)) if tb else pl.BlockSpec((tk, tn), lambda i, j, k: (k, j))
    return pl.pallas_call(
        body, name=name, grid=(M // tm, N // tn, nk),
        in_specs=[a_spec, b_spec], out_specs=pl.BlockSpec((tm, tn), lambda i, j, k: (i, j)),
        out_shape=jax.ShapeDtypeStruct((M, N), out_dtype),
        scratch_shapes=[pltpu.VMEM((tm, tn), F32)] if nk > 1 else [],
        compiler_params=_params(("parallel", "parallel", "arbitrary")),
    )(a, b)


def _mm_rows(a, b, extras, epilogue, out_dtypes, name, tm=512):
    M, K = a.shape
    N = b.shape[1]
    n_ex = len(extras)

    def body(*refs):
        a_ref, b_ref = refs[:2]
        av = a_ref[...]
        acc = jnp.dot(av.astype(BF16), b_ref[...].astype(BF16), preferred_element_type=F32)
        results = epilogue(acc, av, *[e[...] for e in refs[2:2 + n_ex]])
        for o_ref, r in zip(refs[2 + n_ex:], results):
            o_ref[...] = r.astype(o_ref.dtype)

    tile = pl.BlockSpec((tm, N), lambda i: (i, 0))
    in_specs = [pl.BlockSpec((tm, K), lambda i: (i, 0)), pl.BlockSpec((K, N), lambda i: (0, 0))]
    in_specs += [tile if e.shape[0] == M else pl.BlockSpec((1, N), lambda i: (0, 0)) for e in extras]
    return pl.pallas_call(
        body, name=name, grid=(M // tm,), in_specs=in_specs, out_specs=[tile] * len(out_dtypes),
        out_shape=[jax.ShapeDtypeStruct((M, N), dt) for dt in out_dtypes],
        compiler_params=_params(("parallel",)),
    )(a, b, *extras)


ROW_TILE = 512


def _rows(w=D_MODEL, cb=0, tr=ROW_TILE):
    return pl.BlockSpec((tr, w), lambda i: (i, cb))


def _full(shape):
    return pl.BlockSpec(shape, lambda i: (0,) * len(shape))


def _row_call(body, name, T, in_specs, out_specs, out_shape, args, tr=ROW_TILE):
    return pl.pallas_call(body, name=name, grid=(T // tr,), in_specs=in_specs, out_specs=out_specs,
                          out_shape=out_shape, compiler_params=_params(("arbitrary",)))(*args)


def _sds(shape, dtype):
    return jax.ShapeDtypeStruct(shape, dtype)


def _rmsnorm_fwd(x, g):
    T = x.shape[0]

    def body(x_ref, g_ref, h_ref):
        xv = x_ref[...]
        r = lax.rsqrt(jnp.mean(xv * xv, axis=1, keepdims=True) + RMS_EPS)
        h_ref[...] = (xv * r * g_ref[...]).astype(BF16)

    return _row_call(body, "rmsnorm_fwd", T, [_rows(), _full((1, D_MODEL))], _rows(),
                     _sds((T, D_MODEL), BF16), (x, g))


def _post_epilogue(y, a_tile, x, g):
    del a_tile
    return y, x + y * lax.rsqrt(jnp.mean(y * y, axis=1, keepdims=True) + RMS_EPS) * g


def _ple_epilogue(gl, x1, pe):
    return gl, x1 + pe * _sigmoid(gl)


def _dgrad_pre_bwd(a, b, x, g, dx1, add, tk, name, tm=512):
    M, K = a.shape
    N = b.shape[0]
    tk = min(tk, K)
    assert M % tm == 0 and K % tk == 0 and N == D_MODEL
    nk = K // tk

    def body(*refs):
        a_ref, b_ref, x_ref, g_ref, dx1_ref = refs[:5]
        add_ref = refs[5] if add is not None else None
        dx_ref, dg_ref = refs[-3:-1] if nk > 1 else refs[-2:]
        i, k = pl.program_id(0), pl.program_id(1)

        @pl.when((i == 0) & (k == 0))
        def _():
            dg_ref[...] = jnp.zeros_like(dg_ref)

        part = lax.dot_general(a_ref[...].astype(BF16), b_ref[...].astype(BF16), _NT, preferred_element_type=F32)

        def finish(dh):
            if add is not None:
                dh = dh + add_ref[...]
            xv = x_ref[...]
            r = lax.rsqrt(jnp.mean(xv * xv, axis=1, keepdims=True) + RMS_EPS)
            xh = xv * r
            dg_ref[...] += jnp.sum(dh * xh, axis=0, keepdims=True)
            dn = dh * g_ref[...]
            dx_ref[...] = dx1_ref[...] + r * (dn - xh * jnp.mean(dn * xh, axis=1, keepdims=True))

        if nk == 1:
            finish(part)
        else:
            acc_ref = refs[-1]

            @pl.when(k == 0)
            def _():
                acc_ref[...] = part

            @pl.when((k > 0) & (k < nk - 1))
            def _():
                acc_ref[...] += part

            @pl.when(k == nk - 1)
            def _():
                finish(acc_ref[...] + part)

    tile = pl.BlockSpec((tm, N), lambda i, k: (i, 0))
    row = pl.BlockSpec((1, N), lambda i, k: (0, 0))
    in_specs = [pl.BlockSpec((tm, tk), lambda i, k: (i, k)), pl.BlockSpec((N, tk), lambda i, k: (0, k)), tile, row, tile]
    args = [a, b, x, g, dx1]
    if add is not None:
        in_specs.append(tile)
        args.append(add)
    return pl.pallas_call(
        body, name=name, grid=(M // tm, nk), in_specs=in_specs, out_specs=[tile, row],
        out_shape=[_sds((M, N), F32), _sds((1, N), F32)],
        scratch_shapes=[pltpu.VMEM((tm, N), F32)] if nk > 1 else [],
        compiler_params=_params(("arbitrary", "arbitrary")),
    )(*args)


def _ple_post_bwd(dx2, pe, gl, w_gate, y, g_post):
    T = dx2.shape[0]

    def body(d_ref, pe_ref, gl_ref, w_ref, y_ref, g_ref, dpe_ref, dgl_ref, dx1_ref, dy_ref, dg_ref):
        @pl.when(pl.program_id(0) == 0)
        def _():
            dg_ref[...] = jnp.zeros_like(dg_ref)

        dv = d_ref[...]
        sg = _sigmoid(gl_ref[...])
        dpe_ref[...] = (dv * sg).astype(BF16)
        dgl = (dv * pe_ref[...] * sg * (1.0 - sg)).astype(BF16)
        dgl_ref[...] = dgl
        dx1 = dv + lax.dot_general(dgl, w_ref[...], _NT, preferred_element_type=F32)
        dx1_ref[...] = dx1
        yv = y_ref[...]
        r = lax.rsqrt(jnp.mean(yv * yv, axis=1, keepdims=True) + RMS_EPS)
        yh = yv * r
        dg_ref[...] += jnp.sum(dx1 * yh, axis=0, keepdims=True)
        dn = dx1 * g_ref[...]
        dy_ref[...] = (r * (dn - yh * jnp.mean(dn * yh, axis=1, keepdims=True))).astype(BF16)

    row = _full((1, D_MODEL))
    return _row_call(body, "ple_post_bwd", T, [_rows()] * 3 + [_full((D_MODEL, D_MODEL)), _rows(), row],
                     [_rows()] * 4 + [row],
                     [_sds((T, D_MODEL), BF16)] * 2 + [_sds((T, D_MODEL), F32), _sds((T, D_MODEL), BF16),
                                                      _sds((1, D_MODEL), F32)],
                     (dx2, pe, gl, w_gate, y, g_post))


def _loss_fwd_bwd(y, target):
    T = y.shape[0]

    def body(y_ref, t_ref, s_ref, d_ref):
        @pl.when(pl.program_id(0) == 0)
        def _():
            s_ref[...] = jnp.zeros_like(s_ref)

        e = y_ref[...] - t_ref[...]
        s_ref[...] += jnp.sum(e * e).reshape(1, 1)
        d_ref[...] = e * (1.0 / D_MODEL)

    return _row_call(body, "loss", T, [_rows()] * 2, [_full((1, 1)), _rows()],
                     [_sds((1, 1), F32), _sds((T, D_MODEL), F32)], (y, target))


def _attn_combine(outs, lses, proj):
    T = proj.shape[0]

    def body(o0, o1, o2, l0, l1, l2, z_ref, a_ref, o_ref, lse_ref):
        a0, a1, a2 = l0[...], l1[...], l2[...]
        m = jnp.maximum(jnp.maximum(a0, a1), a2)
        e0, e1, e2 = jnp.exp(a0 - m), jnp.exp(a1 - m), jnp.exp(a2 - m)
        ssum = e0 + e1 + e2
        o = (e0 * o0[...] + e1 * o1[...] + e2 * o2[...]) / ssum
        zv = z_ref[...].astype(F32)
        o_ref[...] = o
        lse_ref[...] = m + jnp.log(ssum)
        a_ref[...] = (o * zv * _sigmoid(zv)).astype(BF16)

    return _row_call(body, "attn_combine", T, [_rows()] * 6 + [_rows(cb=3 * N_GROUPS)], [_rows()] * 3,
                     [_sds((T, D_MODEL), BF16), _sds((T, D_MODEL), F32), _sds((T, D_MODEL), F32)],
                     (*outs, *lses, proj))


def _gate_bwd(da, o, proj):
    T = da.shape[0]

    def body(da_ref, o_ref, z_ref, do_ref, dz_ref):
        dv = da_ref[...]
        zv = z_ref[...]
        sg = _sigmoid(zv)
        do_ref[...] = dv * zv * sg
        dz_ref[...] = dv * o_ref[...] * sg * (1.0 + zv * (1.0 - sg))

    zcols = _rows(cb=3 * N_GROUPS)
    return _row_call(body, "gate_bwd", T, [_rows(), _rows(), zcols], [_rows(), zcols],
                     [_sds((T, D_MODEL), F32), _sds((T, PROJ_COLS), F32)], (da, o, proj))


def _rope_tables(positions):
    inv_freq = 1.0 / (ROPE_THETA ** (jnp.arange(0, HEAD_DIM, 2, dtype=F32) / HEAD_DIM))
    ang = positions.astype(F32)[..., None] * inv_freq
    cos, sin = jnp.cos(ang), jnp.sin(ang)
    return jnp.tile(cos, (1, 1, 4)), jnp.concatenate([-sin, sin, -sin, sin], axis=-1)


def _rotate_half_partner(t):
    lane = lax.broadcasted_iota(jnp.int32, t.shape, 1)
    return jnp.where((lane % HEAD_DIM) < HEAD_DIM // 2,
                     pltpu.roll(t, LANES - HEAD_DIM // 2, 1), pltpu.roll(t, HEAD_DIM // 2, 1))


def _mask_bias(first):
    qi = lax.broadcasted_iota(jnp.int32, (N_BACK, 2 * N_BACK), 0)
    kj = lax.broadcasted_iota(jnp.int32, (N_BACK, 2 * N_BACK), 1)
    ok = (kj >= qi) & (kj <= qi + N_BACK)
    if first:
        ok = ok & (kj >= N_BACK)
    return jnp.where(ok, 0.0, NEG_INF).astype(F32)


def _stack_heads(t, head0):
    zero = jnp.zeros_like(t)
    return jnp.concatenate([jnp.where(head0, t, zero), jnp.where(head0, zero, t)], axis=0)


def _unstack_heads(t2, head0):
    return jnp.where(head0, t2[:N_BACK], t2[N_BACK:])


def _block_loop(nb, block):
    first, rest = _mask_bias(True), _mask_bias(False)
    first, rest = jnp.concatenate([first, first], axis=0), jnp.concatenate([rest, rest], axis=0)
    if nb <= BLOCK_UNROLL:
        for n in range(nb):
            block(n, first if n == 0 else rest)
        return

    def step(n, carry):
        block(n, jnp.where(n == 0, first, rest))
        return carry

    lax.fori_loop(0, nb, step, 0, unroll=BLOCK_UNROLL)


def _for(count, body, unroll_fully):
    if unroll_fully:
        for i in range(count):
            body(i)
    else:
        lax.fori_loop(0, count, lambda i, carry: (body(i), carry)[1], 0)


def _residues_together(nb):
    return min(RESIDUES_TOGETHER, max(1, BLOCK_UNROLL // nb))


def _residue_loop(d, nb, residue):
    together = _residues_together(nb)
    assert d % together == 0

    def group(i, carry):
        for u in range(together):
            residue(i * together + u, u)
        return carry

    lax.fori_loop(0, d // together, group, 0)


_NT = (((1,), (1,)), ((), ()))
_TN = (((0,), (0,)), ((), ()))


def _residue_rows(r, i, d):
    start = r + i * (N_BACK * d)
    if d == 1:
        return pl.ds(pl.multiple_of(start, N_BACK), N_BACK)
    return pl.ds(start, N_BACK, stride=d)


def _seq_rows(i):
    return pl.ds(pl.multiple_of(i * N_BACK, N_BACK), N_BACK)


def _rows_at(base, i, size=N_BACK):
    return pl.ds(pl.multiple_of(base + i * N_BACK, N_BACK), size)


def _attn_fwd(proj, cos, sin, group, Bl, S):
    d = WIN_DIL[group][1]
    L = S // d
    nb = L // N_BACK
    P = L + N_BACK
    assert WIN_DIL[group][0] // d == N_BACK and L % N_BACK == 0

    def body(q_ref, k_ref, v_ref, cos_ref, sin_ref, o_ref, lse_ref, qr, kr, vp):
        head0 = lax.broadcasted_iota(jnp.int32, (1, LANES), 1) < HEAD_DIM
        zeros = jnp.zeros((N_BACK, LANES), BF16)

        def residue(r, u):
            del u
            qbase, kbase = r * L, r * P
            kr[_rows_at(kbase, 0), :] = zeros
            vp[_rows_at(kbase, 0), :] = zeros

            def rope(i):
                rows = _residue_rows(r, i, d)
                cs, sn = cos_ref[rows, :], sin_ref[rows, :]
                q, k = q_ref[rows, :], k_ref[rows, :]
                qr[_rows_at(qbase, i), :] = ((q * cs + _rotate_half_partner(q) * sn)
                                            * (HEAD_DIM ** -0.5)).astype(BF16)
                kr[_rows_at(kbase, i + 1), :] = (k * cs + _rotate_half_partner(k) * sn).astype(BF16)
                vp[_rows_at(kbase, i + 1), :] = v_ref[rows, :].astype(BF16)

            _for(nb, rope, nb <= BLOCK_UNROLL)

            def block(n, bias):
                win = _rows_at(kbase, n, 2 * N_BACK)
                q2, kw, vw = _stack_heads(qr[_rows_at(qbase, n), :], head0), kr[win, :], vp[win, :]
                s = lax.dot_general(q2, kw, _NT, preferred_element_type=F32) + bias
                m = jnp.max(s, axis=1, keepdims=True)
                p = jnp.exp(s - m)
                l = jnp.sum(p, axis=1, keepdims=True)
                pv = jnp.dot(p.astype(BF16), vw, preferred_element_type=F32)
                rows = _residue_rows(r, n, d)
                o_ref[rows, :] = _unstack_heads(pv * (1.0 / l), head0)
                lse_ref[rows, :] = _unstack_heads((m + jnp.log(l)) + jnp.zeros((2 * N_BACK, LANES), F32), head0)

            _block_loop(nb, block)

        _residue_loop(d, nb, residue)

    act = pl.BlockSpec((None, S, LANES), lambda b, hp: (b, 0, hp))
    tab = pl.BlockSpec((None, S, LANES), lambda b, hp: (b, 0, 0))
    col = lambda which: pl.BlockSpec((None, S, LANES),
                                     lambda b, hp: (b, 0, (which * N_GROUPS + group) * HEAD_PAIRS + hp))
    seq = lambda rows: pl.BlockSpec((None, None, rows, LANES), lambda b, hp: (b, hp, 0, 0))
    p3 = proj.reshape(Bl, S, PROJ_COLS)
    o, lse, qr, kr, vp = pl.pallas_call(
        body, name="attn_fwd_g%d" % group, grid=(Bl, HEAD_PAIRS),
        in_specs=[col(0), col(1), col(2), tab, tab], out_specs=[act, act, seq(S), seq(d * P), seq(d * P)],
        out_shape=[_sds((Bl, S, D_MODEL), F32)] * 2 + [_sds((Bl, HEAD_PAIRS, S, LANES), BF16)]
        + [_sds((Bl, HEAD_PAIRS, d * P, LANES), BF16)] * 2,
        compiler_params=_params(("parallel", "arbitrary"), VMEM_LIMIT_ATTN),
    )(p3, p3, p3, cos, sin)
    return o.reshape(Bl * S, D_MODEL), lse.reshape(Bl * S, D_MODEL), (qr, kr, vp)


def _attn_bwd(saved, cos, sin, do, o, lse, dproj, group, Bl, S):
    d = WIN_DIL[group][1]
    L = S // d
    nb = L // N_BACK
    P = L + N_BACK
    steps = Bl * HEAD_PAIRS

    def body(qr, kr, vp, cos_ref, sin_ref, do_ref, o_ref, lse_ref, dproj_in, dproj_ref,
             dk_accs, dv_accs, stage, sems):
        del dproj_in
        head0 = lax.broadcasted_iota(jnp.int32, (1, LANES), 1) < HEAD_DIM
        b, hp = pl.program_id(0), pl.program_id(1)
        step = b * HEAD_PAIRS + hp
        slot = step % 2
        dq_s, dk_s, dv_s = stage.at[slot, 0], stage.at[slot, 1], stage.at[slot, 2]

        def copies(which_slot):
            out = []
            for which in range(3):
                col = ((which * N_GROUPS + group) * HEAD_PAIRS + hp) * LANES
                out.append(pltpu.make_async_copy(
                    stage.at[which_slot, which], dproj_ref.at[b, :, pl.ds(pl.multiple_of(col, LANES), LANES)],
                    sems.at[which_slot, which]))
            return out

        @pl.when(step >= 2)
        def _():
            for cp in copies(slot):
                cp.wait()

        def residue(r, u):
            qbase, kbase = r * L, r * P
            dk_acc, dv_acc = dk_accs.at[u], dv_accs.at[u]
            dk_acc[...] = jnp.zeros_like(dk_acc)
            dv_acc[...] = jnp.zeros_like(dv_acc)

            def block(n, bias):
                win = pl.ds(pl.multiple_of(n * N_BACK, N_BACK), 2 * N_BACK)
                kwin = _rows_at(kbase, n, 2 * N_BACK)
                rows = _residue_rows(r, n, d)
                q2, kw, vw = _stack_heads(qr[_rows_at(qbase, n), :], head0), kr[kwin, :], vp[kwin, :]
                dof = do_ref[rows, :]
                do2 = _stack_heads(dof.astype(BF16), head0)
                lse_b = lse_ref[rows, :]
                lse2 = jnp.concatenate([lse_b[:, 0:1], lse_b[:, HEAD_DIM:HEAD_DIM + 1]], axis=0)
                dsum = _stack_heads(dof * o_ref[rows, :], head0)
                delta = jnp.sum(dsum, axis=1, keepdims=True)
                s = lax.dot_general(q2, kw, _NT, preferred_element_type=F32) + bias
                p = jnp.exp(s - lse2)
                dp = lax.dot_general(do2, vw, _NT, preferred_element_type=F32)
                ds = (p * (dp - delta)).astype(BF16)
                dq = _unstack_heads(jnp.dot(ds, kw, preferred_element_type=F32), head0) * (HEAD_DIM ** -0.5)
                cs, sn = cos_ref[rows, :], sin_ref[rows, :]
                dq_s[rows, :] = dq * cs + _rotate_half_partner(dq * sn)
                dk_acc[win, :] += lax.dot_general(ds, q2, _TN, preferred_element_type=F32)
                dv_acc[win, :] += lax.dot_general(p.astype(BF16), do2, _TN, preferred_element_type=F32)

            _block_loop(nb, block)

            def finish(i):
                rows = _residue_rows(r, i, d)
                cs, sn = cos_ref[rows, :], sin_ref[rows, :]
                dk = dk_acc[_seq_rows(i + 1), :]
                dk_s[rows, :] = dk * cs + _rotate_half_partner(dk * sn)
                dv_s[rows, :] = dv_acc[_seq_rows(i + 1), :]

            _for(nb, finish, nb <= BLOCK_UNROLL)

        _residue_loop(d, nb, residue)
        for cp in copies(slot):
            cp.start()

        @pl.when(step == steps - 1)
        def _():
            if steps > 1:
                for cp in copies(1 - slot):
                    cp.wait()
            for cp in copies(slot):
                cp.wait()

    act = pl.BlockSpec((None, S, LANES), lambda b, hp: (b, 0, hp))
    tab = pl.BlockSpec((None, S, LANES), lambda b, hp: (b, 0, 0))
    seq = lambda rows: pl.BlockSpec((None, None, rows, LANES), lambda b, hp: (b, hp, 0, 0))
    view = lambda t: t.reshape(Bl, S, D_MODEL)
    out = pl.pallas_call(
        body, name="attn_bwd_g%d" % group, grid=(Bl, HEAD_PAIRS),
        in_specs=[seq(S), seq(d * P), seq(d * P), tab, tab, act, act, act, _ANY], out_specs=_ANY,
        out_shape=_sds((Bl, S, PROJ_COLS), F32), input_output_aliases={8: 0},
        scratch_shapes=[pltpu.VMEM((_residues_together(nb), P, LANES), F32),
                        pltpu.VMEM((_residues_together(nb), P, LANES), F32),
                        pltpu.VMEM((2, 3, S, LANES), F32), pltpu.SemaphoreType.DMA((2, 3))],
        compiler_params=_params(("arbitrary", "arbitrary"), VMEM_LIMIT_ATTN),
    )(*saved, cos, sin, view(do), view(o), view(lse), dproj.reshape(Bl, S, PROJ_COLS))
    return out.reshape(Bl * S, PROJ_COLS)


CONV_TILE = 256
CONV_CHUNK = 64
SUBLANES = 8
CONV_SHIFT_ROWS = CONV_TILE + CONV_HALO - SUBLANES


def _fill_shifted(shifted, ext, cs):
    for k in range(1, SUBLANES):
        shifted[k - 1] = ext[pl.ds(k, CONV_SHIFT_ROWS), cs]


def _shifted_rows(shifted, ext, cs, off):
    k = off % SUBLANES
    if k == 0:
        return ext[pl.ds(off, CONV_CHUNK), cs]
    return shifted[k - 1, pl.ds(off - k, CONV_CHUNK), :]


def _conv_fwd(proj, z, dw, dwb, ln_g, ln_b, Bl, S):
    tr = CONV_TILE
    nj = S // tr
    hb = tr // CONV_HALO

    def body(a_ref, b_ref, ah_ref, bh_ref, z_ref, dw_ref, dwb_ref, g_ref, bb_ref, u1_ref, out_ref, ext, shifted):
        j = pl.program_id(1)
        halo = ah_ref[0].astype(F32) * _sigmoid(bh_ref[0].astype(F32))
        ext[pl.ds(0, CONV_HALO), :] = jnp.where(j > 0, halo, 0.0)
        ext[pl.ds(CONV_HALO, tr), :] = a_ref[0].astype(F32) * _sigmoid(b_ref[0].astype(F32))

        def cols(c, carry):
            cs = pl.ds(pl.multiple_of(c * LANES, LANES), LANES)
            _fill_shifted(shifted, ext, cs)
            for rc in range(tr // CONV_CHUNK):
                acc = jnp.zeros((CONV_CHUNK, LANES), F32)
                for w in range(CONV_WIDTH):
                    off = rc * CONV_CHUNK + CONV_HALO - (CONV_WIDTH - 1) + w
                    acc = acc + dw_ref[pl.ds(w, 1), cs] * _shifted_rows(shifted, ext, cs, off)
                u1_ref[0, pl.ds(rc * CONV_CHUNK, CONV_CHUNK), cs] = acc + dwb_ref[:, cs]
            return carry

        lax.fori_loop(0, D_MODEL // LANES, cols, 0)
        u1 = u1_ref[0]
        mu = jnp.mean(u1, axis=1, keepdims=True)
        xc = u1 - mu
        rstd = lax.rsqrt(jnp.mean(xc * xc, axis=1, keepdims=True) + LN_EPS)
        u2 = xc * rstd * g_ref[...] + bb_ref[...]
        zv = z_ref[0].astype(F32)
        out_ref[0] = (u2 * _sigmoid(u2) * zv * _sigmoid(zv)).astype(BF16)

    tile = lambda cb: pl.BlockSpec((1, tr, D_MODEL), lambda b, j: (b, j, cb))
    halo = lambda cb: pl.BlockSpec((1, CONV_HALO, D_MODEL), lambda b, j: (b, jnp.maximum(j * hb - 1, 0), cb))
    par = lambda r: pl.BlockSpec((r, D_MODEL), lambda b, j: (0, 0))
    p3 = proj.reshape(Bl, S, 2 * D_MODEL)
    u1, out = pl.pallas_call(
        body, name="conv_fwd", grid=(Bl, nj),
        in_specs=[tile(0), tile(1), halo(0), halo(1), tile(0), par(32), par(1), par(1), par(1)],
        out_specs=[tile(0), tile(0)],
        out_shape=[_sds((Bl, S, D_MODEL), F32), _sds((Bl, S, D_MODEL), BF16)],
        scratch_shapes=[pltpu.VMEM((tr + CONV_HALO, D_MODEL), F32),
                        pltpu.VMEM((SUBLANES - 1, CONV_SHIFT_ROWS, LANES), F32)],
        compiler_params=_params(("parallel", "arbitrary")),
    )(p3, p3, p3, p3, z.reshape(Bl, S, D_MODEL), dw, dwb, ln_g, ln_b)
    return u1.reshape(Bl * S, D_MODEL), out.reshape(Bl * S, D_MODEL)


def _conv_norm_bwd(da2, z, u1, ln_g, ln_b):
    T = da2.shape[0]

    def body(da_ref, z_ref, u_ref, g_ref, b_ref, du_ref, dz_ref, dg_ref, db_ref):
        @pl.when(pl.program_id(0) == 0)
        def _():
            dg_ref[...] = jnp.zeros_like(dg_ref)
            db_ref[...] = jnp.zeros_like(db_ref)

        u1 = u_ref[...]
        mu = jnp.mean(u1, axis=1, keepdims=True)
        xc = u1 - mu
        rstd = lax.rsqrt(jnp.mean(xc * xc, axis=1, keepdims=True) + LN_EPS)
        nrm = xc * rstd
        u2 = nrm * g_ref[...] + b_ref[...]
        s2 = _sigmoid(u2)
        zv = z_ref[...].astype(F32)
        sz = _sigmoid(zv)
        dv = da_ref[...]
        dz_ref[...] = (dv * u2 * s2 * sz * (1.0 + zv * (1.0 - sz))).astype(BF16)
        du2 = dv * zv * sz * s2 * (1.0 + u2 * (1.0 - s2))
        dg_ref[...] += jnp.sum(du2 * nrm, axis=0, keepdims=True)
        db_ref[...] += jnp.sum(du2, axis=0, keepdims=True)
        dn = du2 * g_ref[...]
        du_ref[...] = rstd * (dn - jnp.mean(dn, axis=1, keepdims=True)
                              - nrm * jnp.mean(dn * nrm, axis=1, keepdims=True))

    return _row_call(body, "conv_norm_bwd", T,
                     [_rows(), _rows(), _rows(), _full((1, D_MODEL)), _full((1, D_MODEL))],
                     [_rows(), _rows(), _full((1, D_MODEL)), _full((1, D_MODEL))],
                     [_sds((T, D_MODEL), F32), _sds((T, D_MODEL), BF16), _sds((1, D_MODEL), F32),
                      _sds((1, D_MODEL), F32)], (da2, z, u1, ln_g, ln_b))


def _conv_bwd(proj, du1, dw, Bl, S):
    tr = CONV_TILE
    nj = S // tr
    hb = tr // CONV_HALO

    def body(a_ref, b_ref, ah_ref, bh_ref, du_ref, duh_ref, dw_ref, dab_ref, ddw_ref, ddb_ref, uext, dext, du0,
             ushift, dshift, ddw8):
        first = (pl.program_id(0) == 0) & (pl.program_id(1) == 0)
        last = (pl.program_id(0) == Bl - 1) & (pl.program_id(1) == nj - 1)
        j = pl.program_id(1)

        @pl.when(first)
        def _():
            ddw8[...] = jnp.zeros_like(ddw8)
            ddb_ref[...] = jnp.zeros_like(ddb_ref)

        halo = ah_ref[0].astype(F32) * _sigmoid(bh_ref[0].astype(F32))
        uext[pl.ds(0, CONV_HALO), :] = jnp.where(j > 0, halo, 0.0)
        av = a_ref[0].astype(F32)
        sb = _sigmoid(b_ref[0].astype(F32))
        uext[pl.ds(CONV_HALO, tr), :] = av * sb
        dext[pl.ds(0, tr), :] = du_ref[0]
        dext[pl.ds(tr, CONV_HALO), :] = jnp.where(j < nj - 1, duh_ref[0], 0.0)
        ddb_ref[...] += jnp.sum(du_ref[0], axis=0, keepdims=True)

        def cols(c, carry):
            cs = pl.ds(pl.multiple_of(c * LANES, LANES), LANES)
            _fill_shifted(dshift, dext, cs)
            _fill_shifted(ushift, uext, cs)
            for rc in range(tr // CONV_CHUNK):
                base = rc * CONV_CHUNK
                acc = jnp.zeros((CONV_CHUNK, LANES), F32)
                for w in range(CONV_WIDTH):
                    acc = acc + dw_ref[pl.ds(w, 1), cs] * _shifted_rows(dshift, dext, cs, base + CONV_WIDTH - 1 - w)
                du0[pl.ds(base, CONV_CHUNK), cs] = acc
            for w in range(CONV_WIDTH):
                part = jnp.zeros((SUBLANES, LANES), F32)
                for rc in range(tr // CONV_CHUNK):
                    base = rc * CONV_CHUNK
                    prod = dext[pl.ds(base, CONV_CHUNK), cs] * _shifted_rows(
                        ushift, uext, cs, base + CONV_HALO - (CONV_WIDTH - 1) + w)
                    for i in range(CONV_CHUNK // SUBLANES):
                        part = part + prod[i * SUBLANES:(i + 1) * SUBLANES]
                ddw8[pl.ds(w * SUBLANES, SUBLANES), cs] += part
            return carry

        lax.fori_loop(0, D_MODEL // LANES, cols, 0)
        g = du0[...]
        dab_ref[0, :, 0:D_MODEL] = (g * sb).astype(BF16)
        dab_ref[0, :, D_MODEL:2 * D_MODEL] = (g * av * sb * (1.0 - sb)).astype(BF16)

        @pl.when(last)
        def _():
            for w in range(CONV_WIDTH + 1):
                ddw_ref[pl.ds(w, 1), :] = jnp.sum(ddw8[pl.ds(w * SUBLANES, SUBLANES), :], axis=0, keepdims=True)

    tile = lambda cb: pl.BlockSpec((1, tr, D_MODEL), lambda b, j: (b, j, cb))
    halo = lambda cb: pl.BlockSpec((1, CONV_HALO, D_MODEL), lambda b, j: (b, jnp.maximum(j * hb - 1, 0), cb))
    nxt = pl.BlockSpec((1, CONV_HALO, D_MODEL), lambda b, j: (b, jnp.minimum((j + 1) * hb, S // CONV_HALO - 1), 0))
    par = lambda r: pl.BlockSpec((r, D_MODEL), lambda b, j: (0, 0))
    p3 = proj.reshape(Bl, S, 2 * D_MODEL)
    d3 = du1.reshape(Bl, S, D_MODEL)
    dab, ddw, ddb = pl.pallas_call(
        body, name="conv_bwd", grid=(Bl, nj),
        in_specs=[tile(0), tile(1), halo(0), halo(1), tile(0), nxt, par(32)],
        out_specs=[pl.BlockSpec((1, tr, 2 * D_MODEL), lambda b, j: (b, j, 0)), par(32), par(1)],
        out_shape=[_sds((Bl, S, 2 * D_MODEL), BF16), _sds((32, D_MODEL), F32), _sds((1, D_MODEL), F32)],
        scratch_shapes=[pltpu.VMEM((tr + CONV_HALO, D_MODEL), F32), pltpu.VMEM((tr + CONV_HALO, D_MODEL), F32),
                        pltpu.VMEM((tr, D_MODEL), F32),
                        pltpu.VMEM((SUBLANES - 1, CONV_SHIFT_ROWS, LANES), F32),
                        pltpu.VMEM((SUBLANES - 1, CONV_SHIFT_ROWS, LANES), F32),
                        pltpu.VMEM(((CONV_WIDTH + 1) * SUBLANES, D_MODEL), F32)],
        compiler_params=_params(("arbitrary", "arbitrary")),
    )(p3, p3, p3, p3, d3, d3, dw)
    return dab.reshape(Bl * S, 2 * D_MODEL), ddw, ddb


_LAYOUT = (
    ("pre_norm_g", (4, 1024), None), ("post_norm_g", (4, 1024), None),
    ("attn_w_in", (2, 1024, 2560), 2), ("attn_w_out", (2, 256, 1024), 1),
    ("conv_w_in", (2, 1024, 768), 2), ("conv_dw_w", (2, 31, 256), 2),
    ("conv_dw_b", (2, 256), 1), ("conv_ln_g", (2, 256), 1), ("conv_ln_b", (2, 256), 1),
    ("conv_w_out", (2, 256, 1024), 1), ("ple_w_proj", (4, 256, 256), 2), ("ple_w_gate", (4, 256, 1024), 1),
)
_MATMUL_WEIGHTS = ("attn_w_in", "attn_w_out", "conv_w_in", "conv_w_out", "ple_w_proj", "ple_w_gate")
_FIRST_LAYER = ("attn_w_in", "attn_w_out", "ple_w_proj", "ple_w_gate")
_AXIS = {n: a for n, _, a in _LAYOUT}


def _size(shape):
    n = 1
    for s in shape:
        n *= s
    return n


def _padded_rows(shape):
    rows = _size(shape) // shape[-1]
    return rows + (-rows) % FLAT_ROW_ALIGN


def _rows2d(a):
    a2 = a.reshape(-1, a.shape[-1])
    pad = _padded_rows(a.shape) - a2.shape[0]
    return jnp.pad(a2, ((0, pad), (0, 0))) if pad else a2


def _col_blocks(a):
    a2 = _rows2d(a)
    return jnp.concatenate([a2[:, c:c + FLAT_COLS] for c in range(0, a2.shape[1], FLAT_COLS)], axis=0)


def _from_col_blocks(flat, off, shape):
    rows, nblk = _padded_rows(shape), shape[-1] // FLAT_COLS
    a2 = jnp.concatenate([flat[off + b * rows:off + (b + 1) * rows] for b in range(nblk)], axis=1)
    return a2[:_size(shape) // shape[-1]].reshape(shape), off + nblk * rows


def _shard_col_blocks(full, shape, axis):
    if axis is None:
        blocks = _col_blocks(full)
        return jnp.broadcast_to(blocks[None], (N_CHIPS,) + blocks.shape)
    m = shape[-1]
    if axis == len(shape) - 1:
        a2 = _rows2d(full)
        pieces = [a2[:, c:c + FLAT_COLS] for c in range(0, N_CHIPS * m, FLAT_COLS)]
    else:
        layers, r, _ = shape
        assert axis == 1 and (layers * r) % FLAT_ROW_ALIGN == 0
        pieces = [full[:, s * r:(s + 1) * r, c:c + FLAT_COLS].reshape(layers * r, FLAT_COLS)
                  for s in range(N_CHIPS) for c in range(0, m, FLAT_COLS)]
    return jnp.concatenate(pieces, axis=0).reshape(N_CHIPS, -1, FLAT_COLS)


_FLAT_BIG = ("attn_w_in", "conv_w_in", "ple_w_gate", "attn_w_out", "conv_w_out", "ple_w_proj")
_FLAT_SMALL = ("pre_norm_g", "post_norm_g", "conv_dw_w", "conv_dw_b", "conv_ln_g", "conv_ln_b")
PACK_TILE = 1024


_FIRST_LAYER_PARAMS = _FIRST_LAYER + ("pre_norm_g", "post_norm_g")


def _part_shapes(first):
    out = {}
    for n, shape, _ in _LAYOUT:
        layers = (1 if first else shape[0] - 1) if n in _FIRST_LAYER_PARAMS else (0 if first else shape[0])
        if layers:
            out[n] = (layers,) + shape[1:]
    return out


def _flat_plan(shapes):
    out, off = {}, 0
    for n in _FLAT_BIG + _FLAT_SMALL:
        if n in shapes:
            out[n] = off
            off += _padded_rows(shapes[n]) * (shapes[n][-1] // FLAT_COLS)
    return out, off + (-off) % (2 * FLAT_TILE)


def _unpack_f32(flat, shapes):
    offsets, _ = _flat_plan(shapes)
    return {n: _from_col_blocks(flat, offsets[n], shapes[n])[0] for n in shapes}


def _pack_param(full, shape, axis, off, flat, total_rows):
    layers, r, m = shape
    nblk = m // FLAT_COLS
    if axis == 2:
        rows = layers * r
        tr = math.gcd(math.gcd(rows, PACK_TILE), off) if off else math.gcd(rows, PACK_TILE)
        assert rows % tr == 0 and off % tr == 0 and tr % FLAT_ROW_ALIGN == 0
        src = full.reshape(rows, N_CHIPS * m)
        grid = (N_CHIPS * nblk, rows // tr)
        in_spec = pl.BlockSpec((tr, FLAT_COLS), lambda j, i: (i, j))
        out_spec = pl.BlockSpec((None, tr, FLAT_COLS), lambda j, i: (j // nblk, (off + (j % nblk) * rows) // tr + i, 0))
    else:
        assert axis == 1 and off % r == 0
        src = full.reshape(layers * N_CHIPS * r, m)
        grid = (layers, N_CHIPS, nblk)
        in_spec = pl.BlockSpec((r, FLAT_COLS), lambda l, s, b: (l * N_CHIPS + s, b))
        out_spec = pl.BlockSpec((None, r, FLAT_COLS), lambda l, s, b: (s, (off + b * layers * r) // r + l, 0))

    def copy_body(src_ref, *rest):
        rest[-1][...] = src_ref[...]

    args, in_specs, aliases = [src], [in_spec], {}
    if flat is not None:
        args.append(flat)
        in_specs.append(_ANY)
        aliases = {1: 0}
    return pl.pallas_call(
        copy_body, name="pack_grad", grid=grid, in_specs=in_specs, out_specs=out_spec,
        out_shape=_sds((N_CHIPS, total_rows, FLAT_COLS), F32), input_output_aliases=aliases,
        compiler_params=_params(("arbitrary",) * len(grid)))(*args)


SMALL_ROWS = 40


def _stack_small(w):
    rows = [w["conv_dw_w"]] + [w[n][:, None, :] for n in ("conv_dw_b", "conv_ln_g", "conv_ln_b")]
    stacked = jnp.concatenate(rows, axis=1)
    return jnp.pad(stacked, ((0, 0), (0, SMALL_ROWS - stacked.shape[1]), (0, 0)))


def _unstack_small(small):
    return {"conv_dw_w": small[:, :CONV_WIDTH], "conv_dw_b": small[:, CONV_WIDTH],
            "conv_ln_g": small[:, CONV_WIDTH + 1], "conv_ln_b": small[:, CONV_WIDTH + 2]}


def _pack_full_grads(grads, shapes):
    offsets, total_rows = _flat_plan(shapes)
    flat = None
    for n in _FLAT_BIG:
        if n in shapes:
            flat = _pack_param(grads[n], shapes[n], _AXIS[n], offsets[n], flat, total_rows)
    small_names = [n for n in _FLAT_SMALL if n in shapes]
    small = jnp.concatenate([_shard_col_blocks(grads[n], shapes[n], _AXIS[n]) for n in small_names], axis=1)
    start = offsets[small_names[0]]
    small = jnp.pad(small, ((0, 0), (0, total_rows - start - small.shape[1]), (0, 0)))
    return lax.dynamic_update_slice(flat, small, (0, start, 0))


_ANY = pl.BlockSpec(memory_space=pl.ANY)


def _mesh_pos():
    return lax.axis_index("x"), lax.axis_index("y"), lax.axis_index("c")


def _other_chips(x, y):
    return [(1 - x, y), (x, 1 - y), (1 - x, 1 - y)]


COPIES_PER_ARRAY = 7


def _allgather_weights(shards, axes, small):
    n = len(shards)
    full_shape = lambda a, axis: tuple(d * (N_CHIPS if i == axis else 1) for i, d in enumerate(a.shape))

    def body(*refs):
        ins, small_in = refs[:n], refs[n]
        outs, small_out = refs[n + 1:2 * n + 1], refs[2 * n + 1]
        send_sems, recv_sems = refs[2 * n + 2:]
        x, y, c = _mesh_pos()
        mine, me, sibling = 2 * x + y, (x, y, c), (x, y, 1 - c)
        chips = _other_chips(x, y)

        def region(a, chip, half):
            _, rows, cols = shards[a].shape
            h = rows // 2
            if axes[a] == 2:
                return outs[a].at[:, slice(None) if half is None else pl.ds(half * h, h), pl.ds(chip * cols, cols)]
            if half is None:
                return outs[a].at[:, pl.ds(chip * rows, rows), :]
            return outs[a].at[:, pl.ds(chip * rows + half * h, h), :]

        def copy(k, src, dst, to):
            return pltpu.make_async_remote_copy(src_ref=src, dst_ref=dst, send_sem=send_sems.at[k],
                                                recv_sem=recv_sems.at[k], device_id=to, device_id_type=MESH)

        def arrival(k, dst):
            return copy(k, dst, dst, me)

        sends = []
        for a in range(n):
            h = shards[a].shape[1] // 2
            base = a * COPIES_PER_ARRAY
            sends.append(copy(base + 6, ins[a], region(a, mine, None), sibling))
            for j, (cx, cy) in enumerate(chips):
                sends.append(copy(base + j, ins[a].at[:, pl.ds(c * h, h), :], region(a, mine, c), (cx, cy, c)))
        small_cols = small.shape[2]
        small_region = lambda chip: small_out.at[:, :, pl.ds(chip * small_cols, small_cols)]
        base = n * COPIES_PER_ARRAY
        sends.append(copy(base + 3, small_in, small_region(mine), sibling))
        for j, (cx, cy) in enumerate(chips):
            sends.append(copy(base + j, small_in, small_region(mine), (cx, cy, c)))
        for cp in sends:
            cp.start()
        for j, (cx, cy) in enumerate(chips):
            for a in range(n):
                k = a * COPIES_PER_ARRAY + j
                arrival(k, region(a, 2 * cx + cy, c)).wait_recv()
                passed = copy(k + 3, region(a, 2 * cx + cy, c), region(a, 2 * cx + cy, c), sibling)
                passed.start()
                sends.append(passed)
        for j, (cx, cy) in enumerate(chips):
            for a in range(n):
                arrival(a * COPIES_PER_ARRAY + 3 + j, region(a, 2 * cx + cy, 1 - c)).wait_recv()
            arrival(base + j, small_region(2 * cx + cy)).wait_recv()
        for a in range(n):
            arrival(a * COPIES_PER_ARRAY + 6, region(a, mine, None)).wait_recv()
        arrival(base + 3, small_region(mine)).wait_recv()
        for cp in sends:
            cp.wait_send()

    n_sems = n * COPIES_PER_ARRAY + 4
    out = pl.pallas_call(
        body, name="allgather_weights", in_specs=[_ANY] * (n + 1), out_specs=[_ANY] * (n + 1),
        out_shape=[_sds(full_shape(a, axis), a.dtype) for a, axis in zip(shards, axes)]
        + [_sds(full_shape(small, 2), small.dtype)],
        scratch_shapes=[pltpu.SemaphoreType.DMA((n_sems,)), pltpu.SemaphoreType.DMA((n_sems,))],
    )(*shards, small)
    return out[:n], out[n]


_HBM = pl.BlockSpec(memory_space=pltpu.HBM)
_SEM = pl.BlockSpec(memory_space=pltpu.SEMAPHORE)


def _full_shape(shard, axis):
    return tuple(d * (N_CHIPS if i == axis else 1) for i, d in enumerate(shard.shape))


def _direct_gather_copies(srcs, lands, shapes, axes, send_sems, recv_sems):
    x, y, c = _mesh_pos()
    mine, me, sibling = 2 * x + y, (x, y, c), (x, y, 1 - c)

    def region(a, chip, half):
        _, rows, cols = shapes[a]
        h = rows // 2
        if axes[a] == 2:
            return lands[a].at[:, slice(None) if half is None else pl.ds(half * h, h), pl.ds(chip * cols, cols)]
        if half is None:
            return lands[a].at[:, pl.ds(chip * rows, rows), :]
        return lands[a].at[:, pl.ds(chip * rows + half * h, h), :]

    def copy(k, src, dst, to):
        return pltpu.make_async_remote_copy(src_ref=src, dst_ref=dst, send_sem=send_sems.at[k],
                                            recv_sem=recv_sems.at[k], device_id=to, device_id_type=MESH)

    outgoing, incoming = [], []
    for a in range(len(srcs)):
        h = shapes[a][1] // 2
        base = a * COPIES_PER_ARRAY
        for j, (cx, cy) in enumerate(_other_chips(x, y)):
            for t, ct in enumerate((c, 1 - c)):
                k = base + 2 * j + t
                outgoing.append(copy(k, srcs[a].at[:, pl.ds(c * h, h), :], region(a, mine, c), (cx, cy, ct)))
                landed = region(a, 2 * cx + cy, ct)
                incoming.append(copy(k, landed, landed, me))
        outgoing.append(copy(base + 6, srcs[a], region(a, mine, None), sibling))
        incoming.append(copy(base + 6, region(a, mine, None), region(a, mine, None), me))
    return outgoing, incoming


def _allgather_start(shards, axes):
    n = len(shards)
    shapes = [s.shape for s in shards]

    def body(*refs):
        srcs, lands = refs[:n], refs[n:2 * n]
        send_sems, recv_sems = refs[2 * n], refs[2 * n + 1]
        token = refs[-1]
        outgoing, _ = _direct_gather_copies(srcs, lands, shapes, axes, send_sems, recv_sems)
        for cp in outgoing:
            cp.start()
        token[...] = jnp.zeros_like(token)

    n_sems = n * COPIES_PER_ARRAY
    zones = [pltpu.with_memory_space_constraint(lax.empty(_full_shape(s, ax), s.dtype), pltpu.HBM)
             for s, ax in zip(shards, axes)]
    out = pl.pallas_call(
        body, name="allgather_rest_start",
        out_shape=(pltpu.SemaphoreType.DMA((n_sems,)), pltpu.SemaphoreType.DMA((n_sems,)),
                   *[pltpu.HBM(s.shape, s.dtype) for s in shards], *[pltpu.HBM(z.shape, z.dtype) for z in zones],
                   jax.ShapeDtypeStruct((8, LANES), F32)),
        in_specs=[_HBM] * (2 * n),
        out_specs=(_SEM, _SEM, *[_HBM] * (2 * n), pl.BlockSpec(memory_space=pltpu.VMEM)),
        input_output_aliases={i: 2 + i for i in range(2 * n)},
        compiler_params=pltpu.CompilerParams(has_side_effects=pltpu.SideEffectType.DATAFLOW_SIDE_EFFECTING),
    )(*[pltpu.with_memory_space_constraint(s, pltpu.HBM) for s in shards], *zones)
    return out[0], out[1], out[2:2 + n], out[2 + n:2 + 2 * n], out[-1]


def _allgather_wait(send_sems, recv_sems, shards, zones, axes, after):
    n = len(shards)
    shapes = [s.shape for s in shards]

    def body(*refs):
        srcs, lands = refs[:n], refs[n:2 * n]
        outgoing, incoming = _direct_gather_copies(srcs, lands, shapes, axes, refs[2 * n], refs[2 * n + 1])
        for cp in outgoing:
            cp.wait_send()
        for cp in incoming:
            cp.wait_recv()

    out = pl.pallas_call(
        body, name="allgather_rest_wait",
        out_shape=(*[pltpu.HBM(s.shape, s.dtype) for s in shards], *[pltpu.HBM(z.shape, z.dtype) for z in zones]),
        in_specs=[_HBM] * (2 * n) + [_SEM, _SEM, _ANY], out_specs=[_HBM] * (2 * n),
        input_output_aliases={i: i for i in range(2 * n)},
        compiler_params=pltpu.CompilerParams(has_side_effects=pltpu.SideEffectType.DATAFLOW_SIDE_EFFECTING),
    )(*shards, *zones, send_sems, recv_sems, after)
    return out[n:]


def _exchange_core_halves(g):
    n, _, H, C = g.shape

    def body(g_ref, got_ref, send_sem, recv_sem):
        x, y, c = _mesh_pos()
        swap = pltpu.make_async_remote_copy(
            src_ref=g_ref.at[pl.ds(0, n), 1 - c], dst_ref=got_ref, send_sem=send_sem, recv_sem=recv_sem,
            device_id=(x, y, 1 - c), device_id_type=MESH)
        swap.start()
        swap.wait()

    return pl.pallas_call(
        body, name="exchange_core_halves", in_specs=[_ANY], out_specs=_ANY,
        out_shape=_sds((n, H, C), g.dtype),
        scratch_shapes=[pltpu.SemaphoreType.DMA, pltpu.SemaphoreType.DMA],
    )(g)


def _scatter_to_chips(p):
    n, H, C = p.shape

    def body(p_ref, q_ref, send_sems, recv_sems):
        x, y, c = _mesh_pos()
        chips = _other_chips(x, y)
        sends = [pltpu.make_async_remote_copy(
            src_ref=p_ref.at[2 * cx + cy], dst_ref=q_ref.at[j], send_sem=send_sems.at[j],
            recv_sem=recv_sems.at[j], device_id=(cx, cy, c), device_id_type=MESH)
            for j, (cx, cy) in enumerate(chips)]
        for cp in sends:
            cp.start()
        for cp in sends:
            cp.wait_recv()
        for cp in sends:
            cp.wait_send()

    return pl.pallas_call(
        body, name="scatter_to_chips", in_specs=[_ANY], out_specs=_ANY,
        out_shape=_sds((n - 1, H, C), p.dtype),
        scratch_shapes=[pltpu.SemaphoreType.DMA((3,)), pltpu.SemaphoreType.DMA((3,))],
    )(p)


def _scatter_copies(p_ref, q_ref, send_sems, recv_sems):
    x, y, c = _mesh_pos()
    return [pltpu.make_async_remote_copy(
        src_ref=p_ref.at[2 * cx + cy], dst_ref=q_ref.at[j], send_sem=send_sems.at[j],
        recv_sem=recv_sems.at[j], device_id=(cx, cy, c), device_id_type=MESH)
        for j, (cx, cy) in enumerate(_other_chips(x, y))]


def _scatter_start(p):
    n, H, C = p.shape

    def body(p_ref, q_ref, send_sems, recv_sems, p_thru, q_thru, token):
        for cp in _scatter_copies(p_ref, q_ref, send_sems, recv_sems):
            cp.start()
        token[...] = jnp.zeros_like(token)

    zone = pltpu.with_memory_space_constraint(lax.empty((n - 1, H, C), p.dtype), pltpu.HBM)
    return pl.pallas_call(
        body, name="scatter_rest_start",
        out_shape=(pltpu.SemaphoreType.DMA((n - 1,)), pltpu.SemaphoreType.DMA((n - 1,)), pltpu.HBM(p.shape, p.dtype),
                   pltpu.HBM(zone.shape, zone.dtype), jax.ShapeDtypeStruct((8, LANES), F32)),
        in_specs=[_HBM, _HBM], out_specs=(_SEM, _SEM, _HBM, _HBM, pl.BlockSpec(memory_space=pltpu.VMEM)),
        input_output_aliases={0: 2, 1: 3},
        compiler_params=pltpu.CompilerParams(has_side_effects=pltpu.SideEffectType.DATAFLOW_SIDE_EFFECTING),
    )(pltpu.with_memory_space_constraint(p, pltpu.HBM), zone)


def _scatter_wait(send_sems, recv_sems, p, zone, after):
    def body(p_ref, q_ref, send_sems, recv_sems, after_ref, p_out, q_out):
        copies = _scatter_copies(p_ref, q_ref, send_sems, recv_sems)
        for cp in copies:
            cp.wait_send()
        for cp in copies:
            cp.wait_recv()

    return pl.pallas_call(
        body, name="scatter_rest_wait",
        out_shape=(pltpu.HBM(p.shape, p.dtype), pltpu.HBM(zone.shape, zone.dtype)),
        in_specs=[_HBM, _HBM, _SEM, _SEM, _ANY], out_specs=[_HBM, _HBM], input_output_aliases={0: 0, 1: 1},
        compiler_params=pltpu.CompilerParams(has_side_effects=pltpu.SideEffectType.DATAFLOW_SIDE_EFFECTING),
    )(p, zone, send_sems, recv_sems, after)


def _share_core_halves(r2):
    _, H, C = r2.shape

    def body(r_ref, out_ref, send_sem, recv_sem):
        x, y, c = _mesh_pos()
        send = pltpu.make_async_remote_copy(
            src_ref=r_ref.at[c], dst_ref=out_ref.at[c], send_sem=send_sem, recv_sem=recv_sem,
            device_id=(x, y, 1 - c), device_id_type=MESH)
        send.start()
        send.wait_send()
        pltpu.make_async_remote_copy(
            src_ref=r_ref.at[c], dst_ref=out_ref.at[1 - c], send_sem=send_sem, recv_sem=recv_sem,
            device_id=(x, y, 1 - c), device_id_type=MESH).wait_recv()

    return pl.pallas_call(
        body, name="share_core_halves", in_specs=[_ANY], out_specs=_ANY,
        out_shape=_sds(r2.shape, r2.dtype), input_output_aliases={0: 0},
        scratch_shapes=[pltpu.SemaphoreType.DMA, pltpu.SemaphoreType.DMA],
    )(r2)


def _place():
    x, y, c = _mesh_pos()
    return jnp.stack([c, 2 * x + y]).astype(jnp.int32)


def _sum_pair(g, got, place):
    n, _, H, C = g.shape

    def body(place_ref, a_ref, b_ref, o_ref):
        o_ref[...] = (a_ref[...] + b_ref[...]).astype(BF16)

    spec = pl.BlockSpec((1, FLAT_TILE, C), lambda s, i, pr: (s, i, 0))
    return pl.pallas_call(
        body, name="sum_core_pair",
        grid_spec=pltpu.PrefetchScalarGridSpec(
            num_scalar_prefetch=1, grid=(n, H // FLAT_TILE),
            in_specs=[pl.BlockSpec((1, None, FLAT_TILE, C), lambda s, i, pr: (s, pr[0], i, 0)), spec],
            out_specs=spec),
        out_shape=_sds((n, H, C), BF16),
        compiler_params=_params(("parallel", "parallel")))(place, g, got)


def _sum_chips(p, q, place):
    n, H, C = p.shape

    def body(place_ref, own_ref, qx_ref, qy_ref, qxy_ref, o_ref):
        mine = place_ref[1]
        own, qx, qy, qxy = (t[0].astype(F32) for t in (own_ref, qx_ref, qy_ref, qxy_ref))

        def term(s):
            rel = jnp.full(own.shape, mine ^ s, jnp.int32)
            return jnp.where(rel == 0, own, jnp.where(rel == 2, qx, jnp.where(rel == 1, qy, qxy)))

        o_ref[0] = ((term(0) + term(1)) + term(2)) + term(3)

    qspec = lambda j: pl.BlockSpec((1, FLAT_TILE, C), lambda i, pr: (j, i, 0))
    return pl.pallas_call(
        body, name="sum_chips",
        grid_spec=pltpu.PrefetchScalarGridSpec(
            num_scalar_prefetch=1, grid=(H // FLAT_TILE,),
            in_specs=[pl.BlockSpec((1, FLAT_TILE, C), lambda i, pr: (pr[1], i, 0)), qspec(0), qspec(1), qspec(2)],
            out_specs=pl.BlockSpec((1, FLAT_TILE, C), lambda i, pr: (pr[0], i, 0))),
        out_shape=_sds((2, H, C), F32),
        compiler_params=_params(("parallel",)))(place, p, q, q, q)


ADAMW_BLOCK = 1 << 18


def _adamw(w, g, m, v):
    shape = w.shape
    C = shape[-1]
    R = _size(shape) // C
    tr = R
    while tr * C > ADAMW_BLOCK and tr % 16 == 0:
        tr //= 2
    w, g, m, v = (t.reshape(R, C) for t in (w, g, m, v))

    def body(w_ref, g_ref, m_ref, v_ref, d_ref, nm_ref, nv_ref):
        gv = g_ref[...]
        nm = ADAM_B1 * m_ref[...] + (1.0 - ADAM_B1) * gv
        nv = ADAM_B2 * v_ref[...] + (1.0 - ADAM_B2) * (gv * gv)
        m_hat = nm / (1.0 - ADAM_B1 ** ADAM_STEP)
        v_hat = nv / (1.0 - ADAM_B2 ** ADAM_STEP)
        d_ref[...] = -ADAM_LR * (m_hat / (jnp.sqrt(v_hat) + ADAM_EPS) + ADAM_WD * w_ref[...])
        nm_ref[...] = nm
        nv_ref[...] = nv

    spec = pl.BlockSpec((tr, C), lambda i: (i, 0))
    outs = pl.pallas_call(body, name="adamw", grid=(R // tr,), in_specs=[spec] * 4, out_specs=[spec] * 3,
                          out_shape=[_sds((R, C), F32)] * 3, compiler_params=_params(("parallel",)))(w, g, m, v)
    return tuple(t.reshape(shape) for t in outs)


def _chip_partials(gfull, place):
    n, R, C = gfull.shape
    g4 = gfull.reshape(n, 2, R // 2, C)
    return _sum_pair(g4, _exchange_core_halves(g4), place)


def _finish_reduce(p, q, place):
    r2 = _share_core_halves(_sum_chips(p, q, place))
    return r2.reshape(2 * r2.shape[1], r2.shape[2])


def _reduce_scatter_grads(gfull):
    place = _place()
    p = _chip_partials(gfull, place)
    return _finish_reduce(p, _scatter_to_chips(p), place)


def _local_step(x, p, positions, loss_target, pre_g, post_g, w, later_weights, later_grads):
    Bl, S, _ = x.shape
    T = Bl * S
    cos, sin = _rope_tables(positions)
    xs = x.reshape(T, D_MODEL)
    saved = []
    for i in range(DEPTH):
        if i == 1:
            later_weights(xs)
        j = i // 2
        g_pre, g_post = pre_g[i:i + 1], post_g[i:i + 1]
        h = _rmsnorm_fwd(xs, g_pre)
        st = {"x": xs, "h": h}
        if i % 2 == 0:
            proj = _mm(h, w["attn_w_in"][j], name="attn_in")
            res = [_attn_fwd(proj, cos, sin, g, Bl, S) for g in range(N_GROUPS)]
            a, o, lse = _attn_combine([r[0] for r in res], [r[1] for r in res], proj)
            w_out = w["attn_w_out"][j]
            st.update(proj=proj, a=a, o=o, lse=lse, qkv=[r[2] for r in res])
        else:
            w_ab, w_z = w["conv_w_in"][j][:, :2 * D_MODEL], w["conv_w_in"][j][:, 2 * D_MODEL:]
            ab = _mm(h, w_ab, out_dtype=BF16, name="conv_in_ab")
            z = _mm(h, w_z, out_dtype=BF16, name="conv_in_z")
            dw = jnp.pad(w["conv_dw_w"][j], ((0, 1), (0, 0)))
            u1, a = _conv_fwd(ab, z, dw, w["conv_dw_b"][j:j + 1], w["conv_ln_g"][j:j + 1],
                              w["conv_ln_b"][j:j + 1], Bl, S)
            w_out = w["conv_w_out"][j]
            st.update(w_ab=w_ab, w_z=w_z, ab=ab, z=z, dw=dw, u1=u1, a=a)
        y, x1 = _mm_rows(a, w_out, [xs, g_post], _post_epilogue, (F32, F32), "branch_out_post")
        pi = p[i].reshape(T, PLE_DIM)
        pe = _mm(pi, w["ple_w_proj"][i], name="ple_proj")
        gl, xs = _mm_rows(x1, w["ple_w_gate"][i], [pe], _ple_epilogue, (F32, F32), "ple_gate_fwd")
        st.update(y=y, x1=x1, pi=pi, pe=pe, gl=gl)
        saved.append(st)

    sq, dx = _loss_fwd_bwd(xs, loss_target.reshape(T, D_MODEL))

    grads = {n: [None] * shape[0] for n, shape, _ in _LAYOUT}
    for i in reversed(range(DEPTH)):
        j = i // 2
        st = saved[i]
        g_pre, g_post = pre_g[i:i + 1], post_g[i:i + 1]
        if i == 0:
            rest = {n: jnp.stack(v[1:] if n in _FIRST_LAYER_PARAMS else v) for n, v in grads.items()}
            g_post = g_post + later_grads(rest)[0, 0]
        dpe, dgl, dx1, dy, dg_post = _ple_post_bwd(dx, st["pe"], st["gl"], w["ple_w_gate"][i], st["y"], g_post)
        grads["ple_w_proj"][i] = _mm(st["pi"], dpe, ta=True, name="ple_proj_wgrad")
        grads["ple_w_gate"][i] = _mm(st["x1"], dgl, ta=True, name="ple_gate_wgrad")
        grads["post_norm_g"][i] = dg_post[0]
        if i % 2 == 0:
            grads["attn_w_out"][j] = _mm(st["a"], dy, ta=True, name="attn_out_wgrad")
            da = _mm(dy, w["attn_w_out"][j], tb=True, name="attn_out_dgrad")
            do, dproj = _gate_bwd(da, st["o"], st["proj"])
            for g in range(N_GROUPS):
                dproj = _attn_bwd(st["qkv"][g], cos, sin, do, st["o"], st["lse"], dproj, g, Bl, S)
            dx, dg_pre = _dgrad_pre_bwd(dproj, w["attn_w_in"][j], st["x"], g_pre, dx1, None, 2048, "attn_in_dgrad_pre")
            grads["attn_w_in"][j] = _mm(st["h"], dproj, ta=True, name="attn_in_wgrad")
        else:
            grads["conv_w_out"][j] = _mm(st["a"], dy, ta=True, name="conv_out_wgrad")
            da2 = _mm(dy, w["conv_w_out"][j], tb=True, name="conv_out_dgrad")
            du1, dz, dln_g, dln_b = _conv_norm_bwd(da2, st["z"], st["u1"], w["conv_ln_g"][j:j + 1],
                                                   w["conv_ln_b"][j:j + 1])
            dab, ddw, ddb = _conv_bwd(st["ab"], du1, st["dw"], Bl, S)
            dh = _mm(dz, st["w_z"], tb=True, name="conv_in_z_dgrad")
            dx, dg_pre = _dgrad_pre_bwd(dab, st["w_ab"], st["x"], g_pre, dx1, dh, 2048, "conv_in_dgrad_pre")
            dw_ab = _mm(st["h"], dab, ta=True, name="conv_in_ab_wgrad")
            dw_z = _mm(st["h"], dz, ta=True, name="conv_in_z_wgrad")
            grads["conv_w_in"][j] = jnp.concatenate([dw_ab, dw_z], axis=1)
            grads["conv_dw_w"][j] = ddw[:CONV_WIDTH]
            grads["conv_dw_b"][j] = ddb[0]
            grads["conv_ln_g"][j] = dln_g[0]
            grads["conv_ln_b"][j] = dln_b[0]
        grads["pre_norm_g"][i] = dg_pre[0]
    first = {n: jnp.stack(grads[n][:1]) for n in _FIRST_LAYER_PARAMS}
    return sq, dx.reshape(Bl, S, D_MODEL), first


_NAMES = tuple(n for n, _, _ in _LAYOUT)


def kernel(x, p, positions, pre_norm_g, post_norm_g, attn_w_in, attn_w_out, conv_w_in, conv_dw_w, conv_dw_b, conv_ln_g, conv_ln_b, conv_w_out, ple_w_proj, ple_w_gate, loss_target, m_pre_norm_g, m_post_norm_g, m_attn_w_in, m_attn_w_out, m_conv_w_in, m_conv_dw_w, m_conv_dw_b, m_conv_ln_g, m_conv_ln_b, m_conv_w_out, m_ple_w_proj, m_ple_w_gate, v_pre_norm_g, v_post_norm_g, v_attn_w_in, v_attn_w_out, v_conv_w_in, v_conv_dw_w, v_conv_dw_b, v_conv_ln_g, v_conv_ln_b, v_conv_w_out, v_ple_w_proj, v_ple_w_gate):
    w_loc = dict(zip(_NAMES, (pre_norm_g, post_norm_g, attn_w_in, attn_w_out, conv_w_in, conv_dw_w, conv_dw_b,
                              conv_ln_g, conv_ln_b, conv_w_out, ple_w_proj, ple_w_gate)))
    m_loc = dict(zip(_NAMES, (m_pre_norm_g, m_post_norm_g, m_attn_w_in, m_attn_w_out, m_conv_w_in, m_conv_dw_w,
                              m_conv_dw_b, m_conv_ln_g, m_conv_ln_b, m_conv_w_out, m_ple_w_proj, m_ple_w_gate)))
    v_loc = dict(zip(_NAMES, (v_pre_norm_g, v_post_norm_g, v_attn_w_in, v_attn_w_out, v_conv_w_in, v_conv_dw_w,
                              v_conv_dw_b, v_conv_ln_g, v_conv_ln_b, v_conv_w_out, v_ple_w_proj, v_ple_w_gate)))

    bf = {n: w_loc[n].astype(BF16) for n in _MATMUL_WEIGHTS}
    axes = [_AXIS[n] for n in _MATMUL_WEIGHTS]
    first, small = _allgather_weights([bf[n][:1] for n in _FIRST_LAYER], [_AXIS[n] for n in _FIRST_LAYER],
                                      _stack_small(w_loc))
    send_sems, recv_sems, shards, zones, token = _allgather_start(
        [bf[n][1:] if n in _FIRST_LAYER else bf[n] for n in _MATMUL_WEIGHTS], axes)
    w_full = dict({n: [full[0]] for n, full in zip(_FIRST_LAYER, first)}, **_unstack_small(small))

    def later_weights(after):
        for n, full in zip(_MATMUL_WEIGHTS, _allgather_wait(send_sems, recv_sems, shards, zones, axes, after)):
            w_full[n] = w_full[n] + [full[l] for l in range(full.shape[0])] if n in _FIRST_LAYER else full

    place = _place()
    rest_shapes, first_shapes = _part_shapes(False), _part_shapes(True)
    in_flight = []

    def later_grads(grads):
        in_flight.extend(_scatter_start(_chip_partials(_pack_full_grads(grads, rest_shapes), place)))
        return in_flight[4]

    sq, grad_x, first_grads = _local_step(x, p, positions, loss_target, pre_norm_g + token[0, 0], post_norm_g,
                                          w_full, later_weights, later_grads)
    loss = lax.psum(sq[0, 0] * (0.5 / D_MODEL), ("x", "y", "c"))

    g_first = _unpack_f32(_reduce_scatter_grads(_pack_full_grads(first_grads, first_shapes)), first_shapes)
    p_rest, q_rest = _scatter_wait(*in_flight[:4], grad_x)
    g_rest = _unpack_f32(_finish_reduce(p_rest, q_rest, place), rest_shapes)
    g_out = {n: jnp.concatenate([g_first[n], g_rest[n]]) if n in g_first and n in g_rest
             else g_rest.get(n, g_first.get(n)) for n in _NAMES}
    updates = {n: _adamw(w_loc[n], g_out[n], m_loc[n], v_loc[n]) for n in _NAMES}
    d_out, m_out, v_out = ({n: updates[n][k] for n in _NAMES} for k in range(3))
    return (loss, grad_x, *[g_out[n] for n in _NAMES], *[d_out[n] for n in _NAMES],
            *[m_out[n] for n in _NAMES], *[v_out[n] for n in _NAMES])
```

```python
import math

import jax
import jax.numpy as jnp
from jax import lax
from jax.experimental import pallas as pl
from jax.experimental.pallas import tpu as pltpu

F32 = jnp.float32
BF16 = jnp.bfloat16

D_MODEL = 1024
DEPTH = 4
PLE_DIM = 256
HEAD_DIM = 64
WIN_DIL = ((128, 1), (512, 4), (2048, 16))
N_GROUPS = 3
N_BACK = 128
BLOCK_UNROLL = 8
RESIDUES_TOGETHER = 2
ROPE_THETA = 10000.0
CONV_WIDTH = 31
CONV_HALO = 32
RMS_EPS = 1e-6
LN_EPS = 1e-5
NEG_INF = -1e30
ADAM_LR, ADAM_B1, ADAM_B2, ADAM_EPS, ADAM_WD, ADAM_STEP = 0.001, 0.9, 0.999, 1e-08, 0.01, 10

LANES = 128
N_CHIPS = 4
VMEM_LIMIT = 48 * 1024 * 1024
VMEM_LIMIT_ATTN = 56 * 1024 * 1024
FLAT_COLS = 256
FLAT_TILE = 2048
FLAT_ROW_ALIGN = 16
PROJ_COLS = (3 * N_GROUPS + 1) * D_MODEL
HEAD_PAIRS = D_MODEL // LANES

MESH = pl.DeviceIdType.MESH


def _params(sem=None, vmem=VMEM_LIMIT):
    return pltpu.CompilerParams(dimension_semantics=sem, vmem_limit_bytes=vmem)


def _sigmoid(v):
    return 1.0 / (1.0 + jnp.exp(-v))


def _mm(a, b, *, ta=False, tb=False, out_dtype=F32, tm=1024, tn=1024, tk=1024, name="mm"):
    if ta:
        K, M = a.shape
    else:
        M, K = a.shape
    if tb:
        N, K2 = b.shape
    else:
        K2, N = b.shape
    assert K == K2, (a.shape, b.shape)
    tm, tn, tk = min(tm, M), min(tn, N), min(tk, K)
    assert M % tm == 0 and N % tn == 0 and K % tk == 0
    nk = K // tk
    dims = (((0 if ta else 1,), (1 if tb else 0,)), ((), ()))

    def body(a_ref, b_ref, o_ref, *scratch):
        k = pl.program_id(2)
        part = lax.dot_general(a_ref[...].astype(BF16), b_ref[...].astype(BF16), dims, preferred_element_type=F32)
        if nk == 1:
            o_ref[...] = part.astype(out_dtype)
        else:
            acc_ref, = scratch

            @pl.when(k == 0)
            def _():
                acc_ref[...] = part

            @pl.when((k > 0) & (k < nk - 1))
            def _():
                acc_ref[...] += part

            @pl.when(k == nk - 1)
            def _():
                o_ref[...] = (acc_ref[...] + part).astype(out_dtype)

    a_spec = pl.BlockSpec((tk, tm), lambda i, j, k: (k, i)) if ta else pl.BlockSpec((tm, tk), lambda i, j, k: (i, k))
    b_spec = pl.BlockSpec((tn, tk), lambda i, j, k: (j, k)) if tb else pl.BlockSpec((tk, tn), lambda i, j, k: (k, j))
    return pl.pallas_call(
        body, name=name, grid=(M // tm, N // tn, nk),
        in_specs=[a_spec, b_spec], out_specs=pl.BlockSpec((tm, tn), lambda i, j, k: (i, j)),
        out_shape=jax.ShapeDtypeStruct((M, N), out_dtype),
        scratch_shapes=[pltpu.VMEM((tm, tn), F32)] if nk > 1 else [],
        compiler_params=_params(("parallel", "parallel", "arbitrary")),
    )(a, b)


def _mm_rows(a, b, extras, epilogue, out_dtypes, name, tm=512):
    M, K = a.shape
    N = b.shape[1]
    n_ex = len(extras)

    def body(*refs):
        a_ref, b_ref = refs[:2]
        av = a_ref[...]
        acc = jnp.dot(av.astype(BF16), b_ref[...].astype(BF16), preferred_element_type=F32)
        results = epilogue(acc, av, *[e[...] for e in refs[2:2 + n_ex]])
        for o_ref, r in zip(refs[2 + n_ex:], results):
            o_ref[...] = r.astype(o_ref.dtype)

    tile = pl.BlockSpec((tm, N), lambda i: (i, 0))
    in_specs = [pl.BlockSpec((tm, K), lambda i: (i, 0)), pl.BlockSpec((K, N), lambda i: (0, 0))]
    in_specs += [tile if e.shape[0] == M else pl.BlockSpec((1, N), lambda i: (0, 0)) for e in extras]
    return pl.pallas_call(
        body, name=name, grid=(M // tm,), in_specs=in_specs, out_specs=[tile] * len(out_dtypes),
        out_shape=[jax.ShapeDtypeStruct((M, N), dt) for dt in out_dtypes],
        compiler_params=_params(("parallel",)),
    )(a, b, *extras)


ROW_TILE = 512


def _rows(w=D_MODEL, cb=0, tr=ROW_TILE):
    return pl.BlockSpec((tr, w), lambda i: (i, cb))


def _full(shape):
    return pl.BlockSpec(shape, lambda i: (0,) * len(shape))


def _row_call(body, name, T, in_specs, out_specs, out_shape, args, tr=ROW_TILE):
    return pl.pallas_call(body, name=name, grid=(T // tr,), in_specs=in_specs, out_specs=out_specs,
                          out_shape=out_shape, compiler_params=_params(("arbitrary",)))(*args)


def _sds(shape, dtype):
    return jax.ShapeDtypeStruct(shape, dtype)


def _rmsnorm_fwd(x, g):
    T = x.shape[0]

    def body(x_ref, g_ref, h_ref):
        xv = x_ref[...]
        r = lax.rsqrt(jnp.mean(xv * xv, axis=1, keepdims=True) + RMS_EPS)
        h_ref[...] = (xv * r * g_ref[...]).astype(BF16)

    return _row_call(body, "rmsnorm_fwd", T, [_rows(), _full((1, D_MODEL))], _rows(),
                     _sds((T, D_MODEL), BF16), (x, g))


def _post_epilogue(y, a_tile, x, g):
    del a_tile
    return y, x + y * lax.rsqrt(jnp.mean(y * y, axis=1, keepdims=True) + RMS_EPS) * g


def _ple_epilogue(gl, x1, pe):
    return gl, x1 + pe * _sigmoid(gl)


def _dgrad_pre_bwd(a, b, x, g, dx1, add, tk, name, tm=512):
    M, K = a.shape
    N = b.shape[0]
    tk = min(tk, K)
    assert M % tm == 0 and K % tk == 0 and N == D_MODEL
    nk = K // tk

    def body(*refs):
        a_ref, b_ref, x_ref, g_ref, dx1_ref = refs[:5]
        add_ref = refs[5] if add is not None else None
        dx_ref, dg_ref = refs[-3:-1] if nk > 1 else refs[-2:]
        i, k = pl.program_id(0), pl.program_id(1)

        @pl.when((i == 0) & (k == 0))
        def _():
            dg_ref[...] = jnp.zeros_like(dg_ref)

        part = lax.dot_general(a_ref[...].astype(BF16), b_ref[...].astype(BF16), _NT, preferred_element_type=F32)

        def finish(dh):
            if add is not None:
                dh = dh + add_ref[...]
            xv = x_ref[...]
            r = lax.rsqrt(jnp.mean(xv * xv, axis=1, keepdims=True) + RMS_EPS)
            xh = xv * r
            dg_ref[...] += jnp.sum(dh * xh, axis=0, keepdims=True)
            dn = dh * g_ref[...]
            dx_ref[...] = dx1_ref[...] + r * (dn - xh * jnp.mean(dn * xh, axis=1, keepdims=True))

        if nk == 1:
            finish(part)
        else:
            acc_ref = refs[-1]

            @pl.when(k == 0)
            def _():
                acc_ref[...] = part

            @pl.when((k > 0) & (k < nk - 1))
            def _():
                acc_ref[...] += part

            @pl.when(k == nk - 1)
            def _():
                finish(acc_ref[...] + part)

    tile = pl.BlockSpec((tm, N), lambda i, k: (i, 0))
    row = pl.BlockSpec((1, N), lambda i, k: (0, 0))
    in_specs = [pl.BlockSpec((tm, tk), lambda i, k: (i, k)), pl.BlockSpec((N, tk), lambda i, k: (0, k)), tile, row, tile]
    args = [a, b, x, g, dx1]
    if add is not None:
        in_specs.append(tile)
        args.append(add)
    return pl.pallas_call(
        body, name=name, grid=(M // tm, nk), in_specs=in_specs, out_specs=[tile, row],
        out_shape=[_sds((M, N), F32), _sds((1, N), F32)],
        scratch_shapes=[pltpu.VMEM((tm, N), F32)] if nk > 1 else [],
        compiler_params=_params(("arbitrary", "arbitrary")),
    )(*args)


def _ple_post_bwd(dx2, pe, gl, w_gate, y, g_post):
    T = dx2.shape[0]

    def body(d_ref, pe_ref, gl_ref, w_ref, y_ref, g_ref, dpe_ref, dgl_ref, dx1_ref, dy_ref, dg_ref):
        @pl.when(pl.program_id(0) == 0)
        def _():
            dg_ref[...] = jnp.zeros_like(dg_ref)

        dv = d_ref[...]
        sg = _sigmoid(gl_ref[...])
        dpe_ref[...] = (dv * sg).astype(BF16)
        dgl = (dv * pe_ref[...] * sg * (1.0 - sg)).astype(BF16)
        dgl_ref[...] = dgl
        dx1 = dv + lax.dot_general(dgl, w_ref[...], _NT, preferred_element_type=F32)
        dx1_ref[...] = dx1
        yv = y_ref[...]
        r = lax.rsqrt(jnp.mean(yv * yv, axis=1, keepdims=True) + RMS_EPS)
        yh = yv * r
        dg_ref[...] += jnp.sum(dx1 * yh, axis=0, keepdims=True)
        dn = dx1 * g_ref[...]
        dy_ref[...] = (r * (dn - yh * jnp.mean(dn * yh, axis=1, keepdims=True))).astype(BF16)

    row = _full((1, D_MODEL))
    return _row_call(body, "ple_post_bwd", T, [_rows()] * 3 + [_full((D_MODEL, D_MODEL)), _rows(), row],
                     [_rows()] * 4 + [row],
                     [_sds((T, D_MODEL), BF16)] * 2 + [_sds((T, D_MODEL), F32), _sds((T, D_MODEL), BF16),
                                                      _sds((1, D_MODEL), F32)],
                     (dx2, pe, gl, w_gate, y, g_post))


def _loss_fwd_bwd(y, target):
    T = y.shape[0]

    def body(y_ref, t_ref, s_ref, d_ref):
        @pl.when(pl.program_id(0) == 0)
        def _():
            s_ref[...] = jnp.zeros_like(s_ref)

        e = y_ref[...] - t_ref[...]
        s_ref[...] += jnp.sum(e * e).reshape(1, 1)
        d_ref[...] = e * (1.0 / D_MODEL)

    return _row_call(body, "loss", T, [_rows()] * 2, [_full((1, 1)), _rows()],
                     [_sds((1, 1), F32), _sds((T, D_MODEL), F32)], (y, target))


def _attn_combine(outs, lses, proj):
    T = proj.shape[0]

    def body(o0, o1, o2, l0, l1, l2, z_ref, a_ref, o_ref, lse_ref):
        a0, a1, a2 = l0[...], l1[...], l2[...]
        m = jnp.maximum(jnp.maximum(a0, a1), a2)
        e0, e1, e2 = jnp.exp(a0 - m), jnp.exp(a1 - m), jnp.exp(a2 - m)
        ssum = e0 + e1 + e2
        o = (e0 * o0[...] + e1 * o1[...] + e2 * o2[...]) / ssum
        zv = z_ref[...].astype(F32)
        o_ref[...] = o
        lse_ref[...] = m + jnp.log(ssum)
        a_ref[...] = (o * zv * _sigmoid(zv)).astype(BF16)

    return _row_call(body, "attn_combine", T, [_rows()] * 6 + [_rows(cb=3 * N_GROUPS)], [_rows()] * 3,
                     [_sds((T, D_MODEL), BF16), _sds((T, D_MODEL), F32), _sds((T, D_MODEL), F32)],
                     (*outs, *lses, proj))


def _gate_bwd(da, o, proj):
    T = da.shape[0]

    def body(da_ref, o_ref, z_ref, do_ref, dz_ref):
        dv = da_ref[...]
        zv = z_ref[...]
        sg = _sigmoid(zv)
        do_ref[...] = dv * zv * sg
        dz_ref[...] = dv * o_ref[...] * sg * (1.0 + zv * (1.0 - sg))

    zcols = _rows(cb=3 * N_GROUPS)
    return _row_call(body, "gate_bwd", T, [_rows(), _rows(), zcols], [_rows(), zcols],
                     [_sds((T, D_MODEL), F32), _sds((T, PROJ_COLS), F32)], (da, o, proj))


def _rope_tables(positions):
    inv_freq = 1.0 / (ROPE_THETA ** (jnp.arange(0, HEAD_DIM, 2, dtype=F32) / HEAD_DIM))
    ang = positions.astype(F32)[..., None] * inv_freq
    cos, sin = jnp.cos(ang), jnp.sin(ang)
    return jnp.tile(cos, (1, 1, 4)), jnp.concatenate([-sin, sin, -sin, sin], axis=-1)


def _rotate_half_partner(t):
    lane = lax.broadcasted_iota(jnp.int32, t.shape, 1)
    return jnp.where((lane % HEAD_DIM) < HEAD_DIM // 2,
                     pltpu.roll(t, LANES - HEAD_DIM // 2, 1), pltpu.roll(t, HEAD_DIM // 2, 1))


def _mask_bias(first):
    qi = lax.broadcasted_iota(jnp.int32, (N_BACK, 2 * N_BACK), 0)
    kj = lax.broadcasted_iota(jnp.int32, (N_BACK, 2 * N_BACK), 1)
    ok = (kj >= qi) & (kj <= qi + N_BACK)
    if first:
        ok = ok & (kj >= N_BACK)
    return jnp.where(ok, 0.0, NEG_INF).astype(F32)


def _stack_heads(t, head0):
    zero = jnp.zeros_like(t)
    return jnp.concatenate([jnp.where(head0, t, zero), jnp.where(head0, zero, t)], axis=0)


def _unstack_heads(t2, head0):
    return jnp.where(head0, t2[:N_BACK], t2[N_BACK:])


def _block_loop(nb, block):
    first, rest = _mask_bias(True), _mask_bias(False)
    first, rest = jnp.concatenate([first, first], axis=0), jnp.concatenate([rest, rest], axis=0)
    if nb <= BLOCK_UNROLL:
        for n in range(nb):
            block(n, first if n == 0 else rest)
        return

    def step(n, carry):
        block(n, jnp.where(n == 0, first, rest))
        return carry

    lax.fori_loop(0, nb, step, 0, unroll=BLOCK_UNROLL)


def _for(count, body, unroll_fully):
    if unroll_fully:
        for i in range(count):
            body(i)
    else:
        lax.fori_loop(0, count, lambda i, carry: (body(i), carry)[1], 0)


def _residues_together(nb):
    return min(RESIDUES_TOGETHER, max(1, BLOCK_UNROLL // nb))


def _residue_loop(d, nb, residue):
    together = _residues_together(nb)
    assert d % together == 0

    def group(i, carry):
        for u in range(together):
            residue(i * together + u, u)
        return carry

    lax.fori_loop(0, d // together, group, 0)


_NT = (((1,), (1,)), ((), ()))
_TN = (((0,), (0,)), ((), ()))


def _residue_rows(r, i, d):
    start = r + i * (N_BACK * d)
    if d == 1:
        return pl.ds(pl.multiple_of(start, N_BACK), N_BACK)
    return pl.ds(start, N_BACK, stride=d)


def _seq_rows(i):
    return pl.ds(pl.multiple_of(i * N_BACK, N_BACK), N_BACK)


def _rows_at(base, i, size=N_BACK):
    return pl.ds(pl.multiple_of(base + i * N_BACK, N_BACK), size)


def _attn_fwd(proj, cos, sin, group, Bl, S):
    d = WIN_DIL[group][1]
    L = S // d
    nb = L // N_BACK
    P = L + N_BACK
    assert WIN_DIL[group][0] // d == N_BACK and L % N_BACK == 0

    def body(q_ref, k_ref, v_ref, cos_ref, sin_ref, o_ref, lse_ref, qr, kr, vp):
        head0 = lax.broadcasted_iota(jnp.int32, (1, LANES), 1) < HEAD_DIM
        zeros = jnp.zeros((N_BACK, LANES), BF16)

        def residue(r, u):
            del u
            qbase, kbase = r * L, r * P
            kr[_rows_at(kbase, 0), :] = zeros
            vp[_rows_at(kbase, 0), :] = zeros

            def rope(i):
                rows = _residue_rows(r, i, d)
                cs, sn = cos_ref[rows, :], sin_ref[rows, :]
                q, k = q_ref[rows, :], k_ref[rows, :]
                qr[_rows_at(qbase, i), :] = ((q * cs + _rotate_half_partner(q) * sn)
                                            * (HEAD_DIM ** -0.5)).astype(BF16)
                kr[_rows_at(kbase, i + 1), :] = (k * cs + _rotate_half_partner(k) * sn).astype(BF16)
                vp[_rows_at(kbase, i + 1), :] = v_ref[rows, :].astype(BF16)

            _for(nb, rope, nb <= BLOCK_UNROLL)

            def block(n, bias):
                win = _rows_at(kbase, n, 2 * N_BACK)
                q2, kw, vw = _stack_heads(qr[_rows_at(qbase, n), :], head0), kr[win, :], vp[win, :]
                s = lax.dot_general(q2, kw, _NT, preferred_element_type=F32) + bias
                m = jnp.max(s, axis=1, keepdims=True)
                p = jnp.exp(s - m)
                l = jnp.sum(p, axis=1, keepdims=True)
                pv = jnp.dot(p.astype(BF16), vw, preferred_element_type=F32)
                rows = _residue_rows(r, n, d)
                o_ref[rows, :] = _unstack_heads(pv * (1.0 / l), head0)
                lse_ref[rows, :] = _unstack_heads((m + jnp.log(l)) + jnp.zeros((2 * N_BACK, LANES), F32), head0)

            _block_loop(nb, block)

        _residue_loop(d, nb, residue)

    act = pl.BlockSpec((None, S, LANES), lambda b, hp: (b, 0, hp))
    tab = pl.BlockSpec((None, S, LANES), lambda b, hp: (b, 0, 0))
    col = lambda which: pl.BlockSpec((None, S, LANES),
                                     lambda b, hp: (b, 0, (which * N_GROUPS + group) * HEAD_PAIRS + hp))
    seq = lambda rows: pl.BlockSpec((None, None, rows, LANES), lambda b, hp: (b, hp, 0, 0))
    p3 = proj.reshape(Bl, S, PROJ_COLS)
    o, lse, qr, kr, vp = pl.pallas_call(
        body, name="attn_fwd_g%d" % group, grid=(Bl, HEAD_PAIRS),
        in_specs=[col(0), col(1), col(2), tab, tab], out_specs=[act, act, seq(S), seq(d * P), seq(d * P)],
        out_shape=[_sds((Bl, S, D_MODEL), F32)] * 2 + [_sds((Bl, HEAD_PAIRS, S, LANES), BF16)]
        + [_sds((Bl, HEAD_PAIRS, d * P, LANES), BF16)] * 2,
        compiler_params=_params(("parallel", "arbitrary"), VMEM_LIMIT_ATTN),
    )(p3, p3, p3, cos, sin)
    return o.reshape(Bl * S, D_MODEL), lse.reshape(Bl * S, D_MODEL), (qr, kr, vp)


def _attn_bwd(saved, cos, sin, do, o, lse, dproj, group, Bl, S):
    d = WIN_DIL[group][1]
    L = S // d
    nb = L // N_BACK
    P = L + N_BACK
    steps = Bl * HEAD_PAIRS

    def body(qr, kr, vp, cos_ref, sin_ref, do_ref, o_ref, lse_ref, dproj_in, dproj_ref,
             dk_accs, dv_accs, stage, sems):
        del dproj_in
        head0 = lax.broadcasted_iota(jnp.int32, (1, LANES), 1) < HEAD_DIM
        b, hp = pl.program_id(0), pl.program_id(1)
        step = b * HEAD_PAIRS + hp
        slot = step % 2
        dq_s, dk_s, dv_s = stage.at[slot, 0], stage.at[slot, 1], stage.at[slot, 2]

        def copies(which_slot):
            out = []
            for which in range(3):
                col = ((which * N_GROUPS + group) * HEAD_PAIRS + hp) * LANES
                out.append(pltpu.make_async_copy(
                    stage.at[which_slot, which], dproj_ref.at[b, :, pl.ds(pl.multiple_of(col, LANES), LANES)],
                    sems.at[which_slot, which]))
            return out

        @pl.when(step >= 2)
        def _():
            for cp in copies(slot):
                cp.wait()

        def residue(r, u):
            qbase, kbase = r * L, r * P
            dk_acc, dv_acc = dk_accs.at[u], dv_accs.at[u]
            dk_acc[...] = jnp.zeros_like(dk_acc)
            dv_acc[...] = jnp.zeros_like(dv_acc)

            def block(n, bias):
                win = pl.ds(pl.multiple_of(n * N_BACK, N_BACK), 2 * N_BACK)
                kwin = _rows_at(kbase, n, 2 * N_BACK)
                rows = _residue_rows(r, n, d)
                q2, kw, vw = _stack_heads(qr[_rows_at(qbase, n), :], head0), kr[kwin, :], vp[kwin, :]
                dof = do_ref[rows, :]
                do2 = _stack_heads(dof.astype(BF16), head0)
                lse_b = lse_ref[rows, :]
                lse2 = jnp.concatenate([lse_b[:, 0:1], lse_b[:, HEAD_DIM:HEAD_DIM + 1]], axis=0)
                dsum = _stack_heads(dof * o_ref[rows, :], head0)
                delta = jnp.sum(dsum, axis=1, keepdims=True)
                s = lax.dot_general(q2, kw, _NT, preferred_element_type=F32) + bias
                p = jnp.exp(s - lse2)
                dp = lax.dot_general(do2, vw, _NT, preferred_element_type=F32)
                ds = (p * (dp - delta)).astype(BF16)
                dq = _unstack_heads(jnp.dot(ds, kw, preferred_element_type=F32), head0) * (HEAD_DIM ** -0.5)
                cs, sn = cos_ref[rows, :], sin_ref[rows, :]
                dq_s[rows, :] = dq * cs + _rotate_half_partner(dq * sn)
                dk_acc[win, :] += lax.dot_general(ds, q2, _TN, preferred_element_type=F32)
                dv_acc[win, :] += lax.dot_general(p.astype(BF16), do2, _TN, preferred_element_type=F32)

            _block_loop(nb, block)

            def finish(i):
                rows = _residue_rows(r, i, d)
                cs, sn = cos_ref[rows, :], sin_ref[rows, :]
                dk = dk_acc[_seq_rows(i + 1), :]
                dk_s[rows, :] = dk * cs + _rotate_half_partner(dk * sn)
                dv_s[rows, :] = dv_acc[_seq_rows(i + 1), :]

            _for(nb, finish, nb <= BLOCK_UNROLL)

        _residue_loop(d, nb, residue)
        for cp in copies(slot):
            cp.start()

        @pl.when(step == steps - 1)
        def _():
            if steps > 1:
                for cp in copies(1 - slot):
                    cp.wait()
            for cp in copies(slot):
                cp.wait()

    act = pl.BlockSpec((None, S, LANES), lambda b, hp: (b, 0, hp))
    tab = pl.BlockSpec((None, S, LANES), lambda b, hp: (b, 0, 0))
    seq = lambda rows: pl.BlockSpec((None, None, rows, LANES), lambda b, hp: (b, hp, 0, 0))
    view = lambda t: t.reshape(Bl, S, D_MODEL)
    out = pl.pallas_call(
        body, name="attn_bwd_g%d" % group, grid=(Bl, HEAD_PAIRS),
        in_specs=[seq(S), seq(d * P), seq(d * P), tab, tab, act, act, act, _ANY], out_specs=_ANY,
        out_shape=_sds((Bl, S, PROJ_COLS), F32), input_output_aliases={8: 0},
        scratch_shapes=[pltpu.VMEM((_residues_together(nb), P, LANES), F32),
                        pltpu.VMEM((_residues_together(nb), P, LANES), F32),
                        pltpu.VMEM((2, 3, S, LANES), F32), pltpu.SemaphoreType.DMA((2, 3))],
        compiler_params=_params(("arbitrary", "arbitrary"), VMEM_LIMIT_ATTN),
    )(*saved, cos, sin, view(do), view(o), view(lse), dproj.reshape(Bl, S, PROJ_COLS))
    return out.reshape(Bl * S, PROJ_COLS)


CONV_TILE = 256
CONV_CHUNK = 64
SUBLANES = 8
CONV_SHIFT_ROWS = CONV_TILE + CONV_HALO - SUBLANES


def _fill_shifted(shifted, ext, cs):
    for k in range(1, SUBLANES):
        shifted[k - 1] = ext[pl.ds(k, CONV_SHIFT_ROWS), cs]


def _shifted_rows(shifted, ext, cs, off):
    k = off % SUBLANES
    if k == 0:
        return ext[pl.ds(off, CONV_CHUNK), cs]
    return shifted[k - 1, pl.ds(off - k, CONV_CHUNK), :]


def _conv_fwd(proj, z, dw, dwb, ln_g, ln_b, Bl, S):
    tr = CONV_TILE
    nj = S // tr
    hb = tr // CONV_HALO

    def body(a_ref, b_ref, ah_ref, bh_ref, z_ref, dw_ref, dwb_ref, g_ref, bb_ref, u1_ref, out_ref, ext, shifted):
        j = pl.program_id(1)
        halo = ah_ref[0].astype(F32) * _sigmoid(bh_ref[0].astype(F32))
        ext[pl.ds(0, CONV_HALO), :] = jnp.where(j > 0, halo, 0.0)
        ext[pl.ds(CONV_HALO, tr), :] = a_ref[0].astype(F32) * _sigmoid(b_ref[0].astype(F32))

        def cols(c, carry):
            cs = pl.ds(pl.multiple_of(c * LANES, LANES), LANES)
            _fill_shifted(shifted, ext, cs)
            for rc in range(tr // CONV_CHUNK):
                acc = jnp.zeros((CONV_CHUNK, LANES), F32)
                for w in range(CONV_WIDTH):
                    off = rc * CONV_CHUNK + CONV_HALO - (CONV_WIDTH - 1) + w
                    acc = acc + dw_ref[pl.ds(w, 1), cs] * _shifted_rows(shifted, ext, cs, off)
                u1_ref[0, pl.ds(rc * CONV_CHUNK, CONV_CHUNK), cs] = acc + dwb_ref[:, cs]
            return carry

        lax.fori_loop(0, D_MODEL // LANES, cols, 0)
        u1 = u1_ref[0]
        mu = jnp.mean(u1, axis=1, keepdims=True)
        xc = u1 - mu
        rstd = lax.rsqrt(jnp.mean(xc * xc, axis=1, keepdims=True) + LN_EPS)
        u2 = xc * rstd * g_ref[...] + bb_ref[...]
        zv = z_ref[0].astype(F32)
        out_ref[0] = (u2 * _sigmoid(u2) * zv * _sigmoid(zv)).astype(BF16)

    tile = lambda cb: pl.BlockSpec((1, tr, D_MODEL), lambda b, j: (b, j, cb))
    halo = lambda cb: pl.BlockSpec((1, CONV_HALO, D_MODEL), lambda b, j: (b, jnp.maximum(j * hb - 1, 0), cb))
    par = lambda r: pl.BlockSpec((r, D_MODEL), lambda b, j: (0, 0))
    p3 = proj.reshape(Bl, S, 2 * D_MODEL)
    u1, out = pl.pallas_call(
        body, name="conv_fwd", grid=(Bl, nj),
        in_specs=[tile(0), tile(1), halo(0), halo(1), tile(0), par(32), par(1), par(1), par(1)],
        out_specs=[tile(0), tile(0)],
        out_shape=[_sds((Bl, S, D_MODEL), F32), _sds((Bl, S, D_MODEL), BF16)],
        scratch_shapes=[pltpu.VMEM((tr + CONV_HALO, D_MODEL), F32),
                        pltpu.VMEM((SUBLANES - 1, CONV_SHIFT_ROWS, LANES), F32)],
        compiler_params=_params(("parallel", "arbitrary")),
    )(p3, p3, p3, p3, z.reshape(Bl, S, D_MODEL), dw, dwb, ln_g, ln_b)
    return u1.reshape(Bl * S, D_MODEL), out.reshape(Bl * S, D_MODEL)


def _conv_norm_bwd(da2, z, u1, ln_g, ln_b):
    T = da2.shape[0]

    def body(da_ref, z_ref, u_ref, g_ref, b_ref, du_ref, dz_ref, dg_ref, db_ref):
        @pl.when(pl.program_id(0) == 0)
        def _():
            dg_ref[...] = jnp.zeros_like(dg_ref)
            db_ref[...] = jnp.zeros_like(db_ref)

        u1 = u_ref[...]
        mu = jnp.mean(u1, axis=1, keepdims=True)
        xc = u1 - mu
        rstd = lax.rsqrt(jnp.mean(xc * xc, axis=1, keepdims=True) + LN_EPS)
        nrm = xc * rstd
        u2 = nrm * g_ref[...] + b_ref[...]
        s2 = _sigmoid(u2)
        zv = z_ref[...].astype(F32)
        sz = _sigmoid(zv)
        dv = da_ref[...]
        dz_ref[...] = (dv * u2 * s2 * sz * (1.0 + zv * (1.0 - sz))).astype(BF16)
        du2 = dv * zv * sz * s2 * (1.0 + u2 * (1.0 - s2))
        dg_ref[...] += jnp.sum(du2 * nrm, axis=0, keepdims=True)
        db_ref[...] += jnp.sum(du2, axis=0, keepdims=True)
        dn = du2 * g_ref[...]
        du_ref[...] = rstd * (dn - jnp.mean(dn, axis=1, keepdims=True)
                              - nrm * jnp.mean(dn * nrm, axis=1, keepdims=True))

    return _row_call(body, "conv_norm_bwd", T,
                     [_rows(), _rows(), _rows(), _full((1, D_MODEL)), _full((1, D_MODEL))],
                     [_rows(), _rows(), _full((1, D_MODEL)), _full((1, D_MODEL))],
                     [_sds((T, D_MODEL), F32), _sds((T, D_MODEL), BF16), _sds((1, D_MODEL), F32),
                      _sds((1, D_MODEL), F32)], (da2, z, u1, ln_g, ln_b))


def _conv_bwd(proj, du1, dw, Bl, S):
    tr = CONV_TILE
    nj = S // tr
    hb = tr // CONV_HALO

    def body(a_ref, b_ref, ah_ref, bh_ref, du_ref, duh_ref, dw_ref, dab_ref, ddw_ref, ddb_ref, uext, dext, du0,
             ushift, dshift, ddw8):
        first = (pl.program_id(0) == 0) & (pl.program_id(1) == 0)
        last = (pl.program_id(0) == Bl - 1) & (pl.program_id(1) == nj - 1)
        j = pl.program_id(1)

        @pl.when(first)
        def _():
            ddw8[...] = jnp.zeros_like(ddw8)
            ddb_ref[...] = jnp.zeros_like(ddb_ref)

        halo = ah_ref[0].astype(F32) * _sigmoid(bh_ref[0].astype(F32))
        uext[pl.ds(0, CONV_HALO), :] = jnp.where(j > 0, halo, 0.0)
        av = a_ref[0].astype(F32)
        sb = _sigmoid(b_ref[0].astype(F32))
        uext[pl.ds(CONV_HALO, tr), :] = av * sb
        dext[pl.ds(0, tr), :] = du_ref[0]
        dext[pl.ds(tr, CONV_HALO), :] = jnp.where(j < nj - 1, duh_ref[0], 0.0)
        ddb_ref[...] += jnp.sum(du_ref[0], axis=0, keepdims=True)

        def cols(c, carry):
            cs = pl.ds(pl.multiple_of(c * LANES, LANES), LANES)
            _fill_shifted(dshift, dext, cs)
            _fill_shifted(ushift, uext, cs)
            for rc in range(tr // CONV_CHUNK):
                base = rc * CONV_CHUNK
                acc = jnp.zeros((CONV_CHUNK, LANES), F32)
                for w in range(CONV_WIDTH):
                    acc = acc + dw_ref[pl.ds(w, 1), cs] * _shifted_rows(dshift, dext, cs, base + CONV_WIDTH - 1 - w)
                du0[pl.ds(base, CONV_CHUNK), cs] = acc
            for w in range(CONV_WIDTH):
                part = jnp.zeros((SUBLANES, LANES), F32)
                for rc in range(tr // CONV_CHUNK):
                    base = rc * CONV_CHUNK
                    prod = dext[pl.ds(base, CONV_CHUNK), cs] * _shifted_rows(
                        ushift, uext, cs, base + CONV_HALO - (CONV_WIDTH - 1) + w)
                    for i in range(CONV_CHUNK // SUBLANES):
                        part = part + prod[i * SUBLANES:(i + 1) * SUBLANES]
                ddw8[pl.ds(w * SUBLANES, SUBLANES), cs] += part
            return carry

        lax.fori_loop(0, D_MODEL // LANES, cols, 0)
        g = du0[...]
        dab_ref[0, :, 0:D_MODEL] = (g * sb).astype(BF16)
        dab_ref[0, :, D_MODEL:2 * D_MODEL] = (g * av * sb * (1.0 - sb)).astype(BF16)

        @pl.when(last)
        def _():
            for w in range(CONV_WIDTH + 1):
                ddw_ref[pl.ds(w, 1), :] = jnp.sum(ddw8[pl.ds(w * SUBLANES, SUBLANES), :], axis=0, keepdims=True)

    tile = lambda cb: pl.BlockSpec((1, tr, D_MODEL), lambda b, j: (b, j, cb))
    halo = lambda cb: pl.BlockSpec((1, CONV_HALO, D_MODEL), lambda b, j: (b, jnp.maximum(j * hb - 1, 0), cb))
    nxt = pl.BlockSpec((1, CONV_HALO, D_MODEL), lambda b, j: (b, jnp.minimum((j + 1) * hb, S // CONV_HALO - 1), 0))
    par = lambda r: pl.BlockSpec((r, D_MODEL), lambda b, j: (0, 0))
    p3 = proj.reshape(Bl, S, 2 * D_MODEL)
    d3 = du1.reshape(Bl, S, D_MODEL)
    dab, ddw, ddb = pl.pallas_call(
        body, name="conv_bwd", grid=(Bl, nj),
        in_specs=[tile(0), tile(1), halo(0), halo(1), tile(0), nxt, par(32)],
        out_specs=[pl.BlockSpec((1, tr, 2 * D_MODEL), lambda b, j: (b, j, 0)), par(32), par(1)],
        out_shape=[_sds((Bl, S, 2 * D_MODEL), BF16), _sds((32, D_MODEL), F32), _sds((1, D_MODEL), F32)],
        scratch_shapes=[pltpu.VMEM((tr + CONV_HALO, D_MODEL), F32), pltpu.VMEM((tr + CONV_HALO, D_MODEL), F32),
                        pltpu.VMEM((tr, D_MODEL), F32),
                        pltpu.VMEM((SUBLANES - 1, CONV_SHIFT_ROWS, LANES), F32),
                        pltpu.VMEM((SUBLANES - 1, CONV_SHIFT_ROWS, LANES), F32),
                        pltpu.VMEM(((CONV_WIDTH + 1) * SUBLANES, D_MODEL), F32)],
        compiler_params=_params(("arbitrary", "arbitrary")),
    )(p3, p3, p3, p3, d3, d3, dw)
    return dab.reshape(Bl * S, 2 * D_MODEL), ddw, ddb


_LAYOUT = (
    ("pre_norm_g", (4, 1024), None), ("post_norm_g", (4, 1024), None),
    ("attn_w_in", (2, 1024, 2560), 2), ("attn_w_out", (2, 256, 1024), 1),
    ("conv_w_in", (2, 1024, 768), 2), ("conv_dw_w", (2, 31, 256), 2),
    ("conv_dw_b", (2, 256), 1), ("conv_ln_g", (2, 256), 1), ("conv_ln_b", (2, 256), 1),
    ("conv_w_out", (2, 256, 1024), 1), ("ple_w_proj", (4, 256, 256), 2), ("ple_w_gate", (4, 256, 1024), 1),
)
_MATMUL_WEIGHTS = ("attn_w_in", "attn_w_out", "conv_w_in", "conv_w_out", "ple_w_proj", "ple_w_gate")
_FIRST_LAYER = ("attn_w_in", "attn_w_out", "ple_w_proj", "ple_w_gate")
_AXIS = {n: a for n, _, a in _LAYOUT}


def _size(shape):
    n = 1
    for s in shape:
        n *= s
    return n


def _padded_rows(shape):
    rows = _size(shape) // shape[-1]
    return rows + (-rows) % FLAT_ROW_ALIGN


def _rows2d(a):
    a2 = a.reshape(-1, a.shape[-1])
    pad = _padded_rows(a.shape) - a2.shape[0]
    return jnp.pad(a2, ((0, pad), (0, 0))) if pad else a2


def _col_blocks(a):
    a2 = _rows2d(a)
    return jnp.concatenate([a2[:, c:c + FLAT_COLS] for c in range(0, a2.shape[1], FLAT_COLS)], axis=0)


def _from_col_blocks(flat, off, shape):
    rows, nblk = _padded_rows(shape), shape[-1] // FLAT_COLS
    a2 = jnp.concatenate([flat[off + b * rows:off + (b + 1) * rows] for b in range(nblk)], axis=1)
    return a2[:_size(shape) // shape[-1]].reshape(shape), off + nblk * rows


def _shard_col_blocks(full, shape, axis):
    if axis is None:
        blocks = _col_blocks(full)
        return jnp.broadcast_to(blocks[None], (N_CHIPS,) + blocks.shape)
    m = shape[-1]
    if axis == len(shape) - 1:
        a2 = _rows2d(full)
        pieces = [a2[:, c:c + FLAT_COLS] for c in range(0, N_CHIPS * m, FLAT_COLS)]
    else:
        layers, r, _ = shape
        assert axis == 1 and (layers * r) % FLAT_ROW_ALIGN == 0
        pieces = [full[:, s * r:(s + 1) * r, c:c + FLAT_COLS].reshape(layers * r, FLAT_COLS)
                  for s in range(N_CHIPS) for c in range(0, m, FLAT_COLS)]
    return jnp.concatenate(pieces, axis=0).reshape(N_CHIPS, -1, FLAT_COLS)


_FLAT_BIG = ("attn_w_in", "conv_w_in", "ple_w_gate", "attn_w_out", "conv_w_out", "ple_w_proj")
_FLAT_SMALL = ("pre_norm_g", "post_norm_g", "conv_dw_w", "conv_dw_b", "conv_ln_g", "conv_ln_b")
PACK_TILE = 1024


_FIRST_LAYER_PARAMS = _FIRST_LAYER + ("pre_norm_g", "post_norm_g")


def _part_shapes(first):
    out = {}
    for n, shape, _ in _LAYOUT:
        layers = (1 if first else shape[0] - 1) if n in _FIRST_LAYER_PARAMS else (0 if first else shape[0])
        if layers and not (first and n == "pre_norm_g"):
            out[n] = (layers,) + shape[1:]
    return out


def _flat_plan(shapes):
    out, off = {}, 0
    for n in _FLAT_BIG + _FLAT_SMALL:
        if n in shapes:
            out[n] = off
            off += _padded_rows(shapes[n]) * (shapes[n][-1] // FLAT_COLS)
    return out, off + (-off) % (2 * FLAT_TILE)


def _unpack_f32(flat, shapes):
    offsets, _ = _flat_plan(shapes)
    return {n: _from_col_blocks(flat, offsets[n], shapes[n])[0] for n in shapes}


def _pack_param(full, shape, axis, off, flat, total_rows):
    layers, r, m = shape
    nblk = m // FLAT_COLS
    if axis == 2:
        rows = layers * r
        tr = math.gcd(math.gcd(rows, PACK_TILE), off) if off else math.gcd(rows, PACK_TILE)
        assert rows % tr == 0 and off % tr == 0 and tr % FLAT_ROW_ALIGN == 0
        src = full.reshape(rows, N_CHIPS * m)
        grid = (N_CHIPS * nblk, rows // tr)
        in_spec = pl.BlockSpec((tr, FLAT_COLS), lambda j, i: (i, j))
        out_spec = pl.BlockSpec((None, tr, FLAT_COLS), lambda j, i: (j // nblk, (off + (j % nblk) * rows) // tr + i, 0))
    else:
        assert axis == 1 and off % r == 0
        src = full.reshape(layers * N_CHIPS * r, m)
        grid = (layers, N_CHIPS, nblk)
        in_spec = pl.BlockSpec((r, FLAT_COLS), lambda l, s, b: (l * N_CHIPS + s, b))
        out_spec = pl.BlockSpec((None, r, FLAT_COLS), lambda l, s, b: (s, (off + b * layers * r) // r + l, 0))

    def copy_body(src_ref, *rest):
        rest[-1][...] = src_ref[...]

    args, in_specs, aliases = [src], [in_spec], {}
    if flat is not None:
        args.append(flat)
        in_specs.append(_ANY)
        aliases = {1: 0}
    return pl.pallas_call(
        copy_body, name="pack_grad", grid=grid, in_specs=in_specs, out_specs=out_spec,
        out_shape=_sds((N_CHIPS, total_rows, FLAT_COLS), F32), input_output_aliases=aliases,
        compiler_params=_params(("arbitrary",) * len(grid)))(*args)


SMALL_ROWS = 40


def _stack_small(w):
    rows = [w["conv_dw_w"]] + [w[n][:, None, :] for n in ("conv_dw_b", "conv_ln_g", "conv_ln_b")]
    stacked = jnp.concatenate(rows, axis=1)
    return jnp.pad(stacked, ((0, 0), (0, SMALL_ROWS - stacked.shape[1]), (0, 0)))


def _unstack_small(small):
    return {"conv_dw_w": small[:, :CONV_WIDTH], "conv_dw_b": small[:, CONV_WIDTH],
            "conv_ln_g": small[:, CONV_WIDTH + 1], "conv_ln_b": small[:, CONV_WIDTH + 2]}


def _pack_full_grads(grads, shapes):
    offsets, total_rows = _flat_plan(shapes)
    flat = None
    for n in _FLAT_BIG:
        if n in shapes:
            flat = _pack_param(grads[n], shapes[n], _AXIS[n], offsets[n], flat, total_rows)
    small_names = [n for n in _FLAT_SMALL if n in shapes]
    small = jnp.concatenate([_shard_col_blocks(grads[n], shapes[n], _AXIS[n]) for n in small_names], axis=1)
    start = offsets[small_names[0]]
    small = jnp.pad(small, ((0, 0), (0, total_rows - start - small.shape[1]), (0, 0)))
    return lax.dynamic_update_slice(flat, small, (0, start, 0))


_ANY = pl.BlockSpec(memory_space=pl.ANY)


def _mesh_pos():
    return lax.axis_index("x"), lax.axis_index("y"), lax.axis_index("c")


def _other_chips(x, y):
    return [(1 - x, y), (x, 1 - y), (1 - x, 1 - y)]


COPIES_PER_ARRAY = 7


def _allgather_weights(shards, axes, small):
    n = len(shards)
    full_shape = lambda a, axis: tuple(d * (N_CHIPS if i == axis else 1) for i, d in enumerate(a.shape))

    def body(*refs):
        ins, small_in = refs[:n], refs[n]
        outs, small_out = refs[n + 1:2 * n + 1], refs[2 * n + 1]
        send_sems, recv_sems = refs[2 * n + 2:]
        x, y, c = _mesh_pos()
        mine, me, sibling = 2 * x + y, (x, y, c), (x, y, 1 - c)
        chips = _other_chips(x, y)

        def region(a, chip, half):
            _, rows, cols = shards[a].shape
            h = rows // 2
            if axes[a] == 2:
                return outs[a].at[:, slice(None) if half is None else pl.ds(half * h, h), pl.ds(chip * cols, cols)]
            if half is None:
                return outs[a].at[:, pl.ds(chip * rows, rows), :]
            return outs[a].at[:, pl.ds(chip * rows + half * h, h), :]

        def copy(k, src, dst, to):
            return pltpu.make_async_remote_copy(src_ref=src, dst_ref=dst, send_sem=send_sems.at[k],
                                                recv_sem=recv_sems.at[k], device_id=to, device_id_type=MESH)

        def arrival(k, dst):
            return copy(k, dst, dst, me)

        sends = []
        for a in range(n):
            h = shards[a].shape[1] // 2
            base = a * COPIES_PER_ARRAY
            sends.append(copy(base + 6, ins[a], region(a, mine, None), sibling))
            for j, (cx, cy) in enumerate(chips):
                sends.append(copy(base + j, ins[a].at[:, pl.ds(c * h, h), :], region(a, mine, c), (cx, cy, c)))
        small_cols = small.shape[2]
        small_region = lambda chip: small_out.at[:, :, pl.ds(chip * small_cols, small_cols)]
        base = n * COPIES_PER_ARRAY
        sends.append(copy(base + 3, small_in, small_region(mine), sibling))
        for j, (cx, cy) in enumerate(chips):
            sends.append(copy(base + j, small_in, small_region(mine), (cx, cy, c)))
        for cp in sends:
            cp.start()
        for j, (cx, cy) in enumerate(chips):
            for a in range(n):
                k = a * COPIES_PER_ARRAY + j
                arrival(k, region(a, 2 * cx + cy, c)).wait_recv()
                passed = copy(k + 3, region(a, 2 * cx + cy, c), region(a, 2 * cx + cy, c), sibling)
                passed.start()
                sends.append(passed)
        for j, (cx, cy) in enumerate(chips):
            for a in range(n):
                arrival(a * COPIES_PER_ARRAY + 3 + j, region(a, 2 * cx + cy, 1 - c)).wait_recv()
            arrival(base + j, small_region(2 * cx + cy)).wait_recv()
        for a in range(n):
            arrival(a * COPIES_PER_ARRAY + 6, region(a, mine, None)).wait_recv()
        arrival(base + 3, small_region(mine)).wait_recv()
        for cp in sends:
            cp.wait_send()

    n_sems = n * COPIES_PER_ARRAY + 4
    out = pl.pallas_call(
        body, name="allgather_weights", in_specs=[_ANY] * (n + 1), out_specs=[_ANY] * (n + 1),
        out_shape=[_sds(full_shape(a, axis), a.dtype) for a, axis in zip(shards, axes)]
        + [_sds(full_shape(small, 2), small.dtype)],
        scratch_shapes=[pltpu.SemaphoreType.DMA((n_sems,)), pltpu.SemaphoreType.DMA((n_sems,))],
    )(*shards, small)
    return out[:n], out[n]


_HBM = pl.BlockSpec(memory_space=pltpu.HBM)
_SEM = pl.BlockSpec(memory_space=pltpu.SEMAPHORE)


def _full_shape(shard, axis):
    return tuple(d * (N_CHIPS if i == axis else 1) for i, d in enumerate(shard.shape))


def _direct_gather_copies(srcs, lands, shapes, axes, send_sems, recv_sems):
    x, y, c = _mesh_pos()
    mine, me, sibling = 2 * x + y, (x, y, c), (x, y, 1 - c)

    def region(a, chip, half):
        _, rows, cols = shapes[a]
        h = rows // 2
        if axes[a] == 2:
            return lands[a].at[:, slice(None) if half is None else pl.ds(half * h, h), pl.ds(chip * cols, cols)]
        if half is None:
            return lands[a].at[:, pl.ds(chip * rows, rows), :]
        return lands[a].at[:, pl.ds(chip * rows + half * h, h), :]

    def copy(k, src, dst, to):
        return pltpu.make_async_remote_copy(src_ref=src, dst_ref=dst, send_sem=send_sems.at[k],
                                            recv_sem=recv_sems.at[k], device_id=to, device_id_type=MESH)

    outgoing, incoming = [], []
    for a in range(len(srcs)):
        h = shapes[a][1] // 2
        base = a * COPIES_PER_ARRAY
        for j, (cx, cy) in enumerate(_other_chips(x, y)):
            for t, ct in enumerate((c, 1 - c)):
                k = base + 2 * j + t
                outgoing.append(copy(k, srcs[a].at[:, pl.ds(c * h, h), :], region(a, mine, c), (cx, cy, ct)))
                landed = region(a, 2 * cx + cy, ct)
                incoming.append(copy(k, landed, landed, me))
        outgoing.append(copy(base + 6, srcs[a], region(a, mine, None), sibling))
        incoming.append(copy(base + 6, region(a, mine, None), region(a, mine, None), me))
    return outgoing, incoming


def _allgather_start(shards, axes):
    n = len(shards)
    shapes = [s.shape for s in shards]

    def body(*refs):
        srcs, lands = refs[:n], refs[n:2 * n]
        send_sems, recv_sems = refs[2 * n], refs[2 * n + 1]
        token = refs[-1]
        outgoing, _ = _direct_gather_copies(srcs, lands, shapes, axes, send_sems, recv_sems)
        for cp in outgoing:
            cp.start()
        token[...] = jnp.zeros_like(token)

    n_sems = n * COPIES_PER_ARRAY
    zones = [pltpu.with_memory_space_constraint(lax.empty(_full_shape(s, ax), s.dtype), pltpu.HBM)
             for s, ax in zip(shards, axes)]
    out = pl.pallas_call(
        body, name="allgather_rest_start",
        out_shape=(pltpu.SemaphoreType.DMA((n_sems,)), pltpu.SemaphoreType.DMA((n_sems,)),
                   *[pltpu.HBM(s.shape, s.dtype) for s in shards], *[pltpu.HBM(z.shape, z.dtype) for z in zones],
                   jax.ShapeDtypeStruct((8, LANES), F32)),
        in_specs=[_HBM] * (2 * n),
        out_specs=(_SEM, _SEM, *[_HBM] * (2 * n), pl.BlockSpec(memory_space=pltpu.VMEM)),
        input_output_aliases={i: 2 + i for i in range(2 * n)},
        compiler_params=pltpu.CompilerParams(has_side_effects=pltpu.SideEffectType.DATAFLOW_SIDE_EFFECTING),
    )(*[pltpu.with_memory_space_constraint(s, pltpu.HBM) for s in shards], *zones)
    return out[0], out[1], out[2:2 + n], out[2 + n:2 + 2 * n], out[-1]


def _allgather_wait(send_sems, recv_sems, shards, zones, axes, after):
    n = len(shards)
    shapes = [s.shape for s in shards]

    def body(*refs):
        srcs, lands = refs[:n], refs[n:2 * n]
        outgoing, incoming = _direct_gather_copies(srcs, lands, shapes, axes, refs[2 * n], refs[2 * n + 1])
        for cp in outgoing:
            cp.wait_send()
        for cp in incoming:
            cp.wait_recv()

    out = pl.pallas_call(
        body, name="allgather_rest_wait",
        out_shape=(*[pltpu.HBM(s.shape, s.dtype) for s in shards], *[pltpu.HBM(z.shape, z.dtype) for z in zones]),
        in_specs=[_HBM] * (2 * n) + [_SEM, _SEM, _ANY], out_specs=[_HBM] * (2 * n),
        input_output_aliases={i: i for i in range(2 * n)},
        compiler_params=pltpu.CompilerParams(has_side_effects=pltpu.SideEffectType.DATAFLOW_SIDE_EFFECTING),
    )(*shards, *zones, send_sems, recv_sems, after)
    return out[n:]


def _exchange_core_halves(g):
    n, _, H, C = g.shape

    def body(g_ref, got_ref, send_sem, recv_sem):
        x, y, c = _mesh_pos()
        swap = pltpu.make_async_remote_copy(
            src_ref=g_ref.at[pl.ds(0, n), 1 - c], dst_ref=got_ref, send_sem=send_sem, recv_sem=recv_sem,
            device_id=(x, y, 1 - c), device_id_type=MESH)
        swap.start()
        swap.wait()

    return pl.pallas_call(
        body, name="exchange_core_halves", in_specs=[_ANY], out_specs=_ANY,
        out_shape=_sds((n, H, C), g.dtype),
        scratch_shapes=[pltpu.SemaphoreType.DMA, pltpu.SemaphoreType.DMA],
    )(g)


def _scatter_copies(p_ref, q_ref, send_sems, recv_sems):
    x, y, c = _mesh_pos()
    return [pltpu.make_async_remote_copy(
        src_ref=p_ref.at[2 * cx + cy], dst_ref=q_ref.at[j], send_sem=send_sems.at[j],
        recv_sem=recv_sems.at[j], device_id=(cx, cy, c), device_id_type=MESH)
        for j, (cx, cy) in enumerate(_other_chips(x, y))]


def _scatter_start(p, part):
    n, H, C = p.shape

    def body(p_ref, q_ref, send_sems, recv_sems, p_thru, q_thru, token):
        for cp in _scatter_copies(p_ref, q_ref, send_sems, recv_sems):
            cp.start()
        token[...] = jnp.zeros_like(token)

    zone = pltpu.with_memory_space_constraint(lax.empty((n - 1, H, C), p.dtype), pltpu.HBM)
    return pl.pallas_call(
        body, name="scatter_%s_start" % part,
        out_shape=(pltpu.SemaphoreType.DMA((n - 1,)), pltpu.SemaphoreType.DMA((n - 1,)), pltpu.HBM(p.shape, p.dtype),
                   pltpu.HBM(zone.shape, zone.dtype), jax.ShapeDtypeStruct((8, LANES), F32)),
        in_specs=[_HBM, _HBM], out_specs=(_SEM, _SEM, _HBM, _HBM, pl.BlockSpec(memory_space=pltpu.VMEM)),
        input_output_aliases={0: 2, 1: 3},
        compiler_params=pltpu.CompilerParams(has_side_effects=pltpu.SideEffectType.DATAFLOW_SIDE_EFFECTING),
    )(pltpu.with_memory_space_constraint(p, pltpu.HBM), zone)


def _scatter_wait(send_sems, recv_sems, p, zone, after, part):
    def body(p_ref, q_ref, send_sems, recv_sems, after_ref, p_out, q_out):
        copies = _scatter_copies(p_ref, q_ref, send_sems, recv_sems)
        for cp in copies:
            cp.wait_send()
        for cp in copies:
            cp.wait_recv()

    return pl.pallas_call(
        body, name="scatter_%s_wait" % part,
        out_shape=(pltpu.HBM(p.shape, p.dtype), pltpu.HBM(zone.shape, zone.dtype)),
        in_specs=[_HBM, _HBM, _SEM, _SEM, _ANY], out_specs=[_HBM, _HBM], input_output_aliases={0: 0, 1: 1},
        compiler_params=pltpu.CompilerParams(has_side_effects=pltpu.SideEffectType.DATAFLOW_SIDE_EFFECTING),
    )(p, zone, send_sems, recv_sems, after)


N_DEVICES = 8


def _allreduce_row(v):
    C = v.shape[1]

    def body(v_ref, out_ref, rows, send_sems, recv_sems):
        x, y, c = _mesh_pos()
        me = 4 * x + 2 * y + c
        rows[pl.ds(me, 1)] = v_ref[...].reshape(1, 1, C)
        copies = []
        for m in range(1, N_DEVICES):
            peer = (x ^ (m >> 2), y ^ ((m >> 1) & 1), c ^ (m & 1))
            copies.append(pltpu.make_async_remote_copy(
                src_ref=rows.at[me], dst_ref=rows.at[me], send_sem=send_sems.at[m - 1],
                recv_sem=recv_sems.at[m - 1], device_id=peer, device_id_type=MESH))
        for cp in copies:
            cp.start()
        for m, cp in enumerate(copies, start=1):
            cp.wait_send()
            pltpu.make_async_remote_copy(
                src_ref=rows.at[me], dst_ref=rows.at[me ^ m], send_sem=send_sems.at[m - 1],
                recv_sem=recv_sems.at[m - 1], device_id=(x, y, c), device_id_type=MESH).wait_recv()
        total = rows[0]
        for d in range(1, N_DEVICES):
            total = total + rows[d]
        out_ref[...] = total

    vmem = pl.BlockSpec(memory_space=pltpu.VMEM)
    return pl.pallas_call(
        body, name="allreduce_row", in_specs=[vmem], out_specs=vmem, out_shape=_sds((1, C), F32),
        scratch_shapes=[pltpu.VMEM((N_DEVICES, 1, C), F32), pltpu.SemaphoreType.DMA((N_DEVICES - 1,)),
                        pltpu.SemaphoreType.DMA((N_DEVICES - 1,))],
    )(v)


def _share_core_halves(r2):
    _, H, C = r2.shape

    def body(r_ref, out_ref, send_sem, recv_sem):
        x, y, c = _mesh_pos()
        send = pltpu.make_async_remote_copy(
            src_ref=r_ref.at[c], dst_ref=out_ref.at[c], send_sem=send_sem, recv_sem=recv_sem,
            device_id=(x, y, 1 - c), device_id_type=MESH)
        send.start()
        send.wait_send()
        pltpu.make_async_remote_copy(
            src_ref=r_ref.at[c], dst_ref=out_ref.at[1 - c], send_sem=send_sem, recv_sem=recv_sem,
            device_id=(x, y, 1 - c), device_id_type=MESH).wait_recv()

    return pl.pallas_call(
        body, name="share_core_halves", in_specs=[_ANY], out_specs=_ANY,
        out_shape=_sds(r2.shape, r2.dtype), input_output_aliases={0: 0},
        scratch_shapes=[pltpu.SemaphoreType.DMA, pltpu.SemaphoreType.DMA],
    )(r2)


def _place():
    x, y, c = _mesh_pos()
    return jnp.stack([c, 2 * x + y]).astype(jnp.int32)


def _sum_pair(g, got, place):
    n, _, H, C = g.shape

    def body(place_ref, a_ref, b_ref, o_ref):
        o_ref[...] = (a_ref[...] + b_ref[...]).astype(BF16)

    spec = pl.BlockSpec((1, FLAT_TILE, C), lambda s, i, pr: (s, i, 0))
    return pl.pallas_call(
        body, name="sum_core_pair",
        grid_spec=pltpu.PrefetchScalarGridSpec(
            num_scalar_prefetch=1, grid=(n, H // FLAT_TILE),
            in_specs=[pl.BlockSpec((1, None, FLAT_TILE, C), lambda s, i, pr: (s, pr[0], i, 0)), spec],
            out_specs=spec),
        out_shape=_sds((n, H, C), BF16),
        compiler_params=_params(("parallel", "parallel")))(place, g, got)


def _sum_chips(p, q, place):
    n, H, C = p.shape

    def body(place_ref, own_ref, qx_ref, qy_ref, qxy_ref, o_ref):
        mine = place_ref[1]
        own, qx, qy, qxy = (t[0].astype(F32) for t in (own_ref, qx_ref, qy_ref, qxy_ref))

        def term(s):
            rel = jnp.full(own.shape, mine ^ s, jnp.int32)
            return jnp.where(rel == 0, own, jnp.where(rel == 2, qx, jnp.where(rel == 1, qy, qxy)))

        o_ref[0] = ((term(0) + term(1)) + term(2)) + term(3)

    qspec = lambda j: pl.BlockSpec((1, FLAT_TILE, C), lambda i, pr: (j, i, 0))
    return pl.pallas_call(
        body, name="sum_chips",
        grid_spec=pltpu.PrefetchScalarGridSpec(
            num_scalar_prefetch=1, grid=(H // FLAT_TILE,),
            in_specs=[pl.BlockSpec((1, FLAT_TILE, C), lambda i, pr: (pr[1], i, 0)), qspec(0), qspec(1), qspec(2)],
            out_specs=pl.BlockSpec((1, FLAT_TILE, C), lambda i, pr: (pr[0], i, 0))),
        out_shape=_sds((2, H, C), F32),
        compiler_params=_params(("parallel",)))(place, p, q, q, q)


ADAMW_BLOCK = 1 << 18


def _adamw(w, g, m, v):
    shape = w.shape
    C = shape[-1]
    R = _size(shape) // C
    tr = R
    while tr * C > ADAMW_BLOCK and tr % 16 == 0:
        tr //= 2
    w, g, m, v = (t.reshape(R, C) for t in (w, g, m, v))

    def body(w_ref, g_ref, m_ref, v_ref, d_ref, nm_ref, nv_ref):
        gv = g_ref[...]
        nm = ADAM_B1 * m_ref[...] + (1.0 - ADAM_B1) * gv
        nv = ADAM_B2 * v_ref[...] + (1.0 - ADAM_B2) * (gv * gv)
        m_hat = nm / (1.0 - ADAM_B1 ** ADAM_STEP)
        v_hat = nv / (1.0 - ADAM_B2 ** ADAM_STEP)
        d_ref[...] = -ADAM_LR * (m_hat / (jnp.sqrt(v_hat) + ADAM_EPS) + ADAM_WD * w_ref[...])
        nm_ref[...] = nm
        nv_ref[...] = nv

    spec = pl.BlockSpec((tr, C), lambda i: (i, 0))
    outs = pl.pallas_call(body, name="adamw", grid=(R // tr,), in_specs=[spec] * 4, out_specs=[spec] * 3,
                          out_shape=[_sds((R, C), F32)] * 3, compiler_params=_params(("parallel",)))(w, g, m, v)
    return tuple(t.reshape(shape) for t in outs)


def _chip_partials(gfull, place):
    n, R, C = gfull.shape
    g4 = gfull.reshape(n, 2, R // 2, C)
    return _sum_pair(g4, _exchange_core_halves(g4), place)


def _finish_reduce(p, q, place):
    r2 = _share_core_halves(_sum_chips(p, q, place))
    return r2.reshape(2 * r2.shape[1], r2.shape[2])


def _local_step(x, p, positions, loss_target, pre_g, post_g, w, later_weights, later_grads, first_grads):
    Bl, S, _ = x.shape
    T = Bl * S
    cos, sin = _rope_tables(positions)
    xs = x.reshape(T, D_MODEL)
    saved = []
    for i in range(DEPTH):
        if i == 1:
            later_weights(xs)
        j = i // 2
        g_pre, g_post = pre_g[i:i + 1], post_g[i:i + 1]
        h = _rmsnorm_fwd(xs, g_pre)
        st = {"x": xs, "h": h}
        if i % 2 == 0:
            proj = _mm(h, w["attn_w_in"][j], name="attn_in")
            res = [_attn_fwd(proj, cos, sin, g, Bl, S) for g in range(N_GROUPS)]
            a, o, lse = _attn_combine([r[0] for r in res], [r[1] for r in res], proj)
            w_out = w["attn_w_out"][j]
            st.update(proj=proj, a=a, o=o, lse=lse, qkv=[r[2] for r in res])
        else:
            w_ab, w_z = w["conv_w_in"][j][:, :2 * D_MODEL], w["conv_w_in"][j][:, 2 * D_MODEL:]
            ab = _mm(h, w_ab, out_dtype=BF16, name="conv_in_ab")
            z = _mm(h, w_z, out_dtype=BF16, name="conv_in_z")
            dw = jnp.pad(w["conv_dw_w"][j], ((0, 1), (0, 0)))
            u1, a = _conv_fwd(ab, z, dw, w["conv_dw_b"][j:j + 1], w["conv_ln_g"][j:j + 1],
                              w["conv_ln_b"][j:j + 1], Bl, S)
            w_out = w["conv_w_out"][j]
            st.update(w_ab=w_ab, w_z=w_z, ab=ab, z=z, dw=dw, u1=u1, a=a)
        y, x1 = _mm_rows(a, w_out, [xs, g_post], _post_epilogue, (F32, F32), "branch_out_post")
        pi = p[i].reshape(T, PLE_DIM)
        pe = _mm(pi, w["ple_w_proj"][i], name="ple_proj")
        gl, xs = _mm_rows(x1, w["ple_w_gate"][i], [pe], _ple_epilogue, (F32, F32), "ple_gate_fwd")
        st.update(y=y, x1=x1, pi=pi, pe=pe, gl=gl)
        saved.append(st)

    sq, dx = _loss_fwd_bwd(xs, loss_target.reshape(T, D_MODEL))

    grads = {n: [None] * shape[0] for n, shape, _ in _LAYOUT}
    for i in reversed(range(DEPTH)):
        j = i // 2
        st = saved[i]
        g_pre, g_post = pre_g[i:i + 1], post_g[i:i + 1]
        if i == 0:
            rest = {n: jnp.stack(v[1:] if n in _FIRST_LAYER_PARAMS else v) for n, v in grads.items()}
            g_post = g_post + later_grads(rest)[0, 0]
        dpe, dgl, dx1, dy, dg_post = _ple_post_bwd(dx, st["pe"], st["gl"], w["ple_w_gate"][i], st["y"], g_post)
        grads["ple_w_proj"][i] = _mm(st["pi"], dpe, ta=True, name="ple_proj_wgrad")
        grads["ple_w_gate"][i] = _mm(st["x1"], dgl, ta=True, name="ple_gate_wgrad")
        grads["post_norm_g"][i] = dg_post[0]
        if i % 2 == 0:
            grads["attn_w_out"][j] = _mm(st["a"], dy, ta=True, name="attn_out_wgrad")
            da = _mm(dy, w["attn_w_out"][j], tb=True, name="attn_out_dgrad")
            do, dproj = _gate_bwd(da, st["o"], st["proj"])
            for g in range(N_GROUPS):
                dproj = _attn_bwd(st["qkv"][g], cos, sin, do, st["o"], st["lse"], dproj, g, Bl, S)
            grads["attn_w_in"][j] = _mm(st["h"], dproj, ta=True, name="attn_in_wgrad")
            if i == 0:
                first = {n: jnp.stack(grads[n][:1]) for n in _FIRST_LAYER_PARAMS if n != "pre_norm_g"}
                g_pre = g_pre + first_grads(first)[0, 0]
            dx, dg_pre = _dgrad_pre_bwd(dproj, w["attn_w_in"][j], st["x"], g_pre, dx1, None, 2048, "attn_in_dgrad_pre")
        else:
            grads["conv_w_out"][j] = _mm(st["a"], dy, ta=True, name="conv_out_wgrad")
            da2 = _mm(dy, w["conv_w_out"][j], tb=True, name="conv_out_dgrad")
            du1, dz, dln_g, dln_b = _conv_norm_bwd(da2, st["z"], st["u1"], w["conv_ln_g"][j:j + 1],
                                                   w["conv_ln_b"][j:j + 1])
            dab, ddw, ddb = _conv_bwd(st["ab"], du1, st["dw"], Bl, S)
            dh = _mm(dz, st["w_z"], tb=True, name="conv_in_z_dgrad")
            dx, dg_pre = _dgrad_pre_bwd(dab, st["w_ab"], st["x"], g_pre, dx1, dh, 2048, "conv_in_dgrad_pre")
            dw_ab = _mm(st["h"], dab, ta=True, name="conv_in_ab_wgrad")
            dw_z = _mm(st["h"], dz, ta=True, name="conv_in_z_wgrad")
            grads["conv_w_in"][j] = jnp.concatenate([dw_ab, dw_z], axis=1)
            grads["conv_dw_w"][j] = ddw[:CONV_WIDTH]
            grads["conv_dw_b"][j] = ddb[0]
            grads["conv_ln_g"][j] = dln_g[0]
            grads["conv_ln_b"][j] = dln_b[0]
        grads["pre_norm_g"][i] = dg_pre[0]
    return sq, dx.reshape(Bl, S, D_MODEL), grads["pre_norm_g"][0][None]


_NAMES = tuple(n for n, _, _ in _LAYOUT)


def kernel(x, p, positions, pre_norm_g, post_norm_g, attn_w_in, attn_w_out, conv_w_in, conv_dw_w, conv_dw_b, conv_ln_g, conv_ln_b, conv_w_out, ple_w_proj, ple_w_gate, loss_target, m_pre_norm_g, m_post_norm_g, m_attn_w_in, m_attn_w_out, m_conv_w_in, m_conv_dw_w, m_conv_dw_b, m_conv_ln_g, m_conv_ln_b, m_conv_w_out, m_ple_w_proj, m_ple_w_gate, v_pre_norm_g, v_post_norm_g, v_attn_w_in, v_attn_w_out, v_conv_w_in, v_conv_dw_w, v_conv_dw_b, v_conv_ln_g, v_conv_ln_b, v_conv_w_out, v_ple_w_proj, v_ple_w_gate):
    w_loc = dict(zip(_NAMES, (pre_norm_g, post_norm_g, attn_w_in, attn_w_out, conv_w_in, conv_dw_w, conv_dw_b,
                              conv_ln_g, conv_ln_b, conv_w_out, ple_w_proj, ple_w_gate)))
    m_loc = dict(zip(_NAMES, (m_pre_norm_g, m_post_norm_g, m_attn_w_in, m_attn_w_out, m_conv_w_in, m_conv_dw_w,
                              m_conv_dw_b, m_conv_ln_g, m_conv_ln_b, m_conv_w_out, m_ple_w_proj, m_ple_w_gate)))
    v_loc = dict(zip(_NAMES, (v_pre_norm_g, v_post_norm_g, v_attn_w_in, v_attn_w_out, v_conv_w_in, v_conv_dw_w,
                              v_conv_dw_b, v_conv_ln_g, v_conv_ln_b, v_conv_w_out, v_ple_w_proj, v_ple_w_gate)))

    bf = {n: w_loc[n].astype(BF16) for n in _MATMUL_WEIGHTS}
    axes = [_AXIS[n] for n in _MATMUL_WEIGHTS]
    first, small = _allgather_weights([bf[n][:1] for n in _FIRST_LAYER], [_AXIS[n] for n in _FIRST_LAYER],
                                      _stack_small(w_loc))
    send_sems, recv_sems, shards, zones, token = _allgather_start(
        [bf[n][1:] if n in _FIRST_LAYER else bf[n] for n in _MATMUL_WEIGHTS], axes)
    w_full = dict({n: [full[0]] for n, full in zip(_FIRST_LAYER, first)}, **_unstack_small(small))

    def later_weights(after):
        for n, full in zip(_MATMUL_WEIGHTS, _allgather_wait(send_sems, recv_sems, shards, zones, axes, after)):
            w_full[n] = w_full[n] + [full[l] for l in range(full.shape[0])] if n in _FIRST_LAYER else full

    place = _place()
    rest_shapes, first_shapes = _part_shapes(False), _part_shapes(True)
    rest_flight, first_flight = [], []

    def later_grads(grads):
        rest_flight.extend(_scatter_start(_chip_partials(_pack_full_grads(grads, rest_shapes), place), "rest"))
        return rest_flight[4]

    def first_grads(grads):
        first_flight.extend(_scatter_start(_chip_partials(_pack_full_grads(grads, first_shapes), place), "first"))
        return first_flight[4]

    sq, grad_x, dg_pre0 = _local_step(x, p, positions, loss_target, pre_norm_g + token[0, 0], post_norm_g,
                                      w_full, later_weights, later_grads, first_grads)
    loss = lax.psum(sq[0, 0] * (0.5 / D_MODEL), ("x", "y", "c"))

    p_rest, q_rest = _scatter_wait(*rest_flight[:4], grad_x, "rest")
    g_rest = _unpack_f32(_finish_reduce(p_rest, q_rest, place), rest_shapes)
    p_first, q_first = _scatter_wait(*first_flight[:4], grad_x, "first")
    g_first = _unpack_f32(_finish_reduce(p_first, q_first, place), first_shapes)
    g_first["pre_norm_g"] = _allreduce_row(dg_pre0)
    g_out = {n: jnp.concatenate([g_first[n], g_rest[n]]) if n in g_first and n in g_rest
             else g_rest.get(n, g_first.get(n)) for n in _NAMES}
    updates = {n: _adamw(w_loc[n], g_out[n], m_loc[n], v_loc[n]) for n in _NAMES}
    d_out, m_out, v_out = ({n: updates[n][k] for n in _NAMES} for k in range(3))
    return (loss, grad_x, *[g_out[n] for n in _NAMES], *[d_out[n] for n in _NAMES],
            *[m_out[n] for n in _NAMES], *[v_out[n] for n in _NAMES])
```

```python
import math

import jax
import jax.numpy as jnp
from jax import lax
from jax.experimental import pallas as pl
from jax.experimental.pallas import tpu as pltpu

F32 = jnp.float32
BF16 = jnp.bfloat16

D_MODEL = 1024
DEPTH = 4
PLE_DIM = 256
HEAD_DIM = 64
WIN_DIL = ((128, 1), (512, 4), (2048, 16))
N_GROUPS = 3
N_BACK = 128
BLOCK_UNROLL = 8
RESIDUES_TOGETHER = 2
ROPE_THETA = 10000.0
CONV_WIDTH = 31
CONV_HALO = 32
RMS_EPS = 1e-6
LN_EPS = 1e-5
NEG_INF = -1e30
ADAM_LR, ADAM_B1, ADAM_B2, ADAM_EPS, ADAM_WD, ADAM_STEP = 0.001, 0.9, 0.999, 1e-08, 0.01, 10

LANES = 128
N_CHIPS = 4
VMEM_LIMIT = 48 * 1024 * 1024
VMEM_LIMIT_ATTN = 56 * 1024 * 1024
FLAT_COLS = 256
FLAT_TILE = 2048
FLAT_ROW_ALIGN = 16
PROJ_COLS = (3 * N_GROUPS + 1) * D_MODEL
HEAD_PAIRS = D_MODEL // LANES

MESH = pl.DeviceIdType.MESH


def _params(sem=None, vmem=VMEM_LIMIT):
    return pltpu.CompilerParams(dimension_semantics=sem, vmem_limit_bytes=vmem)


def _sigmoid(v):
    return 1.0 / (1.0 + jnp.exp(-v))


def _mm(a, b, *, ta=False, tb=False, out_dtype=F32, tm=1024, tn=1024, tk=1024, name="mm"):
    if ta:
        K, M = a.shape
    else:
        M, K = a.shape
    if tb:
        N, K2 = b.shape
    else:
        K2, N = b.shape
    assert K == K2, (a.shape, b.shape)
    tm, tn, tk = min(tm, M), min(tn, N), min(tk, K)
    assert M % tm == 0 and N % tn == 0 and K % tk == 0
    nk = K // tk
    dims = (((0 if ta else 1,), (1 if tb else 0,)), ((), ()))

    def body(a_ref, b_ref, o_ref, *scratch):
        k = pl.program_id(2)
        part = lax.dot_general(a_ref[...].astype(BF16), b_ref[...].astype(BF16), dims, preferred_element_type=F32)
        if nk == 1:
            o_ref[...] = part.astype(out_dtype)
        else:
            acc_ref, = scratch

            @pl.when(k == 0)
            def _():
                acc_ref[...] = part

            @pl.when((k > 0) & (k < nk - 1))
            def _():
                acc_ref[...] += part

            @pl.when(k == nk - 1)
            def _():
                o_ref[...] = (acc_ref[...] + part).astype(out_dtype)

    a_spec = pl.BlockSpec((tk, tm), lambda i, j, k: (k, i)) if ta else pl.BlockSpec((tm, tk), lambda i, j, k: (i, k))
    b_spec = pl.BlockSpec((tn, tk), lambda i, j, k: (j, k)) if tb else pl.BlockSpec((tk, tn), lambda i, j, k: (k, j))
    return pl.pallas_call(
        body, name=name, grid=(M // tm, N // tn, nk),
        in_specs=[a_spec, b_spec], out_specs=pl.BlockSpec((tm, tn), lambda i, j, k: (i, j)),
        out_shape=jax.ShapeDtypeStruct((M, N), out_dtype),
        scratch_shapes=[pltpu.VMEM((tm, tn), F32)] if nk > 1 else [],
        compiler_params=_params(("parallel", "parallel", "arbitrary")),
    )(a, b)


def _mm_rows(a, b, extras, epilogue, out_dtypes, name, tm=512):
    M, K = a.shape
    N = b.shape[1]
    n_ex = len(extras)

    def body(*refs):
        a_ref, b_ref = refs[:2]
        av = a_ref[...]
        acc = jnp.dot(av.astype(BF16), b_ref[...].astype(BF16), preferred_element_type=F32)
        results = epilogue(acc, av, *[e[...] for e in refs[2:2 + n_ex]])
        for o_ref, r in zip(refs[2 + n_ex:], results):
            o_ref[...] = r.astype(o_ref.dtype)

    tile = pl.BlockSpec((tm, N), lambda i: (i, 0))
    in_specs = [pl.BlockSpec((tm, K), lambda i: (i, 0)), pl.BlockSpec((K, N), lambda i: (0, 0))]
    in_specs += [tile if e.shape[0] == M else pl.BlockSpec((1, N), lambda i: (0, 0)) for e in extras]
    return pl.pallas_call(
        body, name=name, grid=(M // tm,), in_specs=in_specs, out_specs=[tile] * len(out_dtypes),
        out_shape=[jax.ShapeDtypeStruct((M, N), dt) for dt in out_dtypes],
        compiler_params=_params(("parallel",)),
    )(a, b, *extras)


ROW_TILE = 512


def _rows(w=D_MODEL, cb=0, tr=ROW_TILE):
    return pl.BlockSpec((tr, w), lambda i: (i, cb))


def _full(shape):
    return pl.BlockSpec(shape, lambda i: (0,) * len(shape))


def _row_call(body, name, T, in_specs, out_specs, out_shape, args, tr=ROW_TILE):
    return pl.pallas_call(body, name=name, grid=(T // tr,), in_specs=in_specs, out_specs=out_specs,
                          out_shape=out_shape, compiler_params=_params(("arbitrary",)))(*args)


def _sds(shape, dtype):
    return jax.ShapeDtypeStruct(shape, dtype)


def _rmsnorm_fwd(x, g):
    T = x.shape[0]

    def body(x_ref, g_ref, h_ref):
        xv = x_ref[...]
        r = lax.rsqrt(jnp.mean(xv * xv, axis=1, keepdims=True) + RMS_EPS)
        h_ref[...] = (xv * r * g_ref[...]).astype(BF16)

    return _row_call(body, "rmsnorm_fwd", T, [_rows(), _full((1, D_MODEL))], _rows(),
                     _sds((T, D_MODEL), BF16), (x, g))


def _post_epilogue(y, a_tile, x, g):
    del a_tile
    return y, x + y * lax.rsqrt(jnp.mean(y * y, axis=1, keepdims=True) + RMS_EPS) * g


def _ple_epilogue(gl, x1, pe):
    return gl, x1 + pe * _sigmoid(gl)


def _dgrad_pre_bwd(a, b, x, g, dx1, add, tk, name, tm=512):
    M, K = a.shape
    N = b.shape[0]
    tk = min(tk, K)
    assert M % tm == 0 and K % tk == 0 and N == D_MODEL
    nk = K // tk

    def body(*refs):
        a_ref, b_ref, x_ref, g_ref, dx1_ref = refs[:5]
        add_ref = refs[5] if add is not None else None
        dx_ref, dg_ref = refs[-3:-1] if nk > 1 else refs[-2:]
        i, k = pl.program_id(0), pl.program_id(1)

        @pl.when((i == 0) & (k == 0))
        def _():
            dg_ref[...] = jnp.zeros_like(dg_ref)

        part = lax.dot_general(a_ref[...].astype(BF16), b_ref[...].astype(BF16), _NT, preferred_element_type=F32)

        def finish(dh):
            if add is not None:
                dh = dh + add_ref[...]
            xv = x_ref[...]
            r = lax.rsqrt(jnp.mean(xv * xv, axis=1, keepdims=True) + RMS_EPS)
            xh = xv * r
            dg_ref[...] += jnp.sum(dh * xh, axis=0, keepdims=True)
            dn = dh * g_ref[...]
            dx_ref[...] = dx1_ref[...] + r * (dn - xh * jnp.mean(dn * xh, axis=1, keepdims=True))

        if nk == 1:
            finish(part)
        else:
            acc_ref = refs[-1]

            @pl.when(k == 0)
            def _():
                acc_ref[...] = part

            @pl.when((k > 0) & (k < nk - 1))
            def _():
                acc_ref[...] += part

            @pl.when(k == nk - 1)
            def _():
                finish(acc_ref[...] + part)

    tile = pl.BlockSpec((tm, N), lambda i, k: (i, 0))
    row = pl.BlockSpec((1, N), lambda i, k: (0, 0))
    in_specs = [pl.BlockSpec((tm, tk), lambda i, k: (i, k)), pl.BlockSpec((N, tk), lambda i, k: (0, k)), tile, row, tile]
    args = [a, b, x, g, dx1]
    if add is not None:
        in_specs.append(tile)
        args.append(add)
    return pl.pallas_call(
        body, name=name, grid=(M // tm, nk), in_specs=in_specs, out_specs=[tile, row],
        out_shape=[_sds((M, N), F32), _sds((1, N), F32)],
        scratch_shapes=[pltpu.VMEM((tm, N), F32)] if nk > 1 else [],
        compiler_params=_params(("arbitrary", "arbitrary")),
    )(*args)


def _ple_post_bwd(dx2, pe, gl, w_gate, y, g_post):
    T = dx2.shape[0]

    def body(d_ref, pe_ref, gl_ref, w_ref, y_ref, g_ref, dpe_ref, dgl_ref, dx1_ref, dy_ref, dg_ref):
        @pl.when(pl.program_id(0) == 0)
        def _():
            dg_ref[...] = jnp.zeros_like(dg_ref)

        dv = d_ref[...]
        sg = _sigmoid(gl_ref[...])
        dpe_ref[...] = (dv * sg).astype(BF16)
        dgl = (dv * pe_ref[...] * sg * (1.0 - sg)).astype(BF16)
        dgl_ref[...] = dgl
        dx1 = dv + lax.dot_general(dgl, w_ref[...], _NT, preferred_element_type=F32)
        dx1_ref[...] = dx1
        yv = y_ref[...]
        r = lax.rsqrt(jnp.mean(yv * yv, axis=1, keepdims=True) + RMS_EPS)
        yh = yv * r
        dg_ref[...] += jnp.sum(dx1 * yh, axis=0, keepdims=True)
        dn = dx1 * g_ref[...]
        dy_ref[...] = (r * (dn - yh * jnp.mean(dn * yh, axis=1, keepdims=True))).astype(BF16)

    row = _full((1, D_MODEL))
    return _row_call(body, "ple_post_bwd", T, [_rows()] * 3 + [_full((D_MODEL, D_MODEL)), _rows(), row],
                     [_rows()] * 4 + [row],
                     [_sds((T, D_MODEL), BF16)] * 2 + [_sds((T, D_MODEL), F32), _sds((T, D_MODEL), BF16),
                                                      _sds((1, D_MODEL), F32)],
                     (dx2, pe, gl, w_gate, y, g_post))


def _loss_fwd_bwd(y, target):
    T = y.shape[0]

    def body(y_ref, t_ref, s_ref, d_ref):
        @pl.when(pl.program_id(0) == 0)
        def _():
            s_ref[...] = jnp.zeros_like(s_ref)

        e = y_ref[...] - t_ref[...]
        s_ref[...] += jnp.sum(e * e).reshape(1, 1)
        d_ref[...] = e * (1.0 / D_MODEL)

    return _row_call(body, "loss", T, [_rows()] * 2, [_full((1, 1)), _rows()],
                     [_sds((1, 1), F32), _sds((T, D_MODEL), F32)], (y, target))


def _attn_combine(outs, lses, proj):
    T = proj.shape[0]

    def body(o0, o1, o2, l0, l1, l2, z_ref, a_ref, o_ref, lse_ref):
        a0, a1, a2 = l0[...], l1[...], l2[...]
        m = jnp.maximum(jnp.maximum(a0, a1), a2)
        e0, e1, e2 = jnp.exp(a0 - m), jnp.exp(a1 - m), jnp.exp(a2 - m)
        ssum = e0 + e1 + e2
        o = (e0 * o0[...] + e1 * o1[...] + e2 * o2[...]) / ssum
        zv = z_ref[...].astype(F32)
        o_ref[...] = o
        lse_ref[...] = m + jnp.log(ssum)
        a_ref[...] = (o * zv * _sigmoid(zv)).astype(BF16)

    return _row_call(body, "attn_combine", T, [_rows()] * 6 + [_rows(cb=3 * N_GROUPS)], [_rows()] * 3,
                     [_sds((T, D_MODEL), BF16), _sds((T, D_MODEL), F32), _sds((T, D_MODEL), F32)],
                     (*outs, *lses, proj))


def _gate_bwd(da, o, proj):
    T = da.shape[0]

    def body(da_ref, o_ref, z_ref, do_ref, dz_ref):
        dv = da_ref[...]
        zv = z_ref[...]
        sg = _sigmoid(zv)
        do_ref[...] = dv * zv * sg
        dz_ref[...] = dv * o_ref[...] * sg * (1.0 + zv * (1.0 - sg))

    zcols = _rows(cb=3 * N_GROUPS)
    return _row_call(body, "gate_bwd", T, [_rows(), _rows(), zcols], [_rows(), zcols],
                     [_sds((T, D_MODEL), F32), _sds((T, PROJ_COLS), F32)], (da, o, proj))


def _rope_tables(positions):
    inv_freq = 1.0 / (ROPE_THETA ** (jnp.arange(0, HEAD_DIM, 2, dtype=F32) / HEAD_DIM))
    ang = positions.astype(F32)[..., None] * inv_freq
    cos, sin = jnp.cos(ang), jnp.sin(ang)
    return jnp.tile(cos, (1, 1, 4)), jnp.concatenate([-sin, sin, -sin, sin], axis=-1)


def _rotate_half_partner(t):
    lane = lax.broadcasted_iota(jnp.int32, t.shape, 1)
    return jnp.where((lane % HEAD_DIM) < HEAD_DIM // 2,
                     pltpu.roll(t, LANES - HEAD_DIM // 2, 1), pltpu.roll(t, HEAD_DIM // 2, 1))


def _mask_bias(first):
    qi = lax.broadcasted_iota(jnp.int32, (N_BACK, 2 * N_BACK), 0)
    kj = lax.broadcasted_iota(jnp.int32, (N_BACK, 2 * N_BACK), 1)
    ok = (kj >= qi) & (kj <= qi + N_BACK)
    if first:
        ok = ok & (kj >= N_BACK)
    return jnp.where(ok, 0.0, NEG_INF).astype(F32)


def _stack_heads(t, head0):
    zero = jnp.zeros_like(t)
    return jnp.concatenate([jnp.where(head0, t, zero), jnp.where(head0, zero, t)], axis=0)


def _unstack_heads(t2, head0):
    return jnp.where(head0, t2[:N_BACK], t2[N_BACK:])


def _block_loop(nb, block):
    first, rest = _mask_bias(True), _mask_bias(False)
    first, rest = jnp.concatenate([first, first], axis=0), jnp.concatenate([rest, rest], axis=0)
    if nb <= BLOCK_UNROLL:
        for n in range(nb):
            block(n, first if n == 0 else rest)
        return

    def step(n, carry):
        block(n, jnp.where(n == 0, first, rest))
        return carry

    lax.fori_loop(0, nb, step, 0, unroll=BLOCK_UNROLL)


def _for(count, body, unroll_fully):
    if unroll_fully:
        for i in range(count):
            body(i)
    else:
        lax.fori_loop(0, count, lambda i, carry: (body(i), carry)[1], 0)


def _residues_together(nb):
    return min(RESIDUES_TOGETHER, max(1, BLOCK_UNROLL // nb))


def _residue_loop(d, nb, residue):
    together = _residues_together(nb)
    assert d % together == 0

    def group(i, carry):
        for u in range(together):
            residue(i * together + u, u)
        return carry

    lax.fori_loop(0, d // together, group, 0)


_NT = (((1,), (1,)), ((), ()))
_TN = (((0,), (0,)), ((), ()))


def _residue_rows(r, i, d):
    start = r + i * (N_BACK * d)
    if d == 1:
        return pl.ds(pl.multiple_of(start, N_BACK), N_BACK)
    return pl.ds(start, N_BACK, stride=d)


def _seq_rows(i):
    return pl.ds(pl.multiple_of(i * N_BACK, N_BACK), N_BACK)


def _rows_at(base, i, size=N_BACK):
    return pl.ds(pl.multiple_of(base + i * N_BACK, N_BACK), size)


def _attn_fwd(proj, cos, sin, group, Bl, S):
    d = WIN_DIL[group][1]
    L = S // d
    nb = L // N_BACK
    P = L + N_BACK
    assert WIN_DIL[group][0] // d == N_BACK and L % N_BACK == 0

    def body(q_ref, k_ref, v_ref, cos_ref, sin_ref, o_ref, lse_ref, qr, kr, vp):
        head0 = lax.broadcasted_iota(jnp.int32, (1, LANES), 1) < HEAD_DIM
        zeros = jnp.zeros((N_BACK, LANES), BF16)

        def residue(r, u):
            del u
            qbase, kbase = r * L, r * P
            kr[_rows_at(kbase, 0), :] = zeros
            vp[_rows_at(kbase, 0), :] = zeros

            def rope(i):
                rows = _residue_rows(r, i, d)
                cs, sn = cos_ref[rows, :], sin_ref[rows, :]
                q, k = q_ref[rows, :], k_ref[rows, :]
                qr[_rows_at(qbase, i), :] = ((q * cs + _rotate_half_partner(q) * sn)
                                            * (HEAD_DIM ** -0.5)).astype(BF16)
                kr[_rows_at(kbase, i + 1), :] = (k * cs + _rotate_half_partner(k) * sn).astype(BF16)
                vp[_rows_at(kbase, i + 1), :] = v_ref[rows, :].astype(BF16)

            _for(nb, rope, nb <= BLOCK_UNROLL)

            def block(n, bias):
                win = _rows_at(kbase, n, 2 * N_BACK)
                q2, kw, vw = _stack_heads(qr[_rows_at(qbase, n), :], head0), kr[win, :], vp[win, :]
                s = lax.dot_general(q2, kw, _NT, preferred_element_type=F32) + bias
                m = jnp.max(s, axis=1, keepdims=True)
                p = jnp.exp(s - m)
                l = jnp.sum(p, axis=1, keepdims=True)
                pv = jnp.dot(p.astype(BF16), vw, preferred_element_type=F32)
                rows = _residue_rows(r, n, d)
                o_ref[rows, :] = _unstack_heads(pv * (1.0 / l), head0)
                lse_ref[rows, :] = _unstack_heads((m + jnp.log(l)) + jnp.zeros((2 * N_BACK, LANES), F32), head0)

            _block_loop(nb, block)

        _residue_loop(d, nb, residue)

    act = pl.BlockSpec((None, S, LANES), lambda b, hp: (b, 0, hp))
    tab = pl.BlockSpec((None, S, LANES), lambda b, hp: (b, 0, 0))
    col = lambda which: pl.BlockSpec((None, S, LANES),
                                     lambda b, hp: (b, 0, (which * N_GROUPS + group) * HEAD_PAIRS + hp))
    seq = lambda rows: pl.BlockSpec((None, None, rows, LANES), lambda b, hp: (b, hp, 0, 0))
    p3 = proj.reshape(Bl, S, PROJ_COLS)
    o, lse, qr, kr, vp = pl.pallas_call(
        body, name="attn_fwd_g%d" % group, grid=(Bl, HEAD_PAIRS),
        in_specs=[col(0), col(1), col(2), tab, tab], out_specs=[act, act, seq(S), seq(d * P), seq(d * P)],
        out_shape=[_sds((Bl, S, D_MODEL), F32)] * 2 + [_sds((Bl, HEAD_PAIRS, S, LANES), BF16)]
        + [_sds((Bl, HEAD_PAIRS, d * P, LANES), BF16)] * 2,
        compiler_params=_params(("parallel", "arbitrary"), VMEM_LIMIT_ATTN),
    )(p3, p3, p3, cos, sin)
    return o.reshape(Bl * S, D_MODEL), lse.reshape(Bl * S, D_MODEL), (qr, kr, vp)


def _attn_bwd(saved, cos, sin, do, o, lse, dproj, group, Bl, S):
    d = WIN_DIL[group][1]
    L = S // d
    nb = L // N_BACK
    P = L + N_BACK
    steps = Bl * HEAD_PAIRS

    def body(qr, kr, vp, cos_ref, sin_ref, do_ref, o_ref, lse_ref, dproj_in, dproj_ref,
             dk_accs, dv_accs, stage, sems):
        del dproj_in
        head0 = lax.broadcasted_iota(jnp.int32, (1, LANES), 1) < HEAD_DIM
        b, hp = pl.program_id(0), pl.program_id(1)
        step = b * HEAD_PAIRS + hp
        slot = step % 2
        dq_s, dk_s, dv_s = stage.at[slot, 0], stage.at[slot, 1], stage.at[slot, 2]

        def copies(which_slot):
            out = []
            for which in range(3):
                col = ((which * N_GROUPS + group) * HEAD_PAIRS + hp) * LANES
                out.append(pltpu.make_async_copy(
                    stage.at[which_slot, which], dproj_ref.at[b, :, pl.ds(pl.multiple_of(col, LANES), LANES)],
                    sems.at[which_slot, which]))
            return out

        @pl.when(step >= 2)
        def _():
            for cp in copies(slot):
                cp.wait()

        def residue(r, u):
            qbase, kbase = r * L, r * P
            dk_acc, dv_acc = dk_accs.at[u], dv_accs.at[u]
            dk_acc[...] = jnp.zeros_like(dk_acc)
            dv_acc[...] = jnp.zeros_like(dv_acc)

            def block(n, bias):
                win = pl.ds(pl.multiple_of(n * N_BACK, N_BACK), 2 * N_BACK)
                kwin = _rows_at(kbase, n, 2 * N_BACK)
                rows = _residue_rows(r, n, d)
                q2, kw, vw = _stack_heads(qr[_rows_at(qbase, n), :], head0), kr[kwin, :], vp[kwin, :]
                dof = do_ref[rows, :]
                do2 = _stack_heads(dof.astype(BF16), head0)
                lse_b = lse_ref[rows, :]
                lse2 = jnp.concatenate([lse_b[:, 0:1], lse_b[:, HEAD_DIM:HEAD_DIM + 1]], axis=0)
                dsum = _stack_heads(dof * o_ref[rows, :], head0)
                delta = jnp.sum(dsum, axis=1, keepdims=True)
                s = lax.dot_general(q2, kw, _NT, preferred_element_type=F32) + bias
                p = jnp.exp(s - lse2)
                dp = lax.dot_general(do2, vw, _NT, preferred_element_type=F32)
                ds = (p * (dp - delta)).astype(BF16)
                dq = _unstack_heads(jnp.dot(ds, kw, preferred_element_type=F32), head0) * (HEAD_DIM ** -0.5)
                cs, sn = cos_ref[rows, :], sin_ref[rows, :]
                dq_s[rows, :] = dq * cs + _rotate_half_partner(dq * sn)
                dk_acc[win, :] += lax.dot_general(ds, q2, _TN, preferred_element_type=F32)
                dv_acc[win, :] += lax.dot_general(p.astype(BF16), do2, _TN, preferred_element_type=F32)

            _block_loop(nb, block)

            def finish(i):
                rows = _residue_rows(r, i, d)
                cs, sn = cos_ref[rows, :], sin_ref[rows, :]
                dk = dk_acc[_seq_rows(i + 1), :]
                dk_s[rows, :] = dk * cs + _rotate_half_partner(dk * sn)
                dv_s[rows, :] = dv_acc[_seq_rows(i + 1), :]

            _for(nb, finish, nb <= BLOCK_UNROLL)

        _residue_loop(d, nb, residue)
        for cp in copies(slot):
            cp.start()

        @pl.when(step == steps - 1)
        def _():
            if steps > 1:
                for cp in copies(1 - slot):
                    cp.wait()
            for cp in copies(slot):
                cp.wait()

    act = pl.BlockSpec((None, S, LANES), lambda b, hp: (b, 0, hp))
    tab = pl.BlockSpec((None, S, LANES), lambda b, hp: (b, 0, 0))
    seq = lambda rows: pl.BlockSpec((None, None, rows, LANES), lambda b, hp: (b, hp, 0, 0))
    view = lambda t: t.reshape(Bl, S, D_MODEL)
    out = pl.pallas_call(
        body, name="attn_bwd_g%d" % group, grid=(Bl, HEAD_PAIRS),
        in_specs=[seq(S), seq(d * P), seq(d * P), tab, tab, act, act, act, _ANY], out_specs=_ANY,
        out_shape=_sds((Bl, S, PROJ_COLS), F32), input_output_aliases={8: 0},
        scratch_shapes=[pltpu.VMEM((_residues_together(nb), P, LANES), F32),
                        pltpu.VMEM((_residues_together(nb), P, LANES), F32),
                        pltpu.VMEM((2, 3, S, LANES), F32), pltpu.SemaphoreType.DMA((2, 3))],
        compiler_params=_params(("arbitrary", "arbitrary"), VMEM_LIMIT_ATTN),
    )(*saved, cos, sin, view(do), view(o), view(lse), dproj.reshape(Bl, S, PROJ_COLS))
    return out.reshape(Bl * S, PROJ_COLS)


CONV_TILE = 256
CONV_CHUNK = 64
SUBLANES = 8
CONV_SHIFT_ROWS = CONV_TILE + CONV_HALO - SUBLANES


def _fill_shifted(shifted, ext, cs):
    for k in range(1, SUBLANES):
        shifted[k - 1] = ext[pl.ds(k, CONV_SHIFT_ROWS), cs]


def _shifted_rows(shifted, ext, cs, off):
    k = off % SUBLANES
    if k == 0:
        return ext[pl.ds(off, CONV_CHUNK), cs]
    return shifted[k - 1, pl.ds(off - k, CONV_CHUNK), :]


def _conv_fwd(proj, z, dw, dwb, ln_g, ln_b, Bl, S):
    tr = CONV_TILE
    nj = S // tr
    hb = tr // CONV_HALO

    def body(a_ref, b_ref, ah_ref, bh_ref, z_ref, dw_ref, dwb_ref, g_ref, bb_ref, u1_ref, out_ref, ext, shifted):
        j = pl.program_id(1)
        halo = ah_ref[0].astype(F32) * _sigmoid(bh_ref[0].astype(F32))
        ext[pl.ds(0, CONV_HALO), :] = jnp.where(j > 0, halo, 0.0)
        ext[pl.ds(CONV_HALO, tr), :] = a_ref[0].astype(F32) * _sigmoid(b_ref[0].astype(F32))

        def cols(c, carry):
            cs = pl.ds(pl.multiple_of(c * LANES, LANES), LANES)
            _fill_shifted(shifted, ext, cs)
            for rc in range(tr // CONV_CHUNK):
                acc = jnp.zeros((CONV_CHUNK, LANES), F32)
                for w in range(CONV_WIDTH):
                    off = rc * CONV_CHUNK + CONV_HALO - (CONV_WIDTH - 1) + w
                    acc = acc + dw_ref[pl.ds(w, 1), cs] * _shifted_rows(shifted, ext, cs, off)
                u1_ref[0, pl.ds(rc * CONV_CHUNK, CONV_CHUNK), cs] = acc + dwb_ref[:, cs]
            return carry

        lax.fori_loop(0, D_MODEL // LANES, cols, 0)
        u1 = u1_ref[0]
        mu = jnp.mean(u1, axis=1, keepdims=True)
        xc = u1 - mu
        rstd = lax.rsqrt(jnp.mean(xc * xc, axis=1, keepdims=True) + LN_EPS)
        u2 = xc * rstd * g_ref[...] + bb_ref[...]
        zv = z_ref[0].astype(F32)
        out_ref[0] = (u2 * _sigmoid(u2) * zv * _sigmoid(zv)).astype(BF16)

    tile = lambda cb: pl.BlockSpec((1, tr, D_MODEL), lambda b, j: (b, j, cb))
    halo = lambda cb: pl.BlockSpec((1, CONV_HALO, D_MODEL), lambda b, j: (b, jnp.maximum(j * hb - 1, 0), cb))
    par = lambda r: pl.BlockSpec((r, D_MODEL), lambda b, j: (0, 0))
    p3 = proj.reshape(Bl, S, 2 * D_MODEL)
    u1, out = pl.pallas_call(
        body, name="conv_fwd", grid=(Bl, nj),
        in_specs=[tile(0), tile(1), halo(0), halo(1), tile(0), par(32), par(1), par(1), par(1)],
        out_specs=[tile(0), tile(0)],
        out_shape=[_sds((Bl, S, D_MODEL), F32), _sds((Bl, S, D_MODEL), BF16)],
        scratch_shapes=[pltpu.VMEM((tr + CONV_HALO, D_MODEL), F32),
                        pltpu.VMEM((SUBLANES - 1, CONV_SHIFT_ROWS, LANES), F32)],
        compiler_params=_params(("parallel", "arbitrary")),
    )(p3, p3, p3, p3, z.reshape(Bl, S, D_MODEL), dw, dwb, ln_g, ln_b)
    return u1.reshape(Bl * S, D_MODEL), out.reshape(Bl * S, D_MODEL)


def _conv_norm_bwd(da2, z, u1, ln_g, ln_b):
    T = da2.shape[0]

    def body(da_ref, z_ref, u_ref, g_ref, b_ref, du_ref, dz_ref, dg_ref, db_ref):
        @pl.when(pl.program_id(0) == 0)
        def _():
            dg_ref[...] = jnp.zeros_like(dg_ref)
            db_ref[...] = jnp.zeros_like(db_ref)

        u1 = u_ref[...]
        mu = jnp.mean(u1, axis=1, keepdims=True)
        xc = u1 - mu
        rstd = lax.rsqrt(jnp.mean(xc * xc, axis=1, keepdims=True) + LN_EPS)
        nrm = xc * rstd
        u2 = nrm * g_ref[...] + b_ref[...]
        s2 = _sigmoid(u2)
        zv = z_ref[...].astype(F32)
        sz = _sigmoid(zv)
        dv = da_ref[...]
        dz_ref[...] = (dv * u2 * s2 * sz * (1.0 + zv * (1.0 - sz))).astype(BF16)
        du2 = dv * zv * sz * s2 * (1.0 + u2 * (1.0 - s2))
        dg_ref[...] += jnp.sum(du2 * nrm, axis=0, keepdims=True)
        db_ref[...] += jnp.sum(du2, axis=0, keepdims=True)
        dn = du2 * g_ref[...]
        du_ref[...] = rstd * (dn - jnp.mean(dn, axis=1, keepdims=True)
                              - nrm * jnp.mean(dn * nrm, axis=1, keepdims=True))

    return _row_call(body, "conv_norm_bwd", T,
                     [_rows(), _rows(), _rows(), _full((1, D_MODEL)), _full((1, D_MODEL))],
                     [_rows(), _rows(), _full((1, D_MODEL)), _full((1, D_MODEL))],
                     [_sds((T, D_MODEL), F32), _sds((T, D_MODEL), BF16), _sds((1, D_MODEL), F32),
                      _sds((1, D_MODEL), F32)], (da2, z, u1, ln_g, ln_b))


def _conv_bwd(proj, du1, dw, Bl, S):
    tr = CONV_TILE
    nj = S // tr
    hb = tr // CONV_HALO

    def body(a_ref, b_ref, ah_ref, bh_ref, du_ref, duh_ref, dw_ref, dab_ref, ddw_ref, ddb_ref, uext, dext, du0,
             ushift, dshift, ddw8):
        first = (pl.program_id(0) == 0) & (pl.program_id(1) == 0)
        last = (pl.program_id(0) == Bl - 1) & (pl.program_id(1) == nj - 1)
        j = pl.program_id(1)

        @pl.when(first)
        def _():
            ddw8[...] = jnp.zeros_like(ddw8)
            ddb_ref[...] = jnp.zeros_like(ddb_ref)

        halo = ah_ref[0].astype(F32) * _sigmoid(bh_ref[0].astype(F32))
        uext[pl.ds(0, CONV_HALO), :] = jnp.where(j > 0, halo, 0.0)
        av = a_ref[0].astype(F32)
        sb = _sigmoid(b_ref[0].astype(F32))
        uext[pl.ds(CONV_HALO, tr), :] = av * sb
        dext[pl.ds(0, tr), :] = du_ref[0]
        dext[pl.ds(tr, CONV_HALO), :] = jnp.where(j < nj - 1, duh_ref[0], 0.0)
        ddb_ref[...] += jnp.sum(du_ref[0], axis=0, keepdims=True)

        def cols(c, carry):
            cs = pl.ds(pl.multiple_of(c * LANES, LANES), LANES)
            _fill_shifted(dshift, dext, cs)
            _fill_shifted(ushift, uext, cs)
            for rc in range(tr // CONV_CHUNK):
                base = rc * CONV_CHUNK
                acc = jnp.zeros((CONV_CHUNK, LANES), F32)
                for w in range(CONV_WIDTH):
                    acc = acc + dw_ref[pl.ds(w, 1), cs] * _shifted_rows(dshift, dext, cs, base + CONV_WIDTH - 1 - w)
                du0[pl.ds(base, CONV_CHUNK), cs] = acc
            for w in range(CONV_WIDTH):
                part = jnp.zeros((SUBLANES, LANES), F32)
                for rc in range(tr // CONV_CHUNK):
                    base = rc * CONV_CHUNK
                    prod = dext[pl.ds(base, CONV_CHUNK), cs] * _shifted_rows(
                        ushift, uext, cs, base + CONV_HALO - (CONV_WIDTH - 1) + w)
                    for i in range(CONV_CHUNK // SUBLANES):
                        part = part + prod[i * SUBLANES:(i + 1) * SUBLANES]
                ddw8[pl.ds(w * SUBLANES, SUBLANES), cs] += part
            return carry

        lax.fori_loop(0, D_MODEL // LANES, cols, 0)
        g = du0[...]
        dab_ref[0, :, 0:D_MODEL] = (g * sb).astype(BF16)
        dab_ref[0, :, D_MODEL:2 * D_MODEL] = (g * av * sb * (1.0 - sb)).astype(BF16)

        @pl.when(last)
        def _():
            for w in range(CONV_WIDTH + 1):
                ddw_ref[pl.ds(w, 1), :] = jnp.sum(ddw8[pl.ds(w * SUBLANES, SUBLANES), :], axis=0, keepdims=True)

    tile = lambda cb: pl.BlockSpec((1, tr, D_MODEL), lambda b, j: (b, j, cb))
    halo = lambda cb: pl.BlockSpec((1, CONV_HALO, D_MODEL), lambda b, j: (b, jnp.maximum(j * hb - 1, 0), cb))
    nxt = pl.BlockSpec((1, CONV_HALO, D_MODEL), lambda b, j: (b, jnp.minimum((j + 1) * hb, S // CONV_HALO - 1), 0))
    par = lambda r: pl.BlockSpec((r, D_MODEL), lambda b, j: (0, 0))
    p3 = proj.reshape(Bl, S, 2 * D_MODEL)
    d3 = du1.reshape(Bl, S, D_MODEL)
    dab, ddw, ddb = pl.pallas_call(
        body, name="conv_bwd", grid=(Bl, nj),
        in_specs=[tile(0), tile(1), halo(0), halo(1), tile(0), nxt, par(32)],
        out_specs=[pl.BlockSpec((1, tr, 2 * D_MODEL), lambda b, j: (b, j, 0)), par(32), par(1)],
        out_shape=[_sds((Bl, S, 2 * D_MODEL), BF16), _sds((32, D_MODEL), F32), _sds((1, D_MODEL), F32)],
        scratch_shapes=[pltpu.VMEM((tr + CONV_HALO, D_MODEL), F32), pltpu.VMEM((tr + CONV_HALO, D_MODEL), F32),
                        pltpu.VMEM((tr, D_MODEL), F32),
                        pltpu.VMEM((SUBLANES - 1, CONV_SHIFT_ROWS, LANES), F32),
                        pltpu.VMEM((SUBLANES - 1, CONV_SHIFT_ROWS, LANES), F32),
                        pltpu.VMEM(((CONV_WIDTH + 1) * SUBLANES, D_MODEL), F32)],
        compiler_params=_params(("arbitrary", "arbitrary")),
    )(p3, p3, p3, p3, d3, d3, dw)
    return dab.reshape(Bl * S, 2 * D_MODEL), ddw, ddb


_LAYOUT = (
    ("pre_norm_g", (4, 1024), None), ("post_norm_g", (4, 1024), None),
    ("attn_w_in", (2, 1024, 2560), 2), ("attn_w_out", (2, 256, 1024), 1),
    ("conv_w_in", (2, 1024, 768), 2), ("conv_dw_w", (2, 31, 256), 2),
    ("conv_dw_b", (2, 256), 1), ("conv_ln_g", (2, 256), 1), ("conv_ln_b", (2, 256), 1),
    ("conv_w_out", (2, 256, 1024), 1), ("ple_w_proj", (4, 256, 256), 2), ("ple_w_gate", (4, 256, 1024), 1),
)
_MATMUL_WEIGHTS = ("attn_w_in", "attn_w_out", "conv_w_in", "conv_w_out", "ple_w_proj", "ple_w_gate")
_FIRST_LAYER = ("attn_w_in", "attn_w_out", "ple_w_proj", "ple_w_gate")
_AXIS = {n: a for n, _, a in _LAYOUT}


def _size(shape):
    n = 1
    for s in shape:
        n *= s
    return n


def _padded_rows(shape):
    rows = _size(shape) // shape[-1]
    return rows + (-rows) % FLAT_ROW_ALIGN


def _rows2d(a):
    a2 = a.reshape(-1, a.shape[-1])
    pad = _padded_rows(a.shape) - a2.shape[0]
    return jnp.pad(a2, ((0, pad), (0, 0))) if pad else a2


def _col_blocks(a):
    a2 = _rows2d(a)
    return jnp.concatenate([a2[:, c:c + FLAT_COLS] for c in range(0, a2.shape[1], FLAT_COLS)], axis=0)


def _from_col_blocks(flat, off, shape):
    rows, nblk = _padded_rows(shape), shape[-1] // FLAT_COLS
    a2 = jnp.concatenate([flat[off + b * rows:off + (b + 1) * rows] for b in range(nblk)], axis=1)
    return a2[:_size(shape) // shape[-1]].reshape(shape), off + nblk * rows


def _shard_col_blocks(full, shape, axis):
    if axis is None:
        blocks = _col_blocks(full)
        return jnp.broadcast_to(blocks[None], (N_CHIPS,) + blocks.shape)
    m = shape[-1]
    if axis == len(shape) - 1:
        a2 = _rows2d(full)
        pieces = [a2[:, c:c + FLAT_COLS] for c in range(0, N_CHIPS * m, FLAT_COLS)]
    else:
        layers, r, _ = shape
        assert axis == 1 and (layers * r) % FLAT_ROW_ALIGN == 0
        pieces = [full[:, s * r:(s + 1) * r, c:c + FLAT_COLS].reshape(layers * r, FLAT_COLS)
                  for s in range(N_CHIPS) for c in range(0, m, FLAT_COLS)]
    return jnp.concatenate(pieces, axis=0).reshape(N_CHIPS, -1, FLAT_COLS)


_FLAT_BIG = ("attn_w_in", "conv_w_in", "ple_w_gate", "attn_w_out", "conv_w_out", "ple_w_proj")
_FLAT_SMALL = ("pre_norm_g", "post_norm_g", "conv_dw_w", "conv_dw_b", "conv_ln_g", "conv_ln_b")
PACK_TILE = 1024


_FIRST_LAYER_PARAMS = _FIRST_LAYER + ("pre_norm_g", "post_norm_g")


def _part_shapes(first):
    out = {}
    for n, shape, _ in _LAYOUT:
        layers = (1 if first else shape[0] - 1) if n in _FIRST_LAYER_PARAMS else (0 if first else shape[0])
        if layers and not (first and n == "pre_norm_g"):
            out[n] = (layers,) + shape[1:]
    return out


def _flat_plan(shapes):
    out, off = {}, 0
    for n in _FLAT_BIG + _FLAT_SMALL:
        if n in shapes:
            out[n] = off
            off += _padded_rows(shapes[n]) * (shapes[n][-1] // FLAT_COLS)
    return out, off + (-off) % (2 * FLAT_TILE)


def _unpack_f32(flat, shapes):
    offsets, _ = _flat_plan(shapes)
    return {n: _from_col_blocks(flat, offsets[n], shapes[n])[0] for n in shapes}


def _pack_param(full, shape, axis, off, flat, total_rows):
    layers, r, m = shape
    nblk = m // FLAT_COLS
    if axis == 2:
        rows = layers * r
        tr = math.gcd(math.gcd(rows, PACK_TILE), off) if off else math.gcd(rows, PACK_TILE)
        assert rows % tr == 0 and off % tr == 0 and tr % FLAT_ROW_ALIGN == 0
        src = full.reshape(rows, N_CHIPS * m)
        grid = (N_CHIPS * nblk, rows // tr)
        in_spec = pl.BlockSpec((tr, FLAT_COLS), lambda j, i: (i, j))
        out_spec = pl.BlockSpec((None, tr, FLAT_COLS), lambda j, i: (j // nblk, (off + (j % nblk) * rows) // tr + i, 0))
    else:
        assert axis == 1 and off % r == 0
        src = full.reshape(layers * N_CHIPS * r, m)
        grid = (layers, N_CHIPS, nblk)
        in_spec = pl.BlockSpec((r, FLAT_COLS), lambda l, s, b: (l * N_CHIPS + s, b))
        out_spec = pl.BlockSpec((None, r, FLAT_COLS), lambda l, s, b: (s, (off + b * layers * r) // r + l, 0))

    def copy_body(src_ref, *rest):
        rest[-1][...] = src_ref[...]

    args, in_specs, aliases = [src], [in_spec], {}
    if flat is not None:
        args.append(flat)
        in_specs.append(_ANY)
        aliases = {1: 0}
    return pl.pallas_call(
        copy_body, name="pack_grad", grid=grid, in_specs=in_specs, out_specs=out_spec,
        out_shape=_sds((N_CHIPS, total_rows, FLAT_COLS), F32), input_output_aliases=aliases,
        compiler_params=_params(("arbitrary",) * len(grid)))(*args)


SMALL_ROWS = 40


def _stack_small(w):
    rows = [w["conv_dw_w"]] + [w[n][:, None, :] for n in ("conv_dw_b", "conv_ln_g", "conv_ln_b")]
    stacked = jnp.concatenate(rows, axis=1)
    return jnp.pad(stacked, ((0, 0), (0, SMALL_ROWS - stacked.shape[1]), (0, 0)))


def _unstack_small(small):
    return {"conv_dw_w": small[:, :CONV_WIDTH], "conv_dw_b": small[:, CONV_WIDTH],
            "conv_ln_g": small[:, CONV_WIDTH + 1], "conv_ln_b": small[:, CONV_WIDTH + 2]}


def _pack_full_grads(grads, shapes):
    offsets, total_rows = _flat_plan(shapes)
    flat = None
    for n in _FLAT_BIG:
        if n in shapes:
            flat = _pack_param(grads[n], shapes[n], _AXIS[n], offsets[n], flat, total_rows)
    small_names = [n for n in _FLAT_SMALL if n in shapes]
    small = jnp.concatenate([_shard_col_blocks(grads[n], shapes[n], _AXIS[n]) for n in small_names], axis=1)
    start = offsets[small_names[0]]
    small = jnp.pad(small, ((0, 0), (0, total_rows - start - small.shape[1]), (0, 0)))
    return lax.dynamic_update_slice(flat, small, (0, start, 0))


_ANY = pl.BlockSpec(memory_space=pl.ANY)


def _mesh_pos():
    return lax.axis_index("x"), lax.axis_index("y"), lax.axis_index("c")


def _other_chips(x, y):
    return [(1 - x, y), (x, 1 - y), (1 - x, 1 - y)]


COPIES_PER_ARRAY = 7


def _allgather_weights(shards, axes, small):
    n = len(shards)
    full_shape = lambda a, axis: tuple(d * (N_CHIPS if i == axis else 1) for i, d in enumerate(a.shape))

    def body(*refs):
        ins, small_in = refs[:n], refs[n]
        outs, small_out = refs[n + 1:2 * n + 1], refs[2 * n + 1]
        send_sems, recv_sems = refs[2 * n + 2:]
        x, y, c = _mesh_pos()
        mine, me, sibling = 2 * x + y, (x, y, c), (x, y, 1 - c)
        chips = _other_chips(x, y)

        def region(a, chip, half):
            _, rows, cols = shards[a].shape
            h = rows // 2
            if axes[a] == 2:
                return outs[a].at[:, slice(None) if half is None else pl.ds(half * h, h), pl.ds(chip * cols, cols)]
            if half is None:
                return outs[a].at[:, pl.ds(chip * rows, rows), :]
            return outs[a].at[:, pl.ds(chip * rows + half * h, h), :]

        def copy(k, src, dst, to):
            return pltpu.make_async_remote_copy(src_ref=src, dst_ref=dst, send_sem=send_sems.at[k],
                                                recv_sem=recv_sems.at[k], device_id=to, device_id_type=MESH)

        def arrival(k, dst):
            return copy(k, dst, dst, me)

        sends = []
        for a in range(n):
            h = shards[a].shape[1] // 2
            base = a * COPIES_PER_ARRAY
            sends.append(copy(base + 6, ins[a], region(a, mine, None), sibling))
            for j, (cx, cy) in enumerate(chips):
                sends.append(copy(base + j, ins[a].at[:, pl.ds(c * h, h), :], region(a, mine, c), (cx, cy, c)))
        small_cols = small.shape[2]
        small_region = lambda chip: small_out.at[:, :, pl.ds(chip * small_cols, small_cols)]
        base = n * COPIES_PER_ARRAY
        sends.append(copy(base + 3, small_in, small_region(mine), sibling))
        for j, (cx, cy) in enumerate(chips):
            sends.append(copy(base + j, small_in, small_region(mine), (cx, cy, c)))
        for cp in sends:
            cp.start()
        for j, (cx, cy) in enumerate(chips):
            for a in range(n):
                k = a * COPIES_PER_ARRAY + j
                arrival(k, region(a, 2 * cx + cy, c)).wait_recv()
                passed = copy(k + 3, region(a, 2 * cx + cy, c), region(a, 2 * cx + cy, c), sibling)
                passed.start()
                sends.append(passed)
        for j, (cx, cy) in enumerate(chips):
            for a in range(n):
                arrival(a * COPIES_PER_ARRAY + 3 + j, region(a, 2 * cx + cy, 1 - c)).wait_recv()
            arrival(base + j, small_region(2 * cx + cy)).wait_recv()
        for a in range(n):
            arrival(a * COPIES_PER_ARRAY + 6, region(a, mine, None)).wait_recv()
        arrival(base + 3, small_region(mine)).wait_recv()
        for cp in sends:
            cp.wait_send()

    n_sems = n * COPIES_PER_ARRAY + 4
    out = pl.pallas_call(
        body, name="allgather_weights", in_specs=[_ANY] * (n + 1), out_specs=[_ANY] * (n + 1),
        out_shape=[_sds(full_shape(a, axis), a.dtype) for a, axis in zip(shards, axes)]
        + [_sds(full_shape(small, 2), small.dtype)],
        scratch_shapes=[pltpu.SemaphoreType.DMA((n_sems,)), pltpu.SemaphoreType.DMA((n_sems,))],
    )(*shards, small)
    return out[:n], out[n]


_HBM = pl.BlockSpec(memory_space=pltpu.HBM)
_SEM = pl.BlockSpec(memory_space=pltpu.SEMAPHORE)


def _full_shape(shard, axis):
    return tuple(d * (N_CHIPS if i == axis else 1) for i, d in enumerate(shard.shape))


def _direct_gather_copies(srcs, lands, shapes, axes, send_sems, recv_sems):
    x, y, c = _mesh_pos()
    mine, me, sibling = 2 * x + y, (x, y, c), (x, y, 1 - c)

    def region(a, chip, half):
        _, rows, cols = shapes[a]
        h = rows // 2
        if axes[a] == 2:
            return lands[a].at[:, slice(None) if half is None else pl.ds(half * h, h), pl.ds(chip * cols, cols)]
        if half is None:
            return lands[a].at[:, pl.ds(chip * rows, rows), :]
        return lands[a].at[:, pl.ds(chip * rows + half * h, h), :]

    def copy(k, src, dst, to):
        return pltpu.make_async_remote_copy(src_ref=src, dst_ref=dst, send_sem=send_sems.at[k],
                                            recv_sem=recv_sems.at[k], device_id=to, device_id_type=MESH)

    outgoing, incoming = [], []
    for a in range(len(srcs)):
        h = shapes[a][1] // 2
        base = a * COPIES_PER_ARRAY
        for j, (cx, cy) in enumerate(_other_chips(x, y)):
            for t, ct in enumerate((c, 1 - c)):
                k = base + 2 * j + t
                outgoing.append(copy(k, srcs[a].at[:, pl.ds(c * h, h), :], region(a, mine, c), (cx, cy, ct)))
                landed = region(a, 2 * cx + cy, ct)
                incoming.append(copy(k, landed, landed, me))
        outgoing.append(copy(base + 6, srcs[a], region(a, mine, None), sibling))
        incoming.append(copy(base + 6, region(a, mine, None), region(a, mine, None), me))
    return outgoing, incoming


def _allgather_start(shards, axes):
    n = len(shards)
    shapes = [s.shape for s in shards]

    def body(*refs):
        srcs, lands = refs[:n], refs[n:2 * n]
        send_sems, recv_sems = refs[2 * n], refs[2 * n + 1]
        token = refs[-1]
        outgoing, _ = _direct_gather_copies(srcs, lands, shapes, axes, send_sems, recv_sems)
        for cp in outgoing:
            cp.start()
        token[...] = jnp.zeros_like(token)

    n_sems = n * COPIES_PER_ARRAY
    zones = [pltpu.with_memory_space_constraint(lax.empty(_full_shape(s, ax), s.dtype), pltpu.HBM)
             for s, ax in zip(shards, axes)]
    out = pl.pallas_call(
        body, name="allgather_rest_start",
        out_shape=(pltpu.SemaphoreType.DMA((n_sems,)), pltpu.SemaphoreType.DMA((n_sems,)),
                   *[pltpu.HBM(s.shape, s.dtype) for s in shards], *[pltpu.HBM(z.shape, z.dtype) for z in zones],
                   jax.ShapeDtypeStruct((8, LANES), F32)),
        in_specs=[_HBM] * (2 * n),
        out_specs=(_SEM, _SEM, *[_HBM] * (2 * n), pl.BlockSpec(memory_space=pltpu.VMEM)),
        input_output_aliases={i: 2 + i for i in range(2 * n)},
        compiler_params=pltpu.CompilerParams(has_side_effects=pltpu.SideEffectType.DATAFLOW_SIDE_EFFECTING),
    )(*[pltpu.with_memory_space_constraint(s, pltpu.HBM) for s in shards], *zones)
    return out[0], out[1], out[2:2 + n], out[2 + n:2 + 2 * n], out[-1]


def _allgather_wait(send_sems, recv_sems, shards, zones, axes, after):
    n = len(shards)
    shapes = [s.shape for s in shards]

    def body(*refs):
        srcs, lands = refs[:n], refs[n:2 * n]
        outgoing, incoming = _direct_gather_copies(srcs, lands, shapes, axes, refs[2 * n], refs[2 * n + 1])
        for cp in outgoing:
            cp.wait_send()
        for cp in incoming:
            cp.wait_recv()

    out = pl.pallas_call(
        body, name="allgather_rest_wait",
        out_shape=(*[pltpu.HBM(s.shape, s.dtype) for s in shards], *[pltpu.HBM(z.shape, z.dtype) for z in zones]),
        in_specs=[_HBM] * (2 * n) + [_SEM, _SEM, _ANY], out_specs=[_HBM] * (2 * n),
        input_output_aliases={i: i for i in range(2 * n)},
        compiler_params=pltpu.CompilerParams(has_side_effects=pltpu.SideEffectType.DATAFLOW_SIDE_EFFECTING),
    )(*shards, *zones, send_sems, recv_sems, after)
    return out[n:]


def _exchange_core_halves(g):
    n, _, H, C = g.shape

    def body(g_ref, got_ref, send_sem, recv_sem):
        x, y, c = _mesh_pos()
        swap = pltpu.make_async_remote_copy(
            src_ref=g_ref.at[pl.ds(0, n), 1 - c], dst_ref=got_ref, send_sem=send_sem, recv_sem=recv_sem,
            device_id=(x, y, 1 - c), device_id_type=MESH)
        swap.start()
        swap.wait()

    return pl.pallas_call(
        body, name="exchange_core_halves", in_specs=[_ANY], out_specs=_ANY,
        out_shape=_sds((n, H, C), g.dtype),
        scratch_shapes=[pltpu.SemaphoreType.DMA, pltpu.SemaphoreType.DMA],
    )(g)


def _exchange_copy(g_ref, got_ref, send_sems, recv_sems):
    x, y, c = _mesh_pos()
    return pltpu.make_async_remote_copy(
        src_ref=g_ref.at[pl.ds(0, g_ref.shape[0]), 1 - c], dst_ref=got_ref, send_sem=send_sems.at[0],
        recv_sem=recv_sems.at[0], device_id=(x, y, 1 - c), device_id_type=MESH)


def _exchange_start(g):
    n, _, H, C = g.shape

    def body(g_ref, got_ref, send_sems, recv_sems, g_thru, got_thru, token):
        _exchange_copy(g_ref, got_ref, send_sems, recv_sems).start()
        token[...] = jnp.zeros_like(token)

    zone = pltpu.with_memory_space_constraint(lax.empty((n, H, C), g.dtype), pltpu.HBM)
    return pl.pallas_call(
        body, name="exchange_rest_start",
        out_shape=(pltpu.SemaphoreType.DMA((1,)), pltpu.SemaphoreType.DMA((1,)), pltpu.HBM(g.shape, g.dtype),
                   pltpu.HBM(zone.shape, zone.dtype), jax.ShapeDtypeStruct((8, LANES), F32)),
        in_specs=[_HBM, _HBM], out_specs=(_SEM, _SEM, _HBM, _HBM, pl.BlockSpec(memory_space=pltpu.VMEM)),
        input_output_aliases={0: 2, 1: 3},
        compiler_params=pltpu.CompilerParams(has_side_effects=pltpu.SideEffectType.DATAFLOW_SIDE_EFFECTING),
    )(pltpu.with_memory_space_constraint(g, pltpu.HBM), zone)


def _exchange_wait(send_sems, recv_sems, g, zone, after):
    def body(g_ref, got_ref, send_sems, recv_sems, after_ref, g_out, got_out):
        copy = _exchange_copy(g_ref, got_ref, send_sems, recv_sems)
        copy.wait_send()
        copy.wait_recv()

    return pl.pallas_call(
        body, name="exchange_rest_wait",
        out_shape=(pltpu.HBM(g.shape, g.dtype), pltpu.HBM(zone.shape, zone.dtype)),
        in_specs=[_HBM, _HBM, _SEM, _SEM, _ANY], out_specs=[_HBM, _HBM], input_output_aliases={0: 0, 1: 1},
        compiler_params=pltpu.CompilerParams(has_side_effects=pltpu.SideEffectType.DATAFLOW_SIDE_EFFECTING),
    )(g, zone, send_sems, recv_sems, after)


def _scatter_copies(p_ref, q_ref, send_sems, recv_sems):
    x, y, c = _mesh_pos()
    return [pltpu.make_async_remote_copy(
        src_ref=p_ref.at[2 * cx + cy], dst_ref=q_ref.at[j], send_sem=send_sems.at[j],
        recv_sem=recv_sems.at[j], device_id=(cx, cy, c), device_id_type=MESH)
        for j, (cx, cy) in enumerate(_other_chips(x, y))]


def _scatter_start(p, part):
    n, H, C = p.shape

    def body(p_ref, q_ref, send_sems, recv_sems, p_thru, q_thru, token):
        for cp in _scatter_copies(p_ref, q_ref, send_sems, recv_sems):
            cp.start()
        token[...] = jnp.zeros_like(token)

    zone = pltpu.with_memory_space_constraint(lax.empty((n - 1, H, C), p.dtype), pltpu.HBM)
    return pl.pallas_call(
        body, name="scatter_%s_start" % part,
        out_shape=(pltpu.SemaphoreType.DMA((n - 1,)), pltpu.SemaphoreType.DMA((n - 1,)), pltpu.HBM(p.shape, p.dtype),
                   pltpu.HBM(zone.shape, zone.dtype), jax.ShapeDtypeStruct((8, LANES), F32)),
        in_specs=[_HBM, _HBM], out_specs=(_SEM, _SEM, _HBM, _HBM, pl.BlockSpec(memory_space=pltpu.VMEM)),
        input_output_aliases={0: 2, 1: 3},
        compiler_params=pltpu.CompilerParams(has_side_effects=pltpu.SideEffectType.DATAFLOW_SIDE_EFFECTING),
    )(pltpu.with_memory_space_constraint(p, pltpu.HBM), zone)


def _scatter_wait(send_sems, recv_sems, p, zone, after, part):
    def body(p_ref, q_ref, send_sems, recv_sems, after_ref, p_out, q_out):
        copies = _scatter_copies(p_ref, q_ref, send_sems, recv_sems)
        for cp in copies:
            cp.wait_send()
        for cp in copies:
            cp.wait_recv()

    return pl.pallas_call(
        body, name="scatter_%s_wait" % part,
        out_shape=(pltpu.HBM(p.shape, p.dtype), pltpu.HBM(zone.shape, zone.dtype)),
        in_specs=[_HBM, _HBM, _SEM, _SEM, _ANY], out_specs=[_HBM, _HBM], input_output_aliases={0: 0, 1: 1},
        compiler_params=pltpu.CompilerParams(has_side_effects=pltpu.SideEffectType.DATAFLOW_SIDE_EFFECTING),
    )(p, zone, send_sems, recv_sems, after)


N_DEVICES = 8


def _allreduce_row(v):
    C = v.shape[1]

    def body(v_ref, out_ref, rows, send_sems, recv_sems):
        x, y, c = _mesh_pos()
        me = 4 * x + 2 * y + c
        rows[pl.ds(me, 1)] = v_ref[...].reshape(1, 1, C)
        copies = []
        for m in range(1, N_DEVICES):
            peer = (x ^ (m >> 2), y ^ ((m >> 1) & 1), c ^ (m & 1))
            copies.append(pltpu.make_async_remote_copy(
                src_ref=rows.at[me], dst_ref=rows.at[me], send_sem=send_sems.at[m - 1],
                recv_sem=recv_sems.at[m - 1], device_id=peer, device_id_type=MESH))
        for cp in copies:
            cp.start()
        for m, cp in enumerate(copies, start=1):
            cp.wait_send()
            pltpu.make_async_remote_copy(
                src_ref=rows.at[me], dst_ref=rows.at[me ^ m], send_sem=send_sems.at[m - 1],
                recv_sem=recv_sems.at[m - 1], device_id=(x, y, c), device_id_type=MESH).wait_recv()
        total = rows[0]
        for d in range(1, N_DEVICES):
            total = total + rows[d]
        out_ref[...] = total

    vmem = pl.BlockSpec(memory_space=pltpu.VMEM)
    return pl.pallas_call(
        body, name="allreduce_row", in_specs=[vmem], out_specs=vmem, out_shape=_sds((1, C), F32),
        scratch_shapes=[pltpu.VMEM((N_DEVICES, 1, C), F32), pltpu.SemaphoreType.DMA((N_DEVICES - 1,)),
                        pltpu.SemaphoreType.DMA((N_DEVICES - 1,))],
    )(v)


def _share_core_halves(r2):
    _, H, C = r2.shape

    def body(r_ref, out_ref, send_sem, recv_sem):
        x, y, c = _mesh_pos()
        send = pltpu.make_async_remote_copy(
            src_ref=r_ref.at[c], dst_ref=out_ref.at[c], send_sem=send_sem, recv_sem=recv_sem,
            device_id=(x, y, 1 - c), device_id_type=MESH)
        send.start()
        send.wait_send()
        pltpu.make_async_remote_copy(
            src_ref=r_ref.at[c], dst_ref=out_ref.at[1 - c], send_sem=send_sem, recv_sem=recv_sem,
            device_id=(x, y, 1 - c), device_id_type=MESH).wait_recv()

    return pl.pallas_call(
        body, name="share_core_halves", in_specs=[_ANY], out_specs=_ANY,
        out_shape=_sds(r2.shape, r2.dtype), input_output_aliases={0: 0},
        scratch_shapes=[pltpu.SemaphoreType.DMA, pltpu.SemaphoreType.DMA],
    )(r2)


def _place():
    x, y, c = _mesh_pos()
    return jnp.stack([c, 2 * x + y]).astype(jnp.int32)


def _sum_pair(g, got, place):
    n, _, H, C = g.shape

    def body(place_ref, a_ref, b_ref, o_ref):
        o_ref[...] = (a_ref[...] + b_ref[...]).astype(BF16)

    spec = pl.BlockSpec((1, FLAT_TILE, C), lambda s, i, pr: (s, i, 0))
    return pl.pallas_call(
        body, name="sum_core_pair",
        grid_spec=pltpu.PrefetchScalarGridSpec(
            num_scalar_prefetch=1, grid=(n, H // FLAT_TILE),
            in_specs=[pl.BlockSpec((1, None, FLAT_TILE, C), lambda s, i, pr: (s, pr[0], i, 0)), spec],
            out_specs=spec),
        out_shape=_sds((n, H, C), BF16),
        compiler_params=_params(("parallel", "parallel")))(place, g, got)


def _sum_chips(p, q, place):
    n, H, C = p.shape

    def body(place_ref, own_ref, qx_ref, qy_ref, qxy_ref, o_ref):
        mine = place_ref[1]
        own, qx, qy, qxy = (t[0].astype(F32) for t in (own_ref, qx_ref, qy_ref, qxy_ref))

        def term(s):
            rel = jnp.full(own.shape, mine ^ s, jnp.int32)
            return jnp.where(rel == 0, own, jnp.where(rel == 2, qx, jnp.where(rel == 1, qy, qxy)))

        o_ref[0] = ((term(0) + term(1)) + term(2)) + term(3)

    qspec = lambda j: pl.BlockSpec((1, FLAT_TILE, C), lambda i, pr: (j, i, 0))
    return pl.pallas_call(
        body, name="sum_chips",
        grid_spec=pltpu.PrefetchScalarGridSpec(
            num_scalar_prefetch=1, grid=(H // FLAT_TILE,),
            in_specs=[pl.BlockSpec((1, FLAT_TILE, C), lambda i, pr: (pr[1], i, 0)), qspec(0), qspec(1), qspec(2)],
            out_specs=pl.BlockSpec((1, FLAT_TILE, C), lambda i, pr: (pr[0], i, 0))),
        out_shape=_sds((2, H, C), F32),
        compiler_params=_params(("parallel",)))(place, p, q, q, q)


ADAMW_BLOCK = 1 << 18


def _adamw(w, g, m, v):
    shape = w.shape
    C = shape[-1]
    R = _size(shape) // C
    tr = R
    while tr * C > ADAMW_BLOCK and tr % 16 == 0:
        tr //= 2
    w, g, m, v = (t.reshape(R, C) for t in (w, g, m, v))

    def body(w_ref, g_ref, m_ref, v_ref, d_ref, nm_ref, nv_ref):
        gv = g_ref[...]
        nm = ADAM_B1 * m_ref[...] + (1.0 - ADAM_B1) * gv
        nv = ADAM_B2 * v_ref[...] + (1.0 - ADAM_B2) * (gv * gv)
        m_hat = nm / (1.0 - ADAM_B1 ** ADAM_STEP)
        v_hat = nv / (1.0 - ADAM_B2 ** ADAM_STEP)
        d_ref[...] = -ADAM_LR * (m_hat / (jnp.sqrt(v_hat) + ADAM_EPS) + ADAM_WD * w_ref[...])
        nm_ref[...] = nm
        nv_ref[...] = nv

    spec = pl.BlockSpec((tr, C), lambda i: (i, 0))
    outs = pl.pallas_call(body, name="adamw", grid=(R // tr,), in_specs=[spec] * 4, out_specs=[spec] * 3,
                          out_shape=[_sds((R, C), F32)] * 3, compiler_params=_params(("parallel",)))(w, g, m, v)
    return tuple(t.reshape(shape) for t in outs)


def _chip_partials(gfull, place):
    n, R, C = gfull.shape
    g4 = gfull.reshape(n, 2, R // 2, C)
    return _sum_pair(g4, _exchange_core_halves(g4), place)


def _finish_reduce(p, q, place):
    r2 = _share_core_halves(_sum_chips(p, q, place))
    return r2.reshape(2 * r2.shape[1], r2.shape[2])


def _local_step(x, p, positions, loss_target, pre_g, post_g, w, later_weights, later_grads, later_grads_arrived,
                first_grads):
    Bl, S, _ = x.shape
    T = Bl * S
    cos, sin = _rope_tables(positions)
    xs = x.reshape(T, D_MODEL)
    saved = []
    for i in range(DEPTH):
        if i == 1:
            later_weights(xs)
        j = i // 2
        g_pre, g_post = pre_g[i:i + 1], post_g[i:i + 1]
        h = _rmsnorm_fwd(xs, g_pre)
        st = {"x": xs, "h": h}
        if i % 2 == 0:
            proj = _mm(h, w["attn_w_in"][j], name="attn_in")
            res = [_attn_fwd(proj, cos, sin, g, Bl, S) for g in range(N_GROUPS)]
            a, o, lse = _attn_combine([r[0] for r in res], [r[1] for r in res], proj)
            w_out = w["attn_w_out"][j]
            st.update(proj=proj, a=a, o=o, lse=lse, qkv=[r[2] for r in res])
        else:
            w_ab, w_z = w["conv_w_in"][j][:, :2 * D_MODEL], w["conv_w_in"][j][:, 2 * D_MODEL:]
            ab = _mm(h, w_ab, out_dtype=BF16, name="conv_in_ab")
            z = _mm(h, w_z, out_dtype=BF16, name="conv_in_z")
            dw = jnp.pad(w["conv_dw_w"][j], ((0, 1), (0, 0)))
            u1, a = _conv_fwd(ab, z, dw, w["conv_dw_b"][j:j + 1], w["conv_ln_g"][j:j + 1],
                              w["conv_ln_b"][j:j + 1], Bl, S)
            w_out = w["conv_w_out"][j]
            st.update(w_ab=w_ab, w_z=w_z, ab=ab, z=z, dw=dw, u1=u1, a=a)
        y, x1 = _mm_rows(a, w_out, [xs, g_post], _post_epilogue, (F32, F32), "branch_out_post")
        pi = p[i].reshape(T, PLE_DIM)
        pe = _mm(pi, w["ple_w_proj"][i], name="ple_proj")
        gl, xs = _mm_rows(x1, w["ple_w_gate"][i], [pe], _ple_epilogue, (F32, F32), "ple_gate_fwd")
        st.update(y=y, x1=x1, pi=pi, pe=pe, gl=gl)
        saved.append(st)

    sq, dx = _loss_fwd_bwd(xs, loss_target.reshape(T, D_MODEL))

    grads = {n: [None] * shape[0] for n, shape, _ in _LAYOUT}
    for i in reversed(range(DEPTH)):
        j = i // 2
        st = saved[i]
        g_pre, g_post = pre_g[i:i + 1], post_g[i:i + 1]
        if i == 0:
            rest = {n: jnp.stack(v[1:] if n in _FIRST_LAYER_PARAMS else v) for n, v in grads.items()}
            g_post = g_post + later_grads(rest)[0, 0]
        dpe, dgl, dx1, dy, dg_post = _ple_post_bwd(dx, st["pe"], st["gl"], w["ple_w_gate"][i], st["y"], g_post)
        grads["ple_w_proj"][i] = _mm(st["pi"], dpe, ta=True, name="ple_proj_wgrad")
        grads["ple_w_gate"][i] = _mm(st["x1"], dgl, ta=True, name="ple_gate_wgrad")
        grads["post_norm_g"][i] = dg_post[0]
        if i % 2 == 0:
            grads["attn_w_out"][j] = _mm(st["a"], dy, ta=True, name="attn_out_wgrad")
            da = _mm(dy, w["attn_w_out"][j], tb=True, name="attn_out_dgrad")
            do, dproj = _gate_bwd(da, st["o"], st["proj"])
            cos_b = cos + later_grads_arrived(do)[0, 0] if i == 0 else cos
            for g in range(N_GROUPS):
                dproj = _attn_bwd(st["qkv"][g], cos_b, sin, do, st["o"], st["lse"], dproj, g, Bl, S)
            grads["attn_w_in"][j] = _mm(st["h"], dproj, ta=True, name="attn_in_wgrad")
            if i == 0:
                first = {n: jnp.stack(grads[n][:1]) for n in _FIRST_LAYER_PARAMS if n != "pre_norm_g"}
                g_pre = g_pre + first_grads(first)[0, 0]
            dx, dg_pre = _dgrad_pre_bwd(dproj, w["attn_w_in"][j], st["x"], g_pre, dx1, None, 2048, "attn_in_dgrad_pre")
        else:
            grads["conv_w_out"][j] = _mm(st["a"], dy, ta=True, name="conv_out_wgrad")
            da2 = _mm(dy, w["conv_w_out"][j], tb=True, name="conv_out_dgrad")
            du1, dz, dln_g, dln_b = _conv_norm_bwd(da2, st["z"], st["u1"], w["conv_ln_g"][j:j + 1],
                                                   w["conv_ln_b"][j:j + 1])
            dab, ddw, ddb = _conv_bwd(st["ab"], du1, st["dw"], Bl, S)
            dh = _mm(dz, st["w_z"], tb=True, name="conv_in_z_dgrad")
            dx, dg_pre = _dgrad_pre_bwd(dab, st["w_ab"], st["x"], g_pre, dx1, dh, 2048, "conv_in_dgrad_pre")
            dw_ab = _mm(st["h"], dab, ta=True, name="conv_in_ab_wgrad")
            dw_z = _mm(st["h"], dz, ta=True, name="conv_in_z_wgrad")
            grads["conv_w_in"][j] = jnp.concatenate([dw_ab, dw_z], axis=1)
            grads["conv_dw_w"][j] = ddw[:CONV_WIDTH]
            grads["conv_dw_b"][j] = ddb[0]
            grads["conv_ln_g"][j] = dln_g[0]
            grads["conv_ln_b"][j] = dln_b[0]
        grads["pre_norm_g"][i] = dg_pre[0]
    return sq, dx.reshape(Bl, S, D_MODEL), grads["pre_norm_g"][0][None]


_NAMES = tuple(n for n, _, _ in _LAYOUT)


def kernel(x, p, positions, pre_norm_g, post_norm_g, attn_w_in, attn_w_out, conv_w_in, conv_dw_w, conv_dw_b, conv_ln_g, conv_ln_b, conv_w_out, ple_w_proj, ple_w_gate, loss_target, m_pre_norm_g, m_post_norm_g, m_attn_w_in, m_attn_w_out, m_conv_w_in, m_conv_dw_w, m_conv_dw_b, m_conv_ln_g, m_conv_ln_b, m_conv_w_out, m_ple_w_proj, m_ple_w_gate, v_pre_norm_g, v_post_norm_g, v_attn_w_in, v_attn_w_out, v_conv_w_in, v_conv_dw_w, v_conv_dw_b, v_conv_ln_g, v_conv_ln_b, v_conv_w_out, v_ple_w_proj, v_ple_w_gate):
    w_loc = dict(zip(_NAMES, (pre_norm_g, post_norm_g, attn_w_in, attn_w_out, conv_w_in, conv_dw_w, conv_dw_b,
                              conv_ln_g, conv_ln_b, conv_w_out, ple_w_proj, ple_w_gate)))
    m_loc = dict(zip(_NAMES, (m_pre_norm_g, m_post_norm_g, m_attn_w_in, m_attn_w_out, m_conv_w_in, m_conv_dw_w,
                              m_conv_dw_b, m_conv_ln_g, m_conv_ln_b, m_conv_w_out, m_ple_w_proj, m_ple_w_gate)))
    v_loc = dict(zip(_NAMES, (v_pre_norm_g, v_post_norm_g, v_attn_w_in, v_attn_w_out, v_conv_w_in, v_conv_dw_w,
                              v_conv_dw_b, v_conv_ln_g, v_conv_ln_b, v_conv_w_out, v_ple_w_proj, v_ple_w_gate)))

    bf = {n: w_loc[n].astype(BF16) for n in _MATMUL_WEIGHTS}
    axes = [_AXIS[n] for n in _MATMUL_WEIGHTS]
    first, small = _allgather_weights([bf[n][:1] for n in _FIRST_LAYER], [_AXIS[n] for n in _FIRST_LAYER],
                                      _stack_small(w_loc))
    send_sems, recv_sems, shards, zones, token = _allgather_start(
        [bf[n][1:] if n in _FIRST_LAYER else bf[n] for n in _MATMUL_WEIGHTS], axes)
    w_full = dict({n: [full[0]] for n, full in zip(_FIRST_LAYER, first)}, **_unstack_small(small))

    def later_weights(after):
        for n, full in zip(_MATMUL_WEIGHTS, _allgather_wait(send_sems, recv_sems, shards, zones, axes, after)):
            w_full[n] = w_full[n] + [full[l] for l in range(full.shape[0])] if n in _FIRST_LAYER else full

    place = _place()
    rest_shapes, first_shapes = _part_shapes(False), _part_shapes(True)
    rest_halves, rest_flight, first_flight = [], [], []

    def later_grads(grads):
        gfull = _pack_full_grads(grads, rest_shapes)
        rest_halves.extend(_exchange_start(gfull.reshape(N_CHIPS, 2, gfull.shape[1] // 2, FLAT_COLS)))
        return rest_halves[4]

    def later_grads_arrived(after):
        rest_flight.extend(_scatter_start(_sum_pair(*_exchange_wait(*rest_halves[:4], after), place), "rest"))
        return rest_flight[4]

    def first_grads(grads):
        first_flight.extend(_scatter_start(_chip_partials(_pack_full_grads(grads, first_shapes), place), "first"))
        return first_flight[4]

    sq, grad_x, dg_pre0 = _local_step(x, p, positions, loss_target, pre_norm_g + token[0, 0], post_norm_g,
                                      w_full, later_weights, later_grads, later_grads_arrived, first_grads)
    loss = lax.psum(sq[0, 0] * (0.5 / D_MODEL), ("x", "y", "c"))

    p_rest, q_rest = _scatter_wait(*rest_flight[:4], grad_x, "rest")
    g_rest = _unpack_f32(_finish_reduce(p_rest, q_rest, place), rest_shapes)
    p_first, q_first = _scatter_wait(*first_flight[:4], grad_x, "first")
    g_first = _unpack_f32(_finish_reduce(p_first, q_first, place), first_shapes)
    g_first["pre_norm_g"] = _allreduce_row(dg_pre0)
    g_out = {n: jnp.concatenate([g_first[n], g_rest[n]]) if n in g_first and n in g_rest
             else g_rest.get(n, g_first.get(n)) for n in _NAMES}
    updates = {n: _adamw(w_loc[n], g_out[n], m_loc[n], v_loc[n]) for n in _NAMES}
    d_out, m_out, v_out = ({n: updates[n][k] for n in _NAMES} for k in range(3))
    return (loss, grad_x, *[g_out[n] for n in _NAMES], *[d_out[n] for n in _NAMES],
            *[m_out[n] for n in _NAMES], *[v_out[n] for n in _NAMES])
```

```python
import math

import jax
import jax.numpy as jnp
from jax import lax
from jax.experimental import pallas as pl
from jax.experimental.pallas import tpu as pltpu

F32 = jnp.float32
BF16 = jnp.bfloat16

D_MODEL = 1024
DEPTH = 4
PLE_DIM = 256
HEAD_DIM = 64
WIN_DIL = ((128, 1), (512, 4), (2048, 16))
N_GROUPS = 3
N_BACK = 128
BLOCK_UNROLL = 8
RESIDUES_TOGETHER = 2
ROPE_THETA = 10000.0
CONV_WIDTH = 31
CONV_HALO = 32
RMS_EPS = 1e-6
LN_EPS = 1e-5
NEG_INF = -1e30
ADAM_LR, ADAM_B1, ADAM_B2, ADAM_EPS, ADAM_WD, ADAM_STEP = 0.001, 0.9, 0.999, 1e-08, 0.01, 10

LANES = 128
N_CHIPS = 4
VMEM_LIMIT = 48 * 1024 * 1024
VMEM_LIMIT_ATTN = 56 * 1024 * 1024
FLAT_COLS = 256
FLAT_TILE = 2048
FLAT_ROW_ALIGN = 16
PROJ_COLS = (3 * N_GROUPS + 1) * D_MODEL
HEAD_PAIRS = D_MODEL // LANES

MESH = pl.DeviceIdType.MESH


def _params(sem=None, vmem=VMEM_LIMIT):
    return pltpu.CompilerParams(dimension_semantics=sem, vmem_limit_bytes=vmem)


def _sigmoid(v):
    return 1.0 / (1.0 + jnp.exp(-v))


def _mm(a, b, *, ta=False, tb=False, out_dtype=F32, tm=1024, tn=1024, tk=1024, name="mm"):
    if ta:
        K, M = a.shape
    else:
        M, K = a.shape
    if tb:
        N, K2 = b.shape
    else:
        K2, N = b.shape
    assert K == K2, (a.shape, b.shape)
    tm, tn, tk = min(tm, M), min(tn, N), min(tk, K)
    assert M % tm == 0 and N % tn == 0 and K % tk == 0
    nk = K // tk
    dims = (((0 if ta else 1,), (1 if tb else 0,)), ((), ()))

    def body(a_ref, b_ref, o_ref, *scratch):
        k = pl.program_id(2)
        part = lax.dot_general(a_ref[...].astype(BF16), b_ref[...].astype(BF16), dims, preferred_element_type=F32)
        if nk == 1:
            o_ref[...] = part.astype(out_dtype)
        else:
            acc_ref, = scratch

            @pl.when(k == 0)
            def _():
                acc_ref[...] = part

            @pl.when((k > 0) & (k < nk - 1))
            def _():
                acc_ref[...] += part

            @pl.when(k == nk - 1)
            def _():
                o_ref[...] = (acc_ref[...] + part).astype(out_dtype)

    a_spec = pl.BlockSpec((tk, tm), lambda i, j, k: (k, i)) if ta else pl.BlockSpec((tm, tk), lambda i, j, k: (i, k))
    b_spec = pl.BlockSpec((tn, tk), lambda i, j, k: (j, k)) if tb else pl.BlockSpec((tk, tn), lambda i, j, k: (k, j))
    return pl.pallas_call(
        body, name=name, grid=(M // tm, N // tn, nk),
        in_specs=[a_spec, b_spec], out_specs=pl.BlockSpec((tm, tn), lambda i, j, k: (i, j)),
        out_shape=jax.ShapeDtypeStruct((M, N), out_dtype),
        scratch_shapes=[pltpu.VMEM((tm, tn), F32)] if nk > 1 else [],
        compiler_params=_params(("parallel", "parallel", "arbitrary")),
    )(a, b)


def _mm_rows(a, b, extras, epilogue, out_dtypes, name, tm=512):
    M, K = a.shape
    N = b.shape[1]
    n_ex = len(extras)

    def body(*refs):
        a_ref, b_ref = refs[:2]
        av = a_ref[...]
        acc = jnp.dot(av.astype(BF16), b_ref[...].astype(BF16), preferred_element_type=F32)
        results = epilogue(acc, av, *[e[...] for e in refs[2:2 + n_ex]])
        for o_ref, r in zip(refs[2 + n_ex:], results):
            o_ref[...] = r.astype(o_ref.dtype)

    tile = pl.BlockSpec((tm, N), lambda i: (i, 0))
    in_specs = [pl.BlockSpec((tm, K), lambda i: (i, 0)), pl.BlockSpec((K, N), lambda i: (0, 0))]
    in_specs += [tile if e.shape[0] == M else pl.BlockSpec((1, N), lambda i: (0, 0)) for e in extras]
    return pl.pallas_call(
        body, name=name, grid=(M // tm,), in_specs=in_specs, out_specs=[tile] * len(out_dtypes),
        out_shape=[jax.ShapeDtypeStruct((M, N), dt) for dt in out_dtypes],
        compiler_params=_params(("parallel",)),
    )(a, b, *extras)


ROW_TILE = 512


def _rows(w=D_MODEL, cb=0, tr=ROW_TILE):
    return pl.BlockSpec((tr, w), lambda i: (i, cb))


def _full(shape):
    return pl.BlockSpec(shape, lambda i: (0,) * len(shape))


def _row_call(body, name, T, in_specs, out_specs, out_shape, args, tr=ROW_TILE):
    return pl.pallas_call(body, name=name, grid=(T // tr,), in_specs=in_specs, out_specs=out_specs,
                          out_shape=out_shape, compiler_params=_params(("arbitrary",)))(*args)


def _sds(shape, dtype):
    return jax.ShapeDtypeStruct(shape, dtype)


def _rmsnorm_fwd(x, g):
    T = x.shape[0]

    def body(x_ref, g_ref, h_ref):
        xv = x_ref[...]
        r = lax.rsqrt(jnp.mean(xv * xv, axis=1, keepdims=True) + RMS_EPS)
        h_ref[...] = (xv * r * g_ref[...]).astype(BF16)

    return _row_call(body, "rmsnorm_fwd", T, [_rows(), _full((1, D_MODEL))], _rows(),
                     _sds((T, D_MODEL), BF16), (x, g))


def _post_epilogue(y, a_tile, x, g):
    del a_tile
    return y, x + y * lax.rsqrt(jnp.mean(y * y, axis=1, keepdims=True) + RMS_EPS) * g


def _ple_epilogue(gl, x1, pe):
    return gl, x1 + pe * _sigmoid(gl)


def _dgrad_pre_bwd(a, b, x, g, dx1, add, tk, name, tm=512):
    M, K = a.shape
    N = b.shape[0]
    tk = min(tk, K)
    assert M % tm == 0 and K % tk == 0 and N == D_MODEL
    nk = K // tk

    def body(*refs):
        a_ref, b_ref, x_ref, g_ref, dx1_ref = refs[:5]
        add_ref = refs[5] if add is not None else None
        dx_ref, dg_ref = refs[-3:-1] if nk > 1 else refs[-2:]
        i, k = pl.program_id(0), pl.program_id(1)

        @pl.when((i == 0) & (k == 0))
        def _():
            dg_ref[...] = jnp.zeros_like(dg_ref)

        part = lax.dot_general(a_ref[...].astype(BF16), b_ref[...].astype(BF16), _NT, preferred_element_type=F32)

        def finish(dh):
            if add is not None:
                dh = dh + add_ref[...]
            xv = x_ref[...]
            r = lax.rsqrt(jnp.mean(xv * xv, axis=1, keepdims=True) + RMS_EPS)
            xh = xv * r
            dg_ref[...] += jnp.sum(dh * xh, axis=0, keepdims=True)
            dn = dh * g_ref[...]
            dx_ref[...] = dx1_ref[...] + r * (dn - xh * jnp.mean(dn * xh, axis=1, keepdims=True))

        if nk == 1:
            finish(part)
        else:
            acc_ref = refs[-1]

            @pl.when(k == 0)
            def _():
                acc_ref[...] = part

            @pl.when((k > 0) & (k < nk - 1))
            def _():
                acc_ref[...] += part

            @pl.when(k == nk - 1)
            def _():
                finish(acc_ref[...] + part)

    tile = pl.BlockSpec((tm, N), lambda i, k: (i, 0))
    row = pl.BlockSpec((1, N), lambda i, k: (0, 0))
    in_specs = [pl.BlockSpec((tm, tk), lambda i, k: (i, k)), pl.BlockSpec((N, tk), lambda i, k: (0, k)), tile, row, tile]
    args = [a, b, x, g, dx1]
    if add is not None:
        in_specs.append(tile)
        args.append(add)
    return pl.pallas_call(
        body, name=name, grid=(M // tm, nk), in_specs=in_specs, out_specs=[tile, row],
        out_shape=[_sds((M, N), F32), _sds((1, N), F32)],
        scratch_shapes=[pltpu.VMEM((tm, N), F32)] if nk > 1 else [],
        compiler_params=_params(("arbitrary", "arbitrary")),
    )(*args)


def _ple_post_bwd(dx2, pe, gl, w_gate, y, g_post):
    T = dx2.shape[0]

    def body(d_ref, pe_ref, gl_ref, w_ref, y_ref, g_ref, dpe_ref, dgl_ref, dx1_ref, dy_ref, dg_ref):
        @pl.when(pl.program_id(0) == 0)
        def _():
            dg_ref[...] = jnp.zeros_like(dg_ref)

        dv = d_ref[...]
        sg = _sigmoid(gl_ref[...])
        dpe_ref[...] = (dv * sg).astype(BF16)
        dgl = (dv * pe_ref[...] * sg * (1.0 - sg)).astype(BF16)
        dgl_ref[...] = dgl
        dx1 = dv + lax.dot_general(dgl, w_ref[...], _NT, preferred_element_type=F32)
        dx1_ref[...] = dx1
        yv = y_ref[...]
        r = lax.rsqrt(jnp.mean(yv * yv, axis=1, keepdims=True) + RMS_EPS)
        yh = yv * r
        dg_ref[...] += jnp.sum(dx1 * yh, axis=0, keepdims=True)
        dn = dx1 * g_ref[...]
        dy_ref[...] = (r * (dn - yh * jnp.mean(dn * yh, axis=1, keepdims=True))).astype(BF16)

    row = _full((1, D_MODEL))
    return _row_call(body, "ple_post_bwd", T, [_rows()] * 3 + [_full((D_MODEL, D_MODEL)), _rows(), row],
                     [_rows()] * 4 + [row],
                     [_sds((T, D_MODEL), BF16)] * 2 + [_sds((T, D_MODEL), F32), _sds((T, D_MODEL), BF16),
                                                      _sds((1, D_MODEL), F32)],
                     (dx2, pe, gl, w_gate, y, g_post))


def _loss_fwd_bwd(y, target):
    T = y.shape[0]

    def body(y_ref, t_ref, s_ref, d_ref):
        @pl.when(pl.program_id(0) == 0)
        def _():
            s_ref[...] = jnp.zeros_like(s_ref)

        e = y_ref[...] - t_ref[...]
        s_ref[...] += jnp.sum(e * e).reshape(1, 1)
        d_ref[...] = e * (1.0 / D_MODEL)

    return _row_call(body, "loss", T, [_rows()] * 2, [_full((1, 1)), _rows()],
                     [_sds((1, 1), F32), _sds((T, D_MODEL), F32)], (y, target))


def _attn_combine(outs, lses, proj):
    T = proj.shape[0]

    def body(o0, o1, o2, l0, l1, l2, z_ref, a_ref, o_ref, lse_ref):
        a0, a1, a2 = l0[...], l1[...], l2[...]
        m = jnp.maximum(jnp.maximum(a0, a1), a2)
        e0, e1, e2 = jnp.exp(a0 - m), jnp.exp(a1 - m), jnp.exp(a2 - m)
        ssum = e0 + e1 + e2
        o = (e0 * o0[...] + e1 * o1[...] + e2 * o2[...]) / ssum
        zv = z_ref[...].astype(F32)
        o_ref[...] = o
        lse_ref[...] = m + jnp.log(ssum)
        a_ref[...] = (o * zv * _sigmoid(zv)).astype(BF16)

    return _row_call(body, "attn_combine", T, [_rows()] * 6 + [_rows(cb=3 * N_GROUPS)], [_rows()] * 3,
                     [_sds((T, D_MODEL), BF16), _sds((T, D_MODEL), F32), _sds((T, D_MODEL), F32)],
                     (*outs, *lses, proj))


def _gate_bwd(da, o, proj):
    T = da.shape[0]

    def body(da_ref, o_ref, z_ref, do_ref, dz_ref):
        dv = da_ref[...]
        zv = z_ref[...]
        sg = _sigmoid(zv)
        do_ref[...] = dv * zv * sg
        dz_ref[...] = dv * o_ref[...] * sg * (1.0 + zv * (1.0 - sg))

    zcols = _rows(cb=3 * N_GROUPS)
    return _row_call(body, "gate_bwd", T, [_rows(), _rows(), zcols], [_rows(), zcols],
                     [_sds((T, D_MODEL), F32), _sds((T, PROJ_COLS), F32)], (da, o, proj))


def _rope_tables(positions):
    inv_freq = 1.0 / (ROPE_THETA ** (jnp.arange(0, HEAD_DIM, 2, dtype=F32) / HEAD_DIM))
    ang = positions.astype(F32)[..., None] * inv_freq
    cos, sin = jnp.cos(ang), jnp.sin(ang)
    return jnp.tile(cos, (1, 1, 4)), jnp.concatenate([-sin, sin, -sin, sin], axis=-1)


def _rotate_half_partner(t):
    lane = lax.broadcasted_iota(jnp.int32, t.shape, 1)
    return jnp.where((lane % HEAD_DIM) < HEAD_DIM // 2,
                     pltpu.roll(t, LANES - HEAD_DIM // 2, 1), pltpu.roll(t, HEAD_DIM // 2, 1))


def _mask_bias(first):
    qi = lax.broadcasted_iota(jnp.int32, (N_BACK, 2 * N_BACK), 0)
    kj = lax.broadcasted_iota(jnp.int32, (N_BACK, 2 * N_BACK), 1)
    ok = (kj >= qi) & (kj <= qi + N_BACK)
    if first:
        ok = ok & (kj >= N_BACK)
    return jnp.where(ok, 0.0, NEG_INF).astype(F32)


def _stack_heads(t, head0):
    zero = jnp.zeros_like(t)
    return jnp.concatenate([jnp.where(head0, t, zero), jnp.where(head0, zero, t)], axis=0)


def _unstack_heads(t2, head0):
    return jnp.where(head0, t2[:N_BACK], t2[N_BACK:])


def _block_loop(nb, block):
    first, rest = _mask_bias(True), _mask_bias(False)
    first, rest = jnp.concatenate([first, first], axis=0), jnp.concatenate([rest, rest], axis=0)
    if nb <= BLOCK_UNROLL:
        for n in range(nb):
            block(n, first if n == 0 else rest)
        return

    def step(n, carry):
        block(n, jnp.where(n == 0, first, rest))
        return carry

    lax.fori_loop(0, nb, step, 0, unroll=BLOCK_UNROLL)


def _for(count, body, unroll_fully):
    if unroll_fully:
        for i in range(count):
            body(i)
    else:
        lax.fori_loop(0, count, lambda i, carry: (body(i), carry)[1], 0, unroll=4)


def _residues_together(nb):
    return min(RESIDUES_TOGETHER, max(1, BLOCK_UNROLL // nb))


def _residue_loop(d, nb, residue):
    together = _residues_together(nb)
    assert d % together == 0

    def group(i, carry):
        for u in range(together):
            residue(i * together + u, u)
        return carry

    lax.fori_loop(0, d // together, group, 0)


_NT = (((1,), (1,)), ((), ()))
_TN = (((0,), (0,)), ((), ()))


def _residue_rows(r, i, d):
    start = r + i * (N_BACK * d)
    if d == 1:
        return pl.ds(pl.multiple_of(start, N_BACK), N_BACK)
    return pl.ds(start, N_BACK, stride=d)


def _seq_rows(i):
    return pl.ds(pl.multiple_of(i * N_BACK, N_BACK), N_BACK)


def _rows_at(base, i, size=N_BACK):
    return pl.ds(pl.multiple_of(base + i * N_BACK, N_BACK), size)


def _attn_fwd(proj, cos, sin, group, Bl, S):
    d = WIN_DIL[group][1]
    L = S // d
    nb = L // N_BACK
    P = L + N_BACK
    assert WIN_DIL[group][0] // d == N_BACK and L % N_BACK == 0

    def body(q_ref, k_ref, v_ref, cos_ref, sin_ref, o_ref, lse_ref, qr, kr, vp):
        head0 = lax.broadcasted_iota(jnp.int32, (1, LANES), 1) < HEAD_DIM
        zeros = jnp.zeros((N_BACK, LANES), BF16)

        def residue(r, u):
            del u
            qbase, kbase = r * L, r * P
            kr[_rows_at(kbase, 0), :] = zeros
            vp[_rows_at(kbase, 0), :] = zeros

            def rope(i):
                rows = _residue_rows(r, i, d)
                cs, sn = cos_ref[rows, :], sin_ref[rows, :]
                q, k = q_ref[rows, :], k_ref[rows, :]
                qr[_rows_at(qbase, i), :] = ((q * cs + _rotate_half_partner(q) * sn)
                                            * (HEAD_DIM ** -0.5)).astype(BF16)
                kr[_rows_at(kbase, i + 1), :] = (k * cs + _rotate_half_partner(k) * sn).astype(BF16)
                vp[_rows_at(kbase, i + 1), :] = v_ref[rows, :].astype(BF16)

            _for(nb, rope, nb <= BLOCK_UNROLL)

            def block(n, bias):
                win = _rows_at(kbase, n, 2 * N_BACK)
                q2, kw, vw = _stack_heads(qr[_rows_at(qbase, n), :], head0), kr[win, :], vp[win, :]
                s = lax.dot_general(q2, kw, _NT, preferred_element_type=F32) + bias
                m = jnp.max(s, axis=1, keepdims=True)
                p = jnp.exp(s - m)
                l = jnp.sum(p, axis=1, keepdims=True)
                pv = jnp.dot(p.astype(BF16), vw, preferred_element_type=F32)
                rows = _residue_rows(r, n, d)
                o_ref[rows, :] = _unstack_heads(pv * (1.0 / l), head0)
                lse_ref[rows, :] = _unstack_heads((m + jnp.log(l)) + jnp.zeros((2 * N_BACK, LANES), F32), head0)

            _block_loop(nb, block)

        _residue_loop(d, nb, residue)

    act = pl.BlockSpec((None, S, LANES), lambda b, hp: (b, 0, hp))
    tab = pl.BlockSpec((None, S, LANES), lambda b, hp: (b, 0, 0))
    col = lambda which: pl.BlockSpec((None, S, LANES),
                                     lambda b, hp: (b, 0, (which * N_GROUPS + group) * HEAD_PAIRS + hp))
    seq = lambda rows: pl.BlockSpec((None, None, rows, LANES), lambda b, hp: (b, hp, 0, 0))
    p3 = proj.reshape(Bl, S, PROJ_COLS)
    o, lse, qr, kr, vp = pl.pallas_call(
        body, name="attn_fwd_g%d" % group, grid=(Bl, HEAD_PAIRS),
        in_specs=[col(0), col(1), col(2), tab, tab], out_specs=[act, act, seq(S), seq(d * P), seq(d * P)],
        out_shape=[_sds((Bl, S, D_MODEL), F32)] * 2 + [_sds((Bl, HEAD_PAIRS, S, LANES), BF16)]
        + [_sds((Bl, HEAD_PAIRS, d * P, LANES), BF16)] * 2,
        compiler_params=_params(("parallel", "arbitrary"), VMEM_LIMIT_ATTN),
    )(p3, p3, p3, cos, sin)
    return o.reshape(Bl * S, D_MODEL), lse.reshape(Bl * S, D_MODEL), (qr, kr, vp)


def _attn_bwd(saved, cos, sin, do, o, lse, dproj, group, Bl, S):
    d = WIN_DIL[group][1]
    L = S // d
    nb = L // N_BACK
    P = L + N_BACK
    steps = Bl * HEAD_PAIRS

    def body(qr, kr, vp, cos_ref, sin_ref, do_ref, o_ref, lse_ref, dproj_in, dproj_ref,
             dk_accs, dv_accs, stage, sems):
        del dproj_in
        head0 = lax.broadcasted_iota(jnp.int32, (1, LANES), 1) < HEAD_DIM
        b, hp = pl.program_id(0), pl.program_id(1)
        step = b * HEAD_PAIRS + hp
        slot = step % 2
        dq_s, dk_s, dv_s = stage.at[slot, 0], stage.at[slot, 1], stage.at[slot, 2]

        def copies(which_slot):
            out = []
            for which in range(3):
                col = ((which * N_GROUPS + group) * HEAD_PAIRS + hp) * LANES
                out.append(pltpu.make_async_copy(
                    stage.at[which_slot, which], dproj_ref.at[b, :, pl.ds(pl.multiple_of(col, LANES), LANES)],
                    sems.at[which_slot, which]))
            return out

        @pl.when(step >= 2)
        def _():
            for cp in copies(slot):
                cp.wait()

        def residue(r, u):
            qbase, kbase = r * L, r * P
            dk_acc, dv_acc = dk_accs.at[u], dv_accs.at[u]
            dk_acc[...] = jnp.zeros_like(dk_acc)
            dv_acc[...] = jnp.zeros_like(dv_acc)

            def block(n, bias):
                win = pl.ds(pl.multiple_of(n * N_BACK, N_BACK), 2 * N_BACK)
                kwin = _rows_at(kbase, n, 2 * N_BACK)
                rows = _residue_rows(r, n, d)
                q2, kw, vw = _stack_heads(qr[_rows_at(qbase, n), :], head0), kr[kwin, :], vp[kwin, :]
                dof = do_ref[rows, :]
                do2 = _stack_heads(dof.astype(BF16), head0)
                lse_b = lse_ref[rows, :]
                lse2 = jnp.concatenate([lse_b[:, 0:1], lse_b[:, HEAD_DIM:HEAD_DIM + 1]], axis=0)
                dsum = _stack_heads(dof * o_ref[rows, :], head0)
                delta = jnp.sum(dsum, axis=1, keepdims=True)
                s = lax.dot_general(q2, kw, _NT, preferred_element_type=F32) + bias
                p = jnp.exp(s - lse2)
                dp = lax.dot_general(do2, vw, _NT, preferred_element_type=F32)
                ds = (p * (dp - delta)).astype(BF16)
                dq = _unstack_heads(jnp.dot(ds, kw, preferred_element_type=F32), head0) * (HEAD_DIM ** -0.5)
                cs, sn = cos_ref[rows, :], sin_ref[rows, :]
                dq_s[rows, :] = dq * cs + _rotate_half_partner(dq * sn)
                dk_acc[win, :] += lax.dot_general(ds, q2, _TN, preferred_element_type=F32)
                dv_acc[win, :] += lax.dot_general(p.astype(BF16), do2, _TN, preferred_element_type=F32)

            _block_loop(nb, block)

            def finish(i):
                rows = _residue_rows(r, i, d)
                cs, sn = cos_ref[rows, :], sin_ref[rows, :]
                dk = dk_acc[_seq_rows(i + 1), :]
                dk_s[rows, :] = dk * cs + _rotate_half_partner(dk * sn)
                dv_s[rows, :] = dv_acc[_seq_rows(i + 1), :]

            _for(nb, finish, nb <= BLOCK_UNROLL)

        _residue_loop(d, nb, residue)
        for cp in copies(slot):
            cp.start()

        @pl.when(step == steps - 1)
        def _():
            if steps > 1:
                for cp in copies(1 - slot):
                    cp.wait()
            for cp in copies(slot):
                cp.wait()

    act = pl.BlockSpec((None, S, LANES), lambda b, hp: (b, 0, hp))
    tab = pl.BlockSpec((None, S, LANES), lambda b, hp: (b, 0, 0))
    seq = lambda rows: pl.BlockSpec((None, None, rows, LANES), lambda b, hp: (b, hp, 0, 0))
    view = lambda t: t.reshape(Bl, S, D_MODEL)
    out = pl.pallas_call(
        body, name="attn_bwd_g%d" % group, grid=(Bl, HEAD_PAIRS),
        in_specs=[seq(S), seq(d * P), seq(d * P), tab, tab, act, act, act, _ANY], out_specs=_ANY,
        out_shape=_sds((Bl, S, PROJ_COLS), F32), input_output_aliases={8: 0},
        scratch_shapes=[pltpu.VMEM((_residues_together(nb), P, LANES), F32),
                        pltpu.VMEM((_residues_together(nb), P, LANES), F32),
                        pltpu.VMEM((2, 3, S, LANES), F32), pltpu.SemaphoreType.DMA((2, 3))],
        compiler_params=_params(("arbitrary", "arbitrary"), VMEM_LIMIT_ATTN),
    )(*saved, cos, sin, view(do), view(o), view(lse), dproj.reshape(Bl, S, PROJ_COLS))
    return out.reshape(Bl * S, PROJ_COLS)


CONV_TILE = 256
CONV_CHUNK = 64
SUBLANES = 8
CONV_SHIFT_ROWS = CONV_TILE + CONV_HALO - SUBLANES


def _fill_shifted(shifted, ext, cs):
    for k in range(1, SUBLANES):
        shifted[k - 1] = ext[pl.ds(k, CONV_SHIFT_ROWS), cs]


def _shifted_rows(shifted, ext, cs, off):
    k = off % SUBLANES
    if k == 0:
        return ext[pl.ds(off, CONV_CHUNK), cs]
    return shifted[k - 1, pl.ds(off - k, CONV_CHUNK), :]


def _conv_fwd(proj, z, dw, dwb, ln_g, ln_b, Bl, S):
    tr = CONV_TILE
    nj = S // tr
    hb = tr // CONV_HALO

    def body(a_ref, b_ref, ah_ref, bh_ref, z_ref, dw_ref, dwb_ref, g_ref, bb_ref, u1_ref, out_ref, ext, shifted):
        j = pl.program_id(1)
        halo = ah_ref[0].astype(F32) * _sigmoid(bh_ref[0].astype(F32))
        ext[pl.ds(0, CONV_HALO), :] = jnp.where(j > 0, halo, 0.0)
        ext[pl.ds(CONV_HALO, tr), :] = a_ref[0].astype(F32) * _sigmoid(b_ref[0].astype(F32))

        def cols(c, carry):
            cs = pl.ds(pl.multiple_of(c * LANES, LANES), LANES)
            _fill_shifted(shifted, ext, cs)
            for rc in range(tr // CONV_CHUNK):
                acc = jnp.zeros((CONV_CHUNK, LANES), F32)
                for w in range(CONV_WIDTH):
                    off = rc * CONV_CHUNK + CONV_HALO - (CONV_WIDTH - 1) + w
                    acc = acc + dw_ref[pl.ds(w, 1), cs] * _shifted_rows(shifted, ext, cs, off)
                u1_ref[0, pl.ds(rc * CONV_CHUNK, CONV_CHUNK), cs] = acc + dwb_ref[:, cs]
            return carry

        lax.fori_loop(0, D_MODEL // LANES, cols, 0)
        u1 = u1_ref[0]
        mu = jnp.mean(u1, axis=1, keepdims=True)
        xc = u1 - mu
        rstd = lax.rsqrt(jnp.mean(xc * xc, axis=1, keepdims=True) + LN_EPS)
        u2 = xc * rstd * g_ref[...] + bb_ref[...]
        zv = z_ref[0].astype(F32)
        out_ref[0] = (u2 * _sigmoid(u2) * zv * _sigmoid(zv)).astype(BF16)

    tile = lambda cb: pl.BlockSpec((1, tr, D_MODEL), lambda b, j: (b, j, cb))
    halo = lambda cb: pl.BlockSpec((1, CONV_HALO, D_MODEL), lambda b, j: (b, jnp.maximum(j * hb - 1, 0), cb))
    par = lambda r: pl.BlockSpec((r, D_MODEL), lambda b, j: (0, 0))
    p3 = proj.reshape(Bl, S, 2 * D_MODEL)
    u1, out = pl.pallas_call(
        body, name="conv_fwd", grid=(Bl, nj),
        in_specs=[tile(0), tile(1), halo(0), halo(1), tile(0), par(32), par(1), par(1), par(1)],
        out_specs=[tile(0), tile(0)],
        out_shape=[_sds((Bl, S, D_MODEL), F32), _sds((Bl, S, D_MODEL), BF16)],
        scratch_shapes=[pltpu.VMEM((tr + CONV_HALO, D_MODEL), F32),
                        pltpu.VMEM((SUBLANES - 1, CONV_SHIFT_ROWS, LANES), F32)],
        compiler_params=_params(("parallel", "arbitrary")),
    )(p3, p3, p3, p3, z.reshape(Bl, S, D_MODEL), dw, dwb, ln_g, ln_b)
    return u1.reshape(Bl * S, D_MODEL), out.reshape(Bl * S, D_MODEL)


def _conv_norm_bwd(da2, z, u1, ln_g, ln_b):
    T = da2.shape[0]

    def body(da_ref, z_ref, u_ref, g_ref, b_ref, du_ref, dz_ref, dg_ref, db_ref):
        @pl.when(pl.program_id(0) == 0)
        def _():
            dg_ref[...] = jnp.zeros_like(dg_ref)
            db_ref[...] = jnp.zeros_like(db_ref)

        u1 = u_ref[...]
        mu = jnp.mean(u1, axis=1, keepdims=True)
        xc = u1 - mu
        rstd = lax.rsqrt(jnp.mean(xc * xc, axis=1, keepdims=True) + LN_EPS)
        nrm = xc * rstd
        u2 = nrm * g_ref[...] + b_ref[...]
        s2 = _sigmoid(u2)
        zv = z_ref[...].astype(F32)
        sz = _sigmoid(zv)
        dv = da_ref[...]
        dz_ref[...] = (dv * u2 * s2 * sz * (1.0 + zv * (1.0 - sz))).astype(BF16)
        du2 = dv * zv * sz * s2 * (1.0 + u2 * (1.0 - s2))
        dg_ref[...] += jnp.sum(du2 * nrm, axis=0, keepdims=True)
        db_ref[...] += jnp.sum(du2, axis=0, keepdims=True)
        dn = du2 * g_ref[...]
        du_ref[...] = rstd * (dn - jnp.mean(dn, axis=1, keepdims=True)
                              - nrm * jnp.mean(dn * nrm, axis=1, keepdims=True))

    return _row_call(body, "conv_norm_bwd", T,
                     [_rows(), _rows(), _rows(), _full((1, D_MODEL)), _full((1, D_MODEL))],
                     [_rows(), _rows(), _full((1, D_MODEL)), _full((1, D_MODEL))],
                     [_sds((T, D_MODEL), F32), _sds((T, D_MODEL), BF16), _sds((1, D_MODEL), F32),
                      _sds((1, D_MODEL), F32)], (da2, z, u1, ln_g, ln_b))


def _conv_bwd(proj, du1, dw, Bl, S):
    tr = CONV_TILE
    nj = S // tr
    hb = tr // CONV_HALO

    def body(a_ref, b_ref, ah_ref, bh_ref, du_ref, duh_ref, dw_ref, dab_ref, ddw_ref, ddb_ref, uext, dext, du0,
             ushift, dshift, ddw8):
        first = (pl.program_id(0) == 0) & (pl.program_id(1) == 0)
        last = (pl.program_id(0) == Bl - 1) & (pl.program_id(1) == nj - 1)
        j = pl.program_id(1)

        @pl.when(first)
        def _():
            ddw8[...] = jnp.zeros_like(ddw8)
            ddb_ref[...] = jnp.zeros_like(ddb_ref)

        halo = ah_ref[0].astype(F32) * _sigmoid(bh_ref[0].astype(F32))
        uext[pl.ds(0, CONV_HALO), :] = jnp.where(j > 0, halo, 0.0)
        av = a_ref[0].astype(F32)
        sb = _sigmoid(b_ref[0].astype(F32))
        uext[pl.ds(CONV_HALO, tr), :] = av * sb
        dext[pl.ds(0, tr), :] = du_ref[0]
        dext[pl.ds(tr, CONV_HALO), :] = jnp.where(j < nj - 1, duh_ref[0], 0.0)
        ddb_ref[...] += jnp.sum(du_ref[0], axis=0, keepdims=True)

        def cols(c, carry):
            cs = pl.ds(pl.multiple_of(c * LANES, LANES), LANES)
            _fill_shifted(dshift, dext, cs)
            _fill_shifted(ushift, uext, cs)
            for rc in range(tr // CONV_CHUNK):
                base = rc * CONV_CHUNK
                acc = jnp.zeros((CONV_CHUNK, LANES), F32)
                for w in range(CONV_WIDTH):
                    acc = acc + dw_ref[pl.ds(w, 1), cs] * _shifted_rows(dshift, dext, cs, base + CONV_WIDTH - 1 - w)
                du0[pl.ds(base, CONV_CHUNK), cs] = acc
            for w in range(CONV_WIDTH):
                part = jnp.zeros((SUBLANES, LANES), F32)
                for rc in range(tr // CONV_CHUNK):
                    base = rc * CONV_CHUNK
                    prod = dext[pl.ds(base, CONV_CHUNK), cs] * _shifted_rows(
                        ushift, uext, cs, base + CONV_HALO - (CONV_WIDTH - 1) + w)
                    for i in range(CONV_CHUNK // SUBLANES):
                        part = part + prod[i * SUBLANES:(i + 1) * SUBLANES]
                ddw8[pl.ds(w * SUBLANES, SUBLANES), cs] += part
            return carry

        lax.fori_loop(0, D_MODEL // LANES, cols, 0)
        g = du0[...]
        dab_ref[0, :, 0:D_MODEL] = (g * sb).astype(BF16)
        dab_ref[0, :, D_MODEL:2 * D_MODEL] = (g * av * sb * (1.0 - sb)).astype(BF16)

        @pl.when(last)
        def _():
            for w in range(CONV_WIDTH + 1):
                ddw_ref[pl.ds(w, 1), :] = jnp.sum(ddw8[pl.ds(w * SUBLANES, SUBLANES), :], axis=0, keepdims=True)

    tile = lambda cb: pl.BlockSpec((1, tr, D_MODEL), lambda b, j: (b, j, cb))
    halo = lambda cb: pl.BlockSpec((1, CONV_HALO, D_MODEL), lambda b, j: (b, jnp.maximum(j * hb - 1, 0), cb))
    nxt = pl.BlockSpec((1, CONV_HALO, D_MODEL), lambda b, j: (b, jnp.minimum((j + 1) * hb, S // CONV_HALO - 1), 0))
    par = lambda r: pl.BlockSpec((r, D_MODEL), lambda b, j: (0, 0))
    p3 = proj.reshape(Bl, S, 2 * D_MODEL)
    d3 = du1.reshape(Bl, S, D_MODEL)
    dab, ddw, ddb = pl.pallas_call(
        body, name="conv_bwd", grid=(Bl, nj),
        in_specs=[tile(0), tile(1), halo(0), halo(1), tile(0), nxt, par(32)],
        out_specs=[pl.BlockSpec((1, tr, 2 * D_MODEL), lambda b, j: (b, j, 0)), par(32), par(1)],
        out_shape=[_sds((Bl, S, 2 * D_MODEL), BF16), _sds((32, D_MODEL), F32), _sds((1, D_MODEL), F32)],
        scratch_shapes=[pltpu.VMEM((tr + CONV_HALO, D_MODEL), F32), pltpu.VMEM((tr + CONV_HALO, D_MODEL), F32),
                        pltpu.VMEM((tr, D_MODEL), F32),
                        pltpu.VMEM((SUBLANES - 1, CONV_SHIFT_ROWS, LANES), F32),
                        pltpu.VMEM((SUBLANES - 1, CONV_SHIFT_ROWS, LANES), F32),
                        pltpu.VMEM(((CONV_WIDTH + 1) * SUBLANES, D_MODEL), F32)],
        compiler_params=_params(("arbitrary", "arbitrary")),
    )(p3, p3, p3, p3, d3, d3, dw)
    return dab.reshape(Bl * S, 2 * D_MODEL), ddw, ddb


_LAYOUT = (
    ("pre_norm_g", (4, 1024), None), ("post_norm_g", (4, 1024), None),
    ("attn_w_in", (2, 1024, 2560), 2), ("attn_w_out", (2, 256, 1024), 1),
    ("conv_w_in", (2, 1024, 768), 2), ("conv_dw_w", (2, 31, 256), 2),
    ("conv_dw_b", (2, 256), 1), ("conv_ln_g", (2, 256), 1), ("conv_ln_b", (2, 256), 1),
    ("conv_w_out", (2, 256, 1024), 1), ("ple_w_proj", (4, 256, 256), 2), ("ple_w_gate", (4, 256, 1024), 1),
)
_MATMUL_WEIGHTS = ("attn_w_in", "attn_w_out", "conv_w_in", "conv_w_out", "ple_w_proj", "ple_w_gate")
_FIRST_LAYER = ("attn_w_in", "attn_w_out", "ple_w_proj", "ple_w_gate")
_AXIS = {n: a for n, _, a in _LAYOUT}


def _size(shape):
    n = 1
    for s in shape:
        n *= s
    return n


def _padded_rows(shape):
    rows = _size(shape) // shape[-1]
    return rows + (-rows) % FLAT_ROW_ALIGN


def _rows2d(a):
    a2 = a.reshape(-1, a.shape[-1])
    pad = _padded_rows(a.shape) - a2.shape[0]
    return jnp.pad(a2, ((0, pad), (0, 0))) if pad else a2


def _col_blocks(a):
    a2 = _rows2d(a)
    return jnp.concatenate([a2[:, c:c + FLAT_COLS] for c in range(0, a2.shape[1], FLAT_COLS)], axis=0)


def _from_col_blocks(flat, off, shape):
    rows, nblk = _padded_rows(shape), shape[-1] // FLAT_COLS
    a2 = jnp.concatenate([flat[off + b * rows:off + (b + 1) * rows] for b in range(nblk)], axis=1)
    return a2[:_size(shape) // shape[-1]].reshape(shape), off + nblk * rows


def _shard_col_blocks(full, shape, axis):
    if axis is None:
        blocks = _col_blocks(full)
        return jnp.broadcast_to(blocks[None], (N_CHIPS,) + blocks.shape)
    m = shape[-1]
    if axis == len(shape) - 1:
        a2 = _rows2d(full)
        pieces = [a2[:, c:c + FLAT_COLS] for c in range(0, N_CHIPS * m, FLAT_COLS)]
    else:
        layers, r, _ = shape
        assert axis == 1 and (layers * r) % FLAT_ROW_ALIGN == 0
        pieces = [full[:, s * r:(s + 1) * r, c:c + FLAT_COLS].reshape(layers * r, FLAT_COLS)
                  for s in range(N_CHIPS) for c in range(0, m, FLAT_COLS)]
    return jnp.concatenate(pieces, axis=0).reshape(N_CHIPS, -1, FLAT_COLS)


_FLAT_BIG = ("attn_w_in", "conv_w_in", "ple_w_gate", "attn_w_out", "conv_w_out", "ple_w_proj")
_FLAT_SMALL = ("pre_norm_g", "post_norm_g", "conv_dw_w", "conv_dw_b", "conv_ln_g", "conv_ln_b")
PACK_TILE = 1024


_FIRST_LAYER_PARAMS = _FIRST_LAYER + ("pre_norm_g", "post_norm_g")


def _part_shapes(first):
    out = {}
    for n, shape, _ in _LAYOUT:
        layers = (1 if first else shape[0] - 1) if n in _FIRST_LAYER_PARAMS else (0 if first else shape[0])
        if layers and not (first and n == "pre_norm_g"):
            out[n] = (layers,) + shape[1:]
    return out


def _flat_plan(shapes):
    out, off = {}, 0
    for n in _FLAT_BIG + _FLAT_SMALL:
        if n in shapes:
            out[n] = off
            off += _padded_rows(shapes[n]) * (shapes[n][-1] // FLAT_COLS)
    return out, off + (-off) % (2 * FLAT_TILE)


def _unpack_f32(flat, shapes):
    offsets, _ = _flat_plan(shapes)
    return {n: _from_col_blocks(flat, offsets[n], shapes[n])[0] for n in shapes}


def _pack_param(full, shape, axis, off, flat, total_rows):
    layers, r, m = shape
    nblk = m // FLAT_COLS
    if axis == 2:
        rows = layers * r
        tr = math.gcd(math.gcd(rows, PACK_TILE), off) if off else math.gcd(rows, PACK_TILE)
        assert rows % tr == 0 and off % tr == 0 and tr % FLAT_ROW_ALIGN == 0
        src = full.reshape(rows, N_CHIPS * m)
        grid = (N_CHIPS * nblk, rows // tr)
        in_spec = pl.BlockSpec((tr, FLAT_COLS), lambda j, i: (i, j))
        out_spec = pl.BlockSpec((None, tr, FLAT_COLS), lambda j, i: (j // nblk, (off + (j % nblk) * rows) // tr + i, 0))
    else:
        assert axis == 1 and off % r == 0
        src = full.reshape(layers * N_CHIPS * r, m)
        grid = (layers, N_CHIPS, nblk)
        in_spec = pl.BlockSpec((r, FLAT_COLS), lambda l, s, b: (l * N_CHIPS + s, b))
        out_spec = pl.BlockSpec((None, r, FLAT_COLS), lambda l, s, b: (s, (off + b * layers * r) // r + l, 0))

    def copy_body(src_ref, *rest):
        rest[-1][...] = src_ref[...]

    args, in_specs, aliases = [src], [in_spec], {}
    if flat is not None:
        args.append(flat)
        in_specs.append(_ANY)
        aliases = {1: 0}
    return pl.pallas_call(
        copy_body, name="pack_grad", grid=grid, in_specs=in_specs, out_specs=out_spec,
        out_shape=_sds((N_CHIPS, total_rows, FLAT_COLS), F32), input_output_aliases=aliases,
        compiler_params=_params(("arbitrary",) * len(grid)))(*args)


SMALL_ROWS = 40


def _stack_small(w):
    rows = [w["conv_dw_w"]] + [w[n][:, None, :] for n in ("conv_dw_b", "conv_ln_g", "conv_ln_b")]
    stacked = jnp.concatenate(rows, axis=1)
    return jnp.pad(stacked, ((0, 0), (0, SMALL_ROWS - stacked.shape[1]), (0, 0)))


def _unstack_small(small):
    return {"conv_dw_w": small[:, :CONV_WIDTH], "conv_dw_b": small[:, CONV_WIDTH],
            "conv_ln_g": small[:, CONV_WIDTH + 1], "conv_ln_b": small[:, CONV_WIDTH + 2]}


def _pack_full_grads(grads, shapes):
    offsets, total_rows = _flat_plan(shapes)
    flat = None
    for n in _FLAT_BIG:
        if n in shapes:
            flat = _pack_param(grads[n], shapes[n], _AXIS[n], offsets[n], flat, total_rows)
    small_names = [n for n in _FLAT_SMALL if n in shapes]
    small = jnp.concatenate([_shard_col_blocks(grads[n], shapes[n], _AXIS[n]) for n in small_names], axis=1)
    start = offsets[small_names[0]]
    small = jnp.pad(small, ((0, 0), (0, total_rows - start - small.shape[1]), (0, 0)))
    return lax.dynamic_update_slice(flat, small, (0, start, 0))


_ANY = pl.BlockSpec(memory_space=pl.ANY)


def _mesh_pos():
    return lax.axis_index("x"), lax.axis_index("y"), lax.axis_index("c")


def _other_chips(x, y):
    return [(1 - x, y), (x, 1 - y), (1 - x, 1 - y)]


COPIES_PER_ARRAY = 7


def _allgather_weights(shards, axes, small):
    n = len(shards)
    full_shape = lambda a, axis: tuple(d * (N_CHIPS if i == axis else 1) for i, d in enumerate(a.shape))

    def body(*refs):
        ins, small_in = refs[:n], refs[n]
        outs, small_out = refs[n + 1:2 * n + 1], refs[2 * n + 1]
        send_sems, recv_sems = refs[2 * n + 2:]
        x, y, c = _mesh_pos()
        mine, me, sibling = 2 * x + y, (x, y, c), (x, y, 1 - c)
        chips = _other_chips(x, y)

        def region(a, chip, half):
            _, rows, cols = shards[a].shape
            h = rows // 2
            if axes[a] == 2:
                return outs[a].at[:, slice(None) if half is None else pl.ds(half * h, h), pl.ds(chip * cols, cols)]
            if half is None:
                return outs[a].at[:, pl.ds(chip * rows, rows), :]
            return outs[a].at[:, pl.ds(chip * rows + half * h, h), :]

        def copy(k, src, dst, to):
            return pltpu.make_async_remote_copy(src_ref=src, dst_ref=dst, send_sem=send_sems.at[k],
                                                recv_sem=recv_sems.at[k], device_id=to, device_id_type=MESH)

        def arrival(k, dst):
            return copy(k, dst, dst, me)

        sends = []
        for a in range(n):
            h = shards[a].shape[1] // 2
            base = a * COPIES_PER_ARRAY
            sends.append(copy(base + 6, ins[a], region(a, mine, None), sibling))
            for j, (cx, cy) in enumerate(chips):
                sends.append(copy(base + j, ins[a].at[:, pl.ds(c * h, h), :], region(a, mine, c), (cx, cy, c)))
        small_cols = small.shape[2]
        small_region = lambda chip: small_out.at[:, :, pl.ds(chip * small_cols, small_cols)]
        base = n * COPIES_PER_ARRAY
        sends.append(copy(base + 3, small_in, small_region(mine), sibling))
        for j, (cx, cy) in enumerate(chips):
            sends.append(copy(base + j, small_in, small_region(mine), (cx, cy, c)))
        for cp in sends:
            cp.start()
        for j, (cx, cy) in enumerate(chips):
            for a in range(n):
                k = a * COPIES_PER_ARRAY + j
                arrival(k, region(a, 2 * cx + cy, c)).wait_recv()
                passed = copy(k + 3, region(a, 2 * cx + cy, c), region(a, 2 * cx + cy, c), sibling)
                passed.start()
                sends.append(passed)
        for j, (cx, cy) in enumerate(chips):
            for a in range(n):
                arrival(a * COPIES_PER_ARRAY + 3 + j, region(a, 2 * cx + cy, 1 - c)).wait_recv()
            arrival(base + j, small_region(2 * cx + cy)).wait_recv()
        for a in range(n):
            arrival(a * COPIES_PER_ARRAY + 6, region(a, mine, None)).wait_recv()
        arrival(base + 3, small_region(mine)).wait_recv()
        for cp in sends:
            cp.wait_send()

    n_sems = n * COPIES_PER_ARRAY + 4
    out = pl.pallas_call(
        body, name="allgather_weights", in_specs=[_ANY] * (n + 1), out_specs=[_ANY] * (n + 1),
        out_shape=[_sds(full_shape(a, axis), a.dtype) for a, axis in zip(shards, axes)]
        + [_sds(full_shape(small, 2), small.dtype)],
        scratch_shapes=[pltpu.SemaphoreType.DMA((n_sems,)), pltpu.SemaphoreType.DMA((n_sems,))],
    )(*shards, small)
    return out[:n], out[n]


_HBM = pl.BlockSpec(memory_space=pltpu.HBM)
_SEM = pl.BlockSpec(memory_space=pltpu.SEMAPHORE)


def _full_shape(shard, axis):
    return tuple(d * (N_CHIPS if i == axis else 1) for i, d in enumerate(shard.shape))


def _direct_gather_copies(srcs, lands, shapes, axes, send_sems, recv_sems):
    x, y, c = _mesh_pos()
    mine, me, sibling = 2 * x + y, (x, y, c), (x, y, 1 - c)

    def region(a, chip, half):
        _, rows, cols = shapes[a]
        h = rows // 2
        if axes[a] == 2:
            return lands[a].at[:, slice(None) if half is None else pl.ds(half * h, h), pl.ds(chip * cols, cols)]
        if half is None:
            return lands[a].at[:, pl.ds(chip * rows, rows), :]
        return lands[a].at[:, pl.ds(chip * rows + half * h, h), :]

    def copy(k, src, dst, to):
        return pltpu.make_async_remote_copy(src_ref=src, dst_ref=dst, send_sem=send_sems.at[k],
                                            recv_sem=recv_sems.at[k], device_id=to, device_id_type=MESH)

    outgoing, incoming = [], []
    for a in range(len(srcs)):
        h = shapes[a][1] // 2
        base = a * COPIES_PER_ARRAY
        for j, (cx, cy) in enumerate(_other_chips(x, y)):
            for t, ct in enumerate((c, 1 - c)):
                k = base + 2 * j + t
                outgoing.append(copy(k, srcs[a].at[:, pl.ds(c * h, h), :], region(a, mine, c), (cx, cy, ct)))
                landed = region(a, 2 * cx + cy, ct)
                incoming.append(copy(k, landed, landed, me))
        outgoing.append(copy(base + 6, srcs[a], region(a, mine, None), sibling))
        incoming.append(copy(base + 6, region(a, mine, None), region(a, mine, None), me))
    return outgoing, incoming


def _allgather_start(shards, axes):
    n = len(shards)
    shapes = [s.shape for s in shards]

    def body(*refs):
        srcs, lands = refs[:n], refs[n:2 * n]
        send_sems, recv_sems = refs[2 * n], refs[2 * n + 1]
        token = refs[-1]
        outgoing, _ = _direct_gather_copies(srcs, lands, shapes, axes, send_sems, recv_sems)
        for cp in outgoing:
            cp.start()
        token[...] = jnp.zeros_like(token)

    n_sems = n * COPIES_PER_ARRAY
    zones = [pltpu.with_memory_space_constraint(lax.empty(_full_shape(s, ax), s.dtype), pltpu.HBM)
             for s, ax in zip(shards, axes)]
    out = pl.pallas_call(
        body, name="allgather_rest_start",
        out_shape=(pltpu.SemaphoreType.DMA((n_sems,)), pltpu.SemaphoreType.DMA((n_sems,)),
                   *[pltpu.HBM(s.shape, s.dtype) for s in shards], *[pltpu.HBM(z.shape, z.dtype) for z in zones],
                   jax.ShapeDtypeStruct((8, LANES), F32)),
        in_specs=[_HBM] * (2 * n),
        out_specs=(_SEM, _SEM, *[_HBM] * (2 * n), pl.BlockSpec(memory_space=pltpu.VMEM)),
        input_output_aliases={i: 2 + i for i in range(2 * n)},
        compiler_params=pltpu.CompilerParams(has_side_effects=pltpu.SideEffectType.DATAFLOW_SIDE_EFFECTING),
    )(*[pltpu.with_memory_space_constraint(s, pltpu.HBM) for s in shards], *zones)
    return out[0], out[1], out[2:2 + n], out[2 + n:2 + 2 * n], out[-1]


def _allgather_wait(send_sems, recv_sems, shards, zones, axes, after):
    n = len(shards)
    shapes = [s.shape for s in shards]

    def body(*refs):
        srcs, lands = refs[:n], refs[n:2 * n]
        outgoing, incoming = _direct_gather_copies(srcs, lands, shapes, axes, refs[2 * n], refs[2 * n + 1])
        for cp in outgoing:
            cp.wait_send()
        for cp in incoming:
            cp.wait_recv()

    out = pl.pallas_call(
        body, name="allgather_rest_wait",
        out_shape=(*[pltpu.HBM(s.shape, s.dtype) for s in shards], *[pltpu.HBM(z.shape, z.dtype) for z in zones]),
        in_specs=[_HBM] * (2 * n) + [_SEM, _SEM, _ANY], out_specs=[_HBM] * (2 * n),
        input_output_aliases={i: i for i in range(2 * n)},
        compiler_params=pltpu.CompilerParams(has_side_effects=pltpu.SideEffectType.DATAFLOW_SIDE_EFFECTING),
    )(*shards, *zones, send_sems, recv_sems, after)
    return out[n:]


def _exchange_core_halves(g):
    n, _, H, C = g.shape

    def body(g_ref, got_ref, send_sem, recv_sem):
        x, y, c = _mesh_pos()
        swap = pltpu.make_async_remote_copy(
            src_ref=g_ref.at[pl.ds(0, n), 1 - c], dst_ref=got_ref, send_sem=send_sem, recv_sem=recv_sem,
            device_id=(x, y, 1 - c), device_id_type=MESH)
        swap.start()
        swap.wait()

    return pl.pallas_call(
        body, name="exchange_core_halves", in_specs=[_ANY], out_specs=_ANY,
        out_shape=_sds((n, H, C), g.dtype),
        scratch_shapes=[pltpu.SemaphoreType.DMA, pltpu.SemaphoreType.DMA],
    )(g)


def _exchange_copy(g_ref, got_ref, send_sems, recv_sems):
    x, y, c = _mesh_pos()
    return pltpu.make_async_remote_copy(
        src_ref=g_ref.at[pl.ds(0, g_ref.shape[0]), 1 - c], dst_ref=got_ref, send_sem=send_sems.at[0],
        recv_sem=recv_sems.at[0], device_id=(x, y, 1 - c), device_id_type=MESH)


def _exchange_start(g):
    n, _, H, C = g.shape

    def body(g_ref, got_ref, send_sems, recv_sems, g_thru, got_thru, token):
        _exchange_copy(g_ref, got_ref, send_sems, recv_sems).start()
        token[...] = jnp.zeros_like(token)

    zone = pltpu.with_memory_space_constraint(lax.empty((n, H, C), g.dtype), pltpu.HBM)
    return pl.pallas_call(
        body, name="exchange_rest_start",
        out_shape=(pltpu.SemaphoreType.DMA((1,)), pltpu.SemaphoreType.DMA((1,)), pltpu.HBM(g.shape, g.dtype),
                   pltpu.HBM(zone.shape, zone.dtype), jax.ShapeDtypeStruct((8, LANES), F32)),
        in_specs=[_HBM, _HBM], out_specs=(_SEM, _SEM, _HBM, _HBM, pl.BlockSpec(memory_space=pltpu.VMEM)),
        input_output_aliases={0: 2, 1: 3},
        compiler_params=pltpu.CompilerParams(has_side_effects=pltpu.SideEffectType.DATAFLOW_SIDE_EFFECTING),
    )(pltpu.with_memory_space_constraint(g, pltpu.HBM), zone)


def _exchange_wait(send_sems, recv_sems, g, zone, after):
    def body(g_ref, got_ref, send_sems, recv_sems, after_ref, g_out, got_out):
        copy = _exchange_copy(g_ref, got_ref, send_sems, recv_sems)
        copy.wait_send()
        copy.wait_recv()

    return pl.pallas_call(
        body, name="exchange_rest_wait",
        out_shape=(pltpu.HBM(g.shape, g.dtype), pltpu.HBM(zone.shape, zone.dtype)),
        in_specs=[_HBM, _HBM, _SEM, _SEM, _ANY], out_specs=[_HBM, _HBM], input_output_aliases={0: 0, 1: 1},
        compiler_params=pltpu.CompilerParams(has_side_effects=pltpu.SideEffectType.DATAFLOW_SIDE_EFFECTING),
    )(g, zone, send_sems, recv_sems, after)


def _scatter_copies(p_ref, q_ref, send_sems, recv_sems):
    x, y, c = _mesh_pos()
    return [pltpu.make_async_remote_copy(
        src_ref=p_ref.at[2 * cx + cy], dst_ref=q_ref.at[j], send_sem=send_sems.at[j],
        recv_sem=recv_sems.at[j], device_id=(cx, cy, c), device_id_type=MESH)
        for j, (cx, cy) in enumerate(_other_chips(x, y))]


def _scatter_start(p, part):
    n, H, C = p.shape

    def body(p_ref, q_ref, send_sems, recv_sems, p_thru, q_thru, token):
        for cp in _scatter_copies(p_ref, q_ref, send_sems, recv_sems):
            cp.start()
        token[...] = jnp.zeros_like(token)

    zone = pltpu.with_memory_space_constraint(lax.empty((n - 1, H, C), p.dtype), pltpu.HBM)
    return pl.pallas_call(
        body, name="scatter_%s_start" % part,
        out_shape=(pltpu.SemaphoreType.DMA((n - 1,)), pltpu.SemaphoreType.DMA((n - 1,)), pltpu.HBM(p.shape, p.dtype),
                   pltpu.HBM(zone.shape, zone.dtype), jax.ShapeDtypeStruct((8, LANES), F32)),
        in_specs=[_HBM, _HBM], out_specs=(_SEM, _SEM, _HBM, _HBM, pl.BlockSpec(memory_space=pltpu.VMEM)),
        input_output_aliases={0: 2, 1: 3},
        compiler_params=pltpu.CompilerParams(has_side_effects=pltpu.SideEffectType.DATAFLOW_SIDE_EFFECTING),
    )(pltpu.with_memory_space_constraint(p, pltpu.HBM), zone)


def _scatter_wait(send_sems, recv_sems, p, zone, after, part):
    def body(p_ref, q_ref, send_sems, recv_sems, after_ref, p_out, q_out):
        copies = _scatter_copies(p_ref, q_ref, send_sems, recv_sems)
        for cp in copies:
            cp.wait_send()
        for cp in copies:
            cp.wait_recv()

    return pl.pallas_call(
        body, name="scatter_%s_wait" % part,
        out_shape=(pltpu.HBM(p.shape, p.dtype), pltpu.HBM(zone.shape, zone.dtype)),
        in_specs=[_HBM, _HBM, _SEM, _SEM, _ANY], out_specs=[_HBM, _HBM], input_output_aliases={0: 0, 1: 1},
        compiler_params=pltpu.CompilerParams(has_side_effects=pltpu.SideEffectType.DATAFLOW_SIDE_EFFECTING),
    )(p, zone, send_sems, recv_sems, after)


N_DEVICES = 8


def _allreduce_row(v):
    C = v.shape[1]

    def body(v_ref, out_ref, rows, send_sems, recv_sems):
        x, y, c = _mesh_pos()
        me = 4 * x + 2 * y + c
        rows[pl.ds(me, 1)] = v_ref[...].reshape(1, 1, C)
        copies = []
        for m in range(1, N_DEVICES):
            peer = (x ^ (m >> 2), y ^ ((m >> 1) & 1), c ^ (m & 1))
            copies.append(pltpu.make_async_remote_copy(
                src_ref=rows.at[me], dst_ref=rows.at[me], send_sem=send_sems.at[m - 1],
                recv_sem=recv_sems.at[m - 1], device_id=peer, device_id_type=MESH))
        for cp in copies:
            cp.start()
        for m, cp in enumerate(copies, start=1):
            cp.wait_send()
            pltpu.make_async_remote_copy(
                src_ref=rows.at[me], dst_ref=rows.at[me ^ m], send_sem=send_sems.at[m - 1],
                recv_sem=recv_sems.at[m - 1], device_id=(x, y, c), device_id_type=MESH).wait_recv()
        total = rows[0]
        for d in range(1, N_DEVICES):
            total = total + rows[d]
        out_ref[...] = total

    vmem = pl.BlockSpec(memory_space=pltpu.VMEM)
    return pl.pallas_call(
        body, name="allreduce_row", in_specs=[vmem], out_specs=vmem, out_shape=_sds((1, C), F32),
        scratch_shapes=[pltpu.VMEM((N_DEVICES, 1, C), F32), pltpu.SemaphoreType.DMA((N_DEVICES - 1,)),
                        pltpu.SemaphoreType.DMA((N_DEVICES - 1,))],
    )(v)


def _share_core_halves(r2):
    _, H, C = r2.shape

    def body(r_ref, out_ref, send_sem, recv_sem):
        x, y, c = _mesh_pos()
        send = pltpu.make_async_remote_copy(
            src_ref=r_ref.at[c], dst_ref=out_ref.at[c], send_sem=send_sem, recv_sem=recv_sem,
            device_id=(x, y, 1 - c), device_id_type=MESH)
        send.start()
        send.wait_send()
        pltpu.make_async_remote_copy(
            src_ref=r_ref.at[c], dst_ref=out_ref.at[1 - c], send_sem=send_sem, recv_sem=recv_sem,
            device_id=(x, y, 1 - c), device_id_type=MESH).wait_recv()

    return pl.pallas_call(
        body, name="share_core_halves", in_specs=[_ANY], out_specs=_ANY,
        out_shape=_sds(r2.shape, r2.dtype), input_output_aliases={0: 0},
        scratch_shapes=[pltpu.SemaphoreType.DMA, pltpu.SemaphoreType.DMA],
    )(r2)


def _place():
    x, y, c = _mesh_pos()
    return jnp.stack([c, 2 * x + y]).astype(jnp.int32)


def _sum_pair(g, got, place):
    n, _, H, C = g.shape

    def body(place_ref, a_ref, b_ref, o_ref):
        o_ref[...] = (a_ref[...] + b_ref[...]).astype(BF16)

    spec = pl.BlockSpec((1, FLAT_TILE, C), lambda s, i, pr: (s, i, 0))
    return pl.pallas_call(
        body, name="sum_core_pair",
        grid_spec=pltpu.PrefetchScalarGridSpec(
            num_scalar_prefetch=1, grid=(n, H // FLAT_TILE),
            in_specs=[pl.BlockSpec((1, None, FLAT_TILE, C), lambda s, i, pr: (s, pr[0], i, 0)), spec],
            out_specs=spec),
        out_shape=_sds((n, H, C), BF16),
        compiler_params=_params(("parallel", "parallel")))(place, g, got)


def _sum_chips(p, q, place):
    n, H, C = p.shape

    def body(place_ref, own_ref, qx_ref, qy_ref, qxy_ref, o_ref):
        mine = place_ref[1]
        own, qx, qy, qxy = (t[0].astype(F32) for t in (own_ref, qx_ref, qy_ref, qxy_ref))

        def term(s):
            rel = jnp.full(own.shape, mine ^ s, jnp.int32)
            return jnp.where(rel == 0, own, jnp.where(rel == 2, qx, jnp.where(rel == 1, qy, qxy)))

        o_ref[0] = ((term(0) + term(1)) + term(2)) + term(3)

    qspec = lambda j: pl.BlockSpec((1, FLAT_TILE, C), lambda i, pr: (j, i, 0))
    return pl.pallas_call(
        body, name="sum_chips",
        grid_spec=pltpu.PrefetchScalarGridSpec(
            num_scalar_prefetch=1, grid=(H // FLAT_TILE,),
            in_specs=[pl.BlockSpec((1, FLAT_TILE, C), lambda i, pr: (pr[1], i, 0)), qspec(0), qspec(1), qspec(2)],
            out_specs=pl.BlockSpec((1, FLAT_TILE, C), lambda i, pr: (pr[0], i, 0))),
        out_shape=_sds((2, H, C), F32),
        compiler_params=_params(("parallel",)))(place, p, q, q, q)


ADAMW_BLOCK = 1 << 18


def _adamw(w, g, m, v):
    shape = w.shape
    C = shape[-1]
    R = _size(shape) // C
    tr = R
    while tr * C > ADAMW_BLOCK and tr % 16 == 0:
        tr //= 2
    w, g, m, v = (t.reshape(R, C) for t in (w, g, m, v))

    def body(w_ref, g_ref, m_ref, v_ref, d_ref, nm_ref, nv_ref):
        gv = g_ref[...]
        nm = ADAM_B1 * m_ref[...] + (1.0 - ADAM_B1) * gv
        nv = ADAM_B2 * v_ref[...] + (1.0 - ADAM_B2) * (gv * gv)
        m_hat = nm / (1.0 - ADAM_B1 ** ADAM_STEP)
        v_hat = nv / (1.0 - ADAM_B2 ** ADAM_STEP)
        d_ref[...] = -ADAM_LR * (m_hat / (jnp.sqrt(v_hat) + ADAM_EPS) + ADAM_WD * w_ref[...])
        nm_ref[...] = nm
        nv_ref[...] = nv

    spec = pl.BlockSpec((tr, C), lambda i: (i, 0))
    outs = pl.pallas_call(body, name="adamw", grid=(R // tr,), in_specs=[spec] * 4, out_specs=[spec] * 3,
                          out_shape=[_sds((R, C), F32)] * 3, compiler_params=_params(("parallel",)))(w, g, m, v)
    return tuple(t.reshape(shape) for t in outs)


def _chip_partials(gfull, place):
    n, R, C = gfull.shape
    g4 = gfull.reshape(n, 2, R // 2, C)
    return _sum_pair(g4, _exchange_core_halves(g4), place)


def _finish_reduce(p, q, place):
    r2 = _share_core_halves(_sum_chips(p, q, place))
    return r2.reshape(2 * r2.shape[1], r2.shape[2])


def _local_step(x, p, positions, loss_target, pre_g, post_g, w, later_weights, later_grads, later_grads_arrived,
                first_grads):
    Bl, S, _ = x.shape
    T = Bl * S
    cos, sin = _rope_tables(positions)
    xs = x.reshape(T, D_MODEL)
    saved = []
    for i in range(DEPTH):
        if i == 1:
            later_weights(xs)
        j = i // 2
        g_pre, g_post = pre_g[i:i + 1], post_g[i:i + 1]
        h = _rmsnorm_fwd(xs, g_pre)
        st = {"x": xs, "h": h}
        if i % 2 == 0:
            proj = _mm(h, w["attn_w_in"][j], name="attn_in")
            res = [_attn_fwd(proj, cos, sin, g, Bl, S) for g in range(N_GROUPS)]
            a, o, lse = _attn_combine([r[0] for r in res], [r[1] for r in res], proj)
            w_out = w["attn_w_out"][j]
            st.update(proj=proj, a=a, o=o, lse=lse, qkv=[r[2] for r in res])
        else:
            w_ab, w_z = w["conv_w_in"][j][:, :2 * D_MODEL], w["conv_w_in"][j][:, 2 * D_MODEL:]
            ab = _mm(h, w_ab, out_dtype=BF16, name="conv_in_ab")
            z = _mm(h, w_z, out_dtype=BF16, name="conv_in_z")
            dw = jnp.pad(w["conv_dw_w"][j], ((0, 1), (0, 0)))
            u1, a = _conv_fwd(ab, z, dw, w["conv_dw_b"][j:j + 1], w["conv_ln_g"][j:j + 1],
                              w["conv_ln_b"][j:j + 1], Bl, S)
            w_out = w["conv_w_out"][j]
            st.update(w_ab=w_ab, w_z=w_z, ab=ab, z=z, dw=dw, u1=u1, a=a)
        y, x1 = _mm_rows(a, w_out, [xs, g_post], _post_epilogue, (F32, F32), "branch_out_post")
        pi = p[i].reshape(T, PLE_DIM)
        pe = _mm(pi, w["ple_w_proj"][i], name="ple_proj")
        gl, xs = _mm_rows(x1, w["ple_w_gate"][i], [pe], _ple_epilogue, (F32, F32), "ple_gate_fwd")
        st.update(y=y, x1=x1, pi=pi, pe=pe, gl=gl)
        saved.append(st)

    sq, dx = _loss_fwd_bwd(xs, loss_target.reshape(T, D_MODEL))

    grads = {n: [None] * shape[0] for n, shape, _ in _LAYOUT}
    for i in reversed(range(DEPTH)):
        j = i // 2
        st = saved[i]
        g_pre, g_post = pre_g[i:i + 1], post_g[i:i + 1]
        if i == 0:
            rest = {n: jnp.stack(v[1:] if n in _FIRST_LAYER_PARAMS else v) for n, v in grads.items()}
            g_post = g_post + later_grads(rest)[0, 0]
        dpe, dgl, dx1, dy, dg_post = _ple_post_bwd(dx, st["pe"], st["gl"], w["ple_w_gate"][i], st["y"], g_post)
        grads["ple_w_proj"][i] = _mm(st["pi"], dpe, ta=True, name="ple_proj_wgrad")
        grads["ple_w_gate"][i] = _mm(st["x1"], dgl, ta=True, name="ple_gate_wgrad")
        grads["post_norm_g"][i] = dg_post[0]
        if i % 2 == 0:
            grads["attn_w_out"][j] = _mm(st["a"], dy, ta=True, name="attn_out_wgrad")
            da = _mm(dy, w["attn_w_out"][j], tb=True, name="attn_out_dgrad")
            do, dproj = _gate_bwd(da, st["o"], st["proj"])
            cos_b = cos + later_grads_arrived(do)[0, 0] if i == 0 else cos
            for g in range(N_GROUPS):
                dproj = _attn_bwd(st["qkv"][g], cos_b, sin, do, st["o"], st["lse"], dproj, g, Bl, S)
            grads["attn_w_in"][j] = _mm(st["h"], dproj, ta=True, name="attn_in_wgrad")
            if i == 0:
                first = {n: jnp.stack(grads[n][:1]) for n in _FIRST_LAYER_PARAMS if n != "pre_norm_g"}
                g_pre = g_pre + first_grads(first)[0, 0]
            dx, dg_pre = _dgrad_pre_bwd(dproj, w["attn_w_in"][j], st["x"], g_pre, dx1, None, 2048, "attn_in_dgrad_pre")
        else:
            grads["conv_w_out"][j] = _mm(st["a"], dy, ta=True, name="conv_out_wgrad")
            da2 = _mm(dy, w["conv_w_out"][j], tb=True, name="conv_out_dgrad")
            du1, dz, dln_g, dln_b = _conv_norm_bwd(da2, st["z"], st["u1"], w["conv_ln_g"][j:j + 1],
                                                   w["conv_ln_b"][j:j + 1])
            dab, ddw, ddb = _conv_bwd(st["ab"], du1, st["dw"], Bl, S)
            dh = _mm(dz, st["w_z"], tb=True, name="conv_in_z_dgrad")
            dx, dg_pre = _dgrad_pre_bwd(dab, st["w_ab"], st["x"], g_pre, dx1, dh, 2048, "conv_in_dgrad_pre")
            dw_ab = _mm(st["h"], dab, ta=True, name="conv_in_ab_wgrad")
            dw_z = _mm(st["h"], dz, ta=True, name="conv_in_z_wgrad")
            grads["conv_w_in"][j] = jnp.concatenate([dw_ab, dw_z], axis=1)
            grads["conv_dw_w"][j] = ddw[:CONV_WIDTH]
            grads["conv_dw_b"][j] = ddb[0]
            grads["conv_ln_g"][j] = dln_g[0]
            grads["conv_ln_b"][j] = dln_b[0]
        grads["pre_norm_g"][i] = dg_pre[0]
    return sq, dx.reshape(Bl, S, D_MODEL), grads["pre_norm_g"][0][None]


_NAMES = tuple(n for n, _, _ in _LAYOUT)


def kernel(x, p, positions, pre_norm_g, post_norm_g, attn_w_in, attn_w_out, conv_w_in, conv_dw_w, conv_dw_b, conv_ln_g, conv_ln_b, conv_w_out, ple_w_proj, ple_w_gate, loss_target, m_pre_norm_g, m_post_norm_g, m_attn_w_in, m_attn_w_out, m_conv_w_in, m_conv_dw_w, m_conv_dw_b, m_conv_ln_g, m_conv_ln_b, m_conv_w_out, m_ple_w_proj, m_ple_w_gate, v_pre_norm_g, v_post_norm_g, v_attn_w_in, v_attn_w_out, v_conv_w_in, v_conv_dw_w, v_conv_dw_b, v_conv_ln_g, v_conv_ln_b, v_conv_w_out, v_ple_w_proj, v_ple_w_gate):
    w_loc = dict(zip(_NAMES, (pre_norm_g, post_norm_g, attn_w_in, attn_w_out, conv_w_in, conv_dw_w, conv_dw_b,
                              conv_ln_g, conv_ln_b, conv_w_out, ple_w_proj, ple_w_gate)))
    m_loc = dict(zip(_NAMES, (m_pre_norm_g, m_post_norm_g, m_attn_w_in, m_attn_w_out, m_conv_w_in, m_conv_dw_w,
                              m_conv_dw_b, m_conv_ln_g, m_conv_ln_b, m_conv_w_out, m_ple_w_proj, m_ple_w_gate)))
    v_loc = dict(zip(_NAMES, (v_pre_norm_g, v_post_norm_g, v_attn_w_in, v_attn_w_out, v_conv_w_in, v_conv_dw_w,
                              v_conv_dw_b, v_conv_ln_g, v_conv_ln_b, v_conv_w_out, v_ple_w_proj, v_ple_w_gate)))

    bf = {n: w_loc[n].astype(BF16) for n in _MATMUL_WEIGHTS}
    axes = [_AXIS[n] for n in _MATMUL_WEIGHTS]
    first, small = _allgather_weights([bf[n][:1] for n in _FIRST_LAYER], [_AXIS[n] for n in _FIRST_LAYER],
                                      _stack_small(w_loc))
    send_sems, recv_sems, shards, zones, token = _allgather_start(
        [bf[n][1:] if n in _FIRST_LAYER else bf[n] for n in _MATMUL_WEIGHTS], axes)
    w_full = dict({n: [full[0]] for n, full in zip(_FIRST_LAYER, first)}, **_unstack_small(small))

    def later_weights(after):
        for n, full in zip(_MATMUL_WEIGHTS, _allgather_wait(send_sems, recv_sems, shards, zones, axes, after)):
            w_full[n] = w_full[n] + [full[l] for l in range(full.shape[0])] if n in _FIRST_LAYER else full

    place = _place()
    rest_shapes, first_shapes = _part_shapes(False), _part_shapes(True)
    rest_halves, rest_flight, first_flight = [], [], []

    def later_grads(grads):
        gfull = _pack_full_grads(grads, rest_shapes)
        rest_halves.extend(_exchange_start(gfull.reshape(N_CHIPS, 2, gfull.shape[1] // 2, FLAT_COLS)))
        return rest_halves[4]

    def later_grads_arrived(after):
        rest_flight.extend(_scatter_start(_sum_pair(*_exchange_wait(*rest_halves[:4], after), place), "rest"))
        return rest_flight[4]

    def first_grads(grads):
        first_flight.extend(_scatter_start(_chip_partials(_pack_full_grads(grads, first_shapes), place), "first"))
        return first_flight[4]

    sq, grad_x, dg_pre0 = _local_step(x, p, positions, loss_target, pre_norm_g + token[0, 0], post_norm_g,
                                      w_full, later_weights, later_grads, later_grads_arrived, first_grads)
    loss = lax.psum(sq[0, 0] * (0.5 / D_MODEL), ("x", "y", "c"))

    p_rest, q_rest = _scatter_wait(*rest_flight[:4], grad_x, "rest")
    g_rest = _unpack_f32(_finish_reduce(p_rest, q_rest, place), rest_shapes)
    p_first, q_first = _scatter_wait(*first_flight[:4], grad_x, "first")
    g_first = _unpack_f32(_finish_reduce(p_first, q_first, place), first_shapes)
    g_first["pre_norm_g"] = _allreduce_row(dg_pre0)
    g_out = {n: jnp.concatenate([g_first[n], g_rest[n]]) if n in g_first and n in g_rest
             else g_rest.get(n, g_first.get(n)) for n in _NAMES}
    updates = {n: _adamw(w_loc[n], g_out[n], m_loc[n], v_loc[n]) for n in _NAMES}
    d_out, m_out, v_out = ({n: updates[n][k] for n in _NAMES} for k in range(3))
    return (loss, grad_x, *[g_out[n] for n in _NAMES], *[d_out[n] for n in _NAMES],
            *[m_out[n] for n in _NAMES], *[v_out[n] for n in _NAMES])
```

```python
import math

import jax
import jax.numpy as jnp
from jax import lax
from jax.experimental import pallas as pl
from jax.experimental.pallas import tpu as pltpu

F32 = jnp.float32
BF16 = jnp.bfloat16

D_MODEL = 1024
DEPTH = 4
PLE_DIM = 256
HEAD_DIM = 64
WIN_DIL = ((128, 1), (512, 4), (2048, 16))
N_GROUPS = 3
N_BACK = 128
BLOCK_UNROLL = 8
RESIDUES_TOGETHER = 2
ROPE_THETA = 10000.0
CONV_WIDTH = 31
CONV_HALO = 32
RMS_EPS = 1e-6
LN_EPS = 1e-5
NEG_INF = -1e30
ADAM_LR, ADAM_B1, ADAM_B2, ADAM_EPS, ADAM_WD, ADAM_STEP = 0.001, 0.9, 0.999, 1e-08, 0.01, 10

LANES = 128
N_CHIPS = 4
VMEM_LIMIT = 48 * 1024 * 1024
VMEM_LIMIT_ATTN = 56 * 1024 * 1024
FLAT_COLS = 256
FLAT_TILE = 2048
FLAT_ROW_ALIGN = 16
PROJ_COLS = (3 * N_GROUPS + 1) * D_MODEL
HEAD_PAIRS = D_MODEL // LANES

MESH = pl.DeviceIdType.MESH


def _params(sem=None, vmem=VMEM_LIMIT):
    return pltpu.CompilerParams(dimension_semantics=sem, vmem_limit_bytes=vmem)


def _sigmoid(v):
    return 1.0 / (1.0 + jnp.exp(-v))


def _mm(a, b, *, ta=False, tb=False, out_dtype=F32, tm=1024, tn=1024, tk=1024, name="mm"):
    if ta:
        K, M = a.shape
    else:
        M, K = a.shape
    if tb:
        N, K2 = b.shape
    else:
        K2, N = b.shape
    assert K == K2, (a.shape, b.shape)
    tm, tn, tk = min(tm, M), min(tn, N), min(tk, K)
    assert M % tm == 0 and N % tn == 0 and K % tk == 0
    nk = K // tk
    dims = (((0 if ta else 1,), (1 if tb else 0,)), ((), ()))

    def body(a_ref, b_ref, o_ref, *scratch):
        k = pl.program_id(2)
        part = lax.dot_general(a_ref[...].astype(BF16), b_ref[...].astype(BF16), dims, preferred_element_type=F32)
        if nk == 1:
            o_ref[...] = part.astype(out_dtype)
        else:
            acc_ref, = scratch

            @pl.when(k == 0)
            def _():
                acc_ref[...] = part

            @pl.when((k > 0) & (k < nk - 1))
            def _():
                acc_ref[...] += part

            @pl.when(k == nk - 1)
            def _():
                o_ref[...] = (acc_ref[...] + part).astype(out_dtype)

    a_spec = pl.BlockSpec((tk, tm), lambda i, j, k: (k, i)) if ta else pl.BlockSpec((tm, tk), lambda i, j, k: (i, k))
    b_spec = pl.BlockSpec((tn, tk), lambda i, j, k: (j, k)) if tb else pl.BlockSpec((tk, tn), lambda i, j, k: (k, j))
    return pl.pallas_call(
        body, name=name, grid=(M // tm, N // tn, nk),
        in_specs=[a_spec, b_spec], out_specs=pl.BlockSpec((tm, tn), lambda i, j, k: (i, j)),
        out_shape=jax.ShapeDtypeStruct((M, N), out_dtype),
        scratch_shapes=[pltpu.VMEM((tm, tn), F32)] if nk > 1 else [],
        compiler_params=_params(("parallel", "parallel", "arbitrary")),
    )(a, b)


def _mm_rows(a, b, extras, epilogue, out_dtypes, name, tm=512):
    M, K = a.shape
    N = b.shape[1]
    n_ex = len(extras)

    def body(*refs):
        a_ref, b_ref = refs[:2]
        av = a_ref[...]
        acc = jnp.dot(av.astype(BF16), b_ref[...].astype(BF16), preferred_element_type=F32)
        results = epilogue(acc, av, *[e[...] for e in refs[2:2 + n_ex]])
        for o_ref, r in zip(refs[2 + n_ex:], results):
            o_ref[...] = r.astype(o_ref.dtype)

    tile = pl.BlockSpec((tm, N), lambda i: (i, 0))
    in_specs = [pl.BlockSpec((tm, K), lambda i: (i, 0)), pl.BlockSpec((K, N), lambda i: (0, 0))]
    in_specs += [tile if e.shape[0] == M else pl.BlockSpec((1, N), lambda i: (0, 0)) for e in extras]
    return pl.pallas_call(
        body, name=name, grid=(M // tm,), in_specs=in_specs, out_specs=[tile] * len(out_dtypes),
        out_shape=[jax.ShapeDtypeStruct((M, N), dt) for dt in out_dtypes],
        compiler_params=_params(("parallel",)),
    )(a, b, *extras)


ROW_TILE = 512


def _rows(w=D_MODEL, cb=0, tr=ROW_TILE):
    return pl.BlockSpec((tr, w), lambda i: (i, cb))


def _full(shape):
    return pl.BlockSpec(shape, lambda i: (0,) * len(shape))


def _row_call(body, name, T, in_specs, out_specs, out_shape, args, tr=ROW_TILE):
    return pl.pallas_call(body, name=name, grid=(T // tr,), in_specs=in_specs, out_specs=out_specs,
                          out_shape=out_shape, compiler_params=_params(("arbitrary",)))(*args)


def _sds(shape, dtype):
    return jax.ShapeDtypeStruct(shape, dtype)


def _rmsnorm_fwd(x, g):
    T = x.shape[0]

    def body(x_ref, g_ref, h_ref):
        xv = x_ref[...]
        r = lax.rsqrt(jnp.mean(xv * xv, axis=1, keepdims=True) + RMS_EPS)
        h_ref[...] = (xv * r * g_ref[...]).astype(BF16)

    return _row_call(body, "rmsnorm_fwd", T, [_rows(), _full((1, D_MODEL))], _rows(),
                     _sds((T, D_MODEL), BF16), (x, g))


def _post_epilogue(y, a_tile, x, g):
    del a_tile
    return y, x + y * lax.rsqrt(jnp.mean(y * y, axis=1, keepdims=True) + RMS_EPS) * g


def _ple_epilogue(gl, x1, pe):
    return gl, x1 + pe * _sigmoid(gl)


def _dgrad_pre_bwd(a, b, x, g, dx1, add, tk, name, tm=512):
    M, K = a.shape
    N = b.shape[0]
    tk = min(tk, K)
    assert M % tm == 0 and K % tk == 0 and N == D_MODEL
    nk = K // tk

    def body(*refs):
        a_ref, b_ref, x_ref, g_ref, dx1_ref = refs[:5]
        add_ref = refs[5] if add is not None else None
        dx_ref, dg_ref = refs[-3:-1] if nk > 1 else refs[-2:]
        i, k = pl.program_id(0), pl.program_id(1)

        @pl.when((i == 0) & (k == 0))
        def _():
            dg_ref[...] = jnp.zeros_like(dg_ref)

        part = lax.dot_general(a_ref[...].astype(BF16), b_ref[...].astype(BF16), _NT, preferred_element_type=F32)

        def finish(dh):
            if add is not None:
                dh = dh + add_ref[...]
            xv = x_ref[...]
            r = lax.rsqrt(jnp.mean(xv * xv, axis=1, keepdims=True) + RMS_EPS)
            xh = xv * r
            dg_ref[...] += jnp.sum(dh * xh, axis=0, keepdims=True)
            dn = dh * g_ref[...]
            dx_ref[...] = dx1_ref[...] + r * (dn - xh * jnp.mean(dn * xh, axis=1, keepdims=True))

        if nk == 1:
            finish(part)
        else:
            acc_ref = refs[-1]

            @pl.when(k == 0)
            def _():
                acc_ref[...] = part

            @pl.when((k > 0) & (k < nk - 1))
            def _():
                acc_ref[...] += part

            @pl.when(k == nk - 1)
            def _():
                finish(acc_ref[...] + part)

    tile = pl.BlockSpec((tm, N), lambda i, k: (i, 0))
    row = pl.BlockSpec((1, N), lambda i, k: (0, 0))
    in_specs = [pl.BlockSpec((tm, tk), lambda i, k: (i, k)), pl.BlockSpec((N, tk), lambda i, k: (0, k)), tile, row, tile]
    args = [a, b, x, g, dx1]
    if add is not None:
        in_specs.append(tile)
        args.append(add)
    return pl.pallas_call(
        body, name=name, grid=(M // tm, nk), in_specs=in_specs, out_specs=[tile, row],
        out_shape=[_sds((M, N), F32), _sds((1, N), F32)],
        scratch_shapes=[pltpu.VMEM((tm, N), F32)] if nk > 1 else [],
        compiler_params=_params(("arbitrary", "arbitrary"), VMEM_LIMIT_ATTN),
    )(*args)


def _ple_post_bwd(dx2, pe, gl, w_gate, y, g_post):
    T = dx2.shape[0]

    def body(d_ref, pe_ref, gl_ref, w_ref, y_ref, g_ref, dpe_ref, dgl_ref, dx1_ref, dy_ref, dg_ref):
        @pl.when(pl.program_id(0) == 0)
        def _():
            dg_ref[...] = jnp.zeros_like(dg_ref)

        dv = d_ref[...]
        sg = _sigmoid(gl_ref[...])
        dpe_ref[...] = (dv * sg).astype(BF16)
        dgl = (dv * pe_ref[...] * sg * (1.0 - sg)).astype(BF16)
        dgl_ref[...] = dgl
        dx1 = dv + lax.dot_general(dgl, w_ref[...], _NT, preferred_element_type=F32)
        dx1_ref[...] = dx1
        yv = y_ref[...]
        r = lax.rsqrt(jnp.mean(yv * yv, axis=1, keepdims=True) + RMS_EPS)
        yh = yv * r
        dg_ref[...] += jnp.sum(dx1 * yh, axis=0, keepdims=True)
        dn = dx1 * g_ref[...]
        dy_ref[...] = (r * (dn - yh * jnp.mean(dn * yh, axis=1, keepdims=True))).astype(BF16)

    row = _full((1, D_MODEL))
    return _row_call(body, "ple_post_bwd", T, [_rows()] * 3 + [_full((D_MODEL, D_MODEL)), _rows(), row],
                     [_rows()] * 4 + [row],
                     [_sds((T, D_MODEL), BF16)] * 2 + [_sds((T, D_MODEL), F32), _sds((T, D_MODEL), BF16),
                                                      _sds((1, D_MODEL), F32)],
                     (dx2, pe, gl, w_gate, y, g_post))


def _loss_fwd_bwd(y, target):
    T = y.shape[0]

    def body(y_ref, t_ref, s_ref, d_ref):
        @pl.when(pl.program_id(0) == 0)
        def _():
            s_ref[...] = jnp.zeros_like(s_ref)

        e = y_ref[...] - t_ref[...]
        s_ref[...] += jnp.sum(e * e).reshape(1, 1)
        d_ref[...] = e * (1.0 / D_MODEL)

    return _row_call(body, "loss", T, [_rows()] * 2, [_full((1, 1)), _rows()],
                     [_sds((1, 1), F32), _sds((T, D_MODEL), F32)], (y, target))


def _attn_combine(outs, lses, proj):
    T = proj.shape[0]

    def body(o0, o1, o2, l0, l1, l2, z_ref, a_ref, o_ref, lse_ref):
        a0, a1, a2 = l0[...], l1[...], l2[...]
        m = jnp.maximum(jnp.maximum(a0, a1), a2)
        e0, e1, e2 = jnp.exp(a0 - m), jnp.exp(a1 - m), jnp.exp(a2 - m)
        ssum = e0 + e1 + e2
        o = (e0 * o0[...] + e1 * o1[...] + e2 * o2[...]) / ssum
        zv = z_ref[...].astype(F32)
        o_ref[...] = o
        lse_ref[...] = m + jnp.log(ssum)
        a_ref[...] = (o * zv * _sigmoid(zv)).astype(BF16)

    return _row_call(body, "attn_combine", T, [_rows()] * 6 + [_rows(cb=3 * N_GROUPS)], [_rows()] * 3,
                     [_sds((T, D_MODEL), BF16), _sds((T, D_MODEL), F32), _sds((T, D_MODEL), F32)],
                     (*outs, *lses, proj))


def _gate_bwd(da, o, proj):
    T = da.shape[0]

    def body(da_ref, o_ref, z_ref, do_ref, dz_ref):
        dv = da_ref[...]
        zv = z_ref[...]
        sg = _sigmoid(zv)
        do_ref[...] = dv * zv * sg
        dz_ref[...] = dv * o_ref[...] * sg * (1.0 + zv * (1.0 - sg))

    zcols = _rows(cb=3 * N_GROUPS)
    return _row_call(body, "gate_bwd", T, [_rows(), _rows(), zcols], [_rows(), zcols],
                     [_sds((T, D_MODEL), F32), _sds((T, PROJ_COLS), F32)], (da, o, proj))


def _rope_tables(positions):
    inv_freq = 1.0 / (ROPE_THETA ** (jnp.arange(0, HEAD_DIM, 2, dtype=F32) / HEAD_DIM))
    ang = positions.astype(F32)[..., None] * inv_freq
    cos, sin = jnp.cos(ang), jnp.sin(ang)
    return jnp.tile(cos, (1, 1, 4)), jnp.concatenate([-sin, sin, -sin, sin], axis=-1)


def _rotate_half_partner(t):
    lane = lax.broadcasted_iota(jnp.int32, t.shape, 1)
    return jnp.where((lane % HEAD_DIM) < HEAD_DIM // 2,
                     pltpu.roll(t, LANES - HEAD_DIM // 2, 1), pltpu.roll(t, HEAD_DIM // 2, 1))


def _mask_bias(first):
    qi = lax.broadcasted_iota(jnp.int32, (N_BACK, 2 * N_BACK), 0)
    kj = lax.broadcasted_iota(jnp.int32, (N_BACK, 2 * N_BACK), 1)
    ok = (kj >= qi) & (kj <= qi + N_BACK)
    if first:
        ok = ok & (kj >= N_BACK)
    return jnp.where(ok, 0.0, NEG_INF).astype(F32)


def _stack_heads(t, head0):
    zero = jnp.zeros_like(t)
    return jnp.concatenate([jnp.where(head0, t, zero), jnp.where(head0, zero, t)], axis=0)


def _unstack_heads(t2, head0):
    return jnp.where(head0, t2[:N_BACK], t2[N_BACK:])


def _block_loop(nb, block):
    first, rest = _mask_bias(True), _mask_bias(False)
    first, rest = jnp.concatenate([first, first], axis=0), jnp.concatenate([rest, rest], axis=0)
    if nb <= BLOCK_UNROLL:
        for n in range(nb):
            block(n, first if n == 0 else rest)
        return

    def step(n, carry):
        block(n, jnp.where(n == 0, first, rest))
        return carry

    lax.fori_loop(0, nb, step, 0, unroll=BLOCK_UNROLL)


def _for(count, body, unroll_fully):
    if unroll_fully:
        for i in range(count):
            body(i)
    else:
        lax.fori_loop(0, count, lambda i, carry: (body(i), carry)[1], 0, unroll=4)


def _residues_together(nb):
    return min(RESIDUES_TOGETHER, max(1, BLOCK_UNROLL // nb))


def _residue_loop(d, nb, residue):
    together = _residues_together(nb)
    assert d % together == 0

    def group(i, carry):
        for u in range(together):
            residue(i * together + u, u)
        return carry

    lax.fori_loop(0, d // together, group, 0)


_NT = (((1,), (1,)), ((), ()))
_TN = (((0,), (0,)), ((), ()))


def _residue_rows(r, i, d):
    start = r + i * (N_BACK * d)
    if d == 1:
        return pl.ds(pl.multiple_of(start, N_BACK), N_BACK)
    return pl.ds(start, N_BACK, stride=d)


def _seq_rows(i):
    return pl.ds(pl.multiple_of(i * N_BACK, N_BACK), N_BACK)


def _rows_at(base, i, size=N_BACK):
    return pl.ds(pl.multiple_of(base + i * N_BACK, N_BACK), size)


def _attn_fwd(proj, cos, sin, group, Bl, S):
    d = WIN_DIL[group][1]
    L = S // d
    nb = L // N_BACK
    P = L + N_BACK
    assert WIN_DIL[group][0] // d == N_BACK and L % N_BACK == 0

    def body(q_ref, k_ref, v_ref, cos_ref, sin_ref, o_ref, lse_ref, qr, kr, vp):
        head0 = lax.broadcasted_iota(jnp.int32, (1, LANES), 1) < HEAD_DIM
        zeros = jnp.zeros((N_BACK, LANES), BF16)

        def residue(r, u):
            del u
            qbase, kbase = r * L, r * P
            kr[_rows_at(kbase, 0), :] = zeros
            vp[_rows_at(kbase, 0), :] = zeros

            def rope(i):
                rows = _residue_rows(r, i, d)
                cs, sn = cos_ref[rows, :], sin_ref[rows, :]
                q, k = q_ref[rows, :], k_ref[rows, :]
                qr[_rows_at(qbase, i), :] = ((q * cs + _rotate_half_partner(q) * sn)
                                            * (HEAD_DIM ** -0.5)).astype(BF16)
                kr[_rows_at(kbase, i + 1), :] = (k * cs + _rotate_half_partner(k) * sn).astype(BF16)
                vp[_rows_at(kbase, i + 1), :] = v_ref[rows, :].astype(BF16)

            _for(nb, rope, nb <= BLOCK_UNROLL)

            def block(n, bias):
                win = _rows_at(kbase, n, 2 * N_BACK)
                q2, kw, vw = _stack_heads(qr[_rows_at(qbase, n), :], head0), kr[win, :], vp[win, :]
                s = lax.dot_general(q2, kw, _NT, preferred_element_type=F32) + bias
                m = jnp.max(s, axis=1, keepdims=True)
                p = jnp.exp(s - m)
                l = jnp.sum(p, axis=1, keepdims=True)
                pv = jnp.dot(p.astype(BF16), vw, preferred_element_type=F32)
                rows = _residue_rows(r, n, d)
                o_ref[rows, :] = _unstack_heads(pv * (1.0 / l), head0)
                lse_ref[rows, :] = _unstack_heads((m + jnp.log(l)) + jnp.zeros((2 * N_BACK, LANES), F32), head0)

            _block_loop(nb, block)

        _residue_loop(d, nb, residue)

    act = pl.BlockSpec((None, S, LANES), lambda b, hp: (b, 0, hp))
    tab = pl.BlockSpec((None, S, LANES), lambda b, hp: (b, 0, 0))
    col = lambda which: pl.BlockSpec((None, S, LANES),
                                     lambda b, hp: (b, 0, (which * N_GROUPS + group) * HEAD_PAIRS + hp))
    seq = lambda rows: pl.BlockSpec((None, None, rows, LANES), lambda b, hp: (b, hp, 0, 0))
    p3 = proj.reshape(Bl, S, PROJ_COLS)
    o, lse, qr, kr, vp = pl.pallas_call(
        body, name="attn_fwd_g%d" % group, grid=(Bl, HEAD_PAIRS),
        in_specs=[col(0), col(1), col(2), tab, tab], out_specs=[act, act, seq(S), seq(d * P), seq(d * P)],
        out_shape=[_sds((Bl, S, D_MODEL), F32)] * 2 + [_sds((Bl, HEAD_PAIRS, S, LANES), BF16)]
        + [_sds((Bl, HEAD_PAIRS, d * P, LANES), BF16)] * 2,
        compiler_params=_params(("parallel", "arbitrary"), VMEM_LIMIT_ATTN),
    )(p3, p3, p3, cos, sin)
    return o.reshape(Bl * S, D_MODEL), lse.reshape(Bl * S, D_MODEL), (qr, kr, vp)


def _attn_bwd(saved, cos, sin, do, o, lse, dproj, group, Bl, S):
    d = WIN_DIL[group][1]
    L = S // d
    nb = L // N_BACK
    P = L + N_BACK
    steps = Bl * HEAD_PAIRS

    def body(qr, kr, vp, cos_ref, sin_ref, do_ref, o_ref, lse_ref, dproj_in, dproj_ref,
             dk_accs, dv_accs, stage, sems):
        del dproj_in
        head0 = lax.broadcasted_iota(jnp.int32, (1, LANES), 1) < HEAD_DIM
        b, hp = pl.program_id(0), pl.program_id(1)
        step = b * HEAD_PAIRS + hp
        slot = step % 2
        dq_s, dk_s, dv_s = stage.at[slot, 0], stage.at[slot, 1], stage.at[slot, 2]

        def copies(which_slot):
            out = []
            for which in range(3):
                col = ((which * N_GROUPS + group) * HEAD_PAIRS + hp) * LANES
                out.append(pltpu.make_async_copy(
                    stage.at[which_slot, which], dproj_ref.at[b, :, pl.ds(pl.multiple_of(col, LANES), LANES)],
                    sems.at[which_slot, which]))
            return out

        @pl.when(step >= 2)
        def _():
            for cp in copies(slot):
                cp.wait()

        def residue(r, u):
            qbase, kbase = r * L, r * P
            dk_acc, dv_acc = dk_accs.at[u], dv_accs.at[u]
            dk_acc[...] = jnp.zeros_like(dk_acc)
            dv_acc[...] = jnp.zeros_like(dv_acc)

            def block(n, bias):
                win = pl.ds(pl.multiple_of(n * N_BACK, N_BACK), 2 * N_BACK)
                kwin = _rows_at(kbase, n, 2 * N_BACK)
                rows = _residue_rows(r, n, d)
                q2, kw, vw = _stack_heads(qr[_rows_at(qbase, n), :], head0), kr[kwin, :], vp[kwin, :]
                dof = do_ref[rows, :]
                do2 = _stack_heads(dof.astype(BF16), head0)
                lse_b = lse_ref[rows, :]
                lse2 = jnp.concatenate([lse_b[:, 0:1], lse_b[:, HEAD_DIM:HEAD_DIM + 1]], axis=0)
                dsum = _stack_heads(dof * o_ref[rows, :], head0)
                delta = jnp.sum(dsum, axis=1, keepdims=True)
                s = lax.dot_general(q2, kw, _NT, preferred_element_type=F32) + bias
                p = jnp.exp(s - lse2)
                dp = lax.dot_general(do2, vw, _NT, preferred_element_type=F32)
                ds = (p * (dp - delta)).astype(BF16)
                dq = _unstack_heads(jnp.dot(ds, kw, preferred_element_type=F32), head0) * (HEAD_DIM ** -0.5)
                cs, sn = cos_ref[rows, :], sin_ref[rows, :]
                dq_s[rows, :] = dq * cs + _rotate_half_partner(dq * sn)
                dk_acc[win, :] += lax.dot_general(ds, q2, _TN, preferred_element_type=F32)
                dv_acc[win, :] += lax.dot_general(p.astype(BF16), do2, _TN, preferred_element_type=F32)

            _block_loop(nb, block)

            def finish(i):
                rows = _residue_rows(r, i, d)
                cs, sn = cos_ref[rows, :], sin_ref[rows, :]
                dk = dk_acc[_seq_rows(i + 1), :]
                dk_s[rows, :] = dk * cs + _rotate_half_partner(dk * sn)
                dv_s[rows, :] = dv_acc[_seq_rows(i + 1), :]

            _for(nb, finish, nb <= BLOCK_UNROLL)

        _residue_loop(d, nb, residue)
        for cp in copies(slot):
            cp.start()

        @pl.when(step == steps - 1)
        def _():
            if steps > 1:
                for cp in copies(1 - slot):
                    cp.wait()
            for cp in copies(slot):
                cp.wait()

    act = pl.BlockSpec((None, S, LANES), lambda b, hp: (b, 0, hp))
    tab = pl.BlockSpec((None, S, LANES), lambda b, hp: (b, 0, 0))
    seq = lambda rows: pl.BlockSpec((None, None, rows, LANES), lambda b, hp: (b, hp, 0, 0))
    view = lambda t: t.reshape(Bl, S, D_MODEL)
    out = pl.pallas_call(
        body, name="attn_bwd_g%d" % group, grid=(Bl, HEAD_PAIRS),
        in_specs=[seq(S), seq(d * P), seq(d * P), tab, tab, act, act, act, _ANY], out_specs=_ANY,
        out_shape=_sds((Bl, S, PROJ_COLS), F32), input_output_aliases={8: 0},
        scratch_shapes=[pltpu.VMEM((_residues_together(nb), P, LANES), F32),
                        pltpu.VMEM((_residues_together(nb), P, LANES), F32),
                        pltpu.VMEM((2, 3, S, LANES), F32), pltpu.SemaphoreType.DMA((2, 3))],
        compiler_params=_params(("arbitrary", "arbitrary"), VMEM_LIMIT_ATTN),
    )(*saved, cos, sin, view(do), view(o), view(lse), dproj.reshape(Bl, S, PROJ_COLS))
    return out.reshape(Bl * S, PROJ_COLS)


CONV_TILE = 256
CONV_CHUNK = 64
SUBLANES = 8
CONV_SHIFT_ROWS = CONV_TILE + CONV_HALO - SUBLANES


def _fill_shifted(shifted, ext, cs):
    for k in range(1, SUBLANES):
        shifted[k - 1] = ext[pl.ds(k, CONV_SHIFT_ROWS), cs]


def _shifted_rows(shifted, ext, cs, off):
    k = off % SUBLANES
    if k == 0:
        return ext[pl.ds(off, CONV_CHUNK), cs]
    return shifted[k - 1, pl.ds(off - k, CONV_CHUNK), :]


def _conv_fwd(proj, z, dw, dwb, ln_g, ln_b, Bl, S):
    tr = CONV_TILE
    nj = S // tr
    hb = tr // CONV_HALO

    def body(a_ref, b_ref, ah_ref, bh_ref, z_ref, dw_ref, dwb_ref, g_ref, bb_ref, u1_ref, out_ref, ext, shifted):
        j = pl.program_id(1)
        halo = ah_ref[0].astype(F32) * _sigmoid(bh_ref[0].astype(F32))
        ext[pl.ds(0, CONV_HALO), :] = jnp.where(j > 0, halo, 0.0)
        ext[pl.ds(CONV_HALO, tr), :] = a_ref[0].astype(F32) * _sigmoid(b_ref[0].astype(F32))

        def cols(c, carry):
            cs = pl.ds(pl.multiple_of(c * LANES, LANES), LANES)
            _fill_shifted(shifted, ext, cs)
            for rc in range(tr // CONV_CHUNK):
                acc = jnp.zeros((CONV_CHUNK, LANES), F32)
                for w in range(CONV_WIDTH):
                    off = rc * CONV_CHUNK + CONV_HALO - (CONV_WIDTH - 1) + w
                    acc = acc + dw_ref[pl.ds(w, 1), cs] * _shifted_rows(shifted, ext, cs, off)
                u1_ref[0, pl.ds(rc * CONV_CHUNK, CONV_CHUNK), cs] = acc + dwb_ref[:, cs]
            return carry

        lax.fori_loop(0, D_MODEL // LANES, cols, 0)
        u1 = u1_ref[0]
        mu = jnp.mean(u1, axis=1, keepdims=True)
        xc = u1 - mu
        rstd = lax.rsqrt(jnp.mean(xc * xc, axis=1, keepdims=True) + LN_EPS)
        u2 = xc * rstd * g_ref[...] + bb_ref[...]
        zv = z_ref[0].astype(F32)
        out_ref[0] = (u2 * _sigmoid(u2) * zv * _sigmoid(zv)).astype(BF16)

    tile = lambda cb: pl.BlockSpec((1, tr, D_MODEL), lambda b, j: (b, j, cb))
    halo = lambda cb: pl.BlockSpec((1, CONV_HALO, D_MODEL), lambda b, j: (b, jnp.maximum(j * hb - 1, 0), cb))
    par = lambda r: pl.BlockSpec((r, D_MODEL), lambda b, j: (0, 0))
    p3 = proj.reshape(Bl, S, 2 * D_MODEL)
    u1, out = pl.pallas_call(
        body, name="conv_fwd", grid=(Bl, nj),
        in_specs=[tile(0), tile(1), halo(0), halo(1), tile(0), par(32), par(1), par(1), par(1)],
        out_specs=[tile(0), tile(0)],
        out_shape=[_sds((Bl, S, D_MODEL), F32), _sds((Bl, S, D_MODEL), BF16)],
        scratch_shapes=[pltpu.VMEM((tr + CONV_HALO, D_MODEL), F32),
                        pltpu.VMEM((SUBLANES - 1, CONV_SHIFT_ROWS, LANES), F32)],
        compiler_params=_params(("parallel", "arbitrary")),
    )(p3, p3, p3, p3, z.reshape(Bl, S, D_MODEL), dw, dwb, ln_g, ln_b)
    return u1.reshape(Bl * S, D_MODEL), out.reshape(Bl * S, D_MODEL)


def _conv_norm_bwd(da2, z, u1, ln_g, ln_b):
    T = da2.shape[0]

    def body(da_ref, z_ref, u_ref, g_ref, b_ref, du_ref, dz_ref, dg_ref, db_ref):
        @pl.when(pl.program_id(0) == 0)
        def _():
            dg_ref[...] = jnp.zeros_like(dg_ref)
            db_ref[...] = jnp.zeros_like(db_ref)

        u1 = u_ref[...]
        mu = jnp.mean(u1, axis=1, keepdims=True)
        xc = u1 - mu
        rstd = lax.rsqrt(jnp.mean(xc * xc, axis=1, keepdims=True) + LN_EPS)
        nrm = xc * rstd
        u2 = nrm * g_ref[...] + b_ref[...]
        s2 = _sigmoid(u2)
        zv = z_ref[...].astype(F32)
        sz = _sigmoid(zv)
        dv = da_ref[...]
        dz_ref[...] = (dv * u2 * s2 * sz * (1.0 + zv * (1.0 - sz))).astype(BF16)
        du2 = dv * zv * sz * s2 * (1.0 + u2 * (1.0 - s2))
        dg_ref[...] += jnp.sum(du2 * nrm, axis=0, keepdims=True)
        db_ref[...] += jnp.sum(du2, axis=0, keepdims=True)
        dn = du2 * g_ref[...]
        du_ref[...] = rstd * (dn - jnp.mean(dn, axis=1, keepdims=True)
                              - nrm * jnp.mean(dn * nrm, axis=1, keepdims=True))

    return _row_call(body, "conv_norm_bwd", T,
                     [_rows(), _rows(), _rows(), _full((1, D_MODEL)), _full((1, D_MODEL))],
                     [_rows(), _rows(), _full((1, D_MODEL)), _full((1, D_MODEL))],
                     [_sds((T, D_MODEL), F32), _sds((T, D_MODEL), BF16), _sds((1, D_MODEL), F32),
                      _sds((1, D_MODEL), F32)], (da2, z, u1, ln_g, ln_b))


def _conv_bwd(proj, du1, dw, Bl, S):
    tr = CONV_TILE
    nj = S // tr
    hb = tr // CONV_HALO

    def body(a_ref, b_ref, ah_ref, bh_ref, du_ref, duh_ref, dw_ref, dab_ref, ddw_ref, ddb_ref, uext, dext, du0,
             ushift, dshift, ddw8):
        first = (pl.program_id(0) == 0) & (pl.program_id(1) == 0)
        last = (pl.program_id(0) == Bl - 1) & (pl.program_id(1) == nj - 1)
        j = pl.program_id(1)

        @pl.when(first)
        def _():
            ddw8[...] = jnp.zeros_like(ddw8)
            ddb_ref[...] = jnp.zeros_like(ddb_ref)

        halo = ah_ref[0].astype(F32) * _sigmoid(bh_ref[0].astype(F32))
        uext[pl.ds(0, CONV_HALO), :] = jnp.where(j > 0, halo, 0.0)
        av = a_ref[0].astype(F32)
        sb = _sigmoid(b_ref[0].astype(F32))
        uext[pl.ds(CONV_HALO, tr), :] = av * sb
        dext[pl.ds(0, tr), :] = du_ref[0]
        dext[pl.ds(tr, CONV_HALO), :] = jnp.where(j < nj - 1, duh_ref[0], 0.0)
        ddb_ref[...] += jnp.sum(du_ref[0], axis=0, keepdims=True)

        def cols(c, carry):
            cs = pl.ds(pl.multiple_of(c * LANES, LANES), LANES)
            _fill_shifted(dshift, dext, cs)
            _fill_shifted(ushift, uext, cs)
            for rc in range(tr // CONV_CHUNK):
                base = rc * CONV_CHUNK
                acc = jnp.zeros((CONV_CHUNK, LANES), F32)
                for w in range(CONV_WIDTH):
                    acc = acc + dw_ref[pl.ds(w, 1), cs] * _shifted_rows(dshift, dext, cs, base + CONV_WIDTH - 1 - w)
                du0[pl.ds(base, CONV_CHUNK), cs] = acc
            for w in range(CONV_WIDTH):
                part = jnp.zeros((SUBLANES, LANES), F32)
                for rc in range(tr // CONV_CHUNK):
                    base = rc * CONV_CHUNK
                    prod = dext[pl.ds(base, CONV_CHUNK), cs] * _shifted_rows(
                        ushift, uext, cs, base + CONV_HALO - (CONV_WIDTH - 1) + w)
                    for i in range(CONV_CHUNK // SUBLANES):
                        part = part + prod[i * SUBLANES:(i + 1) * SUBLANES]
                ddw8[pl.ds(w * SUBLANES, SUBLANES), cs] += part
            return carry

        lax.fori_loop(0, D_MODEL // LANES, cols, 0)
        g = du0[...]
        dab_ref[0, :, 0:D_MODEL] = (g * sb).astype(BF16)
        dab_ref[0, :, D_MODEL:2 * D_MODEL] = (g * av * sb * (1.0 - sb)).astype(BF16)

        @pl.when(last)
        def _():
            for w in range(CONV_WIDTH + 1):
                ddw_ref[pl.ds(w, 1), :] = jnp.sum(ddw8[pl.ds(w * SUBLANES, SUBLANES), :], axis=0, keepdims=True)

    tile = lambda cb: pl.BlockSpec((1, tr, D_MODEL), lambda b, j: (b, j, cb))
    halo = lambda cb: pl.BlockSpec((1, CONV_HALO, D_MODEL), lambda b, j: (b, jnp.maximum(j * hb - 1, 0), cb))
    nxt = pl.BlockSpec((1, CONV_HALO, D_MODEL), lambda b, j: (b, jnp.minimum((j + 1) * hb, S // CONV_HALO - 1), 0))
    par = lambda r: pl.BlockSpec((r, D_MODEL), lambda b, j: (0, 0))
    p3 = proj.reshape(Bl, S, 2 * D_MODEL)
    d3 = du1.reshape(Bl, S, D_MODEL)
    dab, ddw, ddb = pl.pallas_call(
        body, name="conv_bwd", grid=(Bl, nj),
        in_specs=[tile(0), tile(1), halo(0), halo(1), tile(0), nxt, par(32)],
        out_specs=[pl.BlockSpec((1, tr, 2 * D_MODEL), lambda b, j: (b, j, 0)), par(32), par(1)],
        out_shape=[_sds((Bl, S, 2 * D_MODEL), BF16), _sds((32, D_MODEL), F32), _sds((1, D_MODEL), F32)],
        scratch_shapes=[pltpu.VMEM((tr + CONV_HALO, D_MODEL), F32), pltpu.VMEM((tr + CONV_HALO, D_MODEL), F32),
                        pltpu.VMEM((tr, D_MODEL), F32),
                        pltpu.VMEM((SUBLANES - 1, CONV_SHIFT_ROWS, LANES), F32),
                        pltpu.VMEM((SUBLANES - 1, CONV_SHIFT_ROWS, LANES), F32),
                        pltpu.VMEM(((CONV_WIDTH + 1) * SUBLANES, D_MODEL), F32)],
        compiler_params=_params(("arbitrary", "arbitrary")),
    )(p3, p3, p3, p3, d3, d3, dw)
    return dab.reshape(Bl * S, 2 * D_MODEL), ddw, ddb


_LAYOUT = (
    ("pre_norm_g", (4, 1024), None), ("post_norm_g", (4, 1024), None),
    ("attn_w_in", (2, 1024, 2560), 2), ("attn_w_out", (2, 256, 1024), 1),
    ("conv_w_in", (2, 1024, 768), 2), ("conv_dw_w", (2, 31, 256), 2),
    ("conv_dw_b", (2, 256), 1), ("conv_ln_g", (2, 256), 1), ("conv_ln_b", (2, 256), 1),
    ("conv_w_out", (2, 256, 1024), 1), ("ple_w_proj", (4, 256, 256), 2), ("ple_w_gate", (4, 256, 1024), 1),
)
_MATMUL_WEIGHTS = ("attn_w_in", "attn_w_out", "conv_w_in", "conv_w_out", "ple_w_proj", "ple_w_gate")
_FIRST_LAYER = ("attn_w_in", "attn_w_out", "ple_w_proj", "ple_w_gate")
_AXIS = {n: a for n, _, a in _LAYOUT}


def _size(shape):
    n = 1
    for s in shape:
        n *= s
    return n


def _padded_rows(shape):
    rows = _size(shape) // shape[-1]
    return rows + (-rows) % FLAT_ROW_ALIGN


def _rows2d(a):
    a2 = a.reshape(-1, a.shape[-1])
    pad = _padded_rows(a.shape) - a2.shape[0]
    return jnp.pad(a2, ((0, pad), (0, 0))) if pad else a2


def _col_blocks(a):
    a2 = _rows2d(a)
    return jnp.concatenate([a2[:, c:c + FLAT_COLS] for c in range(0, a2.shape[1], FLAT_COLS)], axis=0)


def _from_col_blocks(flat, off, shape):
    rows, nblk = _padded_rows(shape), shape[-1] // FLAT_COLS
    a2 = jnp.concatenate([flat[off + b * rows:off + (b + 1) * rows] for b in range(nblk)], axis=1)
    return a2[:_size(shape) // shape[-1]].reshape(shape), off + nblk * rows


def _shard_col_blocks(full, shape, axis):
    if axis is None:
        blocks = _col_blocks(full)
        return jnp.broadcast_to(blocks[None], (N_CHIPS,) + blocks.shape)
    m = shape[-1]
    if axis == len(shape) - 1:
        a2 = _rows2d(full)
        pieces = [a2[:, c:c + FLAT_COLS] for c in range(0, N_CHIPS * m, FLAT_COLS)]
    else:
        layers, r, _ = shape
        assert axis == 1 and (layers * r) % FLAT_ROW_ALIGN == 0
        pieces = [full[:, s * r:(s + 1) * r, c:c + FLAT_COLS].reshape(layers * r, FLAT_COLS)
                  for s in range(N_CHIPS) for c in range(0, m, FLAT_COLS)]
    return jnp.concatenate(pieces, axis=0).reshape(N_CHIPS, -1, FLAT_COLS)


_FLAT_BIG = ("attn_w_in", "conv_w_in", "ple_w_gate", "attn_w_out", "conv_w_out", "ple_w_proj")
_FLAT_SMALL = ("pre_norm_g", "post_norm_g", "conv_dw_w", "conv_dw_b", "conv_ln_g", "conv_ln_b")
PACK_TILE = 1024


_FIRST_LAYER_PARAMS = _FIRST_LAYER + ("pre_norm_g", "post_norm_g")


def _part_shapes(first):
    out = {}
    for n, shape, _ in _LAYOUT:
        layers = (1 if first else shape[0] - 1) if n in _FIRST_LAYER_PARAMS else (0 if first else shape[0])
        if layers and not (first and n == "pre_norm_g"):
            out[n] = (layers,) + shape[1:]
    return out


def _flat_plan(shapes):
    out, off = {}, 0
    for n in _FLAT_BIG + _FLAT_SMALL:
        if n in shapes:
            out[n] = off
            off += _padded_rows(shapes[n]) * (shapes[n][-1] // FLAT_COLS)
    return out, off + (-off) % (2 * FLAT_TILE)


def _unpack_f32(flat, shapes):
    offsets, _ = _flat_plan(shapes)
    return {n: _from_col_blocks(flat, offsets[n], shapes[n])[0] for n in shapes}


def _pack_param(full, shape, axis, off, flat, total_rows):
    layers, r, m = shape
    nblk = m // FLAT_COLS
    if axis == 2:
        rows = layers * r
        tr = math.gcd(math.gcd(rows, PACK_TILE), off) if off else math.gcd(rows, PACK_TILE)
        assert rows % tr == 0 and off % tr == 0 and tr % FLAT_ROW_ALIGN == 0
        src = full.reshape(rows, N_CHIPS * m)
        grid = (N_CHIPS * nblk, rows // tr)
        in_spec = pl.BlockSpec((tr, FLAT_COLS), lambda j, i: (i, j))
        out_spec = pl.BlockSpec((None, tr, FLAT_COLS), lambda j, i: (j // nblk, (off + (j % nblk) * rows) // tr + i, 0))
    else:
        assert axis == 1 and off % r == 0
        src = full.reshape(layers * N_CHIPS * r, m)
        grid = (layers, N_CHIPS, nblk)
        in_spec = pl.BlockSpec((r, FLAT_COLS), lambda l, s, b: (l * N_CHIPS + s, b))
        out_spec = pl.BlockSpec((None, r, FLAT_COLS), lambda l, s, b: (s, (off + b * layers * r) // r + l, 0))

    def copy_body(src_ref, *rest):
        rest[-1][...] = src_ref[...]

    args, in_specs, aliases = [src], [in_spec], {}
    if flat is not None:
        args.append(flat)
        in_specs.append(_ANY)
        aliases = {1: 0}
    return pl.pallas_call(
        copy_body, name="pack_grad", grid=grid, in_specs=in_specs, out_specs=out_spec,
        out_shape=_sds((N_CHIPS, total_rows, FLAT_COLS), F32), input_output_aliases=aliases,
        compiler_params=_params(("arbitrary",) * len(grid)))(*args)


SMALL_ROWS = 40


def _stack_small(w):
    rows = [w["conv_dw_w"]] + [w[n][:, None, :] for n in ("conv_dw_b", "conv_ln_g", "conv_ln_b")]
    stacked = jnp.concatenate(rows, axis=1)
    return jnp.pad(stacked, ((0, 0), (0, SMALL_ROWS - stacked.shape[1]), (0, 0)))


def _unstack_small(small):
    return {"conv_dw_w": small[:, :CONV_WIDTH], "conv_dw_b": small[:, CONV_WIDTH],
            "conv_ln_g": small[:, CONV_WIDTH + 1], "conv_ln_b": small[:, CONV_WIDTH + 2]}


def _pack_full_grads(grads, shapes):
    offsets, total_rows = _flat_plan(shapes)
    flat = None
    for n in _FLAT_BIG:
        if n in shapes:
            flat = _pack_param(grads[n], shapes[n], _AXIS[n], offsets[n], flat, total_rows)
    small_names = [n for n in _FLAT_SMALL if n in shapes]
    small = jnp.concatenate([_shard_col_blocks(grads[n], shapes[n], _AXIS[n]) for n in small_names], axis=1)
    start = offsets[small_names[0]]
    small = jnp.pad(small, ((0, 0), (0, total_rows - start - small.shape[1]), (0, 0)))
    return lax.dynamic_update_slice(flat, small, (0, start, 0))


_ANY = pl.BlockSpec(memory_space=pl.ANY)


def _mesh_pos():
    return lax.axis_index("x"), lax.axis_index("y"), lax.axis_index("c")


def _other_chips(x, y):
    return [(1 - x, y), (x, 1 - y), (1 - x, 1 - y)]


COPIES_PER_ARRAY = 7


def _allgather_weights(shards, axes, small):
    n = len(shards)
    full_shape = lambda a, axis: tuple(d * (N_CHIPS if i == axis else 1) for i, d in enumerate(a.shape))

    def body(*refs):
        ins, small_in = refs[:n], refs[n]
        outs, small_out = refs[n + 1:2 * n + 1], refs[2 * n + 1]
        send_sems, recv_sems = refs[2 * n + 2:]
        x, y, c = _mesh_pos()
        mine, me, sibling = 2 * x + y, (x, y, c), (x, y, 1 - c)
        chips = _other_chips(x, y)

        def region(a, chip, half):
            _, rows, cols = shards[a].shape
            h = rows // 2
            if axes[a] == 2:
                return outs[a].at[:, slice(None) if half is None else pl.ds(half * h, h), pl.ds(chip * cols, cols)]
            if half is None:
                return outs[a].at[:, pl.ds(chip * rows, rows), :]
            return outs[a].at[:, pl.ds(chip * rows + half * h, h), :]

        def copy(k, src, dst, to):
            return pltpu.make_async_remote_copy(src_ref=src, dst_ref=dst, send_sem=send_sems.at[k],
                                                recv_sem=recv_sems.at[k], device_id=to, device_id_type=MESH)

        def arrival(k, dst):
            return copy(k, dst, dst, me)

        sends = []
        for a in range(n):
            h = shards[a].shape[1] // 2
            base = a * COPIES_PER_ARRAY
            sends.append(copy(base + 6, ins[a], region(a, mine, None), sibling))
            for j, (cx, cy) in enumerate(chips):
                sends.append(copy(base + j, ins[a].at[:, pl.ds(c * h, h), :], region(a, mine, c), (cx, cy, c)))
        small_cols = small.shape[2]
        small_region = lambda chip: small_out.at[:, :, pl.ds(chip * small_cols, small_cols)]
        base = n * COPIES_PER_ARRAY
        sends.append(copy(base + 3, small_in, small_region(mine), sibling))
        for j, (cx, cy) in enumerate(chips):
            sends.append(copy(base + j, small_in, small_region(mine), (cx, cy, c)))
        for cp in sends:
            cp.start()
        for j, (cx, cy) in enumerate(chips):
            for a in range(n):
                k = a * COPIES_PER_ARRAY + j
                arrival(k, region(a, 2 * cx + cy, c)).wait_recv()
                passed = copy(k + 3, region(a, 2 * cx + cy, c), region(a, 2 * cx + cy, c), sibling)
                passed.start()
                sends.append(passed)
        for j, (cx, cy) in enumerate(chips):
            for a in range(n):
                arrival(a * COPIES_PER_ARRAY + 3 + j, region(a, 2 * cx + cy, 1 - c)).wait_recv()
            arrival(base + j, small_region(2 * cx + cy)).wait_recv()
        for a in range(n):
            arrival(a * COPIES_PER_ARRAY + 6, region(a, mine, None)).wait_recv()
        arrival(base + 3, small_region(mine)).wait_recv()
        for cp in sends:
            cp.wait_send()

    n_sems = n * COPIES_PER_ARRAY + 4
    out = pl.pallas_call(
        body, name="allgather_weights", in_specs=[_ANY] * (n + 1), out_specs=[_ANY] * (n + 1),
        out_shape=[_sds(full_shape(a, axis), a.dtype) for a, axis in zip(shards, axes)]
        + [_sds(full_shape(small, 2), small.dtype)],
        scratch_shapes=[pltpu.SemaphoreType.DMA((n_sems,)), pltpu.SemaphoreType.DMA((n_sems,))],
    )(*shards, small)
    return out[:n], out[n]


_HBM = pl.BlockSpec(memory_space=pltpu.HBM)
_SEM = pl.BlockSpec(memory_space=pltpu.SEMAPHORE)


def _full_shape(shard, axis):
    return tuple(d * (N_CHIPS if i == axis else 1) for i, d in enumerate(shard.shape))


def _direct_gather_copies(srcs, lands, shapes, axes, send_sems, recv_sems):
    x, y, c = _mesh_pos()
    mine, me, sibling = 2 * x + y, (x, y, c), (x, y, 1 - c)

    def region(a, chip, half):
        _, rows, cols = shapes[a]
        h = rows // 2
        if axes[a] == 2:
            return lands[a].at[:, slice(None) if half is None else pl.ds(half * h, h), pl.ds(chip * cols, cols)]
        if half is None:
            return lands[a].at[:, pl.ds(chip * rows, rows), :]
        return lands[a].at[:, pl.ds(chip * rows + half * h, h), :]

    def copy(k, src, dst, to):
        return pltpu.make_async_remote_copy(src_ref=src, dst_ref=dst, send_sem=send_sems.at[k],
                                            recv_sem=recv_sems.at[k], device_id=to, device_id_type=MESH)

    outgoing, incoming = [], []
    for a in range(len(srcs)):
        h = shapes[a][1] // 2
        base = a * COPIES_PER_ARRAY
        for j, (cx, cy) in enumerate(_other_chips(x, y)):
            for t, ct in enumerate((c, 1 - c)):
                k = base + 2 * j + t
                outgoing.append(copy(k, srcs[a].at[:, pl.ds(c * h, h), :], region(a, mine, c), (cx, cy, ct)))
                landed = region(a, 2 * cx + cy, ct)
                incoming.append(copy(k, landed, landed, me))
        outgoing.append(copy(base + 6, srcs[a], region(a, mine, None), sibling))
        incoming.append(copy(base + 6, region(a, mine, None), region(a, mine, None), me))
    return outgoing, incoming


def _allgather_start(shards, axes):
    n = len(shards)
    shapes = [s.shape for s in shards]

    def body(*refs):
        srcs, lands = refs[:n], refs[n:2 * n]
        send_sems, recv_sems = refs[2 * n], refs[2 * n + 1]
        token = refs[-1]
        outgoing, _ = _direct_gather_copies(srcs, lands, shapes, axes, send_sems, recv_sems)
        for cp in outgoing:
            cp.start()
        token[...] = jnp.zeros_like(token)

    n_sems = n * COPIES_PER_ARRAY
    zones = [pltpu.with_memory_space_constraint(lax.empty(_full_shape(s, ax), s.dtype), pltpu.HBM)
             for s, ax in zip(shards, axes)]
    out = pl.pallas_call(
        body, name="allgather_rest_start",
        out_shape=(pltpu.SemaphoreType.DMA((n_sems,)), pltpu.SemaphoreType.DMA((n_sems,)),
                   *[pltpu.HBM(s.shape, s.dtype) for s in shards], *[pltpu.HBM(z.shape, z.dtype) for z in zones],
                   jax.ShapeDtypeStruct((8, LANES), F32)),
        in_specs=[_HBM] * (2 * n),
        out_specs=(_SEM, _SEM, *[_HBM] * (2 * n), pl.BlockSpec(memory_space=pltpu.VMEM)),
        input_output_aliases={i: 2 + i for i in range(2 * n)},
        compiler_params=pltpu.CompilerParams(has_side_effects=pltpu.SideEffectType.DATAFLOW_SIDE_EFFECTING),
    )(*[pltpu.with_memory_space_constraint(s, pltpu.HBM) for s in shards], *zones)
    return out[0], out[1], out[2:2 + n], out[2 + n:2 + 2 * n], out[-1]


def _allgather_wait(send_sems, recv_sems, shards, zones, axes, after):
    n = len(shards)
    shapes = [s.shape for s in shards]

    def body(*refs):
        srcs, lands = refs[:n], refs[n:2 * n]
        outgoing, incoming = _direct_gather_copies(srcs, lands, shapes, axes, refs[2 * n], refs[2 * n + 1])
        for cp in outgoing:
            cp.wait_send()
        for cp in incoming:
            cp.wait_recv()

    out = pl.pallas_call(
        body, name="allgather_rest_wait",
        out_shape=(*[pltpu.HBM(s.shape, s.dtype) for s in shards], *[pltpu.HBM(z.shape, z.dtype) for z in zones]),
        in_specs=[_HBM] * (2 * n) + [_SEM, _SEM, _ANY], out_specs=[_HBM] * (2 * n),
        input_output_aliases={i: i for i in range(2 * n)},
        compiler_params=pltpu.CompilerParams(has_side_effects=pltpu.SideEffectType.DATAFLOW_SIDE_EFFECTING),
    )(*shards, *zones, send_sems, recv_sems, after)
    return out[n:]


def _exchange_core_halves(g):
    n, _, H, C = g.shape

    def body(g_ref, got_ref, send_sem, recv_sem):
        x, y, c = _mesh_pos()
        swap = pltpu.make_async_remote_copy(
            src_ref=g_ref.at[pl.ds(0, n), 1 - c], dst_ref=got_ref, send_sem=send_sem, recv_sem=recv_sem,
            device_id=(x, y, 1 - c), device_id_type=MESH)
        swap.start()
        swap.wait()

    return pl.pallas_call(
        body, name="exchange_core_halves", in_specs=[_ANY], out_specs=_ANY,
        out_shape=_sds((n, H, C), g.dtype),
        scratch_shapes=[pltpu.SemaphoreType.DMA, pltpu.SemaphoreType.DMA],
    )(g)


def _exchange_copy(g_ref, got_ref, send_sems, recv_sems):
    x, y, c = _mesh_pos()
    return pltpu.make_async_remote_copy(
        src_ref=g_ref.at[pl.ds(0, g_ref.shape[0]), 1 - c], dst_ref=got_ref, send_sem=send_sems.at[0],
        recv_sem=recv_sems.at[0], device_id=(x, y, 1 - c), device_id_type=MESH)


def _exchange_start(g):
    n, _, H, C = g.shape

    def body(g_ref, got_ref, send_sems, recv_sems, g_thru, got_thru, token):
        _exchange_copy(g_ref, got_ref, send_sems, recv_sems).start()
        token[...] = jnp.zeros_like(token)

    zone = pltpu.with_memory_space_constraint(lax.empty((n, H, C), g.dtype), pltpu.HBM)
    return pl.pallas_call(
        body, name="exchange_rest_start",
        out_shape=(pltpu.SemaphoreType.DMA((1,)), pltpu.SemaphoreType.DMA((1,)), pltpu.HBM(g.shape, g.dtype),
                   pltpu.HBM(zone.shape, zone.dtype), jax.ShapeDtypeStruct((8, LANES), F32)),
        in_specs=[_HBM, _HBM], out_specs=(_SEM, _SEM, _HBM, _HBM, pl.BlockSpec(memory_space=pltpu.VMEM)),
        input_output_aliases={0: 2, 1: 3},
        compiler_params=pltpu.CompilerParams(has_side_effects=pltpu.SideEffectType.DATAFLOW_SIDE_EFFECTING),
    )(pltpu.with_memory_space_constraint(g, pltpu.HBM), zone)


def _exchange_wait(send_sems, recv_sems, g, zone, after):
    def body(g_ref, got_ref, send_sems, recv_sems, after_ref, g_out, got_out):
        copy = _exchange_copy(g_ref, got_ref, send_sems, recv_sems)
        copy.wait_send()
        copy.wait_recv()

    return pl.pallas_call(
        body, name="exchange_rest_wait",
        out_shape=(pltpu.HBM(g.shape, g.dtype), pltpu.HBM(zone.shape, zone.dtype)),
        in_specs=[_HBM, _HBM, _SEM, _SEM, _ANY], out_specs=[_HBM, _HBM], input_output_aliases={0: 0, 1: 1},
        compiler_params=pltpu.CompilerParams(has_side_effects=pltpu.SideEffectType.DATAFLOW_SIDE_EFFECTING),
    )(g, zone, send_sems, recv_sems, after)


def _scatter_copies(p_ref, q_ref, send_sems, recv_sems):
    x, y, c = _mesh_pos()
    return [pltpu.make_async_remote_copy(
        src_ref=p_ref.at[2 * cx + cy], dst_ref=q_ref.at[j], send_sem=send_sems.at[j],
        recv_sem=recv_sems.at[j], device_id=(cx, cy, c), device_id_type=MESH)
        for j, (cx, cy) in enumerate(_other_chips(x, y))]


def _scatter_start(p, part):
    n, H, C = p.shape

    def body(p_ref, q_ref, send_sems, recv_sems, p_thru, q_thru, token):
        for cp in _scatter_copies(p_ref, q_ref, send_sems, recv_sems):
            cp.start()
        token[...] = jnp.zeros_like(token)

    zone = pltpu.with_memory_space_constraint(lax.empty((n - 1, H, C), p.dtype), pltpu.HBM)
    return pl.pallas_call(
        body, name="scatter_%s_start" % part,
        out_shape=(pltpu.SemaphoreType.DMA((n - 1,)), pltpu.SemaphoreType.DMA((n - 1,)), pltpu.HBM(p.shape, p.dtype),
                   pltpu.HBM(zone.shape, zone.dtype), jax.ShapeDtypeStruct((8, LANES), F32)),
        in_specs=[_HBM, _HBM], out_specs=(_SEM, _SEM, _HBM, _HBM, pl.BlockSpec(memory_space=pltpu.VMEM)),
        input_output_aliases={0: 2, 1: 3},
        compiler_params=pltpu.CompilerParams(has_side_effects=pltpu.SideEffectType.DATAFLOW_SIDE_EFFECTING),
    )(pltpu.with_memory_space_constraint(p, pltpu.HBM), zone)


def _scatter_wait(send_sems, recv_sems, p, zone, after, part):
    def body(p_ref, q_ref, send_sems, recv_sems, after_ref, p_out, q_out):
        copies = _scatter_copies(p_ref, q_ref, send_sems, recv_sems)
        for cp in copies:
            cp.wait_send()
        for cp in copies:
            cp.wait_recv()

    return pl.pallas_call(
        body, name="scatter_%s_wait" % part,
        out_shape=(pltpu.HBM(p.shape, p.dtype), pltpu.HBM(zone.shape, zone.dtype)),
        in_specs=[_HBM, _HBM, _SEM, _SEM, _ANY], out_specs=[_HBM, _HBM], input_output_aliases={0: 0, 1: 1},
        compiler_params=pltpu.CompilerParams(has_side_effects=pltpu.SideEffectType.DATAFLOW_SIDE_EFFECTING),
    )(p, zone, send_sems, recv_sems, after)


N_DEVICES = 8


def _allreduce_row(v):
    C = v.shape[1]

    def body(v_ref, out_ref, rows, send_sems, recv_sems):
        x, y, c = _mesh_pos()
        me = 4 * x + 2 * y + c
        rows[pl.ds(me, 1)] = v_ref[...].reshape(1, 1, C)
        copies = []
        for m in range(1, N_DEVICES):
            peer = (x ^ (m >> 2), y ^ ((m >> 1) & 1), c ^ (m & 1))
            copies.append(pltpu.make_async_remote_copy(
                src_ref=rows.at[me], dst_ref=rows.at[me], send_sem=send_sems.at[m - 1],
                recv_sem=recv_sems.at[m - 1], device_id=peer, device_id_type=MESH))
        for cp in copies:
            cp.start()
        for m, cp in enumerate(copies, start=1):
            cp.wait_send()
            pltpu.make_async_remote_copy(
                src_ref=rows.at[me], dst_ref=rows.at[me ^ m], send_sem=send_sems.at[m - 1],
                recv_sem=recv_sems.at[m - 1], device_id=(x, y, c), device_id_type=MESH).wait_recv()
        total = rows[0]
        for d in range(1, N_DEVICES):
            total = total + rows[d]
        out_ref[...] = total

    vmem = pl.BlockSpec(memory_space=pltpu.VMEM)
    return pl.pallas_call(
        body, name="allreduce_row", in_specs=[vmem], out_specs=vmem, out_shape=_sds((1, C), F32),
        scratch_shapes=[pltpu.VMEM((N_DEVICES, 1, C), F32), pltpu.SemaphoreType.DMA((N_DEVICES - 1,)),
                        pltpu.SemaphoreType.DMA((N_DEVICES - 1,))],
    )(v)


def _share_core_halves(r2):
    _, H, C = r2.shape

    def body(r_ref, out_ref, send_sem, recv_sem):
        x, y, c = _mesh_pos()
        send = pltpu.make_async_remote_copy(
            src_ref=r_ref.at[c], dst_ref=out_ref.at[c], send_sem=send_sem, recv_sem=recv_sem,
            device_id=(x, y, 1 - c), device_id_type=MESH)
        send.start()
        send.wait_send()
        pltpu.make_async_remote_copy(
            src_ref=r_ref.at[c], dst_ref=out_ref.at[1 - c], send_sem=send_sem, recv_sem=recv_sem,
            device_id=(x, y, 1 - c), device_id_type=MESH).wait_recv()

    return pl.pallas_call(
        body, name="share_core_halves", in_specs=[_ANY], out_specs=_ANY,
        out_shape=_sds(r2.shape, r2.dtype), input_output_aliases={0: 0},
        scratch_shapes=[pltpu.SemaphoreType.DMA, pltpu.SemaphoreType.DMA],
    )(r2)


def _place():
    x, y, c = _mesh_pos()
    return jnp.stack([c, 2 * x + y]).astype(jnp.int32)


def _sum_pair(g, got, place):
    n, _, H, C = g.shape

    def body(place_ref, a_ref, b_ref, o_ref):
        o_ref[...] = (a_ref[...] + b_ref[...]).astype(BF16)

    spec = pl.BlockSpec((1, FLAT_TILE, C), lambda s, i, pr: (s, i, 0))
    return pl.pallas_call(
        body, name="sum_core_pair",
        grid_spec=pltpu.PrefetchScalarGridSpec(
            num_scalar_prefetch=1, grid=(n, H // FLAT_TILE),
            in_specs=[pl.BlockSpec((1, None, FLAT_TILE, C), lambda s, i, pr: (s, pr[0], i, 0)), spec],
            out_specs=spec),
        out_shape=_sds((n, H, C), BF16),
        compiler_params=_params(("parallel", "parallel")))(place, g, got)


def _sum_chips(p, q, place):
    n, H, C = p.shape

    def body(place_ref, own_ref, qx_ref, qy_ref, qxy_ref, o_ref):
        mine = place_ref[1]
        own, qx, qy, qxy = (t[0].astype(F32) for t in (own_ref, qx_ref, qy_ref, qxy_ref))

        def term(s):
            rel = jnp.full(own.shape, mine ^ s, jnp.int32)
            return jnp.where(rel == 0, own, jnp.where(rel == 2, qx, jnp.where(rel == 1, qy, qxy)))

        o_ref[0] = ((term(0) + term(1)) + term(2)) + term(3)

    qspec = lambda j: pl.BlockSpec((1, FLAT_TILE, C), lambda i, pr: (j, i, 0))
    return pl.pallas_call(
        body, name="sum_chips",
        grid_spec=pltpu.PrefetchScalarGridSpec(
            num_scalar_prefetch=1, grid=(H // FLAT_TILE,),
            in_specs=[pl.BlockSpec((1, FLAT_TILE, C), lambda i, pr: (pr[1], i, 0)), qspec(0), qspec(1), qspec(2)],
            out_specs=pl.BlockSpec((1, FLAT_TILE, C), lambda i, pr: (pr[0], i, 0))),
        out_shape=_sds((2, H, C), F32),
        compiler_params=_params(("parallel",)))(place, p, q, q, q)


ADAMW_BLOCK = 1 << 18


def _adamw(w, g, m, v):
    shape = w.shape
    C = shape[-1]
    R = _size(shape) // C
    tr = R
    while tr * C > ADAMW_BLOCK and tr % 16 == 0:
        tr //= 2
    w, g, m, v = (t.reshape(R, C) for t in (w, g, m, v))

    def body(w_ref, g_ref, m_ref, v_ref, d_ref, nm_ref, nv_ref):
        gv = g_ref[...]
        nm = ADAM_B1 * m_ref[...] + (1.0 - ADAM_B1) * gv
        nv = ADAM_B2 * v_ref[...] + (1.0 - ADAM_B2) * (gv * gv)
        m_hat = nm / (1.0 - ADAM_B1 ** ADAM_STEP)
        v_hat = nv / (1.0 - ADAM_B2 ** ADAM_STEP)
        d_ref[...] = -ADAM_LR * (m_hat / (jnp.sqrt(v_hat) + ADAM_EPS) + ADAM_WD * w_ref[...])
        nm_ref[...] = nm
        nv_ref[...] = nv

    spec = pl.BlockSpec((tr, C), lambda i: (i, 0))
    outs = pl.pallas_call(body, name="adamw", grid=(R // tr,), in_specs=[spec] * 4, out_specs=[spec] * 3,
                          out_shape=[_sds((R, C), F32)] * 3, compiler_params=_params(("parallel",)))(w, g, m, v)
    return tuple(t.reshape(shape) for t in outs)


def _chip_partials(gfull, place):
    n, R, C = gfull.shape
    g4 = gfull.reshape(n, 2, R // 2, C)
    return _sum_pair(g4, _exchange_core_halves(g4), place)


def _finish_reduce(p, q, place):
    r2 = _share_core_halves(_sum_chips(p, q, place))
    return r2.reshape(2 * r2.shape[1], r2.shape[2])


def _local_step(x, p, positions, loss_target, pre_g, post_g, w, later_weights, later_grads, later_grads_arrived,
                first_grads):
    Bl, S, _ = x.shape
    T = Bl * S
    cos, sin = _rope_tables(positions)
    xs = x.reshape(T, D_MODEL)
    saved = []
    for i in range(DEPTH):
        if i == 1:
            later_weights(xs)
        j = i // 2
        g_pre, g_post = pre_g[i:i + 1], post_g[i:i + 1]
        h = _rmsnorm_fwd(xs, g_pre)
        st = {"x": xs, "h": h}
        if i % 2 == 0:
            proj = _mm(h, w["attn_w_in"][j], name="attn_in")
            res = [_attn_fwd(proj, cos, sin, g, Bl, S) for g in range(N_GROUPS)]
            a, o, lse = _attn_combine([r[0] for r in res], [r[1] for r in res], proj)
            w_out = w["attn_w_out"][j]
            st.update(proj=proj, a=a, o=o, lse=lse, qkv=[r[2] for r in res])
        else:
            w_ab, w_z = w["conv_w_in"][j][:, :2 * D_MODEL], w["conv_w_in"][j][:, 2 * D_MODEL:]
            ab = _mm(h, w_ab, out_dtype=BF16, name="conv_in_ab")
            z = _mm(h, w_z, out_dtype=BF16, name="conv_in_z")
            dw = jnp.pad(w["conv_dw_w"][j], ((0, 1), (0, 0)))
            u1, a = _conv_fwd(ab, z, dw, w["conv_dw_b"][j:j + 1], w["conv_ln_g"][j:j + 1],
                              w["conv_ln_b"][j:j + 1], Bl, S)
            w_out = w["conv_w_out"][j]
            st.update(w_ab=w_ab, w_z=w_z, ab=ab, z=z, dw=dw, u1=u1, a=a)
        y, x1 = _mm_rows(a, w_out, [xs, g_post], _post_epilogue, (F32, F32), "branch_out_post")
        pi = p[i].reshape(T, PLE_DIM)
        pe = _mm(pi, w["ple_w_proj"][i], name="ple_proj")
        gl, xs = _mm_rows(x1, w["ple_w_gate"][i], [pe], _ple_epilogue, (F32, F32), "ple_gate_fwd")
        st.update(y=y, x1=x1, pi=pi, pe=pe, gl=gl)
        saved.append(st)

    sq, dx = _loss_fwd_bwd(xs, loss_target.reshape(T, D_MODEL))

    grads = {n: [None] * shape[0] for n, shape, _ in _LAYOUT}
    for i in reversed(range(DEPTH)):
        j = i // 2
        st = saved[i]
        g_pre, g_post = pre_g[i:i + 1], post_g[i:i + 1]
        if i == 0:
            rest = {n: jnp.stack(v[1:] if n in _FIRST_LAYER_PARAMS else v) for n, v in grads.items()}
            g_post = g_post + later_grads(rest)[0, 0]
        dpe, dgl, dx1, dy, dg_post = _ple_post_bwd(dx, st["pe"], st["gl"], w["ple_w_gate"][i], st["y"], g_post)
        grads["ple_w_proj"][i] = _mm(st["pi"], dpe, ta=True, name="ple_proj_wgrad")
        grads["ple_w_gate"][i] = _mm(st["x1"], dgl, ta=True, name="ple_gate_wgrad")
        grads["post_norm_g"][i] = dg_post[0]
        if i % 2 == 0:
            grads["attn_w_out"][j] = _mm(st["a"], dy, ta=True, name="attn_out_wgrad")
            da = _mm(dy, w["attn_w_out"][j], tb=True, name="attn_out_dgrad")
            do, dproj = _gate_bwd(da, st["o"], st["proj"])
            cos_b = cos + later_grads_arrived(do)[0, 0] if i == 0 else cos
            for g in range(N_GROUPS):
                dproj = _attn_bwd(st["qkv"][g], cos_b, sin, do, st["o"], st["lse"], dproj, g, Bl, S)
            grads["attn_w_in"][j] = _mm(st["h"], dproj, ta=True, name="attn_in_wgrad")
            if i == 0:
                first = {n: jnp.stack(grads[n][:1]) for n in _FIRST_LAYER_PARAMS if n != "pre_norm_g"}
                g_pre = g_pre + first_grads(first)[0, 0]
            dx, dg_pre = _dgrad_pre_bwd(dproj, w["attn_w_in"][j], st["x"], g_pre, dx1, None, 1024,
                                        "attn_in_dgrad_pre", tm=1024)
        else:
            grads["conv_w_out"][j] = _mm(st["a"], dy, ta=True, name="conv_out_wgrad")
            da2 = _mm(dy, w["conv_w_out"][j], tb=True, name="conv_out_dgrad")
            du1, dz, dln_g, dln_b = _conv_norm_bwd(da2, st["z"], st["u1"], w["conv_ln_g"][j:j + 1],
                                                   w["conv_ln_b"][j:j + 1])
            dab, ddw, ddb = _conv_bwd(st["ab"], du1, st["dw"], Bl, S)
            dh = _mm(dz, st["w_z"], tb=True, name="conv_in_z_dgrad")
            dx, dg_pre = _dgrad_pre_bwd(dab, st["w_ab"], st["x"], g_pre, dx1, dh, 2048, "conv_in_dgrad_pre")
            dw_ab = _mm(st["h"], dab, ta=True, name="conv_in_ab_wgrad")
            dw_z = _mm(st["h"], dz, ta=True, name="conv_in_z_wgrad")
            grads["conv_w_in"][j] = jnp.concatenate([dw_ab, dw_z], axis=1)
            grads["conv_dw_w"][j] = ddw[:CONV_WIDTH]
            grads["conv_dw_b"][j] = ddb[0]
            grads["conv_ln_g"][j] = dln_g[0]
            grads["conv_ln_b"][j] = dln_b[0]
        grads["pre_norm_g"][i] = dg_pre[0]
    return sq, dx.reshape(Bl, S, D_MODEL), grads["pre_norm_g"][0][None]


_NAMES = tuple(n for n, _, _ in _LAYOUT)


def kernel(x, p, positions, pre_norm_g, post_norm_g, attn_w_in, attn_w_out, conv_w_in, conv_dw_w, conv_dw_b, conv_ln_g, conv_ln_b, conv_w_out, ple_w_proj, ple_w_gate, loss_target, m_pre_norm_g, m_post_norm_g, m_attn_w_in, m_attn_w_out, m_conv_w_in, m_conv_dw_w, m_conv_dw_b, m_conv_ln_g, m_conv_ln_b, m_conv_w_out, m_ple_w_proj, m_ple_w_gate, v_pre_norm_g, v_post_norm_g, v_attn_w_in, v_attn_w_out, v_conv_w_in, v_conv_dw_w, v_conv_dw_b, v_conv_ln_g, v_conv_ln_b, v_conv_w_out, v_ple_w_proj, v_ple_w_gate):
    w_loc = dict(zip(_NAMES, (pre_norm_g, post_norm_g, attn_w_in, attn_w_out, conv_w_in, conv_dw_w, conv_dw_b,
                              conv_ln_g, conv_ln_b, conv_w_out, ple_w_proj, ple_w_gate)))
    m_loc = dict(zip(_NAMES, (m_pre_norm_g, m_post_norm_g, m_attn_w_in, m_attn_w_out, m_conv_w_in, m_conv_dw_w,
                              m_conv_dw_b, m_conv_ln_g, m_conv_ln_b, m_conv_w_out, m_ple_w_proj, m_ple_w_gate)))
    v_loc = dict(zip(_NAMES, (v_pre_norm_g, v_post_norm_g, v_attn_w_in, v_attn_w_out, v_conv_w_in, v_conv_dw_w,
                              v_conv_dw_b, v_conv_ln_g, v_conv_ln_b, v_conv_w_out, v_ple_w_proj, v_ple_w_gate)))

    bf = {n: w_loc[n].astype(BF16) for n in _MATMUL_WEIGHTS}
    axes = [_AXIS[n] for n in _MATMUL_WEIGHTS]
    first, small = _allgather_weights([bf[n][:1] for n in _FIRST_LAYER], [_AXIS[n] for n in _FIRST_LAYER],
                                      _stack_small(w_loc))
    send_sems, recv_sems, shards, zones, token = _allgather_start(
        [bf[n][1:] if n in _FIRST_LAYER else bf[n] for n in _MATMUL_WEIGHTS], axes)
    w_full = dict({n: [full[0]] for n, full in zip(_FIRST_LAYER, first)}, **_unstack_small(small))

    def later_weights(after):
        for n, full in zip(_MATMUL_WEIGHTS, _allgather_wait(send_sems, recv_sems, shards, zones, axes, after)):
            w_full[n] = w_full[n] + [full[l] for l in range(full.shape[0])] if n in _FIRST_LAYER else full

    place = _place()
    rest_shapes, first_shapes = _part_shapes(False), _part_shapes(True)
    rest_halves, rest_flight, first_flight = [], [], []

    def later_grads(grads):
        gfull = _pack_full_grads(grads, rest_shapes)
        rest_halves.extend(_exchange_start(gfull.reshape(N_CHIPS, 2, gfull.shape[1] // 2, FLAT_COLS)))
        return rest_halves[4]

    def later_grads_arrived(after):
        rest_flight.extend(_scatter_start(_sum_pair(*_exchange_wait(*rest_halves[:4], after), place), "rest"))
        return rest_flight[4]

    def first_grads(grads):
        first_flight.extend(_scatter_start(_chip_partials(_pack_full_grads(grads, first_shapes), place), "first"))
        return first_flight[4]

    sq, grad_x, dg_pre0 = _local_step(x, p, positions, loss_target, pre_norm_g + token[0, 0], post_norm_g,
                                      w_full, later_weights, later_grads, later_grads_arrived, first_grads)
    loss = lax.psum(sq[0, 0] * (0.5 / D_MODEL), ("x", "y", "c"))

    p_rest, q_rest = _scatter_wait(*rest_flight[:4], grad_x, "rest")
    g_rest = _unpack_f32(_finish_reduce(p_rest, q_rest, place), rest_shapes)
    p_first, q_first = _scatter_wait(*first_flight[:4], grad_x, "first")
    g_first = _unpack_f32(_finish_reduce(p_first, q_first, place), first_shapes)
    g_first["pre_norm_g"] = _allreduce_row(dg_pre0)
    g_out = {n: jnp.concatenate([g_first[n], g_rest[n]]) if n in g_first and n in g_rest
             else g_rest.get(n, g_first.get(n)) for n in _NAMES}
    updates = {n: _adamw(w_loc[n], g_out[n], m_loc[n], v_loc[n]) for n in _NAMES}
    d_out, m_out, v_out = ({n: updates[n][k] for n in _NAMES} for k in range(3))
    return (loss, grad_x, *[g_out[n] for n in _NAMES], *[d_out[n] for n in _NAMES],
            *[m_out[n] for n in _NAMES], *[v_out[n] for n in _NAMES])
```

```python
import math

import jax
import jax.numpy as jnp
from jax import lax
from jax.experimental import pallas as pl
from jax.experimental.pallas import tpu as pltpu

F32 = jnp.float32
BF16 = jnp.bfloat16

D_MODEL = 1024
DEPTH = 4
PLE_DIM = 256
HEAD_DIM = 64
WIN_DIL = ((128, 1), (512, 4), (2048, 16))
N_GROUPS = 3
N_BACK = 128
BLOCK_UNROLL = 8
RESIDUES_TOGETHER = 2
ROPE_THETA = 10000.0
CONV_WIDTH = 31
CONV_HALO = 32
RMS_EPS = 1e-6
LN_EPS = 1e-5
NEG_INF = -1e30
ADAM_LR, ADAM_B1, ADAM_B2, ADAM_EPS, ADAM_WD, ADAM_STEP = 0.001, 0.9, 0.999, 1e-08, 0.01, 10

LANES = 128
N_CHIPS = 4
VMEM_LIMIT = 48 * 1024 * 1024
VMEM_LIMIT_ATTN = 56 * 1024 * 1024
FLAT_COLS = 256
FLAT_TILE = 2048
FLAT_ROW_ALIGN = 16
PROJ_COLS = (3 * N_GROUPS + 1) * D_MODEL
HEAD_PAIRS = D_MODEL // LANES

MESH = pl.DeviceIdType.MESH


def _params(sem=None, vmem=VMEM_LIMIT):
    return pltpu.CompilerParams(dimension_semantics=sem, vmem_limit_bytes=vmem)


def _sigmoid(v):
    return 1.0 / (1.0 + jnp.exp(-v))


def _mm(a, b, *, ta=False, tb=False, out_dtype=F32, tm=1024, tn=1024, tk=1024, name="mm"):
    if ta:
        K, M = a.shape
    else:
        M, K = a.shape
    if tb:
        N, K2 = b.shape
    else:
        K2, N = b.shape
    assert K == K2, (a.shape, b.shape)
    tm, tn, tk = min(tm, M), min(tn, N), min(tk, K)
    assert M % tm == 0 and N % tn == 0 and K % tk == 0
    nk = K // tk
    dims = (((0 if ta else 1,), (1 if tb else 0,)), ((), ()))

    def body(a_ref, b_ref, o_ref, *scratch):
        k = pl.program_id(2)
        part = lax.dot_general(a_ref[...].astype(BF16), b_ref[...].astype(BF16), dims, preferred_element_type=F32)
        if nk == 1:
            o_ref[...] = part.astype(out_dtype)
        else:
            acc_ref, = scratch

            @pl.when(k == 0)
            def _():
                acc_ref[...] = part

            @pl.when((k > 0) & (k < nk - 1))
            def _():
                acc_ref[...] += part

            @pl.when(k == nk - 1)
            def _():
                o_ref[...] = (acc_ref[...] + part).astype(out_dtype)

    a_spec = pl.BlockSpec((tk, tm), lambda i, j, k: (k, i)) if ta else pl.BlockSpec((tm, tk), lambda i, j, k: (i, k))
    b_spec = pl.BlockSpec((tn, tk), lambda i, j, k: (j, k)) if tb else pl.BlockSpec((tk, tn), lambda i, j, k: (k, j))
    return pl.pallas_call(
        body, name=name, grid=(M // tm, N // tn, nk),
        in_specs=[a_spec, b_spec], out_specs=pl.BlockSpec((tm, tn), lambda i, j, k: (i, j)),
        out_shape=jax.ShapeDtypeStruct((M, N), out_dtype),
        scratch_shapes=[pltpu.VMEM((tm, tn), F32)] if nk > 1 else [],
        compiler_params=_params(("parallel", "parallel", "arbitrary")),
    )(a, b)


def _mm_rows(a, b, extras, epilogue, out_dtypes, name, tm=512):
    M, K = a.shape
    N = b.shape[1]
    n_ex = len(extras)

    def body(*refs):
        a_ref, b_ref = refs[:2]
        av = a_ref[...]
        acc = jnp.dot(av.astype(BF16), b_ref[...].astype(BF16), preferred_element_type=F32)
        results = epilogue(acc, av, *[e[...] for e in refs[2:2 + n_ex]])
        for o_ref, r in zip(refs[2 + n_ex:], results):
            o_ref[...] = r.astype(o_ref.dtype)

    tile = pl.BlockSpec((tm, N), lambda i: (i, 0))
    in_specs = [pl.BlockSpec((tm, K), lambda i: (i, 0)), pl.BlockSpec((K, N), lambda i: (0, 0))]
    in_specs += [tile if e.shape[0] == M else pl.BlockSpec((1, N), lambda i: (0, 0)) for e in extras]
    return pl.pallas_call(
        body, name=name, grid=(M // tm,), in_specs=in_specs, out_specs=[tile] * len(out_dtypes),
        out_shape=[jax.ShapeDtypeStruct((M, N), dt) for dt in out_dtypes],
        compiler_params=_params(("parallel",)),
    )(a, b, *extras)


ROW_TILE = 512


def _rows(w=D_MODEL, cb=0, tr=ROW_TILE):
    return pl.BlockSpec((tr, w), lambda i: (i, cb))


def _full(shape):
    return pl.BlockSpec(shape, lambda i: (0,) * len(shape))


def _row_call(body, name, T, in_specs, out_specs, out_shape, args, tr=ROW_TILE):
    return pl.pallas_call(body, name=name, grid=(T // tr,), in_specs=in_specs, out_specs=out_specs,
                          out_shape=out_shape, compiler_params=_params(("arbitrary",)))(*args)


def _sds(shape, dtype):
    return jax.ShapeDtypeStruct(shape, dtype)


def _rmsnorm_fwd(x, g):
    T = x.shape[0]

    def body(x_ref, g_ref, h_ref):
        xv = x_ref[...]
        r = lax.rsqrt(jnp.mean(xv * xv, axis=1, keepdims=True) + RMS_EPS)
        h_ref[...] = (xv * r * g_ref[...]).astype(BF16)

    return _row_call(body, "rmsnorm_fwd", T, [_rows(), _full((1, D_MODEL))], _rows(),
                     _sds((T, D_MODEL), BF16), (x, g))


def _post_epilogue(y, a_tile, x, g):
    del a_tile
    return y, x + y * lax.rsqrt(jnp.mean(y * y, axis=1, keepdims=True) + RMS_EPS) * g


def _ple_epilogue(gl, x1, pe):
    return gl, x1 + pe * _sigmoid(gl)


def _dgrad_pre_bwd(a, b, x, g, dx1, add, tk, name, tm=512):
    M, K = a.shape
    N = b.shape[0]
    tk = min(tk, K)
    assert M % tm == 0 and K % tk == 0 and N == D_MODEL
    nk = K // tk

    def body(*refs):
        a_ref, b_ref, x_ref, g_ref, dx1_ref = refs[:5]
        add_ref = refs[5] if add is not None else None
        dx_ref, dg_ref = refs[-3:-1] if nk > 1 else refs[-2:]
        i, k = pl.program_id(0), pl.program_id(1)

        @pl.when((i == 0) & (k == 0))
        def _():
            dg_ref[...] = jnp.zeros_like(dg_ref)

        part = lax.dot_general(a_ref[...].astype(BF16), b_ref[...].astype(BF16), _NT, preferred_element_type=F32)

        def finish(dh):
            if add is not None:
                dh = dh + add_ref[...]
            xv = x_ref[...]
            r = lax.rsqrt(jnp.mean(xv * xv, axis=1, keepdims=True) + RMS_EPS)
            xh = xv * r
            dg_ref[...] += jnp.sum(dh * xh, axis=0, keepdims=True)
            dn = dh * g_ref[...]
            dx_ref[...] = dx1_ref[...] + r * (dn - xh * jnp.mean(dn * xh, axis=1, keepdims=True))

        if nk == 1:
            finish(part)
        else:
            acc_ref = refs[-1]

            @pl.when(k == 0)
            def _():
                acc_ref[...] = part

            @pl.when((k > 0) & (k < nk - 1))
            def _():
                acc_ref[...] += part

            @pl.when(k == nk - 1)
            def _():
                finish(acc_ref[...] + part)

    tile = pl.BlockSpec((tm, N), lambda i, k: (i, 0))
    row = pl.BlockSpec((1, N), lambda i, k: (0, 0))
    in_specs = [pl.BlockSpec((tm, tk), lambda i, k: (i, k)), pl.BlockSpec((N, tk), lambda i, k: (0, k)), tile, row, tile]
    args = [a, b, x, g, dx1]
    if add is not None:
        in_specs.append(tile)
        args.append(add)
    return pl.pallas_call(
        body, name=name, grid=(M // tm, nk), in_specs=in_specs, out_specs=[tile, row],
        out_shape=[_sds((M, N), F32), _sds((1, N), F32)],
        scratch_shapes=[pltpu.VMEM((tm, N), F32)] if nk > 1 else [],
        compiler_params=_params(("arbitrary", "arbitrary"), VMEM_LIMIT_ATTN),
    )(*args)


def _ple_post_bwd(dx2, pe, gl, w_gate, y, g_post):
    T = dx2.shape[0]

    def body(d_ref, pe_ref, gl_ref, w_ref, y_ref, g_ref, dpe_ref, dgl_ref, dx1_ref, dy_ref, dg_ref):
        @pl.when(pl.program_id(0) == 0)
        def _():
            dg_ref[...] = jnp.zeros_like(dg_ref)

        dv = d_ref[...]
        sg = _sigmoid(gl_ref[...])
        dpe_ref[...] = (dv * sg).astype(BF16)
        dgl = (dv * pe_ref[...] * sg * (1.0 - sg)).astype(BF16)
        dgl_ref[...] = dgl
        dx1 = dv + lax.dot_general(dgl, w_ref[...], _NT, preferred_element_type=F32)
        dx1_ref[...] = dx1
        yv = y_ref[...]
        r = lax.rsqrt(jnp.mean(yv * yv, axis=1, keepdims=True) + RMS_EPS)
        yh = yv * r
        dg_ref[...] += jnp.sum(dx1 * yh, axis=0, keepdims=True)
        dn = dx1 * g_ref[...]
        dy_ref[...] = (r * (dn - yh * jnp.mean(dn * yh, axis=1, keepdims=True))).astype(BF16)

    row = _full((1, D_MODEL))
    return _row_call(body, "ple_post_bwd", T, [_rows()] * 3 + [_full((D_MODEL, D_MODEL)), _rows(), row],
                     [_rows()] * 4 + [row],
                     [_sds((T, D_MODEL), BF16)] * 2 + [_sds((T, D_MODEL), F32), _sds((T, D_MODEL), BF16),
                                                      _sds((1, D_MODEL), F32)],
                     (dx2, pe, gl, w_gate, y, g_post))


def _loss_fwd_bwd(y, target):
    T = y.shape[0]

    def body(y_ref, t_ref, s_ref, d_ref):
        @pl.when(pl.program_id(0) == 0)
        def _():
            s_ref[...] = jnp.zeros_like(s_ref)

        e = y_ref[...] - t_ref[...]
        s_ref[...] += jnp.sum(e * e).reshape(1, 1)
        d_ref[...] = e * (1.0 / D_MODEL)

    return _row_call(body, "loss", T, [_rows()] * 2, [_full((1, 1)), _rows()],
                     [_sds((1, 1), F32), _sds((T, D_MODEL), F32)], (y, target))


def _attn_combine(outs, lses, proj):
    T = proj.shape[0]

    def body(o0, o1, o2, l0, l1, l2, z_ref, a_ref, o_ref, lse_ref):
        a0, a1, a2 = l0[...], l1[...], l2[...]
        m = jnp.maximum(jnp.maximum(a0, a1), a2)
        e0, e1, e2 = jnp.exp(a0 - m), jnp.exp(a1 - m), jnp.exp(a2 - m)
        ssum = e0 + e1 + e2
        o = (e0 * o0[...] + e1 * o1[...] + e2 * o2[...]) / ssum
        zv = z_ref[...].astype(F32)
        o_ref[...] = o
        lse_ref[...] = m + jnp.log(ssum)
        a_ref[...] = (o * zv * _sigmoid(zv)).astype(BF16)

    return _row_call(body, "attn_combine", T, [_rows()] * 6 + [_rows(cb=3 * N_GROUPS)], [_rows()] * 3,
                     [_sds((T, D_MODEL), BF16), _sds((T, D_MODEL), F32), _sds((T, D_MODEL), F32)],
                     (*outs, *lses, proj))


def _gate_bwd(da, o, proj):
    T = da.shape[0]

    def body(da_ref, o_ref, z_ref, do_ref, dz_ref):
        dv = da_ref[...]
        zv = z_ref[...]
        sg = _sigmoid(zv)
        do_ref[...] = dv * zv * sg
        dz_ref[...] = dv * o_ref[...] * sg * (1.0 + zv * (1.0 - sg))

    zcols = _rows(cb=3 * N_GROUPS)
    return _row_call(body, "gate_bwd", T, [_rows(), _rows(), zcols], [_rows(), zcols],
                     [_sds((T, D_MODEL), F32), _sds((T, PROJ_COLS), F32)], (da, o, proj))


def _rope_tables(positions):
    inv_freq = 1.0 / (ROPE_THETA ** (jnp.arange(0, HEAD_DIM, 2, dtype=F32) / HEAD_DIM))
    ang = positions.astype(F32)[..., None] * inv_freq
    cos, sin = jnp.cos(ang), jnp.sin(ang)
    return jnp.tile(cos, (1, 1, 4)), jnp.concatenate([-sin, sin, -sin, sin], axis=-1)


def _rotate_half_partner(t):
    lane = lax.broadcasted_iota(jnp.int32, t.shape, 1)
    return jnp.where((lane % HEAD_DIM) < HEAD_DIM // 2,
                     pltpu.roll(t, LANES - HEAD_DIM // 2, 1), pltpu.roll(t, HEAD_DIM // 2, 1))


def _mask_bias(first):
    qi = lax.broadcasted_iota(jnp.int32, (N_BACK, 2 * N_BACK), 0)
    kj = lax.broadcasted_iota(jnp.int32, (N_BACK, 2 * N_BACK), 1)
    ok = (kj >= qi) & (kj <= qi + N_BACK)
    if first:
        ok = ok & (kj >= N_BACK)
    return jnp.where(ok, 0.0, NEG_INF).astype(F32)


def _stack_heads(t, head0):
    zero = jnp.zeros_like(t)
    return jnp.concatenate([jnp.where(head0, t, zero), jnp.where(head0, zero, t)], axis=0)


def _unstack_heads(t2, head0):
    return jnp.where(head0, t2[:N_BACK], t2[N_BACK:])


def _block_loop(nb, block):
    first, rest = _mask_bias(True), _mask_bias(False)
    first, rest = jnp.concatenate([first, first], axis=0), jnp.concatenate([rest, rest], axis=0)
    if nb <= BLOCK_UNROLL:
        for n in range(nb):
            block(n, first if n == 0 else rest)
        return

    def step(n, carry):
        block(n, jnp.where(n == 0, first, rest))
        return carry

    lax.fori_loop(0, nb, step, 0, unroll=BLOCK_UNROLL)


def _for(count, body, unroll_fully):
    if unroll_fully:
        for i in range(count):
            body(i)
    else:
        lax.fori_loop(0, count, lambda i, carry: (body(i), carry)[1], 0, unroll=4)


def _residues_together(nb):
    return min(RESIDUES_TOGETHER, max(1, BLOCK_UNROLL // nb))


def _residue_loop(d, nb, residue):
    together = _residues_together(nb)
    assert d % together == 0

    def group(i, carry):
        for u in range(together):
            residue(i * together + u, u)
        return carry

    lax.fori_loop(0, d // together, group, 0)


_NT = (((1,), (1,)), ((), ()))
_TN = (((0,), (0,)), ((), ()))


def _residue_rows(r, i, d):
    start = r + i * (N_BACK * d)
    if d == 1:
        return pl.ds(pl.multiple_of(start, N_BACK), N_BACK)
    return pl.ds(start, N_BACK, stride=d)


def _seq_rows(i):
    return pl.ds(pl.multiple_of(i * N_BACK, N_BACK), N_BACK)


def _rows_at(base, i, size=N_BACK):
    return pl.ds(pl.multiple_of(base + i * N_BACK, N_BACK), size)


def _attn_fwd(proj, cos, sin, group, Bl, S):
    d = WIN_DIL[group][1]
    L = S // d
    nb = L // N_BACK
    P = L + N_BACK
    assert WIN_DIL[group][0] // d == N_BACK and L % N_BACK == 0

    def body(q_ref, k_ref, v_ref, cos_ref, sin_ref, o_ref, lse_ref, qr, kr, vp):
        head0 = lax.broadcasted_iota(jnp.int32, (1, LANES), 1) < HEAD_DIM
        zeros = jnp.zeros((N_BACK, LANES), BF16)

        def residue(r, u):
            del u
            qbase, kbase = r * L, r * P
            kr[_rows_at(kbase, 0), :] = zeros
            vp[_rows_at(kbase, 0), :] = zeros

            def rope(i):
                rows = _residue_rows(r, i, d)
                cs, sn = cos_ref[rows, :], sin_ref[rows, :]
                q, k = q_ref[rows, :], k_ref[rows, :]
                qr[_rows_at(qbase, i), :] = ((q * cs + _rotate_half_partner(q) * sn)
                                            * (HEAD_DIM ** -0.5)).astype(BF16)
                kr[_rows_at(kbase, i + 1), :] = (k * cs + _rotate_half_partner(k) * sn).astype(BF16)
                vp[_rows_at(kbase, i + 1), :] = v_ref[rows, :].astype(BF16)

            _for(nb, rope, nb <= BLOCK_UNROLL)

            def block(n, bias):
                win = _rows_at(kbase, n, 2 * N_BACK)
                q2, kw, vw = _stack_heads(qr[_rows_at(qbase, n), :], head0), kr[win, :], vp[win, :]
                s = lax.dot_general(q2, kw, _NT, preferred_element_type=F32) + bias
                m = jnp.max(s, axis=1, keepdims=True)
                p = jnp.exp(s - m)
                l = jnp.sum(p, axis=1, keepdims=True)
                pv = jnp.dot(p.astype(BF16), vw, preferred_element_type=F32)
                rows = _residue_rows(r, n, d)
                o_ref[rows, :] = _unstack_heads(pv * (1.0 / l), head0)
                lse_ref[rows, :] = _unstack_heads((m + jnp.log(l)) + jnp.zeros((2 * N_BACK, LANES), F32), head0)

            _block_loop(nb, block)

        _residue_loop(d, nb, residue)

    act = pl.BlockSpec((None, S, LANES), lambda b, hp: (b, 0, hp))
    tab = pl.BlockSpec((None, S, LANES), lambda b, hp: (b, 0, 0))
    col = lambda which: pl.BlockSpec((None, S, LANES),
                                     lambda b, hp: (b, 0, (which * N_GROUPS + group) * HEAD_PAIRS + hp))
    seq = lambda rows: pl.BlockSpec((None, None, rows, LANES), lambda b, hp: (b, hp, 0, 0))
    p3 = proj.reshape(Bl, S, PROJ_COLS)
    o, lse, qr, kr, vp = pl.pallas_call(
        body, name="attn_fwd_g%d" % group, grid=(Bl, HEAD_PAIRS),
        in_specs=[col(0), col(1), col(2), tab, tab], out_specs=[act, act, seq(S), seq(d * P), seq(d * P)],
        out_shape=[_sds((Bl, S, D_MODEL), F32)] * 2 + [_sds((Bl, HEAD_PAIRS, S, LANES), BF16)]
        + [_sds((Bl, HEAD_PAIRS, d * P, LANES), BF16)] * 2,
        compiler_params=_params(("parallel", "arbitrary"), VMEM_LIMIT_ATTN),
    )(p3, p3, p3, cos, sin)
    return o.reshape(Bl * S, D_MODEL), lse.reshape(Bl * S, D_MODEL), (qr, kr, vp)


def _attn_bwd(saved, cos, sin, do, o, lse, dproj, group, Bl, S):
    d = WIN_DIL[group][1]
    L = S // d
    nb = L // N_BACK
    P = L + N_BACK
    steps = Bl * HEAD_PAIRS

    def body(qr, kr, vp, cos_ref, sin_ref, do_ref, o_ref, lse_ref, dproj_in, dproj_ref,
             dk_accs, dv_accs, stage, sems):
        del dproj_in
        head0 = lax.broadcasted_iota(jnp.int32, (1, LANES), 1) < HEAD_DIM
        b, hp = pl.program_id(0), pl.program_id(1)
        step = b * HEAD_PAIRS + hp
        slot = step % 2
        dq_s, dk_s, dv_s = stage.at[slot, 0], stage.at[slot, 1], stage.at[slot, 2]

        def copies(which_slot):
            out = []
            for which in range(3):
                col = ((which * N_GROUPS + group) * HEAD_PAIRS + hp) * LANES
                out.append(pltpu.make_async_copy(
                    stage.at[which_slot, which], dproj_ref.at[b, :, pl.ds(pl.multiple_of(col, LANES), LANES)],
                    sems.at[which_slot, which]))
            return out

        @pl.when(step >= 2)
        def _():
            for cp in copies(slot):
                cp.wait()

        def residue(r, u):
            qbase, kbase = r * L, r * P
            dk_acc, dv_acc = dk_accs.at[u], dv_accs.at[u]
            dk_acc[...] = jnp.zeros_like(dk_acc)
            dv_acc[...] = jnp.zeros_like(dv_acc)

            def block(n, bias):
                win = pl.ds(pl.multiple_of(n * N_BACK, N_BACK), 2 * N_BACK)
                kwin = _rows_at(kbase, n, 2 * N_BACK)
                rows = _residue_rows(r, n, d)
                q2, kw, vw = _stack_heads(qr[_rows_at(qbase, n), :], head0), kr[kwin, :], vp[kwin, :]
                dof = do_ref[rows, :]
                do2 = _stack_heads(dof.astype(BF16), head0)
                lse_b = lse_ref[rows, :]
                lse2 = jnp.concatenate([lse_b[:, 0:1], lse_b[:, HEAD_DIM:HEAD_DIM + 1]], axis=0)
                dsum = _stack_heads(dof * o_ref[rows, :], head0)
                delta = jnp.sum(dsum, axis=1, keepdims=True)
                s = lax.dot_general(q2, kw, _NT, preferred_element_type=F32) + bias
                p = jnp.exp(s - lse2)
                dp = lax.dot_general(do2, vw, _NT, preferred_element_type=F32)
                ds = (p * (dp - delta)).astype(BF16)
                dq = _unstack_heads(jnp.dot(ds, kw, preferred_element_type=F32), head0) * (HEAD_DIM ** -0.5)
                cs, sn = cos_ref[rows, :], sin_ref[rows, :]
                dq_s[rows, :] = dq * cs + _rotate_half_partner(dq * sn)
                dk_acc[win, :] += lax.dot_general(ds, q2, _TN, preferred_element_type=F32)
                dv_acc[win, :] += lax.dot_general(p.astype(BF16), do2, _TN, preferred_element_type=F32)

            _block_loop(nb, block)

            def finish(i):
                rows = _residue_rows(r, i, d)
                cs, sn = cos_ref[rows, :], sin_ref[rows, :]
                dk = dk_acc[_seq_rows(i + 1), :]
                dk_s[rows, :] = dk * cs + _rotate_half_partner(dk * sn)
                dv_s[rows, :] = dv_acc[_seq_rows(i + 1), :]

            _for(nb, finish, nb <= BLOCK_UNROLL)

        _residue_loop(d, nb, residue)
        for cp in copies(slot):
            cp.start()

        @pl.when(step == steps - 1)
        def _():
            if steps > 1:
                for cp in copies(1 - slot):
                    cp.wait()
            for cp in copies(slot):
                cp.wait()

    act = pl.BlockSpec((None, S, LANES), lambda b, hp: (b, 0, hp))
    tab = pl.BlockSpec((None, S, LANES), lambda b, hp: (b, 0, 0))
    seq = lambda rows: pl.BlockSpec((None, None, rows, LANES), lambda b, hp: (b, hp, 0, 0))
    view = lambda t: t.reshape(Bl, S, D_MODEL)
    out = pl.pallas_call(
        body, name="attn_bwd_g%d" % group, grid=(Bl, HEAD_PAIRS),
        in_specs=[seq(S), seq(d * P), seq(d * P), tab, tab, act, act, act, _ANY], out_specs=_ANY,
        out_shape=_sds((Bl, S, PROJ_COLS), F32), input_output_aliases={8: 0},
        scratch_shapes=[pltpu.VMEM((_residues_together(nb), P, LANES), F32),
                        pltpu.VMEM((_residues_together(nb), P, LANES), F32),
                        pltpu.VMEM((2, 3, S, LANES), F32), pltpu.SemaphoreType.DMA((2, 3))],
        compiler_params=_params(("arbitrary", "arbitrary"), VMEM_LIMIT_ATTN),
    )(*saved, cos, sin, view(do), view(o), view(lse), dproj.reshape(Bl, S, PROJ_COLS))
    return out.reshape(Bl * S, PROJ_COLS)


CONV_TILE = 256
CONV_CHUNK = 64
SUBLANES = 8
CONV_SHIFT_ROWS = CONV_TILE + CONV_HALO - SUBLANES


def _fill_shifted(shifted, ext, cs):
    for k in range(1, SUBLANES):
        shifted[k - 1] = ext[pl.ds(k, CONV_SHIFT_ROWS), cs]


def _shifted_rows(shifted, ext, cs, off):
    k = off % SUBLANES
    if k == 0:
        return ext[pl.ds(off, CONV_CHUNK), cs]
    return shifted[k - 1, pl.ds(off - k, CONV_CHUNK), :]


def _conv_fwd(proj, z, dw, dwb, ln_g, ln_b, Bl, S):
    tr = CONV_TILE
    nj = S // tr
    hb = tr // CONV_HALO

    def body(a_ref, b_ref, ah_ref, bh_ref, z_ref, dw_ref, dwb_ref, g_ref, bb_ref, u1_ref, out_ref, ext, shifted):
        j = pl.program_id(1)
        halo = ah_ref[0].astype(F32) * _sigmoid(bh_ref[0].astype(F32))
        ext[pl.ds(0, CONV_HALO), :] = jnp.where(j > 0, halo, 0.0)
        ext[pl.ds(CONV_HALO, tr), :] = a_ref[0].astype(F32) * _sigmoid(b_ref[0].astype(F32))

        def cols(c, carry):
            cs = pl.ds(pl.multiple_of(c * LANES, LANES), LANES)
            _fill_shifted(shifted, ext, cs)
            for rc in range(tr // CONV_CHUNK):
                acc = jnp.zeros((CONV_CHUNK, LANES), F32)
                for w in range(CONV_WIDTH):
                    off = rc * CONV_CHUNK + CONV_HALO - (CONV_WIDTH - 1) + w
                    acc = acc + dw_ref[pl.ds(w, 1), cs] * _shifted_rows(shifted, ext, cs, off)
                u1_ref[0, pl.ds(rc * CONV_CHUNK, CONV_CHUNK), cs] = acc + dwb_ref[:, cs]
            return carry

        lax.fori_loop(0, D_MODEL // LANES, cols, 0)
        u1 = u1_ref[0]
        mu = jnp.mean(u1, axis=1, keepdims=True)
        xc = u1 - mu
        rstd = lax.rsqrt(jnp.mean(xc * xc, axis=1, keepdims=True) + LN_EPS)
        u2 = xc * rstd * g_ref[...] + bb_ref[...]
        zv = z_ref[0].astype(F32)
        out_ref[0] = (u2 * _sigmoid(u2) * zv * _sigmoid(zv)).astype(BF16)

    tile = lambda cb: pl.BlockSpec((1, tr, D_MODEL), lambda b, j: (b, j, cb))
    halo = lambda cb: pl.BlockSpec((1, CONV_HALO, D_MODEL), lambda b, j: (b, jnp.maximum(j * hb - 1, 0), cb))
    par = lambda r: pl.BlockSpec((r, D_MODEL), lambda b, j: (0, 0))
    p3 = proj.reshape(Bl, S, 2 * D_MODEL)
    u1, out = pl.pallas_call(
        body, name="conv_fwd", grid=(Bl, nj),
        in_specs=[tile(0), tile(1), halo(0), halo(1), tile(0), par(32), par(1), par(1), par(1)],
        out_specs=[tile(0), tile(0)],
        out_shape=[_sds((Bl, S, D_MODEL), F32), _sds((Bl, S, D_MODEL), BF16)],
        scratch_shapes=[pltpu.VMEM((tr + CONV_HALO, D_MODEL), F32),
                        pltpu.VMEM((SUBLANES - 1, CONV_SHIFT_ROWS, LANES), F32)],
        compiler_params=_params(("parallel", "arbitrary")),
    )(p3, p3, p3, p3, z.reshape(Bl, S, D_MODEL), dw, dwb, ln_g, ln_b)
    return u1.reshape(Bl * S, D_MODEL), out.reshape(Bl * S, D_MODEL)


def _conv_norm_bwd(da2, z, u1, ln_g, ln_b):
    T = da2.shape[0]

    def body(da_ref, z_ref, u_ref, g_ref, b_ref, du_ref, dz_ref, dg_ref, db_ref):
        @pl.when(pl.program_id(0) == 0)
        def _():
            dg_ref[...] = jnp.zeros_like(dg_ref)
            db_ref[...] = jnp.zeros_like(db_ref)

        u1 = u_ref[...]
        mu = jnp.mean(u1, axis=1, keepdims=True)
        xc = u1 - mu
        rstd = lax.rsqrt(jnp.mean(xc * xc, axis=1, keepdims=True) + LN_EPS)
        nrm = xc * rstd
        u2 = nrm * g_ref[...] + b_ref[...]
        s2 = _sigmoid(u2)
        zv = z_ref[...].astype(F32)
        sz = _sigmoid(zv)
        dv = da_ref[...]
        dz_ref[...] = (dv * u2 * s2 * sz * (1.0 + zv * (1.0 - sz))).astype(BF16)
        du2 = dv * zv * sz * s2 * (1.0 + u2 * (1.0 - s2))
        dg_ref[...] += jnp.sum(du2 * nrm, axis=0, keepdims=True)
        db_ref[...] += jnp.sum(du2, axis=0, keepdims=True)
        dn = du2 * g_ref[...]
        du_ref[...] = rstd * (dn - jnp.mean(dn, axis=1, keepdims=True)
                              - nrm * jnp.mean(dn * nrm, axis=1, keepdims=True))

    return _row_call(body, "conv_norm_bwd", T,
                     [_rows(), _rows(), _rows(), _full((1, D_MODEL)), _full((1, D_MODEL))],
                     [_rows(), _rows(), _full((1, D_MODEL)), _full((1, D_MODEL))],
                     [_sds((T, D_MODEL), F32), _sds((T, D_MODEL), BF16), _sds((1, D_MODEL), F32),
                      _sds((1, D_MODEL), F32)], (da2, z, u1, ln_g, ln_b))


def _conv_bwd(proj, du1, dw, Bl, S):
    tr = CONV_TILE
    nj = S // tr
    hb = tr // CONV_HALO

    def body(a_ref, b_ref, ah_ref, bh_ref, du_ref, duh_ref, dw_ref, dab_ref, ddw_ref, ddb_ref, uext, dext, du0,
             ushift, dshift, ddw8):
        first = (pl.program_id(0) == 0) & (pl.program_id(1) == 0)
        last = (pl.program_id(0) == Bl - 1) & (pl.program_id(1) == nj - 1)
        j = pl.program_id(1)

        @pl.when(first)
        def _():
            ddw8[...] = jnp.zeros_like(ddw8)
            ddb_ref[...] = jnp.zeros_like(ddb_ref)

        halo = ah_ref[0].astype(F32) * _sigmoid(bh_ref[0].astype(F32))
        uext[pl.ds(0, CONV_HALO), :] = jnp.where(j > 0, halo, 0.0)
        av = a_ref[0].astype(F32)
        sb = _sigmoid(b_ref[0].astype(F32))
        uext[pl.ds(CONV_HALO, tr), :] = av * sb
        dext[pl.ds(0, tr), :] = du_ref[0]
        dext[pl.ds(tr, CONV_HALO), :] = jnp.where(j < nj - 1, duh_ref[0], 0.0)
        ddb_ref[...] += jnp.sum(du_ref[0], axis=0, keepdims=True)

        def cols(c, carry):
            cs = pl.ds(pl.multiple_of(c * LANES, LANES), LANES)
            _fill_shifted(dshift, dext, cs)
            _fill_shifted(ushift, uext, cs)
            for rc in range(tr // CONV_CHUNK):
                base = rc * CONV_CHUNK
                acc = jnp.zeros((CONV_CHUNK, LANES), F32)
                for w in range(CONV_WIDTH):
                    acc = acc + dw_ref[pl.ds(w, 1), cs] * _shifted_rows(dshift, dext, cs, base + CONV_WIDTH - 1 - w)
                du0[pl.ds(base, CONV_CHUNK), cs] = acc
            for w in range(CONV_WIDTH):
                part = jnp.zeros((SUBLANES, LANES), F32)
                for rc in range(tr // CONV_CHUNK):
                    base = rc * CONV_CHUNK
                    prod = dext[pl.ds(base, CONV_CHUNK), cs] * _shifted_rows(
                        ushift, uext, cs, base + CONV_HALO - (CONV_WIDTH - 1) + w)
                    for i in range(CONV_CHUNK // SUBLANES):
                        part = part + prod[i * SUBLANES:(i + 1) * SUBLANES]
                ddw8[pl.ds(w * SUBLANES, SUBLANES), cs] += part
            return carry

        lax.fori_loop(0, D_MODEL // LANES, cols, 0)
        g = du0[...]
        dab_ref[0, :, 0:D_MODEL] = (g * sb).astype(BF16)
        dab_ref[0, :, D_MODEL:2 * D_MODEL] = (g * av * sb * (1.0 - sb)).astype(BF16)

        @pl.when(last)
        def _():
            for w in range(CONV_WIDTH + 1):
                ddw_ref[pl.ds(w, 1), :] = jnp.sum(ddw8[pl.ds(w * SUBLANES, SUBLANES), :], axis=0, keepdims=True)

    tile = lambda cb: pl.BlockSpec((1, tr, D_MODEL), lambda b, j: (b, j, cb))
    halo = lambda cb: pl.BlockSpec((1, CONV_HALO, D_MODEL), lambda b, j: (b, jnp.maximum(j * hb - 1, 0), cb))
    nxt = pl.BlockSpec((1, CONV_HALO, D_MODEL), lambda b, j: (b, jnp.minimum((j + 1) * hb, S // CONV_HALO - 1), 0))
    par = lambda r: pl.BlockSpec((r, D_MODEL), lambda b, j: (0, 0))
    p3 = proj.reshape(Bl, S, 2 * D_MODEL)
    d3 = du1.reshape(Bl, S, D_MODEL)
    dab, ddw, ddb = pl.pallas_call(
        body, name="conv_bwd", grid=(Bl, nj),
        in_specs=[tile(0), tile(1), halo(0), halo(1), tile(0), nxt, par(32)],
        out_specs=[pl.BlockSpec((1, tr, 2 * D_MODEL), lambda b, j: (b, j, 0)), par(32), par(1)],
        out_shape=[_sds((Bl, S, 2 * D_MODEL), BF16), _sds((32, D_MODEL), F32), _sds((1, D_MODEL), F32)],
        scratch_shapes=[pltpu.VMEM((tr + CONV_HALO, D_MODEL), F32), pltpu.VMEM((tr + CONV_HALO, D_MODEL), F32),
                        pltpu.VMEM((tr, D_MODEL), F32),
                        pltpu.VMEM((SUBLANES - 1, CONV_SHIFT_ROWS, LANES), F32),
                        pltpu.VMEM((SUBLANES - 1, CONV_SHIFT_ROWS, LANES), F32),
                        pltpu.VMEM(((CONV_WIDTH + 1) * SUBLANES, D_MODEL), F32)],
        compiler_params=_params(("arbitrary", "arbitrary")),
    )(p3, p3, p3, p3, d3, d3, dw)
    return dab.reshape(Bl * S, 2 * D_MODEL), ddw, ddb


_LAYOUT = (
    ("pre_norm_g", (4, 1024), None), ("post_norm_g", (4, 1024), None),
    ("attn_w_in", (2, 1024, 2560), 2), ("attn_w_out", (2, 256, 1024), 1),
    ("conv_w_in", (2, 1024, 768), 2), ("conv_dw_w", (2, 31, 256), 2),
    ("conv_dw_b", (2, 256), 1), ("conv_ln_g", (2, 256), 1), ("conv_ln_b", (2, 256), 1),
    ("conv_w_out", (2, 256, 1024), 1), ("ple_w_proj", (4, 256, 256), 2), ("ple_w_gate", (4, 256, 1024), 1),
)
_MATMUL_WEIGHTS = ("attn_w_in", "attn_w_out", "conv_w_in", "conv_w_out", "ple_w_proj", "ple_w_gate")
_FIRST_LAYER = ("attn_w_in", "attn_w_out", "ple_w_proj", "ple_w_gate")
_AXIS = {n: a for n, _, a in _LAYOUT}


def _size(shape):
    n = 1
    for s in shape:
        n *= s
    return n


def _padded_rows(shape):
    rows = _size(shape) // shape[-1]
    return rows + (-rows) % FLAT_ROW_ALIGN


def _rows2d(a):
    a2 = a.reshape(-1, a.shape[-1])
    pad = _padded_rows(a.shape) - a2.shape[0]
    return jnp.pad(a2, ((0, pad), (0, 0))) if pad else a2


def _col_blocks(a):
    a2 = _rows2d(a)
    return jnp.concatenate([a2[:, c:c + FLAT_COLS] for c in range(0, a2.shape[1], FLAT_COLS)], axis=0)


def _from_col_blocks(flat, off, shape):
    rows, nblk = _padded_rows(shape), shape[-1] // FLAT_COLS
    a2 = jnp.concatenate([flat[off + b * rows:off + (b + 1) * rows] for b in range(nblk)], axis=1)
    return a2[:_size(shape) // shape[-1]].reshape(shape), off + nblk * rows


def _shard_col_blocks(full, shape, axis):
    if axis is None:
        blocks = _col_blocks(full)
        return jnp.broadcast_to(blocks[None], (N_CHIPS,) + blocks.shape)
    m = shape[-1]
    if axis == len(shape) - 1:
        a2 = _rows2d(full)
        pieces = [a2[:, c:c + FLAT_COLS] for c in range(0, N_CHIPS * m, FLAT_COLS)]
    else:
        layers, r, _ = shape
        assert axis == 1 and (layers * r) % FLAT_ROW_ALIGN == 0
        pieces = [full[:, s * r:(s + 1) * r, c:c + FLAT_COLS].reshape(layers * r, FLAT_COLS)
                  for s in range(N_CHIPS) for c in range(0, m, FLAT_COLS)]
    return jnp.concatenate(pieces, axis=0).reshape(N_CHIPS, -1, FLAT_COLS)


_FLAT_BIG = ("attn_w_in", "conv_w_in", "ple_w_gate", "attn_w_out", "conv_w_out", "ple_w_proj")
_FLAT_SMALL = ("pre_norm_g", "post_norm_g", "conv_dw_w", "conv_dw_b", "conv_ln_g", "conv_ln_b")
PACK_TILE = 1024


_FIRST_LAYER_PARAMS = _FIRST_LAYER + ("pre_norm_g", "post_norm_g")


def _part_shapes(first):
    out = {}
    for n, shape, _ in _LAYOUT:
        layers = (1 if first else shape[0] - 1) if n in _FIRST_LAYER_PARAMS else (0 if first else shape[0])
        if layers and not (first and n == "pre_norm_g"):
            out[n] = (layers,) + shape[1:]
    return out


def _flat_plan(shapes):
    out, off = {}, 0
    for n in _FLAT_BIG + _FLAT_SMALL:
        if n in shapes:
            out[n] = off
            off += _padded_rows(shapes[n]) * (shapes[n][-1] // FLAT_COLS)
    return out, off + (-off) % (2 * FLAT_TILE)


def _unpack_f32(flat, shapes):
    offsets, _ = _flat_plan(shapes)
    return {n: _from_col_blocks(flat, offsets[n], shapes[n])[0] for n in shapes}


def _pack_param(full, shape, axis, off, flat, total_rows):
    layers, r, m = shape
    nblk = m // FLAT_COLS
    if axis == 2:
        rows = layers * r
        tr = math.gcd(math.gcd(rows, PACK_TILE), off) if off else math.gcd(rows, PACK_TILE)
        assert rows % tr == 0 and off % tr == 0 and tr % FLAT_ROW_ALIGN == 0
        src = full.reshape(rows, N_CHIPS * m)
        grid = (N_CHIPS * nblk, rows // tr)
        in_spec = pl.BlockSpec((tr, FLAT_COLS), lambda j, i: (i, j))
        out_spec = pl.BlockSpec((None, tr, FLAT_COLS), lambda j, i: (j // nblk, (off + (j % nblk) * rows) // tr + i, 0))
    else:
        assert axis == 1 and off % r == 0
        src = full.reshape(layers * N_CHIPS * r, m)
        grid = (layers, N_CHIPS, nblk)
        in_spec = pl.BlockSpec((r, FLAT_COLS), lambda l, s, b: (l * N_CHIPS + s, b))
        out_spec = pl.BlockSpec((None, r, FLAT_COLS), lambda l, s, b: (s, (off + b * layers * r) // r + l, 0))

    def copy_body(src_ref, *rest):
        rest[-1][...] = src_ref[...]

    args, in_specs, aliases = [src], [in_spec], {}
    if flat is not None:
        args.append(flat)
        in_specs.append(_ANY)
        aliases = {1: 0}
    return pl.pallas_call(
        copy_body, name="pack_grad", grid=grid, in_specs=in_specs, out_specs=out_spec,
        out_shape=_sds((N_CHIPS, total_rows, FLAT_COLS), F32), input_output_aliases=aliases,
        compiler_params=_params(("arbitrary",) * len(grid)))(*args)


SMALL_ROWS = 40


def _stack_small(w):
    rows = [w["conv_dw_w"]] + [w[n][:, None, :] for n in ("conv_dw_b", "conv_ln_g", "conv_ln_b")]
    stacked = jnp.concatenate(rows, axis=1)
    return jnp.pad(stacked, ((0, 0), (0, SMALL_ROWS - stacked.shape[1]), (0, 0)))


def _unstack_small(small):
    return {"conv_dw_w": small[:, :CONV_WIDTH], "conv_dw_b": small[:, CONV_WIDTH],
            "conv_ln_g": small[:, CONV_WIDTH + 1], "conv_ln_b": small[:, CONV_WIDTH + 2]}


def _pack_full_grads(grads, shapes):
    offsets, total_rows = _flat_plan(shapes)
    flat = None
    for n in _FLAT_BIG:
        if n in shapes:
            flat = _pack_param(grads[n], shapes[n], _AXIS[n], offsets[n], flat, total_rows)
    small_names = [n for n in _FLAT_SMALL if n in shapes]
    small = jnp.concatenate([_shard_col_blocks(grads[n], shapes[n], _AXIS[n]) for n in small_names], axis=1)
    start = offsets[small_names[0]]
    small = jnp.pad(small, ((0, 0), (0, total_rows - start - small.shape[1]), (0, 0)))
    return lax.dynamic_update_slice(flat, small, (0, start, 0))


_ANY = pl.BlockSpec(memory_space=pl.ANY)


def _mesh_pos():
    return lax.axis_index("x"), lax.axis_index("y"), lax.axis_index("c")


def _other_chips(x, y):
    return [(1 - x, y), (x, 1 - y), (1 - x, 1 - y)]


COPIES_PER_ARRAY = 7


def _allgather_weights(shards, axes, small):
    n = len(shards)
    full_shape = lambda a, axis: tuple(d * (N_CHIPS if i == axis else 1) for i, d in enumerate(a.shape))

    def body(*refs):
        ins, small_in = refs[:n], refs[n]
        outs, small_out = refs[n + 1:2 * n + 1], refs[2 * n + 1]
        send_sems, recv_sems = refs[2 * n + 2:]
        x, y, c = _mesh_pos()
        mine, me, sibling = 2 * x + y, (x, y, c), (x, y, 1 - c)
        chips = _other_chips(x, y)

        def region(a, chip, half):
            _, rows, cols = shards[a].shape
            h = rows // 2
            if axes[a] == 2:
                return outs[a].at[:, slice(None) if half is None else pl.ds(half * h, h), pl.ds(chip * cols, cols)]
            if half is None:
                return outs[a].at[:, pl.ds(chip * rows, rows), :]
            return outs[a].at[:, pl.ds(chip * rows + half * h, h), :]

        def copy(k, src, dst, to):
            return pltpu.make_async_remote_copy(src_ref=src, dst_ref=dst, send_sem=send_sems.at[k],
                                                recv_sem=recv_sems.at[k], device_id=to, device_id_type=MESH)

        def arrival(k, dst):
            return copy(k, dst, dst, me)

        sends = []
        for a in range(n):
            h = shards[a].shape[1] // 2
            base = a * COPIES_PER_ARRAY
            sends.append(copy(base + 6, ins[a], region(a, mine, None), sibling))
            for j, (cx, cy) in enumerate(chips):
                sends.append(copy(base + j, ins[a].at[:, pl.ds(c * h, h), :], region(a, mine, c), (cx, cy, c)))
        small_cols = small.shape[2]
        small_region = lambda chip: small_out.at[:, :, pl.ds(chip * small_cols, small_cols)]
        base = n * COPIES_PER_ARRAY
        sends.append(copy(base + 3, small_in, small_region(mine), sibling))
        for j, (cx, cy) in enumerate(chips):
            sends.append(copy(base + j, small_in, small_region(mine), (cx, cy, c)))
        for cp in sends:
            cp.start()
        for j, (cx, cy) in enumerate(chips):
            for a in range(n):
                k = a * COPIES_PER_ARRAY + j
                arrival(k, region(a, 2 * cx + cy, c)).wait_recv()
                passed = copy(k + 3, region(a, 2 * cx + cy, c), region(a, 2 * cx + cy, c), sibling)
                passed.start()
                sends.append(passed)
        for j, (cx, cy) in enumerate(chips):
            for a in range(n):
                arrival(a * COPIES_PER_ARRAY + 3 + j, region(a, 2 * cx + cy, 1 - c)).wait_recv()
            arrival(base + j, small_region(2 * cx + cy)).wait_recv()
        for a in range(n):
            arrival(a * COPIES_PER_ARRAY + 6, region(a, mine, None)).wait_recv()
        arrival(base + 3, small_region(mine)).wait_recv()
        for cp in sends:
            cp.wait_send()

    n_sems = n * COPIES_PER_ARRAY + 4
    out = pl.pallas_call(
        body, name="allgather_weights", in_specs=[_ANY] * (n + 1), out_specs=[_ANY] * (n + 1),
        out_shape=[_sds(full_shape(a, axis), a.dtype) for a, axis in zip(shards, axes)]
        + [_sds(full_shape(small, 2), small.dtype)],
        scratch_shapes=[pltpu.SemaphoreType.DMA((n_sems,)), pltpu.SemaphoreType.DMA((n_sems,))],
    )(*shards, small)
    return out[:n], out[n]


_HBM = pl.BlockSpec(memory_space=pltpu.HBM)
_SEM = pl.BlockSpec(memory_space=pltpu.SEMAPHORE)


def _full_shape(shard, axis):
    return tuple(d * (N_CHIPS if i == axis else 1) for i, d in enumerate(shard.shape))


def _direct_gather_copies(srcs, lands, shapes, axes, send_sems, recv_sems):
    x, y, c = _mesh_pos()
    mine, me, sibling = 2 * x + y, (x, y, c), (x, y, 1 - c)

    def region(a, chip, half):
        _, rows, cols = shapes[a]
        h = rows // 2
        if axes[a] == 2:
            return lands[a].at[:, slice(None) if half is None else pl.ds(half * h, h), pl.ds(chip * cols, cols)]
        if half is None:
            return lands[a].at[:, pl.ds(chip * rows, rows), :]
        return lands[a].at[:, pl.ds(chip * rows + half * h, h), :]

    def copy(k, src, dst, to):
        return pltpu.make_async_remote_copy(src_ref=src, dst_ref=dst, send_sem=send_sems.at[k],
                                            recv_sem=recv_sems.at[k], device_id=to, device_id_type=MESH)

    outgoing, incoming = [], []
    for a in range(len(srcs)):
        h = shapes[a][1] // 2
        base = a * COPIES_PER_ARRAY
        for j, (cx, cy) in enumerate(_other_chips(x, y)):
            for t, ct in enumerate((c, 1 - c)):
                k = base + 2 * j + t
                outgoing.append(copy(k, srcs[a].at[:, pl.ds(c * h, h), :], region(a, mine, c), (cx, cy, ct)))
                landed = region(a, 2 * cx + cy, ct)
                incoming.append(copy(k, landed, landed, me))
        outgoing.append(copy(base + 6, srcs[a], region(a, mine, None), sibling))
        incoming.append(copy(base + 6, region(a, mine, None), region(a, mine, None), me))
    return outgoing, incoming


def _allgather_start(shards, axes):
    n = len(shards)
    shapes = [s.shape for s in shards]

    def body(*refs):
        srcs, lands = refs[:n], refs[n:2 * n]
        send_sems, recv_sems = refs[2 * n], refs[2 * n + 1]
        token = refs[-1]
        outgoing, _ = _direct_gather_copies(srcs, lands, shapes, axes, send_sems, recv_sems)
        for cp in outgoing:
            cp.start()
        token[...] = jnp.zeros_like(token)

    n_sems = n * COPIES_PER_ARRAY
    zones = [pltpu.with_memory_space_constraint(lax.empty(_full_shape(s, ax), s.dtype), pltpu.HBM)
             for s, ax in zip(shards, axes)]
    out = pl.pallas_call(
        body, name="allgather_rest_start",
        out_shape=(pltpu.SemaphoreType.DMA((n_sems,)), pltpu.SemaphoreType.DMA((n_sems,)),
                   *[pltpu.HBM(s.shape, s.dtype) for s in shards], *[pltpu.HBM(z.shape, z.dtype) for z in zones],
                   jax.ShapeDtypeStruct((8, LANES), F32)),
        in_specs=[_HBM] * (2 * n),
        out_specs=(_SEM, _SEM, *[_HBM] * (2 * n), pl.BlockSpec(memory_space=pltpu.VMEM)),
        input_output_aliases={i: 2 + i for i in range(2 * n)},
        compiler_params=pltpu.CompilerParams(has_side_effects=pltpu.SideEffectType.DATAFLOW_SIDE_EFFECTING),
    )(*[pltpu.with_memory_space_constraint(s, pltpu.HBM) for s in shards], *zones)
    return out[0], out[1], out[2:2 + n], out[2 + n:2 + 2 * n], out[-1]


def _allgather_wait(send_sems, recv_sems, shards, zones, axes, after):
    n = len(shards)
    shapes = [s.shape for s in shards]

    def body(*refs):
        srcs, lands = refs[:n], refs[n:2 * n]
        outgoing, incoming = _direct_gather_copies(srcs, lands, shapes, axes, refs[2 * n], refs[2 * n + 1])
        for cp in outgoing:
            cp.wait_send()
        for cp in incoming:
            cp.wait_recv()

    out = pl.pallas_call(
        body, name="allgather_rest_wait",
        out_shape=(*[pltpu.HBM(s.shape, s.dtype) for s in shards], *[pltpu.HBM(z.shape, z.dtype) for z in zones]),
        in_specs=[_HBM] * (2 * n) + [_SEM, _SEM, _ANY], out_specs=[_HBM] * (2 * n),
        input_output_aliases={i: i for i in range(2 * n)},
        compiler_params=pltpu.CompilerParams(has_side_effects=pltpu.SideEffectType.DATAFLOW_SIDE_EFFECTING),
    )(*shards, *zones, send_sems, recv_sems, after)
    return out[n:]


def _exchange_core_halves(g):
    n, _, H, C = g.shape

    def body(g_ref, got_ref, send_sem, recv_sem):
        x, y, c = _mesh_pos()
        swap = pltpu.make_async_remote_copy(
            src_ref=g_ref.at[pl.ds(0, n), 1 - c], dst_ref=got_ref, send_sem=send_sem, recv_sem=recv_sem,
            device_id=(x, y, 1 - c), device_id_type=MESH)
        swap.start()
        swap.wait()

    return pl.pallas_call(
        body, name="exchange_core_halves", in_specs=[_ANY], out_specs=_ANY,
        out_shape=_sds((n, H, C), g.dtype),
        scratch_shapes=[pltpu.SemaphoreType.DMA, pltpu.SemaphoreType.DMA],
    )(g)


def _exchange_copy(g_ref, got_ref, send_sems, recv_sems):
    x, y, c = _mesh_pos()
    return pltpu.make_async_remote_copy(
        src_ref=g_ref.at[pl.ds(0, g_ref.shape[0]), 1 - c], dst_ref=got_ref, send_sem=send_sems.at[0],
        recv_sem=recv_sems.at[0], device_id=(x, y, 1 - c), device_id_type=MESH)


def _exchange_start(g):
    n, _, H, C = g.shape

    def body(g_ref, got_ref, send_sems, recv_sems, g_thru, got_thru, token):
        _exchange_copy(g_ref, got_ref, send_sems, recv_sems).start()
        token[...] = jnp.zeros_like(token)

    zone = pltpu.with_memory_space_constraint(lax.empty((n, H, C), g.dtype), pltpu.HBM)
    return pl.pallas_call(
        body, name="exchange_rest_start",
        out_shape=(pltpu.SemaphoreType.DMA((1,)), pltpu.SemaphoreType.DMA((1,)), pltpu.HBM(g.shape, g.dtype),
                   pltpu.HBM(zone.shape, zone.dtype), jax.ShapeDtypeStruct((8, LANES), F32)),
        in_specs=[_HBM, _HBM], out_specs=(_SEM, _SEM, _HBM, _HBM, pl.BlockSpec(memory_space=pltpu.VMEM)),
        input_output_aliases={0: 2, 1: 3},
        compiler_params=pltpu.CompilerParams(has_side_effects=pltpu.SideEffectType.DATAFLOW_SIDE_EFFECTING),
    )(pltpu.with_memory_space_constraint(g, pltpu.HBM), zone)


def _exchange_wait(send_sems, recv_sems, g, zone, after):
    def body(g_ref, got_ref, send_sems, recv_sems, after_ref, g_out, got_out):
        copy = _exchange_copy(g_ref, got_ref, send_sems, recv_sems)
        copy.wait_send()
        copy.wait_recv()

    return pl.pallas_call(
        body, name="exchange_rest_wait",
        out_shape=(pltpu.HBM(g.shape, g.dtype), pltpu.HBM(zone.shape, zone.dtype)),
        in_specs=[_HBM, _HBM, _SEM, _SEM, _ANY], out_specs=[_HBM, _HBM], input_output_aliases={0: 0, 1: 1},
        compiler_params=pltpu.CompilerParams(has_side_effects=pltpu.SideEffectType.DATAFLOW_SIDE_EFFECTING),
    )(g, zone, send_sems, recv_sems, after)


def _scatter_copies(p_ref, q_ref, send_sems, recv_sems):
    x, y, c = _mesh_pos()
    return [pltpu.make_async_remote_copy(
        src_ref=p_ref.at[2 * cx + cy], dst_ref=q_ref.at[j], send_sem=send_sems.at[j],
        recv_sem=recv_sems.at[j], device_id=(cx, cy, c), device_id_type=MESH)
        for j, (cx, cy) in enumerate(_other_chips(x, y))]


def _scatter_start(p, part):
    n, H, C = p.shape

    def body(p_ref, q_ref, send_sems, recv_sems, p_thru, q_thru, token):
        for cp in _scatter_copies(p_ref, q_ref, send_sems, recv_sems):
            cp.start()
        token[...] = jnp.zeros_like(token)

    zone = pltpu.with_memory_space_constraint(lax.empty((n - 1, H, C), p.dtype), pltpu.HBM)
    return pl.pallas_call(
        body, name="scatter_%s_start" % part,
        out_shape=(pltpu.SemaphoreType.DMA((n - 1,)), pltpu.SemaphoreType.DMA((n - 1,)), pltpu.HBM(p.shape, p.dtype),
                   pltpu.HBM(zone.shape, zone.dtype), jax.ShapeDtypeStruct((8, LANES), F32)),
        in_specs=[_HBM, _HBM], out_specs=(_SEM, _SEM, _HBM, _HBM, pl.BlockSpec(memory_space=pltpu.VMEM)),
        input_output_aliases={0: 2, 1: 3},
        compiler_params=pltpu.CompilerParams(has_side_effects=pltpu.SideEffectType.DATAFLOW_SIDE_EFFECTING),
    )(pltpu.with_memory_space_constraint(p, pltpu.HBM), zone)


def _scatter_wait(send_sems, recv_sems, p, zone, after, part):
    def body(p_ref, q_ref, send_sems, recv_sems, after_ref, p_out, q_out):
        copies = _scatter_copies(p_ref, q_ref, send_sems, recv_sems)
        for cp in copies:
            cp.wait_send()
        for cp in copies:
            cp.wait_recv()

    return pl.pallas_call(
        body, name="scatter_%s_wait" % part,
        out_shape=(pltpu.HBM(p.shape, p.dtype), pltpu.HBM(zone.shape, zone.dtype)),
        in_specs=[_HBM, _HBM, _SEM, _SEM, _ANY], out_specs=[_HBM, _HBM], input_output_aliases={0: 0, 1: 1},
        compiler_params=pltpu.CompilerParams(has_side_effects=pltpu.SideEffectType.DATAFLOW_SIDE_EFFECTING),
    )(p, zone, send_sems, recv_sems, after)


N_DEVICES = 8


def _allreduce_row(v):
    C = v.shape[1]

    def body(v_ref, out_ref, rows, send_sems, recv_sems):
        x, y, c = _mesh_pos()
        me = 4 * x + 2 * y + c
        rows[pl.ds(me, 1)] = v_ref[...].reshape(1, 1, C)
        copies = []
        for m in range(1, N_DEVICES):
            peer = (x ^ (m >> 2), y ^ ((m >> 1) & 1), c ^ (m & 1))
            copies.append(pltpu.make_async_remote_copy(
                src_ref=rows.at[me], dst_ref=rows.at[me], send_sem=send_sems.at[m - 1],
                recv_sem=recv_sems.at[m - 1], device_id=peer, device_id_type=MESH))
        for cp in copies:
            cp.start()
        for m, cp in enumerate(copies, start=1):
            cp.wait_send()
            pltpu.make_async_remote_copy(
                src_ref=rows.at[me], dst_ref=rows.at[me ^ m], send_sem=send_sems.at[m - 1],
                recv_sem=recv_sems.at[m - 1], device_id=(x, y, c), device_id_type=MESH).wait_recv()
        total = rows[0]
        for d in range(1, N_DEVICES):
            total = total + rows[d]
        out_ref[...] = total

    vmem = pl.BlockSpec(memory_space=pltpu.VMEM)
    return pl.pallas_call(
        body, name="allreduce_row", in_specs=[vmem], out_specs=vmem, out_shape=_sds((1, C), F32),
        scratch_shapes=[pltpu.VMEM((N_DEVICES, 1, C), F32), pltpu.SemaphoreType.DMA((N_DEVICES - 1,)),
                        pltpu.SemaphoreType.DMA((N_DEVICES - 1,))],
    )(v)


def _share_core_halves(r2):
    _, H, C = r2.shape

    def body(r_ref, out_ref, send_sem, recv_sem):
        x, y, c = _mesh_pos()
        send = pltpu.make_async_remote_copy(
            src_ref=r_ref.at[c], dst_ref=out_ref.at[c], send_sem=send_sem, recv_sem=recv_sem,
            device_id=(x, y, 1 - c), device_id_type=MESH)
        send.start()
        send.wait_send()
        pltpu.make_async_remote_copy(
            src_ref=r_ref.at[c], dst_ref=out_ref.at[1 - c], send_sem=send_sem, recv_sem=recv_sem,
            device_id=(x, y, 1 - c), device_id_type=MESH).wait_recv()

    return pl.pallas_call(
        body, name="share_core_halves", in_specs=[_ANY], out_specs=_ANY,
        out_shape=_sds(r2.shape, r2.dtype), input_output_aliases={0: 0},
        scratch_shapes=[pltpu.SemaphoreType.DMA, pltpu.SemaphoreType.DMA],
    )(r2)


def _place():
    x, y, c = _mesh_pos()
    return jnp.stack([c, 2 * x + y]).astype(jnp.int32)


def _sum_pair(g, got, place):
    n, _, H, C = g.shape

    def body(place_ref, a_ref, b_ref, o_ref):
        o_ref[...] = (a_ref[...] + b_ref[...]).astype(BF16)

    spec = pl.BlockSpec((1, FLAT_TILE, C), lambda s, i, pr: (s, i, 0))
    return pl.pallas_call(
        body, name="sum_core_pair",
        grid_spec=pltpu.PrefetchScalarGridSpec(
            num_scalar_prefetch=1, grid=(n, H // FLAT_TILE),
            in_specs=[pl.BlockSpec((1, None, FLAT_TILE, C), lambda s, i, pr: (s, pr[0], i, 0)), spec],
            out_specs=spec),
        out_shape=_sds((n, H, C), BF16),
        compiler_params=_params(("parallel", "parallel")))(place, g, got)


def _sum_chips(p, q, place):
    n, H, C = p.shape

    def body(place_ref, own_ref, qx_ref, qy_ref, qxy_ref, o_ref):
        mine = place_ref[1]
        own, qx, qy, qxy = (t[0].astype(F32) for t in (own_ref, qx_ref, qy_ref, qxy_ref))

        def term(s):
            rel = jnp.full(own.shape, mine ^ s, jnp.int32)
            return jnp.where(rel == 0, own, jnp.where(rel == 2, qx, jnp.where(rel == 1, qy, qxy)))

        o_ref[0] = ((term(0) + term(1)) + term(2)) + term(3)

    qspec = lambda j: pl.BlockSpec((1, FLAT_TILE, C), lambda i, pr: (j, i, 0))
    return pl.pallas_call(
        body, name="sum_chips",
        grid_spec=pltpu.PrefetchScalarGridSpec(
            num_scalar_prefetch=1, grid=(H // FLAT_TILE,),
            in_specs=[pl.BlockSpec((1, FLAT_TILE, C), lambda i, pr: (pr[1], i, 0)), qspec(0), qspec(1), qspec(2)],
            out_specs=pl.BlockSpec((1, FLAT_TILE, C), lambda i, pr: (pr[0], i, 0))),
        out_shape=_sds((2, H, C), F32),
        compiler_params=_params(("parallel",)))(place, p, q, q, q)


ADAMW_BLOCK = 1 << 18


def _adamw(w, g, m, v):
    shape = w.shape
    C = shape[-1]
    R = _size(shape) // C
    tr = R
    while tr * C > ADAMW_BLOCK and tr % 16 == 0:
        tr //= 2
    w, g, m, v = (t.reshape(R, C) for t in (w, g, m, v))

    def body(w_ref, g_ref, m_ref, v_ref, d_ref, nm_ref, nv_ref):
        gv = g_ref[...]
        nm = ADAM_B1 * m_ref[...] + (1.0 - ADAM_B1) * gv
        nv = ADAM_B2 * v_ref[...] + (1.0 - ADAM_B2) * (gv * gv)
        m_hat = nm / (1.0 - ADAM_B1 ** ADAM_STEP)
        v_hat = nv / (1.0 - ADAM_B2 ** ADAM_STEP)
        d_ref[...] = -ADAM_LR * (m_hat / (jnp.sqrt(v_hat) + ADAM_EPS) + ADAM_WD * w_ref[...])
        nm_ref[...] = nm
        nv_ref[...] = nv

    spec = pl.BlockSpec((tr, C), lambda i: (i, 0))
    outs = pl.pallas_call(body, name="adamw", grid=(R // tr,), in_specs=[spec] * 4, out_specs=[spec] * 3,
                          out_shape=[_sds((R, C), F32)] * 3, compiler_params=_params(("parallel",)))(w, g, m, v)
    return tuple(t.reshape(shape) for t in outs)


def _chip_partials(gfull, place):
    n, R, C = gfull.shape
    g4 = gfull.reshape(n, 2, R // 2, C)
    return _sum_pair(g4, _exchange_core_halves(g4), place)


def _finish_reduce(p, q, place):
    r2 = _share_core_halves(_sum_chips(p, q, place))
    return r2.reshape(2 * r2.shape[1], r2.shape[2])


def _local_step(x, p, positions, loss_target, pre_g, post_g, w, later_weights, later_grads, later_grads_arrived,
                first_grads):
    Bl, S, _ = x.shape
    T = Bl * S
    cos, sin = _rope_tables(positions)
    xs = x.reshape(T, D_MODEL)
    saved = []
    for i in range(DEPTH):
        if i == 1:
            later_weights(xs)
        j = i // 2
        g_pre, g_post = pre_g[i:i + 1], post_g[i:i + 1]
        h = _rmsnorm_fwd(xs, g_pre)
        st = {"x": xs, "h": h}
        if i % 2 == 0:
            proj = _mm(h, w["attn_w_in"][j], name="attn_in")
            res = [_attn_fwd(proj, cos, sin, g, Bl, S) for g in range(N_GROUPS)]
            a, o, lse = _attn_combine([r[0] for r in res], [r[1] for r in res], proj)
            w_out = w["attn_w_out"][j]
            st.update(proj=proj, a=a, o=o, lse=lse, qkv=[r[2] for r in res])
        else:
            w_ab, w_z = w["conv_w_in"][j][:, :2 * D_MODEL], w["conv_w_in"][j][:, 2 * D_MODEL:]
            ab = _mm(h, w_ab, out_dtype=BF16, name="conv_in_ab")
            z = _mm(h, w_z, out_dtype=BF16, name="conv_in_z")
            dw = jnp.pad(w["conv_dw_w"][j], ((0, 1), (0, 0)))
            u1, a = _conv_fwd(ab, z, dw, w["conv_dw_b"][j:j + 1], w["conv_ln_g"][j:j + 1],
                              w["conv_ln_b"][j:j + 1], Bl, S)
            w_out = w["conv_w_out"][j]
            st.update(w_ab=w_ab, w_z=w_z, ab=ab, z=z, dw=dw, u1=u1, a=a)
        y, x1 = _mm_rows(a, w_out, [xs, g_post], _post_epilogue, (F32, F32), "branch_out_post")
        pi = p[i].reshape(T, PLE_DIM)
        pe = _mm(pi, w["ple_w_proj"][i], name="ple_proj")
        gl, xs = _mm_rows(x1, w["ple_w_gate"][i], [pe], _ple_epilogue, (F32, F32), "ple_gate_fwd")
        st.update(y=y, x1=x1, pi=pi, pe=pe, gl=gl)
        saved.append(st)

    sq, dx = _loss_fwd_bwd(xs, loss_target.reshape(T, D_MODEL))

    grads = {n: [None] * shape[0] for n, shape, _ in _LAYOUT}
    for i in reversed(range(DEPTH)):
        j = i // 2
        st = saved[i]
        g_pre, g_post = pre_g[i:i + 1], post_g[i:i + 1]
        if i == 0:
            rest = {n: jnp.stack(v[1:] if n in _FIRST_LAYER_PARAMS else v) for n, v in grads.items()}
            g_post = g_post + later_grads(rest)[0, 0]
        dpe, dgl, dx1, dy, dg_post = _ple_post_bwd(dx, st["pe"], st["gl"], w["ple_w_gate"][i], st["y"], g_post)
        grads["ple_w_proj"][i] = _mm(st["pi"], dpe, ta=True, name="ple_proj_wgrad")
        grads["ple_w_gate"][i] = _mm(st["x1"], dgl, ta=True, name="ple_gate_wgrad")
        grads["post_norm_g"][i] = dg_post[0]
        if i % 2 == 0:
            grads["attn_w_out"][j] = _mm(st["a"], dy, ta=True, name="attn_out_wgrad")
            da = _mm(dy, w["attn_w_out"][j], tb=True, name="attn_out_dgrad")
            do, dproj = _gate_bwd(da, st["o"], st["proj"])
            cos_b = cos + later_grads_arrived(do)[0, 0] if i == 0 else cos
            for g in range(N_GROUPS):
                dproj = _attn_bwd(st["qkv"][g], cos_b, sin, do, st["o"], st["lse"], dproj, g, Bl, S)
            grads["attn_w_in"][j] = _mm(st["h"], dproj, ta=True, tk=2048, name="attn_in_wgrad")
            if i == 0:
                first = {n: jnp.stack(grads[n][:1]) for n in _FIRST_LAYER_PARAMS if n != "pre_norm_g"}
                g_pre = g_pre + first_grads(first)[0, 0]
            dx, dg_pre = _dgrad_pre_bwd(dproj, w["attn_w_in"][j], st["x"], g_pre, dx1, None, 1024,
                                        "attn_in_dgrad_pre", tm=1024)
        else:
            grads["conv_w_out"][j] = _mm(st["a"], dy, ta=True, name="conv_out_wgrad")
            da2 = _mm(dy, w["conv_w_out"][j], tb=True, name="conv_out_dgrad")
            du1, dz, dln_g, dln_b = _conv_norm_bwd(da2, st["z"], st["u1"], w["conv_ln_g"][j:j + 1],
                                                   w["conv_ln_b"][j:j + 1])
            dab, ddw, ddb = _conv_bwd(st["ab"], du1, st["dw"], Bl, S)
            dh = _mm(dz, st["w_z"], tb=True, name="conv_in_z_dgrad")
            dx, dg_pre = _dgrad_pre_bwd(dab, st["w_ab"], st["x"], g_pre, dx1, dh, 2048, "conv_in_dgrad_pre")
            dw_ab = _mm(st["h"], dab, ta=True, name="conv_in_ab_wgrad")
            dw_z = _mm(st["h"], dz, ta=True, name="conv_in_z_wgrad")
            grads["conv_w_in"][j] = jnp.concatenate([dw_ab, dw_z], axis=1)
            grads["conv_dw_w"][j] = ddw[:CONV_WIDTH]
            grads["conv_dw_b"][j] = ddb[0]
            grads["conv_ln_g"][j] = dln_g[0]
            grads["conv_ln_b"][j] = dln_b[0]
        grads["pre_norm_g"][i] = dg_pre[0]
    return sq, dx.reshape(Bl, S, D_MODEL), grads["pre_norm_g"][0][None]


_NAMES = tuple(n for n, _, _ in _LAYOUT)


def kernel(x, p, positions, pre_norm_g, post_norm_g, attn_w_in, attn_w_out, conv_w_in, conv_dw_w, conv_dw_b, conv_ln_g, conv_ln_b, conv_w_out, ple_w_proj, ple_w_gate, loss_target, m_pre_norm_g, m_post_norm_g, m_attn_w_in, m_attn_w_out, m_conv_w_in, m_conv_dw_w, m_conv_dw_b, m_conv_ln_g, m_conv_ln_b, m_conv_w_out, m_ple_w_proj, m_ple_w_gate, v_pre_norm_g, v_post_norm_g, v_attn_w_in, v_attn_w_out, v_conv_w_in, v_conv_dw_w, v_conv_dw_b, v_conv_ln_g, v_conv_ln_b, v_conv_w_out, v_ple_w_proj, v_ple_w_gate):
    w_loc = dict(zip(_NAMES, (pre_norm_g, post_norm_g, attn_w_in, attn_w_out, conv_w_in, conv_dw_w, conv_dw_b,
                              conv_ln_g, conv_ln_b, conv_w_out, ple_w_proj, ple_w_gate)))
    m_loc = dict(zip(_NAMES, (m_pre_norm_g, m_post_norm_g, m_attn_w_in, m_attn_w_out, m_conv_w_in, m_conv_dw_w,
                              m_conv_dw_b, m_conv_ln_g, m_conv_ln_b, m_conv_w_out, m_ple_w_proj, m_ple_w_gate)))
    v_loc = dict(zip(_NAMES, (v_pre_norm_g, v_post_norm_g, v_attn_w_in, v_attn_w_out, v_conv_w_in, v_conv_dw_w,
                              v_conv_dw_b, v_conv_ln_g, v_conv_ln_b, v_conv_w_out, v_ple_w_proj, v_ple_w_gate)))

    bf = {n: w_loc[n].astype(BF16) for n in _MATMUL_WEIGHTS}
    axes = [_AXIS[n] for n in _MATMUL_WEIGHTS]
    first, small = _allgather_weights([bf[n][:1] for n in _FIRST_LAYER], [_AXIS[n] for n in _FIRST_LAYER],
                                      _stack_small(w_loc))
    send_sems, recv_sems, shards, zones, token = _allgather_start(
        [bf[n][1:] if n in _FIRST_LAYER else bf[n] for n in _MATMUL_WEIGHTS], axes)
    w_full = dict({n: [full[0]] for n, full in zip(_FIRST_LAYER, first)}, **_unstack_small(small))

    def later_weights(after):
        for n, full in zip(_MATMUL_WEIGHTS, _allgather_wait(send_sems, recv_sems, shards, zones, axes, after)):
            w_full[n] = w_full[n] + [full[l] for l in range(full.shape[0])] if n in _FIRST_LAYER else full

    place = _place()
    rest_shapes, first_shapes = _part_shapes(False), _part_shapes(True)
    rest_halves, rest_flight, first_flight = [], [], []

    def later_grads(grads):
        gfull = _pack_full_grads(grads, rest_shapes)
        rest_halves.extend(_exchange_start(gfull.reshape(N_CHIPS, 2, gfull.shape[1] // 2, FLAT_COLS)))
        return rest_halves[4]

    def later_grads_arrived(after):
        rest_flight.extend(_scatter_start(_sum_pair(*_exchange_wait(*rest_halves[:4], after), place), "rest"))
        return rest_flight[4]

    def first_grads(grads):
        first_flight.extend(_scatter_start(_chip_partials(_pack_full_grads(grads, first_shapes), place), "first"))
        return first_flight[4]

    sq, grad_x, dg_pre0 = _local_step(x, p, positions, loss_target, pre_norm_g + token[0, 0], post_norm_g,
                                      w_full, later_weights, later_grads, later_grads_arrived, first_grads)
    loss = lax.psum(sq[0, 0] * (0.5 / D_MODEL), ("x", "y", "c"))

    p_rest, q_rest = _scatter_wait(*rest_flight[:4], grad_x, "rest")
    g_rest = _unpack_f32(_finish_reduce(p_rest, q_rest, place), rest_shapes)
    p_first, q_first = _scatter_wait(*first_flight[:4], grad_x, "first")
    g_first = _unpack_f32(_finish_reduce(p_first, q_first, place), first_shapes)
    g_first["pre_norm_g"] = _allreduce_row(dg_pre0)
    g_out = {n: jnp.concatenate([g_first[n], g_rest[n]]) if n in g_first and n in g_rest
             else g_rest.get(n, g_first.get(n)) for n in _NAMES}
    updates = {n: _adamw(w_loc[n], g_out[n], m_loc[n], v_loc[n]) for n in _NAMES}
    d_out, m_out, v_out = ({n: updates[n][k] for n in _NAMES} for k in range(3))
    return (loss, grad_x, *[g_out[n] for n in _NAMES], *[d_out[n] for n in _NAMES],
            *[m_out[n] for n in _NAMES], *[v_out[n] for n in _NAMES])
```

```python
import math

import jax
import jax.numpy as jnp
from jax import lax
from jax.experimental import pallas as pl
from jax.experimental.pallas import tpu as pltpu

F32 = jnp.float32
BF16 = jnp.bfloat16

D_MODEL = 1024
DEPTH = 4
PLE_DIM = 256
HEAD_DIM = 64
WIN_DIL = ((128, 1), (512, 4), (2048, 16))
N_GROUPS = 3
N_BACK = 128
BLOCK_UNROLL = 8
RESIDUES_TOGETHER = 2
ROPE_THETA = 10000.0
CONV_WIDTH = 31
CONV_HALO = 32
RMS_EPS = 1e-6
LN_EPS = 1e-5
NEG_INF = -1e30
ADAM_LR, ADAM_B1, ADAM_B2, ADAM_EPS, ADAM_WD, ADAM_STEP = 0.001, 0.9, 0.999, 1e-08, 0.01, 10

LANES = 128
N_CHIPS = 4
VMEM_LIMIT = 48 * 1024 * 1024
VMEM_LIMIT_ATTN = 56 * 1024 * 1024
FLAT_COLS = 256
FLAT_TILE = 2048
FLAT_ROW_ALIGN = 16
PROJ_COLS = (3 * N_GROUPS + 1) * D_MODEL
HEAD_PAIRS = D_MODEL // LANES

MESH = pl.DeviceIdType.MESH


def _params(sem=None, vmem=VMEM_LIMIT):
    return pltpu.CompilerParams(dimension_semantics=sem, vmem_limit_bytes=vmem)


def _sigmoid(v):
    return 1.0 / (1.0 + jnp.exp(-v))


def _mm(a, b, *, ta=False, tb=False, out_dtype=F32, tm=1024, tn=1024, tk=1024, name="mm"):
    if ta:
        K, M = a.shape
    else:
        M, K = a.shape
    if tb:
        N, K2 = b.shape
    else:
        K2, N = b.shape
    assert K == K2, (a.shape, b.shape)
    tm, tn, tk = min(tm, M), min(tn, N), min(tk, K)
    assert M % tm == 0 and N % tn == 0 and K % tk == 0
    nk = K // tk
    dims = (((0 if ta else 1,), (1 if tb else 0,)), ((), ()))

    def body(a_ref, b_ref, o_ref, *scratch):
        k = pl.program_id(2)
        part = lax.dot_general(a_ref[...].astype(BF16), b_ref[...].astype(BF16), dims, preferred_element_type=F32)
        if nk == 1:
            o_ref[...] = part.astype(out_dtype)
        else:
            acc_ref, = scratch

            @pl.when(k == 0)
            def _():
                acc_ref[...] = part

            @pl.when((k > 0) & (k < nk - 1))
            def _():
                acc_ref[...] += part

            @pl.when(k == nk - 1)
            def _():
                o_ref[...] = (acc_ref[...] + part).astype(out_dtype)

    a_spec = pl.BlockSpec((tk, tm), lambda i, j, k: (k, i)) if ta else pl.BlockSpec((tm, tk), lambda i, j, k: (i, k))
    b_spec = pl.BlockSpec((tn, tk), lambda i, j, k: (j, k)) if tb else pl.BlockSpec((tk, tn), lambda i, j, k: (k, j))
    return pl.pallas_call(
        body, name=name, grid=(M // tm, N // tn, nk),
        in_specs=[a_spec, b_spec], out_specs=pl.BlockSpec((tm, tn), lambda i, j, k: (i, j)),
        out_shape=jax.ShapeDtypeStruct((M, N), out_dtype),
        scratch_shapes=[pltpu.VMEM((tm, tn), F32)] if nk > 1 else [],
        compiler_params=_params(("parallel", "parallel", "arbitrary")),
    )(a, b)


def _mm_rows(a, b, extras, epilogue, out_dtypes, name, tm=512):
    M, K = a.shape
    N = b.shape[1]
    n_ex = len(extras)

    def body(*refs):
        a_ref, b_ref = refs[:2]
        av = a_ref[...]
        acc = jnp.dot(av.astype(BF16), b_ref[...].astype(BF16), preferred_element_type=F32)
        results = epilogue(acc, av, *[e[...] for e in refs[2:2 + n_ex]])
        for o_ref, r in zip(refs[2 + n_ex:], results):
            o_ref[...] = r.astype(o_ref.dtype)

    tile = pl.BlockSpec((tm, N), lambda i: (i, 0))
    in_specs = [pl.BlockSpec((tm, K), lambda i: (i, 0)), pl.BlockSpec((K, N), lambda i: (0, 0))]
    in_specs += [tile if e.shape[0] == M else pl.BlockSpec((1, N), lambda i: (0, 0)) for e in extras]
    return pl.pallas_call(
        body, name=name, grid=(M // tm,), in_specs=in_specs, out_specs=[tile] * len(out_dtypes),
        out_shape=[jax.ShapeDtypeStruct((M, N), dt) for dt in out_dtypes],
        compiler_params=_params(("parallel",)),
    )(a, b, *extras)


ROW_TILE = 512


def _rows(w=D_MODEL, cb=0, tr=ROW_TILE):
    return pl.BlockSpec((tr, w), lambda i: (i, cb))


def _full(shape):
    return pl.BlockSpec(shape, lambda i: (0,) * len(shape))


def _row_call(body, name, T, in_specs, out_specs, out_shape, args, tr=ROW_TILE):
    return pl.pallas_call(body, name=name, grid=(T // tr,), in_specs=in_specs, out_specs=out_specs,
                          out_shape=out_shape, compiler_params=_params(("arbitrary",)))(*args)


def _sds(shape, dtype):
    return jax.ShapeDtypeStruct(shape, dtype)


def _rmsnorm_fwd(x, g):
    T = x.shape[0]

    def body(x_ref, g_ref, h_ref):
        xv = x_ref[...]
        r = lax.rsqrt(jnp.mean(xv * xv, axis=1, keepdims=True) + RMS_EPS)
        h_ref[...] = (xv * r * g_ref[...]).astype(BF16)

    return _row_call(body, "rmsnorm_fwd", T, [_rows(), _full((1, D_MODEL))], _rows(),
                     _sds((T, D_MODEL), BF16), (x, g))


def _post_epilogue(y, a_tile, x, g):
    del a_tile
    return y, x + y * lax.rsqrt(jnp.mean(y * y, axis=1, keepdims=True) + RMS_EPS) * g


def _ple_epilogue(gl, x1, pe):
    return gl, x1 + pe * _sigmoid(gl)


def _dgrad_pre_bwd(a, b, x, g, dx1, add, tk, name, tm=512):
    M, K = a.shape
    N = b.shape[0]
    tk = min(tk, K)
    assert M % tm == 0 and K % tk == 0 and N == D_MODEL
    nk = K // tk

    def body(*refs):
        a_ref, b_ref, x_ref, g_ref, dx1_ref = refs[:5]
        add_ref = refs[5] if add is not None else None
        dx_ref, dg_ref = refs[-3:-1] if nk > 1 else refs[-2:]
        i, k = pl.program_id(0), pl.program_id(1)

        @pl.when((i == 0) & (k == 0))
        def _():
            dg_ref[...] = jnp.zeros_like(dg_ref)

        part = lax.dot_general(a_ref[...].astype(BF16), b_ref[...].astype(BF16), _NT, preferred_element_type=F32)

        def finish(dh):
            if add is not None:
                dh = dh + add_ref[...]
            xv = x_ref[...]
            r = lax.rsqrt(jnp.mean(xv * xv, axis=1, keepdims=True) + RMS_EPS)
            xh = xv * r
            dg_ref[...] += jnp.sum(dh * xh, axis=0, keepdims=True)
            dn = dh * g_ref[...]
            dx_ref[...] = dx1_ref[...] + r * (dn - xh * jnp.mean(dn * xh, axis=1, keepdims=True))

        if nk == 1:
            finish(part)
        else:
            acc_ref = refs[-1]

            @pl.when(k == 0)
            def _():
                acc_ref[...] = part

            @pl.when((k > 0) & (k < nk - 1))
            def _():
                acc_ref[...] += part

            @pl.when(k == nk - 1)
            def _():
                finish(acc_ref[...] + part)

    tile = pl.BlockSpec((tm, N), lambda i, k: (i, 0))
    row = pl.BlockSpec((1, N), lambda i, k: (0, 0))
    in_specs = [pl.BlockSpec((tm, tk), lambda i, k: (i, k)), pl.BlockSpec((N, tk), lambda i, k: (0, k)), tile, row, tile]
    args = [a, b, x, g, dx1]
    if add is not None:
        in_specs.append(tile)
        args.append(add)
    return pl.pallas_call(
        body, name=name, grid=(M // tm, nk), in_specs=in_specs, out_specs=[tile, row],
        out_shape=[_sds((M, N), F32), _sds((1, N), F32)],
        scratch_shapes=[pltpu.VMEM((tm, N), F32)] if nk > 1 else [],
        compiler_params=_params(("arbitrary", "arbitrary"), VMEM_LIMIT_ATTN),
    )(*args)


def _ple_post_bwd(dx2, pe, gl, w_gate, y, g_post):
    T = dx2.shape[0]

    def body(d_ref, pe_ref, gl_ref, w_ref, y_ref, g_ref, dpe_ref, dgl_ref, dx1_ref, dy_ref, dg_ref):
        @pl.when(pl.program_id(0) == 0)
        def _():
            dg_ref[...] = jnp.zeros_like(dg_ref)

        dv = d_ref[...]
        sg = _sigmoid(gl_ref[...])
        dpe_ref[...] = (dv * sg).astype(BF16)
        dgl = (dv * pe_ref[...] * sg * (1.0 - sg)).astype(BF16)
        dgl_ref[...] = dgl
        dx1 = dv + lax.dot_general(dgl, w_ref[...], _NT, preferred_element_type=F32)
        dx1_ref[...] = dx1
        yv = y_ref[...]
        r = lax.rsqrt(jnp.mean(yv * yv, axis=1, keepdims=True) + RMS_EPS)
        yh = yv * r
        dg_ref[...] += jnp.sum(dx1 * yh, axis=0, keepdims=True)
        dn = dx1 * g_ref[...]
        dy_ref[...] = (r * (dn - yh * jnp.mean(dn * yh, axis=1, keepdims=True))).astype(BF16)

    row = _full((1, D_MODEL))
    return _row_call(body, "ple_post_bwd", T, [_rows()] * 3 + [_full((D_MODEL, D_MODEL)), _rows(), row],
                     [_rows()] * 4 + [row],
                     [_sds((T, D_MODEL), BF16)] * 2 + [_sds((T, D_MODEL), F32), _sds((T, D_MODEL), BF16),
                                                      _sds((1, D_MODEL), F32)],
                     (dx2, pe, gl, w_gate, y, g_post))


def _loss_fwd_bwd(y, target):
    T = y.shape[0]

    def body(y_ref, t_ref, s_ref, d_ref):
        @pl.when(pl.program_id(0) == 0)
        def _():
            s_ref[...] = jnp.zeros_like(s_ref)

        e = y_ref[...] - t_ref[...]
        s_ref[...] += jnp.sum(e * e).reshape(1, 1)
        d_ref[...] = e * (1.0 / D_MODEL)

    return _row_call(body, "loss", T, [_rows()] * 2, [_full((1, 1)), _rows()],
                     [_sds((1, 1), F32), _sds((T, D_MODEL), F32)], (y, target))


def _attn_combine(outs, lses, proj):
    T = proj.shape[0]

    def body(o0, o1, o2, l0, l1, l2, z_ref, a_ref, o_ref, lse_ref):
        a0, a1, a2 = l0[...], l1[...], l2[...]
        m = jnp.maximum(jnp.maximum(a0, a1), a2)
        e0, e1, e2 = jnp.exp(a0 - m), jnp.exp(a1 - m), jnp.exp(a2 - m)
        ssum = e0 + e1 + e2
        o = (e0 * o0[...] + e1 * o1[...] + e2 * o2[...]) / ssum
        zv = z_ref[...].astype(F32)
        o_ref[...] = o
        lse_ref[...] = m + jnp.log(ssum)
        a_ref[...] = (o * zv * _sigmoid(zv)).astype(BF16)

    return _row_call(body, "attn_combine", T, [_rows()] * 6 + [_rows(cb=3 * N_GROUPS)], [_rows()] * 3,
                     [_sds((T, D_MODEL), BF16), _sds((T, D_MODEL), F32), _sds((T, D_MODEL), F32)],
                     (*outs, *lses, proj))


def _gate_bwd(da, o, proj):
    T = da.shape[0]

    def body(da_ref, o_ref, z_ref, do_ref, dz_ref):
        dv = da_ref[...]
        zv = z_ref[...]
        sg = _sigmoid(zv)
        do_ref[...] = dv * zv * sg
        dz_ref[...] = dv * o_ref[...] * sg * (1.0 + zv * (1.0 - sg))

    zcols = _rows(cb=3 * N_GROUPS)
    return _row_call(body, "gate_bwd", T, [_rows(), _rows(), zcols], [_rows(), zcols],
                     [_sds((T, D_MODEL), F32), _sds((T, PROJ_COLS), F32)], (da, o, proj))


def _rope_tables(positions):
    inv_freq = 1.0 / (ROPE_THETA ** (jnp.arange(0, HEAD_DIM, 2, dtype=F32) / HEAD_DIM))
    ang = positions.astype(F32)[..., None] * inv_freq
    cos, sin = jnp.cos(ang), jnp.sin(ang)
    return jnp.tile(cos, (1, 1, 4)), jnp.concatenate([-sin, sin, -sin, sin], axis=-1)


def _rotate_half_partner(t):
    lane = lax.broadcasted_iota(jnp.int32, t.shape, 1)
    return jnp.where((lane % HEAD_DIM) < HEAD_DIM // 2,
                     pltpu.roll(t, LANES - HEAD_DIM // 2, 1), pltpu.roll(t, HEAD_DIM // 2, 1))


def _mask_bias(first):
    qi = lax.broadcasted_iota(jnp.int32, (N_BACK, 2 * N_BACK), 0)
    kj = lax.broadcasted_iota(jnp.int32, (N_BACK, 2 * N_BACK), 1)
    ok = (kj >= qi) & (kj <= qi + N_BACK)
    if first:
        ok = ok & (kj >= N_BACK)
    return jnp.where(ok, 0.0, NEG_INF).astype(F32)


def _stack_heads(t, head0):
    zero = jnp.zeros_like(t)
    return jnp.concatenate([jnp.where(head0, t, zero), jnp.where(head0, zero, t)], axis=0)


def _unstack_heads(t2, head0):
    return jnp.where(head0, t2[:N_BACK], t2[N_BACK:])


def _block_loop(nb, block):
    first, rest = _mask_bias(True), _mask_bias(False)
    first, rest = jnp.concatenate([first, first], axis=0), jnp.concatenate([rest, rest], axis=0)
    if nb <= BLOCK_UNROLL:
        for n in range(nb):
            block(n, first if n == 0 else rest)
        return

    def step(n, carry):
        block(n, jnp.where(n == 0, first, rest))
        return carry

    lax.fori_loop(0, nb, step, 0, unroll=BLOCK_UNROLL)


def _for(count, body, unroll_fully):
    if unroll_fully:
        for i in range(count):
            body(i)
    else:
        lax.fori_loop(0, count, lambda i, carry: (body(i), carry)[1], 0, unroll=4)


def _residues_together(nb):
    return min(RESIDUES_TOGETHER, max(1, BLOCK_UNROLL // nb))


def _residue_loop(d, nb, residue):
    together = _residues_together(nb)
    assert d % together == 0

    def group(i, carry):
        for u in range(together):
            residue(i * together + u, u)
        return carry

    lax.fori_loop(0, d // together, group, 0)


_NT = (((1,), (1,)), ((), ()))
_TN = (((0,), (0,)), ((), ()))


def _residue_rows(r, i, d):
    start = r + i * (N_BACK * d)
    if d == 1:
        return pl.ds(pl.multiple_of(start, N_BACK), N_BACK)
    return pl.ds(start, N_BACK, stride=d)


def _seq_rows(i):
    return pl.ds(pl.multiple_of(i * N_BACK, N_BACK), N_BACK)


def _rows_at(base, i, size=N_BACK):
    return pl.ds(pl.multiple_of(base + i * N_BACK, N_BACK), size)


def _attn_fwd(proj, cos, sin, group, Bl, S):
    d = WIN_DIL[group][1]
    L = S // d
    nb = L // N_BACK
    P = L + N_BACK
    assert WIN_DIL[group][0] // d == N_BACK and L % N_BACK == 0

    def body(q_ref, k_ref, v_ref, cos_ref, sin_ref, o_ref, lse_ref, qr, kr, vp):
        head0 = lax.broadcasted_iota(jnp.int32, (1, LANES), 1) < HEAD_DIM
        zeros = jnp.zeros((N_BACK, LANES), BF16)

        def residue(r, u):
            del u
            qbase, kbase = r * L, r * P
            kr[_rows_at(kbase, 0), :] = zeros
            vp[_rows_at(kbase, 0), :] = zeros

            def rope(i):
                rows = _residue_rows(r, i, d)
                cs, sn = cos_ref[rows, :], sin_ref[rows, :]
                q, k = q_ref[rows, :], k_ref[rows, :]
                qr[_rows_at(qbase, i), :] = ((q * cs + _rotate_half_partner(q) * sn)
                                            * (HEAD_DIM ** -0.5)).astype(BF16)
                kr[_rows_at(kbase, i + 1), :] = (k * cs + _rotate_half_partner(k) * sn).astype(BF16)
                vp[_rows_at(kbase, i + 1), :] = v_ref[rows, :].astype(BF16)

            _for(nb, rope, nb <= BLOCK_UNROLL)

            def block(n, bias):
                win = _rows_at(kbase, n, 2 * N_BACK)
                q2, kw, vw = _stack_heads(qr[_rows_at(qbase, n), :], head0), kr[win, :], vp[win, :]
                s = lax.dot_general(q2, kw, _NT, preferred_element_type=F32) + bias
                m = jnp.max(s, axis=1, keepdims=True)
                p = jnp.exp(s - m)
                l = jnp.sum(p, axis=1, keepdims=True)
                pv = jnp.dot(p.astype(BF16), vw, preferred_element_type=F32)
                rows = _residue_rows(r, n, d)
                o_ref[rows, :] = _unstack_heads(pv * (1.0 / l), head0)
                lse_ref[rows, :] = _unstack_heads((m + jnp.log(l)) + jnp.zeros((2 * N_BACK, LANES), F32), head0)

            _block_loop(nb, block)

        _residue_loop(d, nb, residue)

    act = pl.BlockSpec((None, S, LANES), lambda b, hp: (b, 0, hp))
    tab = pl.BlockSpec((None, S, LANES), lambda b, hp: (b, 0, 0))
    col = lambda which: pl.BlockSpec((None, S, LANES),
                                     lambda b, hp: (b, 0, (which * N_GROUPS + group) * HEAD_PAIRS + hp))
    seq = lambda rows: pl.BlockSpec((None, None, rows, LANES), lambda b, hp: (b, hp, 0, 0))
    p3 = proj.reshape(Bl, S, PROJ_COLS)
    o, lse, qr, kr, vp = pl.pallas_call(
        body, name="attn_fwd_g%d" % group, grid=(Bl, HEAD_PAIRS),
        in_specs=[col(0), col(1), col(2), tab, tab], out_specs=[act, act, seq(S), seq(d * P), seq(d * P)],
        out_shape=[_sds((Bl, S, D_MODEL), F32)] * 2 + [_sds((Bl, HEAD_PAIRS, S, LANES), BF16)]
        + [_sds((Bl, HEAD_PAIRS, d * P, LANES), BF16)] * 2,
        compiler_params=_params(("parallel", "arbitrary"), VMEM_LIMIT_ATTN),
    )(p3, p3, p3, cos, sin)
    return o.reshape(Bl * S, D_MODEL), lse.reshape(Bl * S, D_MODEL), (qr, kr, vp)


def _attn_bwd(saved, cos, sin, do, o, lse, dproj, group, Bl, S):
    d = WIN_DIL[group][1]
    L = S // d
    nb = L // N_BACK
    P = L + N_BACK
    steps = Bl * HEAD_PAIRS

    def body(qr, kr, vp, cos_ref, sin_ref, do_ref, o_ref, lse_ref, dproj_in, dproj_ref,
             dk_accs, dv_accs, stage, sems):
        del dproj_in
        head0 = lax.broadcasted_iota(jnp.int32, (1, LANES), 1) < HEAD_DIM
        b, hp = pl.program_id(0), pl.program_id(1)
        step = b * HEAD_PAIRS + hp
        slot = step % 2
        dq_s, dk_s, dv_s = stage.at[slot, 0], stage.at[slot, 1], stage.at[slot, 2]

        def copies(which_slot):
            out = []
            for which in range(3):
                col = ((which * N_GROUPS + group) * HEAD_PAIRS + hp) * LANES
                out.append(pltpu.make_async_copy(
                    stage.at[which_slot, which], dproj_ref.at[b, :, pl.ds(pl.multiple_of(col, LANES), LANES)],
                    sems.at[which_slot, which]))
            return out

        @pl.when(step >= 2)
        def _():
            for cp in copies(slot):
                cp.wait()

        def residue(r, u):
            qbase, kbase = r * L, r * P
            dk_acc, dv_acc = dk_accs.at[u], dv_accs.at[u]
            dk_acc[...] = jnp.zeros_like(dk_acc)
            dv_acc[...] = jnp.zeros_like(dv_acc)

            def block(n, bias):
                win = pl.ds(pl.multiple_of(n * N_BACK, N_BACK), 2 * N_BACK)
                kwin = _rows_at(kbase, n, 2 * N_BACK)
                rows = _residue_rows(r, n, d)
                q2, kw, vw = _stack_heads(qr[_rows_at(qbase, n), :], head0), kr[kwin, :], vp[kwin, :]
                dof = do_ref[rows, :]
                do2 = _stack_heads(dof.astype(BF16), head0)
                lse_b = lse_ref[rows, :]
                lse2 = jnp.concatenate([lse_b[:, 0:1], lse_b[:, HEAD_DIM:HEAD_DIM + 1]], axis=0)
                dsum = _stack_heads(dof * o_ref[rows, :], head0)
                delta = jnp.sum(dsum, axis=1, keepdims=True)
                s = lax.dot_general(q2, kw, _NT, preferred_element_type=F32) + bias
                p = jnp.exp(s - lse2)
                dp = lax.dot_general(do2, vw, _NT, preferred_element_type=F32)
                ds = (p * (dp - delta)).astype(BF16)
                dq = _unstack_heads(jnp.dot(ds, kw, preferred_element_type=F32), head0) * (HEAD_DIM ** -0.5)
                cs, sn = cos_ref[rows, :], sin_ref[rows, :]
                dq_s[rows, :] = dq * cs + _rotate_half_partner(dq * sn)
                dk_acc[win, :] += lax.dot_general(ds, q2, _TN, preferred_element_type=F32)
                dv_acc[win, :] += lax.dot_general(p.astype(BF16), do2, _TN, preferred_element_type=F32)

            _block_loop(nb, block)

            def finish(i):
                rows = _residue_rows(r, i, d)
                cs, sn = cos_ref[rows, :], sin_ref[rows, :]
                dk = dk_acc[_seq_rows(i + 1), :]
                dk_s[rows, :] = dk * cs + _rotate_half_partner(dk * sn)
                dv_s[rows, :] = dv_acc[_seq_rows(i + 1), :]

            _for(nb, finish, nb <= BLOCK_UNROLL)

        _residue_loop(d, nb, residue)
        for cp in copies(slot):
            cp.start()

        @pl.when(step == steps - 1)
        def _():
            if steps > 1:
                for cp in copies(1 - slot):
                    cp.wait()
            for cp in copies(slot):
                cp.wait()

    act = pl.BlockSpec((None, S, LANES), lambda b, hp: (b, 0, hp))
    tab = pl.BlockSpec((None, S, LANES), lambda b, hp: (b, 0, 0))
    seq = lambda rows: pl.BlockSpec((None, None, rows, LANES), lambda b, hp: (b, hp, 0, 0))
    view = lambda t: t.reshape(Bl, S, D_MODEL)
    out = pl.pallas_call(
        body, name="attn_bwd_g%d" % group, grid=(Bl, HEAD_PAIRS),
        in_specs=[seq(S), seq(d * P), seq(d * P), tab, tab, act, act, act, _ANY], out_specs=_ANY,
        out_shape=_sds((Bl, S, PROJ_COLS), F32), input_output_aliases={8: 0},
        scratch_shapes=[pltpu.VMEM((_residues_together(nb), P, LANES), F32),
                        pltpu.VMEM((_residues_together(nb), P, LANES), F32),
                        pltpu.VMEM((2, 3, S, LANES), F32), pltpu.SemaphoreType.DMA((2, 3))],
        compiler_params=_params(("arbitrary", "arbitrary"), VMEM_LIMIT_ATTN),
    )(*saved, cos, sin, view(do), view(o), view(lse), dproj.reshape(Bl, S, PROJ_COLS))
    return out.reshape(Bl * S, PROJ_COLS)


CONV_TILE = 256
CONV_CHUNK = 64
SUBLANES = 8
CONV_SHIFT_ROWS = CONV_TILE + CONV_HALO - SUBLANES


def _fill_shifted(shifted, ext, cs):
    for k in range(1, SUBLANES):
        shifted[k - 1] = ext[pl.ds(k, CONV_SHIFT_ROWS), cs]


def _shifted_rows(shifted, ext, cs, off):
    k = off % SUBLANES
    if k == 0:
        return ext[pl.ds(off, CONV_CHUNK), cs]
    return shifted[k - 1, pl.ds(off - k, CONV_CHUNK), :]


def _conv_fwd(proj, z, dw, dwb, ln_g, ln_b, Bl, S):
    tr = CONV_TILE
    nj = S // tr
    hb = tr // CONV_HALO

    def body(a_ref, b_ref, ah_ref, bh_ref, z_ref, dw_ref, dwb_ref, g_ref, bb_ref, u1_ref, out_ref, ext, shifted):
        j = pl.program_id(1)
        halo = ah_ref[0].astype(F32) * _sigmoid(bh_ref[0].astype(F32))
        ext[pl.ds(0, CONV_HALO), :] = jnp.where(j > 0, halo, 0.0)
        ext[pl.ds(CONV_HALO, tr), :] = a_ref[0].astype(F32) * _sigmoid(b_ref[0].astype(F32))

        def cols(c, carry):
            cs = pl.ds(pl.multiple_of(c * LANES, LANES), LANES)
            _fill_shifted(shifted, ext, cs)
            for rc in range(tr // CONV_CHUNK):
                acc = jnp.zeros((CONV_CHUNK, LANES), F32)
                for w in range(CONV_WIDTH):
                    off = rc * CONV_CHUNK + CONV_HALO - (CONV_WIDTH - 1) + w
                    acc = acc + dw_ref[pl.ds(w, 1), cs] * _shifted_rows(shifted, ext, cs, off)
                u1_ref[0, pl.ds(rc * CONV_CHUNK, CONV_CHUNK), cs] = acc + dwb_ref[:, cs]
            return carry

        lax.fori_loop(0, D_MODEL // LANES, cols, 0)
        u1 = u1_ref[0]
        mu = jnp.mean(u1, axis=1, keepdims=True)
        xc = u1 - mu
        rstd = lax.rsqrt(jnp.mean(xc * xc, axis=1, keepdims=True) + LN_EPS)
        u2 = xc * rstd * g_ref[...] + bb_ref[...]
        zv = z_ref[0].astype(F32)
        out_ref[0] = (u2 * _sigmoid(u2) * zv * _sigmoid(zv)).astype(BF16)

    tile = lambda cb: pl.BlockSpec((1, tr, D_MODEL), lambda b, j: (b, j, cb))
    halo = lambda cb: pl.BlockSpec((1, CONV_HALO, D_MODEL), lambda b, j: (b, jnp.maximum(j * hb - 1, 0), cb))
    par = lambda r: pl.BlockSpec((r, D_MODEL), lambda b, j: (0, 0))
    p3 = proj.reshape(Bl, S, 2 * D_MODEL)
    u1, out = pl.pallas_call(
        body, name="conv_fwd", grid=(Bl, nj),
        in_specs=[tile(0), tile(1), halo(0), halo(1), tile(0), par(32), par(1), par(1), par(1)],
        out_specs=[tile(0), tile(0)],
        out_shape=[_sds((Bl, S, D_MODEL), F32), _sds((Bl, S, D_MODEL), BF16)],
        scratch_shapes=[pltpu.VMEM((tr + CONV_HALO, D_MODEL), F32),
                        pltpu.VMEM((SUBLANES - 1, CONV_SHIFT_ROWS, LANES), F32)],
        compiler_params=_params(("parallel", "arbitrary")),
    )(p3, p3, p3, p3, z.reshape(Bl, S, D_MODEL), dw, dwb, ln_g, ln_b)
    return u1.reshape(Bl * S, D_MODEL), out.reshape(Bl * S, D_MODEL)


def _conv_norm_bwd(da2, z, u1, ln_g, ln_b):
    T = da2.shape[0]

    def body(da_ref, z_ref, u_ref, g_ref, b_ref, du_ref, dz_ref, dg_ref, db_ref):
        @pl.when(pl.program_id(0) == 0)
        def _():
            dg_ref[...] = jnp.zeros_like(dg_ref)
            db_ref[...] = jnp.zeros_like(db_ref)

        quarter = ROW_TILE // 4
        for part in range(4):
            rows = pl.ds(part * quarter, quarter)
            u1 = u_ref[rows, :]
            mu = jnp.mean(u1, axis=1, keepdims=True)
            xc = u1 - mu
            rstd = lax.rsqrt(jnp.mean(xc * xc, axis=1, keepdims=True) + LN_EPS)
            nrm = xc * rstd
            u2 = nrm * g_ref[...] + b_ref[...]
            s2 = _sigmoid(u2)
            zv = z_ref[rows, :].astype(F32)
            sz = _sigmoid(zv)
            dv = da_ref[rows, :]
            dz_ref[rows, :] = (dv * u2 * s2 * sz * (1.0 + zv * (1.0 - sz))).astype(BF16)
            du2 = dv * zv * sz * s2 * (1.0 + u2 * (1.0 - s2))
            dg_ref[...] += jnp.sum(du2 * nrm, axis=0, keepdims=True)
            db_ref[...] += jnp.sum(du2, axis=0, keepdims=True)
            dn = du2 * g_ref[...]
            du_ref[rows, :] = rstd * (dn - jnp.mean(dn, axis=1, keepdims=True)
                                      - nrm * jnp.mean(dn * nrm, axis=1, keepdims=True))

    return _row_call(body, "conv_norm_bwd", T,
                     [_rows(), _rows(), _rows(), _full((1, D_MODEL)), _full((1, D_MODEL))],
                     [_rows(), _rows(), _full((1, D_MODEL)), _full((1, D_MODEL))],
                     [_sds((T, D_MODEL), F32), _sds((T, D_MODEL), BF16), _sds((1, D_MODEL), F32),
                      _sds((1, D_MODEL), F32)], (da2, z, u1, ln_g, ln_b))


def _conv_bwd(proj, du1, dw, Bl, S):
    tr = CONV_TILE
    nj = S // tr
    hb = tr // CONV_HALO

    def body(a_ref, b_ref, ah_ref, bh_ref, du_ref, duh_ref, dw_ref, dab_ref, ddw_ref, ddb_ref, uext, dext, du0,
             ushift, dshift, ddw8):
        first = (pl.program_id(0) == 0) & (pl.program_id(1) == 0)
        last = (pl.program_id(0) == Bl - 1) & (pl.program_id(1) == nj - 1)
        j = pl.program_id(1)

        @pl.when(first)
        def _():
            ddw8[...] = jnp.zeros_like(ddw8)
            ddb_ref[...] = jnp.zeros_like(ddb_ref)

        halo = ah_ref[0].astype(F32) * _sigmoid(bh_ref[0].astype(F32))
        uext[pl.ds(0, CONV_HALO), :] = jnp.where(j > 0, halo, 0.0)
        av = a_ref[0].astype(F32)
        sb = _sigmoid(b_ref[0].astype(F32))
        uext[pl.ds(CONV_HALO, tr), :] = av * sb
        dext[pl.ds(0, tr), :] = du_ref[0]
        dext[pl.ds(tr, CONV_HALO), :] = jnp.where(j < nj - 1, duh_ref[0], 0.0)
        ddb_ref[...] += jnp.sum(du_ref[0], axis=0, keepdims=True)

        def cols(c, carry):
            cs = pl.ds(pl.multiple_of(c * LANES, LANES), LANES)
            _fill_shifted(dshift, dext, cs)
            _fill_shifted(ushift, uext, cs)
            for rc in range(tr // CONV_CHUNK):
                base = rc * CONV_CHUNK
                acc = jnp.zeros((CONV_CHUNK, LANES), F32)
                for w in range(CONV_WIDTH):
                    acc = acc + dw_ref[pl.ds(w, 1), cs] * _shifted_rows(dshift, dext, cs, base + CONV_WIDTH - 1 - w)
                du0[pl.ds(base, CONV_CHUNK), cs] = acc
            for w in range(CONV_WIDTH):
                part = jnp.zeros((SUBLANES, LANES), F32)
                for rc in range(tr // CONV_CHUNK):
                    base = rc * CONV_CHUNK
                    prod = dext[pl.ds(base, CONV_CHUNK), cs] * _shifted_rows(
                        ushift, uext, cs, base + CONV_HALO - (CONV_WIDTH - 1) + w)
                    for i in range(CONV_CHUNK // SUBLANES):
                        part = part + prod[i * SUBLANES:(i + 1) * SUBLANES]
                ddw8[pl.ds(w * SUBLANES, SUBLANES), cs] += part
            return carry

        lax.fori_loop(0, D_MODEL // LANES, cols, 0)
        g = du0[...]
        dab_ref[0, :, 0:D_MODEL] = (g * sb).astype(BF16)
        dab_ref[0, :, D_MODEL:2 * D_MODEL] = (g * av * sb * (1.0 - sb)).astype(BF16)

        @pl.when(last)
        def _():
            for w in range(CONV_WIDTH + 1):
                ddw_ref[pl.ds(w, 1), :] = jnp.sum(ddw8[pl.ds(w * SUBLANES, SUBLANES), :], axis=0, keepdims=True)

    tile = lambda cb: pl.BlockSpec((1, tr, D_MODEL), lambda b, j: (b, j, cb))
    halo = lambda cb: pl.BlockSpec((1, CONV_HALO, D_MODEL), lambda b, j: (b, jnp.maximum(j * hb - 1, 0), cb))
    nxt = pl.BlockSpec((1, CONV_HALO, D_MODEL), lambda b, j: (b, jnp.minimum((j + 1) * hb, S // CONV_HALO - 1), 0))
    par = lambda r: pl.BlockSpec((r, D_MODEL), lambda b, j: (0, 0))
    p3 = proj.reshape(Bl, S, 2 * D_MODEL)
    d3 = du1.reshape(Bl, S, D_MODEL)
    dab, ddw, ddb = pl.pallas_call(
        body, name="conv_bwd", grid=(Bl, nj),
        in_specs=[tile(0), tile(1), halo(0), halo(1), tile(0), nxt, par(32)],
        out_specs=[pl.BlockSpec((1, tr, 2 * D_MODEL), lambda b, j: (b, j, 0)), par(32), par(1)],
        out_shape=[_sds((Bl, S, 2 * D_MODEL), BF16), _sds((32, D_MODEL), F32), _sds((1, D_MODEL), F32)],
        scratch_shapes=[pltpu.VMEM((tr + CONV_HALO, D_MODEL), F32), pltpu.VMEM((tr + CONV_HALO, D_MODEL), F32),
                        pltpu.VMEM((tr, D_MODEL), F32),
                        pltpu.VMEM((SUBLANES - 1, CONV_SHIFT_ROWS, LANES), F32),
                        pltpu.VMEM((SUBLANES - 1, CONV_SHIFT_ROWS, LANES), F32),
                        pltpu.VMEM(((CONV_WIDTH + 1) * SUBLANES, D_MODEL), F32)],
        compiler_params=_params(("arbitrary", "arbitrary")),
    )(p3, p3, p3, p3, d3, d3, dw)
    return dab.reshape(Bl * S, 2 * D_MODEL), ddw, ddb


_LAYOUT = (
    ("pre_norm_g", (4, 1024), None), ("post_norm_g", (4, 1024), None),
    ("attn_w_in", (2, 1024, 2560), 2), ("attn_w_out", (2, 256, 1024), 1),
    ("conv_w_in", (2, 1024, 768), 2), ("conv_dw_w", (2, 31, 256), 2),
    ("conv_dw_b", (2, 256), 1), ("conv_ln_g", (2, 256), 1), ("conv_ln_b", (2, 256), 1),
    ("conv_w_out", (2, 256, 1024), 1), ("ple_w_proj", (4, 256, 256), 2), ("ple_w_gate", (4, 256, 1024), 1),
)
_MATMUL_WEIGHTS = ("attn_w_in", "attn_w_out", "conv_w_in", "conv_w_out", "ple_w_proj", "ple_w_gate")
_FIRST_LAYER = ("attn_w_in", "attn_w_out", "ple_w_proj", "ple_w_gate")
_AXIS = {n: a for n, _, a in _LAYOUT}


def _size(shape):
    n = 1
    for s in shape:
        n *= s
    return n


def _padded_rows(shape):
    rows = _size(shape) // shape[-1]
    return rows + (-rows) % FLAT_ROW_ALIGN


def _rows2d(a):
    a2 = a.reshape(-1, a.shape[-1])
    pad = _padded_rows(a.shape) - a2.shape[0]
    return jnp.pad(a2, ((0, pad), (0, 0))) if pad else a2


def _col_blocks(a):
    a2 = _rows2d(a)
    return jnp.concatenate([a2[:, c:c + FLAT_COLS] for c in range(0, a2.shape[1], FLAT_COLS)], axis=0)


def _from_col_blocks(flat, off, shape):
    rows, nblk = _padded_rows(shape), shape[-1] // FLAT_COLS
    a2 = jnp.concatenate([flat[off + b * rows:off + (b + 1) * rows] for b in range(nblk)], axis=1)
    return a2[:_size(shape) // shape[-1]].reshape(shape), off + nblk * rows


def _shard_col_blocks(full, shape, axis):
    if axis is None:
        blocks = _col_blocks(full)
        return jnp.broadcast_to(blocks[None], (N_CHIPS,) + blocks.shape)
    m = shape[-1]
    if axis == len(shape) - 1:
        a2 = _rows2d(full)
        pieces = [a2[:, c:c + FLAT_COLS] for c in range(0, N_CHIPS * m, FLAT_COLS)]
    else:
        layers, r, _ = shape
        assert axis == 1 and (layers * r) % FLAT_ROW_ALIGN == 0
        pieces = [full[:, s * r:(s + 1) * r, c:c + FLAT_COLS].reshape(layers * r, FLAT_COLS)
                  for s in range(N_CHIPS) for c in range(0, m, FLAT_COLS)]
    return jnp.concatenate(pieces, axis=0).reshape(N_CHIPS, -1, FLAT_COLS)


_FLAT_BIG = ("attn_w_in", "conv_w_in", "ple_w_gate", "attn_w_out", "conv_w_out", "ple_w_proj")
_FLAT_SMALL = ("pre_norm_g", "post_norm_g", "conv_dw_w", "conv_dw_b", "conv_ln_g", "conv_ln_b")
PACK_TILE = 1024


_FIRST_LAYER_PARAMS = _FIRST_LAYER + ("pre_norm_g", "post_norm_g")


def _part_shapes(first):
    out = {}
    for n, shape, _ in _LAYOUT:
        layers = (1 if first else shape[0] - 1) if n in _FIRST_LAYER_PARAMS else (0 if first else shape[0])
        if layers and not (first and n == "pre_norm_g"):
            out[n] = (layers,) + shape[1:]
    return out


def _flat_plan(shapes):
    out, off = {}, 0
    for n in _FLAT_BIG + _FLAT_SMALL:
        if n in shapes:
            out[n] = off
            off += _padded_rows(shapes[n]) * (shapes[n][-1] // FLAT_COLS)
    return out, off + (-off) % (2 * FLAT_TILE)


def _unpack_f32(flat, shapes):
    offsets, _ = _flat_plan(shapes)
    return {n: _from_col_blocks(flat, offsets[n], shapes[n])[0] for n in shapes}


def _pack_param(full, shape, axis, off, flat, total_rows):
    layers, r, m = shape
    nblk = m // FLAT_COLS
    if axis == 2:
        rows = layers * r
        tr = math.gcd(math.gcd(rows, PACK_TILE), off) if off else math.gcd(rows, PACK_TILE)
        assert rows % tr == 0 and off % tr == 0 and tr % FLAT_ROW_ALIGN == 0
        src = full.reshape(rows, N_CHIPS * m)
        grid = (N_CHIPS * nblk, rows // tr)
        in_spec = pl.BlockSpec((tr, FLAT_COLS), lambda j, i: (i, j))
        out_spec = pl.BlockSpec((None, tr, FLAT_COLS), lambda j, i: (j // nblk, (off + (j % nblk) * rows) // tr + i, 0))
    else:
        assert axis == 1 and off % r == 0
        src = full.reshape(layers * N_CHIPS * r, m)
        grid = (layers, N_CHIPS, nblk)
        in_spec = pl.BlockSpec((r, FLAT_COLS), lambda l, s, b: (l * N_CHIPS + s, b))
        out_spec = pl.BlockSpec((None, r, FLAT_COLS), lambda l, s, b: (s, (off + b * layers * r) // r + l, 0))

    def copy_body(src_ref, *rest):
        rest[-1][...] = src_ref[...]

    args, in_specs, aliases = [src], [in_spec], {}
    if flat is not None:
        args.append(flat)
        in_specs.append(_ANY)
        aliases = {1: 0}
    return pl.pallas_call(
        copy_body, name="pack_grad", grid=grid, in_specs=in_specs, out_specs=out_spec,
        out_shape=_sds((N_CHIPS, total_rows, FLAT_COLS), F32), input_output_aliases=aliases,
        compiler_params=_params(("arbitrary",) * len(grid)))(*args)


SMALL_ROWS = 40


def _stack_small(w):
    rows = [w["conv_dw_w"]] + [w[n][:, None, :] for n in ("conv_dw_b", "conv_ln_g", "conv_ln_b")]
    stacked = jnp.concatenate(rows, axis=1)
    return jnp.pad(stacked, ((0, 0), (0, SMALL_ROWS - stacked.shape[1]), (0, 0)))


def _unstack_small(small):
    return {"conv_dw_w": small[:, :CONV_WIDTH], "conv_dw_b": small[:, CONV_WIDTH],
            "conv_ln_g": small[:, CONV_WIDTH + 1], "conv_ln_b": small[:, CONV_WIDTH + 2]}


def _pack_full_grads(grads, shapes):
    offsets, total_rows = _flat_plan(shapes)
    flat = None
    for n in _FLAT_BIG:
        if n in shapes:
            flat = _pack_param(grads[n], shapes[n], _AXIS[n], offsets[n], flat, total_rows)
    small_names = [n for n in _FLAT_SMALL if n in shapes]
    small = jnp.concatenate([_shard_col_blocks(grads[n], shapes[n], _AXIS[n]) for n in small_names], axis=1)
    start = offsets[small_names[0]]
    small = jnp.pad(small, ((0, 0), (0, total_rows - start - small.shape[1]), (0, 0)))
    return lax.dynamic_update_slice(flat, small, (0, start, 0))


_ANY = pl.BlockSpec(memory_space=pl.ANY)


def _mesh_pos():
    return lax.axis_index("x"), lax.axis_index("y"), lax.axis_index("c")


def _other_chips(x, y):
    return [(1 - x, y), (x, 1 - y), (1 - x, 1 - y)]


COPIES_PER_ARRAY = 7


def _allgather_weights(shards, axes, small):
    n = len(shards)
    full_shape = lambda a, axis: tuple(d * (N_CHIPS if i == axis else 1) for i, d in enumerate(a.shape))

    def body(*refs):
        ins, small_in = refs[:n], refs[n]
        outs, small_out = refs[n + 1:2 * n + 1], refs[2 * n + 1]
        send_sems, recv_sems = refs[2 * n + 2:]
        x, y, c = _mesh_pos()
        mine, me, sibling = 2 * x + y, (x, y, c), (x, y, 1 - c)
        chips = _other_chips(x, y)

        def region(a, chip, half):
            _, rows, cols = shards[a].shape
            h = rows // 2
            if axes[a] == 2:
                return outs[a].at[:, slice(None) if half is None else pl.ds(half * h, h), pl.ds(chip * cols, cols)]
            if half is None:
                return outs[a].at[:, pl.ds(chip * rows, rows), :]
            return outs[a].at[:, pl.ds(chip * rows + half * h, h), :]

        def copy(k, src, dst, to):
            return pltpu.make_async_remote_copy(src_ref=src, dst_ref=dst, send_sem=send_sems.at[k],
                                                recv_sem=recv_sems.at[k], device_id=to, device_id_type=MESH)

        def arrival(k, dst):
            return copy(k, dst, dst, me)

        sends = []
        for a in range(n):
            h = shards[a].shape[1] // 2
            base = a * COPIES_PER_ARRAY
            sends.append(copy(base + 6, ins[a], region(a, mine, None), sibling))
            for j, (cx, cy) in enumerate(chips):
                sends.append(copy(base + j, ins[a].at[:, pl.ds(c * h, h), :], region(a, mine, c), (cx, cy, c)))
        small_cols = small.shape[2]
        small_region = lambda chip: small_out.at[:, :, pl.ds(chip * small_cols, small_cols)]
        base = n * COPIES_PER_ARRAY
        sends.append(copy(base + 3, small_in, small_region(mine), sibling))
        for j, (cx, cy) in enumerate(chips):
            sends.append(copy(base + j, small_in, small_region(mine), (cx, cy, c)))
        for cp in sends:
            cp.start()
        for j, (cx, cy) in enumerate(chips):
            for a in range(n):
                k = a * COPIES_PER_ARRAY + j
                arrival(k, region(a, 2 * cx + cy, c)).wait_recv()
                passed = copy(k + 3, region(a, 2 * cx + cy, c), region(a, 2 * cx + cy, c), sibling)
                passed.start()
                sends.append(passed)
        for j, (cx, cy) in enumerate(chips):
            for a in range(n):
                arrival(a * COPIES_PER_ARRAY + 3 + j, region(a, 2 * cx + cy, 1 - c)).wait_recv()
            arrival(base + j, small_region(2 * cx + cy)).wait_recv()
        for a in range(n):
            arrival(a * COPIES_PER_ARRAY + 6, region(a, mine, None)).wait_recv()
        arrival(base + 3, small_region(mine)).wait_recv()
        for cp in sends:
            cp.wait_send()

    n_sems = n * COPIES_PER_ARRAY + 4
    out = pl.pallas_call(
        body, name="allgather_weights", in_specs=[_ANY] * (n + 1), out_specs=[_ANY] * (n + 1),
        out_shape=[_sds(full_shape(a, axis), a.dtype) for a, axis in zip(shards, axes)]
        + [_sds(full_shape(small, 2), small.dtype)],
        scratch_shapes=[pltpu.SemaphoreType.DMA((n_sems,)), pltpu.SemaphoreType.DMA((n_sems,))],
    )(*shards, small)
    return out[:n], out[n]


_HBM = pl.BlockSpec(memory_space=pltpu.HBM)
_SEM = pl.BlockSpec(memory_space=pltpu.SEMAPHORE)


def _full_shape(shard, axis):
    return tuple(d * (N_CHIPS if i == axis else 1) for i, d in enumerate(shard.shape))


def _direct_gather_copies(srcs, lands, shapes, axes, send_sems, recv_sems):
    x, y, c = _mesh_pos()
    mine, me, sibling = 2 * x + y, (x, y, c), (x, y, 1 - c)

    def region(a, chip, half):
        _, rows, cols = shapes[a]
        h = rows // 2
        if axes[a] == 2:
            return lands[a].at[:, slice(None) if half is None else pl.ds(half * h, h), pl.ds(chip * cols, cols)]
        if half is None:
            return lands[a].at[:, pl.ds(chip * rows, rows), :]
        return lands[a].at[:, pl.ds(chip * rows + half * h, h), :]

    def copy(k, src, dst, to):
        return pltpu.make_async_remote_copy(src_ref=src, dst_ref=dst, send_sem=send_sems.at[k],
                                            recv_sem=recv_sems.at[k], device_id=to, device_id_type=MESH)

    outgoing, incoming = [], []
    for a in range(len(srcs)):
        h = shapes[a][1] // 2
        base = a * COPIES_PER_ARRAY
        for j, (cx, cy) in enumerate(_other_chips(x, y)):
            for t, ct in enumerate((c, 1 - c)):
                k = base + 2 * j + t
                outgoing.append(copy(k, srcs[a].at[:, pl.ds(c * h, h), :], region(a, mine, c), (cx, cy, ct)))
                landed = region(a, 2 * cx + cy, ct)
                incoming.append(copy(k, landed, landed, me))
        outgoing.append(copy(base + 6, srcs[a], region(a, mine, None), sibling))
        incoming.append(copy(base + 6, region(a, mine, None), region(a, mine, None), me))
    return outgoing, incoming


def _allgather_start(shards, axes):
    n = len(shards)
    shapes = [s.shape for s in shards]

    def body(*refs):
        srcs, lands = refs[:n], refs[n:2 * n]
        send_sems, recv_sems = refs[2 * n], refs[2 * n + 1]
        token = refs[-1]
        outgoing, _ = _direct_gather_copies(srcs, lands, shapes, axes, send_sems, recv_sems)
        for cp in outgoing:
            cp.start()
        token[...] = jnp.zeros_like(token)

    n_sems = n * COPIES_PER_ARRAY
    zones = [pltpu.with_memory_space_constraint(lax.empty(_full_shape(s, ax), s.dtype), pltpu.HBM)
             for s, ax in zip(shards, axes)]
    out = pl.pallas_call(
        body, name="allgather_rest_start",
        out_shape=(pltpu.SemaphoreType.DMA((n_sems,)), pltpu.SemaphoreType.DMA((n_sems,)),
                   *[pltpu.HBM(s.shape, s.dtype) for s in shards], *[pltpu.HBM(z.shape, z.dtype) for z in zones],
                   jax.ShapeDtypeStruct((8, LANES), F32)),
        in_specs=[_HBM] * (2 * n),
        out_specs=(_SEM, _SEM, *[_HBM] * (2 * n), pl.BlockSpec(memory_space=pltpu.VMEM)),
        input_output_aliases={i: 2 + i for i in range(2 * n)},
        compiler_params=pltpu.CompilerParams(has_side_effects=pltpu.SideEffectType.DATAFLOW_SIDE_EFFECTING),
    )(*[pltpu.with_memory_space_constraint(s, pltpu.HBM) for s in shards], *zones)
    return out[0], out[1], out[2:2 + n], out[2 + n:2 + 2 * n], out[-1]


def _allgather_wait(send_sems, recv_sems, shards, zones, axes, after):
    n = len(shards)
    shapes = [s.shape for s in shards]

    def body(*refs):
        srcs, lands = refs[:n], refs[n:2 * n]
        outgoing, incoming = _direct_gather_copies(srcs, lands, shapes, axes, refs[2 * n], refs[2 * n + 1])
        for cp in outgoing:
            cp.wait_send()
        for cp in incoming:
            cp.wait_recv()

    out = pl.pallas_call(
        body, name="allgather_rest_wait",
        out_shape=(*[pltpu.HBM(s.shape, s.dtype) for s in shards], *[pltpu.HBM(z.shape, z.dtype) for z in zones]),
        in_specs=[_HBM] * (2 * n) + [_SEM, _SEM, _ANY], out_specs=[_HBM] * (2 * n),
        input_output_aliases={i: i for i in range(2 * n)},
        compiler_params=pltpu.CompilerParams(has_side_effects=pltpu.SideEffectType.DATAFLOW_SIDE_EFFECTING),
    )(*shards, *zones, send_sems, recv_sems, after)
    return out[n:]


def _exchange_core_halves(g):
    n, _, H, C = g.shape

    def body(g_ref, got_ref, send_sem, recv_sem):
        x, y, c = _mesh_pos()
        swap = pltpu.make_async_remote_copy(
            src_ref=g_ref.at[pl.ds(0, n), 1 - c], dst_ref=got_ref, send_sem=send_sem, recv_sem=recv_sem,
            device_id=(x, y, 1 - c), device_id_type=MESH)
        swap.start()
        swap.wait()

    return pl.pallas_call(
        body, name="exchange_core_halves", in_specs=[_ANY], out_specs=_ANY,
        out_shape=_sds((n, H, C), g.dtype),
        scratch_shapes=[pltpu.SemaphoreType.DMA, pltpu.SemaphoreType.DMA],
    )(g)


def _exchange_copy(g_ref, got_ref, send_sems, recv_sems):
    x, y, c = _mesh_pos()
    return pltpu.make_async_remote_copy(
        src_ref=g_ref.at[pl.ds(0, g_ref.shape[0]), 1 - c], dst_ref=got_ref, send_sem=send_sems.at[0],
        recv_sem=recv_sems.at[0], device_id=(x, y, 1 - c), device_id_type=MESH)


def _exchange_start(g):
    n, _, H, C = g.shape

    def body(g_ref, got_ref, send_sems, recv_sems, g_thru, got_thru, token):
        _exchange_copy(g_ref, got_ref, send_sems, recv_sems).start()
        token[...] = jnp.zeros_like(token)

    zone = pltpu.with_memory_space_constraint(lax.empty((n, H, C), g.dtype), pltpu.HBM)
    return pl.pallas_call(
        body, name="exchange_rest_start",
        out_shape=(pltpu.SemaphoreType.DMA((1,)), pltpu.SemaphoreType.DMA((1,)), pltpu.HBM(g.shape, g.dtype),
                   pltpu.HBM(zone.shape, zone.dtype), jax.ShapeDtypeStruct((8, LANES), F32)),
        in_specs=[_HBM, _HBM], out_specs=(_SEM, _SEM, _HBM, _HBM, pl.BlockSpec(memory_space=pltpu.VMEM)),
        input_output_aliases={0: 2, 1: 3},
        compiler_params=pltpu.CompilerParams(has_side_effects=pltpu.SideEffectType.DATAFLOW_SIDE_EFFECTING),
    )(pltpu.with_memory_space_constraint(g, pltpu.HBM), zone)


def _exchange_wait(send_sems, recv_sems, g, zone, after):
    def body(g_ref, got_ref, send_sems, recv_sems, after_ref, g_out, got_out):
        copy = _exchange_copy(g_ref, got_ref, send_sems, recv_sems)
        copy.wait_send()
        copy.wait_recv()

    return pl.pallas_call(
        body, name="exchange_rest_wait",
        out_shape=(pltpu.HBM(g.shape, g.dtype), pltpu.HBM(zone.shape, zone.dtype)),
        in_specs=[_HBM, _HBM, _SEM, _SEM, _ANY], out_specs=[_HBM, _HBM], input_output_aliases={0: 0, 1: 1},
        compiler_params=pltpu.CompilerParams(has_side_effects=pltpu.SideEffectType.DATAFLOW_SIDE_EFFECTING),
    )(g, zone, send_sems, recv_sems, after)


def _scatter_copies(p_ref, q_ref, send_sems, recv_sems):
    x, y, c = _mesh_pos()
    return [pltpu.make_async_remote_copy(
        src_ref=p_ref.at[2 * cx + cy], dst_ref=q_ref.at[j], send_sem=send_sems.at[j],
        recv_sem=recv_sems.at[j], device_id=(cx, cy, c), device_id_type=MESH)
        for j, (cx, cy) in enumerate(_other_chips(x, y))]


def _scatter_start(p, part):
    n, H, C = p.shape

    def body(p_ref, q_ref, send_sems, recv_sems, p_thru, q_thru, token):
        for cp in _scatter_copies(p_ref, q_ref, send_sems, recv_sems):
            cp.start()
        token[...] = jnp.zeros_like(token)

    zone = pltpu.with_memory_space_constraint(lax.empty((n - 1, H, C), p.dtype), pltpu.HBM)
    return pl.pallas_call(
        body, name="scatter_%s_start" % part,
        out_shape=(pltpu.SemaphoreType.DMA((n - 1,)), pltpu.SemaphoreType.DMA((n - 1,)), pltpu.HBM(p.shape, p.dtype),
                   pltpu.HBM(zone.shape, zone.dtype), jax.ShapeDtypeStruct((8, LANES), F32)),
        in_specs=[_HBM, _HBM], out_specs=(_SEM, _SEM, _HBM, _HBM, pl.BlockSpec(memory_space=pltpu.VMEM)),
        input_output_aliases={0: 2, 1: 3},
        compiler_params=pltpu.CompilerParams(has_side_effects=pltpu.SideEffectType.DATAFLOW_SIDE_EFFECTING),
    )(pltpu.with_memory_space_constraint(p, pltpu.HBM), zone)


def _scatter_wait(send_sems, recv_sems, p, zone, after, part):
    def body(p_ref, q_ref, send_sems, recv_sems, after_ref, p_out, q_out):
        copies = _scatter_copies(p_ref, q_ref, send_sems, recv_sems)
        for cp in copies:
            cp.wait_send()
        for cp in copies:
            cp.wait_recv()

    return pl.pallas_call(
        body, name="scatter_%s_wait" % part,
        out_shape=(pltpu.HBM(p.shape, p.dtype), pltpu.HBM(zone.shape, zone.dtype)),
        in_specs=[_HBM, _HBM, _SEM, _SEM, _ANY], out_specs=[_HBM, _HBM], input_output_aliases={0: 0, 1: 1},
        compiler_params=pltpu.CompilerParams(has_side_effects=pltpu.SideEffectType.DATAFLOW_SIDE_EFFECTING),
    )(p, zone, send_sems, recv_sems, after)


N_DEVICES = 8


def _allreduce_row(v):
    C = v.shape[1]

    def body(v_ref, out_ref, rows, send_sems, recv_sems):
        x, y, c = _mesh_pos()
        me = 4 * x + 2 * y + c
        rows[pl.ds(me, 1)] = v_ref[...].reshape(1, 1, C)
        copies = []
        for m in range(1, N_DEVICES):
            peer = (x ^ (m >> 2), y ^ ((m >> 1) & 1), c ^ (m & 1))
            copies.append(pltpu.make_async_remote_copy(
                src_ref=rows.at[me], dst_ref=rows.at[me], send_sem=send_sems.at[m - 1],
                recv_sem=recv_sems.at[m - 1], device_id=peer, device_id_type=MESH))
        for cp in copies:
            cp.start()
        for m, cp in enumerate(copies, start=1):
            cp.wait_send()
            pltpu.make_async_remote_copy(
                src_ref=rows.at[me], dst_ref=rows.at[me ^ m], send_sem=send_sems.at[m - 1],
                recv_sem=recv_sems.at[m - 1], device_id=(x, y, c), device_id_type=MESH).wait_recv()
        total = rows[0]
        for d in range(1, N_DEVICES):
            total = total + rows[d]
        out_ref[...] = total

    vmem = pl.BlockSpec(memory_space=pltpu.VMEM)
    return pl.pallas_call(
        body, name="allreduce_row", in_specs=[vmem], out_specs=vmem, out_shape=_sds((1, C), F32),
        scratch_shapes=[pltpu.VMEM((N_DEVICES, 1, C), F32), pltpu.SemaphoreType.DMA((N_DEVICES - 1,)),
                        pltpu.SemaphoreType.DMA((N_DEVICES - 1,))],
    )(v)


def _share_core_halves(r2):
    _, H, C = r2.shape

    def body(r_ref, out_ref, send_sem, recv_sem):
        x, y, c = _mesh_pos()
        send = pltpu.make_async_remote_copy(
            src_ref=r_ref.at[c], dst_ref=out_ref.at[c], send_sem=send_sem, recv_sem=recv_sem,
            device_id=(x, y, 1 - c), device_id_type=MESH)
        send.start()
        send.wait_send()
        pltpu.make_async_remote_copy(
            src_ref=r_ref.at[c], dst_ref=out_ref.at[1 - c], send_sem=send_sem, recv_sem=recv_sem,
            device_id=(x, y, 1 - c), device_id_type=MESH).wait_recv()

    return pl.pallas_call(
        body, name="share_core_halves", in_specs=[_ANY], out_specs=_ANY,
        out_shape=_sds(r2.shape, r2.dtype), input_output_aliases={0: 0},
        scratch_shapes=[pltpu.SemaphoreType.DMA, pltpu.SemaphoreType.DMA],
    )(r2)


def _place():
    x, y, c = _mesh_pos()
    return jnp.stack([c, 2 * x + y]).astype(jnp.int32)


def _sum_pair(g, got, place):
    n, _, H, C = g.shape

    def body(place_ref, a_ref, b_ref, o_ref):
        o_ref[...] = (a_ref[...] + b_ref[...]).astype(BF16)

    spec = pl.BlockSpec((1, FLAT_TILE, C), lambda s, i, pr: (s, i, 0))
    return pl.pallas_call(
        body, name="sum_core_pair",
        grid_spec=pltpu.PrefetchScalarGridSpec(
            num_scalar_prefetch=1, grid=(n, H // FLAT_TILE),
            in_specs=[pl.BlockSpec((1, None, FLAT_TILE, C), lambda s, i, pr: (s, pr[0], i, 0)), spec],
            out_specs=spec),
        out_shape=_sds((n, H, C), BF16),
        compiler_params=_params(("parallel", "parallel")))(place, g, got)


def _sum_chips(p, q, place):
    n, H, C = p.shape

    def body(place_ref, own_ref, qx_ref, qy_ref, qxy_ref, o_ref):
        mine = place_ref[1]
        own, qx, qy, qxy = (t[0].astype(F32) for t in (own_ref, qx_ref, qy_ref, qxy_ref))

        def term(s):
            rel = jnp.full(own.shape, mine ^ s, jnp.int32)
            return jnp.where(rel == 0, own, jnp.where(rel == 2, qx, jnp.where(rel == 1, qy, qxy)))

        o_ref[0] = ((term(0) + term(1)) + term(2)) + term(3)

    qspec = lambda j: pl.BlockSpec((1, FLAT_TILE, C), lambda i, pr: (j, i, 0))
    return pl.pallas_call(
        body, name="sum_chips",
        grid_spec=pltpu.PrefetchScalarGridSpec(
            num_scalar_prefetch=1, grid=(H // FLAT_TILE,),
            in_specs=[pl.BlockSpec((1, FLAT_TILE, C), lambda i, pr: (pr[1], i, 0)), qspec(0), qspec(1), qspec(2)],
            out_specs=pl.BlockSpec((1, FLAT_TILE, C), lambda i, pr: (pr[0], i, 0))),
        out_shape=_sds((2, H, C), F32),
        compiler_params=_params(("parallel",)))(place, p, q, q, q)


ADAMW_BLOCK = 1 << 18


def _adamw(w, g, m, v):
    shape = w.shape
    C = shape[-1]
    R = _size(shape) // C
    tr = R
    while tr * C > ADAMW_BLOCK and tr % 16 == 0:
        tr //= 2
    w, g, m, v = (t.reshape(R, C) for t in (w, g, m, v))

    def body(w_ref, g_ref, m_ref, v_ref, d_ref, nm_ref, nv_ref):
        gv = g_ref[...]
        nm = ADAM_B1 * m_ref[...] + (1.0 - ADAM_B1) * gv
        nv = ADAM_B2 * v_ref[...] + (1.0 - ADAM_B2) * (gv * gv)
        m_hat = nm / (1.0 - ADAM_B1 ** ADAM_STEP)
        v_hat = nv / (1.0 - ADAM_B2 ** ADAM_STEP)
        d_ref[...] = -ADAM_LR * (m_hat / (jnp.sqrt(v_hat) + ADAM_EPS) + ADAM_WD * w_ref[...])
        nm_ref[...] = nm
        nv_ref[...] = nv

    spec = pl.BlockSpec((tr, C), lambda i: (i, 0))
    outs = pl.pallas_call(body, name="adamw", grid=(R // tr,), in_specs=[spec] * 4, out_specs=[spec] * 3,
                          out_shape=[_sds((R, C), F32)] * 3, compiler_params=_params(("parallel",)))(w, g, m, v)
    return tuple(t.reshape(shape) for t in outs)


def _chip_partials(gfull, place):
    n, R, C = gfull.shape
    g4 = gfull.reshape(n, 2, R // 2, C)
    return _sum_pair(g4, _exchange_core_halves(g4), place)


def _finish_reduce(p, q, place):
    r2 = _share_core_halves(_sum_chips(p, q, place))
    return r2.reshape(2 * r2.shape[1], r2.shape[2])


def _local_step(x, p, positions, loss_target, pre_g, post_g, w, later_weights, later_grads, later_grads_arrived,
                first_grads):
    Bl, S, _ = x.shape
    T = Bl * S
    cos, sin = _rope_tables(positions)
    xs = x.reshape(T, D_MODEL)
    saved = []
    for i in range(DEPTH):
        if i == 1:
            later_weights(xs)
        j = i // 2
        g_pre, g_post = pre_g[i:i + 1], post_g[i:i + 1]
        h = _rmsnorm_fwd(xs, g_pre)
        st = {"x": xs, "h": h}
        if i % 2 == 0:
            proj = _mm(h, w["attn_w_in"][j], name="attn_in")
            res = [_attn_fwd(proj, cos, sin, g, Bl, S) for g in range(N_GROUPS)]
            a, o, lse = _attn_combine([r[0] for r in res], [r[1] for r in res], proj)
            w_out = w["attn_w_out"][j]
            st.update(proj=proj, a=a, o=o, lse=lse, qkv=[r[2] for r in res])
        else:
            w_ab, w_z = w["conv_w_in"][j][:, :2 * D_MODEL], w["conv_w_in"][j][:, 2 * D_MODEL:]
            ab = _mm(h, w_ab, out_dtype=BF16, name="conv_in_ab")
            z = _mm(h, w_z, out_dtype=BF16, name="conv_in_z")
            dw = jnp.pad(w["conv_dw_w"][j], ((0, 1), (0, 0)))
            u1, a = _conv_fwd(ab, z, dw, w["conv_dw_b"][j:j + 1], w["conv_ln_g"][j:j + 1],
                              w["conv_ln_b"][j:j + 1], Bl, S)
            w_out = w["conv_w_out"][j]
            st.update(w_ab=w_ab, w_z=w_z, ab=ab, z=z, dw=dw, u1=u1, a=a)
        y, x1 = _mm_rows(a, w_out, [xs, g_post], _post_epilogue, (F32, F32), "branch_out_post")
        pi = p[i].reshape(T, PLE_DIM)
        pe = _mm(pi, w["ple_w_proj"][i], name="ple_proj")
        gl, xs = _mm_rows(x1, w["ple_w_gate"][i], [pe], _ple_epilogue, (F32, F32), "ple_gate_fwd")
        st.update(y=y, x1=x1, pi=pi, pe=pe, gl=gl)
        saved.append(st)

    sq, dx = _loss_fwd_bwd(xs, loss_target.reshape(T, D_MODEL))

    grads = {n: [None] * shape[0] for n, shape, _ in _LAYOUT}
    for i in reversed(range(DEPTH)):
        j = i // 2
        st = saved[i]
        g_pre, g_post = pre_g[i:i + 1], post_g[i:i + 1]
        if i == 0:
            rest = {n: jnp.stack(v[1:] if n in _FIRST_LAYER_PARAMS else v) for n, v in grads.items()}
            g_post = g_post + later_grads(rest)[0, 0]
        dpe, dgl, dx1, dy, dg_post = _ple_post_bwd(dx, st["pe"], st["gl"], w["ple_w_gate"][i], st["y"], g_post)
        grads["ple_w_proj"][i] = _mm(st["pi"], dpe, ta=True, name="ple_proj_wgrad")
        grads["ple_w_gate"][i] = _mm(st["x1"], dgl, ta=True, name="ple_gate_wgrad")
        grads["post_norm_g"][i] = dg_post[0]
        if i % 2 == 0:
            grads["attn_w_out"][j] = _mm(st["a"], dy, ta=True, name="attn_out_wgrad")
            da = _mm(dy, w["attn_w_out"][j], tb=True, name="attn_out_dgrad")
            do, dproj = _gate_bwd(da, st["o"], st["proj"])
            cos_b = cos + later_grads_arrived(do)[0, 0] if i == 0 else cos
            for g in range(N_GROUPS):
                dproj = _attn_bwd(st["qkv"][g], cos_b, sin, do, st["o"], st["lse"], dproj, g, Bl, S)
            grads["attn_w_in"][j] = _mm(st["h"], dproj, ta=True, tk=2048, name="attn_in_wgrad")
            if i == 0:
                first = {n: jnp.stack(grads[n][:1]) for n in _FIRST_LAYER_PARAMS if n != "pre_norm_g"}
                g_pre = g_pre + first_grads(first)[0, 0]
            dx, dg_pre = _dgrad_pre_bwd(dproj, w["attn_w_in"][j], st["x"], g_pre, dx1, None, 1024,
                                        "attn_in_dgrad_pre", tm=1024)
        else:
            grads["conv_w_out"][j] = _mm(st["a"], dy, ta=True, name="conv_out_wgrad")
            da2 = _mm(dy, w["conv_w_out"][j], tb=True, name="conv_out_dgrad")
            du1, dz, dln_g, dln_b = _conv_norm_bwd(da2, st["z"], st["u1"], w["conv_ln_g"][j:j + 1],
                                                   w["conv_ln_b"][j:j + 1])
            dab, ddw, ddb = _conv_bwd(st["ab"], du1, st["dw"], Bl, S)
            dh = _mm(dz, st["w_z"], tb=True, name="conv_in_z_dgrad")
            dx, dg_pre = _dgrad_pre_bwd(dab, st["w_ab"], st["x"], g_pre, dx1, dh, 2048, "conv_in_dgrad_pre")
            dw_ab = _mm(st["h"], dab, ta=True, name="conv_in_ab_wgrad")
            dw_z = _mm(st["h"], dz, ta=True, name="conv_in_z_wgrad")
            grads["conv_w_in"][j] = jnp.concatenate([dw_ab, dw_z], axis=1)
            grads["conv_dw_w"][j] = ddw[:CONV_WIDTH]
            grads["conv_dw_b"][j] = ddb[0]
            grads["conv_ln_g"][j] = dln_g[0]
            grads["conv_ln_b"][j] = dln_b[0]
        grads["pre_norm_g"][i] = dg_pre[0]
    return sq, dx.reshape(Bl, S, D_MODEL), grads["pre_norm_g"][0][None]


_NAMES = tuple(n for n, _, _ in _LAYOUT)


def kernel(x, p, positions, pre_norm_g, post_norm_g, attn_w_in, attn_w_out, conv_w_in, conv_dw_w, conv_dw_b, conv_ln_g, conv_ln_b, conv_w_out, ple_w_proj, ple_w_gate, loss_target, m_pre_norm_g, m_post_norm_g, m_attn_w_in, m_attn_w_out, m_conv_w_in, m_conv_dw_w, m_conv_dw_b, m_conv_ln_g, m_conv_ln_b, m_conv_w_out, m_ple_w_proj, m_ple_w_gate, v_pre_norm_g, v_post_norm_g, v_attn_w_in, v_attn_w_out, v_conv_w_in, v_conv_dw_w, v_conv_dw_b, v_conv_ln_g, v_conv_ln_b, v_conv_w_out, v_ple_w_proj, v_ple_w_gate):
    w_loc = dict(zip(_NAMES, (pre_norm_g, post_norm_g, attn_w_in, attn_w_out, conv_w_in, conv_dw_w, conv_dw_b,
                              conv_ln_g, conv_ln_b, conv_w_out, ple_w_proj, ple_w_gate)))
    m_loc = dict(zip(_NAMES, (m_pre_norm_g, m_post_norm_g, m_attn_w_in, m_attn_w_out, m_conv_w_in, m_conv_dw_w,
                              m_conv_dw_b, m_conv_ln_g, m_conv_ln_b, m_conv_w_out, m_ple_w_proj, m_ple_w_gate)))
    v_loc = dict(zip(_NAMES, (v_pre_norm_g, v_post_norm_g, v_attn_w_in, v_attn_w_out, v_conv_w_in, v_conv_dw_w,
                              v_conv_dw_b, v_conv_ln_g, v_conv_ln_b, v_conv_w_out, v_ple_w_proj, v_ple_w_gate)))

    bf = {n: w_loc[n].astype(BF16) for n in _MATMUL_WEIGHTS}
    axes = [_AXIS[n] for n in _MATMUL_WEIGHTS]
    first, small = _allgather_weights([bf[n][:1] for n in _FIRST_LAYER], [_AXIS[n] for n in _FIRST_LAYER],
                                      _stack_small(w_loc))
    send_sems, recv_sems, shards, zones, token = _allgather_start(
        [bf[n][1:] if n in _FIRST_LAYER else bf[n] for n in _MATMUL_WEIGHTS], axes)
    w_full = dict({n: [full[0]] for n, full in zip(_FIRST_LAYER, first)}, **_unstack_small(small))

    def later_weights(after):
        for n, full in zip(_MATMUL_WEIGHTS, _allgather_wait(send_sems, recv_sems, shards, zones, axes, after)):
            w_full[n] = w_full[n] + [full[l] for l in range(full.shape[0])] if n in _FIRST_LAYER else full

    place = _place()
    rest_shapes, first_shapes = _part_shapes(False), _part_shapes(True)
    rest_halves, rest_flight, first_flight = [], [], []

    def later_grads(grads):
        gfull = _pack_full_grads(grads, rest_shapes)
        rest_halves.extend(_exchange_start(gfull.reshape(N_CHIPS, 2, gfull.shape[1] // 2, FLAT_COLS)))
        return rest_halves[4]

    def later_grads_arrived(after):
        rest_flight.extend(_scatter_start(_sum_pair(*_exchange_wait(*rest_halves[:4], after), place), "rest"))
        return rest_flight[4]

    def first_grads(grads):
        first_flight.extend(_scatter_start(_chip_partials(_pack_full_grads(grads, first_shapes), place), "first"))
        return first_flight[4]

    sq, grad_x, dg_pre0 = _local_step(x, p, positions, loss_target, pre_norm_g + token[0, 0], post_norm_g,
                                      w_full, later_weights, later_grads, later_grads_arrived, first_grads)
    loss = lax.psum(sq[0, 0] * (0.5 / D_MODEL), ("x", "y", "c"))

    p_rest, q_rest = _scatter_wait(*rest_flight[:4], grad_x, "rest")
    g_rest = _unpack_f32(_finish_reduce(p_rest, q_rest, place), rest_shapes)
    p_first, q_first = _scatter_wait(*first_flight[:4], grad_x, "first")
    g_first = _unpack_f32(_finish_reduce(p_first, q_first, place), first_shapes)
    g_first["pre_norm_g"] = _allreduce_row(dg_pre0)
    g_out = {n: jnp.concatenate([g_first[n], g_rest[n]]) if n in g_first and n in g_rest
             else g_rest.get(n, g_first.get(n)) for n in _NAMES}
    updates = {n: _adamw(w_loc[n], g_out[n], m_loc[n], v_loc[n]) for n in _NAMES}
    d_out, m_out, v_out = ({n: updates[n][k] for n in _NAMES} for k in range(3))
    return (loss, grad_x, *[g_out[n] for n in _NAMES], *[d_out[n] for n in _NAMES],
            *[m_out[n] for n in _NAMES], *[v_out[n] for n in _NAMES])
```
